```python
import math
import jax, jax.numpy as jnp
from jax import lax
import numpy as np

D_MODEL = 1024
BATCH = 4
SEQ = 8192
DEPTH = 2

HEAD_DIM = 64
N_MOBA_HEADS = 6
N_DIFF_HEADS = 6
N_SB_HEADS = 6
DIFF_QK_DIM = HEAD_DIM // 2
MOBA_BLOCK = 256
MOBA_TOPK = 3
MOBA_Q_CHUNK = 64
Q_BLOCK = 128
D_FF = 2816
N_BRANCH = 3
LN_EPS = 1e-5
SUBLN_EPS = 1e-5
MOBA_W = N_MOBA_HEADS * HEAD_DIM
DIFF_QK_W = N_DIFF_HEADS * 2 * DIFF_QK_DIM
DIFF_V_W = N_DIFF_HEADS * HEAD_DIM
SB_W = N_SB_HEADS * HEAD_DIM
IN_SIZES = (MOBA_W, MOBA_W, MOBA_W, DIFF_QK_W, DIFF_QK_W, DIFF_V_W, SB_W, SB_W, SB_W, N_BRANCH * D_MODEL)
N_IN = 3 * MOBA_W + 2 * DIFF_QK_W + DIFF_V_W + 3 * SB_W + N_BRANCH * D_MODEL

kernel_name = "hybrid_moba_diff_stickbreak_deepnorm"

F32 = jnp.float32


def _alibi_slopes(n):
    return (2.0 ** (-8.0 * np.arange(1, n + 1, dtype=np.float32) / n)).astype(np.float32)


def layer_norm(x, g, b):
    xf = x.astype(F32)
    mu = jnp.mean(xf, axis=-1, keepdims=True)
    var = jnp.mean(jnp.square(xf - mu), axis=-1, keepdims=True)
    return ((xf - mu) * lax.rsqrt(var + LN_EPS) * g + b).astype(x.dtype)


def swiglu(x, w_gate, w_up, w_down):
    return (jax.nn.silu(x @ w_gate) * (x @ w_up)) @ w_down


def split_heads(t, n_heads):
    B, S, _ = t.shape
    return t.reshape(B, S, n_heads, -1).transpose(0, 2, 1, 3)


def merge_heads(t):
    B, H, S, d = t.shape
    return t.transpose(0, 2, 1, 3).reshape(B, S, H * d)


def moba_attention(q, k, v, slopes):
    B, H, S, Dh = q.shape
    nb = -(-S // MOBA_BLOCK)
    pad = nb * MOBA_BLOCK - S
    kp = jnp.pad(k, ((0, 0), (0, 0), (0, pad), (0, 0)))
    vp = jnp.pad(v, ((0, 0), (0, 0), (0, pad), (0, 0)))
    kb = kp.reshape(B, H, nb, MOBA_BLOCK, Dh)
    vb = vp.reshape(B, H, nb, MOBA_BLOCK, Dh)
    kmean = jnp.mean(kb.astype(F32), axis=3)
    topk = min(MOBA_TOPK, nb)
    scale = Dh ** -0.5
    bi = jnp.arange(B)[:, None, None, None]
    hi = jnp.arange(H)[None, :, None, None]
    slope = slopes.reshape(1, H, 1, 1)
    blk_off = jnp.arange(MOBA_BLOCK)
    blk_ids = jnp.arange(nb)
    n_chunks = S // MOBA_Q_CHUNK

    def chunk(i):
        t0 = i * MOBA_Q_CHUNK
        qc = lax.dynamic_slice_in_dim(q, t0, MOBA_Q_CHUNK, axis=2)
        tpos = t0 + jnp.arange(MOBA_Q_CHUNK)
        own = t0 // MOBA_BLOCK
        gate = jnp.einsum('bhcd,bhnd->bhcn', qc.astype(F32), kmean)
        gate = jnp.where(blk_ids < own, gate, -jnp.inf)
        _, sel = lax.top_k(gate, topk)
        valid = sel < own
        ksel = kb[bi, hi, sel]
        vsel = vb[bi, hi, sel]
        s_sel = jnp.einsum('bhcd,bhcnkd->bhcnk', qc, ksel).astype(F32) * scale
        dist_sel = tpos[None, None, :, None, None] - (sel[..., None] * MOBA_BLOCK + blk_off)
        s_sel = jnp.where(valid[..., None], s_sel - slope[..., None] * dist_sel, -jnp.inf)
        kown = lax.dynamic_index_in_dim(kb, own, axis=2, keepdims=False)
        vown = lax.dynamic_index_in_dim(vb, own, axis=2, keepdims=False)
        s_own = jnp.einsum('bhcd,bhkd->bhck', qc, kown).astype(F32) * scale
        dist_own = tpos[:, None] - (own * MOBA_BLOCK + blk_off)[None, :]
        s_own = jnp.where(dist_own >= 0, s_own - slope * dist_own, -jnp.inf)
        scores = jnp.concatenate([s_sel.reshape(B, H, MOBA_Q_CHUNK, topk * MOBA_BLOCK), s_own], axis=-1)
        p = jax.nn.softmax(scores, axis=-1).astype(v.dtype)
        p_sel = p[..., :topk * MOBA_BLOCK].reshape(B, H, MOBA_Q_CHUNK, topk, MOBA_BLOCK)
        p_own = p[..., topk * MOBA_BLOCK:]
        return (jnp.einsum('bhcnk,bhcnkd->bhcd', p_sel, vsel)
                + jnp.einsum('bhck,bhkd->bhcd', p_own, vown))

    out = lax.map(chunk, jnp.arange(n_chunks))
    return out.transpose(1, 2, 0, 3, 4).reshape(B, H, S, Dh)


def diff_attention(q, k, v, slopes, lam, subln_g, lambda_init):
    B, H, _, S, dq = q.shape
    dv = v.shape[-1]
    scale = dq ** -0.5
    kpos = jnp.arange(S)
    slope = slopes.reshape(1, H, 1, 1, 1)

    def block(i):
        t0 = i * Q_BLOCK
        qc = lax.dynamic_slice_in_dim(q, t0, Q_BLOCK, axis=3)
        dist = (t0 + jnp.arange(Q_BLOCK))[:, None] - kpos[None, :]
        s = jnp.einsum('bhmcd,bhmsd->bhmcs', qc, k).astype(F32) * scale
        s = jnp.where(dist >= 0, s - slope * dist, -jnp.inf)
        p = jax.nn.softmax(s, axis=-1)
        w = p[:, :, 0] - lam * p[:, :, 1]
        return jnp.einsum('bhcs,bhsd->bhcd', w.astype(v.dtype), v)

    o = lax.map(block, jnp.arange(S // Q_BLOCK))
    o = o.transpose(1, 2, 0, 3, 4).reshape(B, H, S, dv).astype(F32)
    o = o * lax.rsqrt(jnp.mean(jnp.square(o), axis=-1, keepdims=True) + SUBLN_EPS) * subln_g
    return (o * (1.0 - lambda_init)).astype(v.dtype)


def stick_breaking_attention(q, k, v):
    B, H, S, Dh = q.shape
    scale = Dh ** -0.5
    kpos = jnp.arange(S)

    def block(i):
        t0 = i * Q_BLOCK
        qc = lax.dynamic_slice_in_dim(q, t0, Q_BLOCK, axis=2)
        before = kpos[None, :] < (t0 + jnp.arange(Q_BLOCK))[:, None]
        z = jnp.einsum('bhcd,bhsd->bhcs', qc, k).astype(F32) * scale
        log_1m = jnp.where(before, jax.nn.log_sigmoid(-z), 0.0)
        tail = lax.cumsum(log_1m, axis=3, reverse=True) - log_1m
        w = jnp.where(before, jnp.exp(jax.nn.log_sigmoid(z) + tail), 0.0)
        return jnp.einsum('bhcs,bhsd->bhcd', w.astype(v.dtype), v)

    o = lax.map(block, jnp.arange(S // Q_BLOCK))
    return o.transpose(1, 2, 0, 3, 4).reshape(B, H, S, Dh)


def hybrid_mixer(x, w_in, b_gate, diff_lambda, diff_subln_g, w_br_moba, w_br_diff, w_br_sb, w_out, layer_idx):
    B, S, D = x.shape
    h = x @ w_in
    offs = []
    acc = 0
    for n in IN_SIZES[:-1]:
        acc += n
        offs.append(acc)
    q_m, k_m, v_m, q_d, k_d, v_d, q_s, k_s, v_s, g = jnp.split(h, offs, axis=-1)

    slopes = jnp.asarray(_alibi_slopes(N_MOBA_HEADS + N_DIFF_HEADS))
    o_m = moba_attention(split_heads(q_m, N_MOBA_HEADS), split_heads(k_m, N_MOBA_HEADS),
                         split_heads(v_m, N_MOBA_HEADS), slopes[0::2])

    lambda_init = 0.8 - 0.6 * math.exp(-0.3 * layer_idx)
    lf = diff_lambda.astype(F32)
    lam = jnp.exp(jnp.sum(lf[0] * lf[1])) - jnp.exp(jnp.sum(lf[2] * lf[3])) + lambda_init
    qd = q_d.reshape(B, S, N_DIFF_HEADS, 2, DIFF_QK_DIM).transpose(0, 2, 3, 1, 4)
    kd = k_d.reshape(B, S, N_DIFF_HEADS, 2, DIFF_QK_DIM).transpose(0, 2, 3, 1, 4)
    o_d = diff_attention(qd, kd, split_heads(v_d, N_DIFF_HEADS), slopes[1::2], lam, diff_subln_g, lambda_init)

    o_s = stick_breaking_attention(split_heads(q_s, N_SB_HEADS), split_heads(k_s, N_SB_HEADS),
                                   split_heads(v_s, N_SB_HEADS))

    gates = jax.nn.sigmoid((g.reshape(B, S, N_BRANCH, D) + b_gate).astype(F32)).astype(x.dtype)
    merged = (gates[:, :, 0] * (merge_heads(o_m) @ w_br_moba)
              + gates[:, :, 1] * (merge_heads(o_d) @ w_br_diff)
              + gates[:, :, 2] * (merge_heads(o_s) @ w_br_sb))
    return merged @ w_out


def setup_inputs(seed: int = 0) -> dict:
    key = jax.random.key(seed)
    ks = jax.random.split(key, 14)
    D = D_MODEL
    beta = (8.0 * DEPTH) ** -0.25
    nrm = jax.random.normal
    segs = [(MOBA_W, 1.0), (MOBA_W, 1.0), (MOBA_W, beta), (DIFF_QK_W, 1.0), (DIFF_QK_W, 1.0),
            (DIFF_V_W, beta), (SB_W, 1.0), (SB_W, 1.0), (SB_W, beta), (N_BRANCH * D, 1.0)]
    col_scale = jnp.asarray(np.concatenate([np.full(n, s, np.float32) for n, s in segs]))
    return {
        "x": nrm(ks[0], (BATCH, SEQ, D), F32),
        "ln_g": 1.0 + 0.02 * nrm(ks[1], (DEPTH, 3, D), F32),
        "ln_b": 0.02 * nrm(ks[2], (DEPTH, 3, D), F32),
        "ffn_w_gate": nrm(ks[3], (DEPTH, 2, D, D_FF), F32) * D ** -0.5,
        "ffn_w_up": nrm(ks[4], (DEPTH, 2, D, D_FF), F32) * (D ** -0.5 * beta),
        "ffn_w_down": nrm(ks[5], (DEPTH, 2, D_FF, D), F32) * (D_FF ** -0.5 * beta),
        "w_in": nrm(ks[6], (DEPTH, D, N_IN), F32) * D ** -0.5 * col_scale,
        "b_gate": 0.01 * nrm(ks[7], (DEPTH, N_BRANCH, D), F32),
        "diff_lambda": 0.1 * nrm(ks[8], (DEPTH, 4, DIFF_QK_DIM), F32),
        "diff_subln_g": 1.0 + 0.02 * nrm(ks[9], (DEPTH, HEAD_DIM), F32),
        "w_br_moba": nrm(ks[10], (DEPTH, MOBA_W, D), F32) * MOBA_W ** -0.5,
        "w_br_diff": nrm(ks[11], (DEPTH, DIFF_V_W, D), F32) * DIFF_V_W ** -0.5,
        "w_br_sb": nrm(ks[12], (DEPTH, SB_W, D), F32) * SB_W ** -0.5,
        "w_out": nrm(ks[13], (DEPTH, D, D), F32) * (D ** -0.5 * beta),
    }


def reference(x, ln_g, ln_b, ffn_w_gate, ffn_w_up, ffn_w_down, w_in, b_gate, diff_lambda,
              diff_subln_g, w_br_moba, w_br_diff, w_br_sb, w_out):
    alpha = (2.0 * DEPTH) ** 0.25
    for l in range(DEPTH):
        x = layer_norm(alpha * x + 0.5 * swiglu(x, ffn_w_gate[l, 0], ffn_w_up[l, 0], ffn_w_down[l, 0]),
                       ln_g[l, 0], ln_b[l, 0])
        x = layer_norm(alpha * x + hybrid_mixer(x, w_in[l], b_gate[l], diff_lambda[l], diff_subln_g[l],
                                                w_br_moba[l], w_br_diff[l], w_br_sb[l], w_out[l], l),
                       ln_g[l, 1], ln_b[l, 1])
        x = layer_norm(alpha * x + 0.5 * swiglu(x, ffn_w_gate[l, 1], ffn_w_up[l, 1], ffn_w_down[l, 1]),
                       ln_g[l, 2], ln_b[l, 2])
    return x
```

```python
import functools
import math

import numpy as np
import jax
import jax.numpy as jnp
from jax import lax
from jax.experimental import pallas as pl
from jax.experimental.pallas import tpu as pltpu

F32 = jnp.float32
BF16 = jnp.bfloat16

HEAD_DIM = 64
N_HEADS = 6
DIFF_QK_DIM = HEAD_DIM // 2
MOBA_BLOCK = 256
MOBA_TOPK = 3
N_BRANCH = 3
LN_EPS = 1e-5
SUBLN_EPS = 1e-5

LANES = 128
HEADS_PER_BLOCK = LANES // HEAD_DIM
N_HEAD_BLOCKS = N_HEADS // HEADS_PER_BLOCK
MIX_W = N_HEADS * HEAD_DIM
QKV_W = 9 * MIX_W
ATT_T = 256
NEG = -1e30
SB_EXIT = -110.0
VMEM_LIMIT = 56 * 1024 * 1024


def _dot(a, b):
    return jnp.dot(a, b, preferred_element_type=F32)


def _dot_nt(a, b, precision=None):
    return lax.dot_general(a, b, (((1,), (1,)), ((), ())), precision=precision,
                           preferred_element_type=F32)


def _layer_norm(y, g, b):
    mu = jnp.mean(y, axis=-1, keepdims=True)
    yc = y - mu
    var = jnp.mean(yc * yc, axis=-1, keepdims=True)
    return yc * lax.rsqrt(var + LN_EPS) * g + b


def _ffn_ln_kernel(x_ref, wg_ref, wu_ref, wd_ref, lg_ref, lb_ref, o_ref, xb_ref, acc_ref, *, alpha, n_k):
    k = pl.program_id(1)

    @pl.when(k == 0)
    def _init():
        xb_ref[...] = x_ref[...].astype(BF16)
        acc_ref[...] = jnp.zeros_like(acc_ref)

    xb = xb_ref[...]
    gate = _dot(xb, wg_ref[...])
    up = _dot(xb, wu_ref[...])
    hid = gate * jax.nn.sigmoid(gate) * up
    acc_ref[...] += _dot(hid.astype(BF16), wd_ref[...])

    @pl.when(k == n_k - 1)
    def _fin():
        y = alpha * x_ref[...] + 0.5 * acc_ref[...]
        o_ref[...] = _layer_norm(y, lg_ref[...], lb_ref[...])


def _ffn_ln(x, wg, wu, wd, lg, lb, l, j, alpha, tm=1024, tf=256):
    T, D = x.shape
    FF = wg.shape[-1]
    n_k = FF // tf
    return pl.pallas_call(
        functools.partial(_ffn_ln_kernel, alpha=alpha, n_k=n_k),
        grid=(T // tm, n_k),
        in_specs=[
            pl.BlockSpec((tm, D), lambda i, k: (i, 0)),
            pl.BlockSpec((None, None, D, tf), lambda i, k: (l, j, 0, k)),
            pl.BlockSpec((None, None, D, tf), lambda i, k: (l, j, 0, k)),
            pl.BlockSpec((None, None, tf, D), lambda i, k: (l, j, k, 0)),
            pl.BlockSpec((1, D), lambda i, k: (0, 0)),
            pl.BlockSpec((1, D), lambda i, k: (0, 0)),
        ],
        out_specs=pl.BlockSpec((tm, D), lambda i, k: (i, 0)),
        out_shape=jax.ShapeDtypeStruct((T, D), F32),
        scratch_shapes=[pltpu.VMEM((tm, D), BF16), pltpu.VMEM((tm, D), F32)],
        compiler_params=pltpu.CompilerParams(
            dimension_semantics=("parallel", "arbitrary"), vmem_limit_bytes=VMEM_LIMIT),
        name="ffn_ln",
    )(x, wg, wu, wd, lg, lb)


def _proj_kernel(x_ref, w_ref, o_ref, xb_ref):
    @pl.when(pl.program_id(1) == 0)
    def _cast():
        xb_ref[...] = x_ref[...].astype(BF16)

    o_ref[...] = _dot(xb_ref[...], w_ref[...]).astype(o_ref.dtype)


def _qkv_proj(x, w_in, l, tm=1024, tn=1152):
    T, D = x.shape
    return pl.pallas_call(
        _proj_kernel,
        grid=(T // tm, QKV_W // tn),
        in_specs=[
            pl.BlockSpec((tm, D), lambda i, j: (i, 0)),
            pl.BlockSpec((None, D, tn), lambda i, j: (l, 0, j)),
        ],
        out_specs=pl.BlockSpec((tm, tn), lambda i, j: (i, j)),
        out_shape=jax.ShapeDtypeStruct((T, QKV_W), BF16),
        scratch_shapes=[pltpu.VMEM((tm, D), BF16)],
        compiler_params=pltpu.CompilerParams(
            dimension_semantics=("parallel", "arbitrary"), vmem_limit_bytes=VMEM_LIMIT),
        name="qkv_proj",
    )(x, w_in)


def _lane_iota():
    return lax.broadcasted_iota(jnp.int32, (1, LANES), 1)


def _kv_block(ref, j):
    return ref[pl.ds(pl.multiple_of(j * ATT_T, ATT_T), ATT_T), :]


def _flash_first(s, v_blk, m_ref, l_ref, acc_ref, idx):
    m = jnp.max(s, axis=-1, keepdims=True)
    p = jnp.exp(s - m)
    m_ref[idx] = m
    l_ref[idx] = jnp.sum(p, axis=-1, keepdims=True)
    acc_ref[idx] = _dot(p.astype(BF16), v_blk)


def _flash_update(s, v_blk, m_ref, l_ref, acc_ref, idx):
    m_old = m_ref[idx]
    m_new = jnp.maximum(m_old, jnp.max(s, axis=-1, keepdims=True))
    a = jnp.exp(m_old - m_new)
    p = jnp.exp(s - m_new)
    m_ref[idx] = m_new
    l_ref[idx] = a * l_ref[idx] + jnp.sum(p, axis=-1, keepdims=True)
    acc_ref[idx] = a * acc_ref[idx] + _dot(p.astype(BF16), v_blk)


def _alibi_cols(slope, j, qi):
    col = lax.broadcasted_iota(jnp.int32, (1, ATT_T), 1)
    return slope * ((j - qi) * ATT_T + col).astype(F32)


def _causal_mask():
    row = lax.broadcasted_iota(jnp.int32, (ATT_T, ATT_T), 0)
    col = lax.broadcasted_iota(jnp.int32, (ATT_T, ATT_T), 1)
    return row, col


def _moba_kernel(slopes_ref, q_ref, k_ref, v_ref, o_ref, kmean_ref, selm_ref, m_ref, l_ref, acc_ref,
                 *, n_blocks, scale):
    hb = pl.program_id(1)
    qi = pl.program_id(2)
    lane = _lane_iota()

    @pl.when(qi == 0)
    def _block_means():
        kmean_ref[...] = jnp.zeros_like(kmean_ref)

        def body(n, c):
            blk = _kv_block(k_ref, n).astype(F32)
            kmean_ref[pl.ds(n, 1), :] = jnp.mean(blk, axis=0, keepdims=True)
            return c

        lax.fori_loop(0, n_blocks, body, 0)

    q = q_ref[...]
    row, col = _causal_mask()
    lane_f = lane.astype(F32)
    for hh in range(HEADS_PER_BLOCK):
        hmask = (lane >= hh * HEAD_DIM) & (lane < (hh + 1) * HEAD_DIM)
        qh = jnp.where(hmask, q, jnp.zeros_like(q))
        slope = slopes_ref[hb * HEADS_PER_BLOCK + hh]

        gate = _dot_nt(qh.astype(F32), kmean_ref[...], precision=lax.Precision.HIGHEST)
        g = jnp.where(lane < qi, gate, -jnp.inf)
        sel = jnp.zeros(g.shape, dtype=jnp.bool_)
        for _ in range(MOBA_TOPK):
            mx = jnp.max(g, axis=-1, keepdims=True)
            first = jnp.min(jnp.where(g == mx, lane_f, float(LANES)), axis=-1, keepdims=True)
            hit = lane_f == first
            sel = sel | (hit & (mx > -jnp.inf))
            g = jnp.where(hit, -jnp.inf, g)
        selm_ref[...] = jnp.where(sel, 0.0, NEG)

        qs = (qh.astype(F32) * scale).astype(BF16)
        s = _dot_nt(qs, _kv_block(k_ref, qi)) + _alibi_cols(slope, qi, qi)
        s = jnp.where(col <= row, s, NEG)
        _flash_first(s, _kv_block(v_ref, qi), m_ref, l_ref, acc_ref, hh)

        def body(n, c):
            s = _dot_nt(qs, _kv_block(k_ref, n)) + _alibi_cols(slope, n, qi)
            rowmask = jnp.sum(jnp.where(lane == n, selm_ref[...], 0.0), axis=-1, keepdims=True)
            _flash_update(s + rowmask, _kv_block(v_ref, n), m_ref, l_ref, acc_ref, hh)
            return c

        lax.fori_loop(0, qi, body, 0)

    o0 = acc_ref[0] / l_ref[0]
    o1 = acc_ref[1] / l_ref[1]
    o_ref[...] = jnp.where(lane < HEAD_DIM, o0, o1).astype(o_ref.dtype)


def _diff_kernel(slopes_ref, lam_ref, g_ref, q_ref, k_ref, v_ref, o_ref, m_ref, l_ref, acc_ref,
                 *, scale, lambda_init):
    hb = pl.program_id(1)
    qi = pl.program_id(2)
    lane = _lane_iota()
    lf = lam_ref[...].astype(F32)
    lam = (jnp.exp(jnp.sum(lf[0:1] * lf[1:2], axis=-1, keepdims=True))
           - jnp.exp(jnp.sum(lf[2:3] * lf[3:4], axis=-1, keepdims=True)) + lambda_init)

    q = q_ref[...]
    row, col = _causal_mask()
    outs = []
    for hh in range(HEADS_PER_BLOCK):
        slope = slopes_ref[hb * HEADS_PER_BLOCK + hh]
        for mm in range(2):
            idx = 2 * hh + mm
            lo = hh * HEAD_DIM + mm * DIFF_QK_DIM
            qm = jnp.where((lane >= lo) & (lane < lo + DIFF_QK_DIM), q, jnp.zeros_like(q))

            s = _dot_nt(qm, _kv_block(k_ref, qi)) * scale + _alibi_cols(slope, qi, qi)
            s = jnp.where(col <= row, s, NEG)
            _flash_first(s, _kv_block(v_ref, qi), m_ref, l_ref, acc_ref, idx)

            def body(n, c, qm=qm, idx=idx, slope=slope):
                s = _dot_nt(qm, _kv_block(k_ref, n)) * scale + _alibi_cols(slope, n, qi)
                _flash_update(s, _kv_block(v_ref, n), m_ref, l_ref, acc_ref, idx)
                return c

            lax.fori_loop(0, qi, body, 0)

        hmask = (lane >= hh * HEAD_DIM) & (lane < (hh + 1) * HEAD_DIM)
        o = acc_ref[2 * hh] / l_ref[2 * hh] - lam * (acc_ref[2 * hh + 1] / l_ref[2 * hh + 1])
        ms = jnp.sum(jnp.where(hmask, o * o, 0.0), axis=-1, keepdims=True) * (1.0 / HEAD_DIM)
        outs.append(o * lax.rsqrt(ms + SUBLN_EPS) * g_ref[...] * (1.0 - lambda_init))
    o_ref[...] = jnp.where(lane < HEAD_DIM, outs[0], outs[1]).astype(o_ref.dtype)


def _neg_softplus(z):
    return -(jnp.maximum(z, 0.0) + jnp.log1p(jnp.exp(-jnp.abs(z))))


def _sb_kernel(q_ref, k_ref, v_ref, o_ref, carry_ref, acc_ref, *, scale):
    qi = pl.program_id(2)
    lane = _lane_iota()
    q = q_ref[...]
    row, col = _causal_mask()
    before = col < row
    upper = (row > col).astype(F32)
    for hh in range(HEADS_PER_BLOCK):
        hmask = (lane >= hh * HEAD_DIM) & (lane < (hh + 1) * HEAD_DIM)
        qs = (jnp.where(hmask, q, jnp.zeros_like(q)).astype(F32) * scale).astype(BF16)

        z = _dot_nt(qs, _kv_block(k_ref, qi))
        nsp = _neg_softplus(z)
        lg = jnp.where(before, nsp, 0.0)
        tail = jnp.dot(lg, upper, precision=lax.Precision.HIGHEST, preferred_element_type=F32)
        w = jnp.where(before, jnp.exp(z + nsp + tail), 0.0)
        acc_ref[hh] = _dot(w.astype(BF16), _kv_block(v_ref, qi))
        carry0 = jnp.sum(lg, axis=-1, keepdims=True)
        carry_ref[hh] = carry0

        def cond(c):
            j, live = c
            return (j >= 0) & live

        def body(c, qs=qs, hh=hh):
            j, _ = c
            z = _dot_nt(qs, _kv_block(k_ref, j))
            nsp = _neg_softplus(z)
            carry = carry_ref[hh]
            tail = carry + jnp.dot(nsp, upper, precision=lax.Precision.HIGHEST,
                                   preferred_element_type=F32)
            w = jnp.exp(z + nsp + tail)
            acc_ref[hh] += _dot(w.astype(BF16), _kv_block(v_ref, j))
            carry = carry + jnp.sum(nsp, axis=-1, keepdims=True)
            carry_ref[hh] = carry
            return j - 1, jnp.max(carry) > SB_EXIT

        lax.while_loop(cond, body, (qi - 1, jnp.max(carry0) > SB_EXIT))

    o_ref[...] = jnp.where(lane < HEAD_DIM, acc_ref[0], acc_ref[1]).astype(o_ref.dtype)


def _attn_specs(B, S, q_base, k_base, v_base):
    nq = S // ATT_T
    qspec = pl.BlockSpec((None, ATT_T, LANES), lambda b, h, i: (b, i, q_base + h))
    kspec = pl.BlockSpec((None, S, LANES), lambda b, h, i: (b, 0, k_base + h))
    vspec = pl.BlockSpec((None, S, LANES), lambda b, h, i: (b, 0, v_base + h))
    ospec = pl.BlockSpec((None, ATT_T, LANES), lambda b, h, i: (b, i, h))
    grid = (B, N_HEAD_BLOCKS, nq)
    params = pltpu.CompilerParams(
        dimension_semantics=("parallel", "parallel", "arbitrary"), vmem_limit_bytes=VMEM_LIMIT)
    return grid, qspec, kspec, vspec, ospec, params


_SMEM_SPEC = pl.BlockSpec(memory_space=pltpu.SMEM)


def _moba(qkv, slopes):
    B, S, _ = qkv.shape
    grid, qspec, kspec, vspec, ospec, params = _attn_specs(B, S, 0, 3, 6)
    return pl.pallas_call(
        functools.partial(_moba_kernel, n_blocks=S // MOBA_BLOCK, scale=HEAD_DIM ** -0.5),
        grid=grid,
        in_specs=[_SMEM_SPEC, qspec, kspec, vspec],
        out_specs=ospec,
        out_shape=jax.ShapeDtypeStruct((B, S, MIX_W), BF16),
        scratch_shapes=[
            pltpu.VMEM((LANES, LANES), F32),
            pltpu.VMEM((ATT_T, LANES), F32),
            pltpu.VMEM((HEADS_PER_BLOCK, ATT_T, 1), F32),
            pltpu.VMEM((HEADS_PER_BLOCK, ATT_T, 1), F32),
            pltpu.VMEM((HEADS_PER_BLOCK, ATT_T, LANES), F32),
        ],
        compiler_params=params,
        name="moba_attn",
    )(slopes, qkv, qkv, qkv)


def _diff(qkv, slopes, diff_lambda, subln_g, lambda_init):
    B, S, _ = qkv.shape
    grid, qspec, kspec, vspec, ospec, params = _attn_specs(B, S, 9, 12, 15)
    n_state = 2 * HEADS_PER_BLOCK
    return pl.pallas_call(
        functools.partial(_diff_kernel, scale=DIFF_QK_DIM ** -0.5, lambda_init=lambda_init),
        grid=grid,
        in_specs=[
            _SMEM_SPEC,
            pl.BlockSpec(diff_lambda.shape, lambda b, h, i: (0, 0)),
            pl.BlockSpec((1, LANES), lambda b, h, i: (0, 0)),
            qspec, kspec, vspec,
        ],
        out_specs=ospec,
        out_shape=jax.ShapeDtypeStruct((B, S, MIX_W), BF16),
        scratch_shapes=[
            pltpu.VMEM((n_state, ATT_T, 1), F32),
            pltpu.VMEM((n_state, ATT_T, 1), F32),
            pltpu.VMEM((n_state, ATT_T, LANES), F32),
        ],
        compiler_params=params,
        name="diff_attn",
    )(slopes, diff_lambda, subln_g, qkv, qkv, qkv)


def _sb(qkv):
    B, S, _ = qkv.shape
    grid, qspec, kspec, vspec, ospec, params = _attn_specs(B, S, 18, 21, 24)
    return pl.pallas_call(
        functools.partial(_sb_kernel, scale=HEAD_DIM ** -0.5),
        grid=grid,
        in_specs=[qspec, kspec, vspec],
        out_specs=ospec,
        out_shape=jax.ShapeDtypeStruct((B, S, MIX_W), BF16),
        scratch_shapes=[
            pltpu.VMEM((HEADS_PER_BLOCK, ATT_T, 1), F32),
            pltpu.VMEM((HEADS_PER_BLOCK, ATT_T, LANES), F32),
        ],
        compiler_params=params,
        name="sb_attn",
    )(qkv, qkv, qkv)


def _merge_ln_kernel(x_ref, om_ref, od_ref, os_ref, wg_ref, bg_ref, wbr_ref, wo_ref, lg_ref, lb_ref,
                     o_ref, *, alpha, d_model):
    x = x_ref[...]
    xb = x.astype(BF16)
    merged = None
    for b, ob_ref in enumerate((om_ref, od_ref, os_ref)):
        glogit = _dot(xb, wg_ref[:, b * d_model:(b + 1) * d_model]) + bg_ref[b:b + 1, :]
        term = jax.nn.sigmoid(glogit) * _dot(ob_ref[...], wbr_ref[b])
        merged = term if merged is None else merged + term
    y = alpha * x + _dot(merged.astype(BF16), wo_ref[...])
    o_ref[...] = _layer_norm(y, lg_ref[...], lb_ref[...])


def _merge_ln(x, o_m, o_d, o_s, w_gate, b_gate, w_br, w_out, lg, lb, l, alpha, tm=512):
    T, D = x.shape
    return pl.pallas_call(
        functools.partial(_merge_ln_kernel, alpha=alpha, d_model=D),
        grid=(T // tm,),
        in_specs=[
            pl.BlockSpec((tm, D), lambda i: (i, 0)),
            pl.BlockSpec((tm, MIX_W), lambda i: (i, 0)),
            pl.BlockSpec((tm, MIX_W), lambda i: (i, 0)),
            pl.BlockSpec((tm, MIX_W), lambda i: (i, 0)),
            pl.BlockSpec((None, D, N_BRANCH * D), lambda i: (l, 0, 0)),
            pl.BlockSpec((None, N_BRANCH, D), lambda i: (l, 0, 0)),
            pl.BlockSpec((None, N_BRANCH, MIX_W, D), lambda i: (l, 0, 0, 0)),
            pl.BlockSpec((None, D, D), lambda i: (l, 0, 0)),
            pl.BlockSpec((1, D), lambda i: (0, 0)),
            pl.BlockSpec((1, D), lambda i: (0, 0)),
        ],
        out_specs=pl.BlockSpec((tm, D), lambda i: (i, 0)),
        out_shape=jax.ShapeDtypeStruct((T, D), F32),
        compiler_params=pltpu.CompilerParams(
            dimension_semantics=("parallel",), vmem_limit_bytes=VMEM_LIMIT),
        name="merge_ln",
    )(x, o_m, o_d, o_s, w_gate, b_gate, w_br, w_out, lg, lb)


def _alibi_slopes(n):
    return (2.0 ** (-8.0 * np.arange(1, n + 1, dtype=np.float32) / n)).astype(np.float32)


def kernel(x, ln_g, ln_b, ffn_w_gate, ffn_w_up, ffn_w_down, w_in, b_gate, diff_lambda, diff_subln_g,
           w_br_moba, w_br_diff, w_br_sb, w_out):
    B, S, D = x.shape
    depth = ln_g.shape[0]
    assert S % ATT_T == 0 and ATT_T == MOBA_BLOCK and S // MOBA_BLOCK <= LANES
    assert w_in.shape[-1] == QKV_W + N_BRANCH * D
    alpha = (2.0 * depth) ** 0.25

    wg = ffn_w_gate.astype(BF16)
    wu = ffn_w_up.astype(BF16)
    wd = ffn_w_down.astype(BF16)
    w_qkv = w_in[:, :, :QKV_W].astype(BF16)
    w_gate = w_in[:, :, QKV_W:].astype(BF16)
    w_br = jnp.stack([w_br_moba, w_br_diff, w_br_sb], axis=1).astype(BF16)
    w_out_b = w_out.astype(BF16)
    slopes = _alibi_slopes(2 * N_HEADS)
    slopes_moba = jnp.asarray(slopes[0::2])
    slopes_diff = jnp.asarray(slopes[1::2])
    subln_g = jnp.tile(diff_subln_g.astype(F32), (1, HEADS_PER_BLOCK))

    h = x.reshape(B * S, D)
    for l in range(depth):
        lg = ln_g[l][:, None, :]
        lb = ln_b[l][:, None, :]
        h = _ffn_ln(h, wg, wu, wd, lg[0], lb[0], l, 0, alpha)
        qkv = _qkv_proj(h, w_qkv, l).reshape(B, S, QKV_W)
        lambda_init = 0.8 - 0.6 * math.exp(-0.3 * l)
        o_m = _moba(qkv, slopes_moba)
        o_d = _diff(qkv, slopes_diff, diff_lambda[l], subln_g[l][None, :], lambda_init)
        o_s = _sb(qkv)
        T = B * S
        h = _merge_ln(h, o_m.reshape(T, MIX_W), o_d.reshape(T, MIX_W), o_s.reshape(T, MIX_W),
                      w_gate, b_gate, w_br, w_out_b, lg[1], lb[1], l, alpha)
        h = _ffn_ln(h, wg, wu, wd, lg[2], lb[2], l, 1, alpha)
    return h.reshape(B, S, D)
```

```python
import functools
import math

import numpy as np
import jax
import jax.numpy as jnp
from jax import lax
from jax.experimental import pallas as pl
from jax.experimental.pallas import tpu as pltpu

F32 = jnp.float32
BF16 = jnp.bfloat16

HEAD_DIM = 64
N_HEADS = 6
DIFF_QK_DIM = HEAD_DIM // 2
MOBA_BLOCK = 256
MOBA_TOPK = 3
N_BRANCH = 3
LN_EPS = 1e-5
SUBLN_EPS = 1e-5

LANES = 128
SUBLANES = 8
HEADS_PER_BLOCK = LANES // HEAD_DIM
N_HEAD_BLOCKS = N_HEADS // HEADS_PER_BLOCK
MIX_W = N_HEADS * HEAD_DIM
QKV_W = 9 * MIX_W
ATT_T = 256
NEG = -1e30
SB_EXIT = -110.0
LOG2E = math.log2(math.e)
N_BIAS_COLS = 3
VMEM_LIMIT = 56 * 1024 * 1024


def _dot(a, b, precision=None):
    return jnp.dot(a, b, precision=precision, preferred_element_type=F32)


def _layer_norm(y, g, b):
    mu = jnp.mean(y, axis=-1, keepdims=True)
    yc = y - mu
    var = jnp.mean(yc * yc, axis=-1, keepdims=True)
    return yc * lax.rsqrt(var + LN_EPS) * g + b


def _ffn_ln_kernel(x_ref, wg_ref, wu_ref, wd_ref, lg_ref, lb_ref, o_ref, xb_ref, acc_ref, *, alpha, n_k):
    k = pl.program_id(1)

    @pl.when(k == 0)
    def _init():
        xb_ref[...] = x_ref[...].astype(BF16)
        acc_ref[...] = jnp.zeros_like(acc_ref)

    xb = xb_ref[...]
    gate = _dot(xb, wg_ref[...])
    up = _dot(xb, wu_ref[...])
    hid = gate * jax.nn.sigmoid(gate) * up
    acc_ref[...] += _dot(hid.astype(BF16), wd_ref[...])

    @pl.when(k == n_k - 1)
    def _fin():
        y = alpha * x_ref[...] + 0.5 * acc_ref[...]
        o_ref[...] = _layer_norm(y, lg_ref[...], lb_ref[...])


def _ffn_ln(x, wg, wu, wd, lg, lb, l, j, alpha, tm=1024, tf=256):
    T, D = x.shape
    FF = wg.shape[-1]
    n_k = FF // tf
    return pl.pallas_call(
        functools.partial(_ffn_ln_kernel, alpha=alpha, n_k=n_k),
        grid=(T // tm, n_k),
        in_specs=[
            pl.BlockSpec((tm, D), lambda i, k: (i, 0)),
            pl.BlockSpec((None, None, D, tf), lambda i, k: (l, j, 0, k)),
            pl.BlockSpec((None, None, D, tf), lambda i, k: (l, j, 0, k)),
            pl.BlockSpec((None, None, tf, D), lambda i, k: (l, j, k, 0)),
            pl.BlockSpec((1, D), lambda i, k: (0, 0)),
            pl.BlockSpec((1, D), lambda i, k: (0, 0)),
        ],
        out_specs=pl.BlockSpec((tm, D), lambda i, k: (i, 0)),
        out_shape=jax.ShapeDtypeStruct((T, D), F32),
        scratch_shapes=[pltpu.VMEM((tm, D), BF16), pltpu.VMEM((tm, D), F32)],
        compiler_params=pltpu.CompilerParams(
            dimension_semantics=("parallel", "arbitrary"), vmem_limit_bytes=VMEM_LIMIT),
        name="ffn_ln",
    )(x, wg, wu, wd, lg, lb)


def _proj_kernel(x_ref, w_ref, o_ref, xb_ref):
    @pl.when(pl.program_id(1) == 0)
    def _cast():
        xb_ref[...] = x_ref[...].astype(BF16)

    o_ref[...] = _dot(xb_ref[...], w_ref[...]).astype(o_ref.dtype)


def _qkv_proj(x, w_in, l, tm=1024, tn=1152):
    T, D = x.shape
    return pl.pallas_call(
        _proj_kernel,
        grid=(T // tm, QKV_W // tn),
        in_specs=[
            pl.BlockSpec((tm, D), lambda i, j: (i, 0)),
            pl.BlockSpec((None, D, tn), lambda i, j: (l, 0, j)),
        ],
        out_specs=pl.BlockSpec((tm, tn), lambda i, j: (i, j)),
        out_shape=jax.ShapeDtypeStruct((T, QKV_W), BF16),
        scratch_shapes=[pltpu.VMEM((tm, D), BF16)],
        compiler_params=pltpu.CompilerParams(
            dimension_semantics=("parallel", "arbitrary"), vmem_limit_bytes=VMEM_LIMIT),
        name="qkv_proj",
    )(x, w_in)


def _lane_iota():
    return lax.broadcasted_iota(jnp.int32, (1, LANES), 1)


def _head_lanes(hh):
    lane = _lane_iota()
    return (lane >= hh * HEAD_DIM) & (lane < (hh + 1) * HEAD_DIM)


def _bias_lanes(hh):
    lane = _lane_iota()
    lo = (1 - hh) * HEAD_DIM
    return (lane >= lo) & (lane < lo + N_BIAS_COLS)


def _blk(j):
    return pl.ds(pl.multiple_of(j * ATT_T, ATT_T), ATT_T)


def _key_query_iotas():
    krow = lax.broadcasted_iota(jnp.int32, (ATT_T, ATT_T), 0)
    qcol = lax.broadcasted_iota(jnp.int32, (ATT_T, ATT_T), 1)
    return krow, qcol


def _build_softmax_operands(slopes_ref, hb, k_ref, v_ref, kp_ref, vt_ref, *, n_blocks, scale):
    lane = _lane_iota()
    pos = lax.broadcasted_iota(jnp.int32, (ATT_T, LANES), 0).astype(F32)
    bias_cols = []
    for hh in range(HEADS_PER_BLOCK):
        b = pos * (slopes_ref[hb * HEADS_PER_BLOCK + hh] * LOG2E)
        lo = (1 - hh) * HEAD_DIM
        cols = jnp.zeros_like(b)
        for i in range(N_BIAS_COLS):
            piece = b.astype(BF16).astype(F32)
            cols = jnp.where(lane == lo + i, piece, cols)
            b = b - piece
        bias_cols.append(cols)

    def body(n, c):
        kf = k_ref[_blk(n), :].astype(F32) * (scale * LOG2E)
        vf = v_ref[_blk(n), :].astype(F32)
        for hh in range(HEADS_PER_BLOCK):
            hm = _head_lanes(hh)
            kp_ref[hh, _blk(n), :] = jnp.where(hm, kf, bias_cols[hh]).astype(BF16)
            vt_ref[hh, :, _blk(n)] = jnp.where(hm, vf, 1.0).T.astype(BF16)
        return c

    lax.fori_loop(0, n_blocks, body, 0)


def _softmax_diag(c, hh, qi, kp_ref, vt_ref, qt_ref, m_ref, acc_ref):
    krow, qcol = _key_query_iotas()
    s = _dot(kp_ref[hh, _blk(qi), :], qt_ref[c])
    s = jnp.where(krow <= qcol, s, NEG)
    m = jnp.max(s, axis=0, keepdims=True)
    p = jnp.exp2(s - m).astype(BF16)
    m_ref[c] = m
    acc_ref[c] = _dot(vt_ref[hh, :, _blk(qi)], p)


def _softmax_step(c, hh, n, off, kp_ref, vt_ref, qt_ref, m_ref, acc_ref):
    s = _dot(kp_ref[hh, _blk(n), :], qt_ref[c])
    m_old = m_ref[c]
    m_new = jnp.maximum(m_old, jnp.max(s, axis=0, keepdims=True) + off)
    p = jnp.exp2(s - (m_new - off)).astype(BF16)
    acc_ref[c] = acc_ref[c] * jnp.exp2(m_old - m_new) + _dot(vt_ref[hh, :, _blk(n)], p)
    m_ref[c] = m_new


def _normalized(acc, hh):
    r = (1 - hh) * HEAD_DIM
    return acc * (1.0 / acc[r:r + 1, :])


def _row_is_head0():
    return lax.broadcasted_iota(jnp.int32, (LANES, ATT_T), 0) < HEAD_DIM


def _moba_kernel(slopes_ref, q_ref, k_ref, v_ref, o_ref, kp_ref, vt_ref, kmean_ref, selm_ref, qt_ref,
                 m_ref, acc_ref, *, n_blocks, scale):
    hb = pl.program_id(1)
    qi = pl.program_id(2)
    nbp = kmean_ref.shape[0]

    @pl.when(qi == 0)
    def _first_tile():
        _build_softmax_operands(slopes_ref, hb, k_ref, v_ref, kp_ref, vt_ref, n_blocks=n_blocks, scale=scale)
        kmean_ref[...] = jnp.zeros_like(kmean_ref)

        def body(n, c):
            kmean_ref[pl.ds(n, 1), :] = jnp.mean(k_ref[_blk(n), :].astype(F32), axis=0, keepdims=True)
            return c

        lax.fori_loop(0, n_blocks, body, 0)

    qf = q_ref[...].astype(F32)
    blk_id = lax.broadcasted_iota(jnp.int32, (nbp, ATT_T), 0)
    blk_f = blk_id.astype(F32)
    for hh in range(HEADS_PER_BLOCK):
        hm = _head_lanes(hh)
        qt = jnp.where(hm, qf, jnp.where(_bias_lanes(hh), 1.0, 0.0)).T
        qt_ref[hh] = qt.astype(BF16)

        gate = _dot(jnp.where(hm, kmean_ref[...], 0.0), qt, precision=lax.Precision.HIGHEST)
        g = jnp.where(blk_id < qi, gate, -jnp.inf)
        sel = jnp.zeros(g.shape, dtype=jnp.bool_)
        for _ in range(MOBA_TOPK):
            mx = jnp.max(g, axis=0, keepdims=True)
            first = jnp.min(jnp.where(g == mx, blk_f, float(nbp)), axis=0, keepdims=True)
            hit = blk_f == first
            sel = sel | (hit & (mx > -jnp.inf))
            g = jnp.where(hit, -jnp.inf, g)
        selm_ref[hh] = jnp.where(sel, 0.0, NEG)

        _softmax_diag(hh, hh, qi, kp_ref, vt_ref, qt_ref, m_ref, acc_ref)

    def body(n, c):
        for hh in range(HEADS_PER_BLOCK):
            slope2 = slopes_ref[hb * HEADS_PER_BLOCK + hh] * LOG2E
            off = slope2 * ((n - qi) * ATT_T).astype(F32) + selm_ref[hh, pl.ds(n, 1), :]
            _softmax_step(hh, hh, n, off, kp_ref, vt_ref, qt_ref, m_ref, acc_ref)
        return c

    lax.fori_loop(0, qi, body, 0)

    o = jnp.where(_row_is_head0(), _normalized(acc_ref[0], 0), _normalized(acc_ref[1], 1))
    o_ref[...] = o.T.astype(o_ref.dtype)


def _diff_kernel(slopes_ref, lam_ref, g_ref, q_ref, k_ref, v_ref, o_ref, kp_ref, vt_ref, qt_ref,
                 m_ref, acc_ref, *, n_blocks, scale, lambda_init):
    hb = pl.program_id(1)
    qi = pl.program_id(2)
    lane = _lane_iota()

    @pl.when(qi == 0)
    def _first_tile():
        _build_softmax_operands(slopes_ref, hb, k_ref, v_ref, kp_ref, vt_ref, n_blocks=n_blocks, scale=scale)

    lf = lam_ref[...].astype(F32)
    lam = (jnp.exp(jnp.sum(lf[0:1] * lf[1:2], axis=-1, keepdims=True))
           - jnp.exp(jnp.sum(lf[2:3] * lf[3:4], axis=-1, keepdims=True)) + lambda_init)

    qf = q_ref[...].astype(F32)
    chains = [(hh, mm) for hh in range(HEADS_PER_BLOCK) for mm in range(2)]
    for c, (hh, mm) in enumerate(chains):
        lo = hh * HEAD_DIM + mm * DIFF_QK_DIM
        seg = (lane >= lo) & (lane < lo + DIFF_QK_DIM)
        qt_ref[c] = jnp.where(seg, qf, jnp.where(_bias_lanes(hh), 1.0, 0.0)).T.astype(BF16)
        _softmax_diag(c, hh, qi, kp_ref, vt_ref, qt_ref, m_ref, acc_ref)

    def body(n, carry):
        for c, (hh, mm) in enumerate(chains):
            slope2 = slopes_ref[hb * HEADS_PER_BLOCK + hh] * LOG2E
            off = slope2 * ((n - qi) * ATT_T).astype(F32)
            _softmax_step(c, hh, n, off, kp_ref, vt_ref, qt_ref, m_ref, acc_ref)
        return carry

    lax.fori_loop(0, qi, body, 0)

    is_h0 = _row_is_head0()
    outs = []
    for hh in range(HEADS_PER_BLOCK):
        o = _normalized(acc_ref[2 * hh], hh) - lam * _normalized(acc_ref[2 * hh + 1], hh)
        in_head = is_h0 if hh == 0 else jnp.logical_not(is_h0)
        ms = jnp.sum(jnp.where(in_head, o * o, 0.0), axis=0, keepdims=True) * (1.0 / HEAD_DIM)
        outs.append(o * lax.rsqrt(ms + SUBLN_EPS))
    o = jnp.where(is_h0, outs[0], outs[1]).T * (g_ref[...] * (1.0 - lambda_init))
    o_ref[...] = o.astype(o_ref.dtype)


def _neg_softplus(z):
    return -(jnp.maximum(z, 0.0) + jnp.log1p(jnp.exp(-jnp.abs(z))))


def _suffix_sums(upper, lg):
    hi = lg.astype(BF16)
    lo = (lg - hi.astype(F32)).astype(BF16)
    return _dot(upper, hi) + _dot(upper, lo)


def _sb_kernel(q_ref, k_ref, v_ref, o_ref, vt_ref, qt_ref, carry_ref, acc_ref, *, n_blocks, scale):
    qi = pl.program_id(2)

    @pl.when(qi == 0)
    def _first_tile():
        def body(n, c):
            vt_ref[:, _blk(n)] = v_ref[_blk(n), :].astype(F32).T.astype(BF16)
            return c

        lax.fori_loop(0, n_blocks, body, 0)

    krow, qcol = _key_query_iotas()
    before = krow < qcol
    upper = (qcol > krow).astype(BF16)
    qf = q_ref[...].astype(F32) * scale
    for hh in range(HEADS_PER_BLOCK):
        qt_ref[hh] = jnp.where(_head_lanes(hh), qf, 0.0).T.astype(BF16)

        z = _dot(k_ref[_blk(qi), :], qt_ref[hh])
        nsp = _neg_softplus(z)
        lg = jnp.where(before, nsp, 0.0)
        w = jnp.where(before, jnp.exp(z + nsp + _suffix_sums(upper, lg)), 0.0)
        acc_ref[hh] = _dot(vt_ref[:, _blk(qi)], w.astype(BF16))
        carry_ref[hh] = jnp.sum(lg, axis=0, keepdims=True)

    def cond(c):
        j, live = c
        return (j >= 0) & live

    def body(c):
        j, _ = c
        worst = None
        for hh in range(HEADS_PER_BLOCK):
            z = _dot(k_ref[_blk(j), :], qt_ref[hh])
            nsp = _neg_softplus(z)
            carry = carry_ref[hh]
            w = jnp.exp(z + nsp + (_suffix_sums(upper, nsp) + carry))
            acc_ref[hh] += _dot(vt_ref[:, _blk(j)], w.astype(BF16))
            carry = carry + jnp.sum(nsp, axis=0, keepdims=True)
            carry_ref[hh] = carry
            top = jnp.max(carry)
            worst = top if worst is None else jnp.maximum(worst, top)
        return j - 1, worst > SB_EXIT

    live0 = jnp.maximum(jnp.max(carry_ref[0]), jnp.max(carry_ref[1])) > SB_EXIT
    lax.while_loop(cond, body, (qi - 1, live0))

    o = jnp.where(_row_is_head0(), acc_ref[0], acc_ref[1])
    o_ref[...] = o.T.astype(o_ref.dtype)


def _attn_specs(B, S, q_base, k_base, v_base):
    nq = S // ATT_T
    qspec = pl.BlockSpec((None, ATT_T, LANES), lambda b, h, i: (b, i, q_base + h))
    kspec = pl.BlockSpec((None, S, LANES), lambda b, h, i: (b, 0, k_base + h))
    vspec = pl.BlockSpec((None, S, LANES), lambda b, h, i: (b, 0, v_base + h))
    ospec = pl.BlockSpec((None, ATT_T, LANES), lambda b, h, i: (b, i, h))
    grid = (B, N_HEAD_BLOCKS, nq)
    params = pltpu.CompilerParams(
        dimension_semantics=("parallel", "parallel", "arbitrary"), vmem_limit_bytes=VMEM_LIMIT)
    return grid, qspec, kspec, vspec, ospec, params


_SMEM_SPEC = pl.BlockSpec(memory_space=pltpu.SMEM)


def _moba(qkv, slopes):
    B, S, _ = qkv.shape
    grid, qspec, kspec, vspec, ospec, params = _attn_specs(B, S, 0, 3, 6)
    n_blocks = S // MOBA_BLOCK
    nbp = -(-n_blocks // SUBLANES) * SUBLANES
    return pl.pallas_call(
        functools.partial(_moba_kernel, n_blocks=n_blocks, scale=HEAD_DIM ** -0.5),
        grid=grid,
        in_specs=[_SMEM_SPEC, qspec, kspec, vspec],
        out_specs=ospec,
        out_shape=jax.ShapeDtypeStruct((B, S, MIX_W), BF16),
        scratch_shapes=[
            pltpu.VMEM((HEADS_PER_BLOCK, S, LANES), BF16),
            pltpu.VMEM((HEADS_PER_BLOCK, LANES, S), BF16),
            pltpu.VMEM((nbp, LANES), F32),
            pltpu.VMEM((HEADS_PER_BLOCK, nbp, ATT_T), F32),
            pltpu.VMEM((HEADS_PER_BLOCK, LANES, ATT_T), BF16),
            pltpu.VMEM((HEADS_PER_BLOCK, 1, ATT_T), F32),
            pltpu.VMEM((HEADS_PER_BLOCK, LANES, ATT_T), F32),
        ],
        compiler_params=params,
        name="moba_attn",
    )(slopes, qkv, qkv, qkv)


def _diff(qkv, slopes, diff_lambda, subln_g, lambda_init):
    B, S, _ = qkv.shape
    grid, qspec, kspec, vspec, ospec, params = _attn_specs(B, S, 9, 12, 15)
    n_chains = 2 * HEADS_PER_BLOCK
    return pl.pallas_call(
        functools.partial(_diff_kernel, n_blocks=S // ATT_T, scale=DIFF_QK_DIM ** -0.5,
                          lambda_init=lambda_init),
        grid=grid,
        in_specs=[
            _SMEM_SPEC,
            pl.BlockSpec(diff_lambda.shape, lambda b, h, i: (0, 0)),
            pl.BlockSpec((1, LANES), lambda b, h, i: (0, 0)),
            qspec, kspec, vspec,
        ],
        out_specs=ospec,
        out_shape=jax.ShapeDtypeStruct((B, S, MIX_W), BF16),
        scratch_shapes=[
            pltpu.VMEM((HEADS_PER_BLOCK, S, LANES), BF16),
            pltpu.VMEM((HEADS_PER_BLOCK, LANES, S), BF16),
            pltpu.VMEM((n_chains, LANES, ATT_T), BF16),
            pltpu.VMEM((n_chains, 1, ATT_T), F32),
            pltpu.VMEM((n_chains, LANES, ATT_T), F32),
        ],
        compiler_params=params,
        name="diff_attn",
    )(slopes, diff_lambda, subln_g, qkv, qkv, qkv)


def _sb(qkv):
    B, S, _ = qkv.shape
    grid, qspec, kspec, vspec, ospec, params = _attn_specs(B, S, 18, 21, 24)
    return pl.pallas_call(
        functools.partial(_sb_kernel, n_blocks=S // ATT_T, scale=HEAD_DIM ** -0.5),
        grid=grid,
        in_specs=[qspec, kspec, vspec],
        out_specs=ospec,
        out_shape=jax.ShapeDtypeStruct((B, S, MIX_W), BF16),
        scratch_shapes=[
            pltpu.VMEM((LANES, S), BF16),
            pltpu.VMEM((HEADS_PER_BLOCK, LANES, ATT_T), BF16),
            pltpu.VMEM((HEADS_PER_BLOCK, 1, ATT_T), F32),
            pltpu.VMEM((HEADS_PER_BLOCK, LANES, ATT_T), F32),
        ],
        compiler_params=params,
        name="sb_attn",
    )(qkv, qkv, qkv)


def _merge_ln_kernel(x_ref, om_ref, od_ref, os_ref, wg_ref, bg_ref, wbr_ref, wo_ref, lg_ref, lb_ref,
                     o_ref, *, alpha, d_model):
    x = x_ref[...]
    xb = x.astype(BF16)
    merged = None
    for b, ob_ref in enumerate((om_ref, od_ref, os_ref)):
        glogit = _dot(xb, wg_ref[:, b * d_model:(b + 1) * d_model]) + bg_ref[b:b + 1, :]
        term = jax.nn.sigmoid(glogit) * _dot(ob_ref[...], wbr_ref[b])
        merged = term if merged is None else merged + term
    y = alpha * x + _dot(merged.astype(BF16), wo_ref[...])
    o_ref[...] = _layer_norm(y, lg_ref[...], lb_ref[...])


def _merge_ln(x, o_m, o_d, o_s, w_gate, b_gate, w_br, w_out, lg, lb, l, alpha, tm=512):
    T, D = x.shape
    return pl.pallas_call(
        functools.partial(_merge_ln_kernel, alpha=alpha, d_model=D),
        grid=(T // tm,),
        in_specs=[
            pl.BlockSpec((tm, D), lambda i: (i, 0)),
            pl.BlockSpec((tm, MIX_W), lambda i: (i, 0)),
            pl.BlockSpec((tm, MIX_W), lambda i: (i, 0)),
            pl.BlockSpec((tm, MIX_W), lambda i: (i, 0)),
            pl.BlockSpec((None, D, N_BRANCH * D), lambda i: (l, 0, 0)),
            pl.BlockSpec((None, N_BRANCH, D), lambda i: (l, 0, 0)),
            pl.BlockSpec((None, N_BRANCH, MIX_W, D), lambda i: (l, 0, 0, 0)),
            pl.BlockSpec((None, D, D), lambda i: (l, 0, 0)),
            pl.BlockSpec((1, D), lambda i: (0, 0)),
            pl.BlockSpec((1, D), lambda i: (0, 0)),
        ],
        out_specs=pl.BlockSpec((tm, D), lambda i: (i, 0)),
        out_shape=jax.ShapeDtypeStruct((T, D), F32),
        compiler_params=pltpu.CompilerParams(
            dimension_semantics=("parallel",), vmem_limit_bytes=VMEM_LIMIT),
        name="merge_ln",
    )(x, o_m, o_d, o_s, w_gate, b_gate, w_br, w_out, lg, lb)


def _alibi_slopes(n):
    return (2.0 ** (-8.0 * np.arange(1, n + 1, dtype=np.float32) / n)).astype(np.float32)


def kernel(x, ln_g, ln_b, ffn_w_gate, ffn_w_up, ffn_w_down, w_in, b_gate, diff_lambda, diff_subln_g,
           w_br_moba, w_br_diff, w_br_sb, w_out):
    B, S, D = x.shape
    depth = ln_g.shape[0]
    assert S % ATT_T == 0 and ATT_T == MOBA_BLOCK
    assert w_in.shape[-1] == QKV_W + N_BRANCH * D
    alpha = (2.0 * depth) ** 0.25

    wg = ffn_w_gate.astype(BF16)
    wu = ffn_w_up.astype(BF16)
    wd = ffn_w_down.astype(BF16)
    w_qkv = w_in[:, :, :QKV_W].astype(BF16)
    w_gate = w_in[:, :, QKV_W:].astype(BF16)
    w_br = jnp.stack([w_br_moba, w_br_diff, w_br_sb], axis=1).astype(BF16)
    w_out_b = w_out.astype(BF16)
    slopes = _alibi_slopes(2 * N_HEADS)
    slopes_moba = jnp.asarray(slopes[0::2])
    slopes_diff = jnp.asarray(slopes[1::2])
    subln_g = jnp.tile(diff_subln_g.astype(F32), (1, HEADS_PER_BLOCK))

    h = x.reshape(B * S, D)
    for l in range(depth):
        lg = ln_g[l][:, None, :]
        lb = ln_b[l][:, None, :]
        h = _ffn_ln(h, wg, wu, wd, lg[0], lb[0], l, 0, alpha)
        qkv = _qkv_proj(h, w_qkv, l).reshape(B, S, QKV_W)
        lambda_init = 0.8 - 0.6 * math.exp(-0.3 * l)
        o_m = _moba(qkv, slopes_moba)
        o_d = _diff(qkv, slopes_diff, diff_lambda[l], subln_g[l][None, :], lambda_init)
        o_s = _sb(qkv)
        T = B * S
        h = _merge_ln(h, o_m.reshape(T, MIX_W), o_d.reshape(T, MIX_W), o_s.reshape(T, MIX_W),
                      w_gate, b_gate, w_br, w_out_b, lg[1], lb[1], l, alpha)
        h = _ffn_ln(h, wg, wu, wd, lg[2], lb[2], l, 1, alpha)
    return h.reshape(B, S, D)
```

```python
import functools
import math

import numpy as np
import jax
import jax.numpy as jnp
from jax import lax
from jax.experimental import pallas as pl
from jax.experimental.pallas import tpu as pltpu

F32 = jnp.float32
BF16 = jnp.bfloat16

HEAD_DIM = 64
N_HEADS = 6
DIFF_QK_DIM = HEAD_DIM // 2
MOBA_BLOCK = 256
MOBA_TOPK = 3
N_BRANCH = 3
LN_EPS = 1e-5
SUBLN_EPS = 1e-5

LANES = 128
SUBLANES = 8
HEADS_PER_BLOCK = LANES // HEAD_DIM
N_HEAD_BLOCKS = N_HEADS // HEADS_PER_BLOCK
MIX_W = N_HEADS * HEAD_DIM
QKV_W = 9 * MIX_W
ATT_T = 256
NEG = -1e30
SB_EXIT = -110.0
LOG2E = math.log2(math.e)
N_BIAS_COLS = 3
BOUND_SLACK = 1.01
UNDERFLOW_GUARD = 100.0
SKIP_BITS = 250.0
MIN_DENOM = 2.0 ** -60
ITEMS_PER_TRIP = 8
SCORE_LOOKAHEAD = 3
VMEM_LIMIT = 56 * 1024 * 1024


def _dot(a, b, precision=None):
    return jnp.dot(a, b, precision=precision, preferred_element_type=F32)


def _layer_norm(y, g, b):
    mu = jnp.mean(y, axis=-1, keepdims=True)
    yc = y - mu
    var = jnp.mean(yc * yc, axis=-1, keepdims=True)
    return yc * lax.rsqrt(var + LN_EPS) * g + b


def _ffn_ln_kernel(x_ref, wg_ref, wu_ref, wd_ref, lg_ref, lb_ref, o_ref, xb_ref, acc_ref, *, alpha, n_k):
    k = pl.program_id(1)

    @pl.when(k == 0)
    def _init():
        xb_ref[...] = x_ref[...].astype(BF16)
        acc_ref[...] = jnp.zeros_like(acc_ref)

    xb = xb_ref[...]
    gate = _dot(xb, wg_ref[...])
    up = _dot(xb, wu_ref[...])
    hid = gate * jax.nn.sigmoid(gate) * up
    acc_ref[...] += _dot(hid.astype(BF16), wd_ref[...])

    @pl.when(k == n_k - 1)
    def _fin():
        y = alpha * x_ref[...] + 0.5 * acc_ref[...]
        o_ref[...] = _layer_norm(y, lg_ref[...], lb_ref[...])


def _ffn_ln(x, wg, wu, wd, lg, lb, l, j, alpha, tm=1024, tf=256):
    T, D = x.shape
    FF = wg.shape[-1]
    n_k = FF // tf
    return pl.pallas_call(
        functools.partial(_ffn_ln_kernel, alpha=alpha, n_k=n_k),
        grid=(T // tm, n_k),
        in_specs=[
            pl.BlockSpec((tm, D), lambda i, k: (i, 0)),
            pl.BlockSpec((None, None, D, tf), lambda i, k: (l, j, 0, k)),
            pl.BlockSpec((None, None, D, tf), lambda i, k: (l, j, 0, k)),
            pl.BlockSpec((None, None, tf, D), lambda i, k: (l, j, k, 0)),
            pl.BlockSpec((1, D), lambda i, k: (0, 0)),
            pl.BlockSpec((1, D), lambda i, k: (0, 0)),
        ],
        out_specs=pl.BlockSpec((tm, D), lambda i, k: (i, 0)),
        out_shape=jax.ShapeDtypeStruct((T, D), F32),
        scratch_shapes=[pltpu.VMEM((tm, D), BF16), pltpu.VMEM((tm, D), F32)],
        compiler_params=pltpu.CompilerParams(
            dimension_semantics=("parallel", "arbitrary"), vmem_limit_bytes=VMEM_LIMIT),
        name="ffn_ln",
    )(x, wg, wu, wd, lg, lb)


def _proj_kernel(x_ref, w_ref, o_ref, xb_ref):
    @pl.when(pl.program_id(1) == 0)
    def _cast():
        xb_ref[...] = x_ref[...].astype(BF16)

    o_ref[...] = _dot(xb_ref[...], w_ref[...]).astype(o_ref.dtype)


def _qkv_proj(x, w_in, l, tm=1024, tn=1152):
    T, D = x.shape
    return pl.pallas_call(
        _proj_kernel,
        grid=(T // tm, QKV_W // tn),
        in_specs=[
            pl.BlockSpec((tm, D), lambda i, j: (i, 0)),
            pl.BlockSpec((None, D, tn), lambda i, j: (l, 0, j)),
        ],
        out_specs=pl.BlockSpec((tm, tn), lambda i, j: (i, j)),
        out_shape=jax.ShapeDtypeStruct((T, QKV_W), BF16),
        scratch_shapes=[pltpu.VMEM((tm, D), BF16)],
        compiler_params=pltpu.CompilerParams(
            dimension_semantics=("parallel", "arbitrary"), vmem_limit_bytes=VMEM_LIMIT),
        name="qkv_proj",
    )(x, w_in)


def _lane_iota():
    return lax.broadcasted_iota(jnp.int32, (1, LANES), 1)


def _head_lanes(hh):
    lane = _lane_iota()
    return (lane >= hh * HEAD_DIM) & (lane < (hh + 1) * HEAD_DIM)


def _bias_lanes(hh):
    lane = _lane_iota()
    lo = (1 - hh) * HEAD_DIM
    return (lane >= lo) & (lane < lo + N_BIAS_COLS)


def _blk(j):
    return pl.ds(pl.multiple_of(j * ATT_T, ATT_T), ATT_T)


def _key_query_iotas():
    krow = lax.broadcasted_iota(jnp.int32, (ATT_T, ATT_T), 0)
    qcol = lax.broadcasted_iota(jnp.int32, (ATT_T, ATT_T), 1)
    return krow, qcol


def _build_softmax_operands(tab_ref, hb, k_ref, v_ref, kp_ref, vt_ref, kmax_ref, *, n_blocks, scale):
    lane = _lane_iota()
    pos = lax.broadcasted_iota(jnp.int32, (ATT_T, LANES), 0).astype(F32)
    bias_cols = []
    for hh in range(HEADS_PER_BLOCK):
        b = pos * tab_ref[hb * HEADS_PER_BLOCK + hh]
        lo = (1 - hh) * HEAD_DIM
        cols = jnp.zeros_like(b)
        for i in range(N_BIAS_COLS):
            piece = b.astype(BF16).astype(F32)
            cols = jnp.where(lane == lo + i, piece, cols)
            b = b - piece
        bias_cols.append(cols)
    kmax_ref[...] = jnp.zeros_like(kmax_ref)

    def body(n, c):
        kf = k_ref[_blk(n), :].astype(F32) * (scale * LOG2E)
        vf = v_ref[_blk(n), :].astype(F32)
        for hh in range(HEADS_PER_BLOCK):
            hm = _head_lanes(hh)
            kp_ref[hh, _blk(n), :] = jnp.where(hm, kf, bias_cols[hh]).astype(BF16)
            vt_ref[hh, :, _blk(n)] = jnp.where(hm, vf, 1.0).T.astype(BF16)
            norm2 = jnp.sum(jnp.where(hm, kf * kf, 0.0), axis=1, keepdims=True)
            kmax_ref[hh] = jnp.maximum(kmax_ref[hh], jnp.max(norm2, axis=0, keepdims=True))
        return c

    lax.fori_loop(0, n_blocks, body, 0)


def _chain_qt(c, hh, seg, qf, qt_ref):
    row = lax.broadcasted_iota(jnp.int32, (LANES, ATT_T), 0)
    lo = (1 - hh) * HEAD_DIM
    qsel = jnp.where(seg, qf, 0.0).T
    qt_ref[c] = jnp.where((row >= lo) & (row < lo + N_BIAS_COLS), 1.0, qsel).astype(BF16)
    return qsel, jnp.sum(qsel * qsel, axis=0, keepdims=True)


def _chain_diag(c, hh, qi, qn2, slope2, kp_ref, vt_ref, kmax_ref, qt_ref, ref_ref, acc_ref):
    krow, qcol = _key_query_iotas()
    s = _dot(kp_ref[hh, _blk(qi), :], qt_ref[c])
    s = jnp.where(krow <= qcol, s, NEG)
    m = jnp.max(s, axis=0, keepdims=True)
    r = lax.broadcasted_iota(jnp.int32, (1, ATT_T), 1).astype(F32)
    bound = jnp.sqrt(qn2 * kmax_ref[hh]) * BOUND_SLACK + slope2 * r
    ref = jnp.maximum(m, bound - UNDERFLOW_GUARD)
    ref_ref[c] = ref
    acc_ref[c] = _dot(vt_ref[hh, :, _blk(qi)], jnp.exp2(s - ref).astype(BF16))
    return jnp.max(bound - ref)


def _first_block(gap, inv_blk, qi):
    reach = jnp.minimum((gap + SKIP_BITS) * inv_blk, 1e6).astype(jnp.int32)
    return jnp.maximum(qi - 1 - reach, 0)


def _fixed_ref_steps(items, kp_ref, vt_ref, qt_ref, ref_ref, acc_ref):
    def scores(i):
        c, hh, n, _ = items[i]
        return _dot(kp_ref[hh, _blk(n), :], qt_ref[c])

    pending = [scores(i) for i in range(min(SCORE_LOOKAHEAD, len(items)))]
    for i, (c, hh, n, off) in enumerate(items):
        p = jnp.exp2(pending.pop(0) - (ref_ref[c] - off)).astype(BF16)
        if i + SCORE_LOOKAHEAD < len(items):
            pending.append(scores(i + SCORE_LOOKAHEAD))
        acc_ref[c] += _dot(vt_ref[hh, :, _blk(n)], p)


def _online_diag(c, hh, qi, kp_ref, vt_ref, qt_ref, m_ref, acc_ref):
    krow, qcol = _key_query_iotas()
    s = _dot(kp_ref[hh, _blk(qi), :], qt_ref[c])
    s = jnp.where(krow <= qcol, s, NEG)
    m = jnp.max(s, axis=0, keepdims=True)
    m_ref[c] = m
    acc_ref[c] = _dot(vt_ref[hh, :, _blk(qi)], jnp.exp2(s - m).astype(BF16))


def _online_step(c, hh, n, off, kp_ref, vt_ref, qt_ref, m_ref, acc_ref):
    s = _dot(kp_ref[hh, _blk(n), :], qt_ref[c])
    m_old = m_ref[c]
    m_new = jnp.maximum(m_old, jnp.max(s, axis=0, keepdims=True) + off)
    p = jnp.exp2(s - (m_new - off)).astype(BF16)
    acc_ref[c] = acc_ref[c] * jnp.exp2(m_old - m_new) + _dot(vt_ref[hh, :, _blk(n)], p)
    m_ref[c] = m_new


def _softmax_sweep(qi, chains, first, off_fn, kp_ref, vt_ref, qt_ref, ref_ref, acc_ref):
    refs = (kp_ref, vt_ref, qt_ref, ref_ref, acc_ref)

    def run_fixed(lo, hi, heads):
        active = [(c, hh) for c, hh in enumerate(chains) if hh in heads]
        unroll = max(1, ITEMS_PER_TRIP // len(active))

        def items(n, k):
            return [(c, hh, n + j, off_fn(hh, n + j)) for j in range(k) for c, hh in active]

        start = lo
        while unroll >= 1:
            trips = jnp.maximum(hi - start, 0) // unroll

            def body(i, carry, start=start, unroll=unroll):
                _fixed_ref_steps(items(start + i * unroll, unroll), *refs)
                return carry

            lax.fori_loop(0, trips, body, 0)
            start = start + trips * unroll
            unroll //= 2

    both = jnp.maximum(first[0], first[1])
    run_fixed(first[0], both, (0,))
    run_fixed(first[1], both, (1,))
    run_fixed(both, qi, (0, 1))

    smallest = None
    for c, hh in enumerate(chains):
        r = (1 - hh) * HEAD_DIM
        d = jnp.min(acc_ref[c][r:r + 1, :])
        smallest = d if smallest is None else jnp.minimum(smallest, d)

    @pl.when(jnp.logical_not(smallest >= MIN_DENOM))
    def _redo():
        for c, hh in enumerate(chains):
            _online_diag(c, hh, qi, *refs)

        def body(n, carry):
            for c, hh in enumerate(chains):
                _online_step(c, hh, n, off_fn(hh, n), *refs)
            return carry

        lax.fori_loop(0, qi, body, 0)


def _normalized(acc, hh):
    r = (1 - hh) * HEAD_DIM
    return acc * (1.0 / acc[r:r + 1, :])


def _row_is_head0():
    return lax.broadcasted_iota(jnp.int32, (LANES, ATT_T), 0) < HEAD_DIM


def _moba_kernel(tab_ref, q_ref, k_ref, v_ref, o_ref, kp_ref, vt_ref, kmax_ref, kmean_ref, selm_ref,
                 qt_ref, ref_ref, acc_ref, *, n_blocks, scale):
    hb = pl.program_id(1)
    qi = pl.program_id(2)
    nbp = kmean_ref.shape[0]

    @pl.when(qi == 0)
    def _first_tile():
        _build_softmax_operands(tab_ref, hb, k_ref, v_ref, kp_ref, vt_ref, kmax_ref,
                                n_blocks=n_blocks, scale=scale)
        kmean_ref[...] = jnp.zeros_like(kmean_ref)

        def body(n, c):
            kmean_ref[pl.ds(n, 1), :] = jnp.mean(k_ref[_blk(n), :].astype(F32), axis=0, keepdims=True)
            return c

        lax.fori_loop(0, n_blocks, body, 0)

    qf = q_ref[...].astype(F32)
    blk_id = lax.broadcasted_iota(jnp.int32, (nbp, ATT_T), 0)
    blk_f = blk_id.astype(F32)
    chains = list(range(HEADS_PER_BLOCK))
    slope2 = [tab_ref[hb * HEADS_PER_BLOCK + hh] for hh in chains]
    first = []
    for hh in chains:
        hm = _head_lanes(hh)
        qsel, qn2 = _chain_qt(hh, hh, hm, qf, qt_ref)

        gate = _dot(jnp.where(hm, kmean_ref[...], 0.0), qsel, precision=lax.Precision.HIGHEST)
        g = jnp.where(blk_id < qi, gate, -jnp.inf)
        sel = jnp.zeros(g.shape, dtype=jnp.bool_)
        for _ in range(MOBA_TOPK):
            mx = jnp.max(g, axis=0, keepdims=True)
            pick = jnp.min(jnp.where(g == mx, blk_f, float(nbp)), axis=0, keepdims=True)
            hit = blk_f == pick
            sel = sel | (hit & (mx > -jnp.inf))
            g = jnp.where(hit, -jnp.inf, g)
        selm_ref[hh] = jnp.where(sel, 0.0, NEG)

        gap = _chain_diag(hh, hh, qi, qn2, slope2[hh], kp_ref, vt_ref, kmax_ref, qt_ref, ref_ref, acc_ref)
        first.append(_first_block(gap, tab_ref[N_HEADS + hb * HEADS_PER_BLOCK + hh], qi))

    def off_fn(hh, n):
        return slope2[hh] * ((n - qi) * ATT_T).astype(F32) + selm_ref[hh, pl.ds(n, 1), :]

    _softmax_sweep(qi, chains, first, off_fn, kp_ref, vt_ref, qt_ref, ref_ref, acc_ref)

    o = jnp.where(_row_is_head0(), _normalized(acc_ref[0], 0), _normalized(acc_ref[1], 1))
    o_ref[...] = o.T.astype(o_ref.dtype)


def _diff_kernel(tab_ref, lam_ref, g_ref, q_ref, k_ref, v_ref, o_ref, kp_ref, vt_ref, kmax_ref, qt_ref,
                 ref_ref, acc_ref, *, n_blocks, scale, lambda_init):
    hb = pl.program_id(1)
    qi = pl.program_id(2)
    lane = _lane_iota()

    @pl.when(qi == 0)
    def _first_tile():
        _build_softmax_operands(tab_ref, hb, k_ref, v_ref, kp_ref, vt_ref, kmax_ref,
                                n_blocks=n_blocks, scale=scale)

    lf = lam_ref[...].astype(F32)
    lam = (jnp.exp(jnp.sum(lf[0:1] * lf[1:2], axis=-1, keepdims=True))
           - jnp.exp(jnp.sum(lf[2:3] * lf[3:4], axis=-1, keepdims=True)) + lambda_init)

    qf = q_ref[...].astype(F32)
    chains = [hh for hh in range(HEADS_PER_BLOCK) for _ in range(2)]
    slope2 = [tab_ref[hb * HEADS_PER_BLOCK + hh] for hh in range(HEADS_PER_BLOCK)]
    first = [None] * HEADS_PER_BLOCK
    for c, hh in enumerate(chains):
        lo = hh * HEAD_DIM + (c % 2) * DIFF_QK_DIM
        _, qn2 = _chain_qt(c, hh, (lane >= lo) & (lane < lo + DIFF_QK_DIM), qf, qt_ref)
        gap = _chain_diag(c, hh, qi, qn2, slope2[hh], kp_ref, vt_ref, kmax_ref, qt_ref, ref_ref, acc_ref)
        fb = _first_block(gap, tab_ref[N_HEADS + hb * HEADS_PER_BLOCK + hh], qi)
        first[hh] = fb if first[hh] is None else jnp.minimum(first[hh], fb)

    def off_fn(hh, n):
        return slope2[hh] * ((n - qi) * ATT_T).astype(F32)

    _softmax_sweep(qi, chains, first, off_fn, kp_ref, vt_ref, qt_ref, ref_ref, acc_ref)

    is_h0 = _row_is_head0()
    outs = []
    for hh in range(HEADS_PER_BLOCK):
        o = _normalized(acc_ref[2 * hh], hh) - lam * _normalized(acc_ref[2 * hh + 1], hh)
        in_head = is_h0 if hh == 0 else jnp.logical_not(is_h0)
        ms = jnp.sum(jnp.where(in_head, o * o, 0.0), axis=0, keepdims=True) * (1.0 / HEAD_DIM)
        outs.append(o * lax.rsqrt(ms + SUBLN_EPS))
    o = jnp.where(is_h0, outs[0], outs[1]).T * (g_ref[...] * (1.0 - lambda_init))
    o_ref[...] = o.astype(o_ref.dtype)


def _neg_softplus(z):
    return -(jnp.maximum(z, 0.0) + jnp.log1p(jnp.exp(-jnp.abs(z))))


def _suffix_sums(upper, lg):
    hi = lg.astype(BF16)
    lo = (lg - hi.astype(F32)).astype(BF16)
    return _dot(upper, hi) + _dot(upper, lo)


def _sb_kernel(q_ref, k_ref, v_ref, o_ref, vt_ref, qt_ref, carry_ref, acc_ref, *, n_blocks, scale):
    qi = pl.program_id(2)

    @pl.when(qi == 0)
    def _first_tile():
        def body(n, c):
            vt_ref[:, _blk(n)] = v_ref[_blk(n), :].astype(F32).T.astype(BF16)
            return c

        lax.fori_loop(0, n_blocks, body, 0)

    krow, qcol = _key_query_iotas()
    before = krow < qcol
    upper = (qcol > krow).astype(BF16)
    qf = q_ref[...].astype(F32) * scale
    for hh in range(HEADS_PER_BLOCK):
        qt_ref[hh] = jnp.where(_head_lanes(hh), qf, 0.0).T.astype(BF16)

        z = _dot(k_ref[_blk(qi), :], qt_ref[hh])
        nsp = _neg_softplus(z)
        lg = jnp.where(before, nsp, 0.0)
        w = jnp.where(before, jnp.exp(z + nsp + _suffix_sums(upper, lg)), 0.0)
        acc_ref[hh] = _dot(vt_ref[:, _blk(qi)], w.astype(BF16))
        carry_ref[hh] = jnp.sum(lg, axis=0, keepdims=True)

    def cond(c):
        j, live = c
        return (j >= 0) & live

    def body(c):
        j, _ = c
        worst = None
        for hh in range(HEADS_PER_BLOCK):
            z = _dot(k_ref[_blk(j), :], qt_ref[hh])
            nsp = _neg_softplus(z)
            carry = carry_ref[hh]
            w = jnp.exp(z + nsp + (_suffix_sums(upper, nsp) + carry))
            acc_ref[hh] += _dot(vt_ref[:, _blk(j)], w.astype(BF16))
            carry = carry + jnp.sum(nsp, axis=0, keepdims=True)
            carry_ref[hh] = carry
            top = jnp.max(carry)
            worst = top if worst is None else jnp.maximum(worst, top)
        return j - 1, worst > SB_EXIT

    live0 = jnp.maximum(jnp.max(carry_ref[0]), jnp.max(carry_ref[1])) > SB_EXIT
    lax.while_loop(cond, body, (qi - 1, live0))

    o = jnp.where(_row_is_head0(), acc_ref[0], acc_ref[1])
    o_ref[...] = o.T.astype(o_ref.dtype)


def _attn_specs(B, S, q_base, k_base, v_base):
    nq = S // ATT_T
    qspec = pl.BlockSpec((None, ATT_T, LANES), lambda b, h, i: (b, i, q_base + h))
    kspec = pl.BlockSpec((None, S, LANES), lambda b, h, i: (b, 0, k_base + h))
    vspec = pl.BlockSpec((None, S, LANES), lambda b, h, i: (b, 0, v_base + h))
    ospec = pl.BlockSpec((None, ATT_T, LANES), lambda b, h, i: (b, i, h))
    grid = (B, N_HEAD_BLOCKS, nq)
    params = pltpu.CompilerParams(
        dimension_semantics=("parallel", "parallel", "arbitrary"), vmem_limit_bytes=VMEM_LIMIT)
    return grid, qspec, kspec, vspec, ospec, params


_SMEM_SPEC = pl.BlockSpec(memory_space=pltpu.SMEM)


def _moba(qkv, tab):
    B, S, _ = qkv.shape
    grid, qspec, kspec, vspec, ospec, params = _attn_specs(B, S, 0, 3, 6)
    n_blocks = S // MOBA_BLOCK
    nbp = -(-n_blocks // SUBLANES) * SUBLANES
    return pl.pallas_call(
        functools.partial(_moba_kernel, n_blocks=n_blocks, scale=HEAD_DIM ** -0.5),
        grid=grid,
        in_specs=[_SMEM_SPEC, qspec, kspec, vspec],
        out_specs=ospec,
        out_shape=jax.ShapeDtypeStruct((B, S, MIX_W), BF16),
        scratch_shapes=[
            pltpu.VMEM((HEADS_PER_BLOCK, S, LANES), BF16),
            pltpu.VMEM((HEADS_PER_BLOCK, LANES, S), BF16),
            pltpu.VMEM((HEADS_PER_BLOCK, 1, 1), F32),
            pltpu.VMEM((nbp, LANES), F32),
            pltpu.VMEM((HEADS_PER_BLOCK, nbp, ATT_T), F32),
            pltpu.VMEM((HEADS_PER_BLOCK, LANES, ATT_T), BF16),
            pltpu.VMEM((HEADS_PER_BLOCK, 1, ATT_T), F32),
            pltpu.VMEM((HEADS_PER_BLOCK, LANES, ATT_T), F32),
        ],
        compiler_params=params,
        name="moba_attn",
    )(tab, qkv, qkv, qkv)


def _diff(qkv, tab, diff_lambda, subln_g, lambda_init):
    B, S, _ = qkv.shape
    grid, qspec, kspec, vspec, ospec, params = _attn_specs(B, S, 9, 12, 15)
    n_chains = 2 * HEADS_PER_BLOCK
    return pl.pallas_call(
        functools.partial(_diff_kernel, n_blocks=S // ATT_T, scale=DIFF_QK_DIM ** -0.5,
                          lambda_init=lambda_init),
        grid=grid,
        in_specs=[
            _SMEM_SPEC,
            pl.BlockSpec(diff_lambda.shape, lambda b, h, i: (0, 0)),
            pl.BlockSpec((1, LANES), lambda b, h, i: (0, 0)),
            qspec, kspec, vspec,
        ],
        out_specs=ospec,
        out_shape=jax.ShapeDtypeStruct((B, S, MIX_W), BF16),
        scratch_shapes=[
            pltpu.VMEM((HEADS_PER_BLOCK, S, LANES), BF16),
            pltpu.VMEM((HEADS_PER_BLOCK, LANES, S), BF16),
            pltpu.VMEM((HEADS_PER_BLOCK, 1, 1), F32),
            pltpu.VMEM((n_chains, LANES, ATT_T), BF16),
            pltpu.VMEM((n_chains, 1, ATT_T), F32),
            pltpu.VMEM((n_chains, LANES, ATT_T), F32),
        ],
        compiler_params=params,
        name="diff_attn",
    )(tab, diff_lambda, subln_g, qkv, qkv, qkv)


def _sb(qkv):
    B, S, _ = qkv.shape
    grid, qspec, kspec, vspec, ospec, params = _attn_specs(B, S, 18, 21, 24)
    return pl.pallas_call(
        functools.partial(_sb_kernel, n_blocks=S // ATT_T, scale=HEAD_DIM ** -0.5),
        grid=grid,
        in_specs=[qspec, kspec, vspec],
        out_specs=ospec,
        out_shape=jax.ShapeDtypeStruct((B, S, MIX_W), BF16),
        scratch_shapes=[
            pltpu.VMEM((LANES, S), BF16),
            pltpu.VMEM((HEADS_PER_BLOCK, LANES, ATT_T), BF16),
            pltpu.VMEM((HEADS_PER_BLOCK, 1, ATT_T), F32),
            pltpu.VMEM((HEADS_PER_BLOCK, LANES, ATT_T), F32),
        ],
        compiler_params=params,
        name="sb_attn",
    )(qkv, qkv, qkv)


def _merge_ln_kernel(x_ref, om_ref, od_ref, os_ref, wg_ref, bg_ref, wbr_ref, wo_ref, lg_ref, lb_ref,
                     o_ref, *, alpha, d_model):
    x = x_ref[...]
    xb = x.astype(BF16)
    merged = None
    for b, ob_ref in enumerate((om_ref, od_ref, os_ref)):
        glogit = _dot(xb, wg_ref[:, b * d_model:(b + 1) * d_model]) + bg_ref[b:b + 1, :]
        term = jax.nn.sigmoid(glogit) * _dot(ob_ref[...], wbr_ref[b])
        merged = term if merged is None else merged + term
    y = alpha * x + _dot(merged.astype(BF16), wo_ref[...])
    o_ref[...] = _layer_norm(y, lg_ref[...], lb_ref[...])


def _merge_ln(x, o_m, o_d, o_s, w_gate, b_gate, w_br, w_out, lg, lb, l, alpha, tm=512):
    T, D = x.shape
    return pl.pallas_call(
        functools.partial(_merge_ln_kernel, alpha=alpha, d_model=D),
        grid=(T // tm,),
        in_specs=[
            pl.BlockSpec((tm, D), lambda i: (i, 0)),
            pl.BlockSpec((tm, MIX_W), lambda i: (i, 0)),
            pl.BlockSpec((tm, MIX_W), lambda i: (i, 0)),
            pl.BlockSpec((tm, MIX_W), lambda i: (i, 0)),
            pl.BlockSpec((None, D, N_BRANCH * D), lambda i: (l, 0, 0)),
            pl.BlockSpec((None, N_BRANCH, D), lambda i: (l, 0, 0)),
            pl.BlockSpec((None, N_BRANCH, MIX_W, D), lambda i: (l, 0, 0, 0)),
            pl.BlockSpec((None, D, D), lambda i: (l, 0, 0)),
            pl.BlockSpec((1, D), lambda i: (0, 0)),
            pl.BlockSpec((1, D), lambda i: (0, 0)),
        ],
        out_specs=pl.BlockSpec((tm, D), lambda i: (i, 0)),
        out_shape=jax.ShapeDtypeStruct((T, D), F32),
        compiler_params=pltpu.CompilerParams(
            dimension_semantics=("parallel",), vmem_limit_bytes=VMEM_LIMIT),
        name="merge_ln",
    )(x, o_m, o_d, o_s, w_gate, b_gate, w_br, w_out, lg, lb)


def _alibi_slopes(n):
    return (2.0 ** (-8.0 * np.arange(1, n + 1, dtype=np.float32) / n)).astype(np.float32)


def _slope_table(slopes):
    s2 = slopes.astype(np.float64) * LOG2E
    return np.concatenate([s2, 1.0 / (s2 * ATT_T)]).astype(np.float32)


def kernel(x, ln_g, ln_b, ffn_w_gate, ffn_w_up, ffn_w_down, w_in, b_gate, diff_lambda, diff_subln_g,
           w_br_moba, w_br_diff, w_br_sb, w_out):
    B, S, D = x.shape
    depth = ln_g.shape[0]
    assert S % ATT_T == 0 and ATT_T == MOBA_BLOCK
    assert w_in.shape[-1] == QKV_W + N_BRANCH * D
    alpha = (2.0 * depth) ** 0.25

    wg = ffn_w_gate.astype(BF16)
    wu = ffn_w_up.astype(BF16)
    wd = ffn_w_down.astype(BF16)
    w_qkv = w_in[:, :, :QKV_W].astype(BF16)
    w_gate = w_in[:, :, QKV_W:].astype(BF16)
    w_br = jnp.stack([w_br_moba, w_br_diff, w_br_sb], axis=1).astype(BF16)
    w_out_b = w_out.astype(BF16)
    slopes = _alibi_slopes(2 * N_HEADS)
    tab_moba = jnp.asarray(_slope_table(slopes[0::2]))
    tab_diff = jnp.asarray(_slope_table(slopes[1::2]))
    subln_g = jnp.tile(diff_subln_g.astype(F32), (1, HEADS_PER_BLOCK))

    h = x.reshape(B * S, D)
    for l in range(depth):
        lg = ln_g[l][:, None, :]
        lb = ln_b[l][:, None, :]
        h = _ffn_ln(h, wg, wu, wd, lg[0], lb[0], l, 0, alpha)
        qkv = _qkv_proj(h, w_qkv, l).reshape(B, S, QKV_W)
        lambda_init = 0.8 - 0.6 * math.exp(-0.3 * l)
        o_m = _moba(qkv, tab_moba)
        o_d = _diff(qkv, tab_diff, diff_lambda[l], subln_g[l][None, :], lambda_init)
        o_s = _sb(qkv)
        T = B * S
        h = _merge_ln(h, o_m.reshape(T, MIX_W), o_d.reshape(T, MIX_W), o_s.reshape(T, MIX_W),
                      w_gate, b_gate, w_br, w_out_b, lg[1], lb[1], l, alpha)
        h = _ffn_ln(h, wg, wu, wd, lg[2], lb[2], l, 1, alpha)
    return h.reshape(B, S, D)
```

```python
import functools
import math

import numpy as np
import jax
import jax.numpy as jnp
from jax import lax
from jax.experimental import pallas as pl
from jax.experimental.pallas import tpu as pltpu

F32 = jnp.float32
BF16 = jnp.bfloat16

HEAD_DIM = 64
N_HEADS = 6
DIFF_QK_DIM = HEAD_DIM // 2
MOBA_BLOCK = 256
MOBA_TOPK = 3
N_BRANCH = 3
LN_EPS = 1e-5
SUBLN_EPS = 1e-5

LANES = 128
SUBLANES = 8
HEADS_PER_BLOCK = LANES // HEAD_DIM
N_HEAD_BLOCKS = N_HEADS // HEADS_PER_BLOCK
MIX_W = N_HEADS * HEAD_DIM
QKV_W = 9 * MIX_W
ATT_T = 256
NEG = -1e30
SB_EXIT = -110.0
LOG2E = math.log2(math.e)
N_BIAS_COLS = 3
BOUND_SLACK = 1.01
UNDERFLOW_GUARD = 100.0
SKIP_BITS = 250.0
MIN_DENOM = 2.0 ** -60
ITEMS_PER_TRIP = 16
SCORE_LOOKAHEAD = 3
VMEM_LIMIT = 56 * 1024 * 1024


def _dot(a, b, precision=None):
    return jnp.dot(a, b, precision=precision, preferred_element_type=F32)


def _layer_norm(y, g, b):
    mu = jnp.mean(y, axis=-1, keepdims=True)
    yc = y - mu
    var = jnp.mean(yc * yc, axis=-1, keepdims=True)
    return yc * lax.rsqrt(var + LN_EPS) * g + b


def _ffn_ln_kernel(x_ref, wg_ref, wu_ref, wd_ref, lg_ref, lb_ref, o_ref, xb_ref, acc_ref, *, alpha, n_k):
    k = pl.program_id(1)

    @pl.when(k == 0)
    def _init():
        xb_ref[...] = x_ref[...].astype(BF16)
        acc_ref[...] = jnp.zeros_like(acc_ref)

    xb = xb_ref[...]
    gate = _dot(xb, wg_ref[...])
    up = _dot(xb, wu_ref[...])
    hid = gate * jax.nn.sigmoid(gate) * up
    acc_ref[...] += _dot(hid.astype(BF16), wd_ref[...])

    @pl.when(k == n_k - 1)
    def _fin():
        y = alpha * x_ref[...] + 0.5 * acc_ref[...]
        o_ref[...] = _layer_norm(y, lg_ref[...], lb_ref[...])


def _ffn_ln(x, wg, wu, wd, lg, lb, l, j, alpha, tm=1024, tf=256):
    T, D = x.shape
    FF = wg.shape[-1]
    n_k = FF // tf
    return pl.pallas_call(
        functools.partial(_ffn_ln_kernel, alpha=alpha, n_k=n_k),
        grid=(T // tm, n_k),
        in_specs=[
            pl.BlockSpec((tm, D), lambda i, k: (i, 0)),
            pl.BlockSpec((None, None, D, tf), lambda i, k: (l, j, 0, k)),
            pl.BlockSpec((None, None, D, tf), lambda i, k: (l, j, 0, k)),
            pl.BlockSpec((None, None, tf, D), lambda i, k: (l, j, k, 0)),
            pl.BlockSpec((1, D), lambda i, k: (0, 0)),
            pl.BlockSpec((1, D), lambda i, k: (0, 0)),
        ],
        out_specs=pl.BlockSpec((tm, D), lambda i, k: (i, 0)),
        out_shape=jax.ShapeDtypeStruct((T, D), F32),
        scratch_shapes=[pltpu.VMEM((tm, D), BF16), pltpu.VMEM((tm, D), F32)],
        compiler_params=pltpu.CompilerParams(
            dimension_semantics=("parallel", "arbitrary"), vmem_limit_bytes=VMEM_LIMIT),
        name="ffn_ln",
    )(x, wg, wu, wd, lg, lb)


def _proj_kernel(x_ref, w_ref, o_ref, xb_ref):
    @pl.when(pl.program_id(1) == 0)
    def _cast():
        xb_ref[...] = x_ref[...].astype(BF16)

    o_ref[...] = _dot(xb_ref[...], w_ref[...]).astype(o_ref.dtype)


def _qkv_proj(x, w_in, l, tm=1024, tn=1152):
    T, D = x.shape
    return pl.pallas_call(
        _proj_kernel,
        grid=(T // tm, QKV_W // tn),
        in_specs=[
            pl.BlockSpec((tm, D), lambda i, j: (i, 0)),
            pl.BlockSpec((None, D, tn), lambda i, j: (l, 0, j)),
        ],
        out_specs=pl.BlockSpec((tm, tn), lambda i, j: (i, j)),
        out_shape=jax.ShapeDtypeStruct((T, QKV_W), BF16),
        scratch_shapes=[pltpu.VMEM((tm, D), BF16)],
        compiler_params=pltpu.CompilerParams(
            dimension_semantics=("parallel", "arbitrary"), vmem_limit_bytes=VMEM_LIMIT),
        name="qkv_proj",
    )(x, w_in)


def _lane_iota():
    return lax.broadcasted_iota(jnp.int32, (1, LANES), 1)


def _head_lanes(hh):
    lane = _lane_iota()
    return (lane >= hh * HEAD_DIM) & (lane < (hh + 1) * HEAD_DIM)


def _bias_lanes(hh):
    lane = _lane_iota()
    lo = (1 - hh) * HEAD_DIM
    return (lane >= lo) & (lane < lo + N_BIAS_COLS)


def _blk(j):
    return pl.ds(pl.multiple_of(j * ATT_T, ATT_T), ATT_T)


def _key_query_iotas():
    krow = lax.broadcasted_iota(jnp.int32, (ATT_T, ATT_T), 0)
    qcol = lax.broadcasted_iota(jnp.int32, (ATT_T, ATT_T), 1)
    return krow, qcol


def _build_softmax_operands(tab_ref, hb, k_ref, v_ref, kp_ref, vt_ref, kmax_ref, *, n_blocks, scale):
    lane = _lane_iota()
    pos = lax.broadcasted_iota(jnp.int32, (ATT_T, LANES), 0).astype(F32)
    bias_cols = []
    for hh in range(HEADS_PER_BLOCK):
        b = pos * tab_ref[hb * HEADS_PER_BLOCK + hh]
        lo = (1 - hh) * HEAD_DIM
        cols = jnp.zeros_like(b)
        for i in range(N_BIAS_COLS):
            piece = b.astype(BF16).astype(F32)
            cols = jnp.where(lane == lo + i, piece, cols)
            b = b - piece
        bias_cols.append(cols)
    kmax_ref[...] = jnp.zeros_like(kmax_ref)

    def body(n, c):
        kf = k_ref[_blk(n), :].astype(F32) * (scale * LOG2E)
        vf = v_ref[_blk(n), :].astype(F32)
        for hh in range(HEADS_PER_BLOCK):
            hm = _head_lanes(hh)
            kp_ref[hh, _blk(n), :] = jnp.where(hm, kf, bias_cols[hh]).astype(BF16)
            vt_ref[hh, :, _blk(n)] = jnp.where(hm, vf, 1.0).T.astype(BF16)
            norm2 = jnp.sum(jnp.where(hm, kf * kf, 0.0), axis=1, keepdims=True)
            kmax_ref[hh] = jnp.maximum(kmax_ref[hh], jnp.max(norm2, axis=0, keepdims=True))
        return c

    lax.fori_loop(0, n_blocks, body, 0)


def _chain_qt(c, hh, seg, qf, qt_ref):
    row = lax.broadcasted_iota(jnp.int32, (LANES, ATT_T), 0)
    lo = (1 - hh) * HEAD_DIM
    qsel = jnp.where(seg, qf, 0.0).T
    qt_ref[c] = jnp.where((row >= lo) & (row < lo + N_BIAS_COLS), 1.0, qsel).astype(BF16)
    return qsel, jnp.sum(qsel * qsel, axis=0, keepdims=True)


def _diag_blocks(chains, qi, qn2, slope2, kp_ref, vt_ref, kmax_ref, qt_ref, ref_ref, acc_ref):
    krow, qcol = _key_query_iotas()
    causal = krow <= qcol
    r = lax.broadcasted_iota(jnp.int32, (1, ATT_T), 1).astype(F32)
    scores = [_dot(kp_ref[hh, _blk(qi), :], qt_ref[c]) for c, hh in enumerate(chains)]
    gaps = []
    for c, hh in enumerate(chains):
        s = jnp.where(causal, scores[c], NEG)
        m = jnp.max(s, axis=0, keepdims=True)
        bound = jnp.sqrt(qn2[c] * kmax_ref[hh]) * BOUND_SLACK + slope2[hh] * r
        ref = jnp.maximum(m, bound - UNDERFLOW_GUARD)
        ref_ref[c] = ref
        acc_ref[c] = _dot(vt_ref[hh, :, _blk(qi)], jnp.exp2(s - ref).astype(BF16))
        gaps.append(jnp.max(bound - ref))
    return gaps


def _first_block(gap, inv_blk, qi):
    reach = jnp.minimum((gap + SKIP_BITS) * inv_blk, 1e6).astype(jnp.int32)
    return jnp.maximum(qi - 1 - reach, 0)


def _fixed_ref_steps(items, kp_ref, vt_ref, qt_ref, ref_ref, acc_ref):
    def scores(i):
        c, hh, n, _ = items[i]
        return _dot(kp_ref[hh, _blk(n), :], qt_ref[c])

    pending = [scores(i) for i in range(min(SCORE_LOOKAHEAD, len(items)))]
    for i, (c, hh, n, off) in enumerate(items):
        p = jnp.exp2(pending.pop(0) - (ref_ref[c] - off)).astype(BF16)
        if i + SCORE_LOOKAHEAD < len(items):
            pending.append(scores(i + SCORE_LOOKAHEAD))
        acc_ref[c] += _dot(vt_ref[hh, :, _blk(n)], p)


def _online_diag(c, hh, qi, kp_ref, vt_ref, qt_ref, m_ref, acc_ref):
    krow, qcol = _key_query_iotas()
    s = _dot(kp_ref[hh, _blk(qi), :], qt_ref[c])
    s = jnp.where(krow <= qcol, s, NEG)
    m = jnp.max(s, axis=0, keepdims=True)
    m_ref[c] = m
    acc_ref[c] = _dot(vt_ref[hh, :, _blk(qi)], jnp.exp2(s - m).astype(BF16))


def _online_step(c, hh, n, off, kp_ref, vt_ref, qt_ref, m_ref, acc_ref):
    s = _dot(kp_ref[hh, _blk(n), :], qt_ref[c])
    m_old = m_ref[c]
    m_new = jnp.maximum(m_old, jnp.max(s, axis=0, keepdims=True) + off)
    p = jnp.exp2(s - (m_new - off)).astype(BF16)
    acc_ref[c] = acc_ref[c] * jnp.exp2(m_old - m_new) + _dot(vt_ref[hh, :, _blk(n)], p)
    m_ref[c] = m_new


def _softmax_sweep(qi, chains, first, off_fn, kp_ref, vt_ref, qt_ref, ref_ref, acc_ref):
    refs = (kp_ref, vt_ref, qt_ref, ref_ref, acc_ref)

    def run_fixed(lo, hi, heads):
        active = [(c, hh) for c, hh in enumerate(chains) if hh in heads]
        unroll = max(1, ITEMS_PER_TRIP // len(active))

        def items(n, k):
            return [(c, hh, n + j, off_fn(hh, n + j)) for j in range(k) for c, hh in active]

        start = lo
        while unroll >= 1:
            trips = jnp.maximum(hi - start, 0) // unroll

            def body(i, carry, start=start, unroll=unroll):
                _fixed_ref_steps(items(start + i * unroll, unroll), *refs)
                return carry

            lax.fori_loop(0, trips, body, 0)
            start = start + trips * unroll
            unroll //= 2

    both = jnp.maximum(first[0], first[1])
    run_fixed(first[0], both, (0,))
    run_fixed(first[1], both, (1,))
    run_fixed(both, qi, (0, 1))

    smallest = None
    for c, hh in enumerate(chains):
        r = (1 - hh) * HEAD_DIM
        d = jnp.min(acc_ref[c][r:r + 1, :])
        smallest = d if smallest is None else jnp.minimum(smallest, d)

    @pl.when(jnp.logical_not(smallest >= MIN_DENOM))
    def _redo():
        for c, hh in enumerate(chains):
            _online_diag(c, hh, qi, *refs)

        def body(n, carry):
            for c, hh in enumerate(chains):
                _online_step(c, hh, n, off_fn(hh, n), *refs)
            return carry

        lax.fori_loop(0, qi, body, 0)


def _normalized(acc, hh):
    r = (1 - hh) * HEAD_DIM
    return acc * (1.0 / acc[r:r + 1, :])


def _row_is_head0():
    return lax.broadcasted_iota(jnp.int32, (LANES, ATT_T), 0) < HEAD_DIM


def _moba_kernel(tab_ref, q_ref, k_ref, v_ref, o_ref, kp_ref, vt_ref, kmax_ref, kmean_ref, selm_ref,
                 qt_ref, ref_ref, acc_ref, *, n_blocks, scale):
    hb = pl.program_id(1)
    qi = pl.program_id(2)
    nbp = kmean_ref.shape[0]

    @pl.when(qi == 0)
    def _first_tile():
        _build_softmax_operands(tab_ref, hb, k_ref, v_ref, kp_ref, vt_ref, kmax_ref,
                                n_blocks=n_blocks, scale=scale)
        kmean_ref[...] = jnp.zeros_like(kmean_ref)

        def body(n, c):
            kmean_ref[pl.ds(n, 1), :] = jnp.mean(k_ref[_blk(n), :].astype(F32), axis=0, keepdims=True)
            return c

        lax.fori_loop(0, n_blocks, body, 0)

    qf = q_ref[...].astype(F32)
    blk_id = lax.broadcasted_iota(jnp.int32, (nbp, ATT_T), 0)
    blk_f = blk_id.astype(F32)
    chains = list(range(HEADS_PER_BLOCK))
    slope2 = [tab_ref[hb * HEADS_PER_BLOCK + hh] for hh in chains]
    qn2 = []
    for hh in chains:
        hm = _head_lanes(hh)
        qsel, norm2 = _chain_qt(hh, hh, hm, qf, qt_ref)
        qn2.append(norm2)

        gate = _dot(jnp.where(hm, kmean_ref[...], 0.0), qsel, precision=lax.Precision.HIGHEST)
        g = jnp.where(blk_id < qi, gate, -jnp.inf)
        sel = jnp.zeros(g.shape, dtype=jnp.bool_)
        for _ in range(MOBA_TOPK):
            mx = jnp.max(g, axis=0, keepdims=True)
            pick = jnp.min(jnp.where(g == mx, blk_f, float(nbp)), axis=0, keepdims=True)
            hit = blk_f == pick
            sel = sel | (hit & (mx > -jnp.inf))
            g = jnp.where(hit, -jnp.inf, g)
        selm_ref[hh] = jnp.where(sel, 0.0, NEG)

    gaps = _diag_blocks(chains, qi, qn2, slope2, kp_ref, vt_ref, kmax_ref, qt_ref, ref_ref, acc_ref)
    first = [_first_block(gaps[hh], tab_ref[N_HEADS + hb * HEADS_PER_BLOCK + hh], qi) for hh in chains]

    def off_fn(hh, n):
        return slope2[hh] * ((n - qi) * ATT_T).astype(F32) + selm_ref[hh, pl.ds(n, 1), :]

    _softmax_sweep(qi, chains, first, off_fn, kp_ref, vt_ref, qt_ref, ref_ref, acc_ref)

    o = jnp.where(_row_is_head0(), _normalized(acc_ref[0], 0), _normalized(acc_ref[1], 1))
    o_ref[...] = o.T.astype(o_ref.dtype)


def _diff_kernel(tab_ref, lam_ref, g_ref, q_ref, k_ref, v_ref, o_ref, kp_ref, vt_ref, kmax_ref, qt_ref,
                 ref_ref, acc_ref, *, n_blocks, scale, lambda_init):
    hb = pl.program_id(1)
    qi = pl.program_id(2)
    lane = _lane_iota()

    @pl.when(qi == 0)
    def _first_tile():
        _build_softmax_operands(tab_ref, hb, k_ref, v_ref, kp_ref, vt_ref, kmax_ref,
                                n_blocks=n_blocks, scale=scale)

    lf = lam_ref[...].astype(F32)
    lam = (jnp.exp(jnp.sum(lf[0:1] * lf[1:2], axis=-1, keepdims=True))
           - jnp.exp(jnp.sum(lf[2:3] * lf[3:4], axis=-1, keepdims=True)) + lambda_init)

    qf = q_ref[...].astype(F32)
    chains = [hh for hh in range(HEADS_PER_BLOCK) for _ in range(2)]
    slope2 = [tab_ref[hb * HEADS_PER_BLOCK + hh] for hh in range(HEADS_PER_BLOCK)]
    qn2 = []
    for c, hh in enumerate(chains):
        lo = hh * HEAD_DIM + (c % 2) * DIFF_QK_DIM
        qn2.append(_chain_qt(c, hh, (lane >= lo) & (lane < lo + DIFF_QK_DIM), qf, qt_ref)[1])
    gaps = _diag_blocks(chains, qi, qn2, slope2, kp_ref, vt_ref, kmax_ref, qt_ref, ref_ref, acc_ref)
    first = [None] * HEADS_PER_BLOCK
    for c, hh in enumerate(chains):
        fb = _first_block(gaps[c], tab_ref[N_HEADS + hb * HEADS_PER_BLOCK + hh], qi)
        first[hh] = fb if first[hh] is None else jnp.minimum(first[hh], fb)

    def off_fn(hh, n):
        return slope2[hh] * ((n - qi) * ATT_T).astype(F32)

    _softmax_sweep(qi, chains, first, off_fn, kp_ref, vt_ref, qt_ref, ref_ref, acc_ref)

    is_h0 = _row_is_head0()
    outs = []
    for hh in range(HEADS_PER_BLOCK):
        o = _normalized(acc_ref[2 * hh], hh) - lam * _normalized(acc_ref[2 * hh + 1], hh)
        in_head = is_h0 if hh == 0 else jnp.logical_not(is_h0)
        ms = jnp.sum(jnp.where(in_head, o * o, 0.0), axis=0, keepdims=True) * (1.0 / HEAD_DIM)
        outs.append(o * lax.rsqrt(ms + SUBLN_EPS))
    o = jnp.where(is_h0, outs[0], outs[1]).T * (g_ref[...] * (1.0 - lambda_init))
    o_ref[...] = o.astype(o_ref.dtype)


def _log_sigmoids(z):
    l = jnp.log(1.0 + jnp.exp(-jnp.abs(z)))
    return jnp.minimum(-z, 0.0) - l, jnp.minimum(z, 0.0) - l


def _suffix_sums(upper, lg):
    hi = lg.astype(BF16)
    lo = (lg - hi.astype(F32)).astype(BF16)
    return _dot(upper, hi) + _dot(upper, lo)


def _sb_block(n, diag, k_ref, vt_ref, qt_ref, carry_ref, acc_ref):
    krow, qcol = _key_query_iotas()
    before = krow < qcol
    upper = (qcol > krow).astype(BF16)
    heads = range(HEADS_PER_BLOCK)
    z = [_dot(k_ref[_blk(n), :], qt_ref[hh]) for hh in heads]
    log_sig, log_surv, tails = [], [], []
    for hh in heads:
        lneg, lpos = _log_sigmoids(z[hh])
        if diag:
            lneg = jnp.where(before, lneg, 0.0)
        log_surv.append(lneg)
        log_sig.append(lpos)
        tails.append(_suffix_sums(upper, lneg))
    worst = None
    for hh in heads:
        if diag:
            w = jnp.where(before, jnp.exp(log_sig[hh] + tails[hh]), 0.0)
            acc_ref[hh] = _dot(vt_ref[:, _blk(n)], w.astype(BF16))
            carry = jnp.sum(log_surv[hh], axis=0, keepdims=True)
        else:
            carry = carry_ref[hh]
            w = jnp.exp(log_sig[hh] + (tails[hh] + carry))
            acc_ref[hh] += _dot(vt_ref[:, _blk(n)], w.astype(BF16))
            carry = carry + jnp.sum(log_surv[hh], axis=0, keepdims=True)
        carry_ref[hh] = carry
        top = jnp.max(carry)
        worst = top if worst is None else jnp.maximum(worst, top)
    return worst


def _sb_kernel(q_ref, k_ref, v_ref, o_ref, vt_ref, qt_ref, carry_ref, acc_ref, *, n_blocks, scale):
    qi = pl.program_id(2)

    @pl.when(qi == 0)
    def _first_tile():
        def body(n, c):
            vt_ref[:, _blk(n)] = v_ref[_blk(n), :].astype(F32).T.astype(BF16)
            return c

        lax.fori_loop(0, n_blocks, body, 0)

    qf = q_ref[...].astype(F32) * scale
    for hh in range(HEADS_PER_BLOCK):
        qt_ref[hh] = jnp.where(_head_lanes(hh), qf, 0.0).T.astype(BF16)
    refs = (k_ref, vt_ref, qt_ref, carry_ref, acc_ref)
    worst = _sb_block(qi, True, *refs)

    def cond(c):
        j, live = c
        return (j >= 0) & live

    def body(c):
        j, _ = c
        return j - 1, _sb_block(j, False, *refs) > SB_EXIT

    lax.while_loop(cond, body, (qi - 1, worst > SB_EXIT))

    o = jnp.where(_row_is_head0(), acc_ref[0], acc_ref[1])
    o_ref[...] = o.T.astype(o_ref.dtype)


def _attn_specs(B, S, q_base, k_base, v_base):
    nq = S // ATT_T
    qspec = pl.BlockSpec((None, ATT_T, LANES), lambda b, h, i: (b, i, q_base + h))
    kspec = pl.BlockSpec((None, S, LANES), lambda b, h, i: (b, 0, k_base + h))
    vspec = pl.BlockSpec((None, S, LANES), lambda b, h, i: (b, 0, v_base + h))
    ospec = pl.BlockSpec((None, ATT_T, LANES), lambda b, h, i: (b, i, h))
    grid = (B, N_HEAD_BLOCKS, nq)
    params = pltpu.CompilerParams(
        dimension_semantics=("parallel", "parallel", "arbitrary"), vmem_limit_bytes=VMEM_LIMIT)
    return grid, qspec, kspec, vspec, ospec, params


_SMEM_SPEC = pl.BlockSpec(memory_space=pltpu.SMEM)


def _moba(qkv, tab):
    B, S, _ = qkv.shape
    grid, qspec, kspec, vspec, ospec, params = _attn_specs(B, S, 0, 3, 6)
    n_blocks = S // MOBA_BLOCK
    nbp = -(-n_blocks // SUBLANES) * SUBLANES
    return pl.pallas_call(
        functools.partial(_moba_kernel, n_blocks=n_blocks, scale=HEAD_DIM ** -0.5),
        grid=grid,
        in_specs=[_SMEM_SPEC, qspec, kspec, vspec],
        out_specs=ospec,
        out_shape=jax.ShapeDtypeStruct((B, S, MIX_W), BF16),
        scratch_shapes=[
            pltpu.VMEM((HEADS_PER_BLOCK, S, LANES), BF16),
            pltpu.VMEM((HEADS_PER_BLOCK, LANES, S), BF16),
            pltpu.VMEM((HEADS_PER_BLOCK, 1, 1), F32),
            pltpu.VMEM((nbp, LANES), F32),
            pltpu.VMEM((HEADS_PER_BLOCK, nbp, ATT_T), F32),
            pltpu.VMEM((HEADS_PER_BLOCK, LANES, ATT_T), BF16),
            pltpu.VMEM((HEADS_PER_BLOCK, 1, ATT_T), F32),
            pltpu.VMEM((HEADS_PER_BLOCK, LANES, ATT_T), F32),
        ],
        compiler_params=params,
        name="moba_attn",
    )(tab, qkv, qkv, qkv)


def _diff(qkv, tab, diff_lambda, subln_g, lambda_init):
    B, S, _ = qkv.shape
    grid, qspec, kspec, vspec, ospec, params = _attn_specs(B, S, 9, 12, 15)
    n_chains = 2 * HEADS_PER_BLOCK
    return pl.pallas_call(
        functools.partial(_diff_kernel, n_blocks=S // ATT_T, scale=DIFF_QK_DIM ** -0.5,
                          lambda_init=lambda_init),
        grid=grid,
        in_specs=[
            _SMEM_SPEC,
            pl.BlockSpec(diff_lambda.shape, lambda b, h, i: (0, 0)),
            pl.BlockSpec((1, LANES), lambda b, h, i: (0, 0)),
            qspec, kspec, vspec,
        ],
        out_specs=ospec,
        out_shape=jax.ShapeDtypeStruct((B, S, MIX_W), BF16),
        scratch_shapes=[
            pltpu.VMEM((HEADS_PER_BLOCK, S, LANES), BF16),
            pltpu.VMEM((HEADS_PER_BLOCK, LANES, S), BF16),
            pltpu.VMEM((HEADS_PER_BLOCK, 1, 1), F32),
            pltpu.VMEM((n_chains, LANES, ATT_T), BF16),
            pltpu.VMEM((n_chains, 1, ATT_T), F32),
            pltpu.VMEM((n_chains, LANES, ATT_T), F32),
        ],
        compiler_params=params,
        name="diff_attn",
    )(tab, diff_lambda, subln_g, qkv, qkv, qkv)


def _sb(qkv):
    B, S, _ = qkv.shape
    grid, qspec, kspec, vspec, ospec, params = _attn_specs(B, S, 18, 21, 24)
    return pl.pallas_call(
        functools.partial(_sb_kernel, n_blocks=S // ATT_T, scale=HEAD_DIM ** -0.5),
        grid=grid,
        in_specs=[qspec, kspec, vspec],
        out_specs=ospec,
        out_shape=jax.ShapeDtypeStruct((B, S, MIX_W), BF16),
        scratch_shapes=[
            pltpu.VMEM((LANES, S), BF16),
            pltpu.VMEM((HEADS_PER_BLOCK, LANES, ATT_T), BF16),
            pltpu.VMEM((HEADS_PER_BLOCK, 1, ATT_T), F32),
            pltpu.VMEM((HEADS_PER_BLOCK, LANES, ATT_T), F32),
        ],
        compiler_params=params,
        name="sb_attn",
    )(qkv, qkv, qkv)


def _merge_ln_kernel(x_ref, om_ref, od_ref, os_ref, wg_ref, bg_ref, wbr_ref, wo_ref, lg_ref, lb_ref,
                     o_ref, *, alpha, d_model):
    x = x_ref[...]
    xb = x.astype(BF16)
    merged = None
    for b, ob_ref in enumerate((om_ref, od_ref, os_ref)):
        glogit = _dot(xb, wg_ref[:, b * d_model:(b + 1) * d_model]) + bg_ref[b:b + 1, :]
        term = jax.nn.sigmoid(glogit) * _dot(ob_ref[...], wbr_ref[b])
        merged = term if merged is None else merged + term
    y = alpha * x + _dot(merged.astype(BF16), wo_ref[...])
    o_ref[...] = _layer_norm(y, lg_ref[...], lb_ref[...])


def _merge_ln(x, o_m, o_d, o_s, w_gate, b_gate, w_br, w_out, lg, lb, l, alpha, tm=512):
    T, D = x.shape
    return pl.pallas_call(
        functools.partial(_merge_ln_kernel, alpha=alpha, d_model=D),
        grid=(T // tm,),
        in_specs=[
            pl.BlockSpec((tm, D), lambda i: (i, 0)),
            pl.BlockSpec((tm, MIX_W), lambda i: (i, 0)),
            pl.BlockSpec((tm, MIX_W), lambda i: (i, 0)),
            pl.BlockSpec((tm, MIX_W), lambda i: (i, 0)),
            pl.BlockSpec((None, D, N_BRANCH * D), lambda i: (l, 0, 0)),
            pl.BlockSpec((None, N_BRANCH, D), lambda i: (l, 0, 0)),
            pl.BlockSpec((None, N_BRANCH, MIX_W, D), lambda i: (l, 0, 0, 0)),
            pl.BlockSpec((None, D, D), lambda i: (l, 0, 0)),
            pl.BlockSpec((1, D), lambda i: (0, 0)),
            pl.BlockSpec((1, D), lambda i: (0, 0)),
        ],
        out_specs=pl.BlockSpec((tm, D), lambda i: (i, 0)),
        out_shape=jax.ShapeDtypeStruct((T, D), F32),
        compiler_params=pltpu.CompilerParams(
            dimension_semantics=("parallel",), vmem_limit_bytes=VMEM_LIMIT),
        name="merge_ln",
    )(x, o_m, o_d, o_s, w_gate, b_gate, w_br, w_out, lg, lb)


def _alibi_slopes(n):
    return (2.0 ** (-8.0 * np.arange(1, n + 1, dtype=np.float32) / n)).astype(np.float32)


def _slope_table(slopes):
    s2 = slopes.astype(np.float64) * LOG2E
    return np.concatenate([s2, 1.0 / (s2 * ATT_T)]).astype(np.float32)


def kernel(x, ln_g, ln_b, ffn_w_gate, ffn_w_up, ffn_w_down, w_in, b_gate, diff_lambda, diff_subln_g,
           w_br_moba, w_br_diff, w_br_sb, w_out):
    B, S, D = x.shape
    depth = ln_g.shape[0]
    assert S % ATT_T == 0 and ATT_T == MOBA_BLOCK
    assert w_in.shape[-1] == QKV_W + N_BRANCH * D
    alpha = (2.0 * depth) ** 0.25

    wg = ffn_w_gate.astype(BF16)
    wu = ffn_w_up.astype(BF16)
    wd = ffn_w_down.astype(BF16)
    w_qkv = w_in[:, :, :QKV_W].astype(BF16)
    w_gate = w_in[:, :, QKV_W:].astype(BF16)
    w_br = jnp.stack([w_br_moba, w_br_diff, w_br_sb], axis=1).astype(BF16)
    w_out_b = w_out.astype(BF16)
    slopes = _alibi_slopes(2 * N_HEADS)
    tab_moba = jnp.asarray(_slope_table(slopes[0::2]))
    tab_diff = jnp.asarray(_slope_table(slopes[1::2]))
    subln_g = jnp.tile(diff_subln_g.astype(F32), (1, HEADS_PER_BLOCK))

    h = x.reshape(B * S, D)
    for l in range(depth):
        lg = ln_g[l][:, None, :]
        lb = ln_b[l][:, None, :]
        h = _ffn_ln(h, wg, wu, wd, lg[0], lb[0], l, 0, alpha)
        qkv = _qkv_proj(h, w_qkv, l).reshape(B, S, QKV_W)
        lambda_init = 0.8 - 0.6 * math.exp(-0.3 * l)
        o_m = _moba(qkv, tab_moba)
        o_d = _diff(qkv, tab_diff, diff_lambda[l], subln_g[l][None, :], lambda_init)
        o_s = _sb(qkv)
        T = B * S
        h = _merge_ln(h, o_m.reshape(T, MIX_W), o_d.reshape(T, MIX_W), o_s.reshape(T, MIX_W),
                      w_gate, b_gate, w_br, w_out_b, lg[1], lb[1], l, alpha)
        h = _ffn_ln(h, wg, wu, wd, lg[2], lb[2], l, 1, alpha)
    return h.reshape(B, S, D)
```

```python
import functools
import math

import numpy as np
import jax
import jax.numpy as jnp
from jax import lax
from jax.experimental import pallas as pl
from jax.experimental.pallas import tpu as pltpu

F32 = jnp.float32
BF16 = jnp.bfloat16

HEAD_DIM = 64
N_HEADS = 6
DIFF_QK_DIM = HEAD_DIM // 2
MOBA_BLOCK = 256
MOBA_TOPK = 3
N_BRANCH = 3
LN_EPS = 1e-5
SUBLN_EPS = 1e-5

LANES = 128
SUBLANES = 8
HEADS_PER_BLOCK = LANES // HEAD_DIM
N_HEAD_BLOCKS = N_HEADS // HEADS_PER_BLOCK
MIX_W = N_HEADS * HEAD_DIM
QKV_W = 9 * MIX_W
ATT_T = 256
NEG = -1e30
SB_EXIT = -110.0
LOG2E = math.log2(math.e)
N_BIAS_COLS = 3
BOUND_SLACK = 1.01
UNDERFLOW_GUARD = 100.0
SKIP_BITS = 150.0
MIN_DENOM = 2.0 ** -60
ITEMS_PER_TRIP = 8
STAGE_DEPTH = 2
VMEM_LIMIT = 56 * 1024 * 1024


def _dot(a, b, precision=None):
    return jnp.dot(a, b, precision=precision, preferred_element_type=F32)


def _layer_norm(y, g, b):
    mu = jnp.mean(y, axis=-1, keepdims=True)
    yc = y - mu
    var = jnp.mean(yc * yc, axis=-1, keepdims=True)
    return yc * lax.rsqrt(var + LN_EPS) * g + b


def _ffn_ln_kernel(x_ref, wg_ref, wu_ref, wd_ref, lg_ref, lb_ref, o_ref, hid_ref, *, alpha, tf):
    x = x_ref[...]
    xb = x.astype(BF16)
    for c in range(hid_ref.shape[1] // tf):
        cols = slice(c * tf, (c + 1) * tf)
        gate = _dot(xb, wg_ref[:, cols])
        up = _dot(xb, wu_ref[:, cols])
        hid_ref[:, cols] = (gate * jax.nn.sigmoid(gate) * up).astype(BF16)
    y = alpha * x + 0.5 * _dot(hid_ref[...], wd_ref[...])
    o_ref[...] = _layer_norm(y, lg_ref[...], lb_ref[...])


def _ffn_ln(x, wg, wu, wd, lg, lb, l, j, alpha, tm=512, tf=256):
    T, D = x.shape
    FF = wg.shape[-1]
    resident = pl.Buffered(1)
    return pl.pallas_call(
        functools.partial(_ffn_ln_kernel, alpha=alpha, tf=tf),
        grid=(T // tm,),
        in_specs=[
            pl.BlockSpec((tm, D), lambda i: (i, 0)),
            pl.BlockSpec((None, None, D, FF), lambda i: (l, j, 0, 0), pipeline_mode=resident),
            pl.BlockSpec((None, None, D, FF), lambda i: (l, j, 0, 0), pipeline_mode=resident),
            pl.BlockSpec((None, None, FF, D), lambda i: (l, j, 0, 0), pipeline_mode=resident),
            pl.BlockSpec((1, D), lambda i: (0, 0)),
            pl.BlockSpec((1, D), lambda i: (0, 0)),
        ],
        out_specs=pl.BlockSpec((tm, D), lambda i: (i, 0)),
        out_shape=jax.ShapeDtypeStruct((T, D), F32),
        scratch_shapes=[pltpu.VMEM((tm, FF), BF16)],
        compiler_params=pltpu.CompilerParams(
            dimension_semantics=("parallel",), vmem_limit_bytes=VMEM_LIMIT),
        name="ffn_ln",
    )(x, wg, wu, wd, lg, lb)


def _proj_kernel(x_ref, w_ref, o_ref, *, tn):
    xb = x_ref[...].astype(BF16)
    for c in range(o_ref.shape[1] // tn):
        cols = slice(c * tn, (c + 1) * tn)
        o_ref[:, cols] = _dot(xb, w_ref[:, cols]).astype(o_ref.dtype)


def _qkv_proj(x, w_in, l, tm=512, tn=1152):
    T, D = x.shape
    return pl.pallas_call(
        functools.partial(_proj_kernel, tn=tn),
        grid=(T // tm,),
        in_specs=[
            pl.BlockSpec((tm, D), lambda i: (i, 0)),
            pl.BlockSpec((None, D, QKV_W), lambda i: (l, 0, 0), pipeline_mode=pl.Buffered(1)),
        ],
        out_specs=pl.BlockSpec((tm, QKV_W), lambda i: (i, 0)),
        out_shape=jax.ShapeDtypeStruct((T, QKV_W), BF16),
        compiler_params=pltpu.CompilerParams(
            dimension_semantics=("parallel",), vmem_limit_bytes=VMEM_LIMIT),
        name="qkv_proj",
    )(x, w_in)


def _lane_iota():
    return lax.broadcasted_iota(jnp.int32, (1, LANES), 1)


def _head_lanes(hh):
    lane = _lane_iota()
    return (lane >= hh * HEAD_DIM) & (lane < (hh + 1) * HEAD_DIM)


def _bias_lanes(hh):
    lane = _lane_iota()
    lo = (1 - hh) * HEAD_DIM
    return (lane >= lo) & (lane < lo + N_BIAS_COLS)


def _blk(j):
    return pl.ds(pl.multiple_of(j * ATT_T, ATT_T), ATT_T)


def _key_query_iotas():
    krow = lax.broadcasted_iota(jnp.int32, (ATT_T, ATT_T), 0)
    qcol = lax.broadcasted_iota(jnp.int32, (ATT_T, ATT_T), 1)
    return krow, qcol


def _build_softmax_operands(tab_ref, hb, k_ref, v_ref, kp_ref, vt_ref, kmax_ref, *, n_blocks, scale):
    lane = _lane_iota()
    pos = lax.broadcasted_iota(jnp.int32, (ATT_T, LANES), 0).astype(F32)
    bias_cols = []
    for hh in range(HEADS_PER_BLOCK):
        b = pos * tab_ref[hb * HEADS_PER_BLOCK + hh]
        lo = (1 - hh) * HEAD_DIM
        cols = jnp.zeros_like(b)
        for i in range(N_BIAS_COLS):
            piece = b.astype(BF16).astype(F32)
            cols = jnp.where(lane == lo + i, piece, cols)
            b = b - piece
        bias_cols.append(cols)
    kmax_ref[...] = jnp.zeros_like(kmax_ref)

    def body(n, c):
        kf = k_ref[_blk(n), :].astype(F32) * (scale * LOG2E)
        v_t = v_ref[_blk(n), :].astype(F32).T
        row_h0 = _row_is_head0()
        for hh in range(HEADS_PER_BLOCK):
            hm = _head_lanes(hh)
            kp_ref[hh, _blk(n), :] = jnp.where(hm, kf, bias_cols[hh]).astype(BF16)
            in_head = row_h0 if hh == 0 else jnp.logical_not(row_h0)
            vt_ref[hh, :, _blk(n)] = jnp.where(in_head, v_t, 1.0).astype(BF16)
            norm2 = jnp.sum(jnp.where(hm, kf * kf, 0.0), axis=1, keepdims=True)
            kmax_ref[hh] = jnp.maximum(kmax_ref[hh], jnp.max(norm2, axis=0, keepdims=True))
        return c

    lax.fori_loop(0, n_blocks, body, 0)


def _chain_qt(c, hh, seg_lo, seg_width, q_t, qt_ref):
    row = lax.broadcasted_iota(jnp.int32, (LANES, ATT_T), 0)
    lo = (1 - hh) * HEAD_DIM
    qsel = jnp.where((row >= seg_lo) & (row < seg_lo + seg_width), q_t, 0.0)
    qt_ref[c] = jnp.where((row >= lo) & (row < lo + N_BIAS_COLS), 1.0, qsel).astype(BF16)
    return qsel, jnp.sum(qsel * qsel, axis=0, keepdims=True)


def _stage_scores(c, hh, n, slot, kp_ref, qt_ref, sbuf_ref):
    sbuf_ref[c * STAGE_DEPTH + slot] = _dot(kp_ref[hh, _blk(jnp.maximum(n, 0)), :], qt_ref[c])


def _diag_blocks(chains, qi, qn2, slope2, kp_ref, vt_ref, kmax_ref, qt_ref, ref_ref, acc_ref, sbuf_ref):
    krow, qcol = _key_query_iotas()
    causal = krow <= qcol
    r = lax.broadcasted_iota(jnp.int32, (1, ATT_T), 1).astype(F32)
    scores = [_dot(kp_ref[hh, _blk(qi), :], qt_ref[c]) for c, hh in enumerate(chains)]
    for slot in range(STAGE_DEPTH):
        for c, hh in enumerate(chains):
            _stage_scores(c, hh, qi - 1 - slot, slot, kp_ref, qt_ref, sbuf_ref)
    gaps = []
    for c, hh in enumerate(chains):
        s = jnp.where(causal, scores[c], NEG)
        m = jnp.max(s, axis=0, keepdims=True)
        bound = jnp.sqrt(qn2[c] * kmax_ref[hh]) * BOUND_SLACK + slope2[hh] * r
        ref = jnp.maximum(m, bound - UNDERFLOW_GUARD)
        ref_ref[c] = ref
        acc_ref[c] = _dot(vt_ref[hh, :, _blk(qi)], jnp.exp2(s - ref).astype(BF16))
        gaps.append(jnp.max(bound - ref) + jnp.max(ref - m))
    return gaps


def _first_block(gap, inv_blk, qi):
    reach = jnp.minimum((gap + SKIP_BITS) * inv_blk, 1e6).astype(jnp.int32)
    return jnp.maximum(qi - 1 - reach, 0)


def _fixed_ref_step(c, hh, n, slot, off, kp_ref, vt_ref, qt_ref, ref_ref, acc_ref, sbuf_ref):
    p = jnp.exp2(sbuf_ref[c * STAGE_DEPTH + slot] - (ref_ref[c] - off)).astype(BF16)
    _stage_scores(c, hh, n - STAGE_DEPTH, slot, kp_ref, qt_ref, sbuf_ref)
    acc_ref[c] += _dot(vt_ref[hh, :, _blk(n)], p)


def _online_diag(c, hh, qi, kp_ref, vt_ref, qt_ref, m_ref, acc_ref):
    krow, qcol = _key_query_iotas()
    s = _dot(kp_ref[hh, _blk(qi), :], qt_ref[c])
    s = jnp.where(krow <= qcol, s, NEG)
    m = jnp.max(s, axis=0, keepdims=True)
    m_ref[c] = m
    acc_ref[c] = _dot(vt_ref[hh, :, _blk(qi)], jnp.exp2(s - m).astype(BF16))


def _online_step(c, hh, n, off, kp_ref, vt_ref, qt_ref, m_ref, acc_ref):
    s = _dot(kp_ref[hh, _blk(n), :], qt_ref[c])
    m_old = m_ref[c]
    m_new = jnp.maximum(m_old, jnp.max(s, axis=0, keepdims=True) + off)
    p = jnp.exp2(s - (m_new - off)).astype(BF16)
    acc_ref[c] = acc_ref[c] * jnp.exp2(m_old - m_new) + _dot(vt_ref[hh, :, _blk(n)], p)
    m_ref[c] = m_new


def _softmax_sweep(qi, chains, first, off_fn, kp_ref, vt_ref, qt_ref, ref_ref, acc_ref, sbuf_ref,
                   split_heads):
    refs = (kp_ref, vt_ref, qt_ref, ref_ref, acc_ref)
    assert STAGE_DEPTH == 2

    def run_fixed(top, left, heads):
        active = [(c, hh) for c, hh in enumerate(chains) if hh in heads]
        per_trip = max(STAGE_DEPTH, ITEMS_PER_TRIP // len(active))
        assert per_trip & (per_trip - 1) == 0
        while per_trip >= 1:
            trips = left // per_trip

            def body(i, carry, top=top, per_trip=per_trip):
                for j in range(per_trip):
                    n = top - i * per_trip - j
                    for c, hh in active:
                        _fixed_ref_step(c, hh, n, j % STAGE_DEPTH, off_fn(hh, n), *refs, sbuf_ref)
                return carry

            lax.fori_loop(0, trips, body, 0)
            top = top - trips * per_trip
            left = left - trips * per_trip
            per_trip //= 2

    far = jnp.minimum(first[0], first[1])
    if split_heads:
        near = jnp.maximum(first[0], first[1])
        together = jnp.minimum((qi - near + 1) // 2 * 2, qi - far)
        run_fixed(qi - 1, together, (0, 1))
        for hh in range(HEADS_PER_BLOCK):
            run_fixed(qi - 1 - together, jnp.maximum(qi - together - first[hh], 0), (hh,))
    else:
        run_fixed(qi - 1, qi - far, (0, 1))

    smallest = None
    for c, hh in enumerate(chains):
        r = (1 - hh) * HEAD_DIM
        d = jnp.min(acc_ref[c][r:r + 1, :])
        smallest = d if smallest is None else jnp.minimum(smallest, d)

    @pl.when(jnp.logical_not(smallest >= MIN_DENOM))
    def _redo():
        for c, hh in enumerate(chains):
            _online_diag(c, hh, qi, *refs)

        def body(n, carry):
            for c, hh in enumerate(chains):
                _online_step(c, hh, n, off_fn(hh, n), *refs)
            return carry

        lax.fori_loop(0, qi, body, 0)


def _normalized(acc, hh):
    r = (1 - hh) * HEAD_DIM
    return acc * (1.0 / acc[r:r + 1, :])


def _row_is_head0():
    return lax.broadcasted_iota(jnp.int32, (LANES, ATT_T), 0) < HEAD_DIM


def _moba_kernel(tab_ref, q_ref, k_ref, v_ref, o_ref, kp_ref, vt_ref, kmax_ref, kmean_ref, selm_ref,
                 qt_ref, ref_ref, acc_ref, sbuf_ref, *, n_blocks, scale):
    hb = pl.program_id(1)
    qi = pl.program_id(2)
    nbp = kmean_ref.shape[0]

    @pl.when(qi == 0)
    def _first_tile():
        _build_softmax_operands(tab_ref, hb, k_ref, v_ref, kp_ref, vt_ref, kmax_ref,
                                n_blocks=n_blocks, scale=scale)
        kmean_ref[...] = jnp.zeros_like(kmean_ref)

        def body(n, c):
            kmean_ref[pl.ds(n, 1), :] = jnp.mean(k_ref[_blk(n), :].astype(F32), axis=0, keepdims=True)
            return c

        lax.fori_loop(0, n_blocks, body, 0)

    q_t = q_ref[...].astype(F32).T
    blk_id = lax.broadcasted_iota(jnp.int32, (nbp, ATT_T), 0)
    blk_f = blk_id.astype(F32)
    chains = list(range(HEADS_PER_BLOCK))
    slope2 = [tab_ref[hb * HEADS_PER_BLOCK + hh] for hh in chains]
    qn2 = []
    for hh in chains:
        hm = _head_lanes(hh)
        qsel, norm2 = _chain_qt(hh, hh, hh * HEAD_DIM, HEAD_DIM, q_t, qt_ref)
        qn2.append(norm2)

        gate = _dot(jnp.where(hm, kmean_ref[...], 0.0), qsel, precision=lax.Precision.HIGHEST)
        g = jnp.where(blk_id < qi, gate, -jnp.inf)
        sel = jnp.zeros(g.shape, dtype=jnp.bool_)
        for _ in range(MOBA_TOPK):
            mx = jnp.max(g, axis=0, keepdims=True)
            pick = jnp.min(jnp.where(g == mx, blk_f, float(nbp)), axis=0, keepdims=True)
            hit = blk_f == pick
            sel = sel | (hit & (mx > -jnp.inf))
            g = jnp.where(hit, -jnp.inf, g)
        selm_ref[hh] = jnp.where(sel, 0.0, NEG)

    gaps = _diag_blocks(chains, qi, qn2, slope2, kp_ref, vt_ref, kmax_ref, qt_ref, ref_ref, acc_ref, sbuf_ref)
    first = [_first_block(gaps[hh], tab_ref[N_HEADS + hb * HEADS_PER_BLOCK + hh], qi) for hh in chains]

    def off_fn(hh, n):
        return slope2[hh] * ((n - qi) * ATT_T).astype(F32) + selm_ref[hh, pl.ds(n, 1), :]

    _softmax_sweep(qi, chains, first, off_fn, kp_ref, vt_ref, qt_ref, ref_ref, acc_ref, sbuf_ref,
                   split_heads=False)

    o = jnp.where(_row_is_head0(), _normalized(acc_ref[0], 0), _normalized(acc_ref[1], 1))
    o_ref[...] = o.T.astype(o_ref.dtype)


def _diff_kernel(tab_ref, lam_ref, g_ref, q_ref, k_ref, v_ref, o_ref, kp_ref, vt_ref, kmax_ref, qt_ref,
                 ref_ref, acc_ref, sbuf_ref, *, n_blocks, scale, lambda_init):
    hb = pl.program_id(1)
    qi = pl.program_id(2)

    @pl.when(qi == 0)
    def _first_tile():
        _build_softmax_operands(tab_ref, hb, k_ref, v_ref, kp_ref, vt_ref, kmax_ref,
                                n_blocks=n_blocks, scale=scale)

    lf = lam_ref[...].astype(F32)
    lam = (jnp.exp(jnp.sum(lf[0:1] * lf[1:2], axis=-1, keepdims=True))
           - jnp.exp(jnp.sum(lf[2:3] * lf[3:4], axis=-1, keepdims=True)) + lambda_init)

    q_t = q_ref[...].astype(F32).T
    chains = [hh for hh in range(HEADS_PER_BLOCK) for _ in range(2)]
    slope2 = [tab_ref[hb * HEADS_PER_BLOCK + hh] for hh in range(HEADS_PER_BLOCK)]
    qn2 = []
    for c, hh in enumerate(chains):
        qn2.append(_chain_qt(c, hh, hh * HEAD_DIM + (c % 2) * DIFF_QK_DIM, DIFF_QK_DIM, q_t, qt_ref)[1])
    gaps = _diag_blocks(chains, qi, qn2, slope2, kp_ref, vt_ref, kmax_ref, qt_ref, ref_ref, acc_ref, sbuf_ref)
    first = [None] * HEADS_PER_BLOCK
    for c, hh in enumerate(chains):
        fb = _first_block(gaps[c], tab_ref[N_HEADS + hb * HEADS_PER_BLOCK + hh], qi)
        first[hh] = fb if first[hh] is None else jnp.minimum(first[hh], fb)

    def off_fn(hh, n):
        return slope2[hh] * ((n - qi) * ATT_T).astype(F32)

    _softmax_sweep(qi, chains, first, off_fn, kp_ref, vt_ref, qt_ref, ref_ref, acc_ref, sbuf_ref,
                   split_heads=True)

    is_h0 = _row_is_head0()
    outs = []
    for hh in range(HEADS_PER_BLOCK):
        o = _normalized(acc_ref[2 * hh], hh) - lam * _normalized(acc_ref[2 * hh + 1], hh)
        in_head = is_h0 if hh == 0 else jnp.logical_not(is_h0)
        ms = jnp.sum(jnp.where(in_head, o * o, 0.0), axis=0, keepdims=True) * (1.0 / HEAD_DIM)
        outs.append(o * lax.rsqrt(ms + SUBLN_EPS))
    o = jnp.where(is_h0, outs[0], outs[1]).T * (g_ref[...] * (1.0 - lambda_init))
    o_ref[...] = o.astype(o_ref.dtype)


def _log_sigmoids(z):
    l = jnp.log(1.0 + jnp.exp(-jnp.abs(z)))
    return jnp.minimum(-z, 0.0) - l, jnp.minimum(z, 0.0) - l


def _suffix_sums(upper, lg):
    hi = lg.astype(BF16)
    lo = (lg - hi.astype(F32)).astype(BF16)
    return _dot(upper, hi) + _dot(upper, lo)


def _sb_block(n, diag, k_ref, vt_ref, qt_ref, carry_ref, acc_ref):
    krow, qcol = _key_query_iotas()
    before = krow < qcol
    upper = (qcol > krow).astype(BF16)
    heads = range(HEADS_PER_BLOCK)
    z = [_dot(k_ref[_blk(n), :], qt_ref[hh]) for hh in heads]
    log_sig, log_surv, tails = [], [], []
    for hh in heads:
        lneg, lpos = _log_sigmoids(z[hh])
        if diag:
            lneg = jnp.where(before, lneg, 0.0)
        log_surv.append(lneg)
        log_sig.append(lpos)
        tails.append(_suffix_sums(upper, lneg))
    worst = None
    for hh in heads:
        if diag:
            w = jnp.where(before, jnp.exp(log_sig[hh] + tails[hh]), 0.0)
            acc_ref[hh] = _dot(vt_ref[:, _blk(n)], w.astype(BF16))
            carry = jnp.sum(log_surv[hh], axis=0, keepdims=True)
        else:
            carry = carry_ref[hh]
            w = jnp.exp(log_sig[hh] + (tails[hh] + carry))
            acc_ref[hh] += _dot(vt_ref[:, _blk(n)], w.astype(BF16))
            carry = carry + jnp.sum(log_surv[hh], axis=0, keepdims=True)
        carry_ref[hh] = carry
        top = jnp.max(carry)
        worst = top if worst is None else jnp.maximum(worst, top)
    return worst


def _sb_kernel(q_ref, k_ref, v_ref, o_ref, vt_ref, qt_ref, carry_ref, acc_ref, *, n_blocks, scale):
    qi = pl.program_id(2)

    @pl.when(qi == 0)
    def _first_tile():
        def body(n, c):
            vt_ref[:, _blk(n)] = v_ref[_blk(n), :].astype(F32).T.astype(BF16)
            return c

        lax.fori_loop(0, n_blocks, body, 0)

    q_t = (q_ref[...].astype(F32) * scale).T
    row_h0 = _row_is_head0()
    for hh in range(HEADS_PER_BLOCK):
        in_head = row_h0 if hh == 0 else jnp.logical_not(row_h0)
        qt_ref[hh] = jnp.where(in_head, q_t, 0.0).astype(BF16)
    refs = (k_ref, vt_ref, qt_ref, carry_ref, acc_ref)
    worst = _sb_block(qi, True, *refs)

    def cond(c):
        j, live = c
        return (j >= 0) & live

    def body(c):
        j, _ = c
        return j - 1, _sb_block(j, False, *refs) > SB_EXIT

    lax.while_loop(cond, body, (qi - 1, worst > SB_EXIT))

    o = jnp.where(_row_is_head0(), acc_ref[0], acc_ref[1])
    o_ref[...] = o.T.astype(o_ref.dtype)


def _attn_specs(B, S, q_base, k_base, v_base):
    nq = S // ATT_T
    qspec = pl.BlockSpec((None, ATT_T, LANES), lambda b, h, i: (b, i, q_base + h))
    kspec = pl.BlockSpec((None, S, LANES), lambda b, h, i: (b, 0, k_base + h))
    vspec = pl.BlockSpec((None, S, LANES), lambda b, h, i: (b, 0, v_base + h))
    ospec = pl.BlockSpec((None, ATT_T, LANES), lambda b, h, i: (b, i, h))
    grid = (B, N_HEAD_BLOCKS, nq)
    params = pltpu.CompilerParams(
        dimension_semantics=("parallel", "parallel", "arbitrary"), vmem_limit_bytes=VMEM_LIMIT)
    return grid, qspec, kspec, vspec, ospec, params


_SMEM_SPEC = pl.BlockSpec(memory_space=pltpu.SMEM)


def _moba(qkv, tab):
    B, S, _ = qkv.shape
    grid, qspec, kspec, vspec, ospec, params = _attn_specs(B, S, 0, 3, 6)
    n_blocks = S // MOBA_BLOCK
    nbp = -(-n_blocks // SUBLANES) * SUBLANES
    return pl.pallas_call(
        functools.partial(_moba_kernel, n_blocks=n_blocks, scale=HEAD_DIM ** -0.5),
        grid=grid,
        in_specs=[_SMEM_SPEC, qspec, kspec, vspec],
        out_specs=ospec,
        out_shape=jax.ShapeDtypeStruct((B, S, MIX_W), BF16),
        scratch_shapes=[
            pltpu.VMEM((HEADS_PER_BLOCK, S, LANES), BF16),
            pltpu.VMEM((HEADS_PER_BLOCK, LANES, S), BF16),
            pltpu.VMEM((HEADS_PER_BLOCK, 1, 1), F32),
            pltpu.VMEM((nbp, LANES), F32),
            pltpu.VMEM((HEADS_PER_BLOCK, nbp, ATT_T), F32),
            pltpu.VMEM((HEADS_PER_BLOCK, LANES, ATT_T), BF16),
            pltpu.VMEM((HEADS_PER_BLOCK, 1, ATT_T), F32),
            pltpu.VMEM((HEADS_PER_BLOCK, LANES, ATT_T), F32),
            pltpu.VMEM((HEADS_PER_BLOCK * STAGE_DEPTH, ATT_T, ATT_T), F32),
        ],
        compiler_params=params,
        name="moba_attn",
    )(tab, qkv, qkv, qkv)


def _diff(qkv, tab, diff_lambda, subln_g, lambda_init):
    B, S, _ = qkv.shape
    grid, qspec, kspec, vspec, ospec, params = _attn_specs(B, S, 9, 12, 15)
    n_chains = 2 * HEADS_PER_BLOCK
    return pl.pallas_call(
        functools.partial(_diff_kernel, n_blocks=S // ATT_T, scale=DIFF_QK_DIM ** -0.5,
                          lambda_init=lambda_init),
        grid=grid,
        in_specs=[
            _SMEM_SPEC,
            pl.BlockSpec(diff_lambda.shape, lambda b, h, i: (0, 0)),
            pl.BlockSpec((1, LANES), lambda b, h, i: (0, 0)),
            qspec, kspec, vspec,
        ],
        out_specs=ospec,
        out_shape=jax.ShapeDtypeStruct((B, S, MIX_W), BF16),
        scratch_shapes=[
            pltpu.VMEM((HEADS_PER_BLOCK, S, LANES), BF16),
            pltpu.VMEM((HEADS_PER_BLOCK, LANES, S), BF16),
            pltpu.VMEM((HEADS_PER_BLOCK, 1, 1), F32),
            pltpu.VMEM((n_chains, LANES, ATT_T), BF16),
            pltpu.VMEM((n_chains, 1, ATT_T), F32),
            pltpu.VMEM((n_chains, LANES, ATT_T), F32),
            pltpu.VMEM((n_chains * STAGE_DEPTH, ATT_T, ATT_T), F32),
        ],
        compiler_params=params,
        name="diff_attn",
    )(tab, diff_lambda, subln_g, qkv, qkv, qkv)


def _sb(qkv):
    B, S, _ = qkv.shape
    grid, qspec, kspec, vspec, ospec, params = _attn_specs(B, S, 18, 21, 24)
    return pl.pallas_call(
        functools.partial(_sb_kernel, n_blocks=S // ATT_T, scale=HEAD_DIM ** -0.5),
        grid=grid,
        in_specs=[qspec, kspec, vspec],
        out_specs=ospec,
        out_shape=jax.ShapeDtypeStruct((B, S, MIX_W), BF16),
        scratch_shapes=[
            pltpu.VMEM((LANES, S), BF16),
            pltpu.VMEM((HEADS_PER_BLOCK, LANES, ATT_T), BF16),
            pltpu.VMEM((HEADS_PER_BLOCK, 1, ATT_T), F32),
            pltpu.VMEM((HEADS_PER_BLOCK, LANES, ATT_T), F32),
        ],
        compiler_params=params,
        name="sb_attn",
    )(qkv, qkv, qkv)


def _merge_ln_kernel(x_ref, om_ref, od_ref, os_ref, wg_ref, bg_ref, wbr_ref, wo_ref, lg_ref, lb_ref,
                     o_ref, *, alpha, d_model):
    x = x_ref[...]
    xb = x.astype(BF16)
    merged = None
    for b, ob_ref in enumerate((om_ref, od_ref, os_ref)):
        glogit = _dot(xb, wg_ref[:, b * d_model:(b + 1) * d_model]) + bg_ref[b:b + 1, :]
        term = jax.nn.sigmoid(glogit) * _dot(ob_ref[...], wbr_ref[b])
        merged = term if merged is None else merged + term
    y = alpha * x + _dot(merged.astype(BF16), wo_ref[...])
    o_ref[...] = _layer_norm(y, lg_ref[...], lb_ref[...])


def _merge_ln(x, o_m, o_d, o_s, w_gate, b_gate, w_br, w_out, lg, lb, l, alpha, tm=512):
    T, D = x.shape
    return pl.pallas_call(
        functools.partial(_merge_ln_kernel, alpha=alpha, d_model=D),
        grid=(T // tm,),
        in_specs=[
            pl.BlockSpec((tm, D), lambda i: (i, 0)),
            pl.BlockSpec((tm, MIX_W), lambda i: (i, 0)),
            pl.BlockSpec((tm, MIX_W), lambda i: (i, 0)),
            pl.BlockSpec((tm, MIX_W), lambda i: (i, 0)),
            pl.BlockSpec((None, D, N_BRANCH * D), lambda i: (l, 0, 0)),
            pl.BlockSpec((None, N_BRANCH, D), lambda i: (l, 0, 0)),
            pl.BlockSpec((None, N_BRANCH, MIX_W, D), lambda i: (l, 0, 0, 0)),
            pl.BlockSpec((None, D, D), lambda i: (l, 0, 0)),
            pl.BlockSpec((1, D), lambda i: (0, 0)),
            pl.BlockSpec((1, D), lambda i: (0, 0)),
        ],
        out_specs=pl.BlockSpec((tm, D), lambda i: (i, 0)),
        out_shape=jax.ShapeDtypeStruct((T, D), F32),
        compiler_params=pltpu.CompilerParams(
            dimension_semantics=("parallel",), vmem_limit_bytes=VMEM_LIMIT),
        name="merge_ln",
    )(x, o_m, o_d, o_s, w_gate, b_gate, w_br, w_out, lg, lb)


def _alibi_slopes(n):
    return (2.0 ** (-8.0 * np.arange(1, n + 1, dtype=np.float32) / n)).astype(np.float32)


def _slope_table(slopes):
    s2 = slopes.astype(np.float64) * LOG2E
    return np.concatenate([s2, 1.0 / (s2 * ATT_T)]).astype(np.float32)


def kernel(x, ln_g, ln_b, ffn_w_gate, ffn_w_up, ffn_w_down, w_in, b_gate, diff_lambda, diff_subln_g,
           w_br_moba, w_br_diff, w_br_sb, w_out):
    B, S, D = x.shape
    depth = ln_g.shape[0]
    assert S % ATT_T == 0 and ATT_T == MOBA_BLOCK
    assert w_in.shape[-1] == QKV_W + N_BRANCH * D
    alpha = (2.0 * depth) ** 0.25

    wg = ffn_w_gate.astype(BF16)
    wu = ffn_w_up.astype(BF16)
    wd = ffn_w_down.astype(BF16)
    w_qkv = w_in[:, :, :QKV_W].astype(BF16)
    w_gate = w_in[:, :, QKV_W:].astype(BF16)
    w_br = jnp.stack([w_br_moba, w_br_diff, w_br_sb], axis=1).astype(BF16)
    w_out_b = w_out.astype(BF16)
    slopes = _alibi_slopes(2 * N_HEADS)
    tab_moba = jnp.asarray(_slope_table(slopes[0::2]))
    tab_diff = jnp.asarray(_slope_table(slopes[1::2]))
    subln_g = jnp.tile(diff_subln_g.astype(F32), (1, HEADS_PER_BLOCK))

    h = x.reshape(B * S, D)
    for l in range(depth):
        lg = ln_g[l][:, None, :]
        lb = ln_b[l][:, None, :]
        h = _ffn_ln(h, wg, wu, wd, lg[0], lb[0], l, 0, alpha)
        qkv = _qkv_proj(h, w_qkv, l).reshape(B, S, QKV_W)
        lambda_init = 0.8 - 0.6 * math.exp(-0.3 * l)
        o_m = _moba(qkv, tab_moba)
        o_d = _diff(qkv, tab_diff, diff_lambda[l], subln_g[l][None, :], lambda_init)
        o_s = _sb(qkv)
        T = B * S
        h = _merge_ln(h, o_m.reshape(T, MIX_W), o_d.reshape(T, MIX_W), o_s.reshape(T, MIX_W),
                      w_gate, b_gate, w_br, w_out_b, lg[1], lb[1], l, alpha)
        h = _ffn_ln(h, wg, wu, wd, lg[2], lb[2], l, 1, alpha)
    return h.reshape(B, S, D)
```

```python
import functools
import math

import numpy as np
import jax
import jax.numpy as jnp
from jax import lax
from jax.experimental import pallas as pl
from jax.experimental.pallas import tpu as pltpu

F32 = jnp.float32
BF16 = jnp.bfloat16

HEAD_DIM = 64
N_HEADS = 6
DIFF_QK_DIM = HEAD_DIM // 2
MOBA_BLOCK = 256
MOBA_TOPK = 3
N_BRANCH = 3
LN_EPS = 1e-5
SUBLN_EPS = 1e-5

LANES = 128
SUBLANES = 8
DENOM_ROWS = 16
V_ROWS = HEAD_DIM + DENOM_ROWS
HEADS_PER_BLOCK = LANES // HEAD_DIM
N_HEAD_BLOCKS = N_HEADS // HEADS_PER_BLOCK
MIX_W = N_HEADS * HEAD_DIM
QKV_W = 9 * MIX_W
ATT_T = 256
NEG = -1e30
SB_EXIT = -110.0
LOG2E = math.log2(math.e)
N_BIAS_COLS = 3
BOUND_SLACK = 1.01
UNDERFLOW_GUARD = 100.0
SKIP_BITS = 150.0
MIN_DENOM = 2.0 ** -60
ITEMS_PER_TRIP = 16
STAGE_DEPTH = 2
VMEM_LIMIT = 56 * 1024 * 1024


def _dot(a, b, precision=None):
    return jnp.dot(a, b, precision=precision, preferred_element_type=F32)


def _layer_norm(y, g, b):
    mu = jnp.mean(y, axis=-1, keepdims=True)
    yc = y - mu
    var = jnp.mean(yc * yc, axis=-1, keepdims=True)
    return yc * lax.rsqrt(var + LN_EPS) * g + b


def _ffn_ln_kernel(x_ref, wg_ref, wu_ref, wd_ref, lg_ref, lb_ref, o_ref, hid_ref, *, alpha, tf):
    x = x_ref[...]
    xb = x.astype(BF16)
    for c in range(hid_ref.shape[1] // tf):
        cols = slice(c * tf, (c + 1) * tf)
        gate = _dot(xb, wg_ref[:, cols])
        up = _dot(xb, wu_ref[:, cols])
        hid_ref[:, cols] = (gate * jax.nn.sigmoid(gate) * up).astype(BF16)
    y = alpha * x + 0.5 * _dot(hid_ref[...], wd_ref[...])
    o_ref[...] = _layer_norm(y, lg_ref[...], lb_ref[...])


def _ffn_ln(x, wg, wu, wd, lg, lb, l, j, alpha, tm=512, tf=256):
    T, D = x.shape
    FF = wg.shape[-1]
    resident = pl.Buffered(1)
    return pl.pallas_call(
        functools.partial(_ffn_ln_kernel, alpha=alpha, tf=tf),
        grid=(T // tm,),
        in_specs=[
            pl.BlockSpec((tm, D), lambda i: (i, 0)),
            pl.BlockSpec((None, None, D, FF), lambda i: (l, j, 0, 0), pipeline_mode=resident),
            pl.BlockSpec((None, None, D, FF), lambda i: (l, j, 0, 0), pipeline_mode=resident),
            pl.BlockSpec((None, None, FF, D), lambda i: (l, j, 0, 0), pipeline_mode=resident),
            pl.BlockSpec((1, D), lambda i: (0, 0)),
            pl.BlockSpec((1, D), lambda i: (0, 0)),
        ],
        out_specs=pl.BlockSpec((tm, D), lambda i: (i, 0)),
        out_shape=jax.ShapeDtypeStruct((T, D), F32),
        scratch_shapes=[pltpu.VMEM((tm, FF), BF16)],
        compiler_params=pltpu.CompilerParams(
            dimension_semantics=("parallel",), vmem_limit_bytes=VMEM_LIMIT),
        name="ffn_ln",
    )(x, wg, wu, wd, lg, lb)


def _proj_kernel(x_ref, w_ref, o_ref, *, tn):
    xb = x_ref[...].astype(BF16)
    for c in range(o_ref.shape[1] // tn):
        cols = slice(c * tn, (c + 1) * tn)
        o_ref[:, cols] = _dot(xb, w_ref[:, cols]).astype(o_ref.dtype)


def _qkv_proj(x, w_in, l, tm=512, tn=1152):
    T, D = x.shape
    return pl.pallas_call(
        functools.partial(_proj_kernel, tn=tn),
        grid=(T // tm,),
        in_specs=[
            pl.BlockSpec((tm, D), lambda i: (i, 0)),
            pl.BlockSpec((None, D, QKV_W), lambda i: (l, 0, 0), pipeline_mode=pl.Buffered(1)),
        ],
        out_specs=pl.BlockSpec((tm, QKV_W), lambda i: (i, 0)),
        out_shape=jax.ShapeDtypeStruct((T, QKV_W), BF16),
        compiler_params=pltpu.CompilerParams(
            dimension_semantics=("parallel",), vmem_limit_bytes=VMEM_LIMIT),
        name="qkv_proj",
    )(x, w_in)


def _lane_iota():
    return lax.broadcasted_iota(jnp.int32, (1, LANES), 1)


def _head_lanes(hh):
    lane = _lane_iota()
    return (lane >= hh * HEAD_DIM) & (lane < (hh + 1) * HEAD_DIM)


def _bias_lanes(hh):
    lane = _lane_iota()
    lo = (1 - hh) * HEAD_DIM
    return (lane >= lo) & (lane < lo + N_BIAS_COLS)


def _blk(j):
    return pl.ds(pl.multiple_of(j * ATT_T, ATT_T), ATT_T)


def _key_query_iotas():
    krow = lax.broadcasted_iota(jnp.int32, (ATT_T, ATT_T), 0)
    qcol = lax.broadcasted_iota(jnp.int32, (ATT_T, ATT_T), 1)
    return krow, qcol


def _build_softmax_operands(tab_ref, hb, k_ref, v_ref, kp_ref, vt_ref, kmax_ref, *, n_blocks, scale):
    lane = _lane_iota()
    pos = lax.broadcasted_iota(jnp.int32, (ATT_T, LANES), 0).astype(F32)
    bias_cols = []
    for hh in range(HEADS_PER_BLOCK):
        b = pos * tab_ref[hb * HEADS_PER_BLOCK + hh]
        lo = (1 - hh) * HEAD_DIM
        cols = jnp.zeros_like(b)
        for i in range(N_BIAS_COLS):
            piece = b.astype(BF16).astype(F32)
            cols = jnp.where(lane == lo + i, piece, cols)
            b = b - piece
        bias_cols.append(cols)
    kmax_ref[...] = jnp.zeros_like(kmax_ref)

    def body(n, c):
        kf = k_ref[_blk(n), :].astype(F32) * (scale * LOG2E)
        v_t = v_ref[_blk(n), :].astype(F32).T.astype(BF16)
        for hh in range(HEADS_PER_BLOCK):
            hm = _head_lanes(hh)
            kp_ref[hh, _blk(n), :] = jnp.where(hm, kf, bias_cols[hh]).astype(BF16)
            vt_ref[hh, 0:HEAD_DIM, _blk(n)] = v_t[hh * HEAD_DIM:(hh + 1) * HEAD_DIM, :]
            vt_ref[hh, HEAD_DIM:V_ROWS, _blk(n)] = jnp.ones((DENOM_ROWS, ATT_T), BF16)
            norm2 = jnp.sum(jnp.where(hm, kf * kf, 0.0), axis=1, keepdims=True)
            kmax_ref[hh] = jnp.maximum(kmax_ref[hh], jnp.max(norm2, axis=0, keepdims=True))
        return c

    lax.fori_loop(0, n_blocks, body, 0)


def _chain_qt(c, hh, seg_lo, seg_width, q_t, qt_ref):
    row = lax.broadcasted_iota(jnp.int32, (LANES, ATT_T), 0)
    lo = (1 - hh) * HEAD_DIM
    qsel = jnp.where((row >= seg_lo) & (row < seg_lo + seg_width), q_t, 0.0)
    qt_ref[c] = jnp.where((row >= lo) & (row < lo + N_BIAS_COLS), 1.0, qsel).astype(BF16)
    return qsel, jnp.sum(qsel * qsel, axis=0, keepdims=True)


def _stage_scores(c, hh, n, slot, kp_ref, qt_ref, sbuf_ref):
    sbuf_ref[c * STAGE_DEPTH + slot] = _dot(kp_ref[hh, _blk(jnp.maximum(n, 0)), :], qt_ref[c])


def _diag_blocks(chains, qi, qn2, slope2, kp_ref, vt_ref, kmax_ref, qt_ref, ref_ref, acc_ref, sbuf_ref):
    krow, qcol = _key_query_iotas()
    causal = krow <= qcol
    r = lax.broadcasted_iota(jnp.int32, (1, ATT_T), 1).astype(F32)
    scores = [_dot(kp_ref[hh, _blk(qi), :], qt_ref[c]) for c, hh in enumerate(chains)]
    for slot in range(STAGE_DEPTH):
        for c, hh in enumerate(chains):
            _stage_scores(c, hh, qi - 1 - slot, slot, kp_ref, qt_ref, sbuf_ref)
    gaps = []
    for c, hh in enumerate(chains):
        s = jnp.where(causal, scores[c], NEG)
        m = jnp.max(s, axis=0, keepdims=True)
        bound = jnp.sqrt(qn2[c] * kmax_ref[hh]) * BOUND_SLACK + slope2[hh] * r
        ref = jnp.maximum(m, bound - UNDERFLOW_GUARD)
        ref_ref[c] = ref
        acc_ref[c] = _dot(vt_ref[hh, :, _blk(qi)], jnp.exp2(s - ref).astype(BF16))
        gaps.append(jnp.max(bound - ref) + jnp.max(ref - m))
    return gaps


def _first_block(gap, inv_blk, qi):
    reach = jnp.minimum((gap + SKIP_BITS) * inv_blk, 1e6).astype(jnp.int32)
    return jnp.maximum(qi - 1 - reach, 0)


def _fixed_ref_step(c, hh, n, slot, off, kp_ref, vt_ref, qt_ref, ref_ref, acc_ref, sbuf_ref):
    p = jnp.exp2(sbuf_ref[c * STAGE_DEPTH + slot] - (ref_ref[c] - off)).astype(BF16)
    _stage_scores(c, hh, n - STAGE_DEPTH, slot, kp_ref, qt_ref, sbuf_ref)
    acc_ref[c] += _dot(vt_ref[hh, :, _blk(n)], p)


def _online_diag(c, hh, qi, kp_ref, vt_ref, qt_ref, m_ref, acc_ref):
    krow, qcol = _key_query_iotas()
    s = _dot(kp_ref[hh, _blk(qi), :], qt_ref[c])
    s = jnp.where(krow <= qcol, s, NEG)
    m = jnp.max(s, axis=0, keepdims=True)
    m_ref[c] = m
    acc_ref[c] = _dot(vt_ref[hh, :, _blk(qi)], jnp.exp2(s - m).astype(BF16))


def _online_step(c, hh, n, off, kp_ref, vt_ref, qt_ref, m_ref, acc_ref):
    s = _dot(kp_ref[hh, _blk(n), :], qt_ref[c])
    m_old = m_ref[c]
    m_new = jnp.maximum(m_old, jnp.max(s, axis=0, keepdims=True) + off)
    p = jnp.exp2(s - (m_new - off)).astype(BF16)
    acc_ref[c] = acc_ref[c] * jnp.exp2(m_old - m_new) + _dot(vt_ref[hh, :, _blk(n)], p)
    m_ref[c] = m_new


def _softmax_sweep(qi, chains, first, off_fn, kp_ref, vt_ref, qt_ref, ref_ref, acc_ref, sbuf_ref,
                   split_heads):
    refs = (kp_ref, vt_ref, qt_ref, ref_ref, acc_ref)
    assert STAGE_DEPTH == 2

    def run_fixed(top, left, heads):
        active = [(c, hh) for c, hh in enumerate(chains) if hh in heads]
        per_trip = max(STAGE_DEPTH, ITEMS_PER_TRIP // len(active))
        assert per_trip & (per_trip - 1) == 0
        while per_trip >= 1:
            trips = left // per_trip

            def body(i, carry, top=top, per_trip=per_trip):
                for j in range(per_trip):
                    n = top - i * per_trip - j
                    for c, hh in active:
                        _fixed_ref_step(c, hh, n, j % STAGE_DEPTH, off_fn(hh, n), *refs, sbuf_ref)
                return carry

            lax.fori_loop(0, trips, body, 0)
            top = top - trips * per_trip
            left = left - trips * per_trip
            per_trip //= 2

    far = jnp.minimum(first[0], first[1])
    if split_heads:
        near = jnp.maximum(first[0], first[1])
        together = jnp.minimum((qi - near + 1) // 2 * 2, qi - far)
        run_fixed(qi - 1, together, (0, 1))
        for hh in range(HEADS_PER_BLOCK):
            run_fixed(qi - 1 - together, jnp.maximum(qi - together - first[hh], 0), (hh,))
    else:
        run_fixed(qi - 1, qi - far, (0, 1))

    smallest = None
    for c in range(len(chains)):
        d = jnp.min(acc_ref[c][HEAD_DIM:HEAD_DIM + 1, :])
        smallest = d if smallest is None else jnp.minimum(smallest, d)

    @pl.when(jnp.logical_not(smallest >= MIN_DENOM))
    def _redo():
        for c, hh in enumerate(chains):
            _online_diag(c, hh, qi, *refs)

        def body(n, carry):
            for c, hh in enumerate(chains):
                _online_step(c, hh, n, off_fn(hh, n), *refs)
            return carry

        lax.fori_loop(0, qi, body, 0)


def _normalized(acc):
    return acc[0:HEAD_DIM, :] * (1.0 / acc[HEAD_DIM:HEAD_DIM + 1, :])


def _row_is_head0():
    return lax.broadcasted_iota(jnp.int32, (LANES, ATT_T), 0) < HEAD_DIM


def _moba_kernel(tab_ref, q_ref, k_ref, v_ref, o_ref, kp_ref, vt_ref, kmax_ref, kmean_ref, selm_ref,
                 qt_ref, ref_ref, acc_ref, sbuf_ref, *, n_blocks, scale):
    hb = pl.program_id(1)
    qi = pl.program_id(2)
    nbp = kmean_ref.shape[0]

    @pl.when(qi == 0)
    def _first_tile():
        _build_softmax_operands(tab_ref, hb, k_ref, v_ref, kp_ref, vt_ref, kmax_ref,
                                n_blocks=n_blocks, scale=scale)
        kmean_ref[...] = jnp.zeros_like(kmean_ref)

        def body(n, c):
            kmean_ref[pl.ds(n, 1), :] = jnp.mean(k_ref[_blk(n), :].astype(F32), axis=0, keepdims=True)
            return c

        lax.fori_loop(0, n_blocks, body, 0)

    q_t = q_ref[...].astype(F32).T
    blk_id = lax.broadcasted_iota(jnp.int32, (nbp, ATT_T), 0)
    blk_f = blk_id.astype(F32)
    chains = list(range(HEADS_PER_BLOCK))
    slope2 = [tab_ref[hb * HEADS_PER_BLOCK + hh] for hh in chains]
    qn2 = []
    for hh in chains:
        hm = _head_lanes(hh)
        qsel, norm2 = _chain_qt(hh, hh, hh * HEAD_DIM, HEAD_DIM, q_t, qt_ref)
        qn2.append(norm2)

        gate = _dot(jnp.where(hm, kmean_ref[...], 0.0), qsel, precision=lax.Precision.HIGHEST)
        g = jnp.where(blk_id < qi, gate, -jnp.inf)
        sel = jnp.zeros(g.shape, dtype=jnp.bool_)
        for _ in range(MOBA_TOPK):
            mx = jnp.max(g, axis=0, keepdims=True)
            pick = jnp.min(jnp.where(g == mx, blk_f, float(nbp)), axis=0, keepdims=True)
            hit = blk_f == pick
            sel = sel | (hit & (mx > -jnp.inf))
            g = jnp.where(hit, -jnp.inf, g)
        selm_ref[hh] = jnp.where(sel, 0.0, NEG)

    gaps = _diag_blocks(chains, qi, qn2, slope2, kp_ref, vt_ref, kmax_ref, qt_ref, ref_ref, acc_ref, sbuf_ref)
    first = [_first_block(gaps[hh], tab_ref[N_HEADS + hb * HEADS_PER_BLOCK + hh], qi) for hh in chains]

    def off_fn(hh, n):
        return slope2[hh] * ((n - qi) * ATT_T).astype(F32) + selm_ref[hh, pl.ds(n, 1), :]

    _softmax_sweep(qi, chains, first, off_fn, kp_ref, vt_ref, qt_ref, ref_ref, acc_ref, sbuf_ref,
                   split_heads=False)

    o = jnp.concatenate([_normalized(acc_ref[0]), _normalized(acc_ref[1])], axis=0)
    o_ref[...] = o.T.astype(o_ref.dtype)


def _diff_kernel(tab_ref, lam_ref, g_ref, q_ref, k_ref, v_ref, o_ref, kp_ref, vt_ref, kmax_ref, qt_ref,
                 ref_ref, acc_ref, sbuf_ref, *, n_blocks, scale, lambda_init):
    hb = pl.program_id(1)
    qi = pl.program_id(2)

    @pl.when(qi == 0)
    def _first_tile():
        _build_softmax_operands(tab_ref, hb, k_ref, v_ref, kp_ref, vt_ref, kmax_ref,
                                n_blocks=n_blocks, scale=scale)

    lf = lam_ref[...].astype(F32)
    lam = (jnp.exp(jnp.sum(lf[0:1] * lf[1:2], axis=-1, keepdims=True))
           - jnp.exp(jnp.sum(lf[2:3] * lf[3:4], axis=-1, keepdims=True)) + lambda_init)

    q_t = q_ref[...].astype(F32).T
    chains = [hh for hh in range(HEADS_PER_BLOCK) for _ in range(2)]
    slope2 = [tab_ref[hb * HEADS_PER_BLOCK + hh] for hh in range(HEADS_PER_BLOCK)]
    qn2 = []
    for c, hh in enumerate(chains):
        qn2.append(_chain_qt(c, hh, hh * HEAD_DIM + (c % 2) * DIFF_QK_DIM, DIFF_QK_DIM, q_t, qt_ref)[1])
    gaps = _diag_blocks(chains, qi, qn2, slope2, kp_ref, vt_ref, kmax_ref, qt_ref, ref_ref, acc_ref, sbuf_ref)
    first = [None] * HEADS_PER_BLOCK
    for c, hh in enumerate(chains):
        fb = _first_block(gaps[c], tab_ref[N_HEADS + hb * HEADS_PER_BLOCK + hh], qi)
        first[hh] = fb if first[hh] is None else jnp.minimum(first[hh], fb)

    def off_fn(hh, n):
        return slope2[hh] * ((n - qi) * ATT_T).astype(F32)

    _softmax_sweep(qi, chains, first, off_fn, kp_ref, vt_ref, qt_ref, ref_ref, acc_ref, sbuf_ref,
                   split_heads=True)

    outs = []
    for hh in range(HEADS_PER_BLOCK):
        o = _normalized(acc_ref[2 * hh]) - lam * _normalized(acc_ref[2 * hh + 1])
        ms = jnp.mean(o * o, axis=0, keepdims=True)
        outs.append(o * lax.rsqrt(ms + SUBLN_EPS))
    o = jnp.concatenate(outs, axis=0).T * (g_ref[...] * (1.0 - lambda_init))
    o_ref[...] = o.astype(o_ref.dtype)


def _log_sigmoids(z):
    l = jnp.log(1.0 + jnp.exp(-jnp.abs(z)))
    return jnp.minimum(-z, 0.0) - l, jnp.minimum(z, 0.0) - l


def _suffix_sums(upper, lg):
    hi = lg.astype(BF16)
    lo = (lg - hi.astype(F32)).astype(BF16)
    return _dot(upper, hi) + _dot(upper, lo)


def _sb_block(n, diag, k_ref, vt_ref, qt_ref, carry_ref, acc_ref):
    krow, qcol = _key_query_iotas()
    before = krow < qcol
    upper = (qcol > krow).astype(BF16)
    heads = range(HEADS_PER_BLOCK)
    z = [_dot(k_ref[_blk(n), :], qt_ref[hh]) for hh in heads]
    log_sig, log_surv, tails = [], [], []
    for hh in heads:
        lneg, lpos = _log_sigmoids(z[hh])
        if diag:
            lneg = jnp.where(before, lneg, 0.0)
        log_surv.append(lneg)
        log_sig.append(lpos)
        tails.append(_suffix_sums(upper, lneg))
    worst = None
    for hh in heads:
        if diag:
            w = jnp.where(before, jnp.exp(log_sig[hh] + tails[hh]), 0.0)
            acc_ref[hh] = _dot(vt_ref[:, _blk(n)], w.astype(BF16))
            carry = jnp.sum(log_surv[hh], axis=0, keepdims=True)
        else:
            carry = carry_ref[hh]
            w = jnp.exp(log_sig[hh] + (tails[hh] + carry))
            acc_ref[hh] += _dot(vt_ref[:, _blk(n)], w.astype(BF16))
            carry = carry + jnp.sum(log_surv[hh], axis=0, keepdims=True)
        carry_ref[hh] = carry
        top = jnp.max(carry)
        worst = top if worst is None else jnp.maximum(worst, top)
    return worst


def _sb_kernel(q_ref, k_ref, v_ref, o_ref, vt_ref, qt_ref, carry_ref, acc_ref, *, n_blocks, scale):
    qi = pl.program_id(2)

    @pl.when(qi == 0)
    def _first_tile():
        def body(n, c):
            vt_ref[:, _blk(n)] = v_ref[_blk(n), :].astype(F32).T.astype(BF16)
            return c

        lax.fori_loop(0, n_blocks, body, 0)

    q_t = (q_ref[...].astype(F32) * scale).T
    row_h0 = _row_is_head0()
    for hh in range(HEADS_PER_BLOCK):
        in_head = row_h0 if hh == 0 else jnp.logical_not(row_h0)
        qt_ref[hh] = jnp.where(in_head, q_t, 0.0).astype(BF16)
    refs = (k_ref, vt_ref, qt_ref, carry_ref, acc_ref)
    worst = _sb_block(qi, True, *refs)

    def cond(c):
        j, live = c
        return (j >= 0) & live

    def body(c):
        j, _ = c
        return j - 1, _sb_block(j, False, *refs) > SB_EXIT

    lax.while_loop(cond, body, (qi - 1, worst > SB_EXIT))

    o = jnp.where(_row_is_head0(), acc_ref[0], acc_ref[1])
    o_ref[...] = o.T.astype(o_ref.dtype)


def _attn_specs(B, S, q_base, k_base, v_base):
    nq = S // ATT_T
    qspec = pl.BlockSpec((None, ATT_T, LANES), lambda b, h, i: (b, i, q_base + h))
    kspec = pl.BlockSpec((None, S, LANES), lambda b, h, i: (b, 0, k_base + h))
    vspec = pl.BlockSpec((None, S, LANES), lambda b, h, i: (b, 0, v_base + h))
    ospec = pl.BlockSpec((None, ATT_T, LANES), lambda b, h, i: (b, i, h))
    grid = (B, N_HEAD_BLOCKS, nq)
    params = pltpu.CompilerParams(
        dimension_semantics=("parallel", "parallel", "arbitrary"), vmem_limit_bytes=VMEM_LIMIT)
    return grid, qspec, kspec, vspec, ospec, params


_SMEM_SPEC = pl.BlockSpec(memory_space=pltpu.SMEM)


def _moba(qkv, tab):
    B, S, _ = qkv.shape
    grid, qspec, kspec, vspec, ospec, params = _attn_specs(B, S, 0, 3, 6)
    n_blocks = S // MOBA_BLOCK
    nbp = -(-n_blocks // SUBLANES) * SUBLANES
    return pl.pallas_call(
        functools.partial(_moba_kernel, n_blocks=n_blocks, scale=HEAD_DIM ** -0.5),
        grid=grid,
        in_specs=[_SMEM_SPEC, qspec, kspec, vspec],
        out_specs=ospec,
        out_shape=jax.ShapeDtypeStruct((B, S, MIX_W), BF16),
        scratch_shapes=[
            pltpu.VMEM((HEADS_PER_BLOCK, S, LANES), BF16),
            pltpu.VMEM((HEADS_PER_BLOCK, V_ROWS, S), BF16),
            pltpu.VMEM((HEADS_PER_BLOCK, 1, 1), F32),
            pltpu.VMEM((nbp, LANES), F32),
            pltpu.VMEM((HEADS_PER_BLOCK, nbp, ATT_T), F32),
            pltpu.VMEM((HEADS_PER_BLOCK, LANES, ATT_T), BF16),
            pltpu.VMEM((HEADS_PER_BLOCK, 1, ATT_T), F32),
            pltpu.VMEM((HEADS_PER_BLOCK, V_ROWS, ATT_T), F32),
            pltpu.VMEM((HEADS_PER_BLOCK * STAGE_DEPTH, ATT_T, ATT_T), F32),
        ],
        compiler_params=params,
        name="moba_attn",
    )(tab, qkv, qkv, qkv)


def _diff(qkv, tab, diff_lambda, subln_g, lambda_init):
    B, S, _ = qkv.shape
    grid, qspec, kspec, vspec, ospec, params = _attn_specs(B, S, 9, 12, 15)
    n_chains = 2 * HEADS_PER_BLOCK
    return pl.pallas_call(
        functools.partial(_diff_kernel, n_blocks=S // ATT_T, scale=DIFF_QK_DIM ** -0.5,
                          lambda_init=lambda_init),
        grid=grid,
        in_specs=[
            _SMEM_SPEC,
            pl.BlockSpec(diff_lambda.shape, lambda b, h, i: (0, 0)),
            pl.BlockSpec((1, LANES), lambda b, h, i: (0, 0)),
            qspec, kspec, vspec,
        ],
        out_specs=ospec,
        out_shape=jax.ShapeDtypeStruct((B, S, MIX_W), BF16),
        scratch_shapes=[
            pltpu.VMEM((HEADS_PER_BLOCK, S, LANES), BF16),
            pltpu.VMEM((HEADS_PER_BLOCK, V_ROWS, S), BF16),
            pltpu.VMEM((HEADS_PER_BLOCK, 1, 1), F32),
            pltpu.VMEM((n_chains, LANES, ATT_T), BF16),
            pltpu.VMEM((n_chains, 1, ATT_T), F32),
            pltpu.VMEM((n_chains, V_ROWS, ATT_T), F32),
            pltpu.VMEM((n_chains * STAGE_DEPTH, ATT_T, ATT_T), F32),
        ],
        compiler_params=params,
        name="diff_attn",
    )(tab, diff_lambda, subln_g, qkv, qkv, qkv)


def _sb(qkv):
    B, S, _ = qkv.shape
    grid, qspec, kspec, vspec, ospec, params = _attn_specs(B, S, 18, 21, 24)
    return pl.pallas_call(
        functools.partial(_sb_kernel, n_blocks=S // ATT_T, scale=HEAD_DIM ** -0.5),
        grid=grid,
        in_specs=[qspec, kspec, vspec],
        out_specs=ospec,
        out_shape=jax.ShapeDtypeStruct((B, S, MIX_W), BF16),
        scratch_shapes=[
            pltpu.VMEM((LANES, S), BF16),
            pltpu.VMEM((HEADS_PER_BLOCK, LANES, ATT_T), BF16),
            pltpu.VMEM((HEADS_PER_BLOCK, 1, ATT_T), F32),
            pltpu.VMEM((HEADS_PER_BLOCK, LANES, ATT_T), F32),
        ],
        compiler_params=params,
        name="sb_attn",
    )(qkv, qkv, qkv)


def _merge_ln_kernel(x_ref, om_ref, od_ref, os_ref, wg_ref, bg_ref, wbr_ref, wo_ref, lg_ref, lb_ref,
                     o_ref, *, alpha, d_model):
    x = x_ref[...]
    xb = x.astype(BF16)
    merged = None
    for b, ob_ref in enumerate((om_ref, od_ref, os_ref)):
        glogit = _dot(xb, wg_ref[:, b * d_model:(b + 1) * d_model]) + bg_ref[b:b + 1, :]
        term = jax.nn.sigmoid(glogit) * _dot(ob_ref[...], wbr_ref[b])
        merged = term if merged is None else merged + term
    y = alpha * x + _dot(merged.astype(BF16), wo_ref[...])
    o_ref[...] = _layer_norm(y, lg_ref[...], lb_ref[...])


def _merge_ln(x, o_m, o_d, o_s, w_gate, b_gate, w_br, w_out, lg, lb, l, alpha, tm=512):
    T, D = x.shape
    return pl.pallas_call(
        functools.partial(_merge_ln_kernel, alpha=alpha, d_model=D),
        grid=(T // tm,),
        in_specs=[
            pl.BlockSpec((tm, D), lambda i: (i, 0)),
            pl.BlockSpec((tm, MIX_W), lambda i: (i, 0)),
            pl.BlockSpec((tm, MIX_W), lambda i: (i, 0)),
            pl.BlockSpec((tm, MIX_W), lambda i: (i, 0)),
            pl.BlockSpec((None, D, N_BRANCH * D), lambda i: (l, 0, 0)),
            pl.BlockSpec((None, N_BRANCH, D), lambda i: (l, 0, 0)),
            pl.BlockSpec((None, N_BRANCH, MIX_W, D), lambda i: (l, 0, 0, 0)),
            pl.BlockSpec((None, D, D), lambda i: (l, 0, 0)),
            pl.BlockSpec((1, D), lambda i: (0, 0)),
            pl.BlockSpec((1, D), lambda i: (0, 0)),
        ],
        out_specs=pl.BlockSpec((tm, D), lambda i: (i, 0)),
        out_shape=jax.ShapeDtypeStruct((T, D), F32),
        compiler_params=pltpu.CompilerParams(
            dimension_semantics=("parallel",), vmem_limit_bytes=VMEM_LIMIT),
        name="merge_ln",
    )(x, o_m, o_d, o_s, w_gate, b_gate, w_br, w_out, lg, lb)


def _alibi_slopes(n):
    return (2.0 ** (-8.0 * np.arange(1, n + 1, dtype=np.float32) / n)).astype(np.float32)


def _slope_table(slopes):
    s2 = slopes.astype(np.float64) * LOG2E
    return np.concatenate([s2, 1.0 / (s2 * ATT_T)]).astype(np.float32)


def kernel(x, ln_g, ln_b, ffn_w_gate, ffn_w_up, ffn_w_down, w_in, b_gate, diff_lambda, diff_subln_g,
           w_br_moba, w_br_diff, w_br_sb, w_out):
    B, S, D = x.shape
    depth = ln_g.shape[0]
    assert S % ATT_T == 0 and ATT_T == MOBA_BLOCK
    assert w_in.shape[-1] == QKV_W + N_BRANCH * D
    alpha = (2.0 * depth) ** 0.25

    wg = ffn_w_gate.astype(BF16)
    wu = ffn_w_up.astype(BF16)
    wd = ffn_w_down.astype(BF16)
    w_qkv = w_in[:, :, :QKV_W].astype(BF16)
    w_gate = w_in[:, :, QKV_W:].astype(BF16)
    w_br = jnp.stack([w_br_moba, w_br_diff, w_br_sb], axis=1).astype(BF16)
    w_out_b = w_out.astype(BF16)
    slopes = _alibi_slopes(2 * N_HEADS)
    tab_moba = jnp.asarray(_slope_table(slopes[0::2]))
    tab_diff = jnp.asarray(_slope_table(slopes[1::2]))
    subln_g = jnp.tile(diff_subln_g.astype(F32), (1, HEADS_PER_BLOCK))

    h = x.reshape(B * S, D)
    for l in range(depth):
        lg = ln_g[l][:, None, :]
        lb = ln_b[l][:, None, :]
        h = _ffn_ln(h, wg, wu, wd, lg[0], lb[0], l, 0, alpha)
        qkv = _qkv_proj(h, w_qkv, l).reshape(B, S, QKV_W)
        lambda_init = 0.8 - 0.6 * math.exp(-0.3 * l)
        o_m = _moba(qkv, tab_moba)
        o_d = _diff(qkv, tab_diff, diff_lambda[l], subln_g[l][None, :], lambda_init)
        o_s = _sb(qkv)
        T = B * S
        h = _merge_ln(h, o_m.reshape(T, MIX_W), o_d.reshape(T, MIX_W), o_s.reshape(T, MIX_W),
                      w_gate, b_gate, w_br, w_out_b, lg[1], lb[1], l, alpha)
        h = _ffn_ln(h, wg, wu, wd, lg[2], lb[2], l, 1, alpha)
    return h.reshape(B, S, D)
```

```python
import functools
import math

import numpy as np
import jax
import jax.numpy as jnp
from jax import lax
from jax.experimental import pallas as pl
from jax.experimental.pallas import tpu as pltpu

F32 = jnp.float32
BF16 = jnp.bfloat16

HEAD_DIM = 64
N_HEADS = 6
DIFF_QK_DIM = HEAD_DIM // 2
MOBA_BLOCK = 256
MOBA_TOPK = 3
N_BRANCH = 3
LN_EPS = 1e-5
SUBLN_EPS = 1e-5

LANES = 128
SUBLANES = 8
DENOM_ROWS = 16
V_ROWS = HEAD_DIM + DENOM_ROWS
HEADS_PER_BLOCK = LANES // HEAD_DIM
N_HEAD_BLOCKS = N_HEADS // HEADS_PER_BLOCK
MIX_W = N_HEADS * HEAD_DIM
QKV_W = 9 * MIX_W
ATT_T = 256
NEG = -1e30
SB_EXIT = -110.0
LOG2E = math.log2(math.e)
N_BIAS_COLS = 3
BOUND_SLACK = 1.01
UNDERFLOW_GUARD = 100.0
SKIP_BITS = 150.0
MIN_DENOM = 2.0 ** -60
ITEMS_PER_TRIP = 16
STAGE_DEPTH = 2
VMEM_LIMIT = 56 * 1024 * 1024


def _dot(a, b, precision=None):
    return jnp.dot(a, b, precision=precision, preferred_element_type=F32)


def _layer_norm(y, g, b):
    mu = jnp.mean(y, axis=-1, keepdims=True)
    yc = y - mu
    var = jnp.mean(yc * yc, axis=-1, keepdims=True)
    return yc * lax.rsqrt(var + LN_EPS) * g + b


def _ffn_ln_kernel(x_ref, wg_ref, wu_ref, wd_ref, lg_ref, lb_ref, o_ref, hid_ref, *, alpha, tf):
    x = x_ref[...]
    xb = x.astype(BF16)
    for c in range(hid_ref.shape[1] // tf):
        cols = slice(c * tf, (c + 1) * tf)
        gate = _dot(xb, wg_ref[:, cols])
        up = _dot(xb, wu_ref[:, cols])
        hid_ref[:, cols] = (gate * jax.nn.sigmoid(gate) * up).astype(BF16)
    y = alpha * x + 0.5 * _dot(hid_ref[...], wd_ref[...])
    o_ref[...] = _layer_norm(y, lg_ref[...], lb_ref[...])


def _ffn_ln(x, wg, wu, wd, lg, lb, l, j, alpha, tm=512, tf=256):
    T, D = x.shape
    FF = wg.shape[-1]
    resident = pl.Buffered(1)
    return pl.pallas_call(
        functools.partial(_ffn_ln_kernel, alpha=alpha, tf=tf),
        grid=(T // tm,),
        in_specs=[
            pl.BlockSpec((tm, D), lambda i: (i, 0)),
            pl.BlockSpec((None, None, D, FF), lambda i: (l, j, 0, 0), pipeline_mode=resident),
            pl.BlockSpec((None, None, D, FF), lambda i: (l, j, 0, 0), pipeline_mode=resident),
            pl.BlockSpec((None, None, FF, D), lambda i: (l, j, 0, 0), pipeline_mode=resident),
            pl.BlockSpec((1, D), lambda i: (0, 0)),
            pl.BlockSpec((1, D), lambda i: (0, 0)),
        ],
        out_specs=pl.BlockSpec((tm, D), lambda i: (i, 0)),
        out_shape=jax.ShapeDtypeStruct((T, D), F32),
        scratch_shapes=[pltpu.VMEM((tm, FF), BF16)],
        compiler_params=pltpu.CompilerParams(
            dimension_semantics=("parallel",), vmem_limit_bytes=VMEM_LIMIT),
        name="ffn_ln",
    )(x, wg, wu, wd, lg, lb)


def _proj_kernel(x_ref, w_ref, o_ref, *, tn):
    xb = x_ref[...].astype(BF16)
    for c in range(o_ref.shape[1] // tn):
        cols = slice(c * tn, (c + 1) * tn)
        o_ref[:, cols] = _dot(xb, w_ref[:, cols]).astype(o_ref.dtype)


def _qkv_proj(x, w_in, l, tm=512, tn=1152):
    T, D = x.shape
    return pl.pallas_call(
        functools.partial(_proj_kernel, tn=tn),
        grid=(T // tm,),
        in_specs=[
            pl.BlockSpec((tm, D), lambda i: (i, 0)),
            pl.BlockSpec((None, D, QKV_W), lambda i: (l, 0, 0), pipeline_mode=pl.Buffered(1)),
        ],
        out_specs=pl.BlockSpec((tm, QKV_W), lambda i: (i, 0)),
        out_shape=jax.ShapeDtypeStruct((T, QKV_W), BF16),
        compiler_params=pltpu.CompilerParams(
            dimension_semantics=("parallel",), vmem_limit_bytes=VMEM_LIMIT),
        name="qkv_proj",
    )(x, w_in)


def _lane_iota():
    return lax.broadcasted_iota(jnp.int32, (1, LANES), 1)


def _head_lanes(hh):
    lane = _lane_iota()
    return (lane >= hh * HEAD_DIM) & (lane < (hh + 1) * HEAD_DIM)


def _bias_lanes(hh):
    lane = _lane_iota()
    lo = (1 - hh) * HEAD_DIM
    return (lane >= lo) & (lane < lo + N_BIAS_COLS)


def _blk(j):
    return pl.ds(pl.multiple_of(j * ATT_T, ATT_T), ATT_T)


def _key_query_iotas():
    krow = lax.broadcasted_iota(jnp.int32, (ATT_T, ATT_T), 0)
    qcol = lax.broadcasted_iota(jnp.int32, (ATT_T, ATT_T), 1)
    return krow, qcol


def _build_softmax_operands(tab_ref, hb, k_ref, v_ref, kp_ref, vt_ref, kmax_ref, *, n_blocks, scale):
    lane = _lane_iota()
    pos = lax.broadcasted_iota(jnp.int32, (ATT_T, LANES), 0).astype(F32)
    bias_cols = []
    for hh in range(HEADS_PER_BLOCK):
        b = pos * tab_ref[hb * HEADS_PER_BLOCK + hh]
        lo = (1 - hh) * HEAD_DIM
        cols = jnp.zeros_like(b)
        for i in range(N_BIAS_COLS):
            piece = b.astype(BF16).astype(F32)
            cols = jnp.where(lane == lo + i, piece, cols)
            b = b - piece
        bias_cols.append(cols)
    kmax_ref[...] = jnp.zeros_like(kmax_ref)

    def body(n, c):
        kf = k_ref[_blk(n), :].astype(F32) * (scale * LOG2E)
        v_t = v_ref[_blk(n), :].astype(F32).T.astype(BF16)
        for hh in range(HEADS_PER_BLOCK):
            hm = _head_lanes(hh)
            kp_ref[hh, _blk(n), :] = jnp.where(hm, kf, bias_cols[hh]).astype(BF16)
            vt_ref[hh, 0:HEAD_DIM, _blk(n)] = v_t[hh * HEAD_DIM:(hh + 1) * HEAD_DIM, :]
            vt_ref[hh, HEAD_DIM:V_ROWS, _blk(n)] = jnp.ones((DENOM_ROWS, ATT_T), BF16)
            norm2 = jnp.sum(jnp.where(hm, kf * kf, 0.0), axis=1, keepdims=True)
            kmax_ref[hh] = jnp.maximum(kmax_ref[hh], jnp.max(norm2, axis=0, keepdims=True))
        return c

    lax.fori_loop(0, n_blocks, body, 0)


def _chain_qt(c, hh, seg_lo, seg_width, q_t, qt_ref):
    row = lax.broadcasted_iota(jnp.int32, (LANES, ATT_T), 0)
    lo = (1 - hh) * HEAD_DIM
    qsel = jnp.where((row >= seg_lo) & (row < seg_lo + seg_width), q_t, 0.0)
    qt_ref[c] = jnp.where((row >= lo) & (row < lo + N_BIAS_COLS), 1.0, qsel).astype(BF16)
    return qsel, jnp.sum(qsel * qsel, axis=0, keepdims=True)


def _stage_scores(c, hh, n, slot, kp_ref, qt_ref, sbuf_ref):
    sbuf_ref[c * STAGE_DEPTH + slot] = _dot(kp_ref[hh, _blk(jnp.maximum(n, 0)), :], qt_ref[c])


def _diag_blocks(chains, qi, qn2, slope2, kp_ref, vt_ref, kmax_ref, qt_ref, ref_ref, acc_ref, sbuf_ref):
    krow, qcol = _key_query_iotas()
    causal = krow <= qcol
    r = lax.broadcasted_iota(jnp.int32, (1, ATT_T), 1).astype(F32)
    scores = [_dot(kp_ref[hh, _blk(qi), :], qt_ref[c]) for c, hh in enumerate(chains)]
    for slot in range(STAGE_DEPTH):
        for c, hh in enumerate(chains):
            _stage_scores(c, hh, qi - 1 - slot, slot, kp_ref, qt_ref, sbuf_ref)
    gaps = []
    for c, hh in enumerate(chains):
        s = jnp.where(causal, scores[c], NEG)
        m = jnp.max(s, axis=0, keepdims=True)
        bound = jnp.sqrt(qn2[c] * kmax_ref[hh]) * BOUND_SLACK + slope2[hh] * r
        ref = jnp.maximum(m, bound - UNDERFLOW_GUARD)
        ref_ref[c] = ref
        acc_ref[c] = _dot(vt_ref[hh, :, _blk(qi)], jnp.exp2(s - ref).astype(BF16))
        gaps.append(jnp.max(bound - ref) + jnp.max(ref - m))
    return gaps


def _first_block(gap, inv_blk, qi):
    reach = jnp.minimum((gap + SKIP_BITS) * inv_blk, 1e6).astype(jnp.int32)
    return jnp.maximum(qi - 1 - reach, 0)


def _fixed_ref_step(c, hh, n, slot, off, kp_ref, vt_ref, qt_ref, ref_ref, acc_ref, sbuf_ref):
    p = jnp.exp2(sbuf_ref[c * STAGE_DEPTH + slot] - (ref_ref[c] - off)).astype(BF16)
    _stage_scores(c, hh, n - STAGE_DEPTH, slot, kp_ref, qt_ref, sbuf_ref)
    acc_ref[c] += _dot(vt_ref[hh, :, _blk(n)], p)


def _online_diag(c, hh, qi, kp_ref, vt_ref, qt_ref, m_ref, acc_ref):
    krow, qcol = _key_query_iotas()
    s = _dot(kp_ref[hh, _blk(qi), :], qt_ref[c])
    s = jnp.where(krow <= qcol, s, NEG)
    m = jnp.max(s, axis=0, keepdims=True)
    m_ref[c] = m
    acc_ref[c] = _dot(vt_ref[hh, :, _blk(qi)], jnp.exp2(s - m).astype(BF16))


def _online_step(c, hh, n, off, kp_ref, vt_ref, qt_ref, m_ref, acc_ref):
    s = _dot(kp_ref[hh, _blk(n), :], qt_ref[c])
    m_old = m_ref[c]
    m_new = jnp.maximum(m_old, jnp.max(s, axis=0, keepdims=True) + off)
    p = jnp.exp2(s - (m_new - off)).astype(BF16)
    acc_ref[c] = acc_ref[c] * jnp.exp2(m_old - m_new) + _dot(vt_ref[hh, :, _blk(n)], p)
    m_ref[c] = m_new


def _softmax_sweep(qi, chains, first, off_fn, kp_ref, vt_ref, qt_ref, ref_ref, acc_ref, sbuf_ref,
                   split_heads):
    refs = (kp_ref, vt_ref, qt_ref, ref_ref, acc_ref)
    assert STAGE_DEPTH == 2

    def run_fixed(top, left, heads):
        active = [(c, hh) for c, hh in enumerate(chains) if hh in heads]
        per_trip = max(STAGE_DEPTH, ITEMS_PER_TRIP // len(active))
        assert per_trip & (per_trip - 1) == 0
        while per_trip >= 1:
            trips = left // per_trip

            def body(i, carry, top=top, per_trip=per_trip):
                for j in range(per_trip):
                    n = top - i * per_trip - j
                    for c, hh in active:
                        _fixed_ref_step(c, hh, n, j % STAGE_DEPTH, off_fn(hh, n), *refs, sbuf_ref)
                return carry

            lax.fori_loop(0, trips, body, 0)
            top = top - trips * per_trip
            left = left - trips * per_trip
            per_trip //= 2

    far = jnp.minimum(first[0], first[1])
    if split_heads:
        near = jnp.maximum(first[0], first[1])
        together = jnp.minimum((qi - near + 1) // 2 * 2, qi - far)
        run_fixed(qi - 1, together, (0, 1))
        for hh in range(HEADS_PER_BLOCK):
            run_fixed(qi - 1 - together, jnp.maximum(qi - together - first[hh], 0), (hh,))
    else:
        run_fixed(qi - 1, qi - far, (0, 1))

    smallest = None
    for c in range(len(chains)):
        d = jnp.min(acc_ref[c][HEAD_DIM:HEAD_DIM + 1, :])
        smallest = d if smallest is None else jnp.minimum(smallest, d)

    @pl.when(jnp.logical_not(smallest >= MIN_DENOM))
    def _redo():
        for c, hh in enumerate(chains):
            _online_diag(c, hh, qi, *refs)

        def body(n, carry):
            for c, hh in enumerate(chains):
                _online_step(c, hh, n, off_fn(hh, n), *refs)
            return carry

        lax.fori_loop(0, qi, body, 0)


def _normalized(acc):
    return acc[0:HEAD_DIM, :] * (1.0 / acc[HEAD_DIM:HEAD_DIM + 1, :])


def _row_is_head0():
    return lax.broadcasted_iota(jnp.int32, (LANES, ATT_T), 0) < HEAD_DIM


def _moba_kernel(tab_ref, q_ref, k_ref, v_ref, o_ref, kp_ref, vt_ref, kmax_ref, kmean_ref, selm_ref,
                 qt_ref, ref_ref, acc_ref, sbuf_ref, *, n_blocks, scale):
    hb = pl.program_id(1)
    qi = pl.program_id(2)
    nbp = kmean_ref.shape[0]

    @pl.when(qi == 0)
    def _first_tile():
        _build_softmax_operands(tab_ref, hb, k_ref, v_ref, kp_ref, vt_ref, kmax_ref,
                                n_blocks=n_blocks, scale=scale)
        kmean_ref[...] = jnp.zeros_like(kmean_ref)

        def body(n, c):
            kmean_ref[pl.ds(n, 1), :] = jnp.mean(k_ref[_blk(n), :].astype(F32), axis=0, keepdims=True)
            return c

        lax.fori_loop(0, n_blocks, body, 0)

    q_t = q_ref[...].astype(F32).T
    blk_id = lax.broadcasted_iota(jnp.int32, (nbp, ATT_T), 0)
    blk_f = blk_id.astype(F32)
    chains = list(range(HEADS_PER_BLOCK))
    slope2 = [tab_ref[hb * HEADS_PER_BLOCK + hh] for hh in chains]
    qn2 = []
    for hh in chains:
        hm = _head_lanes(hh)
        qsel, norm2 = _chain_qt(hh, hh, hh * HEAD_DIM, HEAD_DIM, q_t, qt_ref)
        qn2.append(norm2)

        gate = _dot(jnp.where(hm, kmean_ref[...], 0.0), qsel, precision=lax.Precision.HIGHEST)
        g = jnp.where(blk_id < qi, gate, -jnp.inf)
        sel = jnp.zeros(g.shape, dtype=jnp.bool_)
        for _ in range(MOBA_TOPK):
            mx = jnp.max(g, axis=0, keepdims=True)
            pick = jnp.min(jnp.where(g == mx, blk_f, float(nbp)), axis=0, keepdims=True)
            hit = blk_f == pick
            sel = sel | (hit & (mx > -jnp.inf))
            g = jnp.where(hit, -jnp.inf, g)
        selm_ref[hh] = jnp.where(sel, 0.0, NEG)

    gaps = _diag_blocks(chains, qi, qn2, slope2, kp_ref, vt_ref, kmax_ref, qt_ref, ref_ref, acc_ref, sbuf_ref)
    first = [_first_block(gaps[hh], tab_ref[N_HEADS + hb * HEADS_PER_BLOCK + hh], qi) for hh in chains]

    def off_fn(hh, n):
        return slope2[hh] * ((n - qi) * ATT_T).astype(F32) + selm_ref[hh, pl.ds(n, 1), :]

    _softmax_sweep(qi, chains, first, off_fn, kp_ref, vt_ref, qt_ref, ref_ref, acc_ref, sbuf_ref,
                   split_heads=False)

    o = jnp.concatenate([_normalized(acc_ref[0]), _normalized(acc_ref[1])], axis=0)
    o_ref[...] = o.T.astype(o_ref.dtype)


def _diff_kernel(tab_ref, lam_ref, g_ref, q_ref, k_ref, v_ref, o_ref, kp_ref, vt_ref, kmax_ref, qt_ref,
                 ref_ref, acc_ref, sbuf_ref, *, n_blocks, scale, lambda_init):
    hb = pl.program_id(1)
    qi = pl.program_id(2)

    @pl.when(qi == 0)
    def _first_tile():
        _build_softmax_operands(tab_ref, hb, k_ref, v_ref, kp_ref, vt_ref, kmax_ref,
                                n_blocks=n_blocks, scale=scale)

    lf = lam_ref[...].astype(F32)
    lam = (jnp.exp(jnp.sum(lf[0:1] * lf[1:2], axis=-1, keepdims=True))
           - jnp.exp(jnp.sum(lf[2:3] * lf[3:4], axis=-1, keepdims=True)) + lambda_init)

    q_t = q_ref[...].astype(F32).T
    chains = [hh for hh in range(HEADS_PER_BLOCK) for _ in range(2)]
    slope2 = [tab_ref[hb * HEADS_PER_BLOCK + hh] for hh in range(HEADS_PER_BLOCK)]
    qn2 = []
    for c, hh in enumerate(chains):
        qn2.append(_chain_qt(c, hh, hh * HEAD_DIM + (c % 2) * DIFF_QK_DIM, DIFF_QK_DIM, q_t, qt_ref)[1])
    gaps = _diag_blocks(chains, qi, qn2, slope2, kp_ref, vt_ref, kmax_ref, qt_ref, ref_ref, acc_ref, sbuf_ref)
    first = [None] * HEADS_PER_BLOCK
    for c, hh in enumerate(chains):
        fb = _first_block(gaps[c], tab_ref[N_HEADS + hb * HEADS_PER_BLOCK + hh], qi)
        first[hh] = fb if first[hh] is None else jnp.minimum(first[hh], fb)

    def off_fn(hh, n):
        return slope2[hh] * ((n - qi) * ATT_T).astype(F32)

    _softmax_sweep(qi, chains, first, off_fn, kp_ref, vt_ref, qt_ref, ref_ref, acc_ref, sbuf_ref,
                   split_heads=True)

    outs = []
    for hh in range(HEADS_PER_BLOCK):
        o = _normalized(acc_ref[2 * hh]) - lam * _normalized(acc_ref[2 * hh + 1])
        ms = jnp.mean(o * o, axis=0, keepdims=True)
        outs.append(o * lax.rsqrt(ms + SUBLN_EPS))
    o = jnp.concatenate(outs, axis=0).T * (g_ref[...] * (1.0 - lambda_init))
    o_ref[...] = o.astype(o_ref.dtype)


def _log_sigmoids(z):
    l = jnp.log(1.0 + jnp.exp(-jnp.abs(z)))
    return jnp.minimum(-z, 0.0) - l, jnp.minimum(z, 0.0) - l


def _suffix_sums(upper, lg):
    hi = lg.astype(BF16)
    lo = (lg - hi.astype(F32)).astype(BF16)
    return _dot(upper, hi) + _dot(upper, lo)


def _sb_blocks(blocks, k_ref, vt_ref, qt_ref, carry_ref, acc_ref):
    krow, qcol = _key_query_iotas()
    before = krow < qcol
    upper = (qcol > krow).astype(BF16)
    heads = range(HEADS_PER_BLOCK)
    items = [(n, diag, hh) for n, diag in blocks for hh in heads]
    z = [_dot(k_ref[_blk(n), :], qt_ref[hh]) for n, _, hh in items]
    log_sig, log_surv, tails = [], [], []
    for i, (n, diag, hh) in enumerate(items):
        lneg, lpos = _log_sigmoids(z[i])
        if diag:
            lneg = jnp.where(before, lneg, 0.0)
        log_surv.append(lneg)
        log_sig.append(lpos)
        tails.append(_suffix_sums(upper, lneg))
    starts_fresh = blocks[0][1]
    carry = [None if starts_fresh else carry_ref[hh] for hh in heads]
    acc = [None if starts_fresh else acc_ref[hh] for hh in heads]
    for i, (n, diag, hh) in enumerate(items):
        if diag:
            w = jnp.where(before, jnp.exp(log_sig[i] + tails[i]), 0.0)
        else:
            w = jnp.exp(log_sig[i] + (tails[i] + carry[hh]))
        part = _dot(vt_ref[:, _blk(n)], w.astype(BF16))
        acc[hh] = part if acc[hh] is None else acc[hh] + part
        total = jnp.sum(log_surv[i], axis=0, keepdims=True)
        carry[hh] = total if carry[hh] is None else carry[hh] + total
    worst = None
    for hh in heads:
        acc_ref[hh] = acc[hh]
        carry_ref[hh] = carry[hh]
        top = jnp.max(carry[hh])
        worst = top if worst is None else jnp.maximum(worst, top)
    return worst


def _sb_kernel(q_ref, k_ref, v_ref, o_ref, vt_ref, qt_ref, carry_ref, acc_ref, *, n_blocks, scale):
    qi = pl.program_id(2)

    @pl.when(qi == 0)
    def _first_tile():
        def body(n, c):
            vt_ref[:, _blk(n)] = v_ref[_blk(n), :].astype(F32).T.astype(BF16)
            return c

        lax.fori_loop(0, n_blocks, body, 0)

    q_t = (q_ref[...].astype(F32) * scale).T
    row_h0 = _row_is_head0()
    for hh in range(HEADS_PER_BLOCK):
        in_head = row_h0 if hh == 0 else jnp.logical_not(row_h0)
        qt_ref[hh] = jnp.where(in_head, q_t, 0.0).astype(BF16)
    refs = (k_ref, vt_ref, qt_ref, carry_ref, acc_ref)
    worst = lax.cond(qi > 0,
                     lambda: _sb_blocks([(qi, True), (qi - 1, False)], *refs),
                     lambda: _sb_blocks([(qi, True)], *refs))

    def cond(c):
        j, live = c
        return (j >= 0) & live

    def body(c):
        j, _ = c
        return j - 1, _sb_blocks([(j, False)], *refs) > SB_EXIT

    lax.while_loop(cond, body, (qi - 2, worst > SB_EXIT))

    o = jnp.where(_row_is_head0(), acc_ref[0], acc_ref[1])
    o_ref[...] = o.T.astype(o_ref.dtype)


def _attn_specs(B, S, q_base, k_base, v_base):
    nq = S // ATT_T
    qspec = pl.BlockSpec((None, ATT_T, LANES), lambda b, h, i: (b, i, q_base + h))
    kspec = pl.BlockSpec((None, S, LANES), lambda b, h, i: (b, 0, k_base + h))
    vspec = pl.BlockSpec((None, S, LANES), lambda b, h, i: (b, 0, v_base + h))
    ospec = pl.BlockSpec((None, ATT_T, LANES), lambda b, h, i: (b, i, h))
    grid = (B, N_HEAD_BLOCKS, nq)
    params = pltpu.CompilerParams(
        dimension_semantics=("parallel", "parallel", "arbitrary"), vmem_limit_bytes=VMEM_LIMIT)
    return grid, qspec, kspec, vspec, ospec, params


_SMEM_SPEC = pl.BlockSpec(memory_space=pltpu.SMEM)


def _moba(qkv, tab):
    B, S, _ = qkv.shape
    grid, qspec, kspec, vspec, ospec, params = _attn_specs(B, S, 0, 3, 6)
    n_blocks = S // MOBA_BLOCK
    nbp = -(-n_blocks // SUBLANES) * SUBLANES
    return pl.pallas_call(
        functools.partial(_moba_kernel, n_blocks=n_blocks, scale=HEAD_DIM ** -0.5),
        grid=grid,
        in_specs=[_SMEM_SPEC, qspec, kspec, vspec],
        out_specs=ospec,
        out_shape=jax.ShapeDtypeStruct((B, S, MIX_W), BF16),
        scratch_shapes=[
            pltpu.VMEM((HEADS_PER_BLOCK, S, LANES), BF16),
            pltpu.VMEM((HEADS_PER_BLOCK, V_ROWS, S), BF16),
            pltpu.VMEM((HEADS_PER_BLOCK, 1, 1), F32),
            pltpu.VMEM((nbp, LANES), F32),
            pltpu.VMEM((HEADS_PER_BLOCK, nbp, ATT_T), F32),
            pltpu.VMEM((HEADS_PER_BLOCK, LANES, ATT_T), BF16),
            pltpu.VMEM((HEADS_PER_BLOCK, 1, ATT_T), F32),
            pltpu.VMEM((HEADS_PER_BLOCK, V_ROWS, ATT_T), F32),
            pltpu.VMEM((HEADS_PER_BLOCK * STAGE_DEPTH, ATT_T, ATT_T), F32),
        ],
        compiler_params=params,
        name="moba_attn",
    )(tab, qkv, qkv, qkv)


def _diff(qkv, tab, diff_lambda, subln_g, lambda_init):
    B, S, _ = qkv.shape
    grid, qspec, kspec, vspec, ospec, params = _attn_specs(B, S, 9, 12, 15)
    n_chains = 2 * HEADS_PER_BLOCK
    return pl.pallas_call(
        functools.partial(_diff_kernel, n_blocks=S // ATT_T, scale=DIFF_QK_DIM ** -0.5,
                          lambda_init=lambda_init),
        grid=grid,
        in_specs=[
            _SMEM_SPEC,
            pl.BlockSpec(diff_lambda.shape, lambda b, h, i: (0, 0)),
            pl.BlockSpec((1, LANES), lambda b, h, i: (0, 0)),
            qspec, kspec, vspec,
        ],
        out_specs=ospec,
        out_shape=jax.ShapeDtypeStruct((B, S, MIX_W), BF16),
        scratch_shapes=[
            pltpu.VMEM((HEADS_PER_BLOCK, S, LANES), BF16),
            pltpu.VMEM((HEADS_PER_BLOCK, V_ROWS, S), BF16),
            pltpu.VMEM((HEADS_PER_BLOCK, 1, 1), F32),
            pltpu.VMEM((n_chains, LANES, ATT_T), BF16),
            pltpu.VMEM((n_chains, 1, ATT_T), F32),
            pltpu.VMEM((n_chains, V_ROWS, ATT_T), F32),
            pltpu.VMEM((n_chains * STAGE_DEPTH, ATT_T, ATT_T), F32),
        ],
        compiler_params=params,
        name="diff_attn",
    )(tab, diff_lambda, subln_g, qkv, qkv, qkv)


def _sb(qkv):
    B, S, _ = qkv.shape
    grid, qspec, kspec, vspec, ospec, params = _attn_specs(B, S, 18, 21, 24)
    return pl.pallas_call(
        functools.partial(_sb_kernel, n_blocks=S // ATT_T, scale=HEAD_DIM ** -0.5),
        grid=grid,
        in_specs=[qspec, kspec, vspec],
        out_specs=ospec,
        out_shape=jax.ShapeDtypeStruct((B, S, MIX_W), BF16),
        scratch_shapes=[
            pltpu.VMEM((LANES, S), BF16),
            pltpu.VMEM((HEADS_PER_BLOCK, LANES, ATT_T), BF16),
            pltpu.VMEM((HEADS_PER_BLOCK, 1, ATT_T), F32),
            pltpu.VMEM((HEADS_PER_BLOCK, LANES, ATT_T), F32),
        ],
        compiler_params=params,
        name="sb_attn",
    )(qkv, qkv, qkv)


def _merge_ln_kernel(x_ref, om_ref, od_ref, os_ref, wg_ref, bg_ref, wbr_ref, wo_ref, lg_ref, lb_ref,
                     o_ref, *, alpha, d_model):
    x = x_ref[...]
    xb = x.astype(BF16)
    merged = None
    for b, ob_ref in enumerate((om_ref, od_ref, os_ref)):
        glogit = _dot(xb, wg_ref[:, b * d_model:(b + 1) * d_model]) + bg_ref[b:b + 1, :]
        term = jax.nn.sigmoid(glogit) * _dot(ob_ref[...], wbr_ref[b])
        merged = term if merged is None else merged + term
    y = alpha * x + _dot(merged.astype(BF16), wo_ref[...])
    o_ref[...] = _layer_norm(y, lg_ref[...], lb_ref[...])


def _merge_ln(x, o_m, o_d, o_s, w_gate, b_gate, w_br, w_out, lg, lb, l, alpha, tm=512):
    T, D = x.shape
    return pl.pallas_call(
        functools.partial(_merge_ln_kernel, alpha=alpha, d_model=D),
        grid=(T // tm,),
        in_specs=[
            pl.BlockSpec((tm, D), lambda i: (i, 0)),
            pl.BlockSpec((tm, MIX_W), lambda i: (i, 0)),
            pl.BlockSpec((tm, MIX_W), lambda i: (i, 0)),
            pl.BlockSpec((tm, MIX_W), lambda i: (i, 0)),
            pl.BlockSpec((None, D, N_BRANCH * D), lambda i: (l, 0, 0)),
            pl.BlockSpec((None, N_BRANCH, D), lambda i: (l, 0, 0)),
            pl.BlockSpec((None, N_BRANCH, MIX_W, D), lambda i: (l, 0, 0, 0)),
            pl.BlockSpec((None, D, D), lambda i: (l, 0, 0)),
            pl.BlockSpec((1, D), lambda i: (0, 0)),
            pl.BlockSpec((1, D), lambda i: (0, 0)),
        ],
        out_specs=pl.BlockSpec((tm, D), lambda i: (i, 0)),
        out_shape=jax.ShapeDtypeStruct((T, D), F32),
        compiler_params=pltpu.CompilerParams(
            dimension_semantics=("parallel",), vmem_limit_bytes=VMEM_LIMIT),
        name="merge_ln",
    )(x, o_m, o_d, o_s, w_gate, b_gate, w_br, w_out, lg, lb)


def _alibi_slopes(n):
    return (2.0 ** (-8.0 * np.arange(1, n + 1, dtype=np.float32) / n)).astype(np.float32)


def _slope_table(slopes):
    s2 = slopes.astype(np.float64) * LOG2E
    return np.concatenate([s2, 1.0 / (s2 * ATT_T)]).astype(np.float32)


def kernel(x, ln_g, ln_b, ffn_w_gate, ffn_w_up, ffn_w_down, w_in, b_gate, diff_lambda, diff_subln_g,
           w_br_moba, w_br_diff, w_br_sb, w_out):
    B, S, D = x.shape
    depth = ln_g.shape[0]
    assert S % ATT_T == 0 and ATT_T == MOBA_BLOCK
    assert w_in.shape[-1] == QKV_W + N_BRANCH * D
    alpha = (2.0 * depth) ** 0.25

    wg = ffn_w_gate.astype(BF16)
    wu = ffn_w_up.astype(BF16)
    wd = ffn_w_down.astype(BF16)
    w_qkv = w_in[:, :, :QKV_W].astype(BF16)
    w_gate = w_in[:, :, QKV_W:].astype(BF16)
    w_br = jnp.stack([w_br_moba, w_br_diff, w_br_sb], axis=1).astype(BF16)
    w_out_b = w_out.astype(BF16)
    slopes = _alibi_slopes(2 * N_HEADS)
    tab_moba = jnp.asarray(_slope_table(slopes[0::2]))
    tab_diff = jnp.asarray(_slope_table(slopes[1::2]))
    subln_g = jnp.tile(diff_subln_g.astype(F32), (1, HEADS_PER_BLOCK))

    h = x.reshape(B * S, D)
    for l in range(depth):
        lg = ln_g[l][:, None, :]
        lb = ln_b[l][:, None, :]
        h = _ffn_ln(h, wg, wu, wd, lg[0], lb[0], l, 0, alpha)
        qkv = _qkv_proj(h, w_qkv, l).reshape(B, S, QKV_W)
        lambda_init = 0.8 - 0.6 * math.exp(-0.3 * l)
        o_m = _moba(qkv, tab_moba)
        o_d = _diff(qkv, tab_diff, diff_lambda[l], subln_g[l][None, :], lambda_init)
        o_s = _sb(qkv)
        T = B * S
        h = _merge_ln(h, o_m.reshape(T, MIX_W), o_d.reshape(T, MIX_W), o_s.reshape(T, MIX_W),
                      w_gate, b_gate, w_br, w_out_b, lg[1], lb[1], l, alpha)
        h = _ffn_ln(h, wg, wu, wd, lg[2], lb[2], l, 1, alpha)
    return h.reshape(B, S, D)
```

```python
import functools
import math

import numpy as np
import jax
import jax.numpy as jnp
from jax import lax
from jax.experimental import pallas as pl
from jax.experimental.pallas import tpu as pltpu

F32 = jnp.float32
BF16 = jnp.bfloat16

HEAD_DIM = 64
N_HEADS = 6
DIFF_QK_DIM = HEAD_DIM // 2
MOBA_BLOCK = 256
MOBA_TOPK = 3
N_BRANCH = 3
LN_EPS = 1e-5
SUBLN_EPS = 1e-5

LANES = 128
SUBLANES = 8
DENOM_ROWS = 16
V_ROWS = HEAD_DIM + DENOM_ROWS
HEADS_PER_BLOCK = LANES // HEAD_DIM
N_HEAD_BLOCKS = N_HEADS // HEADS_PER_BLOCK
MIX_W = N_HEADS * HEAD_DIM
QKV_W = 9 * MIX_W
ATT_T = 256
NEG = -1e30
SB_EXIT = -110.0
LOG2E = math.log2(math.e)
N_BIAS_COLS = 3
BOUND_SLACK = 1.01
UNDERFLOW_GUARD = 100.0
SKIP_BITS = 150.0
MIN_DENOM = 2.0 ** -60
ITEMS_PER_TRIP = 16
STAGE_DEPTH = 2
VMEM_LIMIT = 56 * 1024 * 1024


def _dot(a, b, precision=None):
    return jnp.dot(a, b, precision=precision, preferred_element_type=F32)


def _layer_norm(y, g, b):
    mu = jnp.mean(y, axis=-1, keepdims=True)
    yc = y - mu
    var = jnp.mean(yc * yc, axis=-1, keepdims=True)
    return yc * lax.rsqrt(var + LN_EPS) * g + b


def _ffn_ln_kernel(x_ref, wg_ref, wu_ref, wd_ref, lg_ref, lb_ref, o_ref, hid_ref, *, alpha, tf):
    x = x_ref[...]
    xb = x.astype(BF16)
    for c in range(hid_ref.shape[1] // tf):
        cols = slice(c * tf, (c + 1) * tf)
        gate = _dot(xb, wg_ref[:, cols])
        up = _dot(xb, wu_ref[:, cols])
        hid_ref[:, cols] = (gate * jax.nn.sigmoid(gate) * up).astype(BF16)
    y = alpha * x + 0.5 * _dot(hid_ref[...], wd_ref[...])
    o_ref[...] = _layer_norm(y, lg_ref[...], lb_ref[...])


def _ffn_ln(x, wg, wu, wd, lg, lb, l, j, alpha, tm=512, tf=256):
    T, D = x.shape
    FF = wg.shape[-1]
    resident = pl.Buffered(1)
    return pl.pallas_call(
        functools.partial(_ffn_ln_kernel, alpha=alpha, tf=tf),
        grid=(T // tm,),
        in_specs=[
            pl.BlockSpec((tm, D), lambda i: (i, 0)),
            pl.BlockSpec((None, None, D, FF), lambda i: (l, j, 0, 0), pipeline_mode=resident),
            pl.BlockSpec((None, None, D, FF), lambda i: (l, j, 0, 0), pipeline_mode=resident),
            pl.BlockSpec((None, None, FF, D), lambda i: (l, j, 0, 0), pipeline_mode=resident),
            pl.BlockSpec((1, D), lambda i: (0, 0)),
            pl.BlockSpec((1, D), lambda i: (0, 0)),
        ],
        out_specs=pl.BlockSpec((tm, D), lambda i: (i, 0)),
        out_shape=jax.ShapeDtypeStruct((T, D), F32),
        scratch_shapes=[pltpu.VMEM((tm, FF), BF16)],
        compiler_params=pltpu.CompilerParams(
            dimension_semantics=("parallel",), vmem_limit_bytes=VMEM_LIMIT),
        name="ffn_ln",
    )(x, wg, wu, wd, lg, lb)


def _proj_kernel(x_ref, w_ref, o_ref, *, tn):
    xb = x_ref[...].astype(BF16)
    for c in range(o_ref.shape[1] // tn):
        cols = slice(c * tn, (c + 1) * tn)
        o_ref[:, cols] = _dot(xb, w_ref[:, cols]).astype(o_ref.dtype)


def _qkv_proj(x, w_in, l, tm=512, tn=1152):
    T, D = x.shape
    return pl.pallas_call(
        functools.partial(_proj_kernel, tn=tn),
        grid=(T // tm,),
        in_specs=[
            pl.BlockSpec((tm, D), lambda i: (i, 0)),
            pl.BlockSpec((None, D, QKV_W), lambda i: (l, 0, 0), pipeline_mode=pl.Buffered(1)),
        ],
        out_specs=pl.BlockSpec((tm, QKV_W), lambda i: (i, 0)),
        out_shape=jax.ShapeDtypeStruct((T, QKV_W), BF16),
        compiler_params=pltpu.CompilerParams(
            dimension_semantics=("parallel",), vmem_limit_bytes=VMEM_LIMIT),
        name="qkv_proj",
    )(x, w_in)


def _lane_iota():
    return lax.broadcasted_iota(jnp.int32, (1, LANES), 1)


def _head_lanes(hh):
    lane = _lane_iota()
    return (lane >= hh * HEAD_DIM) & (lane < (hh + 1) * HEAD_DIM)


def _bias_lanes(hh):
    lane = _lane_iota()
    lo = (1 - hh) * HEAD_DIM
    return (lane >= lo) & (lane < lo + N_BIAS_COLS)


def _blk(j):
    return pl.ds(pl.multiple_of(j * ATT_T, ATT_T), ATT_T)


def _key_query_iotas():
    krow = lax.broadcasted_iota(jnp.int32, (ATT_T, ATT_T), 0)
    qcol = lax.broadcasted_iota(jnp.int32, (ATT_T, ATT_T), 1)
    return krow, qcol


def _build_softmax_operands(tab_ref, hb, k_ref, v_ref, kp_ref, vt_ref, kmax_ref, *, n_blocks, scale):
    lane = _lane_iota()
    pos = lax.broadcasted_iota(jnp.int32, (ATT_T, LANES), 0).astype(F32)
    bias_cols = []
    for hh in range(HEADS_PER_BLOCK):
        b = pos * tab_ref[hb * HEADS_PER_BLOCK + hh]
        lo = (1 - hh) * HEAD_DIM
        cols = jnp.zeros_like(b)
        for i in range(N_BIAS_COLS):
            piece = b.astype(BF16).astype(F32)
            cols = jnp.where(lane == lo + i, piece, cols)
            b = b - piece
        bias_cols.append(cols)
    kmax_ref[...] = jnp.zeros_like(kmax_ref)

    def body(n, c):
        kf = k_ref[_blk(n), :].astype(F32) * (scale * LOG2E)
        v_t = v_ref[_blk(n), :].astype(F32).T.astype(BF16)
        for hh in range(HEADS_PER_BLOCK):
            hm = _head_lanes(hh)
            kp_ref[hh, _blk(n), :] = jnp.where(hm, kf, bias_cols[hh]).astype(BF16)
            vt_ref[hh, 0:HEAD_DIM, _blk(n)] = v_t[hh * HEAD_DIM:(hh + 1) * HEAD_DIM, :]
            vt_ref[hh, HEAD_DIM:V_ROWS, _blk(n)] = jnp.ones((DENOM_ROWS, ATT_T), BF16)
            norm2 = jnp.sum(jnp.where(hm, kf * kf, 0.0), axis=1, keepdims=True)
            kmax_ref[hh] = jnp.maximum(kmax_ref[hh], jnp.max(norm2, axis=0, keepdims=True))
        return c

    lax.fori_loop(0, n_blocks, body, 0)


def _chain_qt(c, hh, seg_lo, seg_width, q_t, qt_ref):
    row = lax.broadcasted_iota(jnp.int32, (LANES, ATT_T), 0)
    lo = (1 - hh) * HEAD_DIM
    qsel = jnp.where((row >= seg_lo) & (row < seg_lo + seg_width), q_t, 0.0)
    qt_ref[c] = jnp.where((row >= lo) & (row < lo + N_BIAS_COLS), 1.0, qsel).astype(BF16)
    return qsel, jnp.sum(qsel * qsel, axis=0, keepdims=True)


def _stage_scores(c, hh, n, slot, kp_ref, qt_ref, sbuf_ref):
    sbuf_ref[c * STAGE_DEPTH + slot] = _dot(kp_ref[hh, _blk(jnp.maximum(n, 0)), :], qt_ref[c])


def _diag_blocks(chains, qi, qn2, slope2, kp_ref, vt_ref, kmax_ref, qt_ref, ref_ref, acc_ref, sbuf_ref):
    krow, qcol = _key_query_iotas()
    causal = krow <= qcol
    r = lax.broadcasted_iota(jnp.int32, (1, ATT_T), 1).astype(F32)
    scores = [_dot(kp_ref[hh, _blk(qi), :], qt_ref[c]) for c, hh in enumerate(chains)]
    for slot in range(STAGE_DEPTH):
        for c, hh in enumerate(chains):
            _stage_scores(c, hh, qi - 1 - slot, slot, kp_ref, qt_ref, sbuf_ref)
    gaps = []
    for c, hh in enumerate(chains):
        s = jnp.where(causal, scores[c], NEG)
        m = jnp.max(s, axis=0, keepdims=True)
        bound = jnp.sqrt(qn2[c] * kmax_ref[hh]) * BOUND_SLACK + slope2[hh] * r
        ref = jnp.maximum(m, bound - UNDERFLOW_GUARD)
        ref_ref[c] = ref
        acc_ref[c] = _dot(vt_ref[hh, :, _blk(qi)], jnp.exp2(s - ref).astype(BF16))
        gaps.append(jnp.max(bound - ref) + jnp.max(ref - m))
    return gaps


def _first_block(gap, inv_blk, qi):
    reach = jnp.minimum((gap + SKIP_BITS) * inv_blk, 1e6).astype(jnp.int32)
    return jnp.maximum(qi - 1 - reach, 0)


def _fixed_ref_step(c, hh, n, slot, off, kp_ref, vt_ref, qt_ref, ref_ref, acc_ref, sbuf_ref):
    p = jnp.exp2(sbuf_ref[c * STAGE_DEPTH + slot] - (ref_ref[c] - off)).astype(BF16)
    _stage_scores(c, hh, n - STAGE_DEPTH, slot, kp_ref, qt_ref, sbuf_ref)
    acc_ref[c] += _dot(vt_ref[hh, :, _blk(jnp.maximum(n, 0))], p)


def _online_diag(c, hh, qi, kp_ref, vt_ref, qt_ref, m_ref, acc_ref):
    krow, qcol = _key_query_iotas()
    s = _dot(kp_ref[hh, _blk(qi), :], qt_ref[c])
    s = jnp.where(krow <= qcol, s, NEG)
    m = jnp.max(s, axis=0, keepdims=True)
    m_ref[c] = m
    acc_ref[c] = _dot(vt_ref[hh, :, _blk(qi)], jnp.exp2(s - m).astype(BF16))


def _online_step(c, hh, n, off, kp_ref, vt_ref, qt_ref, m_ref, acc_ref):
    s = _dot(kp_ref[hh, _blk(n), :], qt_ref[c])
    m_old = m_ref[c]
    m_new = jnp.maximum(m_old, jnp.max(s, axis=0, keepdims=True) + off)
    p = jnp.exp2(s - (m_new - off)).astype(BF16)
    acc_ref[c] = acc_ref[c] * jnp.exp2(m_old - m_new) + _dot(vt_ref[hh, :, _blk(n)], p)
    m_ref[c] = m_new


def _softmax_sweep(qi, chains, first, off_fn, kp_ref, vt_ref, qt_ref, ref_ref, acc_ref, sbuf_ref,
                   split_heads):
    refs = (kp_ref, vt_ref, qt_ref, ref_ref, acc_ref)
    assert STAGE_DEPTH == 2

    def run_fixed(top, left, heads):
        active = [(c, hh) for c, hh in enumerate(chains) if hh in heads]
        per_trip = max(STAGE_DEPTH, ITEMS_PER_TRIP // len(active))
        assert per_trip % STAGE_DEPTH == 0

        def sweep(top, trips, per_trip):
            def body(i, carry):
                for j in range(per_trip):
                    n = top - i * per_trip - j
                    off_end = jnp.where(n < 0, NEG, 0.0)
                    for c, hh in active:
                        off = off_fn(hh, jnp.maximum(n, 0)) + off_end
                        _fixed_ref_step(c, hh, n, j % STAGE_DEPTH, off, *refs, sbuf_ref)
                return carry

            lax.fori_loop(0, trips, body, 0)

        whole = left // per_trip
        sweep(top, whole, per_trip)
        if per_trip > STAGE_DEPTH:
            rest = left - whole * per_trip
            sweep(top - whole * per_trip, (rest + STAGE_DEPTH - 1) // STAGE_DEPTH, STAGE_DEPTH)
        else:
            sweep(top - whole * per_trip, left - whole * per_trip, 1)

    far = jnp.minimum(first[0], first[1])
    if split_heads:
        near = jnp.maximum(first[0], first[1])
        together = jnp.minimum((qi - near + 1) // 2 * 2, qi - far)
        run_fixed(qi - 1, together, (0, 1))
        for hh in range(HEADS_PER_BLOCK):
            run_fixed(qi - 1 - together, jnp.maximum(qi - together - first[hh], 0), (hh,))
    else:
        run_fixed(qi - 1, qi - far, (0, 1))

    smallest = None
    for c in range(len(chains)):
        d = jnp.min(acc_ref[c][HEAD_DIM:HEAD_DIM + 1, :])
        smallest = d if smallest is None else jnp.minimum(smallest, d)

    @pl.when(jnp.logical_not(smallest >= MIN_DENOM))
    def _redo():
        for c, hh in enumerate(chains):
            _online_diag(c, hh, qi, *refs)

        def body(n, carry):
            for c, hh in enumerate(chains):
                _online_step(c, hh, n, off_fn(hh, n), *refs)
            return carry

        lax.fori_loop(0, qi, body, 0)


def _normalized(acc):
    return acc[0:HEAD_DIM, :] * (1.0 / acc[HEAD_DIM:HEAD_DIM + 1, :])


def _row_is_head0():
    return lax.broadcasted_iota(jnp.int32, (LANES, ATT_T), 0) < HEAD_DIM


def _moba_kernel(tab_ref, q_ref, k_ref, v_ref, o_ref, kp_ref, vt_ref, kmax_ref, kmean_ref, selm_ref,
                 qt_ref, ref_ref, acc_ref, sbuf_ref, *, n_blocks, scale):
    hb = pl.program_id(1)
    qi = pl.program_id(2)
    nbp = kmean_ref.shape[0]

    @pl.when(qi == 0)
    def _first_tile():
        _build_softmax_operands(tab_ref, hb, k_ref, v_ref, kp_ref, vt_ref, kmax_ref,
                                n_blocks=n_blocks, scale=scale)
        kmean_ref[...] = jnp.zeros_like(kmean_ref)

        def body(n, c):
            kmean_ref[pl.ds(n, 1), :] = jnp.mean(k_ref[_blk(n), :].astype(F32), axis=0, keepdims=True)
            return c

        lax.fori_loop(0, n_blocks, body, 0)

    q_t = q_ref[...].astype(F32).T
    blk_id = lax.broadcasted_iota(jnp.int32, (nbp, ATT_T), 0)
    blk_f = blk_id.astype(F32)
    chains = list(range(HEADS_PER_BLOCK))
    slope2 = [tab_ref[hb * HEADS_PER_BLOCK + hh] for hh in chains]
    qn2 = []
    for hh in chains:
        hm = _head_lanes(hh)
        qsel, norm2 = _chain_qt(hh, hh, hh * HEAD_DIM, HEAD_DIM, q_t, qt_ref)
        qn2.append(norm2)

        gate = _dot(jnp.where(hm, kmean_ref[...], 0.0), qsel, precision=lax.Precision.HIGHEST)
        g = jnp.where(blk_id < qi, gate, -jnp.inf)
        sel = jnp.zeros(g.shape, dtype=jnp.bool_)
        for _ in range(MOBA_TOPK):
            mx = jnp.max(g, axis=0, keepdims=True)
            pick = jnp.min(jnp.where(g == mx, blk_f, float(nbp)), axis=0, keepdims=True)
            hit = blk_f == pick
            sel = sel | (hit & (mx > -jnp.inf))
            g = jnp.where(hit, -jnp.inf, g)
        selm_ref[hh] = jnp.where(sel, 0.0, NEG)

    gaps = _diag_blocks(chains, qi, qn2, slope2, kp_ref, vt_ref, kmax_ref, qt_ref, ref_ref, acc_ref, sbuf_ref)
    first = [_first_block(gaps[hh], tab_ref[N_HEADS + hb * HEADS_PER_BLOCK + hh], qi) for hh in chains]

    def off_fn(hh, n):
        return slope2[hh] * ((n - qi) * ATT_T).astype(F32) + selm_ref[hh, pl.ds(n, 1), :]

    _softmax_sweep(qi, chains, first, off_fn, kp_ref, vt_ref, qt_ref, ref_ref, acc_ref, sbuf_ref,
                   split_heads=False)

    o = jnp.concatenate([_normalized(acc_ref[0]), _normalized(acc_ref[1])], axis=0)
    o_ref[...] = o.T.astype(o_ref.dtype)


def _diff_kernel(tab_ref, lam_ref, g_ref, q_ref, k_ref, v_ref, o_ref, kp_ref, vt_ref, kmax_ref, qt_ref,
                 ref_ref, acc_ref, sbuf_ref, *, n_blocks, scale, lambda_init):
    hb = pl.program_id(1)
    qi = pl.program_id(2)

    @pl.when(qi == 0)
    def _first_tile():
        _build_softmax_operands(tab_ref, hb, k_ref, v_ref, kp_ref, vt_ref, kmax_ref,
                                n_blocks=n_blocks, scale=scale)

    lf = lam_ref[...].astype(F32)
    lam = (jnp.exp(jnp.sum(lf[0:1] * lf[1:2], axis=-1, keepdims=True))
           - jnp.exp(jnp.sum(lf[2:3] * lf[3:4], axis=-1, keepdims=True)) + lambda_init)

    q_t = q_ref[...].astype(F32).T
    chains = [hh for hh in range(HEADS_PER_BLOCK) for _ in range(2)]
    slope2 = [tab_ref[hb * HEADS_PER_BLOCK + hh] for hh in range(HEADS_PER_BLOCK)]
    qn2 = []
    for c, hh in enumerate(chains):
        qn2.append(_chain_qt(c, hh, hh * HEAD_DIM + (c % 2) * DIFF_QK_DIM, DIFF_QK_DIM, q_t, qt_ref)[1])
    gaps = _diag_blocks(chains, qi, qn2, slope2, kp_ref, vt_ref, kmax_ref, qt_ref, ref_ref, acc_ref, sbuf_ref)
    first = [None] * HEADS_PER_BLOCK
    for c, hh in enumerate(chains):
        fb = _first_block(gaps[c], tab_ref[N_HEADS + hb * HEADS_PER_BLOCK + hh], qi)
        first[hh] = fb if first[hh] is None else jnp.minimum(first[hh], fb)

    def off_fn(hh, n):
        return slope2[hh] * ((n - qi) * ATT_T).astype(F32)

    _softmax_sweep(qi, chains, first, off_fn, kp_ref, vt_ref, qt_ref, ref_ref, acc_ref, sbuf_ref,
                   split_heads=True)

    outs = []
    for hh in range(HEADS_PER_BLOCK):
        o = _normalized(acc_ref[2 * hh]) - lam * _normalized(acc_ref[2 * hh + 1])
        ms = jnp.mean(o * o, axis=0, keepdims=True)
        outs.append(o * lax.rsqrt(ms + SUBLN_EPS))
    o = jnp.concatenate(outs, axis=0).T * (g_ref[...] * (1.0 - lambda_init))
    o_ref[...] = o.astype(o_ref.dtype)


def _log_sigmoids(z):
    l = jnp.log(1.0 + jnp.exp(-jnp.abs(z)))
    return jnp.minimum(-z, 0.0) - l, jnp.minimum(z, 0.0) - l


def _suffix_sums(upper, lg):
    hi = lg.astype(BF16)
    lo = (lg - hi.astype(F32)).astype(BF16)
    return _dot(upper, hi) + _dot(upper, lo)


def _sb_blocks(blocks, k_ref, vt_ref, qt_ref, carry_ref, acc_ref):
    krow, qcol = _key_query_iotas()
    before = krow < qcol
    upper = (qcol > krow).astype(BF16)
    heads = range(HEADS_PER_BLOCK)
    items = [(n, diag, hh) for n, diag in blocks for hh in heads]
    z = [_dot(k_ref[_blk(n), :], qt_ref[hh]) for n, _, hh in items]
    log_sig, log_surv, tails = [], [], []
    for i, (n, diag, hh) in enumerate(items):
        lneg, lpos = _log_sigmoids(z[i])
        if diag:
            lneg = jnp.where(before, lneg, 0.0)
        log_surv.append(lneg)
        log_sig.append(lpos)
        tails.append(_suffix_sums(upper, lneg))
    starts_fresh = blocks[0][1]
    carry = [None if starts_fresh else carry_ref[hh] for hh in heads]
    acc = [None if starts_fresh else acc_ref[hh] for hh in heads]
    for i, (n, diag, hh) in enumerate(items):
        if diag:
            w = jnp.where(before, jnp.exp(log_sig[i] + tails[i]), 0.0)
        else:
            w = jnp.exp(log_sig[i] + (tails[i] + carry[hh]))
        part = _dot(vt_ref[hh, :, _blk(n)], w.astype(BF16))
        acc[hh] = part if acc[hh] is None else acc[hh] + part
        total = jnp.sum(log_surv[i], axis=0, keepdims=True)
        carry[hh] = total if carry[hh] is None else carry[hh] + total
    worst = None
    for hh in heads:
        acc_ref[hh] = acc[hh]
        carry_ref[hh] = carry[hh]
        top = jnp.max(carry[hh])
        worst = top if worst is None else jnp.maximum(worst, top)
    return worst


def _sb_kernel(q_ref, k_ref, v_ref, o_ref, vt_ref, qt_ref, carry_ref, acc_ref, *, n_blocks, scale):
    qi = pl.program_id(2)

    @pl.when(qi == 0)
    def _first_tile():
        def body(n, c):
            v_t = v_ref[_blk(n), :].astype(F32).T.astype(BF16)
            for hh in range(HEADS_PER_BLOCK):
                vt_ref[hh, :, _blk(n)] = v_t[hh * HEAD_DIM:(hh + 1) * HEAD_DIM, :]
            return c

        lax.fori_loop(0, n_blocks, body, 0)

    q_t = (q_ref[...].astype(F32) * scale).T
    row_h0 = _row_is_head0()
    for hh in range(HEADS_PER_BLOCK):
        in_head = row_h0 if hh == 0 else jnp.logical_not(row_h0)
        qt_ref[hh] = jnp.where(in_head, q_t, 0.0).astype(BF16)
    refs = (k_ref, vt_ref, qt_ref, carry_ref, acc_ref)
    worst = lax.cond(qi > 0,
                     lambda: _sb_blocks([(qi, True), (qi - 1, False)], *refs),
                     lambda: _sb_blocks([(qi, True)], *refs))

    def cond(c):
        j, live = c
        return (j >= 0) & live

    def body(c):
        j, _ = c
        return j - 1, _sb_blocks([(j, False)], *refs) > SB_EXIT

    lax.while_loop(cond, body, (qi - 2, worst > SB_EXIT))

    o = jnp.concatenate([acc_ref[0], acc_ref[1]], axis=0)
    o_ref[...] = o.T.astype(o_ref.dtype)


def _attn_specs(B, S, q_base, k_base, v_base):
    nq = S // ATT_T
    qspec = pl.BlockSpec((None, ATT_T, LANES), lambda b, h, i: (b, i, q_base + h))
    kspec = pl.BlockSpec((None, S, LANES), lambda b, h, i: (b, 0, k_base + h))
    vspec = pl.BlockSpec((None, S, LANES), lambda b, h, i: (b, 0, v_base + h))
    ospec = pl.BlockSpec((None, ATT_T, LANES), lambda b, h, i: (b, i, h))
    grid = (B, N_HEAD_BLOCKS, nq)
    params = pltpu.CompilerParams(
        dimension_semantics=("parallel", "parallel", "arbitrary"), vmem_limit_bytes=VMEM_LIMIT)
    return grid, qspec, kspec, vspec, ospec, params


_SMEM_SPEC = pl.BlockSpec(memory_space=pltpu.SMEM)


def _moba(qkv, tab):
    B, S, _ = qkv.shape
    grid, qspec, kspec, vspec, ospec, params = _attn_specs(B, S, 0, 3, 6)
    n_blocks = S // MOBA_BLOCK
    nbp = -(-n_blocks // SUBLANES) * SUBLANES
    return pl.pallas_call(
        functools.partial(_moba_kernel, n_blocks=n_blocks, scale=HEAD_DIM ** -0.5),
        grid=grid,
        in_specs=[_SMEM_SPEC, qspec, kspec, vspec],
        out_specs=ospec,
        out_shape=jax.ShapeDtypeStruct((B, S, MIX_W), BF16),
        scratch_shapes=[
            pltpu.VMEM((HEADS_PER_BLOCK, S, LANES), BF16),
            pltpu.VMEM((HEADS_PER_BLOCK, V_ROWS, S), BF16),
            pltpu.VMEM((HEADS_PER_BLOCK, 1, 1), F32),
            pltpu.VMEM((nbp, LANES), F32),
            pltpu.VMEM((HEADS_PER_BLOCK, nbp, ATT_T), F32),
            pltpu.VMEM((HEADS_PER_BLOCK, LANES, ATT_T), BF16),
            pltpu.VMEM((HEADS_PER_BLOCK, 1, ATT_T), F32),
            pltpu.VMEM((HEADS_PER_BLOCK, V_ROWS, ATT_T), F32),
            pltpu.VMEM((HEADS_PER_BLOCK * STAGE_DEPTH, ATT_T, ATT_T), F32),
        ],
        compiler_params=params,
        name="moba_attn",
    )(tab, qkv, qkv, qkv)


def _diff(qkv, tab, diff_lambda, subln_g, lambda_init):
    B, S, _ = qkv.shape
    grid, qspec, kspec, vspec, ospec, params = _attn_specs(B, S, 9, 12, 15)
    n_chains = 2 * HEADS_PER_BLOCK
    return pl.pallas_call(
        functools.partial(_diff_kernel, n_blocks=S // ATT_T, scale=DIFF_QK_DIM ** -0.5,
                          lambda_init=lambda_init),
        grid=grid,
        in_specs=[
            _SMEM_SPEC,
            pl.BlockSpec(diff_lambda.shape, lambda b, h, i: (0, 0)),
            pl.BlockSpec((1, LANES), lambda b, h, i: (0, 0)),
            qspec, kspec, vspec,
        ],
        out_specs=ospec,
        out_shape=jax.ShapeDtypeStruct((B, S, MIX_W), BF16),
        scratch_shapes=[
            pltpu.VMEM((HEADS_PER_BLOCK, S, LANES), BF16),
            pltpu.VMEM((HEADS_PER_BLOCK, V_ROWS, S), BF16),
            pltpu.VMEM((HEADS_PER_BLOCK, 1, 1), F32),
            pltpu.VMEM((n_chains, LANES, ATT_T), BF16),
            pltpu.VMEM((n_chains, 1, ATT_T), F32),
            pltpu.VMEM((n_chains, V_ROWS, ATT_T), F32),
            pltpu.VMEM((n_chains * STAGE_DEPTH, ATT_T, ATT_T), F32),
        ],
        compiler_params=params,
        name="diff_attn",
    )(tab, diff_lambda, subln_g, qkv, qkv, qkv)


def _sb(qkv):
    B, S, _ = qkv.shape
    grid, qspec, kspec, vspec, ospec, params = _attn_specs(B, S, 18, 21, 24)
    return pl.pallas_call(
        functools.partial(_sb_kernel, n_blocks=S // ATT_T, scale=HEAD_DIM ** -0.5),
        grid=grid,
        in_specs=[qspec, kspec, vspec],
        out_specs=ospec,
        out_shape=jax.ShapeDtypeStruct((B, S, MIX_W), BF16),
        scratch_shapes=[
            pltpu.VMEM((HEADS_PER_BLOCK, HEAD_DIM, S), BF16),
            pltpu.VMEM((HEADS_PER_BLOCK, LANES, ATT_T), BF16),
            pltpu.VMEM((HEADS_PER_BLOCK, 1, ATT_T), F32),
            pltpu.VMEM((HEADS_PER_BLOCK, HEAD_DIM, ATT_T), F32),
        ],
        compiler_params=params,
        name="sb_attn",
    )(qkv, qkv, qkv)


def _merge_ln_kernel(x_ref, om_ref, od_ref, os_ref, wg_ref, bg_ref, wbr_ref, wo_ref, lg_ref, lb_ref,
                     o_ref, *, alpha, d_model):
    x = x_ref[...]
    xb = x.astype(BF16)
    merged = None
    for b, ob_ref in enumerate((om_ref, od_ref, os_ref)):
        glogit = _dot(xb, wg_ref[:, b * d_model:(b + 1) * d_model]) + bg_ref[b:b + 1, :]
        term = jax.nn.sigmoid(glogit) * _dot(ob_ref[...], wbr_ref[b])
        merged = term if merged is None else merged + term
    y = alpha * x + _dot(merged.astype(BF16), wo_ref[...])
    o_ref[...] = _layer_norm(y, lg_ref[...], lb_ref[...])


def _merge_ln(x, o_m, o_d, o_s, w_gate, b_gate, w_br, w_out, lg, lb, l, alpha, tm=512):
    T, D = x.shape
    return pl.pallas_call(
        functools.partial(_merge_ln_kernel, alpha=alpha, d_model=D),
        grid=(T // tm,),
        in_specs=[
            pl.BlockSpec((tm, D), lambda i: (i, 0)),
            pl.BlockSpec((tm, MIX_W), lambda i: (i, 0)),
            pl.BlockSpec((tm, MIX_W), lambda i: (i, 0)),
            pl.BlockSpec((tm, MIX_W), lambda i: (i, 0)),
            pl.BlockSpec((None, D, N_BRANCH * D), lambda i: (l, 0, 0), pipeline_mode=pl.Buffered(1)),
            pl.BlockSpec((None, N_BRANCH, D), lambda i: (l, 0, 0)),
            pl.BlockSpec((None, N_BRANCH, MIX_W, D), lambda i: (l, 0, 0, 0), pipeline_mode=pl.Buffered(1)),
            pl.BlockSpec((None, D, D), lambda i: (l, 0, 0), pipeline_mode=pl.Buffered(1)),
            pl.BlockSpec((1, D), lambda i: (0, 0)),
            pl.BlockSpec((1, D), lambda i: (0, 0)),
        ],
        out_specs=pl.BlockSpec((tm, D), lambda i: (i, 0)),
        out_shape=jax.ShapeDtypeStruct((T, D), F32),
        compiler_params=pltpu.CompilerParams(
            dimension_semantics=("parallel",), vmem_limit_bytes=VMEM_LIMIT),
        name="merge_ln",
    )(x, o_m, o_d, o_s, w_gate, b_gate, w_br, w_out, lg, lb)


def _alibi_slopes(n):
    return (2.0 ** (-8.0 * np.arange(1, n + 1, dtype=np.float32) / n)).astype(np.float32)


def _slope_table(slopes):
    s2 = slopes.astype(np.float64) * LOG2E
    return np.concatenate([s2, 1.0 / (s2 * ATT_T)]).astype(np.float32)


def kernel(x, ln_g, ln_b, ffn_w_gate, ffn_w_up, ffn_w_down, w_in, b_gate, diff_lambda, diff_subln_g,
           w_br_moba, w_br_diff, w_br_sb, w_out):
    B, S, D = x.shape
    depth = ln_g.shape[0]
    assert S % ATT_T == 0 and ATT_T == MOBA_BLOCK
    assert w_in.shape[-1] == QKV_W + N_BRANCH * D
    alpha = (2.0 * depth) ** 0.25

    wg = ffn_w_gate.astype(BF16)
    wu = ffn_w_up.astype(BF16)
    wd = ffn_w_down.astype(BF16)
    w_qkv = w_in[:, :, :QKV_W].astype(BF16)
    w_gate = w_in[:, :, QKV_W:].astype(BF16)
    w_br = jnp.stack([w_br_moba, w_br_diff, w_br_sb], axis=1).astype(BF16)
    w_out_b = w_out.astype(BF16)
    slopes = _alibi_slopes(2 * N_HEADS)
    tab_moba = jnp.asarray(_slope_table(slopes[0::2]))
    tab_diff = jnp.asarray(_slope_table(slopes[1::2]))
    subln_g = jnp.tile(diff_subln_g.astype(F32), (1, HEADS_PER_BLOCK))

    h = x.reshape(B * S, D)
    for l in range(depth):
        lg = ln_g[l][:, None, :]
        lb = ln_b[l][:, None, :]
        h = _ffn_ln(h, wg, wu, wd, lg[0], lb[0], l, 0, alpha)
        qkv = _qkv_proj(h, w_qkv, l).reshape(B, S, QKV_W)
        lambda_init = 0.8 - 0.6 * math.exp(-0.3 * l)
        o_m = _moba(qkv, tab_moba)
        o_d = _diff(qkv, tab_diff, diff_lambda[l], subln_g[l][None, :], lambda_init)
        o_s = _sb(qkv)
        T = B * S
        h = _merge_ln(h, o_m.reshape(T, MIX_W), o_d.reshape(T, MIX_W), o_s.reshape(T, MIX_W),
                      w_gate, b_gate, w_br, w_out_b, lg[1], lb[1], l, alpha)
        h = _ffn_ln(h, wg, wu, wd, lg[2], lb[2], l, 1, alpha)
    return h.reshape(B, S, D)
```

```python
import functools
import math

import numpy as np
import jax
import jax.numpy as jnp
from jax import lax
from jax.experimental import pallas as pl
from jax.experimental.pallas import tpu as pltpu

F32 = jnp.float32
BF16 = jnp.bfloat16

HEAD_DIM = 64
N_HEADS = 6
DIFF_QK_DIM = HEAD_DIM // 2
MOBA_BLOCK = 256
MOBA_TOPK = 3
N_BRANCH = 3
LN_EPS = 1e-5
SUBLN_EPS = 1e-5

LANES = 128
SUBLANES = 8
DENOM_ROWS = 16
V_ROWS = HEAD_DIM + DENOM_ROWS
HEADS_PER_BLOCK = LANES // HEAD_DIM
N_HEAD_BLOCKS = N_HEADS // HEADS_PER_BLOCK
MIX_W = N_HEADS * HEAD_DIM
QKV_W = 9 * MIX_W
ATT_T = 256
NEG = -1e30
SB_EXIT = -110.0
LOG2E = math.log2(math.e)
N_BIAS_COLS = 3
BOUND_SLACK = 1.01
MAX_EXPONENT = 100.0
SKIP_BITS = 150.0
ITEMS_PER_TRIP = 16
STAGE_DEPTH = 2
VMEM_LIMIT = 56 * 1024 * 1024


def _dot(a, b, precision=None):
    return jnp.dot(a, b, precision=precision, preferred_element_type=F32)


def _layer_norm(y, g, b):
    mu = jnp.mean(y, axis=-1, keepdims=True)
    yc = y - mu
    var = jnp.mean(yc * yc, axis=-1, keepdims=True)
    return yc * lax.rsqrt(var + LN_EPS) * g + b


def _ffn_ln_kernel(x_ref, wg_ref, wu_ref, wd_ref, lg_ref, lb_ref, o_ref, hid_ref, *, alpha, tf):
    x = x_ref[...]
    xb = x.astype(BF16)
    for c in range(hid_ref.shape[1] // tf):
        cols = slice(c * tf, (c + 1) * tf)
        gate = _dot(xb, wg_ref[:, cols])
        up = _dot(xb, wu_ref[:, cols])
        hid_ref[:, cols] = (gate * jax.nn.sigmoid(gate) * up).astype(BF16)
    y = alpha * x + 0.5 * _dot(hid_ref[...], wd_ref[...])
    o_ref[...] = _layer_norm(y, lg_ref[...], lb_ref[...])


def _ffn_ln(x, wg, wu, wd, lg, lb, l, j, alpha, tm=512, tf=256):
    T, D = x.shape
    FF = wg.shape[-1]
    resident = pl.Buffered(1)
    return pl.pallas_call(
        functools.partial(_ffn_ln_kernel, alpha=alpha, tf=tf),
        grid=(T // tm,),
        in_specs=[
            pl.BlockSpec((tm, D), lambda i: (i, 0)),
            pl.BlockSpec((None, None, D, FF), lambda i: (l, j, 0, 0), pipeline_mode=resident),
            pl.BlockSpec((None, None, D, FF), lambda i: (l, j, 0, 0), pipeline_mode=resident),
            pl.BlockSpec((None, None, FF, D), lambda i: (l, j, 0, 0), pipeline_mode=resident),
            pl.BlockSpec((1, D), lambda i: (0, 0)),
            pl.BlockSpec((1, D), lambda i: (0, 0)),
        ],
        out_specs=pl.BlockSpec((tm, D), lambda i: (i, 0)),
        out_shape=jax.ShapeDtypeStruct((T, D), F32),
        scratch_shapes=[pltpu.VMEM((tm, FF), BF16)],
        compiler_params=pltpu.CompilerParams(
            dimension_semantics=("parallel",), vmem_limit_bytes=VMEM_LIMIT),
        name="ffn_ln",
    )(x, wg, wu, wd, lg, lb)


def _proj_kernel(x_ref, w_ref, o_ref, *, tn):
    xb = x_ref[...].astype(BF16)
    for c in range(o_ref.shape[1] // tn):
        cols = slice(c * tn, (c + 1) * tn)
        o_ref[:, cols] = _dot(xb, w_ref[:, cols]).astype(o_ref.dtype)


def _qkv_proj(x, w_in, l, tm=512, tn=1152):
    T, D = x.shape
    return pl.pallas_call(
        functools.partial(_proj_kernel, tn=tn),
        grid=(T // tm,),
        in_specs=[
            pl.BlockSpec((tm, D), lambda i: (i, 0)),
            pl.BlockSpec((None, D, QKV_W), lambda i: (l, 0, 0), pipeline_mode=pl.Buffered(1)),
        ],
        out_specs=pl.BlockSpec((tm, QKV_W), lambda i: (i, 0)),
        out_shape=jax.ShapeDtypeStruct((T, QKV_W), BF16),
        compiler_params=pltpu.CompilerParams(
            dimension_semantics=("parallel",), vmem_limit_bytes=VMEM_LIMIT),
        name="qkv_proj",
    )(x, w_in)


def _lane_iota():
    return lax.broadcasted_iota(jnp.int32, (1, LANES), 1)


def _head_lanes(hh):
    lane = _lane_iota()
    return (lane >= hh * HEAD_DIM) & (lane < (hh + 1) * HEAD_DIM)


def _bias_lanes(hh):
    lane = _lane_iota()
    lo = (1 - hh) * HEAD_DIM
    return (lane >= lo) & (lane < lo + N_BIAS_COLS)


def _blk(j):
    return pl.ds(pl.multiple_of(j * ATT_T, ATT_T), ATT_T)


def _key_query_iotas():
    krow = lax.broadcasted_iota(jnp.int32, (ATT_T, ATT_T), 0)
    qcol = lax.broadcasted_iota(jnp.int32, (ATT_T, ATT_T), 1)
    return krow, qcol


def _build_softmax_operands(tab_ref, hb, k_ref, v_ref, kp_ref, vt_ref, kmax_ref, *, n_blocks, scale):
    lane = _lane_iota()
    pos = lax.broadcasted_iota(jnp.int32, (ATT_T, LANES), 0).astype(F32)
    bias_cols = []
    for hh in range(HEADS_PER_BLOCK):
        b = pos * tab_ref[hb * HEADS_PER_BLOCK + hh]
        lo = (1 - hh) * HEAD_DIM
        cols = jnp.zeros_like(b)
        for i in range(N_BIAS_COLS):
            piece = b.astype(BF16).astype(F32)
            cols = jnp.where(lane == lo + i, piece, cols)
            b = b - piece
        bias_cols.append(cols)
    kmax_ref[...] = jnp.zeros_like(kmax_ref)

    def body(n, c):
        kf = k_ref[_blk(n), :].astype(F32) * (scale * LOG2E)
        v_t = v_ref[_blk(n), :].astype(F32).T.astype(BF16)
        for hh in range(HEADS_PER_BLOCK):
            hm = _head_lanes(hh)
            kp_ref[hh, _blk(n), :] = jnp.where(hm, kf, bias_cols[hh]).astype(BF16)
            vt_ref[hh, 0:HEAD_DIM, _blk(n)] = v_t[hh * HEAD_DIM:(hh + 1) * HEAD_DIM, :]
            vt_ref[hh, HEAD_DIM:V_ROWS, _blk(n)] = jnp.ones((DENOM_ROWS, ATT_T), BF16)
            norm2 = jnp.sum(jnp.where(hm, kf * kf, 0.0), axis=1, keepdims=True)
            kmax_ref[hh] = jnp.maximum(kmax_ref[hh], jnp.max(norm2, axis=0, keepdims=True))
        return c

    lax.fori_loop(0, n_blocks, body, 0)


def _chain_qt(c, hh, seg_lo, seg_width, q_t, qt_ref):
    row = lax.broadcasted_iota(jnp.int32, (LANES, ATT_T), 0)
    lo = (1 - hh) * HEAD_DIM
    qsel = jnp.where((row >= seg_lo) & (row < seg_lo + seg_width), q_t, 0.0)
    qt_ref[c] = jnp.where((row >= lo) & (row < lo + N_BIAS_COLS), 1.0, qsel).astype(BF16)
    return qsel, jnp.sum(qsel * qsel, axis=0, keepdims=True)


def _stage_scores(c, hh, n, slot, kp_ref, qt_ref, sbuf_ref):
    sbuf_ref[c * STAGE_DEPTH + slot] = _dot(kp_ref[hh, _blk(jnp.maximum(n, 0)), :], qt_ref[c])


def _diag_blocks(chains, qi, qn2, slope2, kp_ref, vt_ref, kmax_ref, qt_ref, ref_ref, acc_ref, sbuf_ref):
    krow, qcol = _key_query_iotas()
    causal = krow <= qcol
    r = lax.broadcasted_iota(jnp.int32, (1, ATT_T), 1).astype(F32)
    scores = [_dot(kp_ref[hh, _blk(qi), :], qt_ref[c]) for c, hh in enumerate(chains)]
    for slot in range(STAGE_DEPTH):
        for c, hh in enumerate(chains):
            _stage_scores(c, hh, qi - 1 - slot, slot, kp_ref, qt_ref, sbuf_ref)
    gaps = []
    for c, hh in enumerate(chains):
        s = jnp.where(causal, scores[c], NEG)
        m = jnp.max(s, axis=0, keepdims=True)
        bound = jnp.sqrt(qn2[c] * kmax_ref[hh]) * BOUND_SLACK + slope2[hh] * r
        ref_ref[c] = m
        acc_ref[c] = _dot(vt_ref[hh, :, _blk(qi)], jnp.exp2(s - m).astype(BF16))
        gaps.append(jnp.max(bound - m))
    return gaps


def _first_block(gap, inv_blk, qi):
    reach = jnp.minimum((gap + SKIP_BITS) * inv_blk, 1e6).astype(jnp.int32)
    return jnp.maximum(qi - 1 - reach, 0)


def _fixed_ref_step(c, hh, n, slot, off, kp_ref, vt_ref, qt_ref, ref_ref, acc_ref, sbuf_ref):
    p = jnp.exp2(sbuf_ref[c * STAGE_DEPTH + slot] - (ref_ref[c] - off)).astype(BF16)
    _stage_scores(c, hh, n - STAGE_DEPTH, slot, kp_ref, qt_ref, sbuf_ref)
    acc_ref[c] += _dot(vt_ref[hh, :, _blk(jnp.maximum(n, 0))], p)


def _online_step(c, hh, n, off, kp_ref, vt_ref, qt_ref, m_ref, acc_ref):
    s = _dot(kp_ref[hh, _blk(n), :], qt_ref[c])
    m_old = m_ref[c]
    m_new = jnp.maximum(m_old, jnp.max(s, axis=0, keepdims=True) + off)
    p = jnp.exp2(s - (m_new - off)).astype(BF16)
    acc_ref[c] = acc_ref[c] * jnp.exp2(m_old - m_new) + _dot(vt_ref[hh, :, _blk(n)], p)
    m_ref[c] = m_new


def _softmax_sweep(qi, chains, gaps, first, off_fn, kp_ref, vt_ref, qt_ref, ref_ref, acc_ref, sbuf_ref,
                   split_heads):
    refs = (kp_ref, vt_ref, qt_ref, ref_ref, acc_ref)
    assert STAGE_DEPTH == 2
    widest = gaps[0]
    for g in gaps[1:]:
        widest = jnp.maximum(widest, g)
    fixed_ok = widest <= MAX_EXPONENT

    def run_fixed(top, left, heads):
        active = [(c, hh) for c, hh in enumerate(chains) if hh in heads]
        per_trip = max(STAGE_DEPTH, ITEMS_PER_TRIP // len(active))
        assert per_trip % STAGE_DEPTH == 0

        def sweep(top, trips, per_trip):
            def body(i, carry):
                for j in range(per_trip):
                    n = top - i * per_trip - j
                    off_end = jnp.where(n < 0, NEG, 0.0)
                    for c, hh in active:
                        off = off_fn(hh, jnp.maximum(n, 0)) + off_end
                        _fixed_ref_step(c, hh, n, j % STAGE_DEPTH, off, *refs, sbuf_ref)
                return carry

            lax.fori_loop(0, trips, body, 0)

        whole = left // per_trip
        sweep(top, whole, per_trip)
        if per_trip > STAGE_DEPTH:
            rest = left - whole * per_trip
            sweep(top - whole * per_trip, (rest + STAGE_DEPTH - 1) // STAGE_DEPTH, STAGE_DEPTH)
        else:
            sweep(top - whole * per_trip, left - whole * per_trip, 1)

    @pl.when(fixed_ok)
    def _fixed_reference():
        far = jnp.minimum(first[0], first[1])
        if split_heads:
            near = jnp.maximum(first[0], first[1])
            together = jnp.minimum((qi - near + 1) // 2 * 2, qi - far)
            run_fixed(qi - 1, together, (0, 1))
            for hh in range(HEADS_PER_BLOCK):
                run_fixed(qi - 1 - together, jnp.maximum(qi - together - first[hh], 0), (hh,))
        else:
            run_fixed(qi - 1, qi - far, (0, 1))

    @pl.when(jnp.logical_not(fixed_ok))
    def _running_maximum():
        def body(n, carry):
            for c, hh in enumerate(chains):
                _online_step(c, hh, n, off_fn(hh, n), *refs)
            return carry

        lax.fori_loop(0, qi, body, 0)


def _normalized(acc):
    return acc[0:HEAD_DIM, :] * (1.0 / acc[HEAD_DIM:HEAD_DIM + 1, :])


def _row_is_head0():
    return lax.broadcasted_iota(jnp.int32, (LANES, ATT_T), 0) < HEAD_DIM


def _moba_kernel(tab_ref, q_ref, k_ref, v_ref, o_ref, kp_ref, vt_ref, kmax_ref, kmean_ref, selm_ref,
                 qt_ref, ref_ref, acc_ref, sbuf_ref, *, n_blocks, scale):
    hb = pl.program_id(1)
    qi = pl.program_id(2)
    nbp = kmean_ref.shape[0]

    @pl.when(qi == 0)
    def _first_tile():
        _build_softmax_operands(tab_ref, hb, k_ref, v_ref, kp_ref, vt_ref, kmax_ref,
                                n_blocks=n_blocks, scale=scale)
        kmean_ref[...] = jnp.zeros_like(kmean_ref)

        def body(n, c):
            kmean_ref[pl.ds(n, 1), :] = jnp.mean(k_ref[_blk(n), :].astype(F32), axis=0, keepdims=True)
            return c

        lax.fori_loop(0, n_blocks, body, 0)

    q_t = q_ref[...].astype(F32).T
    blk_id = lax.broadcasted_iota(jnp.int32, (nbp, ATT_T), 0)
    blk_f = blk_id.astype(F32)
    chains = list(range(HEADS_PER_BLOCK))
    slope2 = [tab_ref[hb * HEADS_PER_BLOCK + hh] for hh in chains]
    qn2 = []
    for hh in chains:
        hm = _head_lanes(hh)
        qsel, norm2 = _chain_qt(hh, hh, hh * HEAD_DIM, HEAD_DIM, q_t, qt_ref)
        qn2.append(norm2)

        gate = _dot(jnp.where(hm, kmean_ref[...], 0.0), qsel, precision=lax.Precision.HIGHEST)
        g = jnp.where(blk_id < qi, gate, -jnp.inf)
        sel = jnp.zeros(g.shape, dtype=jnp.bool_)
        for _ in range(MOBA_TOPK):
            mx = jnp.max(g, axis=0, keepdims=True)
            pick = jnp.min(jnp.where(g == mx, blk_f, float(nbp)), axis=0, keepdims=True)
            hit = blk_f == pick
            sel = sel | (hit & (mx > -jnp.inf))
            g = jnp.where(hit, -jnp.inf, g)
        selm_ref[hh] = jnp.where(sel, 0.0, NEG)

    gaps = _diag_blocks(chains, qi, qn2, slope2, kp_ref, vt_ref, kmax_ref, qt_ref, ref_ref, acc_ref, sbuf_ref)
    first = [_first_block(gaps[hh], tab_ref[N_HEADS + hb * HEADS_PER_BLOCK + hh], qi) for hh in chains]

    def off_fn(hh, n):
        return slope2[hh] * ((n - qi) * ATT_T).astype(F32) + selm_ref[hh, pl.ds(n, 1), :]

    _softmax_sweep(qi, chains, gaps, first, off_fn, kp_ref, vt_ref, qt_ref, ref_ref, acc_ref, sbuf_ref,
                   split_heads=False)

    o = jnp.concatenate([_normalized(acc_ref[0]), _normalized(acc_ref[1])], axis=0)
    o_ref[...] = o.T.astype(o_ref.dtype)


def _diff_kernel(tab_ref, lam_ref, g_ref, q_ref, k_ref, v_ref, o_ref, kp_ref, vt_ref, kmax_ref, qt_ref,
                 ref_ref, acc_ref, sbuf_ref, *, n_blocks, scale, lambda_init):
    hb = pl.program_id(1)
    qi = pl.program_id(2)

    @pl.when(qi == 0)
    def _first_tile():
        _build_softmax_operands(tab_ref, hb, k_ref, v_ref, kp_ref, vt_ref, kmax_ref,
                                n_blocks=n_blocks, scale=scale)

    lf = lam_ref[...].astype(F32)
    lam = (jnp.exp(jnp.sum(lf[0:1] * lf[1:2], axis=-1, keepdims=True))
           - jnp.exp(jnp.sum(lf[2:3] * lf[3:4], axis=-1, keepdims=True)) + lambda_init)

    q_t = q_ref[...].astype(F32).T
    chains = [hh for hh in range(HEADS_PER_BLOCK) for _ in range(2)]
    slope2 = [tab_ref[hb * HEADS_PER_BLOCK + hh] for hh in range(HEADS_PER_BLOCK)]
    qn2 = []
    for c, hh in enumerate(chains):
        qn2.append(_chain_qt(c, hh, hh * HEAD_DIM + (c % 2) * DIFF_QK_DIM, DIFF_QK_DIM, q_t, qt_ref)[1])
    gaps = _diag_blocks(chains, qi, qn2, slope2, kp_ref, vt_ref, kmax_ref, qt_ref, ref_ref, acc_ref, sbuf_ref)
    first = [None] * HEADS_PER_BLOCK
    for c, hh in enumerate(chains):
        fb = _first_block(gaps[c], tab_ref[N_HEADS + hb * HEADS_PER_BLOCK + hh], qi)
        first[hh] = fb if first[hh] is None else jnp.minimum(first[hh], fb)

    def off_fn(hh, n):
        return slope2[hh] * ((n - qi) * ATT_T).astype(F32)

    _softmax_sweep(qi, chains, gaps, first, off_fn, kp_ref, vt_ref, qt_ref, ref_ref, acc_ref, sbuf_ref,
                   split_heads=True)

    outs = []
    for hh in range(HEADS_PER_BLOCK):
        o = _normalized(acc_ref[2 * hh]) - lam * _normalized(acc_ref[2 * hh + 1])
        ms = jnp.mean(o * o, axis=0, keepdims=True)
        outs.append(o * lax.rsqrt(ms + SUBLN_EPS))
    o = jnp.concatenate(outs, axis=0).T * (g_ref[...] * (1.0 - lambda_init))
    o_ref[...] = o.astype(o_ref.dtype)


def _log_sigmoids(z):
    l = jnp.log(1.0 + jnp.exp(-jnp.abs(z)))
    return jnp.minimum(-z, 0.0) - l, jnp.minimum(z, 0.0) - l


def _suffix_sums(upper, lg):
    hi = lg.astype(BF16)
    lo = (lg - hi.astype(F32)).astype(BF16)
    return _dot(upper, hi) + _dot(upper, lo)


def _sb_blocks(blocks, k_ref, vt_ref, qt_ref, carry_ref, acc_ref):
    krow, qcol = _key_query_iotas()
    before = krow < qcol
    upper = (qcol > krow).astype(BF16)
    heads = range(HEADS_PER_BLOCK)
    items = [(n, diag, hh) for n, diag in blocks for hh in heads]
    z = [_dot(k_ref[_blk(n), :], qt_ref[hh]) for n, _, hh in items]
    log_sig, log_surv, tails = [], [], []
    for i, (n, diag, hh) in enumerate(items):
        lneg, lpos = _log_sigmoids(z[i])
        if diag:
            lneg = jnp.where(before, lneg, 0.0)
        log_surv.append(lneg)
        log_sig.append(lpos)
        tails.append(_suffix_sums(upper, lneg))
    starts_fresh = blocks[0][1]
    carry = [None if starts_fresh else carry_ref[hh] for hh in heads]
    acc = [None if starts_fresh else acc_ref[hh] for hh in heads]
    for i, (n, diag, hh) in enumerate(items):
        if diag:
            w = jnp.where(before, jnp.exp(log_sig[i] + tails[i]), 0.0)
        else:
            w = jnp.exp(log_sig[i] + (tails[i] + carry[hh]))
        part = _dot(vt_ref[hh, :, _blk(n)], w.astype(BF16))
        acc[hh] = part if acc[hh] is None else acc[hh] + part
        total = jnp.sum(log_surv[i], axis=0, keepdims=True)
        carry[hh] = total if carry[hh] is None else carry[hh] + total
    worst = None
    for hh in heads:
        acc_ref[hh] = acc[hh]
        carry_ref[hh] = carry[hh]
        top = jnp.max(carry[hh])
        worst = top if worst is None else jnp.maximum(worst, top)
    return worst


def _sb_kernel(q_ref, k_ref, v_ref, o_ref, vt_ref, qt_ref, carry_ref, acc_ref, *, n_blocks, scale):
    qi = pl.program_id(2)

    @pl.when(qi == 0)
    def _first_tile():
        def body(n, c):
            v_t = v_ref[_blk(n), :].astype(F32).T.astype(BF16)
            for hh in range(HEADS_PER_BLOCK):
                vt_ref[hh, :, _blk(n)] = v_t[hh * HEAD_DIM:(hh + 1) * HEAD_DIM, :]
            return c

        lax.fori_loop(0, n_blocks, body, 0)

    q_t = (q_ref[...].astype(F32) * scale).T
    row_h0 = _row_is_head0()
    for hh in range(HEADS_PER_BLOCK):
        in_head = row_h0 if hh == 0 else jnp.logical_not(row_h0)
        qt_ref[hh] = jnp.where(in_head, q_t, 0.0).astype(BF16)
    refs = (k_ref, vt_ref, qt_ref, carry_ref, acc_ref)
    worst = lax.cond(qi > 0,
                     lambda: _sb_blocks([(qi, True), (qi - 1, False)], *refs),
                     lambda: _sb_blocks([(qi, True)], *refs))

    def cond(c):
        j, live = c
        return (j >= 0) & live

    def body(c):
        j, _ = c
        return j - 1, _sb_blocks([(j, False)], *refs) > SB_EXIT

    lax.while_loop(cond, body, (qi - 2, worst > SB_EXIT))

    o = jnp.concatenate([acc_ref[0], acc_ref[1]], axis=0)
    o_ref[...] = o.T.astype(o_ref.dtype)


def _attn_specs(B, S, q_base, k_base, v_base):
    nq = S // ATT_T
    qspec = pl.BlockSpec((None, ATT_T, LANES), lambda b, h, i: (b, i, q_base + h))
    kspec = pl.BlockSpec((None, S, LANES), lambda b, h, i: (b, 0, k_base + h))
    vspec = pl.BlockSpec((None, S, LANES), lambda b, h, i: (b, 0, v_base + h))
    ospec = pl.BlockSpec((None, ATT_T, LANES), lambda b, h, i: (b, i, h))
    grid = (B, N_HEAD_BLOCKS, nq)
    params = pltpu.CompilerParams(
        dimension_semantics=("parallel", "parallel", "arbitrary"), vmem_limit_bytes=VMEM_LIMIT)
    return grid, qspec, kspec, vspec, ospec, params


_SMEM_SPEC = pl.BlockSpec(memory_space=pltpu.SMEM)


def _moba(qkv, tab):
    B, S, _ = qkv.shape
    grid, qspec, kspec, vspec, ospec, params = _attn_specs(B, S, 0, 3, 6)
    n_blocks = S // MOBA_BLOCK
    nbp = -(-n_blocks // SUBLANES) * SUBLANES
    return pl.pallas_call(
        functools.partial(_moba_kernel, n_blocks=n_blocks, scale=HEAD_DIM ** -0.5),
        grid=grid,
        in_specs=[_SMEM_SPEC, qspec, kspec, vspec],
        out_specs=ospec,
        out_shape=jax.ShapeDtypeStruct((B, S, MIX_W), BF16),
        scratch_shapes=[
            pltpu.VMEM((HEADS_PER_BLOCK, S, LANES), BF16),
            pltpu.VMEM((HEADS_PER_BLOCK, V_ROWS, S), BF16),
            pltpu.VMEM((HEADS_PER_BLOCK, 1, 1), F32),
            pltpu.VMEM((nbp, LANES), F32),
            pltpu.VMEM((HEADS_PER_BLOCK, nbp, ATT_T), F32),
            pltpu.VMEM((HEADS_PER_BLOCK, LANES, ATT_T), BF16),
            pltpu.VMEM((HEADS_PER_BLOCK, 1, ATT_T), F32),
            pltpu.VMEM((HEADS_PER_BLOCK, V_ROWS, ATT_T), F32),
            pltpu.VMEM((HEADS_PER_BLOCK * STAGE_DEPTH, ATT_T, ATT_T), F32),
        ],
        compiler_params=params,
        name="moba_attn",
    )(tab, qkv, qkv, qkv)


def _diff(qkv, tab, diff_lambda, subln_g, lambda_init):
    B, S, _ = qkv.shape
    grid, qspec, kspec, vspec, ospec, params = _attn_specs(B, S, 9, 12, 15)
    n_chains = 2 * HEADS_PER_BLOCK
    return pl.pallas_call(
        functools.partial(_diff_kernel, n_blocks=S // ATT_T, scale=DIFF_QK_DIM ** -0.5,
                          lambda_init=lambda_init),
        grid=grid,
        in_specs=[
            _SMEM_SPEC,
            pl.BlockSpec(diff_lambda.shape, lambda b, h, i: (0, 0)),
            pl.BlockSpec((1, LANES), lambda b, h, i: (0, 0)),
            qspec, kspec, vspec,
        ],
        out_specs=ospec,
        out_shape=jax.ShapeDtypeStruct((B, S, MIX_W), BF16),
        scratch_shapes=[
            pltpu.VMEM((HEADS_PER_BLOCK, S, LANES), BF16),
            pltpu.VMEM((HEADS_PER_BLOCK, V_ROWS, S), BF16),
            pltpu.VMEM((HEADS_PER_BLOCK, 1, 1), F32),
            pltpu.VMEM((n_chains, LANES, ATT_T), BF16),
            pltpu.VMEM((n_chains, 1, ATT_T), F32),
            pltpu.VMEM((n_chains, V_ROWS, ATT_T), F32),
            pltpu.VMEM((n_chains * STAGE_DEPTH, ATT_T, ATT_T), F32),
        ],
        compiler_params=params,
        name="diff_attn",
    )(tab, diff_lambda, subln_g, qkv, qkv, qkv)


def _sb(qkv):
    B, S, _ = qkv.shape
    grid, qspec, kspec, vspec, ospec, params = _attn_specs(B, S, 18, 21, 24)
    return pl.pallas_call(
        functools.partial(_sb_kernel, n_blocks=S // ATT_T, scale=HEAD_DIM ** -0.5),
        grid=grid,
        in_specs=[qspec, kspec, vspec],
        out_specs=ospec,
        out_shape=jax.ShapeDtypeStruct((B, S, MIX_W), BF16),
        scratch_shapes=[
            pltpu.VMEM((HEADS_PER_BLOCK, HEAD_DIM, S), BF16),
            pltpu.VMEM((HEADS_PER_BLOCK, LANES, ATT_T), BF16),
            pltpu.VMEM((HEADS_PER_BLOCK, 1, ATT_T), F32),
            pltpu.VMEM((HEADS_PER_BLOCK, HEAD_DIM, ATT_T), F32),
        ],
        compiler_params=params,
        name="sb_attn",
    )(qkv, qkv, qkv)


def _merge_ln_kernel(x_ref, om_ref, od_ref, os_ref, wg_ref, bg_ref, wbr_ref, wo_ref, lg_ref, lb_ref,
                     o_ref, *, alpha, d_model):
    x = x_ref[...]
    xb = x.astype(BF16)
    merged = None
    for b, ob_ref in enumerate((om_ref, od_ref, os_ref)):
        glogit = _dot(xb, wg_ref[:, b * d_model:(b + 1) * d_model]) + bg_ref[b:b + 1, :]
        term = jax.nn.sigmoid(glogit) * _dot(ob_ref[...], wbr_ref[b])
        merged = term if merged is None else merged + term
    y = alpha * x + _dot(merged.astype(BF16), wo_ref[...])
    o_ref[...] = _layer_norm(y, lg_ref[...], lb_ref[...])


def _merge_ln(x, o_m, o_d, o_s, w_gate, b_gate, w_br, w_out, lg, lb, l, alpha, tm=512):
    T, D = x.shape
    return pl.pallas_call(
        functools.partial(_merge_ln_kernel, alpha=alpha, d_model=D),
        grid=(T // tm,),
        in_specs=[
            pl.BlockSpec((tm, D), lambda i: (i, 0)),
            pl.BlockSpec((tm, MIX_W), lambda i: (i, 0)),
            pl.BlockSpec((tm, MIX_W), lambda i: (i, 0)),
            pl.BlockSpec((tm, MIX_W), lambda i: (i, 0)),
            pl.BlockSpec((None, D, N_BRANCH * D), lambda i: (l, 0, 0), pipeline_mode=pl.Buffered(1)),
            pl.BlockSpec((None, N_BRANCH, D), lambda i: (l, 0, 0)),
            pl.BlockSpec((None, N_BRANCH, MIX_W, D), lambda i: (l, 0, 0, 0), pipeline_mode=pl.Buffered(1)),
            pl.BlockSpec((None, D, D), lambda i: (l, 0, 0), pipeline_mode=pl.Buffered(1)),
            pl.BlockSpec((1, D), lambda i: (0, 0)),
            pl.BlockSpec((1, D), lambda i: (0, 0)),
        ],
        out_specs=pl.BlockSpec((tm, D), lambda i: (i, 0)),
        out_shape=jax.ShapeDtypeStruct((T, D), F32),
        compiler_params=pltpu.CompilerParams(
            dimension_semantics=("parallel",), vmem_limit_bytes=VMEM_LIMIT),
        name="merge_ln",
    )(x, o_m, o_d, o_s, w_gate, b_gate, w_br, w_out, lg, lb)


def _alibi_slopes(n):
    return (2.0 ** (-8.0 * np.arange(1, n + 1, dtype=np.float32) / n)).astype(np.float32)


def _slope_table(slopes):
    s2 = slopes.astype(np.float64) * LOG2E
    return np.concatenate([s2, 1.0 / (s2 * ATT_T)]).astype(np.float32)


def kernel(x, ln_g, ln_b, ffn_w_gate, ffn_w_up, ffn_w_down, w_in, b_gate, diff_lambda, diff_subln_g,
           w_br_moba, w_br_diff, w_br_sb, w_out):
    B, S, D = x.shape
    depth = ln_g.shape[0]
    assert S % ATT_T == 0 and ATT_T == MOBA_BLOCK
    assert w_in.shape[-1] == QKV_W + N_BRANCH * D
    alpha = (2.0 * depth) ** 0.25

    wg = ffn_w_gate.astype(BF16)
    wu = ffn_w_up.astype(BF16)
    wd = ffn_w_down.astype(BF16)
    w_qkv = w_in[:, :, :QKV_W].astype(BF16)
    w_gate = w_in[:, :, QKV_W:].astype(BF16)
    w_br = jnp.stack([w_br_moba, w_br_diff, w_br_sb], axis=1).astype(BF16)
    w_out_b = w_out.astype(BF16)
    slopes = _alibi_slopes(2 * N_HEADS)
    tab_moba = jnp.asarray(_slope_table(slopes[0::2]))
    tab_diff = jnp.asarray(_slope_table(slopes[1::2]))
    subln_g = jnp.tile(diff_subln_g.astype(F32), (1, HEADS_PER_BLOCK))

    h = x.reshape(B * S, D)
    for l in range(depth):
        lg = ln_g[l][:, None, :]
        lb = ln_b[l][:, None, :]
        h = _ffn_ln(h, wg, wu, wd, lg[0], lb[0], l, 0, alpha)
        qkv = _qkv_proj(h, w_qkv, l).reshape(B, S, QKV_W)
        lambda_init = 0.8 - 0.6 * math.exp(-0.3 * l)
        o_m = _moba(qkv, tab_moba)
        o_d = _diff(qkv, tab_diff, diff_lambda[l], subln_g[l][None, :], lambda_init)
        o_s = _sb(qkv)
        T = B * S
        h = _merge_ln(h, o_m.reshape(T, MIX_W), o_d.reshape(T, MIX_W), o_s.reshape(T, MIX_W),
                      w_gate, b_gate, w_br, w_out_b, lg[1], lb[1], l, alpha)
        h = _ffn_ln(h, wg, wu, wd, lg[2], lb[2], l, 1, alpha)
    return h.reshape(B, S, D)
```

```python
import functools
import math

import numpy as np
import jax
import jax.numpy as jnp
from jax import lax
from jax.experimental import pallas as pl
from jax.experimental.pallas import tpu as pltpu

F32 = jnp.float32
BF16 = jnp.bfloat16

HEAD_DIM = 64
N_HEADS = 6
DIFF_QK_DIM = HEAD_DIM // 2
MOBA_BLOCK = 256
MOBA_TOPK = 3
N_BRANCH = 3
LN_EPS = 1e-5
SUBLN_EPS = 1e-5

LANES = 128
SUBLANES = 8
DENOM_ROWS = 16
V_ROWS = HEAD_DIM + DENOM_ROWS
HEADS_PER_BLOCK = LANES // HEAD_DIM
N_HEAD_BLOCKS = N_HEADS // HEADS_PER_BLOCK
MIX_W = N_HEADS * HEAD_DIM
QKV_W = 9 * MIX_W
ATT_T = 256
NEG = -1e30
SB_EXIT = -110.0
LOG2E = math.log2(math.e)
N_BIAS_COLS = 3
BOUND_SLACK = 1.01
MAX_EXPONENT = 100.0
SKIP_BITS = 150.0
ITEMS_PER_TRIP = 16
Q_TILES_PER_STEP = 4
STAGE_DEPTH = 2
VMEM_LIMIT = 56 * 1024 * 1024


def _dot(a, b, precision=None):
    return jnp.dot(a, b, precision=precision, preferred_element_type=F32)


def _layer_norm(y, g, b):
    mu = jnp.mean(y, axis=-1, keepdims=True)
    yc = y - mu
    var = jnp.mean(yc * yc, axis=-1, keepdims=True)
    return yc * lax.rsqrt(var + LN_EPS) * g + b


def _ffn_ln_kernel(x_ref, wg_ref, wu_ref, wd_ref, lg_ref, lb_ref, o_ref, hid_ref, *, alpha, tf):
    x = x_ref[...]
    xb = x.astype(BF16)
    for c in range(hid_ref.shape[1] // tf):
        cols = slice(c * tf, (c + 1) * tf)
        gate = _dot(xb, wg_ref[:, cols])
        up = _dot(xb, wu_ref[:, cols])
        hid_ref[:, cols] = (gate * jax.nn.sigmoid(gate) * up).astype(BF16)
    y = alpha * x + 0.5 * _dot(hid_ref[...], wd_ref[...])
    o_ref[...] = _layer_norm(y, lg_ref[...], lb_ref[...])


def _ffn_ln(x, wg, wu, wd, lg, lb, l, j, alpha, tm=512, tf=256):
    T, D = x.shape
    FF = wg.shape[-1]
    resident = pl.Buffered(1)
    return pl.pallas_call(
        functools.partial(_ffn_ln_kernel, alpha=alpha, tf=tf),
        grid=(T // tm,),
        in_specs=[
            pl.BlockSpec((tm, D), lambda i: (i, 0)),
            pl.BlockSpec((None, None, D, FF), lambda i: (l, j, 0, 0), pipeline_mode=resident),
            pl.BlockSpec((None, None, D, FF), lambda i: (l, j, 0, 0), pipeline_mode=resident),
            pl.BlockSpec((None, None, FF, D), lambda i: (l, j, 0, 0), pipeline_mode=resident),
            pl.BlockSpec((1, D), lambda i: (0, 0)),
            pl.BlockSpec((1, D), lambda i: (0, 0)),
        ],
        out_specs=pl.BlockSpec((tm, D), lambda i: (i, 0)),
        out_shape=jax.ShapeDtypeStruct((T, D), F32),
        scratch_shapes=[pltpu.VMEM((tm, FF), BF16)],
        compiler_params=pltpu.CompilerParams(
            dimension_semantics=("parallel",), vmem_limit_bytes=VMEM_LIMIT),
        name="ffn_ln",
    )(x, wg, wu, wd, lg, lb)


def _proj_kernel(x_ref, w_ref, o_ref, *, tn):
    xb = x_ref[...].astype(BF16)
    for c in range(o_ref.shape[1] // tn):
        cols = slice(c * tn, (c + 1) * tn)
        o_ref[:, cols] = _dot(xb, w_ref[:, cols]).astype(o_ref.dtype)


def _qkv_proj(x, w_in, l, tm=512, tn=1152):
    T, D = x.shape
    return pl.pallas_call(
        functools.partial(_proj_kernel, tn=tn),
        grid=(T // tm,),
        in_specs=[
            pl.BlockSpec((tm, D), lambda i: (i, 0)),
            pl.BlockSpec((None, D, QKV_W), lambda i: (l, 0, 0), pipeline_mode=pl.Buffered(1)),
        ],
        out_specs=pl.BlockSpec((tm, QKV_W), lambda i: (i, 0)),
        out_shape=jax.ShapeDtypeStruct((T, QKV_W), BF16),
        compiler_params=pltpu.CompilerParams(
            dimension_semantics=("parallel",), vmem_limit_bytes=VMEM_LIMIT),
        name="qkv_proj",
    )(x, w_in)


def _lane_iota():
    return lax.broadcasted_iota(jnp.int32, (1, LANES), 1)


def _head_lanes(hh):
    lane = _lane_iota()
    return (lane >= hh * HEAD_DIM) & (lane < (hh + 1) * HEAD_DIM)


def _bias_lanes(hh):
    lane = _lane_iota()
    lo = (1 - hh) * HEAD_DIM
    return (lane >= lo) & (lane < lo + N_BIAS_COLS)


def _blk(j):
    return pl.ds(pl.multiple_of(j * ATT_T, ATT_T), ATT_T)


def _key_query_iotas():
    krow = lax.broadcasted_iota(jnp.int32, (ATT_T, ATT_T), 0)
    qcol = lax.broadcasted_iota(jnp.int32, (ATT_T, ATT_T), 1)
    return krow, qcol


def _build_softmax_operands(tab_ref, hb, k_ref, v_ref, kp_ref, vt_ref, kmax_ref, *, n_blocks, scale):
    lane = _lane_iota()
    pos = lax.broadcasted_iota(jnp.int32, (ATT_T, LANES), 0).astype(F32)
    bias_cols = []
    for hh in range(HEADS_PER_BLOCK):
        b = pos * tab_ref[hb * HEADS_PER_BLOCK + hh]
        lo = (1 - hh) * HEAD_DIM
        cols = jnp.zeros_like(b)
        for i in range(N_BIAS_COLS):
            piece = b.astype(BF16).astype(F32)
            cols = jnp.where(lane == lo + i, piece, cols)
            b = b - piece
        bias_cols.append(cols)
    kmax_ref[...] = jnp.zeros_like(kmax_ref)

    def body(n, c):
        kf = k_ref[_blk(n), :].astype(F32) * (scale * LOG2E)
        v_t = v_ref[_blk(n), :].astype(F32).T.astype(BF16)
        for hh in range(HEADS_PER_BLOCK):
            hm = _head_lanes(hh)
            kp_ref[hh, _blk(n), :] = jnp.where(hm, kf, bias_cols[hh]).astype(BF16)
            vt_ref[hh, 0:HEAD_DIM, _blk(n)] = v_t[hh * HEAD_DIM:(hh + 1) * HEAD_DIM, :]
            vt_ref[hh, HEAD_DIM:V_ROWS, _blk(n)] = jnp.ones((DENOM_ROWS, ATT_T), BF16)
            norm2 = jnp.sum(jnp.where(hm, kf * kf, 0.0), axis=1, keepdims=True)
            kmax_ref[hh] = jnp.maximum(kmax_ref[hh], jnp.max(norm2, axis=0, keepdims=True))
        return c

    lax.fori_loop(0, n_blocks, body, 0)


def _chain_qt(c, hh, seg_lo, seg_width, q_t, qt_ref):
    row = lax.broadcasted_iota(jnp.int32, (LANES, ATT_T), 0)
    lo = (1 - hh) * HEAD_DIM
    qsel = jnp.where((row >= seg_lo) & (row < seg_lo + seg_width), q_t, 0.0)
    qt_ref[c] = jnp.where((row >= lo) & (row < lo + N_BIAS_COLS), 1.0, qsel).astype(BF16)
    return qsel, jnp.sum(qsel * qsel, axis=0, keepdims=True)


def _stage_scores(c, hh, n, slot, kp_ref, qt_ref, sbuf_ref):
    sbuf_ref[c * STAGE_DEPTH + slot] = _dot(kp_ref[hh, _blk(jnp.maximum(n, 0)), :], qt_ref[c])


def _diag_blocks(chains, qi, qn2, slope2, kp_ref, vt_ref, kmax_ref, qt_ref, ref_ref, acc_ref, sbuf_ref):
    krow, qcol = _key_query_iotas()
    causal = krow <= qcol
    r = lax.broadcasted_iota(jnp.int32, (1, ATT_T), 1).astype(F32)
    scores = [_dot(kp_ref[hh, _blk(qi), :], qt_ref[c]) for c, hh in enumerate(chains)]
    for slot in range(STAGE_DEPTH):
        for c, hh in enumerate(chains):
            _stage_scores(c, hh, qi - 1 - slot, slot, kp_ref, qt_ref, sbuf_ref)
    gaps = []
    for c, hh in enumerate(chains):
        s = jnp.where(causal, scores[c], NEG)
        m = jnp.max(s, axis=0, keepdims=True)
        bound = jnp.sqrt(qn2[c] * kmax_ref[hh]) * BOUND_SLACK + slope2[hh] * r
        ref_ref[c] = m
        acc_ref[c] = _dot(vt_ref[hh, :, _blk(qi)], jnp.exp2(s - m).astype(BF16))
        gaps.append(jnp.max(bound - m))
    return gaps


def _first_block(gap, inv_blk, qi):
    reach = jnp.minimum((gap + SKIP_BITS) * inv_blk, 1e6).astype(jnp.int32)
    return jnp.maximum(qi - 1 - reach, 0)


def _fixed_ref_step(c, hh, n, slot, off, kp_ref, vt_ref, qt_ref, ref_ref, acc_ref, sbuf_ref):
    p = jnp.exp2(sbuf_ref[c * STAGE_DEPTH + slot] - (ref_ref[c] - off)).astype(BF16)
    _stage_scores(c, hh, n - STAGE_DEPTH, slot, kp_ref, qt_ref, sbuf_ref)
    acc_ref[c] += _dot(vt_ref[hh, :, _blk(jnp.maximum(n, 0))], p)


def _online_step(c, hh, n, off, kp_ref, vt_ref, qt_ref, m_ref, acc_ref):
    s = _dot(kp_ref[hh, _blk(n), :], qt_ref[c])
    m_old = m_ref[c]
    m_new = jnp.maximum(m_old, jnp.max(s, axis=0, keepdims=True) + off)
    p = jnp.exp2(s - (m_new - off)).astype(BF16)
    acc_ref[c] = acc_ref[c] * jnp.exp2(m_old - m_new) + _dot(vt_ref[hh, :, _blk(n)], p)
    m_ref[c] = m_new


def _softmax_sweep(qi, chains, gaps, first, off_fn, kp_ref, vt_ref, qt_ref, ref_ref, acc_ref, sbuf_ref,
                   split_heads):
    refs = (kp_ref, vt_ref, qt_ref, ref_ref, acc_ref)
    assert STAGE_DEPTH == 2
    widest = gaps[0]
    for g in gaps[1:]:
        widest = jnp.maximum(widest, g)
    fixed_ok = widest <= MAX_EXPONENT

    def run_fixed(top, left, heads):
        active = [(c, hh) for c, hh in enumerate(chains) if hh in heads]
        per_trip = max(STAGE_DEPTH, ITEMS_PER_TRIP // len(active))
        assert per_trip % STAGE_DEPTH == 0

        def sweep(top, trips, per_trip):
            def body(i, carry):
                for j in range(per_trip):
                    n = top - i * per_trip - j
                    off_end = jnp.where(n < 0, NEG, 0.0)
                    for c, hh in active:
                        off = off_fn(hh, jnp.maximum(n, 0)) + off_end
                        _fixed_ref_step(c, hh, n, j % STAGE_DEPTH, off, *refs, sbuf_ref)
                return carry

            lax.fori_loop(0, trips, body, 0)

        whole = left // per_trip
        sweep(top, whole, per_trip)
        if per_trip > STAGE_DEPTH:
            rest = left - whole * per_trip
            sweep(top - whole * per_trip, (rest + STAGE_DEPTH - 1) // STAGE_DEPTH, STAGE_DEPTH)
        else:
            sweep(top - whole * per_trip, left - whole * per_trip, 1)

    @pl.when(fixed_ok)
    def _fixed_reference():
        far = jnp.minimum(first[0], first[1])
        if split_heads:
            near = jnp.maximum(first[0], first[1])
            together = jnp.minimum((qi - near + 1) // 2 * 2, qi - far)
            run_fixed(qi - 1, together, (0, 1))
            for hh in range(HEADS_PER_BLOCK):
                run_fixed(qi - 1 - together, jnp.maximum(qi - together - first[hh], 0), (hh,))
        else:
            run_fixed(qi - 1, qi - far, (0, 1))

    @pl.when(jnp.logical_not(fixed_ok))
    def _running_maximum():
        def body(n, carry):
            for c, hh in enumerate(chains):
                _online_step(c, hh, n, off_fn(hh, n), *refs)
            return carry

        lax.fori_loop(0, qi, body, 0)


def _normalized(acc):
    return acc[0:HEAD_DIM, :] * (1.0 / acc[HEAD_DIM:HEAD_DIM + 1, :])


def _row_is_head0():
    return lax.broadcasted_iota(jnp.int32, (LANES, ATT_T), 0) < HEAD_DIM


def _moba_kernel(tab_ref, q_ref, k_ref, v_ref, o_ref, kp_ref, vt_ref, kmax_ref, kmean_ref, selm_ref,
                 qt_ref, ref_ref, acc_ref, sbuf_ref, *, n_blocks, scale):
    hb = pl.program_id(1)
    step = pl.program_id(2)
    nbp = kmean_ref.shape[0]

    @pl.when(step == 0)
    def _first_tile():
        _build_softmax_operands(tab_ref, hb, k_ref, v_ref, kp_ref, vt_ref, kmax_ref,
                                n_blocks=n_blocks, scale=scale)
        kmean_ref[...] = jnp.zeros_like(kmean_ref)

        def body(n, c):
            kmean_ref[pl.ds(n, 1), :] = jnp.mean(k_ref[_blk(n), :].astype(F32), axis=0, keepdims=True)
            return c

        lax.fori_loop(0, n_blocks, body, 0)

    def one_tile(i, carry):
        qi = step * Q_TILES_PER_STEP + i
        rows = pl.ds(pl.multiple_of(i * ATT_T, ATT_T), ATT_T)
        q_t = q_ref[rows, :].astype(F32).T
        blk_id = lax.broadcasted_iota(jnp.int32, (nbp, ATT_T), 0)
        blk_f = blk_id.astype(F32)
        chains = list(range(HEADS_PER_BLOCK))
        slope2 = [tab_ref[hb * HEADS_PER_BLOCK + hh] for hh in chains]
        qn2 = []
        for hh in chains:
            hm = _head_lanes(hh)
            qsel, norm2 = _chain_qt(hh, hh, hh * HEAD_DIM, HEAD_DIM, q_t, qt_ref)
            qn2.append(norm2)

            gate = _dot(jnp.where(hm, kmean_ref[...], 0.0), qsel, precision=lax.Precision.HIGHEST)
            g = jnp.where(blk_id < qi, gate, -jnp.inf)
            sel = jnp.zeros(g.shape, dtype=jnp.bool_)
            for _ in range(MOBA_TOPK):
                mx = jnp.max(g, axis=0, keepdims=True)
                pick = jnp.min(jnp.where(g == mx, blk_f, float(nbp)), axis=0, keepdims=True)
                hit = blk_f == pick
                sel = sel | (hit & (mx > -jnp.inf))
                g = jnp.where(hit, -jnp.inf, g)
            selm_ref[hh] = jnp.where(sel, 0.0, NEG)

        gaps = _diag_blocks(chains, qi, qn2, slope2, kp_ref, vt_ref, kmax_ref, qt_ref, ref_ref, acc_ref, sbuf_ref)
        first = [_first_block(gaps[hh], tab_ref[N_HEADS + hb * HEADS_PER_BLOCK + hh], qi) for hh in chains]

        def off_fn(hh, n):
            return slope2[hh] * ((n - qi) * ATT_T).astype(F32) + selm_ref[hh, pl.ds(n, 1), :]

        _softmax_sweep(qi, chains, gaps, first, off_fn, kp_ref, vt_ref, qt_ref, ref_ref, acc_ref, sbuf_ref,
                       split_heads=False)

        o = jnp.concatenate([_normalized(acc_ref[0]), _normalized(acc_ref[1])], axis=0)
        o_ref[rows, :] = o.T.astype(o_ref.dtype)
        return carry

    lax.fori_loop(0, Q_TILES_PER_STEP, one_tile, 0)


def _diff_kernel(tab_ref, lam_ref, g_ref, q_ref, k_ref, v_ref, o_ref, kp_ref, vt_ref, kmax_ref, qt_ref,
                 ref_ref, acc_ref, sbuf_ref, *, n_blocks, scale, lambda_init):
    hb = pl.program_id(1)
    step = pl.program_id(2)

    @pl.when(step == 0)
    def _first_tile():
        _build_softmax_operands(tab_ref, hb, k_ref, v_ref, kp_ref, vt_ref, kmax_ref,
                                n_blocks=n_blocks, scale=scale)

    lf = lam_ref[...].astype(F32)
    lam = (jnp.exp(jnp.sum(lf[0:1] * lf[1:2], axis=-1, keepdims=True))
           - jnp.exp(jnp.sum(lf[2:3] * lf[3:4], axis=-1, keepdims=True)) + lambda_init)

    def one_tile(i, carry):
        qi = step * Q_TILES_PER_STEP + i
        rows = pl.ds(pl.multiple_of(i * ATT_T, ATT_T), ATT_T)
        q_t = q_ref[rows, :].astype(F32).T
        chains = [hh for hh in range(HEADS_PER_BLOCK) for _ in range(2)]
        slope2 = [tab_ref[hb * HEADS_PER_BLOCK + hh] for hh in range(HEADS_PER_BLOCK)]
        qn2 = []
        for c, hh in enumerate(chains):
            qn2.append(_chain_qt(c, hh, hh * HEAD_DIM + (c % 2) * DIFF_QK_DIM, DIFF_QK_DIM, q_t, qt_ref)[1])
        gaps = _diag_blocks(chains, qi, qn2, slope2, kp_ref, vt_ref, kmax_ref, qt_ref, ref_ref, acc_ref, sbuf_ref)
        first = [None] * HEADS_PER_BLOCK
        for c, hh in enumerate(chains):
            fb = _first_block(gaps[c], tab_ref[N_HEADS + hb * HEADS_PER_BLOCK + hh], qi)
            first[hh] = fb if first[hh] is None else jnp.minimum(first[hh], fb)

        def off_fn(hh, n):
            return slope2[hh] * ((n - qi) * ATT_T).astype(F32)

        _softmax_sweep(qi, chains, gaps, first, off_fn, kp_ref, vt_ref, qt_ref, ref_ref, acc_ref, sbuf_ref,
                       split_heads=True)

        outs = []
        for hh in range(HEADS_PER_BLOCK):
            o = _normalized(acc_ref[2 * hh]) - lam * _normalized(acc_ref[2 * hh + 1])
            ms = jnp.mean(o * o, axis=0, keepdims=True)
            outs.append(o * lax.rsqrt(ms + SUBLN_EPS))
        o = jnp.concatenate(outs, axis=0).T * (g_ref[...] * (1.0 - lambda_init))
        o_ref[rows, :] = o.astype(o_ref.dtype)
        return carry

    lax.fori_loop(0, Q_TILES_PER_STEP, one_tile, 0)


def _log_sigmoids(z):
    l = jnp.log(1.0 + jnp.exp(-jnp.abs(z)))
    return jnp.minimum(-z, 0.0) - l, jnp.minimum(z, 0.0) - l


def _suffix_sums(upper, lg):
    hi = lg.astype(BF16)
    lo = (lg - hi.astype(F32)).astype(BF16)
    return _dot(upper, hi) + _dot(upper, lo)


def _sb_blocks(blocks, k_ref, vt_ref, qt_ref, carry_ref, acc_ref):
    krow, qcol = _key_query_iotas()
    before = krow < qcol
    upper = (qcol > krow).astype(BF16)
    heads = range(HEADS_PER_BLOCK)
    items = [(n, diag, hh) for n, diag in blocks for hh in heads]
    z = [_dot(k_ref[_blk(n), :], qt_ref[hh]) for n, _, hh in items]
    log_sig, log_surv, tails = [], [], []
    for i, (n, diag, hh) in enumerate(items):
        lneg, lpos = _log_sigmoids(z[i])
        if diag:
            lneg = jnp.where(before, lneg, 0.0)
        log_surv.append(lneg)
        log_sig.append(lpos)
        tails.append(_suffix_sums(upper, lneg))
    starts_fresh = blocks[0][1]
    carry = [None if starts_fresh else carry_ref[hh] for hh in heads]
    acc = [None if starts_fresh else acc_ref[hh] for hh in heads]
    for i, (n, diag, hh) in enumerate(items):
        if diag:
            w = jnp.where(before, jnp.exp(log_sig[i] + tails[i]), 0.0)
        else:
            w = jnp.exp(log_sig[i] + (tails[i] + carry[hh]))
        part = _dot(vt_ref[hh, :, _blk(n)], w.astype(BF16))
        acc[hh] = part if acc[hh] is None else acc[hh] + part
        total = jnp.sum(log_surv[i], axis=0, keepdims=True)
        carry[hh] = total if carry[hh] is None else carry[hh] + total
    worst = None
    for hh in heads:
        acc_ref[hh] = acc[hh]
        carry_ref[hh] = carry[hh]
        top = jnp.max(carry[hh])
        worst = top if worst is None else jnp.maximum(worst, top)
    return worst


def _sb_kernel(q_ref, k_ref, v_ref, o_ref, vt_ref, qt_ref, carry_ref, acc_ref, *, n_blocks, scale):
    step = pl.program_id(2)

    @pl.when(step == 0)
    def _first_tile():
        def body(n, c):
            v_t = v_ref[_blk(n), :].astype(F32).T.astype(BF16)
            for hh in range(HEADS_PER_BLOCK):
                vt_ref[hh, :, _blk(n)] = v_t[hh * HEAD_DIM:(hh + 1) * HEAD_DIM, :]
            return c

        lax.fori_loop(0, n_blocks, body, 0)

    def one_tile(i, carry):
        qi = step * Q_TILES_PER_STEP + i
        rows = pl.ds(pl.multiple_of(i * ATT_T, ATT_T), ATT_T)
        q_t = (q_ref[rows, :].astype(F32) * scale).T
        row_h0 = _row_is_head0()
        for hh in range(HEADS_PER_BLOCK):
            in_head = row_h0 if hh == 0 else jnp.logical_not(row_h0)
            qt_ref[hh] = jnp.where(in_head, q_t, 0.0).astype(BF16)
        refs = (k_ref, vt_ref, qt_ref, carry_ref, acc_ref)
        worst = lax.cond(qi > 0,
                         lambda: _sb_blocks([(qi, True), (qi - 1, False)], *refs),
                         lambda: _sb_blocks([(qi, True)], *refs))

        def cond(c):
            j, live = c
            return (j >= 0) & live

        def body(c):
            j, _ = c
            return j - 1, _sb_blocks([(j, False)], *refs) > SB_EXIT

        lax.while_loop(cond, body, (qi - 2, worst > SB_EXIT))

        o = jnp.concatenate([acc_ref[0], acc_ref[1]], axis=0)
        o_ref[rows, :] = o.T.astype(o_ref.dtype)
        return carry

    lax.fori_loop(0, Q_TILES_PER_STEP, one_tile, 0)


def _attn_specs(B, S, q_base, k_base, v_base):
    tq = Q_TILES_PER_STEP * ATT_T
    nq = S // tq
    qspec = pl.BlockSpec((None, tq, LANES), lambda b, h, i: (b, i, q_base + h))
    kspec = pl.BlockSpec((None, S, LANES), lambda b, h, i: (b, 0, k_base + h))
    vspec = pl.BlockSpec((None, S, LANES), lambda b, h, i: (b, 0, v_base + h))
    ospec = pl.BlockSpec((None, tq, LANES), lambda b, h, i: (b, i, h))
    grid = (B, N_HEAD_BLOCKS, nq)
    params = pltpu.CompilerParams(
        dimension_semantics=("parallel", "parallel", "arbitrary"), vmem_limit_bytes=VMEM_LIMIT)
    return grid, qspec, kspec, vspec, ospec, params


_SMEM_SPEC = pl.BlockSpec(memory_space=pltpu.SMEM)


def _moba(qkv, tab):
    B, S, _ = qkv.shape
    grid, qspec, kspec, vspec, ospec, params = _attn_specs(B, S, 0, 3, 6)
    n_blocks = S // MOBA_BLOCK
    nbp = -(-n_blocks // SUBLANES) * SUBLANES
    return pl.pallas_call(
        functools.partial(_moba_kernel, n_blocks=n_blocks, scale=HEAD_DIM ** -0.5),
        grid=grid,
        in_specs=[_SMEM_SPEC, qspec, kspec, vspec],
        out_specs=ospec,
        out_shape=jax.ShapeDtypeStruct((B, S, MIX_W), BF16),
        scratch_shapes=[
            pltpu.VMEM((HEADS_PER_BLOCK, S, LANES), BF16),
            pltpu.VMEM((HEADS_PER_BLOCK, V_ROWS, S), BF16),
            pltpu.VMEM((HEADS_PER_BLOCK, 1, 1), F32),
            pltpu.VMEM((nbp, LANES), F32),
            pltpu.VMEM((HEADS_PER_BLOCK, nbp, ATT_T), F32),
            pltpu.VMEM((HEADS_PER_BLOCK, LANES, ATT_T), BF16),
            pltpu.VMEM((HEADS_PER_BLOCK, 1, ATT_T), F32),
            pltpu.VMEM((HEADS_PER_BLOCK, V_ROWS, ATT_T), F32),
            pltpu.VMEM((HEADS_PER_BLOCK * STAGE_DEPTH, ATT_T, ATT_T), F32),
        ],
        compiler_params=params,
        name="moba_attn",
    )(tab, qkv, qkv, qkv)


def _diff(qkv, tab, diff_lambda, subln_g, lambda_init):
    B, S, _ = qkv.shape
    grid, qspec, kspec, vspec, ospec, params = _attn_specs(B, S, 9, 12, 15)
    n_chains = 2 * HEADS_PER_BLOCK
    return pl.pallas_call(
        functools.partial(_diff_kernel, n_blocks=S // ATT_T, scale=DIFF_QK_DIM ** -0.5,
                          lambda_init=lambda_init),
        grid=grid,
        in_specs=[
            _SMEM_SPEC,
            pl.BlockSpec(diff_lambda.shape, lambda b, h, i: (0, 0)),
            pl.BlockSpec((1, LANES), lambda b, h, i: (0, 0)),
            qspec, kspec, vspec,
        ],
        out_specs=ospec,
        out_shape=jax.ShapeDtypeStruct((B, S, MIX_W), BF16),
        scratch_shapes=[
            pltpu.VMEM((HEADS_PER_BLOCK, S, LANES), BF16),
            pltpu.VMEM((HEADS_PER_BLOCK, V_ROWS, S), BF16),
            pltpu.VMEM((HEADS_PER_BLOCK, 1, 1), F32),
            pltpu.VMEM((n_chains, LANES, ATT_T), BF16),
            pltpu.VMEM((n_chains, 1, ATT_T), F32),
            pltpu.VMEM((n_chains, V_ROWS, ATT_T), F32),
            pltpu.VMEM((n_chains * STAGE_DEPTH, ATT_T, ATT_T), F32),
        ],
        compiler_params=params,
        name="diff_attn",
    )(tab, diff_lambda, subln_g, qkv, qkv, qkv)


def _sb(qkv):
    B, S, _ = qkv.shape
    grid, qspec, kspec, vspec, ospec, params = _attn_specs(B, S, 18, 21, 24)
    return pl.pallas_call(
        functools.partial(_sb_kernel, n_blocks=S // ATT_T, scale=HEAD_DIM ** -0.5),
        grid=grid,
        in_specs=[qspec, kspec, vspec],
        out_specs=ospec,
        out_shape=jax.ShapeDtypeStruct((B, S, MIX_W), BF16),
        scratch_shapes=[
            pltpu.VMEM((HEADS_PER_BLOCK, HEAD_DIM, S), BF16),
            pltpu.VMEM((HEADS_PER_BLOCK, LANES, ATT_T), BF16),
            pltpu.VMEM((HEADS_PER_BLOCK, 1, ATT_T), F32),
            pltpu.VMEM((HEADS_PER_BLOCK, HEAD_DIM, ATT_T), F32),
        ],
        compiler_params=params,
        name="sb_attn",
    )(qkv, qkv, qkv)


def _merge_ln_kernel(x_ref, om_ref, od_ref, os_ref, wg_ref, bg_ref, wbr_ref, wo_ref, lg_ref, lb_ref,
                     o_ref, *, alpha, d_model):
    x = x_ref[...]
    xb = x.astype(BF16)
    merged = None
    for b, ob_ref in enumerate((om_ref, od_ref, os_ref)):
        glogit = _dot(xb, wg_ref[:, b * d_model:(b + 1) * d_model]) + bg_ref[b:b + 1, :]
        term = jax.nn.sigmoid(glogit) * _dot(ob_ref[...], wbr_ref[b])
        merged = term if merged is None else merged + term
    y = alpha * x + _dot(merged.astype(BF16), wo_ref[...])
    o_ref[...] = _layer_norm(y, lg_ref[...], lb_ref[...])


def _merge_ln(x, o_m, o_d, o_s, w_gate, b_gate, w_br, w_out, lg, lb, l, alpha, tm=512):
    T, D = x.shape
    return pl.pallas_call(
        functools.partial(_merge_ln_kernel, alpha=alpha, d_model=D),
        grid=(T // tm,),
        in_specs=[
            pl.BlockSpec((tm, D), lambda i: (i, 0)),
            pl.BlockSpec((tm, MIX_W), lambda i: (i, 0)),
            pl.BlockSpec((tm, MIX_W), lambda i: (i, 0)),
            pl.BlockSpec((tm, MIX_W), lambda i: (i, 0)),
            pl.BlockSpec((None, D, N_BRANCH * D), lambda i: (l, 0, 0), pipeline_mode=pl.Buffered(1)),
            pl.BlockSpec((None, N_BRANCH, D), lambda i: (l, 0, 0)),
            pl.BlockSpec((None, N_BRANCH, MIX_W, D), lambda i: (l, 0, 0, 0), pipeline_mode=pl.Buffered(1)),
            pl.BlockSpec((None, D, D), lambda i: (l, 0, 0), pipeline_mode=pl.Buffered(1)),
            pl.BlockSpec((1, D), lambda i: (0, 0)),
            pl.BlockSpec((1, D), lambda i: (0, 0)),
        ],
        out_specs=pl.BlockSpec((tm, D), lambda i: (i, 0)),
        out_shape=jax.ShapeDtypeStruct((T, D), F32),
        compiler_params=pltpu.CompilerParams(
            dimension_semantics=("parallel",), vmem_limit_bytes=VMEM_LIMIT),
        name="merge_ln",
    )(x, o_m, o_d, o_s, w_gate, b_gate, w_br, w_out, lg, lb)


def _alibi_slopes(n):
    return (2.0 ** (-8.0 * np.arange(1, n + 1, dtype=np.float32) / n)).astype(np.float32)


def _slope_table(slopes):
    s2 = slopes.astype(np.float64) * LOG2E
    return np.concatenate([s2, 1.0 / (s2 * ATT_T)]).astype(np.float32)


def kernel(x, ln_g, ln_b, ffn_w_gate, ffn_w_up, ffn_w_down, w_in, b_gate, diff_lambda, diff_subln_g,
           w_br_moba, w_br_diff, w_br_sb, w_out):
    B, S, D = x.shape
    depth = ln_g.shape[0]
    assert S % (Q_TILES_PER_STEP * ATT_T) == 0 and ATT_T == MOBA_BLOCK
    assert w_in.shape[-1] == QKV_W + N_BRANCH * D
    alpha = (2.0 * depth) ** 0.25

    wg = ffn_w_gate.astype(BF16)
    wu = ffn_w_up.astype(BF16)
    wd = ffn_w_down.astype(BF16)
    w_qkv = w_in[:, :, :QKV_W].astype(BF16)
    w_gate = w_in[:, :, QKV_W:].astype(BF16)
    w_br = jnp.stack([w_br_moba, w_br_diff, w_br_sb], axis=1).astype(BF16)
    w_out_b = w_out.astype(BF16)
    slopes = _alibi_slopes(2 * N_HEADS)
    tab_moba = jnp.asarray(_slope_table(slopes[0::2]))
    tab_diff = jnp.asarray(_slope_table(slopes[1::2]))
    subln_g = jnp.tile(diff_subln_g.astype(F32), (1, HEADS_PER_BLOCK))

    h = x.reshape(B * S, D)
    for l in range(depth):
        lg = ln_g[l][:, None, :]
        lb = ln_b[l][:, None, :]
        h = _ffn_ln(h, wg, wu, wd, lg[0], lb[0], l, 0, alpha)
        qkv = _qkv_proj(h, w_qkv, l).reshape(B, S, QKV_W)
        lambda_init = 0.8 - 0.6 * math.exp(-0.3 * l)
        o_m = _moba(qkv, tab_moba)
        o_d = _diff(qkv, tab_diff, diff_lambda[l], subln_g[l][None, :], lambda_init)
        o_s = _sb(qkv)
        T = B * S
        h = _merge_ln(h, o_m.reshape(T, MIX_W), o_d.reshape(T, MIX_W), o_s.reshape(T, MIX_W),
                      w_gate, b_gate, w_br, w_out_b, lg[1], lb[1], l, alpha)
        h = _ffn_ln(h, wg, wu, wd, lg[2], lb[2], l, 1, alpha)
    return h.reshape(B, S, D)
```

```python
import functools
import math

import numpy as np
import jax
import jax.numpy as jnp
from jax import lax
from jax.experimental import pallas as pl
from jax.experimental.pallas import tpu as pltpu

F32 = jnp.float32
BF16 = jnp.bfloat16

HEAD_DIM = 64
N_HEADS = 6
DIFF_QK_DIM = HEAD_DIM // 2
MOBA_BLOCK = 256
MOBA_TOPK = 3
N_BRANCH = 3
LN_EPS = 1e-5
SUBLN_EPS = 1e-5

LANES = 128
SUBLANES = 8
DENOM_ROWS = 16
V_ROWS = HEAD_DIM + DENOM_ROWS
HEADS_PER_BLOCK = LANES // HEAD_DIM
N_HEAD_BLOCKS = N_HEADS // HEADS_PER_BLOCK
MIX_W = N_HEADS * HEAD_DIM
QKV_W = 9 * MIX_W
ATT_T = 256
NEG = -1e30
SB_EXIT = -110.0
LOG2E = math.log2(math.e)
N_BIAS_COLS = 3
BOUND_SLACK = 1.01
MAX_EXPONENT = 100.0
SKIP_BITS = 150.0
ITEMS_PER_TRIP = 16
STAGE_DEPTH = 2
VMEM_LIMIT = 56 * 1024 * 1024


def _dot(a, b, precision=None):
    return jnp.dot(a, b, precision=precision, preferred_element_type=F32)


def _layer_norm(y, g, b):
    mu = jnp.mean(y, axis=-1, keepdims=True)
    yc = y - mu
    var = jnp.mean(yc * yc, axis=-1, keepdims=True)
    return yc * lax.rsqrt(var + LN_EPS) * g + b


def _ffn_ln_kernel(x_ref, wg_ref, wu_ref, wd_ref, lg_ref, lb_ref, o_ref, hid_ref, *, alpha, tf):
    x = x_ref[...]
    xb = x.astype(BF16)
    for c in range(hid_ref.shape[1] // tf):
        cols = slice(c * tf, (c + 1) * tf)
        gate = _dot(xb, wg_ref[:, cols])
        up = _dot(xb, wu_ref[:, cols])
        hid_ref[:, cols] = (gate * jax.nn.sigmoid(gate) * up).astype(BF16)
    y = alpha * x + 0.5 * _dot(hid_ref[...], wd_ref[...])
    o_ref[...] = _layer_norm(y, lg_ref[...], lb_ref[...])


def _ffn_ln(x, wg, wu, wd, lg, lb, l, j, alpha, tm=512, tf=256):
    T, D = x.shape
    FF = wg.shape[-1]
    resident = pl.Buffered(1)
    return pl.pallas_call(
        functools.partial(_ffn_ln_kernel, alpha=alpha, tf=tf),
        grid=(T // tm,),
        in_specs=[
            pl.BlockSpec((tm, D), lambda i: (i, 0)),
            pl.BlockSpec((None, None, D, FF), lambda i: (l, j, 0, 0), pipeline_mode=resident),
            pl.BlockSpec((None, None, D, FF), lambda i: (l, j, 0, 0), pipeline_mode=resident),
            pl.BlockSpec((None, None, FF, D), lambda i: (l, j, 0, 0), pipeline_mode=resident),
            pl.BlockSpec((1, D), lambda i: (0, 0)),
            pl.BlockSpec((1, D), lambda i: (0, 0)),
        ],
        out_specs=pl.BlockSpec((tm, D), lambda i: (i, 0)),
        out_shape=jax.ShapeDtypeStruct((T, D), F32),
        scratch_shapes=[pltpu.VMEM((tm, FF), BF16)],
        compiler_params=pltpu.CompilerParams(
            dimension_semantics=("parallel",), vmem_limit_bytes=VMEM_LIMIT),
        name="ffn_ln",
    )(x, wg, wu, wd, lg, lb)


def _proj_kernel(x_ref, w_ref, o_ref, *, tn):
    xb = x_ref[...].astype(BF16)
    for c in range(o_ref.shape[1] // tn):
        cols = slice(c * tn, (c + 1) * tn)
        o_ref[:, cols] = _dot(xb, w_ref[:, cols]).astype(o_ref.dtype)


def _qkv_proj(x, w_in, l, tm=512, tn=1152):
    T, D = x.shape
    return pl.pallas_call(
        functools.partial(_proj_kernel, tn=tn),
        grid=(T // tm,),
        in_specs=[
            pl.BlockSpec((tm, D), lambda i: (i, 0)),
            pl.BlockSpec((None, D, QKV_W), lambda i: (l, 0, 0), pipeline_mode=pl.Buffered(1)),
        ],
        out_specs=pl.BlockSpec((tm, QKV_W), lambda i: (i, 0)),
        out_shape=jax.ShapeDtypeStruct((T, QKV_W), BF16),
        compiler_params=pltpu.CompilerParams(
            dimension_semantics=("parallel",), vmem_limit_bytes=VMEM_LIMIT),
        name="qkv_proj",
    )(x, w_in)


def _lane_iota():
    return lax.broadcasted_iota(jnp.int32, (1, LANES), 1)


def _head_lanes(hh):
    lane = _lane_iota()
    return (lane >= hh * HEAD_DIM) & (lane < (hh + 1) * HEAD_DIM)


def _bias_lanes(hh):
    lane = _lane_iota()
    lo = (1 - hh) * HEAD_DIM
    return (lane >= lo) & (lane < lo + N_BIAS_COLS)


def _blk(j):
    return pl.ds(pl.multiple_of(j * ATT_T, ATT_T), ATT_T)


def _key_query_iotas():
    krow = lax.broadcasted_iota(jnp.int32, (ATT_T, ATT_T), 0)
    qcol = lax.broadcasted_iota(jnp.int32, (ATT_T, ATT_T), 1)
    return krow, qcol


def _build_softmax_operands(tab_ref, hb, k_ref, v_ref, kp_ref, vt_ref, kmax_ref, *, n_blocks, scale):
    lane = _lane_iota()
    pos = lax.broadcasted_iota(jnp.int32, (ATT_T, LANES), 0).astype(F32)
    bias_cols = []
    for hh in range(HEADS_PER_BLOCK):
        b = pos * tab_ref[hb * HEADS_PER_BLOCK + hh]
        lo = (1 - hh) * HEAD_DIM
        cols = jnp.zeros_like(b)
        for i in range(N_BIAS_COLS):
            piece = b.astype(BF16).astype(F32)
            cols = jnp.where(lane == lo + i, piece, cols)
            b = b - piece
        bias_cols.append(cols)
    kmax_ref[...] = jnp.zeros_like(kmax_ref)

    def body(n, c):
        kf = k_ref[_blk(n), :].astype(F32) * (scale * LOG2E)
        v_t = v_ref[_blk(n), :].astype(F32).T.astype(BF16)
        for hh in range(HEADS_PER_BLOCK):
            hm = _head_lanes(hh)
            kp_ref[hh, _blk(n), :] = jnp.where(hm, kf, bias_cols[hh]).astype(BF16)
            vt_ref[hh, 0:HEAD_DIM, _blk(n)] = v_t[hh * HEAD_DIM:(hh + 1) * HEAD_DIM, :]
            vt_ref[hh, HEAD_DIM:V_ROWS, _blk(n)] = jnp.ones((DENOM_ROWS, ATT_T), BF16)
            norm2 = jnp.sum(jnp.where(hm, kf * kf, 0.0), axis=1, keepdims=True)
            kmax_ref[hh] = jnp.maximum(kmax_ref[hh], jnp.max(norm2, axis=0, keepdims=True))
        return c

    lax.fori_loop(0, n_blocks, body, 0)


def _chain_qt(c, hh, seg_lo, seg_width, q_t, qt_ref):
    row = lax.broadcasted_iota(jnp.int32, (LANES, ATT_T), 0)
    lo = (1 - hh) * HEAD_DIM
    qsel = jnp.where((row >= seg_lo) & (row < seg_lo + seg_width), q_t, 0.0)
    qt_ref[c] = jnp.where((row >= lo) & (row < lo + N_BIAS_COLS), 1.0, qsel).astype(BF16)
    return qsel, jnp.sum(qsel * qsel, axis=0, keepdims=True)


def _stage_scores(c, hh, n, slot, kp_ref, qt_ref, sbuf_ref):
    sbuf_ref[c * STAGE_DEPTH + slot] = _dot(kp_ref[hh, _blk(jnp.maximum(n, 0)), :], qt_ref[c])


def _diag_blocks(chains, qi, qn2, slope2, kp_ref, vt_ref, kmax_ref, qt_ref, ref_ref, acc_ref, sbuf_ref):
    krow, qcol = _key_query_iotas()
    causal = krow <= qcol
    r = lax.broadcasted_iota(jnp.int32, (1, ATT_T), 1).astype(F32)
    scores = [_dot(kp_ref[hh, _blk(qi), :], qt_ref[c]) for c, hh in enumerate(chains)]
    for slot in range(STAGE_DEPTH):
        for c, hh in enumerate(chains):
            _stage_scores(c, hh, qi - 1 - slot, slot, kp_ref, qt_ref, sbuf_ref)
    gaps = []
    for c, hh in enumerate(chains):
        s = jnp.where(causal, scores[c], NEG)
        m = jnp.max(s, axis=0, keepdims=True)
        bound = jnp.sqrt(qn2[c] * kmax_ref[hh]) * BOUND_SLACK + slope2[hh] * r
        ref_ref[c] = m
        acc_ref[c] = _dot(vt_ref[hh, :, _blk(qi)], jnp.exp2(s - m).astype(BF16))
        gaps.append(jnp.max(bound - m))
    return gaps


def _first_block(gap, inv_blk, qi):
    reach = jnp.minimum((gap + SKIP_BITS) * inv_blk, 1e6).astype(jnp.int32)
    return jnp.maximum(qi - 1 - reach, 0)


def _fixed_ref_step(c, hh, n, slot, off, kp_ref, vt_ref, qt_ref, ref_ref, acc_ref, sbuf_ref):
    p = jnp.exp2(sbuf_ref[c * STAGE_DEPTH + slot] - (ref_ref[c] - off)).astype(BF16)
    _stage_scores(c, hh, n - STAGE_DEPTH, slot, kp_ref, qt_ref, sbuf_ref)
    acc_ref[c] += _dot(vt_ref[hh, :, _blk(jnp.maximum(n, 0))], p)


def _online_step(c, hh, n, off, kp_ref, vt_ref, qt_ref, m_ref, acc_ref):
    s = _dot(kp_ref[hh, _blk(n), :], qt_ref[c])
    m_old = m_ref[c]
    m_new = jnp.maximum(m_old, jnp.max(s, axis=0, keepdims=True) + off)
    p = jnp.exp2(s - (m_new - off)).astype(BF16)
    acc_ref[c] = acc_ref[c] * jnp.exp2(m_old - m_new) + _dot(vt_ref[hh, :, _blk(n)], p)
    m_ref[c] = m_new


def _softmax_sweep(qi, chains, gaps, first, off_fn, kp_ref, vt_ref, qt_ref, ref_ref, acc_ref, sbuf_ref,
                   split_heads):
    refs = (kp_ref, vt_ref, qt_ref, ref_ref, acc_ref)
    assert STAGE_DEPTH == 2
    widest = gaps[0]
    for g in gaps[1:]:
        widest = jnp.maximum(widest, g)
    fixed_ok = widest <= MAX_EXPONENT

    def run_fixed(top, left, heads):
        active = [(c, hh) for c, hh in enumerate(chains) if hh in heads]
        per_trip = max(STAGE_DEPTH, ITEMS_PER_TRIP // len(active))
        assert per_trip % STAGE_DEPTH == 0

        def sweep(top, trips, per_trip):
            def body(i, carry):
                for j in range(per_trip):
                    n = top - i * per_trip - j
                    off_end = jnp.where(n < 0, NEG, 0.0)
                    for c, hh in active:
                        off = off_fn(hh, jnp.maximum(n, 0)) + off_end
                        _fixed_ref_step(c, hh, n, j % STAGE_DEPTH, off, *refs, sbuf_ref)
                return carry

            lax.fori_loop(0, trips, body, 0)

        whole = left // per_trip
        sweep(top, whole, per_trip)
        if per_trip > STAGE_DEPTH:
            rest = left - whole * per_trip
            sweep(top - whole * per_trip, (rest + STAGE_DEPTH - 1) // STAGE_DEPTH, STAGE_DEPTH)
        else:
            sweep(top - whole * per_trip, left - whole * per_trip, 1)

    @pl.when(fixed_ok)
    def _fixed_reference():
        far = jnp.minimum(first[0], first[1])
        if split_heads:
            near = jnp.maximum(first[0], first[1])
            together = jnp.minimum((qi - near + 1) // 2 * 2, qi - far)
            run_fixed(qi - 1, together, (0, 1))
            for hh in range(HEADS_PER_BLOCK):
                run_fixed(qi - 1 - together, jnp.maximum(qi - together - first[hh], 0), (hh,))
        else:
            run_fixed(qi - 1, qi - far, (0, 1))

    @pl.when(jnp.logical_not(fixed_ok))
    def _running_maximum():
        def body(n, carry):
            for c, hh in enumerate(chains):
                _online_step(c, hh, n, off_fn(hh, n), *refs)
            return carry

        lax.fori_loop(0, qi, body, 0)


def _normalized(acc):
    return acc[0:HEAD_DIM, :] * (1.0 / acc[HEAD_DIM:HEAD_DIM + 1, :])


def _row_is_head0():
    return lax.broadcasted_iota(jnp.int32, (LANES, ATT_T), 0) < HEAD_DIM


def _moba_kernel(tab_ref, q_ref, k_ref, v_ref, o_ref, kp_ref, vt_ref, kmax_ref, kmean_ref, selm_ref,
                 qt_ref, ref_ref, acc_ref, sbuf_ref, *, n_blocks, scale):
    hb = pl.program_id(1)
    nbp = kmean_ref.shape[0]
    _build_softmax_operands(tab_ref, hb, k_ref, v_ref, kp_ref, vt_ref, kmax_ref, n_blocks=n_blocks, scale=scale)
    kmean_ref[...] = jnp.zeros_like(kmean_ref)

    def block_mean(n, c):
        kmean_ref[pl.ds(n, 1), :] = jnp.mean(k_ref[_blk(n), :].astype(F32), axis=0, keepdims=True)
        return c

    lax.fori_loop(0, n_blocks, block_mean, 0)

    def one_tile(qi, carry):
        rows = _blk(qi)
        q_t = q_ref[rows, :].astype(F32).T
        blk_id = lax.broadcasted_iota(jnp.int32, (nbp, ATT_T), 0)
        blk_f = blk_id.astype(F32)
        chains = list(range(HEADS_PER_BLOCK))
        slope2 = [tab_ref[hb * HEADS_PER_BLOCK + hh] for hh in chains]
        qn2 = []
        for hh in chains:
            hm = _head_lanes(hh)
            qsel, norm2 = _chain_qt(hh, hh, hh * HEAD_DIM, HEAD_DIM, q_t, qt_ref)
            qn2.append(norm2)

            gate = _dot(jnp.where(hm, kmean_ref[...], 0.0), qsel, precision=lax.Precision.HIGHEST)
            g = jnp.where(blk_id < qi, gate, -jnp.inf)
            sel = jnp.zeros(g.shape, dtype=jnp.bool_)
            for _ in range(MOBA_TOPK):
                mx = jnp.max(g, axis=0, keepdims=True)
                pick = jnp.min(jnp.where(g == mx, blk_f, float(nbp)), axis=0, keepdims=True)
                hit = blk_f == pick
                sel = sel | (hit & (mx > -jnp.inf))
                g = jnp.where(hit, -jnp.inf, g)
            selm_ref[hh] = jnp.where(sel, 0.0, NEG)

        gaps = _diag_blocks(chains, qi, qn2, slope2, kp_ref, vt_ref, kmax_ref, qt_ref, ref_ref, acc_ref, sbuf_ref)
        first = [_first_block(gaps[hh], tab_ref[N_HEADS + hb * HEADS_PER_BLOCK + hh], qi) for hh in chains]

        def off_fn(hh, n):
            return slope2[hh] * ((n - qi) * ATT_T).astype(F32) + selm_ref[hh, pl.ds(n, 1), :]

        _softmax_sweep(qi, chains, gaps, first, off_fn, kp_ref, vt_ref, qt_ref, ref_ref, acc_ref, sbuf_ref,
                       split_heads=False)

        o = jnp.concatenate([_normalized(acc_ref[0]), _normalized(acc_ref[1])], axis=0)
        o_ref[rows, :] = o.T.astype(o_ref.dtype)
        return carry

    lax.fori_loop(0, n_blocks, one_tile, 0)


def _diff_kernel(tab_ref, lam_ref, g_ref, q_ref, k_ref, v_ref, o_ref, kp_ref, vt_ref, kmax_ref, qt_ref,
                 ref_ref, acc_ref, sbuf_ref, *, n_blocks, scale, lambda_init):
    hb = pl.program_id(1)
    _build_softmax_operands(tab_ref, hb, k_ref, v_ref, kp_ref, vt_ref, kmax_ref, n_blocks=n_blocks, scale=scale)

    lf = lam_ref[...].astype(F32)
    lam = (jnp.exp(jnp.sum(lf[0:1] * lf[1:2], axis=-1, keepdims=True))
           - jnp.exp(jnp.sum(lf[2:3] * lf[3:4], axis=-1, keepdims=True)) + lambda_init)

    def one_tile(qi, carry):
        rows = _blk(qi)
        q_t = q_ref[rows, :].astype(F32).T
        chains = [hh for hh in range(HEADS_PER_BLOCK) for _ in range(2)]
        slope2 = [tab_ref[hb * HEADS_PER_BLOCK + hh] for hh in range(HEADS_PER_BLOCK)]
        qn2 = []
        for c, hh in enumerate(chains):
            qn2.append(_chain_qt(c, hh, hh * HEAD_DIM + (c % 2) * DIFF_QK_DIM, DIFF_QK_DIM, q_t, qt_ref)[1])
        gaps = _diag_blocks(chains, qi, qn2, slope2, kp_ref, vt_ref, kmax_ref, qt_ref, ref_ref, acc_ref, sbuf_ref)
        first = [None] * HEADS_PER_BLOCK
        for c, hh in enumerate(chains):
            fb = _first_block(gaps[c], tab_ref[N_HEADS + hb * HEADS_PER_BLOCK + hh], qi)
            first[hh] = fb if first[hh] is None else jnp.minimum(first[hh], fb)

        def off_fn(hh, n):
            return slope2[hh] * ((n - qi) * ATT_T).astype(F32)

        _softmax_sweep(qi, chains, gaps, first, off_fn, kp_ref, vt_ref, qt_ref, ref_ref, acc_ref, sbuf_ref,
                       split_heads=True)

        outs = []
        for hh in range(HEADS_PER_BLOCK):
            o = _normalized(acc_ref[2 * hh]) - lam * _normalized(acc_ref[2 * hh + 1])
            ms = jnp.mean(o * o, axis=0, keepdims=True)
            outs.append(o * lax.rsqrt(ms + SUBLN_EPS))
        o = jnp.concatenate(outs, axis=0).T * (g_ref[...] * (1.0 - lambda_init))
        o_ref[rows, :] = o.astype(o_ref.dtype)
        return carry

    lax.fori_loop(0, n_blocks, one_tile, 0)


def _log_sigmoids(z):
    l = jnp.log(1.0 + jnp.exp(-jnp.abs(z)))
    return jnp.minimum(-z, 0.0) - l, jnp.minimum(z, 0.0) - l


def _suffix_sums(upper, lg):
    hi = lg.astype(BF16)
    lo = (lg - hi.astype(F32)).astype(BF16)
    return _dot(upper, hi) + _dot(upper, lo)


def _sb_blocks(blocks, k_ref, vt_ref, qt_ref, carry_ref, acc_ref):
    krow, qcol = _key_query_iotas()
    before = krow < qcol
    upper = (qcol > krow).astype(BF16)
    heads = range(HEADS_PER_BLOCK)
    items = [(n, diag, hh) for n, diag in blocks for hh in heads]
    z = [_dot(k_ref[_blk(n), :], qt_ref[hh]) for n, _, hh in items]
    log_sig, log_surv, tails = [], [], []
    for i, (n, diag, hh) in enumerate(items):
        lneg, lpos = _log_sigmoids(z[i])
        if diag:
            lneg = jnp.where(before, lneg, 0.0)
        log_surv.append(lneg)
        log_sig.append(lpos)
        tails.append(_suffix_sums(upper, lneg))
    starts_fresh = blocks[0][1]
    carry = [None if starts_fresh else carry_ref[hh] for hh in heads]
    acc = [None if starts_fresh else acc_ref[hh] for hh in heads]
    for i, (n, diag, hh) in enumerate(items):
        if diag:
            w = jnp.where(before, jnp.exp(log_sig[i] + tails[i]), 0.0)
        else:
            w = jnp.exp(log_sig[i] + (tails[i] + carry[hh]))
        part = _dot(vt_ref[hh, :, _blk(n)], w.astype(BF16))
        acc[hh] = part if acc[hh] is None else acc[hh] + part
        total = jnp.sum(log_surv[i], axis=0, keepdims=True)
        carry[hh] = total if carry[hh] is None else carry[hh] + total
    worst = None
    for hh in heads:
        acc_ref[hh] = acc[hh]
        carry_ref[hh] = carry[hh]
        top = jnp.max(carry[hh])
        worst = top if worst is None else jnp.maximum(worst, top)
    return worst


def _sb_kernel(q_ref, k_ref, v_ref, o_ref, vt_ref, qt_ref, carry_ref, acc_ref, *, n_blocks, scale):
    def transpose_values(n, c):
        v_t = v_ref[_blk(n), :].astype(F32).T.astype(BF16)
        for hh in range(HEADS_PER_BLOCK):
            vt_ref[hh, :, _blk(n)] = v_t[hh * HEAD_DIM:(hh + 1) * HEAD_DIM, :]
        return c

    lax.fori_loop(0, n_blocks, transpose_values, 0)

    def one_tile(qi, carry):
        rows = _blk(qi)
        q_t = (q_ref[rows, :].astype(F32) * scale).T
        row_h0 = _row_is_head0()
        for hh in range(HEADS_PER_BLOCK):
            in_head = row_h0 if hh == 0 else jnp.logical_not(row_h0)
            qt_ref[hh] = jnp.where(in_head, q_t, 0.0).astype(BF16)
        refs = (k_ref, vt_ref, qt_ref, carry_ref, acc_ref)
        worst = lax.cond(qi > 0,
                         lambda: _sb_blocks([(qi, True), (qi - 1, False)], *refs),
                         lambda: _sb_blocks([(qi, True)], *refs))

        def cond(c):
            j, live = c
            return (j >= 0) & live

        def body(c):
            j, _ = c
            return j - 1, _sb_blocks([(j, False)], *refs) > SB_EXIT

        lax.while_loop(cond, body, (qi - 2, worst > SB_EXIT))

        o = jnp.concatenate([acc_ref[0], acc_ref[1]], axis=0)
        o_ref[rows, :] = o.T.astype(o_ref.dtype)
        return carry

    lax.fori_loop(0, n_blocks, one_tile, 0)


def _attn_specs(B, S, q_base, k_base, v_base):
    qspec = pl.BlockSpec((None, S, LANES), lambda b, h: (b, 0, q_base + h))
    kspec = pl.BlockSpec((None, S, LANES), lambda b, h: (b, 0, k_base + h))
    vspec = pl.BlockSpec((None, S, LANES), lambda b, h: (b, 0, v_base + h))
    ospec = pl.BlockSpec((None, S, LANES), lambda b, h: (b, 0, h))
    grid = (B, N_HEAD_BLOCKS)
    params = pltpu.CompilerParams(
        dimension_semantics=("parallel", "parallel"), vmem_limit_bytes=VMEM_LIMIT)
    return grid, qspec, kspec, vspec, ospec, params


_SMEM_SPEC = pl.BlockSpec(memory_space=pltpu.SMEM)


def _moba(qkv, tab):
    B, S, _ = qkv.shape
    grid, qspec, kspec, vspec, ospec, params = _attn_specs(B, S, 0, 3, 6)
    n_blocks = S // MOBA_BLOCK
    nbp = -(-n_blocks // SUBLANES) * SUBLANES
    return pl.pallas_call(
        functools.partial(_moba_kernel, n_blocks=n_blocks, scale=HEAD_DIM ** -0.5),
        grid=grid,
        in_specs=[_SMEM_SPEC, qspec, kspec, vspec],
        out_specs=ospec,
        out_shape=jax.ShapeDtypeStruct((B, S, MIX_W), BF16),
        scratch_shapes=[
            pltpu.VMEM((HEADS_PER_BLOCK, S, LANES), BF16),
            pltpu.VMEM((HEADS_PER_BLOCK, V_ROWS, S), BF16),
            pltpu.VMEM((HEADS_PER_BLOCK, 1, 1), F32),
            pltpu.VMEM((nbp, LANES), F32),
            pltpu.VMEM((HEADS_PER_BLOCK, nbp, ATT_T), F32),
            pltpu.VMEM((HEADS_PER_BLOCK, LANES, ATT_T), BF16),
            pltpu.VMEM((HEADS_PER_BLOCK, 1, ATT_T), F32),
            pltpu.VMEM((HEADS_PER_BLOCK, V_ROWS, ATT_T), F32),
            pltpu.VMEM((HEADS_PER_BLOCK * STAGE_DEPTH, ATT_T, ATT_T), F32),
        ],
        compiler_params=params,
        name="moba_attn",
    )(tab, qkv, qkv, qkv)


def _diff(qkv, tab, diff_lambda, subln_g, lambda_init):
    B, S, _ = qkv.shape
    grid, qspec, kspec, vspec, ospec, params = _attn_specs(B, S, 9, 12, 15)
    n_chains = 2 * HEADS_PER_BLOCK
    return pl.pallas_call(
        functools.partial(_diff_kernel, n_blocks=S // ATT_T, scale=DIFF_QK_DIM ** -0.5,
                          lambda_init=lambda_init),
        grid=grid,
        in_specs=[
            _SMEM_SPEC,
            pl.BlockSpec(diff_lambda.shape, lambda b, h: (0, 0)),
            pl.BlockSpec((1, LANES), lambda b, h: (0, 0)),
            qspec, kspec, vspec,
        ],
        out_specs=ospec,
        out_shape=jax.ShapeDtypeStruct((B, S, MIX_W), BF16),
        scratch_shapes=[
            pltpu.VMEM((HEADS_PER_BLOCK, S, LANES), BF16),
            pltpu.VMEM((HEADS_PER_BLOCK, V_ROWS, S), BF16),
            pltpu.VMEM((HEADS_PER_BLOCK, 1, 1), F32),
            pltpu.VMEM((n_chains, LANES, ATT_T), BF16),
            pltpu.VMEM((n_chains, 1, ATT_T), F32),
            pltpu.VMEM((n_chains, V_ROWS, ATT_T), F32),
            pltpu.VMEM((n_chains * STAGE_DEPTH, ATT_T, ATT_T), F32),
        ],
        compiler_params=params,
        name="diff_attn",
    )(tab, diff_lambda, subln_g, qkv, qkv, qkv)


def _sb(qkv):
    B, S, _ = qkv.shape
    grid, qspec, kspec, vspec, ospec, params = _attn_specs(B, S, 18, 21, 24)
    return pl.pallas_call(
        functools.partial(_sb_kernel, n_blocks=S // ATT_T, scale=HEAD_DIM ** -0.5),
        grid=grid,
        in_specs=[qspec, kspec, vspec],
        out_specs=ospec,
        out_shape=jax.ShapeDtypeStruct((B, S, MIX_W), BF16),
        scratch_shapes=[
            pltpu.VMEM((HEADS_PER_BLOCK, HEAD_DIM, S), BF16),
            pltpu.VMEM((HEADS_PER_BLOCK, LANES, ATT_T), BF16),
            pltpu.VMEM((HEADS_PER_BLOCK, 1, ATT_T), F32),
            pltpu.VMEM((HEADS_PER_BLOCK, HEAD_DIM, ATT_T), F32),
        ],
        compiler_params=params,
        name="sb_attn",
    )(qkv, qkv, qkv)


def _merge_ln_kernel(x_ref, om_ref, od_ref, os_ref, wg_ref, bg_ref, wbr_ref, wo_ref, lg_ref, lb_ref,
                     o_ref, *, alpha, d_model):
    x = x_ref[...]
    xb = x.astype(BF16)
    merged = None
    for b, ob_ref in enumerate((om_ref, od_ref, os_ref)):
        glogit = _dot(xb, wg_ref[:, b * d_model:(b + 1) * d_model]) + bg_ref[b:b + 1, :]
        term = jax.nn.sigmoid(glogit) * _dot(ob_ref[...], wbr_ref[b])
        merged = term if merged is None else merged + term
    y = alpha * x + _dot(merged.astype(BF16), wo_ref[...])
    o_ref[...] = _layer_norm(y, lg_ref[...], lb_ref[...])


def _merge_ln(x, o_m, o_d, o_s, w_gate, b_gate, w_br, w_out, lg, lb, l, alpha, tm=512):
    T, D = x.shape
    return pl.pallas_call(
        functools.partial(_merge_ln_kernel, alpha=alpha, d_model=D),
        grid=(T // tm,),
        in_specs=[
            pl.BlockSpec((tm, D), lambda i: (i, 0)),
            pl.BlockSpec((tm, MIX_W), lambda i: (i, 0)),
            pl.BlockSpec((tm, MIX_W), lambda i: (i, 0)),
            pl.BlockSpec((tm, MIX_W), lambda i: (i, 0)),
            pl.BlockSpec((None, D, N_BRANCH * D), lambda i: (l, 0, 0), pipeline_mode=pl.Buffered(1)),
            pl.BlockSpec((None, N_BRANCH, D), lambda i: (l, 0, 0)),
            pl.BlockSpec((None, N_BRANCH, MIX_W, D), lambda i: (l, 0, 0, 0), pipeline_mode=pl.Buffered(1)),
            pl.BlockSpec((None, D, D), lambda i: (l, 0, 0), pipeline_mode=pl.Buffered(1)),
            pl.BlockSpec((1, D), lambda i: (0, 0)),
            pl.BlockSpec((1, D), lambda i: (0, 0)),
        ],
        out_specs=pl.BlockSpec((tm, D), lambda i: (i, 0)),
        out_shape=jax.ShapeDtypeStruct((T, D), F32),
        compiler_params=pltpu.CompilerParams(
            dimension_semantics=("parallel",), vmem_limit_bytes=VMEM_LIMIT),
        name="merge_ln",
    )(x, o_m, o_d, o_s, w_gate, b_gate, w_br, w_out, lg, lb)


def _alibi_slopes(n):
    return (2.0 ** (-8.0 * np.arange(1, n + 1, dtype=np.float32) / n)).astype(np.float32)


def _slope_table(slopes):
    s2 = slopes.astype(np.float64) * LOG2E
    return np.concatenate([s2, 1.0 / (s2 * ATT_T)]).astype(np.float32)


def kernel(x, ln_g, ln_b, ffn_w_gate, ffn_w_up, ffn_w_down, w_in, b_gate, diff_lambda, diff_subln_g,
           w_br_moba, w_br_diff, w_br_sb, w_out):
    B, S, D = x.shape
    depth = ln_g.shape[0]
    assert S % ATT_T == 0 and ATT_T == MOBA_BLOCK
    assert w_in.shape[-1] == QKV_W + N_BRANCH * D
    alpha = (2.0 * depth) ** 0.25

    wg = ffn_w_gate.astype(BF16)
    wu = ffn_w_up.astype(BF16)
    wd = ffn_w_down.astype(BF16)
    w_qkv = w_in[:, :, :QKV_W].astype(BF16)
    w_gate = w_in[:, :, QKV_W:].astype(BF16)
    w_br = jnp.stack([w_br_moba, w_br_diff, w_br_sb], axis=1).astype(BF16)
    w_out_b = w_out.astype(BF16)
    slopes = _alibi_slopes(2 * N_HEADS)
    tab_moba = jnp.asarray(_slope_table(slopes[0::2]))
    tab_diff = jnp.asarray(_slope_table(slopes[1::2]))
    subln_g = jnp.tile(diff_subln_g.astype(F32), (1, HEADS_PER_BLOCK))

    h = x.reshape(B * S, D)
    for l in range(depth):
        lg = ln_g[l][:, None, :]
        lb = ln_b[l][:, None, :]
        h = _ffn_ln(h, wg, wu, wd, lg[0], lb[0], l, 0, alpha)
        qkv = _qkv_proj(h, w_qkv, l).reshape(B, S, QKV_W)
        lambda_init = 0.8 - 0.6 * math.exp(-0.3 * l)
        o_m = _moba(qkv, tab_moba)
        o_d = _diff(qkv, tab_diff, diff_lambda[l], subln_g[l][None, :], lambda_init)
        o_s = _sb(qkv)
        T = B * S
        h = _merge_ln(h, o_m.reshape(T, MIX_W), o_d.reshape(T, MIX_W), o_s.reshape(T, MIX_W),
                      w_gate, b_gate, w_br, w_out_b, lg[1], lb[1], l, alpha)
        h = _ffn_ln(h, wg, wu, wd, lg[2], lb[2], l, 1, alpha)
    return h.reshape(B, S, D)
```

```python
import functools
import math

import numpy as np
import jax
import jax.numpy as jnp
from jax import lax
from jax.experimental import pallas as pl
from jax.experimental.pallas import tpu as pltpu

F32 = jnp.float32
BF16 = jnp.bfloat16

HEAD_DIM = 64
N_HEADS = 6
DIFF_QK_DIM = HEAD_DIM // 2
MOBA_BLOCK = 256
MOBA_TOPK = 3
N_BRANCH = 3
LN_EPS = 1e-5
SUBLN_EPS = 1e-5

LANES = 128
SUBLANES = 8
DENOM_ROWS = 16
V_ROWS = HEAD_DIM + DENOM_ROWS
HEADS_PER_BLOCK = LANES // HEAD_DIM
N_HEAD_BLOCKS = N_HEADS // HEADS_PER_BLOCK
MIX_W = N_HEADS * HEAD_DIM
QKV_W = 9 * MIX_W
ATT_T = 256
NEG = -1e30
SB_EXIT = -110.0
LOG2E = math.log2(math.e)
N_BIAS_COLS = 3
BOUND_SLACK = 1.01
MAX_EXPONENT = 100.0
SKIP_BITS = 150.0
ITEMS_PER_TRIP = 16
BUILD_UNROLL = 2
STAGE_DEPTH = 2
VMEM_LIMIT = 56 * 1024 * 1024


def _dot(a, b, precision=None):
    return jnp.dot(a, b, precision=precision, preferred_element_type=F32)


def _layer_norm(y, g, b):
    mu = jnp.mean(y, axis=-1, keepdims=True)
    yc = y - mu
    var = jnp.mean(yc * yc, axis=-1, keepdims=True)
    return yc * lax.rsqrt(var + LN_EPS) * g + b


def _ffn_ln_kernel(x_ref, wg_ref, wu_ref, wd_ref, lg_ref, lb_ref, o_ref, hid_ref, *, alpha, tf):
    x = x_ref[...]
    xb = x.astype(BF16)
    for c in range(hid_ref.shape[1] // tf):
        cols = slice(c * tf, (c + 1) * tf)
        gate = _dot(xb, wg_ref[:, cols])
        up = _dot(xb, wu_ref[:, cols])
        hid_ref[:, cols] = (gate * jax.nn.sigmoid(gate) * up).astype(BF16)
    y = alpha * x + 0.5 * _dot(hid_ref[...], wd_ref[...])
    o_ref[...] = _layer_norm(y, lg_ref[...], lb_ref[...])


def _ffn_ln(x, wg, wu, wd, lg, lb, l, j, alpha, tm=512, tf=256):
    T, D = x.shape
    FF = wg.shape[-1]
    resident = pl.Buffered(1)
    return pl.pallas_call(
        functools.partial(_ffn_ln_kernel, alpha=alpha, tf=tf),
        grid=(T // tm,),
        in_specs=[
            pl.BlockSpec((tm, D), lambda i: (i, 0)),
            pl.BlockSpec((None, None, D, FF), lambda i: (l, j, 0, 0), pipeline_mode=resident),
            pl.BlockSpec((None, None, D, FF), lambda i: (l, j, 0, 0), pipeline_mode=resident),
            pl.BlockSpec((None, None, FF, D), lambda i: (l, j, 0, 0), pipeline_mode=resident),
            pl.BlockSpec((1, D), lambda i: (0, 0)),
            pl.BlockSpec((1, D), lambda i: (0, 0)),
        ],
        out_specs=pl.BlockSpec((tm, D), lambda i: (i, 0)),
        out_shape=jax.ShapeDtypeStruct((T, D), F32),
        scratch_shapes=[pltpu.VMEM((tm, FF), BF16)],
        compiler_params=pltpu.CompilerParams(
            dimension_semantics=("parallel",), vmem_limit_bytes=VMEM_LIMIT),
        name="ffn_ln",
    )(x, wg, wu, wd, lg, lb)


def _proj_kernel(x_ref, w_ref, o_ref, *, tn):
    xb = x_ref[...].astype(BF16)
    for c in range(o_ref.shape[1] // tn):
        cols = slice(c * tn, (c + 1) * tn)
        o_ref[:, cols] = _dot(xb, w_ref[:, cols]).astype(o_ref.dtype)


def _qkv_proj(x, w_in, l, tm=512, tn=1152):
    T, D = x.shape
    return pl.pallas_call(
        functools.partial(_proj_kernel, tn=tn),
        grid=(T // tm,),
        in_specs=[
            pl.BlockSpec((tm, D), lambda i: (i, 0)),
            pl.BlockSpec((None, D, QKV_W), lambda i: (l, 0, 0), pipeline_mode=pl.Buffered(1)),
        ],
        out_specs=pl.BlockSpec((tm, QKV_W), lambda i: (i, 0)),
        out_shape=jax.ShapeDtypeStruct((T, QKV_W), BF16),
        compiler_params=pltpu.CompilerParams(
            dimension_semantics=("parallel",), vmem_limit_bytes=VMEM_LIMIT),
        name="qkv_proj",
    )(x, w_in)


def _lane_iota():
    return lax.broadcasted_iota(jnp.int32, (1, LANES), 1)


def _head_lanes(hh):
    lane = _lane_iota()
    return (lane >= hh * HEAD_DIM) & (lane < (hh + 1) * HEAD_DIM)


def _blk(j):
    return pl.ds(pl.multiple_of(j * ATT_T, ATT_T), ATT_T)


def _key_query_iotas():
    krow = lax.broadcasted_iota(jnp.int32, (ATT_T, ATT_T), 0)
    qcol = lax.broadcasted_iota(jnp.int32, (ATT_T, ATT_T), 1)
    return krow, qcol


def _build_softmax_operands(tab_ref, hb, k_ref, v_ref, kp_ref, vt_ref, kmax_ref, *, n_blocks, scale):
    lane = _lane_iota()
    pos = lax.broadcasted_iota(jnp.int32, (ATT_T, LANES), 0).astype(F32)
    bias_cols = []
    for hh in range(HEADS_PER_BLOCK):
        b = pos * tab_ref[hb * HEADS_PER_BLOCK + hh]
        lo = (1 - hh) * HEAD_DIM
        cols = jnp.zeros_like(b)
        for i in range(N_BIAS_COLS):
            piece = b.astype(BF16).astype(F32)
            cols = jnp.where(lane == lo + i, piece, cols)
            b = b - piece
        bias_cols.append(cols)
    kmax_ref[...] = jnp.zeros_like(kmax_ref)
    assert HEADS_PER_BLOCK == 2
    head_r = jnp.where(lax.broadcasted_iota(jnp.int32, (LANES, LANES), 0) < HEAD_DIM, 0, 1)
    head_c = jnp.where(lax.broadcasted_iota(jnp.int32, (LANES, LANES), 1) < HEAD_DIM, 0, 1)
    same_head = jnp.where(head_r == head_c, 1.0, 0.0).astype(BF16)

    def body(n, c):
        kf = k_ref[_blk(n), :].astype(F32) * (scale * LOG2E)
        v_t = v_ref[_blk(n), :].astype(F32).T.astype(BF16)
        for hh in range(HEADS_PER_BLOCK):
            kp_ref[hh, _blk(n), :] = jnp.where(_head_lanes(hh), kf, bias_cols[hh]).astype(BF16)
            vt_ref[hh, 0:HEAD_DIM, _blk(n)] = v_t[hh * HEAD_DIM:(hh + 1) * HEAD_DIM, :]
            vt_ref[hh, HEAD_DIM:V_ROWS, _blk(n)] = jnp.ones((DENOM_ROWS, ATT_T), BF16)
        norm2 = _dot((kf * kf).astype(BF16), same_head)
        kmax_ref[...] = jnp.maximum(kmax_ref[...], jnp.max(norm2, axis=0, keepdims=True))
        return c

    lax.fori_loop(0, n_blocks, body, 0, unroll=BUILD_UNROLL)


def _chain_qt(c, hh, seg_lo, seg_width, q_t, qt_ref):
    row = lax.broadcasted_iota(jnp.int32, (LANES, ATT_T), 0)
    lo = (1 - hh) * HEAD_DIM
    qsel = jnp.where((row >= seg_lo) & (row < seg_lo + seg_width), q_t, 0.0)
    qt_ref[c] = jnp.where((row >= lo) & (row < lo + N_BIAS_COLS), 1.0, qsel).astype(BF16)
    return qsel, jnp.sum(qsel * qsel, axis=0, keepdims=True)


def _stage_scores(c, hh, n, slot, kp_ref, qt_ref, sbuf_ref):
    sbuf_ref[c * STAGE_DEPTH + slot] = _dot(kp_ref[hh, _blk(jnp.maximum(n, 0)), :], qt_ref[c])


def _diag_blocks(chains, qi, qn2, slope2, kp_ref, vt_ref, kmax_ref, qt_ref, ref_ref, acc_ref, sbuf_ref):
    krow, qcol = _key_query_iotas()
    causal = krow <= qcol
    r = lax.broadcasted_iota(jnp.int32, (1, ATT_T), 1).astype(F32)
    scores = [_dot(kp_ref[hh, _blk(qi), :], qt_ref[c]) for c, hh in enumerate(chains)]
    for slot in range(STAGE_DEPTH):
        for c, hh in enumerate(chains):
            _stage_scores(c, hh, qi - 1 - slot, slot, kp_ref, qt_ref, sbuf_ref)
    gaps = []
    for c, hh in enumerate(chains):
        s = jnp.where(causal, scores[c], NEG)
        m = jnp.max(s, axis=0, keepdims=True)
        kmax2 = kmax_ref[:, hh * HEAD_DIM:hh * HEAD_DIM + 1]
        bound = jnp.sqrt(qn2[c] * kmax2) * BOUND_SLACK + slope2[hh] * r
        ref_ref[c] = m
        acc_ref[c] = _dot(vt_ref[hh, :, _blk(qi)], jnp.exp2(s - m).astype(BF16))
        gaps.append(jnp.max(bound - m))
    return gaps


def _first_block(gap, inv_blk, qi):
    reach = jnp.minimum((gap + SKIP_BITS) * inv_blk, 1e6).astype(jnp.int32)
    return jnp.maximum(qi - 1 - reach, 0)


def _fixed_ref_step(c, hh, n, slot, off, kp_ref, vt_ref, qt_ref, ref_ref, acc_ref, sbuf_ref):
    p = jnp.exp2(sbuf_ref[c * STAGE_DEPTH + slot] - (ref_ref[c] - off)).astype(BF16)
    _stage_scores(c, hh, n - STAGE_DEPTH, slot, kp_ref, qt_ref, sbuf_ref)
    acc_ref[c] += _dot(vt_ref[hh, :, _blk(jnp.maximum(n, 0))], p)


def _online_step(c, hh, n, off, kp_ref, vt_ref, qt_ref, m_ref, acc_ref):
    s = _dot(kp_ref[hh, _blk(n), :], qt_ref[c])
    m_old = m_ref[c]
    m_new = jnp.maximum(m_old, jnp.max(s, axis=0, keepdims=True) + off)
    p = jnp.exp2(s - (m_new - off)).astype(BF16)
    acc_ref[c] = acc_ref[c] * jnp.exp2(m_old - m_new) + _dot(vt_ref[hh, :, _blk(n)], p)
    m_ref[c] = m_new


def _softmax_sweep(qi, chains, gaps, first, off_fn, kp_ref, vt_ref, qt_ref, ref_ref, acc_ref, sbuf_ref,
                   split_heads):
    refs = (kp_ref, vt_ref, qt_ref, ref_ref, acc_ref)
    assert STAGE_DEPTH == 2
    widest = gaps[0]
    for g in gaps[1:]:
        widest = jnp.maximum(widest, g)
    fixed_ok = widest <= MAX_EXPONENT

    def run_fixed(top, left, heads):
        active = [(c, hh) for c, hh in enumerate(chains) if hh in heads]
        per_trip = max(STAGE_DEPTH, ITEMS_PER_TRIP // len(active))
        assert per_trip % STAGE_DEPTH == 0

        def sweep(top, trips, per_trip):
            def body(i, carry):
                for j in range(per_trip):
                    n = top - i * per_trip - j
                    off_end = jnp.where(n < 0, NEG, 0.0)
                    for c, hh in active:
                        off = off_fn(hh, jnp.maximum(n, 0)) + off_end
                        _fixed_ref_step(c, hh, n, j % STAGE_DEPTH, off, *refs, sbuf_ref)
                return carry

            lax.fori_loop(0, trips, body, 0)

        whole = left // per_trip
        sweep(top, whole, per_trip)
        if per_trip > STAGE_DEPTH:
            rest = left - whole * per_trip
            sweep(top - whole * per_trip, (rest + STAGE_DEPTH - 1) // STAGE_DEPTH, STAGE_DEPTH)
        else:
            sweep(top - whole * per_trip, left - whole * per_trip, 1)

    @pl.when(fixed_ok)
    def _fixed_reference():
        far = jnp.minimum(first[0], first[1])
        if split_heads:
            near = jnp.maximum(first[0], first[1])
            together = jnp.minimum((qi - near + 1) // 2 * 2, qi - far)
            run_fixed(qi - 1, together, (0, 1))
            for hh in range(HEADS_PER_BLOCK):
                run_fixed(qi - 1 - together, jnp.maximum(qi - together - first[hh], 0), (hh,))
        else:
            run_fixed(qi - 1, qi - far, (0, 1))

    @pl.when(jnp.logical_not(fixed_ok))
    def _running_maximum():
        def body(n, carry):
            for c, hh in enumerate(chains):
                _online_step(c, hh, n, off_fn(hh, n), *refs)
            return carry

        lax.fori_loop(0, qi, body, 0)


def _normalized(acc):
    return acc[0:HEAD_DIM, :] * (1.0 / acc[HEAD_DIM:HEAD_DIM + 1, :])


def _row_is_head0():
    return lax.broadcasted_iota(jnp.int32, (LANES, ATT_T), 0) < HEAD_DIM


def _moba_kernel(tab_ref, q_ref, k_ref, v_ref, o_ref, kp_ref, vt_ref, kmax_ref, kmean_ref, selm_ref,
                 qt_ref, ref_ref, acc_ref, sbuf_ref, *, n_blocks, scale):
    hb = pl.program_id(1)
    nbp = kmean_ref.shape[0]
    _build_softmax_operands(tab_ref, hb, k_ref, v_ref, kp_ref, vt_ref, kmax_ref, n_blocks=n_blocks, scale=scale)
    kmean_ref[...] = jnp.zeros_like(kmean_ref)

    def block_mean(n, c):
        kmean_ref[pl.ds(n, 1), :] = jnp.mean(k_ref[_blk(n), :].astype(F32), axis=0, keepdims=True)
        return c

    lax.fori_loop(0, n_blocks, block_mean, 0, unroll=BUILD_UNROLL)

    def one_tile(qi, carry):
        rows = _blk(qi)
        q_t = q_ref[rows, :].astype(F32).T
        blk_id = lax.broadcasted_iota(jnp.int32, (nbp, ATT_T), 0)
        blk_f = blk_id.astype(F32)
        chains = list(range(HEADS_PER_BLOCK))
        slope2 = [tab_ref[hb * HEADS_PER_BLOCK + hh] for hh in chains]
        qn2 = []
        for hh in chains:
            hm = _head_lanes(hh)
            qsel, norm2 = _chain_qt(hh, hh, hh * HEAD_DIM, HEAD_DIM, q_t, qt_ref)
            qn2.append(norm2)

            gate = _dot(jnp.where(hm, kmean_ref[...], 0.0), qsel, precision=lax.Precision.HIGHEST)
            g = jnp.where(blk_id < qi, gate, -jnp.inf)
            sel = jnp.zeros(g.shape, dtype=jnp.bool_)
            for _ in range(MOBA_TOPK):
                mx = jnp.max(g, axis=0, keepdims=True)
                pick = jnp.min(jnp.where(g == mx, blk_f, float(nbp)), axis=0, keepdims=True)
                hit = blk_f == pick
                sel = sel | (hit & (mx > -jnp.inf))
                g = jnp.where(hit, -jnp.inf, g)
            selm_ref[hh] = jnp.where(sel, 0.0, NEG)

        gaps = _diag_blocks(chains, qi, qn2, slope2, kp_ref, vt_ref, kmax_ref, qt_ref, ref_ref, acc_ref, sbuf_ref)
        first = [_first_block(gaps[hh], tab_ref[N_HEADS + hb * HEADS_PER_BLOCK + hh], qi) for hh in chains]

        def off_fn(hh, n):
            return slope2[hh] * ((n - qi) * ATT_T).astype(F32) + selm_ref[hh, pl.ds(n, 1), :]

        _softmax_sweep(qi, chains, gaps, first, off_fn, kp_ref, vt_ref, qt_ref, ref_ref, acc_ref, sbuf_ref,
                       split_heads=False)

        o = jnp.concatenate([_normalized(acc_ref[0]), _normalized(acc_ref[1])], axis=0)
        o_ref[rows, :] = o.T.astype(o_ref.dtype)
        return carry

    lax.fori_loop(0, n_blocks, one_tile, 0)


def _diff_kernel(tab_ref, lam_ref, g_ref, q_ref, k_ref, v_ref, o_ref, kp_ref, vt_ref, kmax_ref, qt_ref,
                 ref_ref, acc_ref, sbuf_ref, *, n_blocks, scale, lambda_init):
    hb = pl.program_id(1)
    _build_softmax_operands(tab_ref, hb, k_ref, v_ref, kp_ref, vt_ref, kmax_ref, n_blocks=n_blocks, scale=scale)

    lf = lam_ref[...].astype(F32)
    lam = (jnp.exp(jnp.sum(lf[0:1] * lf[1:2], axis=-1, keepdims=True))
           - jnp.exp(jnp.sum(lf[2:3] * lf[3:4], axis=-1, keepdims=True)) + lambda_init)

    def one_tile(qi, carry):
        rows = _blk(qi)
        q_t = q_ref[rows, :].astype(F32).T
        chains = [hh for hh in range(HEADS_PER_BLOCK) for _ in range(2)]
        slope2 = [tab_ref[hb * HEADS_PER_BLOCK + hh] for hh in range(HEADS_PER_BLOCK)]
        qn2 = []
        for c, hh in enumerate(chains):
            qn2.append(_chain_qt(c, hh, hh * HEAD_DIM + (c % 2) * DIFF_QK_DIM, DIFF_QK_DIM, q_t, qt_ref)[1])
        gaps = _diag_blocks(chains, qi, qn2, slope2, kp_ref, vt_ref, kmax_ref, qt_ref, ref_ref, acc_ref, sbuf_ref)
        first = [None] * HEADS_PER_BLOCK
        for c, hh in enumerate(chains):
            fb = _first_block(gaps[c], tab_ref[N_HEADS + hb * HEADS_PER_BLOCK + hh], qi)
            first[hh] = fb if first[hh] is None else jnp.minimum(first[hh], fb)

        def off_fn(hh, n):
            return slope2[hh] * ((n - qi) * ATT_T).astype(F32)

        _softmax_sweep(qi, chains, gaps, first, off_fn, kp_ref, vt_ref, qt_ref, ref_ref, acc_ref, sbuf_ref,
                       split_heads=True)

        outs = []
        for hh in range(HEADS_PER_BLOCK):
            o = _normalized(acc_ref[2 * hh]) - lam * _normalized(acc_ref[2 * hh + 1])
            ms = jnp.mean(o * o, axis=0, keepdims=True)
            outs.append(o * lax.rsqrt(ms + SUBLN_EPS))
        o = jnp.concatenate(outs, axis=0).T * (g_ref[...] * (1.0 - lambda_init))
        o_ref[rows, :] = o.astype(o_ref.dtype)
        return carry

    lax.fori_loop(0, n_blocks, one_tile, 0)


def _log_sigmoids(z):
    l = jnp.log(1.0 + jnp.exp(-jnp.abs(z)))
    return jnp.minimum(-z, 0.0) - l, jnp.minimum(z, 0.0) - l


def _suffix_sums(upper, lg):
    hi = lg.astype(BF16)
    lo = (lg - hi.astype(F32)).astype(BF16)
    return _dot(upper, hi) + _dot(upper, lo)


def _sb_blocks(blocks, k_ref, vt_ref, qt_ref, carry_ref, acc_ref):
    krow, qcol = _key_query_iotas()
    before = krow < qcol
    upper = (qcol > krow).astype(BF16)
    heads = range(HEADS_PER_BLOCK)
    items = [(n, diag, hh) for n, diag in blocks for hh in heads]
    z = [_dot(k_ref[_blk(n), :], qt_ref[hh]) for n, _, hh in items]
    log_sig, log_surv, tails = [], [], []
    for i, (n, diag, hh) in enumerate(items):
        lneg, lpos = _log_sigmoids(z[i])
        if diag:
            lneg = jnp.where(before, lneg, 0.0)
        log_surv.append(lneg)
        log_sig.append(lpos)
        tails.append(_suffix_sums(upper, lneg))
    starts_fresh = blocks[0][1]
    carry = [None if starts_fresh else carry_ref[hh] for hh in heads]
    acc = [None if starts_fresh else acc_ref[hh] for hh in heads]
    for i, (n, diag, hh) in enumerate(items):
        if diag:
            w = jnp.where(before, jnp.exp(log_sig[i] + tails[i]), 0.0)
        else:
            w = jnp.exp(log_sig[i] + (tails[i] + carry[hh]))
        part = _dot(vt_ref[hh, :, _blk(n)], w.astype(BF16))
        acc[hh] = part if acc[hh] is None else acc[hh] + part
        total = jnp.sum(log_surv[i], axis=0, keepdims=True)
        carry[hh] = total if carry[hh] is None else carry[hh] + total
    worst = None
    for hh in heads:
        acc_ref[hh] = acc[hh]
        carry_ref[hh] = carry[hh]
        top = jnp.max(carry[hh])
        worst = top if worst is None else jnp.maximum(worst, top)
    return worst


def _sb_kernel(q_ref, k_ref, v_ref, o_ref, vt_ref, qt_ref, carry_ref, acc_ref, *, n_blocks, scale):
    def transpose_values(n, c):
        v_t = v_ref[_blk(n), :].astype(F32).T.astype(BF16)
        for hh in range(HEADS_PER_BLOCK):
            vt_ref[hh, :, _blk(n)] = v_t[hh * HEAD_DIM:(hh + 1) * HEAD_DIM, :]
        return c

    lax.fori_loop(0, n_blocks, transpose_values, 0, unroll=BUILD_UNROLL)

    def one_tile(qi, carry):
        rows = _blk(qi)
        q_t = (q_ref[rows, :].astype(F32) * scale).T
        row_h0 = _row_is_head0()
        for hh in range(HEADS_PER_BLOCK):
            in_head = row_h0 if hh == 0 else jnp.logical_not(row_h0)
            qt_ref[hh] = jnp.where(in_head, q_t, 0.0).astype(BF16)
        refs = (k_ref, vt_ref, qt_ref, carry_ref, acc_ref)
        worst = lax.cond(qi > 0,
                         lambda: _sb_blocks([(qi, True), (qi - 1, False)], *refs),
                         lambda: _sb_blocks([(qi, True)], *refs))

        def cond(c):
            j, live = c
            return (j >= 0) & live

        def body(c):
            j, _ = c
            return j - 1, _sb_blocks([(j, False)], *refs) > SB_EXIT

        lax.while_loop(cond, body, (qi - 2, worst > SB_EXIT))

        o = jnp.concatenate([acc_ref[0], acc_ref[1]], axis=0)
        o_ref[rows, :] = o.T.astype(o_ref.dtype)
        return carry

    lax.fori_loop(0, n_blocks, one_tile, 0)


def _attn_specs(B, S, q_base, k_base, v_base):
    qspec = pl.BlockSpec((None, S, LANES), lambda b, h: (b, 0, q_base + h))
    kspec = pl.BlockSpec((None, S, LANES), lambda b, h: (b, 0, k_base + h))
    vspec = pl.BlockSpec((None, S, LANES), lambda b, h: (b, 0, v_base + h))
    ospec = pl.BlockSpec((None, S, LANES), lambda b, h: (b, 0, h))
    grid = (B, N_HEAD_BLOCKS)
    params = pltpu.CompilerParams(
        dimension_semantics=("parallel", "parallel"), vmem_limit_bytes=VMEM_LIMIT)
    return grid, qspec, kspec, vspec, ospec, params


_SMEM_SPEC = pl.BlockSpec(memory_space=pltpu.SMEM)


def _moba(qkv, tab):
    B, S, _ = qkv.shape
    grid, qspec, kspec, vspec, ospec, params = _attn_specs(B, S, 0, 3, 6)
    n_blocks = S // MOBA_BLOCK
    nbp = -(-n_blocks // SUBLANES) * SUBLANES
    return pl.pallas_call(
        functools.partial(_moba_kernel, n_blocks=n_blocks, scale=HEAD_DIM ** -0.5),
        grid=grid,
        in_specs=[_SMEM_SPEC, qspec, kspec, vspec],
        out_specs=ospec,
        out_shape=jax.ShapeDtypeStruct((B, S, MIX_W), BF16),
        scratch_shapes=[
            pltpu.VMEM((HEADS_PER_BLOCK, S, LANES), BF16),
            pltpu.VMEM((HEADS_PER_BLOCK, V_ROWS, S), BF16),
            pltpu.VMEM((1, LANES), F32),
            pltpu.VMEM((nbp, LANES), F32),
            pltpu.VMEM((HEADS_PER_BLOCK, nbp, ATT_T), F32),
            pltpu.VMEM((HEADS_PER_BLOCK, LANES, ATT_T), BF16),
            pltpu.VMEM((HEADS_PER_BLOCK, 1, ATT_T), F32),
            pltpu.VMEM((HEADS_PER_BLOCK, V_ROWS, ATT_T), F32),
            pltpu.VMEM((HEADS_PER_BLOCK * STAGE_DEPTH, ATT_T, ATT_T), F32),
        ],
        compiler_params=params,
        name="moba_attn",
    )(tab, qkv, qkv, qkv)


def _diff(qkv, tab, diff_lambda, subln_g, lambda_init):
    B, S, _ = qkv.shape
    grid, qspec, kspec, vspec, ospec, params = _attn_specs(B, S, 9, 12, 15)
    n_chains = 2 * HEADS_PER_BLOCK
    return pl.pallas_call(
        functools.partial(_diff_kernel, n_blocks=S // ATT_T, scale=DIFF_QK_DIM ** -0.5,
                          lambda_init=lambda_init),
        grid=grid,
        in_specs=[
            _SMEM_SPEC,
            pl.BlockSpec(diff_lambda.shape, lambda b, h: (0, 0)),
            pl.BlockSpec((1, LANES), lambda b, h: (0, 0)),
            qspec, kspec, vspec,
        ],
        out_specs=ospec,
        out_shape=jax.ShapeDtypeStruct((B, S, MIX_W), BF16),
        scratch_shapes=[
            pltpu.VMEM((HEADS_PER_BLOCK, S, LANES), BF16),
            pltpu.VMEM((HEADS_PER_BLOCK, V_ROWS, S), BF16),
            pltpu.VMEM((1, LANES), F32),
            pltpu.VMEM((n_chains, LANES, ATT_T), BF16),
            pltpu.VMEM((n_chains, 1, ATT_T), F32),
            pltpu.VMEM((n_chains, V_ROWS, ATT_T), F32),
            pltpu.VMEM((n_chains * STAGE_DEPTH, ATT_T, ATT_T), F32),
        ],
        compiler_params=params,
        name="diff_attn",
    )(tab, diff_lambda, subln_g, qkv, qkv, qkv)


def _sb(qkv):
    B, S, _ = qkv.shape
    grid, qspec, kspec, vspec, ospec, params = _attn_specs(B, S, 18, 21, 24)
    return pl.pallas_call(
        functools.partial(_sb_kernel, n_blocks=S // ATT_T, scale=HEAD_DIM ** -0.5),
        grid=grid,
        in_specs=[qspec, kspec, vspec],
        out_specs=ospec,
        out_shape=jax.ShapeDtypeStruct((B, S, MIX_W), BF16),
        scratch_shapes=[
            pltpu.VMEM((HEADS_PER_BLOCK, HEAD_DIM, S), BF16),
            pltpu.VMEM((HEADS_PER_BLOCK, LANES, ATT_T), BF16),
            pltpu.VMEM((HEADS_PER_BLOCK, 1, ATT_T), F32),
            pltpu.VMEM((HEADS_PER_BLOCK, HEAD_DIM, ATT_T), F32),
        ],
        compiler_params=params,
        name="sb_attn",
    )(qkv, qkv, qkv)


def _merge_ln_kernel(x_ref, om_ref, od_ref, os_ref, wg_ref, bg_ref, wbr_ref, wo_ref, lg_ref, lb_ref,
                     o_ref, *, alpha, d_model):
    x = x_ref[...]
    xb = x.astype(BF16)
    merged = None
    for b, ob_ref in enumerate((om_ref, od_ref, os_ref)):
        glogit = _dot(xb, wg_ref[:, b * d_model:(b + 1) * d_model]) + bg_ref[b:b + 1, :]
        term = jax.nn.sigmoid(glogit) * _dot(ob_ref[...], wbr_ref[b])
        merged = term if merged is None else merged + term
    y = alpha * x + _dot(merged.astype(BF16), wo_ref[...])
    o_ref[...] = _layer_norm(y, lg_ref[...], lb_ref[...])


def _merge_ln(x, o_m, o_d, o_s, w_gate, b_gate, w_br, w_out, lg, lb, l, alpha, tm=512):
    T, D = x.shape
    return pl.pallas_call(
        functools.partial(_merge_ln_kernel, alpha=alpha, d_model=D),
        grid=(T // tm,),
        in_specs=[
            pl.BlockSpec((tm, D), lambda i: (i, 0)),
            pl.BlockSpec((tm, MIX_W), lambda i: (i, 0)),
            pl.BlockSpec((tm, MIX_W), lambda i: (i, 0)),
            pl.BlockSpec((tm, MIX_W), lambda i: (i, 0)),
            pl.BlockSpec((None, D, N_BRANCH * D), lambda i: (l, 0, 0), pipeline_mode=pl.Buffered(1)),
            pl.BlockSpec((None, N_BRANCH, D), lambda i: (l, 0, 0)),
            pl.BlockSpec((None, N_BRANCH, MIX_W, D), lambda i: (l, 0, 0, 0), pipeline_mode=pl.Buffered(1)),
            pl.BlockSpec((None, D, D), lambda i: (l, 0, 0), pipeline_mode=pl.Buffered(1)),
            pl.BlockSpec((1, D), lambda i: (0, 0)),
            pl.BlockSpec((1, D), lambda i: (0, 0)),
        ],
        out_specs=pl.BlockSpec((tm, D), lambda i: (i, 0)),
        out_shape=jax.ShapeDtypeStruct((T, D), F32),
        compiler_params=pltpu.CompilerParams(
            dimension_semantics=("parallel",), vmem_limit_bytes=VMEM_LIMIT),
        name="merge_ln",
    )(x, o_m, o_d, o_s, w_gate, b_gate, w_br, w_out, lg, lb)


def _alibi_slopes(n):
    return (2.0 ** (-8.0 * np.arange(1, n + 1, dtype=np.float32) / n)).astype(np.float32)


def _slope_table(slopes):
    s2 = slopes.astype(np.float64) * LOG2E
    return np.concatenate([s2, 1.0 / (s2 * ATT_T)]).astype(np.float32)


def kernel(x, ln_g, ln_b, ffn_w_gate, ffn_w_up, ffn_w_down, w_in, b_gate, diff_lambda, diff_subln_g,
           w_br_moba, w_br_diff, w_br_sb, w_out):
    B, S, D = x.shape
    depth = ln_g.shape[0]
    assert S % ATT_T == 0 and ATT_T == MOBA_BLOCK
    assert w_in.shape[-1] == QKV_W + N_BRANCH * D
    alpha = (2.0 * depth) ** 0.25

    wg = ffn_w_gate.astype(BF16)
    wu = ffn_w_up.astype(BF16)
    wd = ffn_w_down.astype(BF16)
    w_qkv = w_in[:, :, :QKV_W].astype(BF16)
    w_gate = w_in[:, :, QKV_W:].astype(BF16)
    w_br = jnp.stack([w_br_moba, w_br_diff, w_br_sb], axis=1).astype(BF16)
    w_out_b = w_out.astype(BF16)
    slopes = _alibi_slopes(2 * N_HEADS)
    tab_moba = jnp.asarray(_slope_table(slopes[0::2]))
    tab_diff = jnp.asarray(_slope_table(slopes[1::2]))
    subln_g = jnp.tile(diff_subln_g.astype(F32), (1, HEADS_PER_BLOCK))

    h = x.reshape(B * S, D)
    for l in range(depth):
        lg = ln_g[l][:, None, :]
        lb = ln_b[l][:, None, :]
        h = _ffn_ln(h, wg, wu, wd, lg[0], lb[0], l, 0, alpha)
        qkv = _qkv_proj(h, w_qkv, l).reshape(B, S, QKV_W)
        lambda_init = 0.8 - 0.6 * math.exp(-0.3 * l)
        o_m = _moba(qkv, tab_moba)
        o_d = _diff(qkv, tab_diff, diff_lambda[l], subln_g[l][None, :], lambda_init)
        o_s = _sb(qkv)
        T = B * S
        h = _merge_ln(h, o_m.reshape(T, MIX_W), o_d.reshape(T, MIX_W), o_s.reshape(T, MIX_W),
                      w_gate, b_gate, w_br, w_out_b, lg[1], lb[1], l, alpha)
        h = _ffn_ln(h, wg, wu, wd, lg[2], lb[2], l, 1, alpha)
    return h.reshape(B, S, D)
```

```python
import functools
import math

import numpy as np
import jax
import jax.numpy as jnp
from jax import lax
from jax.experimental import pallas as pl
from jax.experimental.pallas import tpu as pltpu

F32 = jnp.float32
BF16 = jnp.bfloat16

HEAD_DIM = 64
N_HEADS = 6
DIFF_QK_DIM = HEAD_DIM // 2
MOBA_BLOCK = 256
MOBA_TOPK = 3
N_BRANCH = 3
LN_EPS = 1e-5
SUBLN_EPS = 1e-5

LANES = 128
SUBLANES = 8
DENOM_ROWS = 16
V_ROWS = HEAD_DIM + DENOM_ROWS
HEADS_PER_BLOCK = LANES // HEAD_DIM
N_HEAD_BLOCKS = N_HEADS // HEADS_PER_BLOCK
MIX_W = N_HEADS * HEAD_DIM
QKV_W = 9 * MIX_W
ATT_T = 256
NEG = -1e30
SB_EXIT = -110.0
LOG2E = math.log2(math.e)
N_BIAS_COLS = 3
BOUND_SLACK = 1.01
MAX_EXPONENT = 100.0
SKIP_BITS = 150.0
ITEMS_PER_TRIP = 16
LONE_HEAD_BLOCKS_PER_TRIP = 4
BUILD_UNROLL = 4
STAGE_DEPTH = 2
VMEM_LIMIT = 56 * 1024 * 1024


def _dot(a, b, precision=None):
    return jnp.dot(a, b, precision=precision, preferred_element_type=F32)


def _layer_norm(y, g, b):
    mu = jnp.mean(y, axis=-1, keepdims=True)
    yc = y - mu
    var = jnp.mean(yc * yc, axis=-1, keepdims=True)
    return yc * lax.rsqrt(var + LN_EPS) * g + b


def _ffn_ln_kernel(x_ref, wg_ref, wu_ref, wd_ref, lg_ref, lb_ref, o_ref, hid_ref, *, alpha, tf):
    x = x_ref[...]
    xb = x.astype(BF16)
    for c in range(hid_ref.shape[1] // tf):
        cols = slice(c * tf, (c + 1) * tf)
        gate = _dot(xb, wg_ref[:, cols])
        up = _dot(xb, wu_ref[:, cols])
        hid_ref[:, cols] = (gate * jax.nn.sigmoid(gate) * up).astype(BF16)
    y = alpha * x + 0.5 * _dot(hid_ref[...], wd_ref[...])
    o_ref[...] = _layer_norm(y, lg_ref[...], lb_ref[...])


def _ffn_ln(x, wg, wu, wd, lg, lb, l, j, alpha, tm=512, tf=256):
    T, D = x.shape
    FF = wg.shape[-1]
    resident = pl.Buffered(1)
    return pl.pallas_call(
        functools.partial(_ffn_ln_kernel, alpha=alpha, tf=tf),
        grid=(T // tm,),
        in_specs=[
            pl.BlockSpec((tm, D), lambda i: (i, 0)),
            pl.BlockSpec((None, None, D, FF), lambda i: (l, j, 0, 0), pipeline_mode=resident),
            pl.BlockSpec((None, None, D, FF), lambda i: (l, j, 0, 0), pipeline_mode=resident),
            pl.BlockSpec((None, None, FF, D), lambda i: (l, j, 0, 0), pipeline_mode=resident),
            pl.BlockSpec((1, D), lambda i: (0, 0)),
            pl.BlockSpec((1, D), lambda i: (0, 0)),
        ],
        out_specs=pl.BlockSpec((tm, D), lambda i: (i, 0)),
        out_shape=jax.ShapeDtypeStruct((T, D), F32),
        scratch_shapes=[pltpu.VMEM((tm, FF), BF16)],
        compiler_params=pltpu.CompilerParams(
            dimension_semantics=("parallel",), vmem_limit_bytes=VMEM_LIMIT),
        name="ffn_ln",
    )(x, wg, wu, wd, lg, lb)


def _proj_kernel(x_ref, w_ref, o_ref, *, tn):
    xb = x_ref[...].astype(BF16)
    for c in range(o_ref.shape[1] // tn):
        cols = slice(c * tn, (c + 1) * tn)
        o_ref[:, cols] = _dot(xb, w_ref[:, cols]).astype(o_ref.dtype)


def _qkv_proj(x, w_in, l, tm=512, tn=1152):
    T, D = x.shape
    return pl.pallas_call(
        functools.partial(_proj_kernel, tn=tn),
        grid=(T // tm,),
        in_specs=[
            pl.BlockSpec((tm, D), lambda i: (i, 0)),
            pl.BlockSpec((None, D, QKV_W), lambda i: (l, 0, 0), pipeline_mode=pl.Buffered(1)),
        ],
        out_specs=pl.BlockSpec((tm, QKV_W), lambda i: (i, 0)),
        out_shape=jax.ShapeDtypeStruct((T, QKV_W), BF16),
        compiler_params=pltpu.CompilerParams(
            dimension_semantics=("parallel",), vmem_limit_bytes=VMEM_LIMIT),
        name="qkv_proj",
    )(x, w_in)


def _lane_iota():
    return lax.broadcasted_iota(jnp.int32, (1, LANES), 1)


def _head_lanes(hh):
    lane = _lane_iota()
    return (lane >= hh * HEAD_DIM) & (lane < (hh + 1) * HEAD_DIM)


def _blk(j):
    return pl.ds(pl.multiple_of(j * ATT_T, ATT_T), ATT_T)


def _key_query_iotas():
    krow = lax.broadcasted_iota(jnp.int32, (ATT_T, ATT_T), 0)
    qcol = lax.broadcasted_iota(jnp.int32, (ATT_T, ATT_T), 1)
    return krow, qcol


def _build_softmax_operands(tab_ref, hb, k_ref, v_ref, kp_ref, vt_ref, kmax_ref, *, n_blocks, scale):
    lane = _lane_iota()
    pos = lax.broadcasted_iota(jnp.int32, (ATT_T, LANES), 0).astype(F32)
    bias_cols = []
    for hh in range(HEADS_PER_BLOCK):
        b = pos * tab_ref[hb * HEADS_PER_BLOCK + hh]
        lo = (1 - hh) * HEAD_DIM
        cols = jnp.zeros_like(b)
        for i in range(N_BIAS_COLS):
            piece = b.astype(BF16).astype(F32)
            cols = jnp.where(lane == lo + i, piece, cols)
            b = b - piece
        bias_cols.append(cols)
    kmax_ref[...] = jnp.zeros_like(kmax_ref)
    assert HEADS_PER_BLOCK == 2
    head_r = jnp.where(lax.broadcasted_iota(jnp.int32, (LANES, LANES), 0) < HEAD_DIM, 0, 1)
    head_c = jnp.where(lax.broadcasted_iota(jnp.int32, (LANES, LANES), 1) < HEAD_DIM, 0, 1)
    same_head = jnp.where(head_r == head_c, 1.0, 0.0).astype(BF16)

    def body(n, c):
        kf = k_ref[_blk(n), :].astype(F32) * (scale * LOG2E)
        v_t = v_ref[_blk(n), :].astype(F32).T.astype(BF16)
        for hh in range(HEADS_PER_BLOCK):
            kp_ref[hh, _blk(n), :] = jnp.where(_head_lanes(hh), kf, bias_cols[hh]).astype(BF16)
            vt_ref[hh, 0:HEAD_DIM, _blk(n)] = v_t[hh * HEAD_DIM:(hh + 1) * HEAD_DIM, :]
            vt_ref[hh, HEAD_DIM:V_ROWS, _blk(n)] = jnp.ones((DENOM_ROWS, ATT_T), BF16)
        norm2 = _dot((kf * kf).astype(BF16), same_head)
        kmax_ref[...] = jnp.maximum(kmax_ref[...], jnp.max(norm2, axis=0, keepdims=True))
        return c

    lax.fori_loop(0, n_blocks, body, 0, unroll=BUILD_UNROLL)


def _chain_qt(c, hh, seg_lo, seg_width, q_t, qt_ref):
    row = lax.broadcasted_iota(jnp.int32, (LANES, ATT_T), 0)
    lo = (1 - hh) * HEAD_DIM
    qsel = jnp.where((row >= seg_lo) & (row < seg_lo + seg_width), q_t, 0.0)
    qt_ref[c] = jnp.where((row >= lo) & (row < lo + N_BIAS_COLS), 1.0, qsel).astype(BF16)
    return qsel, jnp.sum(qsel * qsel, axis=0, keepdims=True)


def _stage_scores(c, hh, n, slot, kp_ref, qt_ref, sbuf_ref):
    sbuf_ref[c * STAGE_DEPTH + slot] = _dot(kp_ref[hh, _blk(jnp.maximum(n, 0)), :], qt_ref[c])


def _diag_blocks(chains, qi, qn2, slope2, kp_ref, vt_ref, kmax_ref, qt_ref, ref_ref, acc_ref, sbuf_ref):
    krow, qcol = _key_query_iotas()
    causal = krow <= qcol
    r = lax.broadcasted_iota(jnp.int32, (1, ATT_T), 1).astype(F32)
    scores = [_dot(kp_ref[hh, _blk(qi), :], qt_ref[c]) for c, hh in enumerate(chains)]
    for slot in range(STAGE_DEPTH):
        for c, hh in enumerate(chains):
            _stage_scores(c, hh, qi - 1 - slot, slot, kp_ref, qt_ref, sbuf_ref)
    gaps = []
    for c, hh in enumerate(chains):
        s = jnp.where(causal, scores[c], NEG)
        m = jnp.max(s, axis=0, keepdims=True)
        kmax2 = kmax_ref[:, hh * HEAD_DIM:hh * HEAD_DIM + 1]
        bound = jnp.sqrt(qn2[c] * kmax2) * BOUND_SLACK + slope2[hh] * r
        ref_ref[c] = m
        acc_ref[c] = _dot(vt_ref[hh, :, _blk(qi)], jnp.exp2(s - m).astype(BF16))
        gaps.append(jnp.max(bound - m))
    return gaps


def _first_block(gap, inv_blk, qi):
    reach = jnp.minimum((gap + SKIP_BITS) * inv_blk, 1e6).astype(jnp.int32)
    return jnp.maximum(qi - 1 - reach, 0)


def _fixed_ref_step(c, hh, n, slot, off, kp_ref, vt_ref, qt_ref, ref_ref, acc_ref, sbuf_ref):
    p = jnp.exp2(sbuf_ref[c * STAGE_DEPTH + slot] - (ref_ref[c] - off)).astype(BF16)
    _stage_scores(c, hh, n - STAGE_DEPTH, slot, kp_ref, qt_ref, sbuf_ref)
    acc_ref[c] += _dot(vt_ref[hh, :, _blk(jnp.maximum(n, 0))], p)


def _online_step(c, hh, n, off, kp_ref, vt_ref, qt_ref, m_ref, acc_ref):
    s = _dot(kp_ref[hh, _blk(n), :], qt_ref[c])
    m_old = m_ref[c]
    m_new = jnp.maximum(m_old, jnp.max(s, axis=0, keepdims=True) + off)
    p = jnp.exp2(s - (m_new - off)).astype(BF16)
    acc_ref[c] = acc_ref[c] * jnp.exp2(m_old - m_new) + _dot(vt_ref[hh, :, _blk(n)], p)
    m_ref[c] = m_new


def _softmax_sweep(qi, chains, gaps, first, off_fn, kp_ref, vt_ref, qt_ref, ref_ref, acc_ref, sbuf_ref,
                   split_heads):
    refs = (kp_ref, vt_ref, qt_ref, ref_ref, acc_ref)
    assert STAGE_DEPTH == 2
    widest = gaps[0]
    for g in gaps[1:]:
        widest = jnp.maximum(widest, g)
    fixed_ok = widest <= MAX_EXPONENT

    def run_fixed(top, left, heads):
        active = [(c, hh) for c, hh in enumerate(chains) if hh in heads]
        per_trip = max(STAGE_DEPTH, ITEMS_PER_TRIP // len(active))
        if len(heads) == 1:
            per_trip = min(per_trip, LONE_HEAD_BLOCKS_PER_TRIP)
        assert per_trip % STAGE_DEPTH == 0

        def sweep(top, trips, per_trip):
            def body(i, carry):
                for j in range(per_trip):
                    n = top - i * per_trip - j
                    off_end = jnp.where(n < 0, NEG, 0.0)
                    for c, hh in active:
                        off = off_fn(hh, jnp.maximum(n, 0)) + off_end
                        _fixed_ref_step(c, hh, n, j % STAGE_DEPTH, off, *refs, sbuf_ref)
                return carry

            lax.fori_loop(0, trips, body, 0)

        whole = left // per_trip
        sweep(top, whole, per_trip)
        if per_trip > STAGE_DEPTH:
            rest = left - whole * per_trip
            sweep(top - whole * per_trip, (rest + STAGE_DEPTH - 1) // STAGE_DEPTH, STAGE_DEPTH)
        else:
            sweep(top - whole * per_trip, left - whole * per_trip, 1)

    @pl.when(fixed_ok)
    def _fixed_reference():
        far = jnp.minimum(first[0], first[1])
        if split_heads:
            near = jnp.maximum(first[0], first[1])
            together = jnp.minimum((qi - near + 1) // 2 * 2, qi - far)
            run_fixed(qi - 1, together, (0, 1))
            for hh in range(HEADS_PER_BLOCK):
                run_fixed(qi - 1 - together, jnp.maximum(qi - together - first[hh], 0), (hh,))
        else:
            run_fixed(qi - 1, qi - far, (0, 1))

    @pl.when(jnp.logical_not(fixed_ok))
    def _running_maximum():
        def body(n, carry):
            for c, hh in enumerate(chains):
                _online_step(c, hh, n, off_fn(hh, n), *refs)
            return carry

        lax.fori_loop(0, qi, body, 0)


def _normalized(acc):
    return acc[0:HEAD_DIM, :] * (1.0 / acc[HEAD_DIM:HEAD_DIM + 1, :])


def _row_is_head0():
    return lax.broadcasted_iota(jnp.int32, (LANES, ATT_T), 0) < HEAD_DIM


def _moba_kernel(tab_ref, q_ref, k_ref, v_ref, o_ref, kp_ref, vt_ref, kmax_ref, kmean_ref, selm_ref,
                 qt_ref, ref_ref, acc_ref, sbuf_ref, *, n_blocks, scale):
    hb = pl.program_id(1)
    nbp = kmean_ref.shape[0]
    _build_softmax_operands(tab_ref, hb, k_ref, v_ref, kp_ref, vt_ref, kmax_ref, n_blocks=n_blocks, scale=scale)
    kmean_ref[...] = jnp.zeros_like(kmean_ref)

    def block_mean(n, c):
        kmean_ref[pl.ds(n, 1), :] = jnp.mean(k_ref[_blk(n), :].astype(F32), axis=0, keepdims=True)
        return c

    lax.fori_loop(0, n_blocks, block_mean, 0, unroll=BUILD_UNROLL)

    def one_tile(qi, carry):
        rows = _blk(qi)
        q_t = q_ref[rows, :].astype(F32).T
        blk_id = lax.broadcasted_iota(jnp.int32, (nbp, ATT_T), 0)
        blk_f = blk_id.astype(F32)
        chains = list(range(HEADS_PER_BLOCK))
        slope2 = [tab_ref[hb * HEADS_PER_BLOCK + hh] for hh in chains]
        qn2 = []
        for hh in chains:
            hm = _head_lanes(hh)
            qsel, norm2 = _chain_qt(hh, hh, hh * HEAD_DIM, HEAD_DIM, q_t, qt_ref)
            qn2.append(norm2)

            gate = _dot(jnp.where(hm, kmean_ref[...], 0.0), qsel, precision=lax.Precision.HIGHEST)
            g = jnp.where(blk_id < qi, gate, -jnp.inf)
            sel = jnp.zeros(g.shape, dtype=jnp.bool_)
            for _ in range(MOBA_TOPK):
                mx = jnp.max(g, axis=0, keepdims=True)
                pick = jnp.min(jnp.where(g == mx, blk_f, float(nbp)), axis=0, keepdims=True)
                hit = blk_f == pick
                sel = sel | (hit & (mx > -jnp.inf))
                g = jnp.where(hit, -jnp.inf, g)
            selm_ref[hh] = jnp.where(sel, 0.0, NEG)

        gaps = _diag_blocks(chains, qi, qn2, slope2, kp_ref, vt_ref, kmax_ref, qt_ref, ref_ref, acc_ref, sbuf_ref)
        first = [_first_block(gaps[hh], tab_ref[N_HEADS + hb * HEADS_PER_BLOCK + hh], qi) for hh in chains]

        def off_fn(hh, n):
            return slope2[hh] * jnp.asarray((n - qi) * ATT_T, F32) + selm_ref[hh, pl.ds(n, 1), :]

        _softmax_sweep(qi, chains, gaps, first, off_fn, kp_ref, vt_ref, qt_ref, ref_ref, acc_ref, sbuf_ref,
                       split_heads=False)

        o = jnp.concatenate([_normalized(acc_ref[0]), _normalized(acc_ref[1])], axis=0)
        o_ref[rows, :] = o.T.astype(o_ref.dtype)
        return carry

    lax.fori_loop(0, n_blocks, one_tile, 0)


def _diff_kernel(tab_ref, lam_ref, g_ref, q_ref, k_ref, v_ref, o_ref, kp_ref, vt_ref, kmax_ref, qt_ref,
                 ref_ref, acc_ref, sbuf_ref, *, n_blocks, scale, lambda_init):
    hb = pl.program_id(1)
    _build_softmax_operands(tab_ref, hb, k_ref, v_ref, kp_ref, vt_ref, kmax_ref, n_blocks=n_blocks, scale=scale)

    lf = lam_ref[...].astype(F32)
    lam = (jnp.exp(jnp.sum(lf[0:1] * lf[1:2], axis=-1, keepdims=True))
           - jnp.exp(jnp.sum(lf[2:3] * lf[3:4], axis=-1, keepdims=True)) + lambda_init)

    def one_tile(qi, carry):
        rows = _blk(qi)
        q_t = q_ref[rows, :].astype(F32).T
        chains = [hh for hh in range(HEADS_PER_BLOCK) for _ in range(2)]
        slope2 = [tab_ref[hb * HEADS_PER_BLOCK + hh] for hh in range(HEADS_PER_BLOCK)]
        qn2 = []
        for c, hh in enumerate(chains):
            qn2.append(_chain_qt(c, hh, hh * HEAD_DIM + (c % 2) * DIFF_QK_DIM, DIFF_QK_DIM, q_t, qt_ref)[1])
        gaps = _diag_blocks(chains, qi, qn2, slope2, kp_ref, vt_ref, kmax_ref, qt_ref, ref_ref, acc_ref, sbuf_ref)
        first = [None] * HEADS_PER_BLOCK
        for c, hh in enumerate(chains):
            fb = _first_block(gaps[c], tab_ref[N_HEADS + hb * HEADS_PER_BLOCK + hh], qi)
            first[hh] = fb if first[hh] is None else jnp.minimum(first[hh], fb)

        def off_fn(hh, n):
            return slope2[hh] * jnp.asarray((n - qi) * ATT_T, F32)

        _softmax_sweep(qi, chains, gaps, first, off_fn, kp_ref, vt_ref, qt_ref, ref_ref, acc_ref, sbuf_ref,
                       split_heads=True)

        outs = []
        for hh in range(HEADS_PER_BLOCK):
            o = _normalized(acc_ref[2 * hh]) - lam * _normalized(acc_ref[2 * hh + 1])
            ms = jnp.mean(o * o, axis=0, keepdims=True)
            outs.append(o * lax.rsqrt(ms + SUBLN_EPS))
        o = jnp.concatenate(outs, axis=0).T * (g_ref[...] * (1.0 - lambda_init))
        o_ref[rows, :] = o.astype(o_ref.dtype)
        return carry

    lax.fori_loop(0, n_blocks, one_tile, 0)


def _log_sigmoids(z):
    l = jnp.log(1.0 + jnp.exp(-jnp.abs(z)))
    return jnp.minimum(-z, 0.0) - l, jnp.minimum(z, 0.0) - l


def _suffix_sums(upper, lg):
    hi = lg.astype(BF16)
    lo = (lg - hi.astype(F32)).astype(BF16)
    return _dot(upper, hi) + _dot(upper, lo)


def _sb_blocks(blocks, k_ref, vt_ref, qt_ref, carry_ref, acc_ref):
    krow, qcol = _key_query_iotas()
    before = krow < qcol
    upper = (qcol > krow).astype(BF16)
    heads = range(HEADS_PER_BLOCK)
    items = [(n, diag, hh) for n, diag in blocks for hh in heads]
    z = [_dot(k_ref[_blk(n), :], qt_ref[hh]) for n, _, hh in items]
    log_sig, log_surv, tails = [], [], []
    for i, (n, diag, hh) in enumerate(items):
        lneg, lpos = _log_sigmoids(z[i])
        if diag:
            lneg = jnp.where(before, lneg, 0.0)
        log_surv.append(lneg)
        log_sig.append(lpos)
        tails.append(_suffix_sums(upper, lneg))
    starts_fresh = blocks[0][1]
    carry = [None if starts_fresh else carry_ref[hh] for hh in heads]
    acc = [None if starts_fresh else acc_ref[hh] for hh in heads]
    for i, (n, diag, hh) in enumerate(items):
        if diag:
            w = jnp.where(before, jnp.exp(log_sig[i] + tails[i]), 0.0)
        else:
            w = jnp.exp(log_sig[i] + (tails[i] + carry[hh]))
        part = _dot(vt_ref[hh, :, _blk(n)], w.astype(BF16))
        acc[hh] = part if acc[hh] is None else acc[hh] + part
        total = jnp.sum(log_surv[i], axis=0, keepdims=True)
        carry[hh] = total if carry[hh] is None else carry[hh] + total
    worst = None
    for hh in heads:
        acc_ref[hh] = acc[hh]
        carry_ref[hh] = carry[hh]
        top = jnp.max(carry[hh])
        worst = top if worst is None else jnp.maximum(worst, top)
    return worst


def _sb_kernel(q_ref, k_ref, v_ref, o_ref, vt_ref, qt_ref, carry_ref, acc_ref, *, n_blocks, scale):
    def transpose_values(n, c):
        v_t = v_ref[_blk(n), :].astype(F32).T.astype(BF16)
        for hh in range(HEADS_PER_BLOCK):
            vt_ref[hh, :, _blk(n)] = v_t[hh * HEAD_DIM:(hh + 1) * HEAD_DIM, :]
        return c

    lax.fori_loop(0, n_blocks, transpose_values, 0, unroll=BUILD_UNROLL)

    def one_tile(qi, carry):
        rows = _blk(qi)
        q_t = (q_ref[rows, :].astype(F32) * scale).T
        row_h0 = _row_is_head0()
        for hh in range(HEADS_PER_BLOCK):
            in_head = row_h0 if hh == 0 else jnp.logical_not(row_h0)
            qt_ref[hh] = jnp.where(in_head, q_t, 0.0).astype(BF16)
        refs = (k_ref, vt_ref, qt_ref, carry_ref, acc_ref)
        worst = lax.cond(qi > 0,
                         lambda: _sb_blocks([(qi, True), (qi - 1, False)], *refs),
                         lambda: _sb_blocks([(qi, True)], *refs))

        def cond(c):
            j, live = c
            return (j >= 0) & live

        def body(c):
            j, _ = c
            return j - 1, _sb_blocks([(j, False)], *refs) > SB_EXIT

        lax.while_loop(cond, body, (qi - 2, worst > SB_EXIT))

        o = jnp.concatenate([acc_ref[0], acc_ref[1]], axis=0)
        o_ref[rows, :] = o.T.astype(o_ref.dtype)
        return carry

    lax.fori_loop(0, n_blocks, one_tile, 0)


def _attn_specs(B, S, q_base, k_base, v_base):
    qspec = pl.BlockSpec((None, S, LANES), lambda b, h: (b, 0, q_base + h))
    kspec = pl.BlockSpec((None, S, LANES), lambda b, h: (b, 0, k_base + h))
    vspec = pl.BlockSpec((None, S, LANES), lambda b, h: (b, 0, v_base + h))
    ospec = pl.BlockSpec((None, S, LANES), lambda b, h: (b, 0, h))
    grid = (B, N_HEAD_BLOCKS)
    params = pltpu.CompilerParams(
        dimension_semantics=("parallel", "parallel"), vmem_limit_bytes=VMEM_LIMIT)
    return grid, qspec, kspec, vspec, ospec, params


_SMEM_SPEC = pl.BlockSpec(memory_space=pltpu.SMEM)


def _moba(qkv, tab):
    B, S, _ = qkv.shape
    grid, qspec, kspec, vspec, ospec, params = _attn_specs(B, S, 0, 3, 6)
    n_blocks = S // MOBA_BLOCK
    nbp = -(-n_blocks // SUBLANES) * SUBLANES
    return pl.pallas_call(
        functools.partial(_moba_kernel, n_blocks=n_blocks, scale=HEAD_DIM ** -0.5),
        grid=grid,
        in_specs=[_SMEM_SPEC, qspec, kspec, vspec],
        out_specs=ospec,
        out_shape=jax.ShapeDtypeStruct((B, S, MIX_W), BF16),
        scratch_shapes=[
            pltpu.VMEM((HEADS_PER_BLOCK, S, LANES), BF16),
            pltpu.VMEM((HEADS_PER_BLOCK, V_ROWS, S), BF16),
            pltpu.VMEM((1, LANES), F32),
            pltpu.VMEM((nbp, LANES), F32),
            pltpu.VMEM((HEADS_PER_BLOCK, nbp, ATT_T), F32),
            pltpu.VMEM((HEADS_PER_BLOCK, LANES, ATT_T), BF16),
            pltpu.VMEM((HEADS_PER_BLOCK, 1, ATT_T), F32),
            pltpu.VMEM((HEADS_PER_BLOCK, V_ROWS, ATT_T), F32),
            pltpu.VMEM((HEADS_PER_BLOCK * STAGE_DEPTH, ATT_T, ATT_T), F32),
        ],
        compiler_params=params,
        name="moba_attn",
    )(tab, qkv, qkv, qkv)


def _diff(qkv, tab, diff_lambda, subln_g, lambda_init):
    B, S, _ = qkv.shape
    grid, qspec, kspec, vspec, ospec, params = _attn_specs(B, S, 9, 12, 15)
    n_chains = 2 * HEADS_PER_BLOCK
    return pl.pallas_call(
        functools.partial(_diff_kernel, n_blocks=S // ATT_T, scale=DIFF_QK_DIM ** -0.5,
                          lambda_init=lambda_init),
        grid=grid,
        in_specs=[
            _SMEM_SPEC,
            pl.BlockSpec(diff_lambda.shape, lambda b, h: (0, 0)),
            pl.BlockSpec((1, LANES), lambda b, h: (0, 0)),
            qspec, kspec, vspec,
        ],
        out_specs=ospec,
        out_shape=jax.ShapeDtypeStruct((B, S, MIX_W), BF16),
        scratch_shapes=[
            pltpu.VMEM((HEADS_PER_BLOCK, S, LANES), BF16),
            pltpu.VMEM((HEADS_PER_BLOCK, V_ROWS, S), BF16),
            pltpu.VMEM((1, LANES), F32),
            pltpu.VMEM((n_chains, LANES, ATT_T), BF16),
            pltpu.VMEM((n_chains, 1, ATT_T), F32),
            pltpu.VMEM((n_chains, V_ROWS, ATT_T), F32),
            pltpu.VMEM((n_chains * STAGE_DEPTH, ATT_T, ATT_T), F32),
        ],
        compiler_params=params,
        name="diff_attn",
    )(tab, diff_lambda, subln_g, qkv, qkv, qkv)


def _sb(qkv):
    B, S, _ = qkv.shape
    grid, qspec, kspec, vspec, ospec, params = _attn_specs(B, S, 18, 21, 24)
    return pl.pallas_call(
        functools.partial(_sb_kernel, n_blocks=S // ATT_T, scale=HEAD_DIM ** -0.5),
        grid=grid,
        in_specs=[qspec, kspec, vspec],
        out_specs=ospec,
        out_shape=jax.ShapeDtypeStruct((B, S, MIX_W), BF16),
        scratch_shapes=[
            pltpu.VMEM((HEADS_PER_BLOCK, HEAD_DIM, S), BF16),
            pltpu.VMEM((HEADS_PER_BLOCK, LANES, ATT_T), BF16),
            pltpu.VMEM((HEADS_PER_BLOCK, 1, ATT_T), F32),
            pltpu.VMEM((HEADS_PER_BLOCK, HEAD_DIM, ATT_T), F32),
        ],
        compiler_params=params,
        name="sb_attn",
    )(qkv, qkv, qkv)


def _merge_ln_kernel(x_ref, om_ref, od_ref, os_ref, wg_ref, bg_ref, wbr_ref, wo_ref, lg_ref, lb_ref,
                     o_ref, *, alpha, d_model):
    x = x_ref[...]
    xb = x.astype(BF16)
    merged = None
    for b, ob_ref in enumerate((om_ref, od_ref, os_ref)):
        glogit = _dot(xb, wg_ref[:, b * d_model:(b + 1) * d_model]) + bg_ref[b:b + 1, :]
        term = jax.nn.sigmoid(glogit) * _dot(ob_ref[...], wbr_ref[b])
        merged = term if merged is None else merged + term
    y = alpha * x + _dot(merged.astype(BF16), wo_ref[...])
    o_ref[...] = _layer_norm(y, lg_ref[...], lb_ref[...])


def _merge_ln(x, o_m, o_d, o_s, w_gate, b_gate, w_br, w_out, lg, lb, l, alpha, tm=512):
    T, D = x.shape
    return pl.pallas_call(
        functools.partial(_merge_ln_kernel, alpha=alpha, d_model=D),
        grid=(T // tm,),
        in_specs=[
            pl.BlockSpec((tm, D), lambda i: (i, 0)),
            pl.BlockSpec((tm, MIX_W), lambda i: (i, 0)),
            pl.BlockSpec((tm, MIX_W), lambda i: (i, 0)),
            pl.BlockSpec((tm, MIX_W), lambda i: (i, 0)),
            pl.BlockSpec((None, D, N_BRANCH * D), lambda i: (l, 0, 0), pipeline_mode=pl.Buffered(1)),
            pl.BlockSpec((None, N_BRANCH, D), lambda i: (l, 0, 0)),
            pl.BlockSpec((None, N_BRANCH, MIX_W, D), lambda i: (l, 0, 0, 0), pipeline_mode=pl.Buffered(1)),
            pl.BlockSpec((None, D, D), lambda i: (l, 0, 0), pipeline_mode=pl.Buffered(1)),
            pl.BlockSpec((1, D), lambda i: (0, 0)),
            pl.BlockSpec((1, D), lambda i: (0, 0)),
        ],
        out_specs=pl.BlockSpec((tm, D), lambda i: (i, 0)),
        out_shape=jax.ShapeDtypeStruct((T, D), F32),
        compiler_params=pltpu.CompilerParams(
            dimension_semantics=("parallel",), vmem_limit_bytes=VMEM_LIMIT),
        name="merge_ln",
    )(x, o_m, o_d, o_s, w_gate, b_gate, w_br, w_out, lg, lb)


def _alibi_slopes(n):
    return (2.0 ** (-8.0 * np.arange(1, n + 1, dtype=np.float32) / n)).astype(np.float32)


def _slope_table(slopes):
    s2 = slopes.astype(np.float64) * LOG2E
    return np.concatenate([s2, 1.0 / (s2 * ATT_T)]).astype(np.float32)


def kernel(x, ln_g, ln_b, ffn_w_gate, ffn_w_up, ffn_w_down, w_in, b_gate, diff_lambda, diff_subln_g,
           w_br_moba, w_br_diff, w_br_sb, w_out):
    B, S, D = x.shape
    depth = ln_g.shape[0]
    assert S % ATT_T == 0 and ATT_T == MOBA_BLOCK
    assert w_in.shape[-1] == QKV_W + N_BRANCH * D
    alpha = (2.0 * depth) ** 0.25

    wg = ffn_w_gate.astype(BF16)
    wu = ffn_w_up.astype(BF16)
    wd = ffn_w_down.astype(BF16)
    w_qkv = w_in[:, :, :QKV_W].astype(BF16)
    w_gate = w_in[:, :, QKV_W:].astype(BF16)
    w_br = jnp.stack([w_br_moba, w_br_diff, w_br_sb], axis=1).astype(BF16)
    w_out_b = w_out.astype(BF16)
    slopes = _alibi_slopes(2 * N_HEADS)
    tab_moba = jnp.asarray(_slope_table(slopes[0::2]))
    tab_diff = jnp.asarray(_slope_table(slopes[1::2]))
    subln_g = jnp.tile(diff_subln_g.astype(F32), (1, HEADS_PER_BLOCK))

    h = x.reshape(B * S, D)
    for l in range(depth):
        lg = ln_g[l][:, None, :]
        lb = ln_b[l][:, None, :]
        h = _ffn_ln(h, wg, wu, wd, lg[0], lb[0], l, 0, alpha)
        qkv = _qkv_proj(h, w_qkv, l).reshape(B, S, QKV_W)
        lambda_init = 0.8 - 0.6 * math.exp(-0.3 * l)
        o_m = _moba(qkv, tab_moba)
        o_d = _diff(qkv, tab_diff, diff_lambda[l], subln_g[l][None, :], lambda_init)
        o_s = _sb(qkv)
        T = B * S
        h = _merge_ln(h, o_m.reshape(T, MIX_W), o_d.reshape(T, MIX_W), o_s.reshape(T, MIX_W),
                      w_gate, b_gate, w_br, w_out_b, lg[1], lb[1], l, alpha)
        h = _ffn_ln(h, wg, wu, wd, lg[2], lb[2], l, 1, alpha)
    return h.reshape(B, S, D)
```

```python
import functools
import math

import numpy as np
import jax
import jax.numpy as jnp
from jax import lax
from jax.experimental import pallas as pl
from jax.experimental.pallas import tpu as pltpu

F32 = jnp.float32
BF16 = jnp.bfloat16

HEAD_DIM = 64
N_HEADS = 6
DIFF_QK_DIM = HEAD_DIM // 2
MOBA_BLOCK = 256
MOBA_TOPK = 3
N_BRANCH = 3
LN_EPS = 1e-5
SUBLN_EPS = 1e-5

LANES = 128
SUBLANES = 8
DENOM_ROWS = 16
V_ROWS = HEAD_DIM + DENOM_ROWS
HEADS_PER_BLOCK = LANES // HEAD_DIM
N_HEAD_BLOCKS = N_HEADS // HEADS_PER_BLOCK
MIX_W = N_HEADS * HEAD_DIM
QKV_W = 9 * MIX_W
ATT_T = 256
NEG = -1e30
SB_EXIT = -110.0
SB_TILES_PER_PASS = 4
LOG2E = math.log2(math.e)
N_BIAS_COLS = 3
BOUND_SLACK = 1.01
MAX_EXPONENT = 100.0
SKIP_BITS = 150.0
ITEMS_PER_TRIP = 16
LONE_HEAD_BLOCKS_PER_TRIP = 4
BUILD_UNROLL = 4
STAGE_DEPTH = 2
VMEM_LIMIT = 56 * 1024 * 1024


def _dot(a, b, precision=None):
    return jnp.dot(a, b, precision=precision, preferred_element_type=F32)


def _layer_norm(y, g, b):
    mu = jnp.mean(y, axis=-1, keepdims=True)
    yc = y - mu
    var = jnp.mean(yc * yc, axis=-1, keepdims=True)
    return yc * lax.rsqrt(var + LN_EPS) * g + b


def _ffn_ln_kernel(x_ref, wg_ref, wu_ref, wd_ref, lg_ref, lb_ref, o_ref, hid_ref, *, alpha, tf):
    x = x_ref[...]
    xb = x.astype(BF16)
    for c in range(hid_ref.shape[1] // tf):
        cols = slice(c * tf, (c + 1) * tf)
        gate = _dot(xb, wg_ref[:, cols])
        up = _dot(xb, wu_ref[:, cols])
        hid_ref[:, cols] = (gate * jax.nn.sigmoid(gate) * up).astype(BF16)
    y = alpha * x + 0.5 * _dot(hid_ref[...], wd_ref[...])
    o_ref[...] = _layer_norm(y, lg_ref[...], lb_ref[...])


def _ffn_ln(x, wg, wu, wd, lg, lb, l, j, alpha, tm=512, tf=256):
    T, D = x.shape
    FF = wg.shape[-1]
    resident = pl.Buffered(1)
    return pl.pallas_call(
        functools.partial(_ffn_ln_kernel, alpha=alpha, tf=tf),
        grid=(T // tm,),
        in_specs=[
            pl.BlockSpec((tm, D), lambda i: (i, 0)),
            pl.BlockSpec((None, None, D, FF), lambda i: (l, j, 0, 0), pipeline_mode=resident),
            pl.BlockSpec((None, None, D, FF), lambda i: (l, j, 0, 0), pipeline_mode=resident),
            pl.BlockSpec((None, None, FF, D), lambda i: (l, j, 0, 0), pipeline_mode=resident),
            pl.BlockSpec((1, D), lambda i: (0, 0)),
            pl.BlockSpec((1, D), lambda i: (0, 0)),
        ],
        out_specs=pl.BlockSpec((tm, D), lambda i: (i, 0)),
        out_shape=jax.ShapeDtypeStruct((T, D), F32),
        scratch_shapes=[pltpu.VMEM((tm, FF), BF16)],
        compiler_params=pltpu.CompilerParams(
            dimension_semantics=("parallel",), vmem_limit_bytes=VMEM_LIMIT),
        name="ffn_ln",
    )(x, wg, wu, wd, lg, lb)


def _proj_kernel(x_ref, w_ref, o_ref, *, tn):
    xb = x_ref[...].astype(BF16)
    for c in range(o_ref.shape[1] // tn):
        cols = slice(c * tn, (c + 1) * tn)
        o_ref[:, cols] = _dot(xb, w_ref[:, cols]).astype(o_ref.dtype)


def _qkv_proj(x, w_in, l, tm=512, tn=1152):
    T, D = x.shape
    return pl.pallas_call(
        functools.partial(_proj_kernel, tn=tn),
        grid=(T // tm,),
        in_specs=[
            pl.BlockSpec((tm, D), lambda i: (i, 0)),
            pl.BlockSpec((None, D, QKV_W), lambda i: (l, 0, 0), pipeline_mode=pl.Buffered(1)),
        ],
        out_specs=pl.BlockSpec((tm, QKV_W), lambda i: (i, 0)),
        out_shape=jax.ShapeDtypeStruct((T, QKV_W), BF16),
        compiler_params=pltpu.CompilerParams(
            dimension_semantics=("parallel",), vmem_limit_bytes=VMEM_LIMIT),
        name="qkv_proj",
    )(x, w_in)


def _lane_iota():
    return lax.broadcasted_iota(jnp.int32, (1, LANES), 1)


def _head_lanes(hh):
    lane = _lane_iota()
    return (lane >= hh * HEAD_DIM) & (lane < (hh + 1) * HEAD_DIM)


def _blk(j):
    return pl.ds(pl.multiple_of(j * ATT_T, ATT_T), ATT_T)


def _key_query_iotas():
    krow = lax.broadcasted_iota(jnp.int32, (ATT_T, ATT_T), 0)
    qcol = lax.broadcasted_iota(jnp.int32, (ATT_T, ATT_T), 1)
    return krow, qcol


def _build_softmax_operands(tab_ref, hb, k_ref, v_ref, kp_ref, vt_ref, kmax_ref, *, n_blocks, scale):
    lane = _lane_iota()
    pos = lax.broadcasted_iota(jnp.int32, (ATT_T, LANES), 0).astype(F32)
    bias_cols = []
    for hh in range(HEADS_PER_BLOCK):
        b = pos * tab_ref[hb * HEADS_PER_BLOCK + hh]
        lo = (1 - hh) * HEAD_DIM
        cols = jnp.zeros_like(b)
        for i in range(N_BIAS_COLS):
            piece = b.astype(BF16).astype(F32)
            cols = jnp.where(lane == lo + i, piece, cols)
            b = b - piece
        bias_cols.append(cols)
    kmax_ref[...] = jnp.zeros_like(kmax_ref)
    assert HEADS_PER_BLOCK == 2
    head_r = jnp.where(lax.broadcasted_iota(jnp.int32, (LANES, LANES), 0) < HEAD_DIM, 0, 1)
    head_c = jnp.where(lax.broadcasted_iota(jnp.int32, (LANES, LANES), 1) < HEAD_DIM, 0, 1)
    same_head = jnp.where(head_r == head_c, 1.0, 0.0).astype(BF16)

    def body(n, c):
        kf = k_ref[_blk(n), :].astype(F32) * (scale * LOG2E)
        v_t = v_ref[_blk(n), :].astype(F32).T.astype(BF16)
        for hh in range(HEADS_PER_BLOCK):
            kp_ref[hh, _blk(n), :] = jnp.where(_head_lanes(hh), kf, bias_cols[hh]).astype(BF16)
            vt_ref[hh, 0:HEAD_DIM, _blk(n)] = v_t[hh * HEAD_DIM:(hh + 1) * HEAD_DIM, :]
            vt_ref[hh, HEAD_DIM:V_ROWS, _blk(n)] = jnp.ones((DENOM_ROWS, ATT_T), BF16)
        norm2 = _dot((kf * kf).astype(BF16), same_head)
        kmax_ref[...] = jnp.maximum(kmax_ref[...], jnp.max(norm2, axis=0, keepdims=True))
        return c

    lax.fori_loop(0, n_blocks, body, 0, unroll=BUILD_UNROLL)


def _chain_qt(c, hh, seg_lo, seg_width, q_t, qt_ref):
    row = lax.broadcasted_iota(jnp.int32, (LANES, ATT_T), 0)
    lo = (1 - hh) * HEAD_DIM
    qsel = jnp.where((row >= seg_lo) & (row < seg_lo + seg_width), q_t, 0.0)
    qt_ref[c] = jnp.where((row >= lo) & (row < lo + N_BIAS_COLS), 1.0, qsel).astype(BF16)
    return qsel, jnp.sum(qsel * qsel, axis=0, keepdims=True)


def _stage_scores(c, hh, n, slot, kp_ref, qt_ref, sbuf_ref):
    sbuf_ref[c * STAGE_DEPTH + slot] = _dot(kp_ref[hh, _blk(jnp.maximum(n, 0)), :], qt_ref[c])


def _diag_blocks(chains, qi, qn2, slope2, kp_ref, vt_ref, kmax_ref, qt_ref, ref_ref, acc_ref, sbuf_ref):
    krow, qcol = _key_query_iotas()
    causal = krow <= qcol
    r = lax.broadcasted_iota(jnp.int32, (1, ATT_T), 1).astype(F32)
    scores = [_dot(kp_ref[hh, _blk(qi), :], qt_ref[c]) for c, hh in enumerate(chains)]
    for slot in range(STAGE_DEPTH):
        for c, hh in enumerate(chains):
            _stage_scores(c, hh, qi - 1 - slot, slot, kp_ref, qt_ref, sbuf_ref)
    gaps = []
    for c, hh in enumerate(chains):
        s = jnp.where(causal, scores[c], NEG)
        m = jnp.max(s, axis=0, keepdims=True)
        kmax2 = kmax_ref[:, hh * HEAD_DIM:hh * HEAD_DIM + 1]
        bound = jnp.sqrt(qn2[c] * kmax2) * BOUND_SLACK + slope2[hh] * r
        ref_ref[c] = m
        acc_ref[c] = _dot(vt_ref[hh, :, _blk(qi)], jnp.exp2(s - m).astype(BF16))
        gaps.append(jnp.max(bound - m))
    return gaps


def _first_block(gap, inv_blk, qi):
    reach = jnp.minimum((gap + SKIP_BITS) * inv_blk, 1e6).astype(jnp.int32)
    return jnp.maximum(qi - 1 - reach, 0)


def _fixed_ref_step(c, hh, n, slot, off, kp_ref, vt_ref, qt_ref, ref_ref, acc_ref, sbuf_ref):
    p = jnp.exp2(sbuf_ref[c * STAGE_DEPTH + slot] - (ref_ref[c] - off)).astype(BF16)
    _stage_scores(c, hh, n - STAGE_DEPTH, slot, kp_ref, qt_ref, sbuf_ref)
    acc_ref[c] += _dot(vt_ref[hh, :, _blk(jnp.maximum(n, 0))], p)


def _online_step(c, hh, n, off, kp_ref, vt_ref, qt_ref, m_ref, acc_ref):
    s = _dot(kp_ref[hh, _blk(n), :], qt_ref[c])
    m_old = m_ref[c]
    m_new = jnp.maximum(m_old, jnp.max(s, axis=0, keepdims=True) + off)
    p = jnp.exp2(s - (m_new - off)).astype(BF16)
    acc_ref[c] = acc_ref[c] * jnp.exp2(m_old - m_new) + _dot(vt_ref[hh, :, _blk(n)], p)
    m_ref[c] = m_new


def _softmax_sweep(qi, chains, gaps, first, off_fn, kp_ref, vt_ref, qt_ref, ref_ref, acc_ref, sbuf_ref,
                   split_heads):
    refs = (kp_ref, vt_ref, qt_ref, ref_ref, acc_ref)
    assert STAGE_DEPTH == 2
    widest = gaps[0]
    for g in gaps[1:]:
        widest = jnp.maximum(widest, g)
    fixed_ok = widest <= MAX_EXPONENT

    def run_fixed(top, left, heads):
        active = [(c, hh) for c, hh in enumerate(chains) if hh in heads]
        per_trip = max(STAGE_DEPTH, ITEMS_PER_TRIP // len(active))
        if len(heads) == 1:
            per_trip = min(per_trip, LONE_HEAD_BLOCKS_PER_TRIP)
        assert per_trip % STAGE_DEPTH == 0

        def sweep(top, trips, per_trip):
            def body(i, carry):
                for j in range(per_trip):
                    n = top - i * per_trip - j
                    off_end = jnp.where(n < 0, NEG, 0.0)
                    for c, hh in active:
                        off = off_fn(hh, jnp.maximum(n, 0)) + off_end
                        _fixed_ref_step(c, hh, n, j % STAGE_DEPTH, off, *refs, sbuf_ref)
                return carry

            lax.fori_loop(0, trips, body, 0)

        whole = left // per_trip
        sweep(top, whole, per_trip)
        if per_trip > STAGE_DEPTH:
            rest = left - whole * per_trip
            sweep(top - whole * per_trip, (rest + STAGE_DEPTH - 1) // STAGE_DEPTH, STAGE_DEPTH)
        else:
            sweep(top - whole * per_trip, left - whole * per_trip, 1)

    @pl.when(fixed_ok)
    def _fixed_reference():
        far = jnp.minimum(first[0], first[1])
        if split_heads:
            near = jnp.maximum(first[0], first[1])
            together = jnp.minimum((qi - near + 1) // 2 * 2, qi - far)
            run_fixed(qi - 1, together, (0, 1))
            for hh in range(HEADS_PER_BLOCK):
                run_fixed(qi - 1 - together, jnp.maximum(qi - together - first[hh], 0), (hh,))
        else:
            run_fixed(qi - 1, qi - far, (0, 1))

    @pl.when(jnp.logical_not(fixed_ok))
    def _running_maximum():
        def body(n, carry):
            for c, hh in enumerate(chains):
                _online_step(c, hh, n, off_fn(hh, n), *refs)
            return carry

        lax.fori_loop(0, qi, body, 0)


def _normalized(acc):
    return acc[0:HEAD_DIM, :] * (1.0 / acc[HEAD_DIM:HEAD_DIM + 1, :])


def _row_is_head0():
    return lax.broadcasted_iota(jnp.int32, (LANES, ATT_T), 0) < HEAD_DIM


def _moba_kernel(tab_ref, q_ref, k_ref, v_ref, o_ref, kp_ref, vt_ref, kmax_ref, kmean_ref, selm_ref,
                 qt_ref, ref_ref, acc_ref, sbuf_ref, *, n_blocks, scale):
    hb = pl.program_id(1)
    nbp = kmean_ref.shape[0]
    _build_softmax_operands(tab_ref, hb, k_ref, v_ref, kp_ref, vt_ref, kmax_ref, n_blocks=n_blocks, scale=scale)
    kmean_ref[...] = jnp.zeros_like(kmean_ref)

    def block_mean(n, c):
        kmean_ref[pl.ds(n, 1), :] = jnp.mean(k_ref[_blk(n), :].astype(F32), axis=0, keepdims=True)
        return c

    lax.fori_loop(0, n_blocks, block_mean, 0, unroll=BUILD_UNROLL)

    def one_tile(qi, carry):
        rows = _blk(qi)
        q_t = q_ref[rows, :].astype(F32).T
        blk_id = lax.broadcasted_iota(jnp.int32, (nbp, ATT_T), 0)
        blk_f = blk_id.astype(F32)
        chains = list(range(HEADS_PER_BLOCK))
        slope2 = [tab_ref[hb * HEADS_PER_BLOCK + hh] for hh in chains]
        qn2 = []
        for hh in chains:
            hm = _head_lanes(hh)
            qsel, norm2 = _chain_qt(hh, hh, hh * HEAD_DIM, HEAD_DIM, q_t, qt_ref)
            qn2.append(norm2)

            gate = _dot(jnp.where(hm, kmean_ref[...], 0.0), qsel, precision=lax.Precision.HIGHEST)
            g = jnp.where(blk_id < qi, gate, -jnp.inf)
            sel = jnp.zeros(g.shape, dtype=jnp.bool_)
            for _ in range(MOBA_TOPK):
                mx = jnp.max(g, axis=0, keepdims=True)
                pick = jnp.min(jnp.where(g == mx, blk_f, float(nbp)), axis=0, keepdims=True)
                hit = blk_f == pick
                sel = sel | (hit & (mx > -jnp.inf))
                g = jnp.where(hit, -jnp.inf, g)
            selm_ref[hh] = jnp.where(sel, 0.0, NEG)

        gaps = _diag_blocks(chains, qi, qn2, slope2, kp_ref, vt_ref, kmax_ref, qt_ref, ref_ref, acc_ref, sbuf_ref)
        first = [_first_block(gaps[hh], tab_ref[N_HEADS + hb * HEADS_PER_BLOCK + hh], qi) for hh in chains]

        def off_fn(hh, n):
            return slope2[hh] * jnp.asarray((n - qi) * ATT_T, F32) + selm_ref[hh, pl.ds(n, 1), :]

        _softmax_sweep(qi, chains, gaps, first, off_fn, kp_ref, vt_ref, qt_ref, ref_ref, acc_ref, sbuf_ref,
                       split_heads=False)

        o = jnp.concatenate([_normalized(acc_ref[0]), _normalized(acc_ref[1])], axis=0)
        o_ref[rows, :] = o.T.astype(o_ref.dtype)
        return carry

    lax.fori_loop(0, n_blocks, one_tile, 0)


def _diff_kernel(tab_ref, lam_ref, g_ref, q_ref, k_ref, v_ref, o_ref, kp_ref, vt_ref, kmax_ref, qt_ref,
                 ref_ref, acc_ref, sbuf_ref, *, n_blocks, scale, lambda_init):
    hb = pl.program_id(1)
    _build_softmax_operands(tab_ref, hb, k_ref, v_ref, kp_ref, vt_ref, kmax_ref, n_blocks=n_blocks, scale=scale)

    lf = lam_ref[...].astype(F32)
    lam = (jnp.exp(jnp.sum(lf[0:1] * lf[1:2], axis=-1, keepdims=True))
           - jnp.exp(jnp.sum(lf[2:3] * lf[3:4], axis=-1, keepdims=True)) + lambda_init)

    def one_tile(qi, carry):
        rows = _blk(qi)
        q_t = q_ref[rows, :].astype(F32).T
        chains = [hh for hh in range(HEADS_PER_BLOCK) for _ in range(2)]
        slope2 = [tab_ref[hb * HEADS_PER_BLOCK + hh] for hh in range(HEADS_PER_BLOCK)]
        qn2 = []
        for c, hh in enumerate(chains):
            qn2.append(_chain_qt(c, hh, hh * HEAD_DIM + (c % 2) * DIFF_QK_DIM, DIFF_QK_DIM, q_t, qt_ref)[1])
        gaps = _diag_blocks(chains, qi, qn2, slope2, kp_ref, vt_ref, kmax_ref, qt_ref, ref_ref, acc_ref, sbuf_ref)
        first = [None] * HEADS_PER_BLOCK
        for c, hh in enumerate(chains):
            fb = _first_block(gaps[c], tab_ref[N_HEADS + hb * HEADS_PER_BLOCK + hh], qi)
            first[hh] = fb if first[hh] is None else jnp.minimum(first[hh], fb)

        def off_fn(hh, n):
            return slope2[hh] * jnp.asarray((n - qi) * ATT_T, F32)

        _softmax_sweep(qi, chains, gaps, first, off_fn, kp_ref, vt_ref, qt_ref, ref_ref, acc_ref, sbuf_ref,
                       split_heads=True)

        outs = []
        for hh in range(HEADS_PER_BLOCK):
            o = _normalized(acc_ref[2 * hh]) - lam * _normalized(acc_ref[2 * hh + 1])
            ms = jnp.mean(o * o, axis=0, keepdims=True)
            outs.append(o * lax.rsqrt(ms + SUBLN_EPS))
        o = jnp.concatenate(outs, axis=0).T * (g_ref[...] * (1.0 - lambda_init))
        o_ref[rows, :] = o.astype(o_ref.dtype)
        return carry

    lax.fori_loop(0, n_blocks, one_tile, 0)


def _log_sigmoids(z):
    l = jnp.log(1.0 + jnp.exp(-jnp.abs(z)))
    return jnp.minimum(-z, 0.0) - l, jnp.minimum(z, 0.0) - l


def _suffix_sums(upper, lg):
    hi = lg.astype(BF16)
    lo = (lg - hi.astype(F32)).astype(BF16)
    return _dot(upper, hi) + _dot(upper, lo)


def _sb_pass(tiles, k_ref, vt_ref, qt_ref, carry_ref, acc_ref):
    krow, qcol = _key_query_iotas()
    before = krow < qcol
    upper = (qcol > krow).astype(BF16)
    items = [(slot * HEADS_PER_BLOCK + hh, hh, n, diag)
             for slot, blocks in tiles for n, diag in blocks for hh in range(HEADS_PER_BLOCK)]
    z = [_dot(k_ref[_blk(n), :], qt_ref[st]) for st, _, n, _ in items]
    log_sig, log_surv, tails = [], [], []
    for i, (st, hh, n, diag) in enumerate(items):
        lneg, lpos = _log_sigmoids(z[i])
        if diag:
            lneg = jnp.where(before, lneg, 0.0)
        log_surv.append(lneg)
        log_sig.append(lpos)
        tails.append(_suffix_sums(upper, lneg))
    carry, acc = {}, {}
    for slot, blocks in tiles:
        for hh in range(HEADS_PER_BLOCK):
            st = slot * HEADS_PER_BLOCK + hh
            starts_fresh = blocks[0][1]
            carry[st] = None if starts_fresh else carry_ref[st]
            acc[st] = None if starts_fresh else acc_ref[st]
    for i, (st, hh, n, diag) in enumerate(items):
        if diag:
            w = jnp.where(before, jnp.exp(log_sig[i] + tails[i]), 0.0)
        else:
            w = jnp.exp(log_sig[i] + (tails[i] + carry[st]))
        part = _dot(vt_ref[hh, :, _blk(n)], w.astype(BF16))
        acc[st] = part if acc[st] is None else acc[st] + part
        total = jnp.sum(log_surv[i], axis=0, keepdims=True)
        carry[st] = total if carry[st] is None else carry[st] + total
    worst = []
    for slot, _ in tiles:
        tops = []
        for hh in range(HEADS_PER_BLOCK):
            st = slot * HEADS_PER_BLOCK + hh
            acc_ref[st] = acc[st]
            carry_ref[st] = carry[st]
            tops.append(jnp.max(carry[st]))
        worst.append(functools.reduce(jnp.maximum, tops))
    return tuple(worst)


def _sb_kernel(q_ref, k_ref, v_ref, o_ref, vt_ref, qt_ref, carry_ref, acc_ref, *, n_blocks, scale):
    def transpose_values(n, c):
        v_t = v_ref[_blk(n), :].astype(F32).T.astype(BF16)
        for hh in range(HEADS_PER_BLOCK):
            vt_ref[hh, :, _blk(n)] = v_t[hh * HEAD_DIM:(hh + 1) * HEAD_DIM, :]
        return c

    lax.fori_loop(0, n_blocks, transpose_values, 0, unroll=BUILD_UNROLL)

    assert n_blocks % SB_TILES_PER_PASS == 0
    refs = (k_ref, vt_ref, qt_ref, carry_ref, acc_ref)
    row_h0 = _row_is_head0()

    def tile_group(g, carry):
        tiles = [g * SB_TILES_PER_PASS + slot for slot in range(SB_TILES_PER_PASS)]
        for slot, qi in enumerate(tiles):
            q_t = (q_ref[_blk(qi), :].astype(F32) * scale).T
            for hh in range(HEADS_PER_BLOCK):
                in_head = row_h0 if hh == 0 else jnp.logical_not(row_h0)
                qt_ref[slot * HEADS_PER_BLOCK + hh] = jnp.where(in_head, q_t, 0.0).astype(BF16)

        def blocks_of(qi, with_previous):
            return [(qi, True), (qi - 1, False)] if with_previous else [(qi, True)]

        worst = lax.cond(
            g > 0,
            lambda: _sb_pass([(slot, blocks_of(qi, True)) for slot, qi in enumerate(tiles)], *refs),
            lambda: _sb_pass([(slot, blocks_of(qi, slot > 0)) for slot, qi in enumerate(tiles)], *refs))

        for slot, qi in enumerate(tiles):
            def cond(c):
                j, live = c
                return (j >= 0) & live

            def body(c, slot=slot):
                j, _ = c
                return j - 1, _sb_pass([(slot, [(j, False)])], *refs)[0] > SB_EXIT

            lax.while_loop(cond, body, (qi - 2, worst[slot] > SB_EXIT))
            st = slot * HEADS_PER_BLOCK
            o = jnp.concatenate([acc_ref[st + hh] for hh in range(HEADS_PER_BLOCK)], axis=0)
            o_ref[_blk(qi), :] = o.T.astype(o_ref.dtype)
        return carry

    lax.fori_loop(0, n_blocks // SB_TILES_PER_PASS, tile_group, 0)


def _attn_specs(B, S, q_base, k_base, v_base):
    qspec = pl.BlockSpec((None, S, LANES), lambda b, h: (b, 0, q_base + h))
    kspec = pl.BlockSpec((None, S, LANES), lambda b, h: (b, 0, k_base + h))
    vspec = pl.BlockSpec((None, S, LANES), lambda b, h: (b, 0, v_base + h))
    ospec = pl.BlockSpec((None, S, LANES), lambda b, h: (b, 0, h))
    grid = (B, N_HEAD_BLOCKS)
    params = pltpu.CompilerParams(
        dimension_semantics=("parallel", "parallel"), vmem_limit_bytes=VMEM_LIMIT)
    return grid, qspec, kspec, vspec, ospec, params


_SMEM_SPEC = pl.BlockSpec(memory_space=pltpu.SMEM)


def _moba(qkv, tab):
    B, S, _ = qkv.shape
    grid, qspec, kspec, vspec, ospec, params = _attn_specs(B, S, 0, 3, 6)
    n_blocks = S // MOBA_BLOCK
    nbp = -(-n_blocks // SUBLANES) * SUBLANES
    return pl.pallas_call(
        functools.partial(_moba_kernel, n_blocks=n_blocks, scale=HEAD_DIM ** -0.5),
        grid=grid,
        in_specs=[_SMEM_SPEC, qspec, kspec, vspec],
        out_specs=ospec,
        out_shape=jax.ShapeDtypeStruct((B, S, MIX_W), BF16),
        scratch_shapes=[
            pltpu.VMEM((HEADS_PER_BLOCK, S, LANES), BF16),
            pltpu.VMEM((HEADS_PER_BLOCK, V_ROWS, S), BF16),
            pltpu.VMEM((1, LANES), F32),
            pltpu.VMEM((nbp, LANES), F32),
            pltpu.VMEM((HEADS_PER_BLOCK, nbp, ATT_T), F32),
            pltpu.VMEM((HEADS_PER_BLOCK, LANES, ATT_T), BF16),
            pltpu.VMEM((HEADS_PER_BLOCK, 1, ATT_T), F32),
            pltpu.VMEM((HEADS_PER_BLOCK, V_ROWS, ATT_T), F32),
            pltpu.VMEM((HEADS_PER_BLOCK * STAGE_DEPTH, ATT_T, ATT_T), F32),
        ],
        compiler_params=params,
        name="moba_attn",
    )(tab, qkv, qkv, qkv)


def _diff(qkv, tab, diff_lambda, subln_g, lambda_init):
    B, S, _ = qkv.shape
    grid, qspec, kspec, vspec, ospec, params = _attn_specs(B, S, 9, 12, 15)
    n_chains = 2 * HEADS_PER_BLOCK
    return pl.pallas_call(
        functools.partial(_diff_kernel, n_blocks=S // ATT_T, scale=DIFF_QK_DIM ** -0.5,
                          lambda_init=lambda_init),
        grid=grid,
        in_specs=[
            _SMEM_SPEC,
            pl.BlockSpec(diff_lambda.shape, lambda b, h: (0, 0)),
            pl.BlockSpec((1, LANES), lambda b, h: (0, 0)),
            qspec, kspec, vspec,
        ],
        out_specs=ospec,
        out_shape=jax.ShapeDtypeStruct((B, S, MIX_W), BF16),
        scratch_shapes=[
            pltpu.VMEM((HEADS_PER_BLOCK, S, LANES), BF16),
            pltpu.VMEM((HEADS_PER_BLOCK, V_ROWS, S), BF16),
            pltpu.VMEM((1, LANES), F32),
            pltpu.VMEM((n_chains, LANES, ATT_T), BF16),
            pltpu.VMEM((n_chains, 1, ATT_T), F32),
            pltpu.VMEM((n_chains, V_ROWS, ATT_T), F32),
            pltpu.VMEM((n_chains * STAGE_DEPTH, ATT_T, ATT_T), F32),
        ],
        compiler_params=params,
        name="diff_attn",
    )(tab, diff_lambda, subln_g, qkv, qkv, qkv)


def _sb(qkv):
    B, S, _ = qkv.shape
    grid, qspec, kspec, vspec, ospec, params = _attn_specs(B, S, 18, 21, 24)
    return pl.pallas_call(
        functools.partial(_sb_kernel, n_blocks=S // ATT_T, scale=HEAD_DIM ** -0.5),
        grid=grid,
        in_specs=[qspec, kspec, vspec],
        out_specs=ospec,
        out_shape=jax.ShapeDtypeStruct((B, S, MIX_W), BF16),
        scratch_shapes=[
            pltpu.VMEM((HEADS_PER_BLOCK, HEAD_DIM, S), BF16),
            pltpu.VMEM((SB_TILES_PER_PASS * HEADS_PER_BLOCK, LANES, ATT_T), BF16),
            pltpu.VMEM((SB_TILES_PER_PASS * HEADS_PER_BLOCK, 1, ATT_T), F32),
            pltpu.VMEM((SB_TILES_PER_PASS * HEADS_PER_BLOCK, HEAD_DIM, ATT_T), F32),
        ],
        compiler_params=params,
        name="sb_attn",
    )(qkv, qkv, qkv)


def _merge_ln_kernel(x_ref, om_ref, od_ref, os_ref, wg_ref, bg_ref, wbr_ref, wo_ref, lg_ref, lb_ref,
                     o_ref, *, alpha, d_model):
    x = x_ref[...]
    xb = x.astype(BF16)
    merged = None
    for b, ob_ref in enumerate((om_ref, od_ref, os_ref)):
        glogit = _dot(xb, wg_ref[:, b * d_model:(b + 1) * d_model]) + bg_ref[b:b + 1, :]
        term = jax.nn.sigmoid(glogit) * _dot(ob_ref[...], wbr_ref[b])
        merged = term if merged is None else merged + term
    y = alpha * x + _dot(merged.astype(BF16), wo_ref[...])
    o_ref[...] = _layer_norm(y, lg_ref[...], lb_ref[...])


def _merge_ln(x, o_m, o_d, o_s, w_gate, b_gate, w_br, w_out, lg, lb, l, alpha, tm=512):
    T, D = x.shape
    return pl.pallas_call(
        functools.partial(_merge_ln_kernel, alpha=alpha, d_model=D),
        grid=(T // tm,),
        in_specs=[
            pl.BlockSpec((tm, D), lambda i: (i, 0)),
            pl.BlockSpec((tm, MIX_W), lambda i: (i, 0)),
            pl.BlockSpec((tm, MIX_W), lambda i: (i, 0)),
            pl.BlockSpec((tm, MIX_W), lambda i: (i, 0)),
            pl.BlockSpec((None, D, N_BRANCH * D), lambda i: (l, 0, 0), pipeline_mode=pl.Buffered(1)),
            pl.BlockSpec((None, N_BRANCH, D), lambda i: (l, 0, 0)),
            pl.BlockSpec((None, N_BRANCH, MIX_W, D), lambda i: (l, 0, 0, 0), pipeline_mode=pl.Buffered(1)),
            pl.BlockSpec((None, D, D), lambda i: (l, 0, 0), pipeline_mode=pl.Buffered(1)),
            pl.BlockSpec((1, D), lambda i: (0, 0)),
            pl.BlockSpec((1, D), lambda i: (0, 0)),
        ],
        out_specs=pl.BlockSpec((tm, D), lambda i: (i, 0)),
        out_shape=jax.ShapeDtypeStruct((T, D), F32),
        compiler_params=pltpu.CompilerParams(
            dimension_semantics=("parallel",), vmem_limit_bytes=VMEM_LIMIT),
        name="merge_ln",
    )(x, o_m, o_d, o_s, w_gate, b_gate, w_br, w_out, lg, lb)


def _alibi_slopes(n):
    return (2.0 ** (-8.0 * np.arange(1, n + 1, dtype=np.float32) / n)).astype(np.float32)


def _slope_table(slopes):
    s2 = slopes.astype(np.float64) * LOG2E
    return np.concatenate([s2, 1.0 / (s2 * ATT_T)]).astype(np.float32)


def kernel(x, ln_g, ln_b, ffn_w_gate, ffn_w_up, ffn_w_down, w_in, b_gate, diff_lambda, diff_subln_g,
           w_br_moba, w_br_diff, w_br_sb, w_out):
    B, S, D = x.shape
    depth = ln_g.shape[0]
    assert S % ATT_T == 0 and ATT_T == MOBA_BLOCK
    assert w_in.shape[-1] == QKV_W + N_BRANCH * D
    alpha = (2.0 * depth) ** 0.25

    wg = ffn_w_gate.astype(BF16)
    wu = ffn_w_up.astype(BF16)
    wd = ffn_w_down.astype(BF16)
    w_qkv = w_in[:, :, :QKV_W].astype(BF16)
    w_gate = w_in[:, :, QKV_W:].astype(BF16)
    w_br = jnp.stack([w_br_moba, w_br_diff, w_br_sb], axis=1).astype(BF16)
    w_out_b = w_out.astype(BF16)
    slopes = _alibi_slopes(2 * N_HEADS)
    tab_moba = jnp.asarray(_slope_table(slopes[0::2]))
    tab_diff = jnp.asarray(_slope_table(slopes[1::2]))
    subln_g = jnp.tile(diff_subln_g.astype(F32), (1, HEADS_PER_BLOCK))

    h = x.reshape(B * S, D)
    for l in range(depth):
        lg = ln_g[l][:, None, :]
        lb = ln_b[l][:, None, :]
        h = _ffn_ln(h, wg, wu, wd, lg[0], lb[0], l, 0, alpha)
        qkv = _qkv_proj(h, w_qkv, l).reshape(B, S, QKV_W)
        lambda_init = 0.8 - 0.6 * math.exp(-0.3 * l)
        o_m = _moba(qkv, tab_moba)
        o_d = _diff(qkv, tab_diff, diff_lambda[l], subln_g[l][None, :], lambda_init)
        o_s = _sb(qkv)
        T = B * S
        h = _merge_ln(h, o_m.reshape(T, MIX_W), o_d.reshape(T, MIX_W), o_s.reshape(T, MIX_W),
                      w_gate, b_gate, w_br, w_out_b, lg[1], lb[1], l, alpha)
        h = _ffn_ln(h, wg, wu, wd, lg[2], lb[2], l, 1, alpha)
    return h.reshape(B, S, D)
```

```python
import functools
import math

import numpy as np
import jax
import jax.numpy as jnp
from jax import lax
from jax.experimental import pallas as pl
from jax.experimental.pallas import tpu as pltpu

F32 = jnp.float32
BF16 = jnp.bfloat16

HEAD_DIM = 64
N_HEADS = 6
DIFF_QK_DIM = HEAD_DIM // 2
MOBA_BLOCK = 256
MOBA_TOPK = 3
N_BRANCH = 3
LN_EPS = 1e-5
SUBLN_EPS = 1e-5

LANES = 128
SUBLANES = 8
DENOM_ROWS = 16
V_ROWS = HEAD_DIM + DENOM_ROWS
HEADS_PER_BLOCK = LANES // HEAD_DIM
N_HEAD_BLOCKS = N_HEADS // HEADS_PER_BLOCK
MIX_W = N_HEADS * HEAD_DIM
QKV_W = 9 * MIX_W
ATT_T = 256
NEG = -1e30
SB_EXIT = -110.0
SB_TILES_PER_PASS = 4
LOG2E = math.log2(math.e)
N_BIAS_COLS = 3
BOUND_SLACK = 1.01
MAX_EXPONENT = 100.0
SKIP_BITS = 150.0
ITEMS_PER_TRIP = 16
LONE_HEAD_BLOCKS_PER_TRIP = 4
BUILD_UNROLL = 4
STAGE_DEPTH = 2
VMEM_LIMIT = 56 * 1024 * 1024


def _dot(a, b, precision=None):
    return jnp.dot(a, b, precision=precision, preferred_element_type=F32)


def _layer_norm(y, g, b):
    mu = jnp.mean(y, axis=-1, keepdims=True)
    yc = y - mu
    var = jnp.mean(yc * yc, axis=-1, keepdims=True)
    return yc * lax.rsqrt(var + LN_EPS) * g + b


def _ffn_ln_kernel(x_ref, wg_ref, wu_ref, wd_ref, lg_ref, lb_ref, o_ref, hid_ref, *, alpha, tf):
    x = x_ref[...]
    xb = x.astype(BF16)
    for c in range(hid_ref.shape[1] // tf):
        cols = slice(c * tf, (c + 1) * tf)
        gate = _dot(xb, wg_ref[:, cols])
        up = _dot(xb, wu_ref[:, cols])
        hid_ref[:, cols] = (gate * jax.nn.sigmoid(gate) * up).astype(BF16)
    y = alpha * x + 0.5 * _dot(hid_ref[...], wd_ref[...])
    o_ref[...] = _layer_norm(y, lg_ref[...], lb_ref[...])


def _ffn_ln(x, wg, wu, wd, lg, lb, l, j, alpha, tm=512, tf=256):
    T, D = x.shape
    FF = wg.shape[-1]
    resident = pl.Buffered(1)
    return pl.pallas_call(
        functools.partial(_ffn_ln_kernel, alpha=alpha, tf=tf),
        grid=(T // tm,),
        in_specs=[
            pl.BlockSpec((tm, D), lambda i: (i, 0)),
            pl.BlockSpec((None, None, D, FF), lambda i: (l, j, 0, 0), pipeline_mode=resident),
            pl.BlockSpec((None, None, D, FF), lambda i: (l, j, 0, 0), pipeline_mode=resident),
            pl.BlockSpec((None, None, FF, D), lambda i: (l, j, 0, 0), pipeline_mode=resident),
            pl.BlockSpec((1, D), lambda i: (0, 0)),
            pl.BlockSpec((1, D), lambda i: (0, 0)),
        ],
        out_specs=pl.BlockSpec((tm, D), lambda i: (i, 0)),
        out_shape=jax.ShapeDtypeStruct((T, D), F32),
        scratch_shapes=[pltpu.VMEM((tm, FF), BF16)],
        compiler_params=pltpu.CompilerParams(
            dimension_semantics=("parallel",), vmem_limit_bytes=VMEM_LIMIT),
        name="ffn_ln",
    )(x, wg, wu, wd, lg, lb)


def _proj_kernel(x_ref, w_ref, o_ref, *, tn):
    xb = x_ref[...].astype(BF16)
    for c in range(o_ref.shape[1] // tn):
        cols = slice(c * tn, (c + 1) * tn)
        o_ref[:, cols] = _dot(xb, w_ref[:, cols]).astype(o_ref.dtype)


def _qkv_proj(x, w_in, l, tm=512, tn=1152):
    T, D = x.shape
    return pl.pallas_call(
        functools.partial(_proj_kernel, tn=tn),
        grid=(T // tm,),
        in_specs=[
            pl.BlockSpec((tm, D), lambda i: (i, 0)),
            pl.BlockSpec((None, D, QKV_W), lambda i: (l, 0, 0), pipeline_mode=pl.Buffered(1)),
        ],
        out_specs=pl.BlockSpec((tm, QKV_W), lambda i: (i, 0)),
        out_shape=jax.ShapeDtypeStruct((T, QKV_W), BF16),
        compiler_params=pltpu.CompilerParams(
            dimension_semantics=("parallel",), vmem_limit_bytes=VMEM_LIMIT),
        name="qkv_proj",
    )(x, w_in)


def _lane_iota():
    return lax.broadcasted_iota(jnp.int32, (1, LANES), 1)


def _head_lanes(hh):
    lane = _lane_iota()
    return (lane >= hh * HEAD_DIM) & (lane < (hh + 1) * HEAD_DIM)


def _blk(j):
    return pl.ds(pl.multiple_of(j * ATT_T, ATT_T), ATT_T)


def _key_query_iotas():
    krow = lax.broadcasted_iota(jnp.int32, (ATT_T, ATT_T), 0)
    qcol = lax.broadcasted_iota(jnp.int32, (ATT_T, ATT_T), 1)
    return krow, qcol


def _build_softmax_operands(tab_ref, hb, k_ref, v_ref, kp_ref, vt_ref, kmax_ref, *, n_blocks, scale):
    lane = _lane_iota()
    pos = lax.broadcasted_iota(jnp.int32, (ATT_T, LANES), 0).astype(F32)
    bias_cols = []
    for hh in range(HEADS_PER_BLOCK):
        b = pos * tab_ref[hb * HEADS_PER_BLOCK + hh]
        lo = (1 - hh) * HEAD_DIM
        cols = jnp.zeros_like(b)
        for i in range(N_BIAS_COLS):
            piece = b.astype(BF16).astype(F32)
            cols = jnp.where(lane == lo + i, piece, cols)
            b = b - piece
        bias_cols.append(cols)
    kmax_ref[...] = jnp.zeros_like(kmax_ref)
    assert HEADS_PER_BLOCK == 2
    head_r = jnp.where(lax.broadcasted_iota(jnp.int32, (LANES, LANES), 0) < HEAD_DIM, 0, 1)
    head_c = jnp.where(lax.broadcasted_iota(jnp.int32, (LANES, LANES), 1) < HEAD_DIM, 0, 1)
    same_head = jnp.where(head_r == head_c, 1.0, 0.0).astype(BF16)

    def body(n, c):
        kf = k_ref[_blk(n), :].astype(F32) * (scale * LOG2E)
        v_t = v_ref[_blk(n), :].astype(F32).T.astype(BF16)
        for hh in range(HEADS_PER_BLOCK):
            kp_ref[hh, _blk(n), :] = jnp.where(_head_lanes(hh), kf, bias_cols[hh]).astype(BF16)
            vt_ref[hh, 0:HEAD_DIM, _blk(n)] = v_t[hh * HEAD_DIM:(hh + 1) * HEAD_DIM, :]
            vt_ref[hh, HEAD_DIM:V_ROWS, _blk(n)] = jnp.ones((DENOM_ROWS, ATT_T), BF16)
        norm2 = _dot((kf * kf).astype(BF16), same_head)
        kmax_ref[...] = jnp.maximum(kmax_ref[...], jnp.max(norm2, axis=0, keepdims=True))
        return c

    lax.fori_loop(0, n_blocks, body, 0, unroll=BUILD_UNROLL)


def _chain_qt(c, hh, seg_lo, seg_width, q_t, qt_ref):
    row = lax.broadcasted_iota(jnp.int32, (LANES, ATT_T), 0)
    lo = (1 - hh) * HEAD_DIM
    qsel = jnp.where((row >= seg_lo) & (row < seg_lo + seg_width), q_t, 0.0)
    qt_ref[c] = jnp.where((row >= lo) & (row < lo + N_BIAS_COLS), 1.0, qsel).astype(BF16)
    return qsel, jnp.sum(qsel * qsel, axis=0, keepdims=True)


def _stage_scores(c, hh, n, slot, kp_ref, qt_ref, sbuf_ref):
    sbuf_ref[c * STAGE_DEPTH + slot] = _dot(kp_ref[hh, _blk(jnp.maximum(n, 0)), :], qt_ref[c])


def _diag_blocks(chains, qi, qn2, slope2, kp_ref, vt_ref, kmax_ref, qt_ref, ref_ref, acc_ref, sbuf_ref):
    krow, qcol = _key_query_iotas()
    causal = krow <= qcol
    r = lax.broadcasted_iota(jnp.int32, (1, ATT_T), 1).astype(F32)
    scores = [_dot(kp_ref[hh, _blk(qi), :], qt_ref[c]) for c, hh in enumerate(chains)]
    for slot in range(STAGE_DEPTH):
        for c, hh in enumerate(chains):
            _stage_scores(c, hh, qi - 1 - slot, slot, kp_ref, qt_ref, sbuf_ref)
    gaps = []
    for c, hh in enumerate(chains):
        s = jnp.where(causal, scores[c], NEG)
        m = jnp.max(s, axis=0, keepdims=True)
        kmax2 = kmax_ref[:, hh * HEAD_DIM:hh * HEAD_DIM + 1]
        bound = jnp.sqrt(qn2[c] * kmax2) * BOUND_SLACK + slope2[hh] * r
        ref_ref[c] = m
        acc_ref[c] = _dot(vt_ref[hh, :, _blk(qi)], jnp.exp2(s - m).astype(BF16))
        gaps.append(jnp.max(bound - m))
    return gaps


def _first_block(gap, inv_blk, qi):
    reach = jnp.minimum((gap + SKIP_BITS) * inv_blk, 1e6).astype(jnp.int32)
    return jnp.maximum(qi - 1 - reach, 0)


def _fixed_ref_step(c, hh, n, slot, off, kp_ref, vt_ref, qt_ref, ref_ref, acc_ref, sbuf_ref):
    p = jnp.exp2(sbuf_ref[c * STAGE_DEPTH + slot] - (ref_ref[c] - off)).astype(BF16)
    _stage_scores(c, hh, n - STAGE_DEPTH, slot, kp_ref, qt_ref, sbuf_ref)
    acc_ref[c] += _dot(vt_ref[hh, :, _blk(jnp.maximum(n, 0))], p)


def _online_step(c, hh, n, off, kp_ref, vt_ref, qt_ref, m_ref, acc_ref):
    s = _dot(kp_ref[hh, _blk(n), :], qt_ref[c])
    m_old = m_ref[c]
    m_new = jnp.maximum(m_old, jnp.max(s, axis=0, keepdims=True) + off)
    p = jnp.exp2(s - (m_new - off)).astype(BF16)
    acc_ref[c] = acc_ref[c] * jnp.exp2(m_old - m_new) + _dot(vt_ref[hh, :, _blk(n)], p)
    m_ref[c] = m_new


def _softmax_sweep(qi, chains, gaps, first, off_fn, kp_ref, vt_ref, qt_ref, ref_ref, acc_ref, sbuf_ref,
                   split_heads):
    refs = (kp_ref, vt_ref, qt_ref, ref_ref, acc_ref)
    assert STAGE_DEPTH == 2
    widest = gaps[0]
    for g in gaps[1:]:
        widest = jnp.maximum(widest, g)
    fixed_ok = widest <= MAX_EXPONENT

    def run_fixed(top, left, heads):
        active = [(c, hh) for c, hh in enumerate(chains) if hh in heads]
        per_trip = max(STAGE_DEPTH, ITEMS_PER_TRIP // len(active))
        if len(heads) == 1:
            per_trip = min(per_trip, LONE_HEAD_BLOCKS_PER_TRIP)
        assert per_trip % STAGE_DEPTH == 0

        def sweep(top, trips, per_trip):
            def body(i, carry):
                for j in range(per_trip):
                    n = top - i * per_trip - j
                    off_end = jnp.where(n < 0, NEG, 0.0)
                    for c, hh in active:
                        off = off_fn(hh, jnp.maximum(n, 0)) + off_end
                        _fixed_ref_step(c, hh, n, j % STAGE_DEPTH, off, *refs, sbuf_ref)
                return carry

            lax.fori_loop(0, trips, body, 0)

        whole = left // per_trip
        sweep(top, whole, per_trip)
        if per_trip > STAGE_DEPTH:
            rest = left - whole * per_trip
            sweep(top - whole * per_trip, (rest + STAGE_DEPTH - 1) // STAGE_DEPTH, STAGE_DEPTH)
        else:
            sweep(top - whole * per_trip, left - whole * per_trip, 1)

    @pl.when(fixed_ok)
    def _fixed_reference():
        far = jnp.minimum(first[0], first[1])
        if split_heads:
            near = jnp.maximum(first[0], first[1])
            together = jnp.minimum((qi - near + 1) // 2 * 2, qi - far)
            run_fixed(qi - 1, together, (0, 1))
            for hh in range(HEADS_PER_BLOCK):
                run_fixed(qi - 1 - together, jnp.maximum(qi - together - first[hh], 0), (hh,))
        else:
            run_fixed(qi - 1, qi - far, (0, 1))

    @pl.when(jnp.logical_not(fixed_ok))
    def _running_maximum():
        def body(n, carry):
            for c, hh in enumerate(chains):
                _online_step(c, hh, n, off_fn(hh, n), *refs)
            return carry

        lax.fori_loop(0, qi, body, 0)


def _normalized(acc):
    return acc[0:HEAD_DIM, :] * (1.0 / acc[HEAD_DIM:HEAD_DIM + 1, :])


def _row_is_head0():
    return lax.broadcasted_iota(jnp.int32, (LANES, ATT_T), 0) < HEAD_DIM


def _moba_kernel(tab_ref, q_ref, k_ref, v_ref, o_ref, kp_ref, vt_ref, kmax_ref, kmean_ref, selm_ref,
                 qt_ref, ref_ref, acc_ref, sbuf_ref, *, n_blocks, scale):
    hb = pl.program_id(1)
    nbp = kmean_ref.shape[0]
    _build_softmax_operands(tab_ref, hb, k_ref, v_ref, kp_ref, vt_ref, kmax_ref, n_blocks=n_blocks, scale=scale)
    kmean_ref[...] = jnp.zeros_like(kmean_ref)

    def block_mean(n, c):
        kmean_ref[pl.ds(n, 1), :] = jnp.mean(k_ref[_blk(n), :].astype(F32), axis=0, keepdims=True)
        return c

    lax.fori_loop(0, n_blocks, block_mean, 0, unroll=BUILD_UNROLL)

    def one_tile(qi, carry):
        rows = _blk(qi)
        q_t = q_ref[rows, :].astype(F32).T
        blk_id = lax.broadcasted_iota(jnp.int32, (nbp, ATT_T), 0)
        blk_f = blk_id.astype(F32)
        chains = list(range(HEADS_PER_BLOCK))
        slope2 = [tab_ref[hb * HEADS_PER_BLOCK + hh] for hh in chains]
        qn2 = []
        for hh in chains:
            hm = _head_lanes(hh)
            qsel, norm2 = _chain_qt(hh, hh, hh * HEAD_DIM, HEAD_DIM, q_t, qt_ref)
            qn2.append(norm2)

            gate = _dot(jnp.where(hm, kmean_ref[...], 0.0), qsel, precision=lax.Precision.HIGHEST)
            g = jnp.where(blk_id < qi, gate, -jnp.inf)
            sel = jnp.zeros(g.shape, dtype=jnp.bool_)
            for _ in range(MOBA_TOPK):
                mx = jnp.max(g, axis=0, keepdims=True)
                pick = jnp.min(jnp.where(g == mx, blk_f, float(nbp)), axis=0, keepdims=True)
                hit = blk_f == pick
                sel = sel | (hit & (mx > -jnp.inf))
                g = jnp.where(hit, -jnp.inf, g)
            selm_ref[hh] = jnp.where(sel, 0.0, NEG)

        gaps = _diag_blocks(chains, qi, qn2, slope2, kp_ref, vt_ref, kmax_ref, qt_ref, ref_ref, acc_ref, sbuf_ref)
        first = [_first_block(gaps[hh], tab_ref[N_HEADS + hb * HEADS_PER_BLOCK + hh], qi) for hh in chains]

        def off_fn(hh, n):
            return slope2[hh] * jnp.asarray((n - qi) * ATT_T, F32) + selm_ref[hh, pl.ds(n, 1), :]

        _softmax_sweep(qi, chains, gaps, first, off_fn, kp_ref, vt_ref, qt_ref, ref_ref, acc_ref, sbuf_ref,
                       split_heads=False)

        o = jnp.concatenate([_normalized(acc_ref[0]), _normalized(acc_ref[1])], axis=0)
        o_ref[rows, :] = o.T.astype(o_ref.dtype)
        return carry

    lax.fori_loop(0, n_blocks, one_tile, 0)


def _diff_kernel(tab_ref, lam_ref, g_ref, q_ref, k_ref, v_ref, o_ref, kp_ref, vt_ref, kmax_ref, qt_ref,
                 ref_ref, acc_ref, sbuf_ref, *, n_blocks, scale, lambda_init):
    hb = pl.program_id(1)
    _build_softmax_operands(tab_ref, hb, k_ref, v_ref, kp_ref, vt_ref, kmax_ref, n_blocks=n_blocks, scale=scale)

    lf = lam_ref[...].astype(F32)
    lam = (jnp.exp(jnp.sum(lf[0:1] * lf[1:2], axis=-1, keepdims=True))
           - jnp.exp(jnp.sum(lf[2:3] * lf[3:4], axis=-1, keepdims=True)) + lambda_init)

    def one_tile(qi, carry):
        rows = _blk(qi)
        q_t = q_ref[rows, :].astype(F32).T
        chains = [hh for hh in range(HEADS_PER_BLOCK) for _ in range(2)]
        slope2 = [tab_ref[hb * HEADS_PER_BLOCK + hh] for hh in range(HEADS_PER_BLOCK)]
        qn2 = []
        for c, hh in enumerate(chains):
            qn2.append(_chain_qt(c, hh, hh * HEAD_DIM + (c % 2) * DIFF_QK_DIM, DIFF_QK_DIM, q_t, qt_ref)[1])
        gaps = _diag_blocks(chains, qi, qn2, slope2, kp_ref, vt_ref, kmax_ref, qt_ref, ref_ref, acc_ref, sbuf_ref)
        first = [None] * HEADS_PER_BLOCK
        for c, hh in enumerate(chains):
            fb = _first_block(gaps[c], tab_ref[N_HEADS + hb * HEADS_PER_BLOCK + hh], qi)
            first[hh] = fb if first[hh] is None else jnp.minimum(first[hh], fb)

        def off_fn(hh, n):
            return slope2[hh] * jnp.asarray((n - qi) * ATT_T, F32)

        _softmax_sweep(qi, chains, gaps, first, off_fn, kp_ref, vt_ref, qt_ref, ref_ref, acc_ref, sbuf_ref,
                       split_heads=True)

        outs = []
        for hh in range(HEADS_PER_BLOCK):
            o = _normalized(acc_ref[2 * hh]) - lam * _normalized(acc_ref[2 * hh + 1])
            ms = jnp.mean(o * o, axis=0, keepdims=True)
            outs.append(o * lax.rsqrt(ms + SUBLN_EPS))
        o = jnp.concatenate(outs, axis=0).T * (g_ref[...] * (1.0 - lambda_init))
        o_ref[rows, :] = o.astype(o_ref.dtype)
        return carry

    lax.fori_loop(0, n_blocks, one_tile, 0)


def _log_sigmoids(z):
    lneg = jnp.minimum(-z, 0.0) - jnp.log(1.0 + jnp.exp(-jnp.abs(z)))
    return lneg, lneg + z


def _suffix_sums(upper, lg):
    hi = lg.astype(BF16)
    lo = (lg - hi.astype(F32)).astype(BF16)
    return _dot(upper, hi) + _dot(upper, lo)


def _sb_pass(tiles, k_ref, vt_ref, qt_ref, carry_ref, acc_ref):
    krow, qcol = _key_query_iotas()
    before = krow < qcol
    upper = (qcol > krow).astype(BF16)
    items = [(slot * HEADS_PER_BLOCK + hh, hh, n, diag)
             for slot, blocks in tiles for n, diag in blocks for hh in range(HEADS_PER_BLOCK)]
    z = [_dot(k_ref[_blk(n), :], qt_ref[st]) for st, _, n, _ in items]
    log_sig, log_surv, tails = [], [], []
    for i, (st, hh, n, diag) in enumerate(items):
        lneg, lpos = _log_sigmoids(jnp.where(before, z[i], NEG) if diag else z[i])
        log_surv.append(lneg)
        log_sig.append(lpos)
        tails.append(_suffix_sums(upper, lneg))
    carry, acc = {}, {}
    for slot, blocks in tiles:
        for hh in range(HEADS_PER_BLOCK):
            st = slot * HEADS_PER_BLOCK + hh
            starts_fresh = blocks[0][1]
            carry[st] = None if starts_fresh else carry_ref[st]
            acc[st] = None if starts_fresh else acc_ref[st]
    for i, (st, hh, n, diag) in enumerate(items):
        w = jnp.exp(log_sig[i] + (tails[i] if diag else tails[i] + carry[st]))
        part = _dot(vt_ref[hh, :, _blk(n)], w.astype(BF16))
        acc[st] = part if acc[st] is None else acc[st] + part
        total = jnp.sum(log_surv[i], axis=0, keepdims=True)
        carry[st] = total if carry[st] is None else carry[st] + total
    worst = []
    for slot, _ in tiles:
        tops = []
        for hh in range(HEADS_PER_BLOCK):
            st = slot * HEADS_PER_BLOCK + hh
            acc_ref[st] = acc[st]
            carry_ref[st] = carry[st]
            tops.append(jnp.max(carry[st]))
        worst.append(functools.reduce(jnp.maximum, tops))
    return tuple(worst)


def _sb_kernel(q_ref, k_ref, v_ref, o_ref, vt_ref, qt_ref, carry_ref, acc_ref, *, n_blocks, scale):
    def transpose_values(n, c):
        v_t = v_ref[_blk(n), :].astype(F32).T.astype(BF16)
        for hh in range(HEADS_PER_BLOCK):
            vt_ref[hh, :, _blk(n)] = v_t[hh * HEAD_DIM:(hh + 1) * HEAD_DIM, :]
        return c

    lax.fori_loop(0, n_blocks, transpose_values, 0, unroll=BUILD_UNROLL)

    assert n_blocks % SB_TILES_PER_PASS == 0
    refs = (k_ref, vt_ref, qt_ref, carry_ref, acc_ref)
    row_h0 = _row_is_head0()

    def tile_group(g, carry):
        tiles = [g * SB_TILES_PER_PASS + slot for slot in range(SB_TILES_PER_PASS)]
        for slot, qi in enumerate(tiles):
            q_t = (q_ref[_blk(qi), :].astype(F32) * scale).T
            for hh in range(HEADS_PER_BLOCK):
                in_head = row_h0 if hh == 0 else jnp.logical_not(row_h0)
                qt_ref[slot * HEADS_PER_BLOCK + hh] = jnp.where(in_head, q_t, 0.0).astype(BF16)

        def blocks_of(qi, with_previous):
            return [(qi, True), (qi - 1, False)] if with_previous else [(qi, True)]

        worst = lax.cond(
            g > 0,
            lambda: _sb_pass([(slot, blocks_of(qi, True)) for slot, qi in enumerate(tiles)], *refs),
            lambda: _sb_pass([(slot, blocks_of(qi, slot > 0)) for slot, qi in enumerate(tiles)], *refs))

        for slot, qi in enumerate(tiles):
            def cond(c):
                j, live = c
                return (j >= 0) & live

            def body(c, slot=slot):
                j, _ = c
                return j - 1, _sb_pass([(slot, [(j, False)])], *refs)[0] > SB_EXIT

            lax.while_loop(cond, body, (qi - 2, worst[slot] > SB_EXIT))
            st = slot * HEADS_PER_BLOCK
            o = jnp.concatenate([acc_ref[st + hh] for hh in range(HEADS_PER_BLOCK)], axis=0)
            o_ref[_blk(qi), :] = o.T.astype(o_ref.dtype)
        return carry

    lax.fori_loop(0, n_blocks // SB_TILES_PER_PASS, tile_group, 0)


def _attn_specs(B, S, q_base, k_base, v_base):
    qspec = pl.BlockSpec((None, S, LANES), lambda b, h: (b, 0, q_base + h))
    kspec = pl.BlockSpec((None, S, LANES), lambda b, h: (b, 0, k_base + h))
    vspec = pl.BlockSpec((None, S, LANES), lambda b, h: (b, 0, v_base + h))
    ospec = pl.BlockSpec((None, S, LANES), lambda b, h: (b, 0, h))
    grid = (B, N_HEAD_BLOCKS)
    params = pltpu.CompilerParams(
        dimension_semantics=("parallel", "parallel"), vmem_limit_bytes=VMEM_LIMIT)
    return grid, qspec, kspec, vspec, ospec, params


_SMEM_SPEC = pl.BlockSpec(memory_space=pltpu.SMEM)


def _moba(qkv, tab):
    B, S, _ = qkv.shape
    grid, qspec, kspec, vspec, ospec, params = _attn_specs(B, S, 0, 3, 6)
    n_blocks = S // MOBA_BLOCK
    nbp = -(-n_blocks // SUBLANES) * SUBLANES
    return pl.pallas_call(
        functools.partial(_moba_kernel, n_blocks=n_blocks, scale=HEAD_DIM ** -0.5),
        grid=grid,
        in_specs=[_SMEM_SPEC, qspec, kspec, vspec],
        out_specs=ospec,
        out_shape=jax.ShapeDtypeStruct((B, S, MIX_W), BF16),
        scratch_shapes=[
            pltpu.VMEM((HEADS_PER_BLOCK, S, LANES), BF16),
            pltpu.VMEM((HEADS_PER_BLOCK, V_ROWS, S), BF16),
            pltpu.VMEM((1, LANES), F32),
            pltpu.VMEM((nbp, LANES), F32),
            pltpu.VMEM((HEADS_PER_BLOCK, nbp, ATT_T), F32),
            pltpu.VMEM((HEADS_PER_BLOCK, LANES, ATT_T), BF16),
            pltpu.VMEM((HEADS_PER_BLOCK, 1, ATT_T), F32),
            pltpu.VMEM((HEADS_PER_BLOCK, V_ROWS, ATT_T), F32),
            pltpu.VMEM((HEADS_PER_BLOCK * STAGE_DEPTH, ATT_T, ATT_T), F32),
        ],
        compiler_params=params,
        name="moba_attn",
    )(tab, qkv, qkv, qkv)


def _diff(qkv, tab, diff_lambda, subln_g, lambda_init):
    B, S, _ = qkv.shape
    grid, qspec, kspec, vspec, ospec, params = _attn_specs(B, S, 9, 12, 15)
    n_chains = 2 * HEADS_PER_BLOCK
    return pl.pallas_call(
        functools.partial(_diff_kernel, n_blocks=S // ATT_T, scale=DIFF_QK_DIM ** -0.5,
                          lambda_init=lambda_init),
        grid=grid,
        in_specs=[
            _SMEM_SPEC,
            pl.BlockSpec(diff_lambda.shape, lambda b, h: (0, 0)),
            pl.BlockSpec((1, LANES), lambda b, h: (0, 0)),
            qspec, kspec, vspec,
        ],
        out_specs=ospec,
        out_shape=jax.ShapeDtypeStruct((B, S, MIX_W), BF16),
        scratch_shapes=[
            pltpu.VMEM((HEADS_PER_BLOCK, S, LANES), BF16),
            pltpu.VMEM((HEADS_PER_BLOCK, V_ROWS, S), BF16),
            pltpu.VMEM((1, LANES), F32),
            pltpu.VMEM((n_chains, LANES, ATT_T), BF16),
            pltpu.VMEM((n_chains, 1, ATT_T), F32),
            pltpu.VMEM((n_chains, V_ROWS, ATT_T), F32),
            pltpu.VMEM((n_chains * STAGE_DEPTH, ATT_T, ATT_T), F32),
        ],
        compiler_params=params,
        name="diff_attn",
    )(tab, diff_lambda, subln_g, qkv, qkv, qkv)


def _sb(qkv):
    B, S, _ = qkv.shape
    grid, qspec, kspec, vspec, ospec, params = _attn_specs(B, S, 18, 21, 24)
    return pl.pallas_call(
        functools.partial(_sb_kernel, n_blocks=S // ATT_T, scale=HEAD_DIM ** -0.5),
        grid=grid,
        in_specs=[qspec, kspec, vspec],
        out_specs=ospec,
        out_shape=jax.ShapeDtypeStruct((B, S, MIX_W), BF16),
        scratch_shapes=[
            pltpu.VMEM((HEADS_PER_BLOCK, HEAD_DIM, S), BF16),
            pltpu.VMEM((SB_TILES_PER_PASS * HEADS_PER_BLOCK, LANES, ATT_T), BF16),
            pltpu.VMEM((SB_TILES_PER_PASS * HEADS_PER_BLOCK, 1, ATT_T), F32),
            pltpu.VMEM((SB_TILES_PER_PASS * HEADS_PER_BLOCK, HEAD_DIM, ATT_T), F32),
        ],
        compiler_params=params,
        name="sb_attn",
    )(qkv, qkv, qkv)


def _merge_ln_kernel(x_ref, om_ref, od_ref, os_ref, wg_ref, bg_ref, wbr_ref, wo_ref, lg_ref, lb_ref,
                     o_ref, *, alpha, d_model):
    x = x_ref[...]
    xb = x.astype(BF16)
    merged = None
    for b, ob_ref in enumerate((om_ref, od_ref, os_ref)):
        glogit = _dot(xb, wg_ref[:, b * d_model:(b + 1) * d_model]) + bg_ref[b:b + 1, :]
        term = jax.nn.sigmoid(glogit) * _dot(ob_ref[...], wbr_ref[b])
        merged = term if merged is None else merged + term
    y = alpha * x + _dot(merged.astype(BF16), wo_ref[...])
    o_ref[...] = _layer_norm(y, lg_ref[...], lb_ref[...])


def _merge_ln(x, o_m, o_d, o_s, w_gate, b_gate, w_br, w_out, lg, lb, l, alpha, tm=512):
    T, D = x.shape
    return pl.pallas_call(
        functools.partial(_merge_ln_kernel, alpha=alpha, d_model=D),
        grid=(T // tm,),
        in_specs=[
            pl.BlockSpec((tm, D), lambda i: (i, 0)),
            pl.BlockSpec((tm, MIX_W), lambda i: (i, 0)),
            pl.BlockSpec((tm, MIX_W), lambda i: (i, 0)),
            pl.BlockSpec((tm, MIX_W), lambda i: (i, 0)),
            pl.BlockSpec((None, D, N_BRANCH * D), lambda i: (l, 0, 0), pipeline_mode=pl.Buffered(1)),
            pl.BlockSpec((None, N_BRANCH, D), lambda i: (l, 0, 0)),
            pl.BlockSpec((None, N_BRANCH, MIX_W, D), lambda i: (l, 0, 0, 0), pipeline_mode=pl.Buffered(1)),
            pl.BlockSpec((None, D, D), lambda i: (l, 0, 0), pipeline_mode=pl.Buffered(1)),
            pl.BlockSpec((1, D), lambda i: (0, 0)),
            pl.BlockSpec((1, D), lambda i: (0, 0)),
        ],
        out_specs=pl.BlockSpec((tm, D), lambda i: (i, 0)),
        out_shape=jax.ShapeDtypeStruct((T, D), F32),
        compiler_params=pltpu.CompilerParams(
            dimension_semantics=("parallel",), vmem_limit_bytes=VMEM_LIMIT),
        name="merge_ln",
    )(x, o_m, o_d, o_s, w_gate, b_gate, w_br, w_out, lg, lb)


def _alibi_slopes(n):
    return (2.0 ** (-8.0 * np.arange(1, n + 1, dtype=np.float32) / n)).astype(np.float32)


def _slope_table(slopes):
    s2 = slopes.astype(np.float64) * LOG2E
    return np.concatenate([s2, 1.0 / (s2 * ATT_T)]).astype(np.float32)


def kernel(x, ln_g, ln_b, ffn_w_gate, ffn_w_up, ffn_w_down, w_in, b_gate, diff_lambda, diff_subln_g,
           w_br_moba, w_br_diff, w_br_sb, w_out):
    B, S, D = x.shape
    depth = ln_g.shape[0]
    assert S % ATT_T == 0 and ATT_T == MOBA_BLOCK
    assert w_in.shape[-1] == QKV_W + N_BRANCH * D
    alpha = (2.0 * depth) ** 0.25

    wg = ffn_w_gate.astype(BF16)
    wu = ffn_w_up.astype(BF16)
    wd = ffn_w_down.astype(BF16)
    w_qkv = w_in[:, :, :QKV_W].astype(BF16)
    w_gate = w_in[:, :, QKV_W:].astype(BF16)
    w_br = jnp.stack([w_br_moba, w_br_diff, w_br_sb], axis=1).astype(BF16)
    w_out_b = w_out.astype(BF16)
    slopes = _alibi_slopes(2 * N_HEADS)
    tab_moba = jnp.asarray(_slope_table(slopes[0::2]))
    tab_diff = jnp.asarray(_slope_table(slopes[1::2]))
    subln_g = jnp.tile(diff_subln_g.astype(F32), (1, HEADS_PER_BLOCK))

    h = x.reshape(B * S, D)
    for l in range(depth):
        lg = ln_g[l][:, None, :]
        lb = ln_b[l][:, None, :]
        h = _ffn_ln(h, wg, wu, wd, lg[0], lb[0], l, 0, alpha)
        qkv = _qkv_proj(h, w_qkv, l).reshape(B, S, QKV_W)
        lambda_init = 0.8 - 0.6 * math.exp(-0.3 * l)
        o_m = _moba(qkv, tab_moba)
        o_d = _diff(qkv, tab_diff, diff_lambda[l], subln_g[l][None, :], lambda_init)
        o_s = _sb(qkv)
        T = B * S
        h = _merge_ln(h, o_m.reshape(T, MIX_W), o_d.reshape(T, MIX_W), o_s.reshape(T, MIX_W),
                      w_gate, b_gate, w_br, w_out_b, lg[1], lb[1], l, alpha)
        h = _ffn_ln(h, wg, wu, wd, lg[2], lb[2], l, 1, alpha)
    return h.reshape(B, S, D)
```

```python
import functools
import math

import numpy as np
import jax
import jax.numpy as jnp
from jax import lax
from jax.experimental import pallas as pl
from jax.experimental.pallas import tpu as pltpu

F32 = jnp.float32
BF16 = jnp.bfloat16

HEAD_DIM = 64
N_HEADS = 6
DIFF_QK_DIM = HEAD_DIM // 2
MOBA_BLOCK = 256
MOBA_TOPK = 3
N_BRANCH = 3
LN_EPS = 1e-5
SUBLN_EPS = 1e-5

LANES = 128
SUBLANES = 8
DENOM_ROWS = 16
V_ROWS = HEAD_DIM + DENOM_ROWS
HEADS_PER_BLOCK = LANES // HEAD_DIM
N_HEAD_BLOCKS = N_HEADS // HEADS_PER_BLOCK
MIX_W = N_HEADS * HEAD_DIM
QKV_W = 9 * MIX_W
ATT_T = 256
NEG = -1e30
SB_EXIT = -110.0
TILES_PER_PASS = 2
SB_TILES_PER_PASS = 4
LOG2E = math.log2(math.e)
N_BIAS_COLS = 3
BOUND_SLACK = 1.01
MAX_EXPONENT = 100.0
SKIP_BITS = 150.0
ITEMS_PER_TRIP = 16
LONE_HEAD_BLOCKS_PER_TRIP = 4
BUILD_UNROLL = 4
STAGE_DEPTH = 2
VMEM_LIMIT = 56 * 1024 * 1024


def _dot(a, b, precision=None):
    return jnp.dot(a, b, precision=precision, preferred_element_type=F32)


def _layer_norm(y, g, b):
    mu = jnp.mean(y, axis=-1, keepdims=True)
    yc = y - mu
    var = jnp.mean(yc * yc, axis=-1, keepdims=True)
    return yc * lax.rsqrt(var + LN_EPS) * g + b


def _ffn_ln_kernel(x_ref, wg_ref, wu_ref, wd_ref, lg_ref, lb_ref, o_ref, hid_ref, *, alpha, tf):
    x = x_ref[...]
    xb = x.astype(BF16)
    for c in range(hid_ref.shape[1] // tf):
        cols = slice(c * tf, (c + 1) * tf)
        gate = _dot(xb, wg_ref[:, cols])
        up = _dot(xb, wu_ref[:, cols])
        hid_ref[:, cols] = (gate * jax.nn.sigmoid(gate) * up).astype(BF16)
    y = alpha * x + 0.5 * _dot(hid_ref[...], wd_ref[...])
    o_ref[...] = _layer_norm(y, lg_ref[...], lb_ref[...])


def _ffn_ln(x, wg, wu, wd, lg, lb, l, j, alpha, tm=512, tf=256):
    T, D = x.shape
    FF = wg.shape[-1]
    resident = pl.Buffered(1)
    return pl.pallas_call(
        functools.partial(_ffn_ln_kernel, alpha=alpha, tf=tf),
        grid=(T // tm,),
        in_specs=[
            pl.BlockSpec((tm, D), lambda i: (i, 0)),
            pl.BlockSpec((None, None, D, FF), lambda i: (l, j, 0, 0), pipeline_mode=resident),
            pl.BlockSpec((None, None, D, FF), lambda i: (l, j, 0, 0), pipeline_mode=resident),
            pl.BlockSpec((None, None, FF, D), lambda i: (l, j, 0, 0), pipeline_mode=resident),
            pl.BlockSpec((1, D), lambda i: (0, 0)),
            pl.BlockSpec((1, D), lambda i: (0, 0)),
        ],
        out_specs=pl.BlockSpec((tm, D), lambda i: (i, 0)),
        out_shape=jax.ShapeDtypeStruct((T, D), F32),
        scratch_shapes=[pltpu.VMEM((tm, FF), BF16)],
        compiler_params=pltpu.CompilerParams(
            dimension_semantics=("parallel",), vmem_limit_bytes=VMEM_LIMIT),
        name="ffn_ln",
    )(x, wg, wu, wd, lg, lb)


def _proj_kernel(x_ref, w_ref, o_ref, *, tn):
    xb = x_ref[...].astype(BF16)
    for c in range(o_ref.shape[1] // tn):
        cols = slice(c * tn, (c + 1) * tn)
        o_ref[:, cols] = _dot(xb, w_ref[:, cols]).astype(o_ref.dtype)


def _qkv_proj(x, w_in, l, tm=512, tn=1152):
    T, D = x.shape
    return pl.pallas_call(
        functools.partial(_proj_kernel, tn=tn),
        grid=(T // tm,),
        in_specs=[
            pl.BlockSpec((tm, D), lambda i: (i, 0)),
            pl.BlockSpec((None, D, QKV_W), lambda i: (l, 0, 0), pipeline_mode=pl.Buffered(1)),
        ],
        out_specs=pl.BlockSpec((tm, QKV_W), lambda i: (i, 0)),
        out_shape=jax.ShapeDtypeStruct((T, QKV_W), BF16),
        compiler_params=pltpu.CompilerParams(
            dimension_semantics=("parallel",), vmem_limit_bytes=VMEM_LIMIT),
        name="qkv_proj",
    )(x, w_in)


def _lane_iota():
    return lax.broadcasted_iota(jnp.int32, (1, LANES), 1)


def _head_lanes(hh):
    lane = _lane_iota()
    return (lane >= hh * HEAD_DIM) & (lane < (hh + 1) * HEAD_DIM)


def _blk(j):
    return pl.ds(pl.multiple_of(j * ATT_T, ATT_T), ATT_T)


def _key_query_iotas():
    krow = lax.broadcasted_iota(jnp.int32, (ATT_T, ATT_T), 0)
    qcol = lax.broadcasted_iota(jnp.int32, (ATT_T, ATT_T), 1)
    return krow, qcol


def _build_softmax_operands(tab_ref, hb, k_ref, v_ref, kp_ref, vt_ref, kmax_ref, *, n_blocks, scale):
    lane = _lane_iota()
    pos = lax.broadcasted_iota(jnp.int32, (ATT_T, LANES), 0).astype(F32)
    bias_cols = []
    for hh in range(HEADS_PER_BLOCK):
        b = pos * tab_ref[hb * HEADS_PER_BLOCK + hh]
        lo = (1 - hh) * HEAD_DIM
        cols = jnp.zeros_like(b)
        for i in range(N_BIAS_COLS):
            piece = b.astype(BF16).astype(F32)
            cols = jnp.where(lane == lo + i, piece, cols)
            b = b - piece
        bias_cols.append(cols)
    kmax_ref[...] = jnp.zeros_like(kmax_ref)
    assert HEADS_PER_BLOCK == 2
    head_r = jnp.where(lax.broadcasted_iota(jnp.int32, (LANES, LANES), 0) < HEAD_DIM, 0, 1)
    head_c = jnp.where(lax.broadcasted_iota(jnp.int32, (LANES, LANES), 1) < HEAD_DIM, 0, 1)
    same_head = jnp.where(head_r == head_c, 1.0, 0.0).astype(BF16)

    def body(n, c):
        kf = k_ref[_blk(n), :].astype(F32) * (scale * LOG2E)
        v_t = v_ref[_blk(n), :].astype(F32).T.astype(BF16)
        for hh in range(HEADS_PER_BLOCK):
            kp_ref[hh, _blk(n), :] = jnp.where(_head_lanes(hh), kf, bias_cols[hh]).astype(BF16)
            vt_ref[hh, 0:HEAD_DIM, _blk(n)] = v_t[hh * HEAD_DIM:(hh + 1) * HEAD_DIM, :]
            vt_ref[hh, HEAD_DIM:V_ROWS, _blk(n)] = jnp.ones((DENOM_ROWS, ATT_T), BF16)
        norm2 = _dot((kf * kf).astype(BF16), same_head)
        kmax_ref[...] = jnp.maximum(kmax_ref[...], jnp.max(norm2, axis=0, keepdims=True))
        return c

    lax.fori_loop(0, n_blocks, body, 0, unroll=BUILD_UNROLL)


def _chain_qt(c, hh, seg_lo, seg_width, q_t, qt_ref):
    row = lax.broadcasted_iota(jnp.int32, (LANES, ATT_T), 0)
    lo = (1 - hh) * HEAD_DIM
    qsel = jnp.where((row >= seg_lo) & (row < seg_lo + seg_width), q_t, 0.0)
    qt_ref[c] = jnp.where((row >= lo) & (row < lo + N_BIAS_COLS), 1.0, qsel).astype(BF16)
    return qsel, jnp.sum(qsel * qsel, axis=0, keepdims=True)


def _stage_scores(c, hh, n, slot, kp_ref, qt_ref, sbuf_ref):
    sbuf_ref[c * STAGE_DEPTH + slot] = _dot(kp_ref[hh, _blk(jnp.maximum(n, 0)), :], qt_ref[c])


def _diag_blocks(chains, qi_of, qn2, slope2, kp_ref, vt_ref, kmax_ref, qt_ref, ref_ref, acc_ref, sbuf_ref):
    krow, qcol = _key_query_iotas()
    causal = krow <= qcol
    r = lax.broadcasted_iota(jnp.int32, (1, ATT_T), 1).astype(F32)
    scores = [_dot(kp_ref[hh, _blk(qi_of[c]), :], qt_ref[c]) for c, hh in enumerate(chains)]
    for slot in range(STAGE_DEPTH):
        for c, hh in enumerate(chains):
            _stage_scores(c, hh, qi_of[c] - 1 - slot, slot, kp_ref, qt_ref, sbuf_ref)
    gaps = []
    for c, hh in enumerate(chains):
        s = jnp.where(causal, scores[c], NEG)
        m = jnp.max(s, axis=0, keepdims=True)
        kmax2 = kmax_ref[:, hh * HEAD_DIM:hh * HEAD_DIM + 1]
        bound = jnp.sqrt(qn2[c] * kmax2) * BOUND_SLACK + slope2[hh] * r
        ref_ref[c] = m
        acc_ref[c] = _dot(vt_ref[hh, :, _blk(qi_of[c])], jnp.exp2(s - m).astype(BF16))
        gaps.append(jnp.max(bound - m))
    return gaps


def _first_block(gap, inv_blk, qi):
    reach = jnp.minimum((gap + SKIP_BITS) * inv_blk, 1e6).astype(jnp.int32)
    return jnp.maximum(qi - 1 - reach, 0)


def _fixed_ref_step(c, hh, n, slot, off, kp_ref, vt_ref, qt_ref, ref_ref, acc_ref, sbuf_ref):
    p = jnp.exp2(sbuf_ref[c * STAGE_DEPTH + slot] - (ref_ref[c] - off)).astype(BF16)
    _stage_scores(c, hh, n - STAGE_DEPTH, slot, kp_ref, qt_ref, sbuf_ref)
    acc_ref[c] += _dot(vt_ref[hh, :, _blk(jnp.maximum(n, 0))], p)


def _online_step(c, hh, n, off, kp_ref, vt_ref, qt_ref, m_ref, acc_ref):
    s = _dot(kp_ref[hh, _blk(n), :], qt_ref[c])
    m_old = m_ref[c]
    m_new = jnp.maximum(m_old, jnp.max(s, axis=0, keepdims=True) + off)
    p = jnp.exp2(s - (m_new - off)).astype(BF16)
    acc_ref[c] = acc_ref[c] * jnp.exp2(m_old - m_new) + _dot(vt_ref[hh, :, _blk(n)], p)
    m_ref[c] = m_new


def _softmax_sweep(qi, chains, base, gaps, first, off_fn, kp_ref, vt_ref, qt_ref, ref_ref, acc_ref, sbuf_ref,
                   split_heads):
    refs = (kp_ref, vt_ref, qt_ref, ref_ref, acc_ref)
    assert STAGE_DEPTH == 2
    widest = gaps[0]
    for g in gaps[1:]:
        widest = jnp.maximum(widest, g)
    fixed_ok = widest <= MAX_EXPONENT

    def run_fixed(top, left, heads):
        active = [(base + c, hh) for c, hh in enumerate(chains) if hh in heads]
        per_trip = max(STAGE_DEPTH, ITEMS_PER_TRIP // len(active))
        if len(heads) == 1:
            per_trip = min(per_trip, LONE_HEAD_BLOCKS_PER_TRIP)
        assert per_trip % STAGE_DEPTH == 0

        def sweep(top, trips, per_trip):
            def body(i, carry):
                for j in range(per_trip):
                    n = top - i * per_trip - j
                    off_end = jnp.where(n < 0, NEG, 0.0)
                    for c, hh in active:
                        off = off_fn(hh, jnp.maximum(n, 0)) + off_end
                        _fixed_ref_step(c, hh, n, j % STAGE_DEPTH, off, *refs, sbuf_ref)
                return carry

            lax.fori_loop(0, trips, body, 0)

        whole = left // per_trip
        sweep(top, whole, per_trip)
        if per_trip > STAGE_DEPTH:
            rest = left - whole * per_trip
            sweep(top - whole * per_trip, (rest + STAGE_DEPTH - 1) // STAGE_DEPTH, STAGE_DEPTH)
        else:
            sweep(top - whole * per_trip, left - whole * per_trip, 1)

    @pl.when(fixed_ok)
    def _fixed_reference():
        far = jnp.minimum(first[0], first[1])
        if split_heads:
            near = jnp.maximum(first[0], first[1])
            together = jnp.minimum((qi - near + 1) // 2 * 2, qi - far)
            run_fixed(qi - 1, together, (0, 1))
            for hh in range(HEADS_PER_BLOCK):
                run_fixed(qi - 1 - together, jnp.maximum(qi - together - first[hh], 0), (hh,))
        else:
            run_fixed(qi - 1, qi - far, (0, 1))

    @pl.when(jnp.logical_not(fixed_ok))
    def _running_maximum():
        def body(n, carry):
            for c, hh in enumerate(chains):
                _online_step(base + c, hh, n, off_fn(hh, n), *refs)
            return carry

        lax.fori_loop(0, qi, body, 0)


def _normalized(acc):
    return acc[0:HEAD_DIM, :] * (1.0 / acc[HEAD_DIM:HEAD_DIM + 1, :])


def _row_is_head0():
    return lax.broadcasted_iota(jnp.int32, (LANES, ATT_T), 0) < HEAD_DIM


def _moba_kernel(tab_ref, q_ref, k_ref, v_ref, o_ref, kp_ref, vt_ref, kmax_ref, kmean_ref, selm_ref,
                 qt_ref, ref_ref, acc_ref, sbuf_ref, *, n_blocks, scale):
    hb = pl.program_id(1)
    nbp = kmean_ref.shape[0]
    _build_softmax_operands(tab_ref, hb, k_ref, v_ref, kp_ref, vt_ref, kmax_ref, n_blocks=n_blocks, scale=scale)
    kmean_ref[...] = jnp.zeros_like(kmean_ref)

    def block_mean(n, c):
        kmean_ref[pl.ds(n, 1), :] = jnp.mean(k_ref[_blk(n), :].astype(F32), axis=0, keepdims=True)
        return c

    lax.fori_loop(0, n_blocks, block_mean, 0, unroll=BUILD_UNROLL)

    blk_id = lax.broadcasted_iota(jnp.int32, (nbp, ATT_T), 0)
    blk_f = blk_id.astype(F32)
    chains = list(range(HEADS_PER_BLOCK))
    n_ch = len(chains)
    slope2 = [tab_ref[hb * HEADS_PER_BLOCK + hh] for hh in chains]
    assert n_blocks % TILES_PER_PASS == 0

    def tile_group(g, carry):
        tiles = [g * TILES_PER_PASS + s for s in range(TILES_PER_PASS)]
        qn2 = []
        for s, qi in enumerate(tiles):
            q_t = q_ref[_blk(qi), :].astype(F32).T
            for hh in chains:
                hm = _head_lanes(hh)
                qsel, norm2 = _chain_qt(s * n_ch + hh, hh, hh * HEAD_DIM, HEAD_DIM, q_t, qt_ref)
                qn2.append(norm2)

                gate = _dot(jnp.where(hm, kmean_ref[...], 0.0), qsel, precision=lax.Precision.HIGHEST)
                gt = jnp.where(blk_id < qi, gate, -jnp.inf)
                sel = jnp.zeros(gt.shape, dtype=jnp.bool_)
                for _ in range(MOBA_TOPK):
                    mx = jnp.max(gt, axis=0, keepdims=True)
                    pick = jnp.min(jnp.where(gt == mx, blk_f, float(nbp)), axis=0, keepdims=True)
                    hit = blk_f == pick
                    sel = sel | (hit & (mx > -jnp.inf))
                    gt = jnp.where(hit, -jnp.inf, gt)
                selm_ref[s * n_ch + hh] = jnp.where(sel, 0.0, NEG)

        qi_of = [tiles[c // n_ch] for c in range(TILES_PER_PASS * n_ch)]
        gaps = _diag_blocks(chains * TILES_PER_PASS, qi_of, qn2, slope2, kp_ref, vt_ref, kmax_ref, qt_ref,
                            ref_ref, acc_ref, sbuf_ref)
        for s, qi in enumerate(tiles):
            base = s * n_ch
            first = [_first_block(gaps[base + hh], tab_ref[N_HEADS + hb * HEADS_PER_BLOCK + hh], qi)
                     for hh in chains]

            def off_fn(hh, n, qi=qi, base=base):
                return slope2[hh] * jnp.asarray((n - qi) * ATT_T, F32) + selm_ref[base + hh, pl.ds(n, 1), :]

            _softmax_sweep(qi, chains, base, gaps[base:base + n_ch], first, off_fn, kp_ref, vt_ref, qt_ref,
                           ref_ref, acc_ref, sbuf_ref, split_heads=False)
            o = jnp.concatenate([_normalized(acc_ref[base + hh]) for hh in chains], axis=0)
            o_ref[_blk(qi), :] = o.T.astype(o_ref.dtype)
        return carry

    lax.fori_loop(0, n_blocks // TILES_PER_PASS, tile_group, 0)


def _diff_kernel(tab_ref, lam_ref, g_ref, q_ref, k_ref, v_ref, o_ref, kp_ref, vt_ref, kmax_ref, qt_ref,
                 ref_ref, acc_ref, sbuf_ref, *, n_blocks, scale, lambda_init):
    hb = pl.program_id(1)
    _build_softmax_operands(tab_ref, hb, k_ref, v_ref, kp_ref, vt_ref, kmax_ref, n_blocks=n_blocks, scale=scale)

    lf = lam_ref[...].astype(F32)
    lam = (jnp.exp(jnp.sum(lf[0:1] * lf[1:2], axis=-1, keepdims=True))
           - jnp.exp(jnp.sum(lf[2:3] * lf[3:4], axis=-1, keepdims=True)) + lambda_init)

    chains = [hh for hh in range(HEADS_PER_BLOCK) for _ in range(2)]
    n_ch = len(chains)
    slope2 = [tab_ref[hb * HEADS_PER_BLOCK + hh] for hh in range(HEADS_PER_BLOCK)]
    assert n_blocks % TILES_PER_PASS == 0

    def tile_group(g, carry):
        tiles = [g * TILES_PER_PASS + s for s in range(TILES_PER_PASS)]
        qn2 = []
        for s, qi in enumerate(tiles):
            q_t = q_ref[_blk(qi), :].astype(F32).T
            for c, hh in enumerate(chains):
                seg_lo = hh * HEAD_DIM + (c % 2) * DIFF_QK_DIM
                qn2.append(_chain_qt(s * n_ch + c, hh, seg_lo, DIFF_QK_DIM, q_t, qt_ref)[1])
        qi_of = [tiles[c // n_ch] for c in range(TILES_PER_PASS * n_ch)]
        gaps = _diag_blocks(chains * TILES_PER_PASS, qi_of, qn2, slope2, kp_ref, vt_ref, kmax_ref, qt_ref,
                            ref_ref, acc_ref, sbuf_ref)
        for s, qi in enumerate(tiles):
            base = s * n_ch
            first = [None] * HEADS_PER_BLOCK
            for c, hh in enumerate(chains):
                fb = _first_block(gaps[base + c], tab_ref[N_HEADS + hb * HEADS_PER_BLOCK + hh], qi)
                first[hh] = fb if first[hh] is None else jnp.minimum(first[hh], fb)

            def off_fn(hh, n, qi=qi):
                return slope2[hh] * jnp.asarray((n - qi) * ATT_T, F32)

            _softmax_sweep(qi, chains, base, gaps[base:base + n_ch], first, off_fn, kp_ref, vt_ref, qt_ref,
                           ref_ref, acc_ref, sbuf_ref, split_heads=True)

            outs = []
            for hh in range(HEADS_PER_BLOCK):
                o = (_normalized(acc_ref[base + 2 * hh])
                     - lam * _normalized(acc_ref[base + 2 * hh + 1]))
                ms = jnp.mean(o * o, axis=0, keepdims=True)
                outs.append(o * lax.rsqrt(ms + SUBLN_EPS))
            o = jnp.concatenate(outs, axis=0).T * (g_ref[...] * (1.0 - lambda_init))
            o_ref[_blk(qi), :] = o.astype(o_ref.dtype)
        return carry

    lax.fori_loop(0, n_blocks // TILES_PER_PASS, tile_group, 0)


def _log_sigmoids(z):
    lneg = jnp.minimum(-z, 0.0) - jnp.log(1.0 + jnp.exp(-jnp.abs(z)))
    return lneg, lneg + z


def _suffix_sums(upper, lg):
    hi = lg.astype(BF16)
    lo = (lg - hi.astype(F32)).astype(BF16)
    return _dot(upper, hi) + _dot(upper, lo)


def _sb_pass(tiles, k_ref, vt_ref, qt_ref, carry_ref, acc_ref):
    krow, qcol = _key_query_iotas()
    before = krow < qcol
    upper = (qcol > krow).astype(BF16)
    items = [(slot * HEADS_PER_BLOCK + hh, hh, n, diag)
             for slot, blocks in tiles for n, diag in blocks for hh in range(HEADS_PER_BLOCK)]
    z = [_dot(k_ref[_blk(n), :], qt_ref[st]) for st, _, n, _ in items]
    log_sig, log_surv, tails = [], [], []
    for i, (st, hh, n, diag) in enumerate(items):
        lneg, lpos = _log_sigmoids(jnp.where(before, z[i], NEG) if diag else z[i])
        log_surv.append(lneg)
        log_sig.append(lpos)
        tails.append(_suffix_sums(upper, lneg))
    carry, acc = {}, {}
    for slot, blocks in tiles:
        for hh in range(HEADS_PER_BLOCK):
            st = slot * HEADS_PER_BLOCK + hh
            starts_fresh = blocks[0][1]
            carry[st] = None if starts_fresh else carry_ref[st]
            acc[st] = None if starts_fresh else acc_ref[st]
    for i, (st, hh, n, diag) in enumerate(items):
        w = jnp.exp(log_sig[i] + (tails[i] if diag else tails[i] + carry[st]))
        part = _dot(vt_ref[hh, :, _blk(n)], w.astype(BF16))
        acc[st] = part if acc[st] is None else acc[st] + part
        total = jnp.sum(log_surv[i], axis=0, keepdims=True)
        carry[st] = total if carry[st] is None else carry[st] + total
    worst = []
    for slot, _ in tiles:
        tops = []
        for hh in range(HEADS_PER_BLOCK):
            st = slot * HEADS_PER_BLOCK + hh
            acc_ref[st] = acc[st]
            carry_ref[st] = carry[st]
            tops.append(jnp.max(carry[st]))
        worst.append(functools.reduce(jnp.maximum, tops))
    return tuple(worst)


def _sb_kernel(q_ref, k_ref, v_ref, o_ref, vt_ref, qt_ref, carry_ref, acc_ref, *, n_blocks, scale):
    def transpose_values(n, c):
        v_t = v_ref[_blk(n), :].astype(F32).T.astype(BF16)
        for hh in range(HEADS_PER_BLOCK):
            vt_ref[hh, :, _blk(n)] = v_t[hh * HEAD_DIM:(hh + 1) * HEAD_DIM, :]
        return c

    lax.fori_loop(0, n_blocks, transpose_values, 0, unroll=BUILD_UNROLL)

    assert n_blocks % SB_TILES_PER_PASS == 0
    refs = (k_ref, vt_ref, qt_ref, carry_ref, acc_ref)
    row_h0 = _row_is_head0()

    def tile_group(g, carry):
        tiles = [g * SB_TILES_PER_PASS + slot for slot in range(SB_TILES_PER_PASS)]
        for slot, qi in enumerate(tiles):
            q_t = (q_ref[_blk(qi), :].astype(F32) * scale).T
            for hh in range(HEADS_PER_BLOCK):
                in_head = row_h0 if hh == 0 else jnp.logical_not(row_h0)
                qt_ref[slot * HEADS_PER_BLOCK + hh] = jnp.where(in_head, q_t, 0.0).astype(BF16)

        def blocks_of(qi, with_previous):
            return [(qi, True), (qi - 1, False)] if with_previous else [(qi, True)]

        worst = lax.cond(
            g > 0,
            lambda: _sb_pass([(slot, blocks_of(qi, True)) for slot, qi in enumerate(tiles)], *refs),
            lambda: _sb_pass([(slot, blocks_of(qi, slot > 0)) for slot, qi in enumerate(tiles)], *refs))

        for slot, qi in enumerate(tiles):
            def cond(c):
                j, live = c
                return (j >= 0) & live

            def body(c, slot=slot):
                j, _ = c
                return j - 1, _sb_pass([(slot, [(j, False)])], *refs)[0] > SB_EXIT

            lax.while_loop(cond, body, (qi - 2, worst[slot] > SB_EXIT))
            st = slot * HEADS_PER_BLOCK
            o = jnp.concatenate([acc_ref[st + hh] for hh in range(HEADS_PER_BLOCK)], axis=0)
            o_ref[_blk(qi), :] = o.T.astype(o_ref.dtype)
        return carry

    lax.fori_loop(0, n_blocks // SB_TILES_PER_PASS, tile_group, 0)


def _attn_specs(B, S, q_base, k_base, v_base):
    qspec = pl.BlockSpec((None, S, LANES), lambda b, h: (b, 0, q_base + h))
    kspec = pl.BlockSpec((None, S, LANES), lambda b, h: (b, 0, k_base + h))
    vspec = pl.BlockSpec((None, S, LANES), lambda b, h: (b, 0, v_base + h))
    ospec = pl.BlockSpec((None, S, LANES), lambda b, h: (b, 0, h))
    grid = (B, N_HEAD_BLOCKS)
    params = pltpu.CompilerParams(
        dimension_semantics=("parallel", "parallel"), vmem_limit_bytes=VMEM_LIMIT)
    return grid, qspec, kspec, vspec, ospec, params


_SMEM_SPEC = pl.BlockSpec(memory_space=pltpu.SMEM)


def _moba(qkv, tab):
    B, S, _ = qkv.shape
    grid, qspec, kspec, vspec, ospec, params = _attn_specs(B, S, 0, 3, 6)
    n_blocks = S // MOBA_BLOCK
    nbp = -(-n_blocks // SUBLANES) * SUBLANES
    n_slots = TILES_PER_PASS * HEADS_PER_BLOCK
    return pl.pallas_call(
        functools.partial(_moba_kernel, n_blocks=n_blocks, scale=HEAD_DIM ** -0.5),
        grid=grid,
        in_specs=[_SMEM_SPEC, qspec, kspec, vspec],
        out_specs=ospec,
        out_shape=jax.ShapeDtypeStruct((B, S, MIX_W), BF16),
        scratch_shapes=[
            pltpu.VMEM((HEADS_PER_BLOCK, S, LANES), BF16),
            pltpu.VMEM((HEADS_PER_BLOCK, V_ROWS, S), BF16),
            pltpu.VMEM((1, LANES), F32),
            pltpu.VMEM((nbp, LANES), F32),
            pltpu.VMEM((n_slots, nbp, ATT_T), F32),
            pltpu.VMEM((n_slots, LANES, ATT_T), BF16),
            pltpu.VMEM((n_slots, 1, ATT_T), F32),
            pltpu.VMEM((n_slots, V_ROWS, ATT_T), F32),
            pltpu.VMEM((n_slots * STAGE_DEPTH, ATT_T, ATT_T), F32),
        ],
        compiler_params=params,
        name="moba_attn",
    )(tab, qkv, qkv, qkv)


def _diff(qkv, tab, diff_lambda, subln_g, lambda_init):
    B, S, _ = qkv.shape
    grid, qspec, kspec, vspec, ospec, params = _attn_specs(B, S, 9, 12, 15)
    n_chains = TILES_PER_PASS * 2 * HEADS_PER_BLOCK
    return pl.pallas_call(
        functools.partial(_diff_kernel, n_blocks=S // ATT_T, scale=DIFF_QK_DIM ** -0.5,
                          lambda_init=lambda_init),
        grid=grid,
        in_specs=[
            _SMEM_SPEC,
            pl.BlockSpec(diff_lambda.shape, lambda b, h: (0, 0)),
            pl.BlockSpec((1, LANES), lambda b, h: (0, 0)),
            qspec, kspec, vspec,
        ],
        out_specs=ospec,
        out_shape=jax.ShapeDtypeStruct((B, S, MIX_W), BF16),
        scratch_shapes=[
            pltpu.VMEM((HEADS_PER_BLOCK, S, LANES), BF16),
            pltpu.VMEM((HEADS_PER_BLOCK, V_ROWS, S), BF16),
            pltpu.VMEM((1, LANES), F32),
            pltpu.VMEM((n_chains, LANES, ATT_T), BF16),
            pltpu.VMEM((n_chains, 1, ATT_T), F32),
            pltpu.VMEM((n_chains, V_ROWS, ATT_T), F32),
            pltpu.VMEM((n_chains * STAGE_DEPTH, ATT_T, ATT_T), F32),
        ],
        compiler_params=params,
        name="diff_attn",
    )(tab, diff_lambda, subln_g, qkv, qkv, qkv)


def _sb(qkv):
    B, S, _ = qkv.shape
    grid, qspec, kspec, vspec, ospec, params = _attn_specs(B, S, 18, 21, 24)
    return pl.pallas_call(
        functools.partial(_sb_kernel, n_blocks=S // ATT_T, scale=HEAD_DIM ** -0.5),
        grid=grid,
        in_specs=[qspec, kspec, vspec],
        out_specs=ospec,
        out_shape=jax.ShapeDtypeStruct((B, S, MIX_W), BF16),
        scratch_shapes=[
            pltpu.VMEM((HEADS_PER_BLOCK, HEAD_DIM, S), BF16),
            pltpu.VMEM((SB_TILES_PER_PASS * HEADS_PER_BLOCK, LANES, ATT_T), BF16),
            pltpu.VMEM((SB_TILES_PER_PASS * HEADS_PER_BLOCK, 1, ATT_T), F32),
            pltpu.VMEM((SB_TILES_PER_PASS * HEADS_PER_BLOCK, HEAD_DIM, ATT_T), F32),
        ],
        compiler_params=params,
        name="sb_attn",
    )(qkv, qkv, qkv)


def _merge_ln_kernel(x_ref, om_ref, od_ref, os_ref, wg_ref, bg_ref, wbr_ref, wo_ref, lg_ref, lb_ref,
                     o_ref, *, alpha, d_model):
    x = x_ref[...]
    xb = x.astype(BF16)
    merged = None
    for b, ob_ref in enumerate((om_ref, od_ref, os_ref)):
        glogit = _dot(xb, wg_ref[:, b * d_model:(b + 1) * d_model]) + bg_ref[b:b + 1, :]
        term = jax.nn.sigmoid(glogit) * _dot(ob_ref[...], wbr_ref[b])
        merged = term if merged is None else merged + term
    y = alpha * x + _dot(merged.astype(BF16), wo_ref[...])
    o_ref[...] = _layer_norm(y, lg_ref[...], lb_ref[...])


def _merge_ln(x, o_m, o_d, o_s, w_gate, b_gate, w_br, w_out, lg, lb, l, alpha, tm=512):
    T, D = x.shape
    return pl.pallas_call(
        functools.partial(_merge_ln_kernel, alpha=alpha, d_model=D),
        grid=(T // tm,),
        in_specs=[
            pl.BlockSpec((tm, D), lambda i: (i, 0)),
            pl.BlockSpec((tm, MIX_W), lambda i: (i, 0)),
            pl.BlockSpec((tm, MIX_W), lambda i: (i, 0)),
            pl.BlockSpec((tm, MIX_W), lambda i: (i, 0)),
            pl.BlockSpec((None, D, N_BRANCH * D), lambda i: (l, 0, 0), pipeline_mode=pl.Buffered(1)),
            pl.BlockSpec((None, N_BRANCH, D), lambda i: (l, 0, 0)),
            pl.BlockSpec((None, N_BRANCH, MIX_W, D), lambda i: (l, 0, 0, 0), pipeline_mode=pl.Buffered(1)),
            pl.BlockSpec((None, D, D), lambda i: (l, 0, 0), pipeline_mode=pl.Buffered(1)),
            pl.BlockSpec((1, D), lambda i: (0, 0)),
            pl.BlockSpec((1, D), lambda i: (0, 0)),
        ],
        out_specs=pl.BlockSpec((tm, D), lambda i: (i, 0)),
        out_shape=jax.ShapeDtypeStruct((T, D), F32),
        compiler_params=pltpu.CompilerParams(
            dimension_semantics=("parallel",), vmem_limit_bytes=VMEM_LIMIT),
        name="merge_ln",
    )(x, o_m, o_d, o_s, w_gate, b_gate, w_br, w_out, lg, lb)


def _alibi_slopes(n):
    return (2.0 ** (-8.0 * np.arange(1, n + 1, dtype=np.float32) / n)).astype(np.float32)


def _slope_table(slopes):
    s2 = slopes.astype(np.float64) * LOG2E
    return np.concatenate([s2, 1.0 / (s2 * ATT_T)]).astype(np.float32)


def kernel(x, ln_g, ln_b, ffn_w_gate, ffn_w_up, ffn_w_down, w_in, b_gate, diff_lambda, diff_subln_g,
           w_br_moba, w_br_diff, w_br_sb, w_out):
    B, S, D = x.shape
    depth = ln_g.shape[0]
    assert S % ATT_T == 0 and ATT_T == MOBA_BLOCK
    assert w_in.shape[-1] == QKV_W + N_BRANCH * D
    alpha = (2.0 * depth) ** 0.25

    wg = ffn_w_gate.astype(BF16)
    wu = ffn_w_up.astype(BF16)
    wd = ffn_w_down.astype(BF16)
    w_qkv = w_in[:, :, :QKV_W].astype(BF16)
    w_gate = w_in[:, :, QKV_W:].astype(BF16)
    w_br = jnp.stack([w_br_moba, w_br_diff, w_br_sb], axis=1).astype(BF16)
    w_out_b = w_out.astype(BF16)
    slopes = _alibi_slopes(2 * N_HEADS)
    tab_moba = jnp.asarray(_slope_table(slopes[0::2]))
    tab_diff = jnp.asarray(_slope_table(slopes[1::2]))
    subln_g = jnp.tile(diff_subln_g.astype(F32), (1, HEADS_PER_BLOCK))

    h = x.reshape(B * S, D)
    for l in range(depth):
        lg = ln_g[l][:, None, :]
        lb = ln_b[l][:, None, :]
        h = _ffn_ln(h, wg, wu, wd, lg[0], lb[0], l, 0, alpha)
        qkv = _qkv_proj(h, w_qkv, l).reshape(B, S, QKV_W)
        lambda_init = 0.8 - 0.6 * math.exp(-0.3 * l)
        o_m = _moba(qkv, tab_moba)
        o_d = _diff(qkv, tab_diff, diff_lambda[l], subln_g[l][None, :], lambda_init)
        o_s = _sb(qkv)
        T = B * S
        h = _merge_ln(h, o_m.reshape(T, MIX_W), o_d.reshape(T, MIX_W), o_s.reshape(T, MIX_W),
                      w_gate, b_gate, w_br, w_out_b, lg[1], lb[1], l, alpha)
        h = _ffn_ln(h, wg, wu, wd, lg[2], lb[2], l, 1, alpha)
    return h.reshape(B, S, D)
```

```python
import functools
import math

import numpy as np
import jax
import jax.numpy as jnp
from jax import lax
from jax.experimental import pallas as pl
from jax.experimental.pallas import tpu as pltpu

F32 = jnp.float32
BF16 = jnp.bfloat16

HEAD_DIM = 64
N_HEADS = 6
DIFF_QK_DIM = HEAD_DIM // 2
MOBA_BLOCK = 256
MOBA_TOPK = 3
N_BRANCH = 3
LN_EPS = 1e-5
SUBLN_EPS = 1e-5

LANES = 128
SUBLANES = 8
DENOM_ROWS = 16
V_ROWS = HEAD_DIM + DENOM_ROWS
HEADS_PER_BLOCK = LANES // HEAD_DIM
N_HEAD_BLOCKS = N_HEADS // HEADS_PER_BLOCK
MIX_W = N_HEADS * HEAD_DIM
QKV_W = 9 * MIX_W
ATT_T = 256
NEG = -1e30
SB_EXIT = -110.0
TILES_PER_PASS = 2
SB_TILES_PER_PASS = 4
LOG2E = math.log2(math.e)
N_BIAS_COLS = 3
BOUND_SLACK = 1.01
MAX_EXPONENT = 100.0
SKIP_BITS = 150.0
ITEMS_PER_TRIP = 16
LONE_HEAD_BLOCKS_PER_TRIP = 4
BUILD_UNROLL = 4
STAGE_DEPTH = 2
VMEM_LIMIT = 56 * 1024 * 1024


def _dot(a, b, precision=None):
    return jnp.dot(a, b, precision=precision, preferred_element_type=F32)


def _layer_norm(y, g, b):
    mu = jnp.mean(y, axis=-1, keepdims=True)
    yc = y - mu
    var = jnp.mean(yc * yc, axis=-1, keepdims=True)
    return yc * lax.rsqrt(var + LN_EPS) * g + b


def _ffn_ln_kernel(x_ref, wg_ref, wu_ref, wd_ref, lg_ref, lb_ref, o_ref, hid_ref, *, alpha, tf):
    x = x_ref[...]
    xb = x.astype(BF16)
    for c in range(hid_ref.shape[1] // tf):
        cols = slice(c * tf, (c + 1) * tf)
        gate = _dot(xb, wg_ref[:, cols])
        up = _dot(xb, wu_ref[:, cols])
        hid_ref[:, cols] = (gate * jax.nn.sigmoid(gate) * up).astype(BF16)
    y = alpha * x + 0.5 * _dot(hid_ref[...], wd_ref[...])
    o_ref[...] = _layer_norm(y, lg_ref[...], lb_ref[...])


def _ffn_ln(x, wg, wu, wd, lg, lb, l, j, alpha, tm=512, tf=256):
    T, D = x.shape
    FF = wg.shape[-1]
    resident = pl.Buffered(1)
    return pl.pallas_call(
        functools.partial(_ffn_ln_kernel, alpha=alpha, tf=tf),
        grid=(T // tm,),
        in_specs=[
            pl.BlockSpec((tm, D), lambda i: (i, 0)),
            pl.BlockSpec((None, None, D, FF), lambda i: (l, j, 0, 0), pipeline_mode=resident),
            pl.BlockSpec((None, None, D, FF), lambda i: (l, j, 0, 0), pipeline_mode=resident),
            pl.BlockSpec((None, None, FF, D), lambda i: (l, j, 0, 0), pipeline_mode=resident),
            pl.BlockSpec((1, D), lambda i: (0, 0)),
            pl.BlockSpec((1, D), lambda i: (0, 0)),
        ],
        out_specs=pl.BlockSpec((tm, D), lambda i: (i, 0)),
        out_shape=jax.ShapeDtypeStruct((T, D), F32),
        scratch_shapes=[pltpu.VMEM((tm, FF), BF16)],
        compiler_params=pltpu.CompilerParams(
            dimension_semantics=("parallel",), vmem_limit_bytes=VMEM_LIMIT),
        name="ffn_ln",
    )(x, wg, wu, wd, lg, lb)


def _proj_kernel(x_ref, w_ref, o_ref, *, tn):
    xb = x_ref[...].astype(BF16)
    for c in range(o_ref.shape[1] // tn):
        cols = slice(c * tn, (c + 1) * tn)
        o_ref[:, cols] = _dot(xb, w_ref[:, cols]).astype(o_ref.dtype)


def _qkv_proj(x, w_in, l, tm=512, tn=1152):
    T, D = x.shape
    return pl.pallas_call(
        functools.partial(_proj_kernel, tn=tn),
        grid=(T // tm,),
        in_specs=[
            pl.BlockSpec((tm, D), lambda i: (i, 0)),
            pl.BlockSpec((None, D, QKV_W), lambda i: (l, 0, 0), pipeline_mode=pl.Buffered(1)),
        ],
        out_specs=pl.BlockSpec((tm, QKV_W), lambda i: (i, 0)),
        out_shape=jax.ShapeDtypeStruct((T, QKV_W), BF16),
        compiler_params=pltpu.CompilerParams(
            dimension_semantics=("parallel",), vmem_limit_bytes=VMEM_LIMIT),
        name="qkv_proj",
    )(x, w_in)


def _lane_iota():
    return lax.broadcasted_iota(jnp.int32, (1, LANES), 1)


def _head_lanes(hh):
    lane = _lane_iota()
    return (lane >= hh * HEAD_DIM) & (lane < (hh + 1) * HEAD_DIM)


def _blk(j):
    return pl.ds(pl.multiple_of(j * ATT_T, ATT_T), ATT_T)


def _key_query_iotas():
    krow = lax.broadcasted_iota(jnp.int32, (ATT_T, ATT_T), 0)
    qcol = lax.broadcasted_iota(jnp.int32, (ATT_T, ATT_T), 1)
    return krow, qcol


def _build_softmax_operands(tab_ref, hb, k_ref, v_ref, kp_ref, vt_ref, kmax_ref, *, n_blocks, scale):
    lane = _lane_iota()
    pos = lax.broadcasted_iota(jnp.int32, (ATT_T, LANES), 0).astype(F32)
    bias_cols = []
    for hh in range(HEADS_PER_BLOCK):
        b = pos * tab_ref[hb * HEADS_PER_BLOCK + hh]
        lo = (1 - hh) * HEAD_DIM
        cols = jnp.zeros_like(b)
        for i in range(N_BIAS_COLS):
            piece = b.astype(BF16).astype(F32)
            cols = jnp.where(lane == lo + i, piece, cols)
            b = b - piece
        bias_cols.append(cols)
    kmax_ref[...] = jnp.zeros_like(kmax_ref)
    assert HEADS_PER_BLOCK == 2
    head_r = jnp.where(lax.broadcasted_iota(jnp.int32, (LANES, LANES), 0) < HEAD_DIM, 0, 1)
    head_c = jnp.where(lax.broadcasted_iota(jnp.int32, (LANES, LANES), 1) < HEAD_DIM, 0, 1)
    same_head = jnp.where(head_r == head_c, 1.0, 0.0).astype(BF16)

    def body(n, c):
        kf = k_ref[_blk(n), :].astype(F32) * (scale * LOG2E)
        v_t = v_ref[_blk(n), :].astype(F32).T.astype(BF16)
        for hh in range(HEADS_PER_BLOCK):
            kp_ref[hh, _blk(n), :] = jnp.where(_head_lanes(hh), kf, bias_cols[hh]).astype(BF16)
            vt_ref[hh, 0:HEAD_DIM, _blk(n)] = v_t[hh * HEAD_DIM:(hh + 1) * HEAD_DIM, :]
            vt_ref[hh, HEAD_DIM:V_ROWS, _blk(n)] = jnp.ones((DENOM_ROWS, ATT_T), BF16)
        norm2 = _dot((kf * kf).astype(BF16), same_head)
        kmax_ref[...] = jnp.maximum(kmax_ref[...], jnp.max(norm2, axis=0, keepdims=True))
        return c

    lax.fori_loop(0, n_blocks, body, 0, unroll=BUILD_UNROLL)


def _chain_qt(c, hh, seg_lo, seg_width, q_t, qt_ref):
    row = lax.broadcasted_iota(jnp.int32, (LANES, ATT_T), 0)
    lo = (1 - hh) * HEAD_DIM
    qsel = jnp.where((row >= seg_lo) & (row < seg_lo + seg_width), q_t, 0.0)
    qt_ref[c] = jnp.where((row >= lo) & (row < lo + N_BIAS_COLS), 1.0, qsel).astype(BF16)
    return qsel, jnp.sum(qsel * qsel, axis=0, keepdims=True)


def _stage_scores(c, hh, n, slot, kp_ref, qt_ref, sbuf_ref):
    sbuf_ref[c * STAGE_DEPTH + slot] = _dot(kp_ref[hh, _blk(jnp.maximum(n, 0)), :], qt_ref[c])


def _diag_blocks(chains, qi_of, qn2, slope2, kp_ref, vt_ref, kmax_ref, qt_ref, ref_ref, acc_ref, sbuf_ref):
    krow, qcol = _key_query_iotas()
    causal = krow <= qcol
    r = lax.broadcasted_iota(jnp.int32, (1, ATT_T), 1).astype(F32)
    scores = [_dot(kp_ref[hh, _blk(qi_of[c]), :], qt_ref[c]) for c, hh in enumerate(chains)]
    for slot in range(STAGE_DEPTH):
        for c, hh in enumerate(chains):
            _stage_scores(c, hh, qi_of[c] - 1 - slot, slot, kp_ref, qt_ref, sbuf_ref)
    gaps = []
    for c, hh in enumerate(chains):
        s = jnp.where(causal, scores[c], NEG)
        m = jnp.max(s, axis=0, keepdims=True)
        kmax2 = kmax_ref[:, hh * HEAD_DIM:hh * HEAD_DIM + 1]
        bound = jnp.sqrt(qn2[c] * kmax2) * BOUND_SLACK + slope2[hh] * r
        ref_ref[c] = m
        acc_ref[c] = _dot(vt_ref[hh, :, _blk(qi_of[c])], jnp.exp2(s - m).astype(BF16))
        gaps.append(jnp.max(bound - m))
    return gaps


def _first_block(gap, inv_blk, qi):
    reach = jnp.minimum((gap + SKIP_BITS) * inv_blk, 1e6).astype(jnp.int32)
    return jnp.maximum(qi - 1 - reach, 0)


def _fixed_ref_step(c, hh, n, slot, off, kp_ref, vt_ref, qt_ref, ref_ref, acc_ref, sbuf_ref):
    p = jnp.exp2(sbuf_ref[c * STAGE_DEPTH + slot] - (ref_ref[c] - off)).astype(BF16)
    _stage_scores(c, hh, n - STAGE_DEPTH, slot, kp_ref, qt_ref, sbuf_ref)
    acc_ref[c] += _dot(vt_ref[hh, :, _blk(jnp.maximum(n, 0))], p)


def _online_step(c, hh, n, off, kp_ref, vt_ref, qt_ref, m_ref, acc_ref):
    s = _dot(kp_ref[hh, _blk(n), :], qt_ref[c])
    m_old = m_ref[c]
    m_new = jnp.maximum(m_old, jnp.max(s, axis=0, keepdims=True) + off)
    p = jnp.exp2(s - (m_new - off)).astype(BF16)
    acc_ref[c] = acc_ref[c] * jnp.exp2(m_old - m_new) + _dot(vt_ref[hh, :, _blk(n)], p)
    m_ref[c] = m_new


def _softmax_sweep(qi_of, chains, gaps, depth, off_fn, kp_ref, vt_ref, qt_ref, ref_ref, acc_ref, sbuf_ref,
                   split_heads):
    refs = (kp_ref, vt_ref, qt_ref, ref_ref, acc_ref)
    assert STAGE_DEPTH == 2
    fixed_ok = functools.reduce(jnp.maximum, gaps) <= MAX_EXPONENT

    def run_fixed(start, count, heads):
        active = [(c, hh) for c, hh in enumerate(chains) if hh in heads]
        per_trip = max(STAGE_DEPTH, ITEMS_PER_TRIP // len(active))
        if len(heads) == 1:
            per_trip = min(per_trip, LONE_HEAD_BLOCKS_PER_TRIP)
        assert per_trip % STAGE_DEPTH == 0

        def sweep(start, trips, per_trip):
            def body(i, carry):
                for j in range(per_trip):
                    below = start + i * per_trip + j
                    for c, hh in active:
                        n = qi_of[c] - 1 - below
                        off = off_fn(c, hh, jnp.maximum(n, 0)) + jnp.where(n < 0, NEG, 0.0)
                        _fixed_ref_step(c, hh, n, j % STAGE_DEPTH, off, *refs, sbuf_ref)
                return carry

            lax.fori_loop(0, trips, body, 0)

        whole = count // per_trip
        sweep(start, whole, per_trip)
        if per_trip > STAGE_DEPTH:
            rest = count - whole * per_trip
            sweep(start + whole * per_trip, (rest + STAGE_DEPTH - 1) // STAGE_DEPTH, STAGE_DEPTH)
        else:
            sweep(start + whole * per_trip, count - whole * per_trip, 1)

    @pl.when(fixed_ok)
    def _fixed_reference():
        far = jnp.maximum(depth[0], depth[1])
        if split_heads:
            near = jnp.minimum(depth[0], depth[1])
            together = jnp.minimum((near + 1) // 2 * 2, far)
            run_fixed(0, together, (0, 1))
            for hh in range(HEADS_PER_BLOCK):
                run_fixed(together, jnp.maximum(depth[hh] - together, 0), (hh,))
        else:
            run_fixed(0, far, (0, 1))

    @pl.when(jnp.logical_not(fixed_ok))
    def _running_maximum():
        def body(n, carry):
            for c, hh in enumerate(chains):
                off = off_fn(c, hh, n) + jnp.where(n < qi_of[c], 0.0, NEG)
                _online_step(c, hh, n, off, *refs)
            return carry

        lax.fori_loop(0, functools.reduce(jnp.maximum, qi_of), body, 0)


def _normalized(acc):
    return acc[0:HEAD_DIM, :] * (1.0 / acc[HEAD_DIM:HEAD_DIM + 1, :])


def _row_is_head0():
    return lax.broadcasted_iota(jnp.int32, (LANES, ATT_T), 0) < HEAD_DIM


def _moba_kernel(tab_ref, q_ref, k_ref, v_ref, o_ref, kp_ref, vt_ref, kmax_ref, kmean_ref, selm_ref,
                 qt_ref, ref_ref, acc_ref, sbuf_ref, *, n_blocks, scale):
    hb = pl.program_id(1)
    nbp = kmean_ref.shape[0]
    _build_softmax_operands(tab_ref, hb, k_ref, v_ref, kp_ref, vt_ref, kmax_ref, n_blocks=n_blocks, scale=scale)
    kmean_ref[...] = jnp.zeros_like(kmean_ref)

    def block_mean(n, c):
        kmean_ref[pl.ds(n, 1), :] = jnp.mean(k_ref[_blk(n), :].astype(F32), axis=0, keepdims=True)
        return c

    lax.fori_loop(0, n_blocks, block_mean, 0, unroll=BUILD_UNROLL)

    blk_id = lax.broadcasted_iota(jnp.int32, (nbp, ATT_T), 0)
    blk_f = blk_id.astype(F32)
    chains = list(range(HEADS_PER_BLOCK))
    n_ch = len(chains)
    slope2 = [tab_ref[hb * HEADS_PER_BLOCK + hh] for hh in chains]
    assert n_blocks % TILES_PER_PASS == 0

    def tile_group(g, carry):
        tiles = [g * TILES_PER_PASS + s for s in range(TILES_PER_PASS)]
        qn2 = []
        for s, qi in enumerate(tiles):
            q_t = q_ref[_blk(qi), :].astype(F32).T
            for hh in chains:
                hm = _head_lanes(hh)
                qsel, norm2 = _chain_qt(s * n_ch + hh, hh, hh * HEAD_DIM, HEAD_DIM, q_t, qt_ref)
                qn2.append(norm2)

                gate = _dot(jnp.where(hm, kmean_ref[...], 0.0), qsel, precision=lax.Precision.HIGHEST)
                gt = jnp.where(blk_id < qi, gate, -jnp.inf)
                sel = jnp.zeros(gt.shape, dtype=jnp.bool_)
                for _ in range(MOBA_TOPK):
                    mx = jnp.max(gt, axis=0, keepdims=True)
                    pick = jnp.min(jnp.where(gt == mx, blk_f, float(nbp)), axis=0, keepdims=True)
                    hit = blk_f == pick
                    sel = sel | (hit & (mx > -jnp.inf))
                    gt = jnp.where(hit, -jnp.inf, gt)
                selm_ref[s * n_ch + hh] = jnp.where(sel, 0.0, NEG)

        qi_of = [tiles[c // n_ch] for c in range(TILES_PER_PASS * n_ch)]
        gaps = _diag_blocks(chains * TILES_PER_PASS, qi_of, qn2, slope2, kp_ref, vt_ref, kmax_ref, qt_ref,
                            ref_ref, acc_ref, sbuf_ref)
        all_chains = chains * TILES_PER_PASS
        depth = [None] * HEADS_PER_BLOCK
        for c, hh in enumerate(all_chains):
            d = qi_of[c] - _first_block(gaps[c], tab_ref[N_HEADS + hb * HEADS_PER_BLOCK + hh], qi_of[c])
            depth[hh] = d if depth[hh] is None else jnp.maximum(depth[hh], d)

        def off_fn(c, hh, n):
            return slope2[hh] * jnp.asarray((n - qi_of[c]) * ATT_T, F32) + selm_ref[c, pl.ds(n, 1), :]

        _softmax_sweep(qi_of, all_chains, gaps, depth, off_fn, kp_ref, vt_ref, qt_ref, ref_ref, acc_ref,
                       sbuf_ref, split_heads=False)
        for s, qi in enumerate(tiles):
            o = jnp.concatenate([_normalized(acc_ref[s * n_ch + hh]) for hh in chains], axis=0)
            o_ref[_blk(qi), :] = o.T.astype(o_ref.dtype)
        return carry

    lax.fori_loop(0, n_blocks // TILES_PER_PASS, tile_group, 0)


def _diff_kernel(tab_ref, lam_ref, g_ref, q_ref, k_ref, v_ref, o_ref, kp_ref, vt_ref, kmax_ref, qt_ref,
                 ref_ref, acc_ref, sbuf_ref, *, n_blocks, scale, lambda_init):
    hb = pl.program_id(1)
    _build_softmax_operands(tab_ref, hb, k_ref, v_ref, kp_ref, vt_ref, kmax_ref, n_blocks=n_blocks, scale=scale)

    lf = lam_ref[...].astype(F32)
    lam = (jnp.exp(jnp.sum(lf[0:1] * lf[1:2], axis=-1, keepdims=True))
           - jnp.exp(jnp.sum(lf[2:3] * lf[3:4], axis=-1, keepdims=True)) + lambda_init)

    chains = [hh for hh in range(HEADS_PER_BLOCK) for _ in range(2)]
    n_ch = len(chains)
    slope2 = [tab_ref[hb * HEADS_PER_BLOCK + hh] for hh in range(HEADS_PER_BLOCK)]
    assert n_blocks % TILES_PER_PASS == 0

    def tile_group(g, carry):
        tiles = [g * TILES_PER_PASS + s for s in range(TILES_PER_PASS)]
        qn2 = []
        for s, qi in enumerate(tiles):
            q_t = q_ref[_blk(qi), :].astype(F32).T
            for c, hh in enumerate(chains):
                seg_lo = hh * HEAD_DIM + (c % 2) * DIFF_QK_DIM
                qn2.append(_chain_qt(s * n_ch + c, hh, seg_lo, DIFF_QK_DIM, q_t, qt_ref)[1])
        qi_of = [tiles[c // n_ch] for c in range(TILES_PER_PASS * n_ch)]
        gaps = _diag_blocks(chains * TILES_PER_PASS, qi_of, qn2, slope2, kp_ref, vt_ref, kmax_ref, qt_ref,
                            ref_ref, acc_ref, sbuf_ref)
        all_chains = chains * TILES_PER_PASS
        depth = [None] * HEADS_PER_BLOCK
        for c, hh in enumerate(all_chains):
            d = qi_of[c] - _first_block(gaps[c], tab_ref[N_HEADS + hb * HEADS_PER_BLOCK + hh], qi_of[c])
            depth[hh] = d if depth[hh] is None else jnp.maximum(depth[hh], d)

        def off_fn(c, hh, n):
            return slope2[hh] * jnp.asarray((n - qi_of[c]) * ATT_T, F32)

        _softmax_sweep(qi_of, all_chains, gaps, depth, off_fn, kp_ref, vt_ref, qt_ref, ref_ref, acc_ref,
                       sbuf_ref, split_heads=True)
        for s, qi in enumerate(tiles):
            base = s * n_ch
            outs = []
            for hh in range(HEADS_PER_BLOCK):
                o = (_normalized(acc_ref[base + 2 * hh])
                     - lam * _normalized(acc_ref[base + 2 * hh + 1]))
                ms = jnp.mean(o * o, axis=0, keepdims=True)
                outs.append(o * lax.rsqrt(ms + SUBLN_EPS))
            o = jnp.concatenate(outs, axis=0).T * (g_ref[...] * (1.0 - lambda_init))
            o_ref[_blk(qi), :] = o.astype(o_ref.dtype)
        return carry

    lax.fori_loop(0, n_blocks // TILES_PER_PASS, tile_group, 0)


def _log_sigmoids(z):
    lneg = jnp.minimum(-z, 0.0) - jnp.log(1.0 + jnp.exp(-jnp.abs(z)))
    return lneg, lneg + z


def _suffix_sums(upper, lg):
    hi = lg.astype(BF16)
    lo = (lg - hi.astype(F32)).astype(BF16)
    return _dot(upper, hi) + _dot(upper, lo)


def _sb_pass(tiles, k_ref, vt_ref, qt_ref, carry_ref, acc_ref):
    krow, qcol = _key_query_iotas()
    before = krow < qcol
    upper = (qcol > krow).astype(BF16)
    items = [(slot * HEADS_PER_BLOCK + hh, hh, n, diag)
             for slot, blocks in tiles for n, diag in blocks for hh in range(HEADS_PER_BLOCK)]
    z = [_dot(k_ref[_blk(n), :], qt_ref[st]) for st, _, n, _ in items]
    log_sig, log_surv, tails = [], [], []
    for i, (st, hh, n, diag) in enumerate(items):
        lneg, lpos = _log_sigmoids(jnp.where(before, z[i], NEG) if diag else z[i])
        log_surv.append(lneg)
        log_sig.append(lpos)
        tails.append(_suffix_sums(upper, lneg))
    carry, acc = {}, {}
    for slot, blocks in tiles:
        for hh in range(HEADS_PER_BLOCK):
            st = slot * HEADS_PER_BLOCK + hh
            starts_fresh = blocks[0][1]
            carry[st] = None if starts_fresh else carry_ref[st]
            acc[st] = None if starts_fresh else acc_ref[st]
    for i, (st, hh, n, diag) in enumerate(items):
        w = jnp.exp(log_sig[i] + (tails[i] if diag else tails[i] + carry[st]))
        part = _dot(vt_ref[hh, :, _blk(n)], w.astype(BF16))
        acc[st] = part if acc[st] is None else acc[st] + part
        total = jnp.sum(log_surv[i], axis=0, keepdims=True)
        carry[st] = total if carry[st] is None else carry[st] + total
    worst = []
    for slot, _ in tiles:
        tops = []
        for hh in range(HEADS_PER_BLOCK):
            st = slot * HEADS_PER_BLOCK + hh
            acc_ref[st] = acc[st]
            carry_ref[st] = carry[st]
            tops.append(jnp.max(carry[st]))
        worst.append(functools.reduce(jnp.maximum, tops))
    return tuple(worst)


def _sb_kernel(q_ref, k_ref, v_ref, o_ref, vt_ref, qt_ref, carry_ref, acc_ref, *, n_blocks, scale):
    def transpose_values(n, c):
        v_t = v_ref[_blk(n), :].astype(F32).T.astype(BF16)
        for hh in range(HEADS_PER_BLOCK):
            vt_ref[hh, :, _blk(n)] = v_t[hh * HEAD_DIM:(hh + 1) * HEAD_DIM, :]
        return c

    lax.fori_loop(0, n_blocks, transpose_values, 0, unroll=BUILD_UNROLL)

    assert n_blocks % SB_TILES_PER_PASS == 0
    refs = (k_ref, vt_ref, qt_ref, carry_ref, acc_ref)
    row_h0 = _row_is_head0()

    def tile_group(g, carry):
        tiles = [g * SB_TILES_PER_PASS + slot for slot in range(SB_TILES_PER_PASS)]
        for slot, qi in enumerate(tiles):
            q_t = (q_ref[_blk(qi), :].astype(F32) * scale).T
            for hh in range(HEADS_PER_BLOCK):
                in_head = row_h0 if hh == 0 else jnp.logical_not(row_h0)
                qt_ref[slot * HEADS_PER_BLOCK + hh] = jnp.where(in_head, q_t, 0.0).astype(BF16)

        def blocks_of(qi, with_previous):
            return [(qi, True), (qi - 1, False)] if with_previous else [(qi, True)]

        worst = lax.cond(
            g > 0,
            lambda: _sb_pass([(slot, blocks_of(qi, True)) for slot, qi in enumerate(tiles)], *refs),
            lambda: _sb_pass([(slot, blocks_of(qi, slot > 0)) for slot, qi in enumerate(tiles)], *refs))

        for slot, qi in enumerate(tiles):
            def cond(c):
                j, live = c
                return (j >= 0) & live

            def body(c, slot=slot):
                j, _ = c
                return j - 1, _sb_pass([(slot, [(j, False)])], *refs)[0] > SB_EXIT

            lax.while_loop(cond, body, (qi - 2, worst[slot] > SB_EXIT))
            st = slot * HEADS_PER_BLOCK
            o = jnp.concatenate([acc_ref[st + hh] for hh in range(HEADS_PER_BLOCK)], axis=0)
            o_ref[_blk(qi), :] = o.T.astype(o_ref.dtype)
        return carry

    lax.fori_loop(0, n_blocks // SB_TILES_PER_PASS, tile_group, 0)


def _attn_specs(B, S, q_base, k_base, v_base):
    qspec = pl.BlockSpec((None, S, LANES), lambda b, h: (b, 0, q_base + h))
    kspec = pl.BlockSpec((None, S, LANES), lambda b, h: (b, 0, k_base + h))
    vspec = pl.BlockSpec((None, S, LANES), lambda b, h: (b, 0, v_base + h))
    ospec = pl.BlockSpec((None, S, LANES), lambda b, h: (b, 0, h))
    grid = (B, N_HEAD_BLOCKS)
    params = pltpu.CompilerParams(
        dimension_semantics=("parallel", "parallel"), vmem_limit_bytes=VMEM_LIMIT)
    return grid, qspec, kspec, vspec, ospec, params


_SMEM_SPEC = pl.BlockSpec(memory_space=pltpu.SMEM)


def _moba(qkv, tab):
    B, S, _ = qkv.shape
    grid, qspec, kspec, vspec, ospec, params = _attn_specs(B, S, 0, 3, 6)
    n_blocks = S // MOBA_BLOCK
    nbp = -(-n_blocks // SUBLANES) * SUBLANES
    n_slots = TILES_PER_PASS * HEADS_PER_BLOCK
    return pl.pallas_call(
        functools.partial(_moba_kernel, n_blocks=n_blocks, scale=HEAD_DIM ** -0.5),
        grid=grid,
        in_specs=[_SMEM_SPEC, qspec, kspec, vspec],
        out_specs=ospec,
        out_shape=jax.ShapeDtypeStruct((B, S, MIX_W), BF16),
        scratch_shapes=[
            pltpu.VMEM((HEADS_PER_BLOCK, S, LANES), BF16),
            pltpu.VMEM((HEADS_PER_BLOCK, V_ROWS, S), BF16),
            pltpu.VMEM((1, LANES), F32),
            pltpu.VMEM((nbp, LANES), F32),
            pltpu.VMEM((n_slots, nbp, ATT_T), F32),
            pltpu.VMEM((n_slots, LANES, ATT_T), BF16),
            pltpu.VMEM((n_slots, 1, ATT_T), F32),
            pltpu.VMEM((n_slots, V_ROWS, ATT_T), F32),
            pltpu.VMEM((n_slots * STAGE_DEPTH, ATT_T, ATT_T), F32),
        ],
        compiler_params=params,
        name="moba_attn",
    )(tab, qkv, qkv, qkv)


def _diff(qkv, tab, diff_lambda, subln_g, lambda_init):
    B, S, _ = qkv.shape
    grid, qspec, kspec, vspec, ospec, params = _attn_specs(B, S, 9, 12, 15)
    n_chains = TILES_PER_PASS * 2 * HEADS_PER_BLOCK
    return pl.pallas_call(
        functools.partial(_diff_kernel, n_blocks=S // ATT_T, scale=DIFF_QK_DIM ** -0.5,
                          lambda_init=lambda_init),
        grid=grid,
        in_specs=[
            _SMEM_SPEC,
            pl.BlockSpec(diff_lambda.shape, lambda b, h: (0, 0)),
            pl.BlockSpec((1, LANES), lambda b, h: (0, 0)),
            qspec, kspec, vspec,
        ],
        out_specs=ospec,
        out_shape=jax.ShapeDtypeStruct((B, S, MIX_W), BF16),
        scratch_shapes=[
            pltpu.VMEM((HEADS_PER_BLOCK, S, LANES), BF16),
            pltpu.VMEM((HEADS_PER_BLOCK, V_ROWS, S), BF16),
            pltpu.VMEM((1, LANES), F32),
            pltpu.VMEM((n_chains, LANES, ATT_T), BF16),
            pltpu.VMEM((n_chains, 1, ATT_T), F32),
            pltpu.VMEM((n_chains, V_ROWS, ATT_T), F32),
            pltpu.VMEM((n_chains * STAGE_DEPTH, ATT_T, ATT_T), F32),
        ],
        compiler_params=params,
        name="diff_attn",
    )(tab, diff_lambda, subln_g, qkv, qkv, qkv)


def _sb(qkv):
    B, S, _ = qkv.shape
    grid, qspec, kspec, vspec, ospec, params = _attn_specs(B, S, 18, 21, 24)
    return pl.pallas_call(
        functools.partial(_sb_kernel, n_blocks=S // ATT_T, scale=HEAD_DIM ** -0.5),
        grid=grid,
        in_specs=[qspec, kspec, vspec],
        out_specs=ospec,
        out_shape=jax.ShapeDtypeStruct((B, S, MIX_W), BF16),
        scratch_shapes=[
            pltpu.VMEM((HEADS_PER_BLOCK, HEAD_DIM, S), BF16),
            pltpu.VMEM((SB_TILES_PER_PASS * HEADS_PER_BLOCK, LANES, ATT_T), BF16),
            pltpu.VMEM((SB_TILES_PER_PASS * HEADS_PER_BLOCK, 1, ATT_T), F32),
            pltpu.VMEM((SB_TILES_PER_PASS * HEADS_PER_BLOCK, HEAD_DIM, ATT_T), F32),
        ],
        compiler_params=params,
        name="sb_attn",
    )(qkv, qkv, qkv)


def _merge_ln_kernel(x_ref, om_ref, od_ref, os_ref, wg_ref, bg_ref, wbr_ref, wo_ref, lg_ref, lb_ref,
                     o_ref, *, alpha, d_model):
    x = x_ref[...]
    xb = x.astype(BF16)
    merged = None
    for b, ob_ref in enumerate((om_ref, od_ref, os_ref)):
        glogit = _dot(xb, wg_ref[:, b * d_model:(b + 1) * d_model]) + bg_ref[b:b + 1, :]
        term = jax.nn.sigmoid(glogit) * _dot(ob_ref[...], wbr_ref[b])
        merged = term if merged is None else merged + term
    y = alpha * x + _dot(merged.astype(BF16), wo_ref[...])
    o_ref[...] = _layer_norm(y, lg_ref[...], lb_ref[...])


def _merge_ln(x, o_m, o_d, o_s, w_gate, b_gate, w_br, w_out, lg, lb, l, alpha, tm=512):
    T, D = x.shape
    return pl.pallas_call(
        functools.partial(_merge_ln_kernel, alpha=alpha, d_model=D),
        grid=(T // tm,),
        in_specs=[
            pl.BlockSpec((tm, D), lambda i: (i, 0)),
            pl.BlockSpec((tm, MIX_W), lambda i: (i, 0)),
            pl.BlockSpec((tm, MIX_W), lambda i: (i, 0)),
            pl.BlockSpec((tm, MIX_W), lambda i: (i, 0)),
            pl.BlockSpec((None, D, N_BRANCH * D), lambda i: (l, 0, 0), pipeline_mode=pl.Buffered(1)),
            pl.BlockSpec((None, N_BRANCH, D), lambda i: (l, 0, 0)),
            pl.BlockSpec((None, N_BRANCH, MIX_W, D), lambda i: (l, 0, 0, 0), pipeline_mode=pl.Buffered(1)),
            pl.BlockSpec((None, D, D), lambda i: (l, 0, 0), pipeline_mode=pl.Buffered(1)),
            pl.BlockSpec((1, D), lambda i: (0, 0)),
            pl.BlockSpec((1, D), lambda i: (0, 0)),
        ],
        out_specs=pl.BlockSpec((tm, D), lambda i: (i, 0)),
        out_shape=jax.ShapeDtypeStruct((T, D), F32),
        compiler_params=pltpu.CompilerParams(
            dimension_semantics=("parallel",), vmem_limit_bytes=VMEM_LIMIT),
        name="merge_ln",
    )(x, o_m, o_d, o_s, w_gate, b_gate, w_br, w_out, lg, lb)


def _alibi_slopes(n):
    return (2.0 ** (-8.0 * np.arange(1, n + 1, dtype=np.float32) / n)).astype(np.float32)


def _slope_table(slopes):
    s2 = slopes.astype(np.float64) * LOG2E
    return np.concatenate([s2, 1.0 / (s2 * ATT_T)]).astype(np.float32)


def kernel(x, ln_g, ln_b, ffn_w_gate, ffn_w_up, ffn_w_down, w_in, b_gate, diff_lambda, diff_subln_g,
           w_br_moba, w_br_diff, w_br_sb, w_out):
    B, S, D = x.shape
    depth = ln_g.shape[0]
    assert S % ATT_T == 0 and ATT_T == MOBA_BLOCK
    assert w_in.shape[-1] == QKV_W + N_BRANCH * D
    alpha = (2.0 * depth) ** 0.25

    wg = ffn_w_gate.astype(BF16)
    wu = ffn_w_up.astype(BF16)
    wd = ffn_w_down.astype(BF16)
    w_qkv = w_in[:, :, :QKV_W].astype(BF16)
    w_gate = w_in[:, :, QKV_W:].astype(BF16)
    w_br = jnp.stack([w_br_moba, w_br_diff, w_br_sb], axis=1).astype(BF16)
    w_out_b = w_out.astype(BF16)
    slopes = _alibi_slopes(2 * N_HEADS)
    tab_moba = jnp.asarray(_slope_table(slopes[0::2]))
    tab_diff = jnp.asarray(_slope_table(slopes[1::2]))
    subln_g = jnp.tile(diff_subln_g.astype(F32), (1, HEADS_PER_BLOCK))

    h = x.reshape(B * S, D)
    for l in range(depth):
        lg = ln_g[l][:, None, :]
        lb = ln_b[l][:, None, :]
        h = _ffn_ln(h, wg, wu, wd, lg[0], lb[0], l, 0, alpha)
        qkv = _qkv_proj(h, w_qkv, l).reshape(B, S, QKV_W)
        lambda_init = 0.8 - 0.6 * math.exp(-0.3 * l)
        o_m = _moba(qkv, tab_moba)
        o_d = _diff(qkv, tab_diff, diff_lambda[l], subln_g[l][None, :], lambda_init)
        o_s = _sb(qkv)
        T = B * S
        h = _merge_ln(h, o_m.reshape(T, MIX_W), o_d.reshape(T, MIX_W), o_s.reshape(T, MIX_W),
                      w_gate, b_gate, w_br, w_out_b, lg[1], lb[1], l, alpha)
        h = _ffn_ln(h, wg, wu, wd, lg[2], lb[2], l, 1, alpha)
    return h.reshape(B, S, D)
```

```python
import functools
import math

import numpy as np
import jax
import jax.numpy as jnp
from jax import lax
from jax.experimental import pallas as pl
from jax.experimental.pallas import tpu as pltpu

F32 = jnp.float32
BF16 = jnp.bfloat16

HEAD_DIM = 64
N_HEADS = 6
DIFF_QK_DIM = HEAD_DIM // 2
MOBA_BLOCK = 256
MOBA_TOPK = 3
N_BRANCH = 3
LN_EPS = 1e-5
SUBLN_EPS = 1e-5

LANES = 128
SUBLANES = 8
DENOM_ROWS = 16
V_ROWS = HEAD_DIM + DENOM_ROWS
HEADS_PER_BLOCK = LANES // HEAD_DIM
N_HEAD_BLOCKS = N_HEADS // HEADS_PER_BLOCK
MIX_W = N_HEADS * HEAD_DIM
QKV_W = 9 * MIX_W
ATT_T = 256
NEG = -1e30
SB_EXIT = -110.0
TILES_PER_PASS = 4
SB_TILES_PER_PASS = 4
LOG2E = math.log2(math.e)
N_BIAS_COLS = 3
BOUND_SLACK = 1.01
MAX_EXPONENT = 100.0
SKIP_BITS = 150.0
ITEMS_PER_TRIP = 32
LONE_HEAD_BLOCKS_PER_TRIP = 4
BUILD_UNROLL = 4
STAGE_DEPTH = 2
VMEM_LIMIT = 56 * 1024 * 1024


def _dot(a, b, precision=None):
    return jnp.dot(a, b, precision=precision, preferred_element_type=F32)


def _layer_norm(y, g, b):
    mu = jnp.mean(y, axis=-1, keepdims=True)
    yc = y - mu
    var = jnp.mean(yc * yc, axis=-1, keepdims=True)
    return yc * lax.rsqrt(var + LN_EPS) * g + b


def _ffn_ln_kernel(x_ref, wg_ref, wu_ref, wd_ref, lg_ref, lb_ref, o_ref, hid_ref, *, alpha, tf):
    x = x_ref[...]
    xb = x.astype(BF16)
    for c in range(hid_ref.shape[1] // tf):
        cols = slice(c * tf, (c + 1) * tf)
        gate = _dot(xb, wg_ref[:, cols])
        up = _dot(xb, wu_ref[:, cols])
        hid_ref[:, cols] = (gate * jax.nn.sigmoid(gate) * up).astype(BF16)
    y = alpha * x + 0.5 * _dot(hid_ref[...], wd_ref[...])
    o_ref[...] = _layer_norm(y, lg_ref[...], lb_ref[...])


def _ffn_ln(x, wg, wu, wd, lg, lb, l, j, alpha, tm=512, tf=256):
    T, D = x.shape
    FF = wg.shape[-1]
    resident = pl.Buffered(1)
    return pl.pallas_call(
        functools.partial(_ffn_ln_kernel, alpha=alpha, tf=tf),
        grid=(T // tm,),
        in_specs=[
            pl.BlockSpec((tm, D), lambda i: (i, 0)),
            pl.BlockSpec((None, None, D, FF), lambda i: (l, j, 0, 0), pipeline_mode=resident),
            pl.BlockSpec((None, None, D, FF), lambda i: (l, j, 0, 0), pipeline_mode=resident),
            pl.BlockSpec((None, None, FF, D), lambda i: (l, j, 0, 0), pipeline_mode=resident),
            pl.BlockSpec((1, D), lambda i: (0, 0)),
            pl.BlockSpec((1, D), lambda i: (0, 0)),
        ],
        out_specs=pl.BlockSpec((tm, D), lambda i: (i, 0)),
        out_shape=jax.ShapeDtypeStruct((T, D), F32),
        scratch_shapes=[pltpu.VMEM((tm, FF), BF16)],
        compiler_params=pltpu.CompilerParams(
            dimension_semantics=("parallel",), vmem_limit_bytes=VMEM_LIMIT),
        name="ffn_ln",
    )(x, wg, wu, wd, lg, lb)


def _proj_kernel(x_ref, w_ref, o_ref, *, tn):
    xb = x_ref[...].astype(BF16)
    for c in range(o_ref.shape[1] // tn):
        cols = slice(c * tn, (c + 1) * tn)
        o_ref[:, cols] = _dot(xb, w_ref[:, cols]).astype(o_ref.dtype)


def _qkv_proj(x, w_in, l, tm=512, tn=1152):
    T, D = x.shape
    return pl.pallas_call(
        functools.partial(_proj_kernel, tn=tn),
        grid=(T // tm,),
        in_specs=[
            pl.BlockSpec((tm, D), lambda i: (i, 0)),
            pl.BlockSpec((None, D, QKV_W), lambda i: (l, 0, 0), pipeline_mode=pl.Buffered(1)),
        ],
        out_specs=pl.BlockSpec((tm, QKV_W), lambda i: (i, 0)),
        out_shape=jax.ShapeDtypeStruct((T, QKV_W), BF16),
        compiler_params=pltpu.CompilerParams(
            dimension_semantics=("parallel",), vmem_limit_bytes=VMEM_LIMIT),
        name="qkv_proj",
    )(x, w_in)


def _lane_iota():
    return lax.broadcasted_iota(jnp.int32, (1, LANES), 1)


def _head_lanes(hh):
    lane = _lane_iota()
    return (lane >= hh * HEAD_DIM) & (lane < (hh + 1) * HEAD_DIM)


def _blk(j):
    return pl.ds(pl.multiple_of(j * ATT_T, ATT_T), ATT_T)


def _key_query_iotas():
    krow = lax.broadcasted_iota(jnp.int32, (ATT_T, ATT_T), 0)
    qcol = lax.broadcasted_iota(jnp.int32, (ATT_T, ATT_T), 1)
    return krow, qcol


def _build_softmax_operands(tab_ref, hb, k_ref, v_ref, kp_ref, vt_ref, kmax_ref, *, n_blocks, scale):
    lane = _lane_iota()
    pos = lax.broadcasted_iota(jnp.int32, (ATT_T, LANES), 0).astype(F32)
    bias_cols = []
    for hh in range(HEADS_PER_BLOCK):
        b = pos * tab_ref[hb * HEADS_PER_BLOCK + hh]
        lo = (1 - hh) * HEAD_DIM
        cols = jnp.zeros_like(b)
        for i in range(N_BIAS_COLS):
            piece = b.astype(BF16).astype(F32)
            cols = jnp.where(lane == lo + i, piece, cols)
            b = b - piece
        bias_cols.append(cols)
    kmax_ref[...] = jnp.zeros_like(kmax_ref)
    assert HEADS_PER_BLOCK == 2
    head_r = jnp.where(lax.broadcasted_iota(jnp.int32, (LANES, LANES), 0) < HEAD_DIM, 0, 1)
    head_c = jnp.where(lax.broadcasted_iota(jnp.int32, (LANES, LANES), 1) < HEAD_DIM, 0, 1)
    same_head = jnp.where(head_r == head_c, 1.0, 0.0).astype(BF16)

    def body(n, c):
        kf = k_ref[_blk(n), :].astype(F32) * (scale * LOG2E)
        v_t = v_ref[_blk(n), :].astype(F32).T.astype(BF16)
        for hh in range(HEADS_PER_BLOCK):
            kp_ref[hh, _blk(n), :] = jnp.where(_head_lanes(hh), kf, bias_cols[hh]).astype(BF16)
            vt_ref[hh, 0:HEAD_DIM, _blk(n)] = v_t[hh * HEAD_DIM:(hh + 1) * HEAD_DIM, :]
            vt_ref[hh, HEAD_DIM:V_ROWS, _blk(n)] = jnp.ones((DENOM_ROWS, ATT_T), BF16)
        norm2 = _dot((kf * kf).astype(BF16), same_head)
        kmax_ref[...] = jnp.maximum(kmax_ref[...], jnp.max(norm2, axis=0, keepdims=True))
        return c

    lax.fori_loop(0, n_blocks, body, 0, unroll=BUILD_UNROLL)


def _chain_qt(c, hh, seg_lo, seg_width, q_t, qt_ref):
    row = lax.broadcasted_iota(jnp.int32, (LANES, ATT_T), 0)
    lo = (1 - hh) * HEAD_DIM
    qsel = jnp.where((row >= seg_lo) & (row < seg_lo + seg_width), q_t, 0.0)
    qt_ref[c] = jnp.where((row >= lo) & (row < lo + N_BIAS_COLS), 1.0, qsel).astype(BF16)
    return qsel, jnp.sum(qsel * qsel, axis=0, keepdims=True)


def _stage_scores(c, hh, n, slot, kp_ref, qt_ref, sbuf_ref):
    sbuf_ref[c * STAGE_DEPTH + slot] = _dot(kp_ref[hh, _blk(jnp.maximum(n, 0)), :], qt_ref[c])


def _diag_blocks(chains, qi_of, qn2, slope2, kp_ref, vt_ref, kmax_ref, qt_ref, ref_ref, acc_ref, sbuf_ref):
    krow, qcol = _key_query_iotas()
    causal = krow <= qcol
    r = lax.broadcasted_iota(jnp.int32, (1, ATT_T), 1).astype(F32)
    scores = [_dot(kp_ref[hh, _blk(qi_of[c]), :], qt_ref[c]) for c, hh in enumerate(chains)]
    for slot in range(STAGE_DEPTH):
        for c, hh in enumerate(chains):
            _stage_scores(c, hh, qi_of[c] - 1 - slot, slot, kp_ref, qt_ref, sbuf_ref)
    gaps = []
    for c, hh in enumerate(chains):
        s = jnp.where(causal, scores[c], NEG)
        m = jnp.max(s, axis=0, keepdims=True)
        kmax2 = kmax_ref[:, hh * HEAD_DIM:hh * HEAD_DIM + 1]
        bound = jnp.sqrt(qn2[c] * kmax2) * BOUND_SLACK + slope2[hh] * r
        ref_ref[c] = m
        acc_ref[c] = _dot(vt_ref[hh, :, _blk(qi_of[c])], jnp.exp2(s - m).astype(BF16))
        gaps.append(jnp.max(bound - m))
    return gaps


def _first_block(gap, inv_blk, qi):
    reach = jnp.minimum((gap + SKIP_BITS) * inv_blk, 1e6).astype(jnp.int32)
    return jnp.maximum(qi - 1 - reach, 0)


def _fixed_ref_step(c, hh, n, slot, off, kp_ref, vt_ref, qt_ref, ref_ref, acc_ref, sbuf_ref):
    p = jnp.exp2(sbuf_ref[c * STAGE_DEPTH + slot] - (ref_ref[c] - off)).astype(BF16)
    _stage_scores(c, hh, n - STAGE_DEPTH, slot, kp_ref, qt_ref, sbuf_ref)
    acc_ref[c] += _dot(vt_ref[hh, :, _blk(jnp.maximum(n, 0))], p)


def _online_step(c, hh, n, off, kp_ref, vt_ref, qt_ref, m_ref, acc_ref):
    s = _dot(kp_ref[hh, _blk(n), :], qt_ref[c])
    m_old = m_ref[c]
    m_new = jnp.maximum(m_old, jnp.max(s, axis=0, keepdims=True) + off)
    p = jnp.exp2(s - (m_new - off)).astype(BF16)
    acc_ref[c] = acc_ref[c] * jnp.exp2(m_old - m_new) + _dot(vt_ref[hh, :, _blk(n)], p)
    m_ref[c] = m_new


def _softmax_sweep(qi_of, chains, gaps, depth, off_fn, kp_ref, vt_ref, qt_ref, ref_ref, acc_ref, sbuf_ref,
                   split_heads):
    refs = (kp_ref, vt_ref, qt_ref, ref_ref, acc_ref)
    assert STAGE_DEPTH == 2
    fixed_ok = functools.reduce(jnp.maximum, gaps) <= MAX_EXPONENT

    def run_fixed(start, count, heads):
        active = [(c, hh) for c, hh in enumerate(chains) if hh in heads]
        per_trip = max(STAGE_DEPTH, ITEMS_PER_TRIP // len(active))
        if len(heads) == 1:
            per_trip = min(per_trip, LONE_HEAD_BLOCKS_PER_TRIP)
        assert per_trip % STAGE_DEPTH == 0

        def sweep(start, trips, per_trip):
            def body(i, carry):
                for j in range(per_trip):
                    below = start + i * per_trip + j
                    for c, hh in active:
                        n = qi_of[c] - 1 - below
                        off = off_fn(c, hh, jnp.maximum(n, 0)) + jnp.where(n < 0, NEG, 0.0)
                        _fixed_ref_step(c, hh, n, j % STAGE_DEPTH, off, *refs, sbuf_ref)
                return carry

            lax.fori_loop(0, trips, body, 0)

        whole = count // per_trip
        sweep(start, whole, per_trip)
        if per_trip > STAGE_DEPTH:
            rest = count - whole * per_trip
            sweep(start + whole * per_trip, (rest + STAGE_DEPTH - 1) // STAGE_DEPTH, STAGE_DEPTH)
        else:
            sweep(start + whole * per_trip, count - whole * per_trip, 1)

    @pl.when(fixed_ok)
    def _fixed_reference():
        far = jnp.maximum(depth[0], depth[1])
        if split_heads:
            near = jnp.minimum(depth[0], depth[1])
            together = jnp.minimum((near + 1) // 2 * 2, far)
            run_fixed(0, together, (0, 1))
            for hh in range(HEADS_PER_BLOCK):
                run_fixed(together, jnp.maximum(depth[hh] - together, 0), (hh,))
        else:
            run_fixed(0, far, (0, 1))

    @pl.when(jnp.logical_not(fixed_ok))
    def _running_maximum():
        def body(n, carry):
            for c, hh in enumerate(chains):
                off = off_fn(c, hh, n) + jnp.where(n < qi_of[c], 0.0, NEG)
                _online_step(c, hh, n, off, *refs)
            return carry

        lax.fori_loop(0, functools.reduce(jnp.maximum, qi_of), body, 0)


def _normalized(acc):
    return acc[0:HEAD_DIM, :] * (1.0 / acc[HEAD_DIM:HEAD_DIM + 1, :])


def _row_is_head0():
    return lax.broadcasted_iota(jnp.int32, (LANES, ATT_T), 0) < HEAD_DIM


def _moba_kernel(tab_ref, q_ref, k_ref, v_ref, o_ref, kp_ref, vt_ref, kmax_ref, kmean_ref, selm_ref,
                 qt_ref, ref_ref, acc_ref, sbuf_ref, *, n_blocks, scale):
    hb = pl.program_id(1)
    nbp = kmean_ref.shape[0]
    _build_softmax_operands(tab_ref, hb, k_ref, v_ref, kp_ref, vt_ref, kmax_ref, n_blocks=n_blocks, scale=scale)
    kmean_ref[...] = jnp.zeros_like(kmean_ref)

    def block_mean(n, c):
        kmean_ref[pl.ds(n, 1), :] = jnp.mean(k_ref[_blk(n), :].astype(F32), axis=0, keepdims=True)
        return c

    lax.fori_loop(0, n_blocks, block_mean, 0, unroll=BUILD_UNROLL)

    blk_id = lax.broadcasted_iota(jnp.int32, (nbp, ATT_T), 0)
    blk_f = blk_id.astype(F32)
    chains = list(range(HEADS_PER_BLOCK))
    n_ch = len(chains)
    slope2 = [tab_ref[hb * HEADS_PER_BLOCK + hh] for hh in chains]
    assert n_blocks % TILES_PER_PASS == 0

    def tile_group(g, carry):
        tiles = [g * TILES_PER_PASS + s for s in range(TILES_PER_PASS)]
        qn2 = []
        for s, qi in enumerate(tiles):
            q_t = q_ref[_blk(qi), :].astype(F32).T
            for hh in chains:
                hm = _head_lanes(hh)
                qsel, norm2 = _chain_qt(s * n_ch + hh, hh, hh * HEAD_DIM, HEAD_DIM, q_t, qt_ref)
                qn2.append(norm2)

                gate = _dot(jnp.where(hm, kmean_ref[...], 0.0), qsel, precision=lax.Precision.HIGHEST)
                gt = jnp.where(blk_id < qi, gate, -jnp.inf)
                sel = jnp.zeros(gt.shape, dtype=jnp.bool_)
                for _ in range(MOBA_TOPK):
                    mx = jnp.max(gt, axis=0, keepdims=True)
                    pick = jnp.min(jnp.where(gt == mx, blk_f, float(nbp)), axis=0, keepdims=True)
                    hit = blk_f == pick
                    sel = sel | (hit & (mx > -jnp.inf))
                    gt = jnp.where(hit, -jnp.inf, gt)
                selm_ref[s * n_ch + hh] = jnp.where(sel, 0.0, NEG)

        qi_of = [tiles[c // n_ch] for c in range(TILES_PER_PASS * n_ch)]
        gaps = _diag_blocks(chains * TILES_PER_PASS, qi_of, qn2, slope2, kp_ref, vt_ref, kmax_ref, qt_ref,
                            ref_ref, acc_ref, sbuf_ref)
        all_chains = chains * TILES_PER_PASS
        depth = [None] * HEADS_PER_BLOCK
        for c, hh in enumerate(all_chains):
            d = qi_of[c] - _first_block(gaps[c], tab_ref[N_HEADS + hb * HEADS_PER_BLOCK + hh], qi_of[c])
            depth[hh] = d if depth[hh] is None else jnp.maximum(depth[hh], d)

        def off_fn(c, hh, n):
            return slope2[hh] * jnp.asarray((n - qi_of[c]) * ATT_T, F32) + selm_ref[c, pl.ds(n, 1), :]

        _softmax_sweep(qi_of, all_chains, gaps, depth, off_fn, kp_ref, vt_ref, qt_ref, ref_ref, acc_ref,
                       sbuf_ref, split_heads=False)
        for s, qi in enumerate(tiles):
            o = jnp.concatenate([_normalized(acc_ref[s * n_ch + hh]) for hh in chains], axis=0)
            o_ref[_blk(qi), :] = o.T.astype(o_ref.dtype)
        return carry

    lax.fori_loop(0, n_blocks // TILES_PER_PASS, tile_group, 0)


def _diff_kernel(tab_ref, lam_ref, g_ref, q_ref, k_ref, v_ref, o_ref, kp_ref, vt_ref, kmax_ref, qt_ref,
                 ref_ref, acc_ref, sbuf_ref, *, n_blocks, scale, lambda_init):
    hb = pl.program_id(1)
    _build_softmax_operands(tab_ref, hb, k_ref, v_ref, kp_ref, vt_ref, kmax_ref, n_blocks=n_blocks, scale=scale)

    lf = lam_ref[...].astype(F32)
    lam = (jnp.exp(jnp.sum(lf[0:1] * lf[1:2], axis=-1, keepdims=True))
           - jnp.exp(jnp.sum(lf[2:3] * lf[3:4], axis=-1, keepdims=True)) + lambda_init)

    chains = [hh for hh in range(HEADS_PER_BLOCK) for _ in range(2)]
    n_ch = len(chains)
    slope2 = [tab_ref[hb * HEADS_PER_BLOCK + hh] for hh in range(HEADS_PER_BLOCK)]
    assert n_blocks % TILES_PER_PASS == 0

    def tile_group(g, carry):
        tiles = [g * TILES_PER_PASS + s for s in range(TILES_PER_PASS)]
        qn2 = []
        for s, qi in enumerate(tiles):
            q_t = q_ref[_blk(qi), :].astype(F32).T
            for c, hh in enumerate(chains):
                seg_lo = hh * HEAD_DIM + (c % 2) * DIFF_QK_DIM
                qn2.append(_chain_qt(s * n_ch + c, hh, seg_lo, DIFF_QK_DIM, q_t, qt_ref)[1])
        qi_of = [tiles[c // n_ch] for c in range(TILES_PER_PASS * n_ch)]
        gaps = _diag_blocks(chains * TILES_PER_PASS, qi_of, qn2, slope2, kp_ref, vt_ref, kmax_ref, qt_ref,
                            ref_ref, acc_ref, sbuf_ref)
        all_chains = chains * TILES_PER_PASS
        depth = [None] * HEADS_PER_BLOCK
        for c, hh in enumerate(all_chains):
            d = qi_of[c] - _first_block(gaps[c], tab_ref[N_HEADS + hb * HEADS_PER_BLOCK + hh], qi_of[c])
            depth[hh] = d if depth[hh] is None else jnp.maximum(depth[hh], d)

        def off_fn(c, hh, n):
            return slope2[hh] * jnp.asarray((n - qi_of[c]) * ATT_T, F32)

        _softmax_sweep(qi_of, all_chains, gaps, depth, off_fn, kp_ref, vt_ref, qt_ref, ref_ref, acc_ref,
                       sbuf_ref, split_heads=True)
        for s, qi in enumerate(tiles):
            base = s * n_ch
            outs = []
            for hh in range(HEADS_PER_BLOCK):
                o = (_normalized(acc_ref[base + 2 * hh])
                     - lam * _normalized(acc_ref[base + 2 * hh + 1]))
                ms = jnp.mean(o * o, axis=0, keepdims=True)
                outs.append(o * lax.rsqrt(ms + SUBLN_EPS))
            o = jnp.concatenate(outs, axis=0).T * (g_ref[...] * (1.0 - lambda_init))
            o_ref[_blk(qi), :] = o.astype(o_ref.dtype)
        return carry

    lax.fori_loop(0, n_blocks // TILES_PER_PASS, tile_group, 0)


def _log_sigmoids(z):
    lneg = jnp.minimum(-z, 0.0) - jnp.log(1.0 + jnp.exp(-jnp.abs(z)))
    return lneg, lneg + z


def _suffix_sums(upper, lg):
    hi = lg.astype(BF16)
    lo = (lg - hi.astype(F32)).astype(BF16)
    return _dot(upper, hi) + _dot(upper, lo)


def _sb_pass(tiles, k_ref, vt_ref, qt_ref, carry_ref, acc_ref):
    krow, qcol = _key_query_iotas()
    before = krow < qcol
    upper = (qcol > krow).astype(BF16)
    items = [(slot * HEADS_PER_BLOCK + hh, hh, n, diag)
             for slot, blocks in tiles for n, diag in blocks for hh in range(HEADS_PER_BLOCK)]
    z = [_dot(k_ref[_blk(n), :], qt_ref[st]) for st, _, n, _ in items]
    log_sig, log_surv, tails = [], [], []
    for i, (st, hh, n, diag) in enumerate(items):
        lneg, lpos = _log_sigmoids(jnp.where(before, z[i], NEG) if diag else z[i])
        log_surv.append(lneg)
        log_sig.append(lpos)
        tails.append(_suffix_sums(upper, lneg))
    carry, acc = {}, {}
    for slot, blocks in tiles:
        for hh in range(HEADS_PER_BLOCK):
            st = slot * HEADS_PER_BLOCK + hh
            starts_fresh = blocks[0][1]
            carry[st] = None if starts_fresh else carry_ref[st]
            acc[st] = None if starts_fresh else acc_ref[st]
    for i, (st, hh, n, diag) in enumerate(items):
        w = jnp.exp(log_sig[i] + (tails[i] if diag else tails[i] + carry[st]))
        part = _dot(vt_ref[hh, :, _blk(n)], w.astype(BF16))
        acc[st] = part if acc[st] is None else acc[st] + part
        total = jnp.sum(log_surv[i], axis=0, keepdims=True)
        carry[st] = total if carry[st] is None else carry[st] + total
    worst = []
    for slot, _ in tiles:
        tops = []
        for hh in range(HEADS_PER_BLOCK):
            st = slot * HEADS_PER_BLOCK + hh
            acc_ref[st] = acc[st]
            carry_ref[st] = carry[st]
            tops.append(jnp.max(carry[st]))
        worst.append(functools.reduce(jnp.maximum, tops))
    return tuple(worst)


def _sb_kernel(q_ref, k_ref, v_ref, o_ref, vt_ref, qt_ref, carry_ref, acc_ref, *, n_blocks, scale):
    def transpose_values(n, c):
        v_t = v_ref[_blk(n), :].astype(F32).T.astype(BF16)
        for hh in range(HEADS_PER_BLOCK):
            vt_ref[hh, :, _blk(n)] = v_t[hh * HEAD_DIM:(hh + 1) * HEAD_DIM, :]
        return c

    lax.fori_loop(0, n_blocks, transpose_values, 0, unroll=BUILD_UNROLL)

    assert n_blocks % SB_TILES_PER_PASS == 0
    refs = (k_ref, vt_ref, qt_ref, carry_ref, acc_ref)
    row_h0 = _row_is_head0()

    def tile_group(g, carry):
        tiles = [g * SB_TILES_PER_PASS + slot for slot in range(SB_TILES_PER_PASS)]
        for slot, qi in enumerate(tiles):
            q_t = (q_ref[_blk(qi), :].astype(F32) * scale).T
            for hh in range(HEADS_PER_BLOCK):
                in_head = row_h0 if hh == 0 else jnp.logical_not(row_h0)
                qt_ref[slot * HEADS_PER_BLOCK + hh] = jnp.where(in_head, q_t, 0.0).astype(BF16)

        def blocks_of(qi, with_previous):
            return [(qi, True), (qi - 1, False)] if with_previous else [(qi, True)]

        worst = lax.cond(
            g > 0,
            lambda: _sb_pass([(slot, blocks_of(qi, True)) for slot, qi in enumerate(tiles)], *refs),
            lambda: _sb_pass([(slot, blocks_of(qi, slot > 0)) for slot, qi in enumerate(tiles)], *refs))

        for slot, qi in enumerate(tiles):
            def cond(c):
                j, live = c
                return (j >= 0) & live

            def body(c, slot=slot):
                j, _ = c
                return j - 1, _sb_pass([(slot, [(j, False)])], *refs)[0] > SB_EXIT

            lax.while_loop(cond, body, (qi - 2, worst[slot] > SB_EXIT))
            st = slot * HEADS_PER_BLOCK
            o = jnp.concatenate([acc_ref[st + hh] for hh in range(HEADS_PER_BLOCK)], axis=0)
            o_ref[_blk(qi), :] = o.T.astype(o_ref.dtype)
        return carry

    lax.fori_loop(0, n_blocks // SB_TILES_PER_PASS, tile_group, 0)


def _attn_specs(B, S, q_base, k_base, v_base):
    qspec = pl.BlockSpec((None, S, LANES), lambda b, h: (b, 0, q_base + h))
    kspec = pl.BlockSpec((None, S, LANES), lambda b, h: (b, 0, k_base + h))
    vspec = pl.BlockSpec((None, S, LANES), lambda b, h: (b, 0, v_base + h))
    ospec = pl.BlockSpec((None, S, LANES), lambda b, h: (b, 0, h))
    grid = (B, N_HEAD_BLOCKS)
    params = pltpu.CompilerParams(
        dimension_semantics=("parallel", "parallel"), vmem_limit_bytes=VMEM_LIMIT)
    return grid, qspec, kspec, vspec, ospec, params


_SMEM_SPEC = pl.BlockSpec(memory_space=pltpu.SMEM)


def _moba(qkv, tab):
    B, S, _ = qkv.shape
    grid, qspec, kspec, vspec, ospec, params = _attn_specs(B, S, 0, 3, 6)
    n_blocks = S // MOBA_BLOCK
    nbp = -(-n_blocks // SUBLANES) * SUBLANES
    n_slots = TILES_PER_PASS * HEADS_PER_BLOCK
    return pl.pallas_call(
        functools.partial(_moba_kernel, n_blocks=n_blocks, scale=HEAD_DIM ** -0.5),
        grid=grid,
        in_specs=[_SMEM_SPEC, qspec, kspec, vspec],
        out_specs=ospec,
        out_shape=jax.ShapeDtypeStruct((B, S, MIX_W), BF16),
        scratch_shapes=[
            pltpu.VMEM((HEADS_PER_BLOCK, S, LANES), BF16),
            pltpu.VMEM((HEADS_PER_BLOCK, V_ROWS, S), BF16),
            pltpu.VMEM((1, LANES), F32),
            pltpu.VMEM((nbp, LANES), F32),
            pltpu.VMEM((n_slots, nbp, ATT_T), F32),
            pltpu.VMEM((n_slots, LANES, ATT_T), BF16),
            pltpu.VMEM((n_slots, 1, ATT_T), F32),
            pltpu.VMEM((n_slots, V_ROWS, ATT_T), F32),
            pltpu.VMEM((n_slots * STAGE_DEPTH, ATT_T, ATT_T), F32),
        ],
        compiler_params=params,
        name="moba_attn",
    )(tab, qkv, qkv, qkv)


def _diff(qkv, tab, diff_lambda, subln_g, lambda_init):
    B, S, _ = qkv.shape
    grid, qspec, kspec, vspec, ospec, params = _attn_specs(B, S, 9, 12, 15)
    n_chains = TILES_PER_PASS * 2 * HEADS_PER_BLOCK
    return pl.pallas_call(
        functools.partial(_diff_kernel, n_blocks=S // ATT_T, scale=DIFF_QK_DIM ** -0.5,
                          lambda_init=lambda_init),
        grid=grid,
        in_specs=[
            _SMEM_SPEC,
            pl.BlockSpec(diff_lambda.shape, lambda b, h: (0, 0)),
            pl.BlockSpec((1, LANES), lambda b, h: (0, 0)),
            qspec, kspec, vspec,
        ],
        out_specs=ospec,
        out_shape=jax.ShapeDtypeStruct((B, S, MIX_W), BF16),
        scratch_shapes=[
            pltpu.VMEM((HEADS_PER_BLOCK, S, LANES), BF16),
            pltpu.VMEM((HEADS_PER_BLOCK, V_ROWS, S), BF16),
            pltpu.VMEM((1, LANES), F32),
            pltpu.VMEM((n_chains, LANES, ATT_T), BF16),
            pltpu.VMEM((n_chains, 1, ATT_T), F32),
            pltpu.VMEM((n_chains, V_ROWS, ATT_T), F32),
            pltpu.VMEM((n_chains * STAGE_DEPTH, ATT_T, ATT_T), F32),
        ],
        compiler_params=params,
        name="diff_attn",
    )(tab, diff_lambda, subln_g, qkv, qkv, qkv)


def _sb(qkv):
    B, S, _ = qkv.shape
    grid, qspec, kspec, vspec, ospec, params = _attn_specs(B, S, 18, 21, 24)
    return pl.pallas_call(
        functools.partial(_sb_kernel, n_blocks=S // ATT_T, scale=HEAD_DIM ** -0.5),
        grid=grid,
        in_specs=[qspec, kspec, vspec],
        out_specs=ospec,
        out_shape=jax.ShapeDtypeStruct((B, S, MIX_W), BF16),
        scratch_shapes=[
            pltpu.VMEM((HEADS_PER_BLOCK, HEAD_DIM, S), BF16),
            pltpu.VMEM((SB_TILES_PER_PASS * HEADS_PER_BLOCK, LANES, ATT_T), BF16),
            pltpu.VMEM((SB_TILES_PER_PASS * HEADS_PER_BLOCK, 1, ATT_T), F32),
            pltpu.VMEM((SB_TILES_PER_PASS * HEADS_PER_BLOCK, HEAD_DIM, ATT_T), F32),
        ],
        compiler_params=params,
        name="sb_attn",
    )(qkv, qkv, qkv)


def _merge_ln_kernel(x_ref, om_ref, od_ref, os_ref, wg_ref, bg_ref, wbr_ref, wo_ref, lg_ref, lb_ref,
                     o_ref, *, alpha, d_model):
    x = x_ref[...]
    xb = x.astype(BF16)
    merged = None
    for b, ob_ref in enumerate((om_ref, od_ref, os_ref)):
        glogit = _dot(xb, wg_ref[:, b * d_model:(b + 1) * d_model]) + bg_ref[b:b + 1, :]
        term = jax.nn.sigmoid(glogit) * _dot(ob_ref[...], wbr_ref[b])
        merged = term if merged is None else merged + term
    y = alpha * x + _dot(merged.astype(BF16), wo_ref[...])
    o_ref[...] = _layer_norm(y, lg_ref[...], lb_ref[...])


def _merge_ln(x, o_m, o_d, o_s, w_gate, b_gate, w_br, w_out, lg, lb, l, alpha, tm=512):
    T, D = x.shape
    return pl.pallas_call(
        functools.partial(_merge_ln_kernel, alpha=alpha, d_model=D),
        grid=(T // tm,),
        in_specs=[
            pl.BlockSpec((tm, D), lambda i: (i, 0)),
            pl.BlockSpec((tm, MIX_W), lambda i: (i, 0)),
            pl.BlockSpec((tm, MIX_W), lambda i: (i, 0)),
            pl.BlockSpec((tm, MIX_W), lambda i: (i, 0)),
            pl.BlockSpec((None, D, N_BRANCH * D), lambda i: (l, 0, 0), pipeline_mode=pl.Buffered(1)),
            pl.BlockSpec((None, N_BRANCH, D), lambda i: (l, 0, 0)),
            pl.BlockSpec((None, N_BRANCH, MIX_W, D), lambda i: (l, 0, 0, 0), pipeline_mode=pl.Buffered(1)),
            pl.BlockSpec((None, D, D), lambda i: (l, 0, 0), pipeline_mode=pl.Buffered(1)),
            pl.BlockSpec((1, D), lambda i: (0, 0)),
            pl.BlockSpec((1, D), lambda i: (0, 0)),
        ],
        out_specs=pl.BlockSpec((tm, D), lambda i: (i, 0)),
        out_shape=jax.ShapeDtypeStruct((T, D), F32),
        compiler_params=pltpu.CompilerParams(
            dimension_semantics=("parallel",), vmem_limit_bytes=VMEM_LIMIT),
        name="merge_ln",
    )(x, o_m, o_d, o_s, w_gate, b_gate, w_br, w_out, lg, lb)


def _alibi_slopes(n):
    return (2.0 ** (-8.0 * np.arange(1, n + 1, dtype=np.float32) / n)).astype(np.float32)


def _slope_table(slopes):
    s2 = slopes.astype(np.float64) * LOG2E
    return np.concatenate([s2, 1.0 / (s2 * ATT_T)]).astype(np.float32)


def kernel(x, ln_g, ln_b, ffn_w_gate, ffn_w_up, ffn_w_down, w_in, b_gate, diff_lambda, diff_subln_g,
           w_br_moba, w_br_diff, w_br_sb, w_out):
    B, S, D = x.shape
    depth = ln_g.shape[0]
    assert S % ATT_T == 0 and ATT_T == MOBA_BLOCK
    assert w_in.shape[-1] == QKV_W + N_BRANCH * D
    alpha = (2.0 * depth) ** 0.25

    wg = ffn_w_gate.astype(BF16)
    wu = ffn_w_up.astype(BF16)
    wd = ffn_w_down.astype(BF16)
    w_qkv = w_in[:, :, :QKV_W].astype(BF16)
    w_gate = w_in[:, :, QKV_W:].astype(BF16)
    w_br = jnp.stack([w_br_moba, w_br_diff, w_br_sb], axis=1).astype(BF16)
    w_out_b = w_out.astype(BF16)
    slopes = _alibi_slopes(2 * N_HEADS)
    tab_moba = jnp.asarray(_slope_table(slopes[0::2]))
    tab_diff = jnp.asarray(_slope_table(slopes[1::2]))
    subln_g = jnp.tile(diff_subln_g.astype(F32), (1, HEADS_PER_BLOCK))

    h = x.reshape(B * S, D)
    for l in range(depth):
        lg = ln_g[l][:, None, :]
        lb = ln_b[l][:, None, :]
        h = _ffn_ln(h, wg, wu, wd, lg[0], lb[0], l, 0, alpha)
        qkv = _qkv_proj(h, w_qkv, l).reshape(B, S, QKV_W)
        lambda_init = 0.8 - 0.6 * math.exp(-0.3 * l)
        o_m = _moba(qkv, tab_moba)
        o_d = _diff(qkv, tab_diff, diff_lambda[l], subln_g[l][None, :], lambda_init)
        o_s = _sb(qkv)
        T = B * S
        h = _merge_ln(h, o_m.reshape(T, MIX_W), o_d.reshape(T, MIX_W), o_s.reshape(T, MIX_W),
                      w_gate, b_gate, w_br, w_out_b, lg[1], lb[1], l, alpha)
        h = _ffn_ln(h, wg, wu, wd, lg[2], lb[2], l, 1, alpha)
    return h.reshape(B, S, D)
```

```python
import functools
import math

import numpy as np
import jax
import jax.numpy as jnp
from jax import lax
from jax.experimental import pallas as pl
from jax.experimental.pallas import tpu as pltpu

F32 = jnp.float32
BF16 = jnp.bfloat16

HEAD_DIM = 64
N_HEADS = 6
DIFF_QK_DIM = HEAD_DIM // 2
MOBA_BLOCK = 256
MOBA_TOPK = 3
N_BRANCH = 3
LN_EPS = 1e-5
SUBLN_EPS = 1e-5

LANES = 128
SUBLANES = 8
DENOM_ROWS = 16
V_ROWS = HEAD_DIM + DENOM_ROWS
HEADS_PER_BLOCK = LANES // HEAD_DIM
N_HEAD_BLOCKS = N_HEADS // HEADS_PER_BLOCK
MIX_W = N_HEADS * HEAD_DIM
QKV_W = 9 * MIX_W
ATT_T = 256
NEG = -1e30
SB_EXIT = -110.0
TILES_PER_PASS = 8
SB_TILES_PER_PASS = 4
LOG2E = math.log2(math.e)
N_BIAS_COLS = 3
BOUND_SLACK = 1.01
MAX_EXPONENT = 100.0
SKIP_BITS = 150.0
ITEMS_PER_TRIP = 32
LONE_HEAD_BLOCKS_PER_TRIP = 4
BUILD_UNROLL = 4
STAGE_DEPTH = 2
VMEM_LIMIT = 56 * 1024 * 1024


def _dot(a, b, precision=None):
    return jnp.dot(a, b, precision=precision, preferred_element_type=F32)


def _layer_norm(y, g, b):
    mu = jnp.mean(y, axis=-1, keepdims=True)
    yc = y - mu
    var = jnp.mean(yc * yc, axis=-1, keepdims=True)
    return yc * lax.rsqrt(var + LN_EPS) * g + b


def _ffn_ln_kernel(x_ref, wg_ref, wu_ref, wd_ref, lg_ref, lb_ref, o_ref, hid_ref, *, alpha, tf):
    x = x_ref[...]
    xb = x.astype(BF16)
    for c in range(hid_ref.shape[1] // tf):
        cols = slice(c * tf, (c + 1) * tf)
        gate = _dot(xb, wg_ref[:, cols])
        up = _dot(xb, wu_ref[:, cols])
        hid_ref[:, cols] = (gate * jax.nn.sigmoid(gate) * up).astype(BF16)
    y = alpha * x + 0.5 * _dot(hid_ref[...], wd_ref[...])
    o_ref[...] = _layer_norm(y, lg_ref[...], lb_ref[...])


def _ffn_ln(x, wg, wu, wd, lg, lb, l, j, alpha, tm=512, tf=256):
    T, D = x.shape
    FF = wg.shape[-1]
    resident = pl.Buffered(1)
    return pl.pallas_call(
        functools.partial(_ffn_ln_kernel, alpha=alpha, tf=tf),
        grid=(T // tm,),
        in_specs=[
            pl.BlockSpec((tm, D), lambda i: (i, 0)),
            pl.BlockSpec((None, None, D, FF), lambda i: (l, j, 0, 0), pipeline_mode=resident),
            pl.BlockSpec((None, None, D, FF), lambda i: (l, j, 0, 0), pipeline_mode=resident),
            pl.BlockSpec((None, None, FF, D), lambda i: (l, j, 0, 0), pipeline_mode=resident),
            pl.BlockSpec((1, D), lambda i: (0, 0)),
            pl.BlockSpec((1, D), lambda i: (0, 0)),
        ],
        out_specs=pl.BlockSpec((tm, D), lambda i: (i, 0)),
        out_shape=jax.ShapeDtypeStruct((T, D), F32),
        scratch_shapes=[pltpu.VMEM((tm, FF), BF16)],
        compiler_params=pltpu.CompilerParams(
            dimension_semantics=("parallel",), vmem_limit_bytes=VMEM_LIMIT),
        name="ffn_ln",
    )(x, wg, wu, wd, lg, lb)


def _proj_kernel(x_ref, w_ref, o_ref, *, tn):
    xb = x_ref[...].astype(BF16)
    for c in range(o_ref.shape[1] // tn):
        cols = slice(c * tn, (c + 1) * tn)
        o_ref[:, cols] = _dot(xb, w_ref[:, cols]).astype(o_ref.dtype)


def _qkv_proj(x, w_in, l, tm=512, tn=1152):
    T, D = x.shape
    return pl.pallas_call(
        functools.partial(_proj_kernel, tn=tn),
        grid=(T // tm,),
        in_specs=[
            pl.BlockSpec((tm, D), lambda i: (i, 0)),
            pl.BlockSpec((None, D, QKV_W), lambda i: (l, 0, 0), pipeline_mode=pl.Buffered(1)),
        ],
        out_specs=pl.BlockSpec((tm, QKV_W), lambda i: (i, 0)),
        out_shape=jax.ShapeDtypeStruct((T, QKV_W), BF16),
        compiler_params=pltpu.CompilerParams(
            dimension_semantics=("parallel",), vmem_limit_bytes=VMEM_LIMIT),
        name="qkv_proj",
    )(x, w_in)


def _lane_iota():
    return lax.broadcasted_iota(jnp.int32, (1, LANES), 1)


def _head_lanes(hh):
    lane = _lane_iota()
    return (lane >= hh * HEAD_DIM) & (lane < (hh + 1) * HEAD_DIM)


def _blk(j):
    return pl.ds(pl.multiple_of(j * ATT_T, ATT_T), ATT_T)


def _key_query_iotas():
    krow = lax.broadcasted_iota(jnp.int32, (ATT_T, ATT_T), 0)
    qcol = lax.broadcasted_iota(jnp.int32, (ATT_T, ATT_T), 1)
    return krow, qcol


def _build_softmax_operands(tab_ref, hb, k_ref, v_ref, kp_ref, vt_ref, kmax_ref, *, n_blocks, scale):
    lane = _lane_iota()
    pos = lax.broadcasted_iota(jnp.int32, (ATT_T, LANES), 0).astype(F32)
    bias_cols = []
    for hh in range(HEADS_PER_BLOCK):
        b = pos * tab_ref[hb * HEADS_PER_BLOCK + hh]
        lo = (1 - hh) * HEAD_DIM
        cols = jnp.zeros_like(b)
        for i in range(N_BIAS_COLS):
            piece = b.astype(BF16).astype(F32)
            cols = jnp.where(lane == lo + i, piece, cols)
            b = b - piece
        bias_cols.append(cols)
    kmax_ref[...] = jnp.zeros_like(kmax_ref)
    assert HEADS_PER_BLOCK == 2
    head_r = jnp.where(lax.broadcasted_iota(jnp.int32, (LANES, LANES), 0) < HEAD_DIM, 0, 1)
    head_c = jnp.where(lax.broadcasted_iota(jnp.int32, (LANES, LANES), 1) < HEAD_DIM, 0, 1)
    same_head = jnp.where(head_r == head_c, 1.0, 0.0).astype(BF16)

    def body(n, c):
        kf = k_ref[_blk(n), :].astype(F32) * (scale * LOG2E)
        v_t = v_ref[_blk(n), :].astype(F32).T.astype(BF16)
        for hh in range(HEADS_PER_BLOCK):
            kp_ref[hh, _blk(n), :] = jnp.where(_head_lanes(hh), kf, bias_cols[hh]).astype(BF16)
            vt_ref[hh, 0:HEAD_DIM, _blk(n)] = v_t[hh * HEAD_DIM:(hh + 1) * HEAD_DIM, :]
            vt_ref[hh, HEAD_DIM:V_ROWS, _blk(n)] = jnp.ones((DENOM_ROWS, ATT_T), BF16)
        norm2 = _dot((kf * kf).astype(BF16), same_head)
        kmax_ref[...] = jnp.maximum(kmax_ref[...], jnp.max(norm2, axis=0, keepdims=True))
        return c

    lax.fori_loop(0, n_blocks, body, 0, unroll=BUILD_UNROLL)


def _chain_qt(c, hh, seg_lo, seg_width, q_t, qt_ref):
    row = lax.broadcasted_iota(jnp.int32, (LANES, ATT_T), 0)
    lo = (1 - hh) * HEAD_DIM
    qsel = jnp.where((row >= seg_lo) & (row < seg_lo + seg_width), q_t, 0.0)
    qt_ref[c] = jnp.where((row >= lo) & (row < lo + N_BIAS_COLS), 1.0, qsel).astype(BF16)
    return qsel, jnp.sum(qsel * qsel, axis=0, keepdims=True)


def _stage_scores(c, hh, n, slot, kp_ref, qt_ref, sbuf_ref):
    sbuf_ref[c * STAGE_DEPTH + slot] = _dot(kp_ref[hh, _blk(jnp.maximum(n, 0)), :], qt_ref[c])


def _diag_blocks(chains, qi_of, qn2, slope2, kp_ref, vt_ref, kmax_ref, qt_ref, ref_ref, acc_ref, sbuf_ref):
    krow, qcol = _key_query_iotas()
    causal = krow <= qcol
    r = lax.broadcasted_iota(jnp.int32, (1, ATT_T), 1).astype(F32)
    scores = [_dot(kp_ref[hh, _blk(qi_of[c]), :], qt_ref[c]) for c, hh in enumerate(chains)]
    for slot in range(STAGE_DEPTH):
        for c, hh in enumerate(chains):
            _stage_scores(c, hh, qi_of[c] - 1 - slot, slot, kp_ref, qt_ref, sbuf_ref)
    gaps = []
    for c, hh in enumerate(chains):
        s = jnp.where(causal, scores[c], NEG)
        m = jnp.max(s, axis=0, keepdims=True)
        kmax2 = kmax_ref[:, hh * HEAD_DIM:hh * HEAD_DIM + 1]
        bound = jnp.sqrt(qn2[c] * kmax2) * BOUND_SLACK + slope2[hh] * r
        ref_ref[c] = m
        acc_ref[c] = _dot(vt_ref[hh, :, _blk(qi_of[c])], jnp.exp2(s - m).astype(BF16))
        gaps.append(jnp.max(bound - m))
    return gaps


def _first_block(gap, inv_blk, qi):
    reach = jnp.minimum((gap + SKIP_BITS) * inv_blk, 1e6).astype(jnp.int32)
    return jnp.maximum(qi - 1 - reach, 0)


def _fixed_ref_step(c, hh, n, slot, off, kp_ref, vt_ref, qt_ref, ref_ref, acc_ref, sbuf_ref):
    p = jnp.exp2(sbuf_ref[c * STAGE_DEPTH + slot] - (ref_ref[c] - off)).astype(BF16)
    _stage_scores(c, hh, n - STAGE_DEPTH, slot, kp_ref, qt_ref, sbuf_ref)
    acc_ref[c] += _dot(vt_ref[hh, :, _blk(jnp.maximum(n, 0))], p)


def _online_step(c, hh, n, off, kp_ref, vt_ref, qt_ref, m_ref, acc_ref):
    s = _dot(kp_ref[hh, _blk(n), :], qt_ref[c])
    m_old = m_ref[c]
    m_new = jnp.maximum(m_old, jnp.max(s, axis=0, keepdims=True) + off)
    p = jnp.exp2(s - (m_new - off)).astype(BF16)
    acc_ref[c] = acc_ref[c] * jnp.exp2(m_old - m_new) + _dot(vt_ref[hh, :, _blk(n)], p)
    m_ref[c] = m_new


def _softmax_sweep(qi_of, chains, gaps, depth, off_fn, kp_ref, vt_ref, qt_ref, ref_ref, acc_ref, sbuf_ref,
                   split_heads):
    refs = (kp_ref, vt_ref, qt_ref, ref_ref, acc_ref)
    assert STAGE_DEPTH == 2
    fixed_ok = functools.reduce(jnp.maximum, gaps) <= MAX_EXPONENT

    def run_fixed(start, count, heads):
        active = [(c, hh) for c, hh in enumerate(chains) if hh in heads]
        per_trip = max(STAGE_DEPTH, ITEMS_PER_TRIP // len(active))
        if len(heads) == 1:
            per_trip = min(per_trip, LONE_HEAD_BLOCKS_PER_TRIP)
        assert per_trip % STAGE_DEPTH == 0

        def sweep(start, trips, per_trip):
            def body(i, carry):
                for j in range(per_trip):
                    below = start + i * per_trip + j
                    for c, hh in active:
                        n = qi_of[c] - 1 - below
                        off = off_fn(c, hh, jnp.maximum(n, 0)) + jnp.where(n < 0, NEG, 0.0)
                        _fixed_ref_step(c, hh, n, j % STAGE_DEPTH, off, *refs, sbuf_ref)
                return carry

            lax.fori_loop(0, trips, body, 0)

        whole = count // per_trip
        sweep(start, whole, per_trip)
        if per_trip > STAGE_DEPTH:
            rest = count - whole * per_trip
            sweep(start + whole * per_trip, (rest + STAGE_DEPTH - 1) // STAGE_DEPTH, STAGE_DEPTH)
        else:
            sweep(start + whole * per_trip, count - whole * per_trip, 1)

    @pl.when(fixed_ok)
    def _fixed_reference():
        far = jnp.maximum(depth[0], depth[1])
        if split_heads:
            near = jnp.minimum(depth[0], depth[1])
            together = jnp.minimum((near + 1) // 2 * 2, far)
            run_fixed(0, together, (0, 1))
            for hh in range(HEADS_PER_BLOCK):
                run_fixed(together, jnp.maximum(depth[hh] - together, 0), (hh,))
        else:
            run_fixed(0, far, (0, 1))

    @pl.when(jnp.logical_not(fixed_ok))
    def _running_maximum():
        def body(n, carry):
            for c, hh in enumerate(chains):
                off = off_fn(c, hh, n) + jnp.where(n < qi_of[c], 0.0, NEG)
                _online_step(c, hh, n, off, *refs)
            return carry

        lax.fori_loop(0, functools.reduce(jnp.maximum, qi_of), body, 0)


def _normalized(acc):
    return acc[0:HEAD_DIM, :] * (1.0 / acc[HEAD_DIM:HEAD_DIM + 1, :])


def _row_is_head0():
    return lax.broadcasted_iota(jnp.int32, (LANES, ATT_T), 0) < HEAD_DIM


def _moba_kernel(tab_ref, q_ref, k_ref, v_ref, o_ref, kp_ref, vt_ref, kmax_ref, kmean_ref, selm_ref,
                 qt_ref, ref_ref, acc_ref, sbuf_ref, *, n_blocks, scale):
    hb = pl.program_id(1)
    nbp = kmean_ref.shape[0]
    _build_softmax_operands(tab_ref, hb, k_ref, v_ref, kp_ref, vt_ref, kmax_ref, n_blocks=n_blocks, scale=scale)
    kmean_ref[...] = jnp.zeros_like(kmean_ref)

    def block_mean(n, c):
        kmean_ref[pl.ds(n, 1), :] = jnp.mean(k_ref[_blk(n), :].astype(F32), axis=0, keepdims=True)
        return c

    lax.fori_loop(0, n_blocks, block_mean, 0, unroll=BUILD_UNROLL)

    blk_id = lax.broadcasted_iota(jnp.int32, (nbp, ATT_T), 0)
    blk_f = blk_id.astype(F32)
    chains = list(range(HEADS_PER_BLOCK))
    n_ch = len(chains)
    slope2 = [tab_ref[hb * HEADS_PER_BLOCK + hh] for hh in chains]
    assert n_blocks % TILES_PER_PASS == 0

    def tile_group(g, carry):
        tiles = [g * TILES_PER_PASS + s for s in range(TILES_PER_PASS)]
        qn2 = []
        for s, qi in enumerate(tiles):
            q_t = q_ref[_blk(qi), :].astype(F32).T
            for hh in chains:
                hm = _head_lanes(hh)
                qsel, norm2 = _chain_qt(s * n_ch + hh, hh, hh * HEAD_DIM, HEAD_DIM, q_t, qt_ref)
                qn2.append(norm2)

                gate = _dot(jnp.where(hm, kmean_ref[...], 0.0), qsel, precision=lax.Precision.HIGHEST)
                gt = jnp.where(blk_id < qi, gate, -jnp.inf)
                sel = jnp.zeros(gt.shape, dtype=jnp.bool_)
                for _ in range(MOBA_TOPK):
                    mx = jnp.max(gt, axis=0, keepdims=True)
                    pick = jnp.min(jnp.where(gt == mx, blk_f, float(nbp)), axis=0, keepdims=True)
                    hit = blk_f == pick
                    sel = sel | (hit & (mx > -jnp.inf))
                    gt = jnp.where(hit, -jnp.inf, gt)
                selm_ref[s * n_ch + hh] = jnp.where(sel, 0.0, NEG)

        qi_of = [tiles[c // n_ch] for c in range(TILES_PER_PASS * n_ch)]
        gaps = _diag_blocks(chains * TILES_PER_PASS, qi_of, qn2, slope2, kp_ref, vt_ref, kmax_ref, qt_ref,
                            ref_ref, acc_ref, sbuf_ref)
        all_chains = chains * TILES_PER_PASS
        depth = [None] * HEADS_PER_BLOCK
        for c, hh in enumerate(all_chains):
            d = qi_of[c] - _first_block(gaps[c], tab_ref[N_HEADS + hb * HEADS_PER_BLOCK + hh], qi_of[c])
            depth[hh] = d if depth[hh] is None else jnp.maximum(depth[hh], d)

        def off_fn(c, hh, n):
            return slope2[hh] * jnp.asarray((n - qi_of[c]) * ATT_T, F32) + selm_ref[c, pl.ds(n, 1), :]

        _softmax_sweep(qi_of, all_chains, gaps, depth, off_fn, kp_ref, vt_ref, qt_ref, ref_ref, acc_ref,
                       sbuf_ref, split_heads=False)
        for s, qi in enumerate(tiles):
            o = jnp.concatenate([_normalized(acc_ref[s * n_ch + hh]) for hh in chains], axis=0)
            o_ref[_blk(qi), :] = o.T.astype(o_ref.dtype)
        return carry

    lax.fori_loop(0, n_blocks // TILES_PER_PASS, tile_group, 0)


def _diff_kernel(tab_ref, lam_ref, g_ref, q_ref, k_ref, v_ref, o_ref, kp_ref, vt_ref, kmax_ref, qt_ref,
                 ref_ref, acc_ref, sbuf_ref, *, n_blocks, scale, lambda_init):
    hb = pl.program_id(1)
    _build_softmax_operands(tab_ref, hb, k_ref, v_ref, kp_ref, vt_ref, kmax_ref, n_blocks=n_blocks, scale=scale)

    lf = lam_ref[...].astype(F32)
    lam = (jnp.exp(jnp.sum(lf[0:1] * lf[1:2], axis=-1, keepdims=True))
           - jnp.exp(jnp.sum(lf[2:3] * lf[3:4], axis=-1, keepdims=True)) + lambda_init)

    chains = [hh for hh in range(HEADS_PER_BLOCK) for _ in range(2)]
    n_ch = len(chains)
    slope2 = [tab_ref[hb * HEADS_PER_BLOCK + hh] for hh in range(HEADS_PER_BLOCK)]
    assert n_blocks % TILES_PER_PASS == 0

    def tile_group(g, carry):
        tiles = [g * TILES_PER_PASS + s for s in range(TILES_PER_PASS)]
        qn2 = []
        for s, qi in enumerate(tiles):
            q_t = q_ref[_blk(qi), :].astype(F32).T
            for c, hh in enumerate(chains):
                seg_lo = hh * HEAD_DIM + (c % 2) * DIFF_QK_DIM
                qn2.append(_chain_qt(s * n_ch + c, hh, seg_lo, DIFF_QK_DIM, q_t, qt_ref)[1])
        qi_of = [tiles[c // n_ch] for c in range(TILES_PER_PASS * n_ch)]
        gaps = _diag_blocks(chains * TILES_PER_PASS, qi_of, qn2, slope2, kp_ref, vt_ref, kmax_ref, qt_ref,
                            ref_ref, acc_ref, sbuf_ref)
        all_chains = chains * TILES_PER_PASS
        depth = [None] * HEADS_PER_BLOCK
        for c, hh in enumerate(all_chains):
            d = qi_of[c] - _first_block(gaps[c], tab_ref[N_HEADS + hb * HEADS_PER_BLOCK + hh], qi_of[c])
            depth[hh] = d if depth[hh] is None else jnp.maximum(depth[hh], d)

        def off_fn(c, hh, n):
            return slope2[hh] * jnp.asarray((n - qi_of[c]) * ATT_T, F32)

        _softmax_sweep(qi_of, all_chains, gaps, depth, off_fn, kp_ref, vt_ref, qt_ref, ref_ref, acc_ref,
                       sbuf_ref, split_heads=True)
        for s, qi in enumerate(tiles):
            base = s * n_ch
            outs = []
            for hh in range(HEADS_PER_BLOCK):
                o = (_normalized(acc_ref[base + 2 * hh])
                     - lam * _normalized(acc_ref[base + 2 * hh + 1]))
                ms = jnp.mean(o * o, axis=0, keepdims=True)
                outs.append(o * lax.rsqrt(ms + SUBLN_EPS))
            o = jnp.concatenate(outs, axis=0).T * (g_ref[...] * (1.0 - lambda_init))
            o_ref[_blk(qi), :] = o.astype(o_ref.dtype)
        return carry

    lax.fori_loop(0, n_blocks // TILES_PER_PASS, tile_group, 0)


def _log_sigmoids(z):
    lneg = jnp.minimum(-z, 0.0) - jnp.log(1.0 + jnp.exp(-jnp.abs(z)))
    return lneg, lneg + z


def _suffix_sums(upper, lg):
    hi = lg.astype(BF16)
    lo = (lg - hi.astype(F32)).astype(BF16)
    return _dot(upper, hi) + _dot(upper, lo)


def _sb_pass(tiles, k_ref, vt_ref, qt_ref, carry_ref, acc_ref):
    krow, qcol = _key_query_iotas()
    before = krow < qcol
    upper = (qcol > krow).astype(BF16)
    items = [(slot * HEADS_PER_BLOCK + hh, hh, n, diag)
             for slot, blocks in tiles for n, diag in blocks for hh in range(HEADS_PER_BLOCK)]
    z = [_dot(k_ref[_blk(n), :], qt_ref[st]) for st, _, n, _ in items]
    log_sig, log_surv, tails = [], [], []
    for i, (st, hh, n, diag) in enumerate(items):
        lneg, lpos = _log_sigmoids(jnp.where(before, z[i], NEG) if diag else z[i])
        log_surv.append(lneg)
        log_sig.append(lpos)
        tails.append(_suffix_sums(upper, lneg))
    carry, acc = {}, {}
    for slot, blocks in tiles:
        for hh in range(HEADS_PER_BLOCK):
            st = slot * HEADS_PER_BLOCK + hh
            starts_fresh = blocks[0][1]
            carry[st] = None if starts_fresh else carry_ref[st]
            acc[st] = None if starts_fresh else acc_ref[st]
    for i, (st, hh, n, diag) in enumerate(items):
        w = jnp.exp(log_sig[i] + (tails[i] if diag else tails[i] + carry[st]))
        part = _dot(vt_ref[hh, :, _blk(n)], w.astype(BF16))
        acc[st] = part if acc[st] is None else acc[st] + part
        total = jnp.sum(log_surv[i], axis=0, keepdims=True)
        carry[st] = total if carry[st] is None else carry[st] + total
    worst = []
    for slot, _ in tiles:
        tops = []
        for hh in range(HEADS_PER_BLOCK):
            st = slot * HEADS_PER_BLOCK + hh
            acc_ref[st] = acc[st]
            carry_ref[st] = carry[st]
            tops.append(jnp.max(carry[st]))
        worst.append(functools.reduce(jnp.maximum, tops))
    return tuple(worst)


def _sb_kernel(q_ref, k_ref, v_ref, o_ref, vt_ref, qt_ref, carry_ref, acc_ref, *, n_blocks, scale):
    def transpose_values(n, c):
        v_t = v_ref[_blk(n), :].astype(F32).T.astype(BF16)
        for hh in range(HEADS_PER_BLOCK):
            vt_ref[hh, :, _blk(n)] = v_t[hh * HEAD_DIM:(hh + 1) * HEAD_DIM, :]
        return c

    lax.fori_loop(0, n_blocks, transpose_values, 0, unroll=BUILD_UNROLL)

    assert n_blocks % SB_TILES_PER_PASS == 0
    refs = (k_ref, vt_ref, qt_ref, carry_ref, acc_ref)
    row_h0 = _row_is_head0()

    def tile_group(g, carry):
        tiles = [g * SB_TILES_PER_PASS + slot for slot in range(SB_TILES_PER_PASS)]
        for slot, qi in enumerate(tiles):
            q_t = (q_ref[_blk(qi), :].astype(F32) * scale).T
            for hh in range(HEADS_PER_BLOCK):
                in_head = row_h0 if hh == 0 else jnp.logical_not(row_h0)
                qt_ref[slot * HEADS_PER_BLOCK + hh] = jnp.where(in_head, q_t, 0.0).astype(BF16)

        def blocks_of(qi, with_previous):
            return [(qi, True), (qi - 1, False)] if with_previous else [(qi, True)]

        worst = lax.cond(
            g > 0,
            lambda: _sb_pass([(slot, blocks_of(qi, True)) for slot, qi in enumerate(tiles)], *refs),
            lambda: _sb_pass([(slot, blocks_of(qi, slot > 0)) for slot, qi in enumerate(tiles)], *refs))

        for slot, qi in enumerate(tiles):
            def cond(c):
                j, live = c
                return (j >= 0) & live

            def body(c, slot=slot):
                j, _ = c
                return j - 1, _sb_pass([(slot, [(j, False)])], *refs)[0] > SB_EXIT

            lax.while_loop(cond, body, (qi - 2, worst[slot] > SB_EXIT))
            st = slot * HEADS_PER_BLOCK
            o = jnp.concatenate([acc_ref[st + hh] for hh in range(HEADS_PER_BLOCK)], axis=0)
            o_ref[_blk(qi), :] = o.T.astype(o_ref.dtype)
        return carry

    lax.fori_loop(0, n_blocks // SB_TILES_PER_PASS, tile_group, 0)


def _attn_specs(B, S, q_base, k_base, v_base):
    qspec = pl.BlockSpec((None, S, LANES), lambda b, h: (b, 0, q_base + h))
    kspec = pl.BlockSpec((None, S, LANES), lambda b, h: (b, 0, k_base + h))
    vspec = pl.BlockSpec((None, S, LANES), lambda b, h: (b, 0, v_base + h))
    ospec = pl.BlockSpec((None, S, LANES), lambda b, h: (b, 0, h))
    grid = (B, N_HEAD_BLOCKS)
    params = pltpu.CompilerParams(
        dimension_semantics=("parallel", "parallel"), vmem_limit_bytes=VMEM_LIMIT)
    return grid, qspec, kspec, vspec, ospec, params


_SMEM_SPEC = pl.BlockSpec(memory_space=pltpu.SMEM)


def _moba(qkv, tab):
    B, S, _ = qkv.shape
    grid, qspec, kspec, vspec, ospec, params = _attn_specs(B, S, 0, 3, 6)
    n_blocks = S // MOBA_BLOCK
    nbp = -(-n_blocks // SUBLANES) * SUBLANES
    n_slots = TILES_PER_PASS * HEADS_PER_BLOCK
    return pl.pallas_call(
        functools.partial(_moba_kernel, n_blocks=n_blocks, scale=HEAD_DIM ** -0.5),
        grid=grid,
        in_specs=[_SMEM_SPEC, qspec, kspec, vspec],
        out_specs=ospec,
        out_shape=jax.ShapeDtypeStruct((B, S, MIX_W), BF16),
        scratch_shapes=[
            pltpu.VMEM((HEADS_PER_BLOCK, S, LANES), BF16),
            pltpu.VMEM((HEADS_PER_BLOCK, V_ROWS, S), BF16),
            pltpu.VMEM((1, LANES), F32),
            pltpu.VMEM((nbp, LANES), F32),
            pltpu.VMEM((n_slots, nbp, ATT_T), F32),
            pltpu.VMEM((n_slots, LANES, ATT_T), BF16),
            pltpu.VMEM((n_slots, 1, ATT_T), F32),
            pltpu.VMEM((n_slots, V_ROWS, ATT_T), F32),
            pltpu.VMEM((n_slots * STAGE_DEPTH, ATT_T, ATT_T), F32),
        ],
        compiler_params=params,
        name="moba_attn",
    )(tab, qkv, qkv, qkv)


def _diff(qkv, tab, diff_lambda, subln_g, lambda_init):
    B, S, _ = qkv.shape
    grid, qspec, kspec, vspec, ospec, params = _attn_specs(B, S, 9, 12, 15)
    n_chains = TILES_PER_PASS * 2 * HEADS_PER_BLOCK
    return pl.pallas_call(
        functools.partial(_diff_kernel, n_blocks=S // ATT_T, scale=DIFF_QK_DIM ** -0.5,
                          lambda_init=lambda_init),
        grid=grid,
        in_specs=[
            _SMEM_SPEC,
            pl.BlockSpec(diff_lambda.shape, lambda b, h: (0, 0)),
            pl.BlockSpec((1, LANES), lambda b, h: (0, 0)),
            qspec, kspec, vspec,
        ],
        out_specs=ospec,
        out_shape=jax.ShapeDtypeStruct((B, S, MIX_W), BF16),
        scratch_shapes=[
            pltpu.VMEM((HEADS_PER_BLOCK, S, LANES), BF16),
            pltpu.VMEM((HEADS_PER_BLOCK, V_ROWS, S), BF16),
            pltpu.VMEM((1, LANES), F32),
            pltpu.VMEM((n_chains, LANES, ATT_T), BF16),
            pltpu.VMEM((n_chains, 1, ATT_T), F32),
            pltpu.VMEM((n_chains, V_ROWS, ATT_T), F32),
            pltpu.VMEM((n_chains * STAGE_DEPTH, ATT_T, ATT_T), F32),
        ],
        compiler_params=params,
        name="diff_attn",
    )(tab, diff_lambda, subln_g, qkv, qkv, qkv)


def _sb(qkv):
    B, S, _ = qkv.shape
    grid, qspec, kspec, vspec, ospec, params = _attn_specs(B, S, 18, 21, 24)
    return pl.pallas_call(
        functools.partial(_sb_kernel, n_blocks=S // ATT_T, scale=HEAD_DIM ** -0.5),
        grid=grid,
        in_specs=[qspec, kspec, vspec],
        out_specs=ospec,
        out_shape=jax.ShapeDtypeStruct((B, S, MIX_W), BF16),
        scratch_shapes=[
            pltpu.VMEM((HEADS_PER_BLOCK, HEAD_DIM, S), BF16),
            pltpu.VMEM((SB_TILES_PER_PASS * HEADS_PER_BLOCK, LANES, ATT_T), BF16),
            pltpu.VMEM((SB_TILES_PER_PASS * HEADS_PER_BLOCK, 1, ATT_T), F32),
            pltpu.VMEM((SB_TILES_PER_PASS * HEADS_PER_BLOCK, HEAD_DIM, ATT_T), F32),
        ],
        compiler_params=params,
        name="sb_attn",
    )(qkv, qkv, qkv)


def _merge_ln_kernel(x_ref, om_ref, od_ref, os_ref, wg_ref, bg_ref, wbr_ref, wo_ref, lg_ref, lb_ref,
                     o_ref, *, alpha, d_model):
    x = x_ref[...]
    xb = x.astype(BF16)
    merged = None
    for b, ob_ref in enumerate((om_ref, od_ref, os_ref)):
        glogit = _dot(xb, wg_ref[:, b * d_model:(b + 1) * d_model]) + bg_ref[b:b + 1, :]
        term = jax.nn.sigmoid(glogit) * _dot(ob_ref[...], wbr_ref[b])
        merged = term if merged is None else merged + term
    y = alpha * x + _dot(merged.astype(BF16), wo_ref[...])
    o_ref[...] = _layer_norm(y, lg_ref[...], lb_ref[...])


def _merge_ln(x, o_m, o_d, o_s, w_gate, b_gate, w_br, w_out, lg, lb, l, alpha, tm=512):
    T, D = x.shape
    return pl.pallas_call(
        functools.partial(_merge_ln_kernel, alpha=alpha, d_model=D),
        grid=(T // tm,),
        in_specs=[
            pl.BlockSpec((tm, D), lambda i: (i, 0)),
            pl.BlockSpec((tm, MIX_W), lambda i: (i, 0)),
            pl.BlockSpec((tm, MIX_W), lambda i: (i, 0)),
            pl.BlockSpec((tm, MIX_W), lambda i: (i, 0)),
            pl.BlockSpec((None, D, N_BRANCH * D), lambda i: (l, 0, 0), pipeline_mode=pl.Buffered(1)),
            pl.BlockSpec((None, N_BRANCH, D), lambda i: (l, 0, 0)),
            pl.BlockSpec((None, N_BRANCH, MIX_W, D), lambda i: (l, 0, 0, 0), pipeline_mode=pl.Buffered(1)),
            pl.BlockSpec((None, D, D), lambda i: (l, 0, 0), pipeline_mode=pl.Buffered(1)),
            pl.BlockSpec((1, D), lambda i: (0, 0)),
            pl.BlockSpec((1, D), lambda i: (0, 0)),
        ],
        out_specs=pl.BlockSpec((tm, D), lambda i: (i, 0)),
        out_shape=jax.ShapeDtypeStruct((T, D), F32),
        compiler_params=pltpu.CompilerParams(
            dimension_semantics=("parallel",), vmem_limit_bytes=VMEM_LIMIT),
        name="merge_ln",
    )(x, o_m, o_d, o_s, w_gate, b_gate, w_br, w_out, lg, lb)


def _alibi_slopes(n):
    return (2.0 ** (-8.0 * np.arange(1, n + 1, dtype=np.float32) / n)).astype(np.float32)


def _slope_table(slopes):
    s2 = slopes.astype(np.float64) * LOG2E
    return np.concatenate([s2, 1.0 / (s2 * ATT_T)]).astype(np.float32)


def kernel(x, ln_g, ln_b, ffn_w_gate, ffn_w_up, ffn_w_down, w_in, b_gate, diff_lambda, diff_subln_g,
           w_br_moba, w_br_diff, w_br_sb, w_out):
    B, S, D = x.shape
    depth = ln_g.shape[0]
    assert S % ATT_T == 0 and ATT_T == MOBA_BLOCK
    assert w_in.shape[-1] == QKV_W + N_BRANCH * D
    alpha = (2.0 * depth) ** 0.25

    wg = ffn_w_gate.astype(BF16)
    wu = ffn_w_up.astype(BF16)
    wd = ffn_w_down.astype(BF16)
    w_qkv = w_in[:, :, :QKV_W].astype(BF16)
    w_gate = w_in[:, :, QKV_W:].astype(BF16)
    w_br = jnp.stack([w_br_moba, w_br_diff, w_br_sb], axis=1).astype(BF16)
    w_out_b = w_out.astype(BF16)
    slopes = _alibi_slopes(2 * N_HEADS)
    tab_moba = jnp.asarray(_slope_table(slopes[0::2]))
    tab_diff = jnp.asarray(_slope_table(slopes[1::2]))
    subln_g = jnp.tile(diff_subln_g.astype(F32), (1, HEADS_PER_BLOCK))

    h = x.reshape(B * S, D)
    for l in range(depth):
        lg = ln_g[l][:, None, :]
        lb = ln_b[l][:, None, :]
        h = _ffn_ln(h, wg, wu, wd, lg[0], lb[0], l, 0, alpha)
        qkv = _qkv_proj(h, w_qkv, l).reshape(B, S, QKV_W)
        lambda_init = 0.8 - 0.6 * math.exp(-0.3 * l)
        o_m = _moba(qkv, tab_moba)
        o_d = _diff(qkv, tab_diff, diff_lambda[l], subln_g[l][None, :], lambda_init)
        o_s = _sb(qkv)
        T = B * S
        h = _merge_ln(h, o_m.reshape(T, MIX_W), o_d.reshape(T, MIX_W), o_s.reshape(T, MIX_W),
                      w_gate, b_gate, w_br, w_out_b, lg[1], lb[1], l, alpha)
        h = _ffn_ln(h, wg, wu, wd, lg[2], lb[2], l, 1, alpha)
    return h.reshape(B, S, D)
```

```python
import functools
import math

import numpy as np
import jax
import jax.numpy as jnp
from jax import lax
from jax.experimental import pallas as pl
from jax.experimental.pallas import tpu as pltpu

F32 = jnp.float32
BF16 = jnp.bfloat16

HEAD_DIM = 64
N_HEADS = 6
DIFF_QK_DIM = HEAD_DIM // 2
MOBA_BLOCK = 256
MOBA_TOPK = 3
N_BRANCH = 3
LN_EPS = 1e-5
SUBLN_EPS = 1e-5

LANES = 128
SUBLANES = 8
DENOM_ROWS = 16
V_ROWS = HEAD_DIM + DENOM_ROWS
HEADS_PER_BLOCK = LANES // HEAD_DIM
N_HEAD_BLOCKS = N_HEADS // HEADS_PER_BLOCK
MIX_W = N_HEADS * HEAD_DIM
QKV_W = 9 * MIX_W
ATT_T = 256
NEG = -1e30
SB_EXIT = -110.0
TILES_PER_PASS = 8
SB_TILES_PER_PASS = 4
LOG2E = math.log2(math.e)
N_BIAS_COLS = 3
BOUND_SLACK = 1.01
MAX_EXPONENT = 100.0
SKIP_BITS = 150.0
ITEMS_PER_TRIP = 32
LONE_HEAD_BLOCKS_PER_TRIP = 4
BUILD_UNROLL = 4
STAGE_DEPTH = 2
VMEM_LIMIT = 56 * 1024 * 1024


def _dot(a, b, precision=None):
    return jnp.dot(a, b, precision=precision, preferred_element_type=F32)


def _layer_norm(y, g, b):
    mu = jnp.mean(y, axis=-1, keepdims=True)
    yc = y - mu
    var = jnp.mean(yc * yc, axis=-1, keepdims=True)
    return yc * lax.rsqrt(var + LN_EPS) * g + b


def _ffn_ln_kernel(x_ref, wg_ref, wu_ref, wd_ref, lg_ref, lb_ref, o_ref, hid_ref, *, alpha, tf):
    x = x_ref[...]
    xb = x.astype(BF16)
    for c in range(hid_ref.shape[1] // tf):
        cols = slice(c * tf, (c + 1) * tf)
        gate = _dot(xb, wg_ref[:, cols])
        up = _dot(xb, wu_ref[:, cols])
        hid_ref[:, cols] = (gate * jax.nn.sigmoid(gate) * up).astype(BF16)
    y = alpha * x + 0.5 * _dot(hid_ref[...], wd_ref[...])
    o_ref[...] = _layer_norm(y, lg_ref[...], lb_ref[...])


def _ffn_ln(x, wg, wu, wd, lg, lb, l, j, alpha, tm=512, tf=256):
    T, D = x.shape
    FF = wg.shape[-1]
    resident = pl.Buffered(1)
    return pl.pallas_call(
        functools.partial(_ffn_ln_kernel, alpha=alpha, tf=tf),
        grid=(T // tm,),
        in_specs=[
            pl.BlockSpec((tm, D), lambda i: (i, 0)),
            pl.BlockSpec((None, None, D, FF), lambda i: (l, j, 0, 0), pipeline_mode=resident),
            pl.BlockSpec((None, None, D, FF), lambda i: (l, j, 0, 0), pipeline_mode=resident),
            pl.BlockSpec((None, None, FF, D), lambda i: (l, j, 0, 0), pipeline_mode=resident),
            pl.BlockSpec((1, D), lambda i: (0, 0)),
            pl.BlockSpec((1, D), lambda i: (0, 0)),
        ],
        out_specs=pl.BlockSpec((tm, D), lambda i: (i, 0)),
        out_shape=jax.ShapeDtypeStruct((T, D), F32),
        scratch_shapes=[pltpu.VMEM((tm, FF), BF16)],
        compiler_params=pltpu.CompilerParams(
            dimension_semantics=("parallel",), vmem_limit_bytes=VMEM_LIMIT),
        name="ffn_ln",
    )(x, wg, wu, wd, lg, lb)


def _proj_kernel(x_ref, w_ref, o_ref, *, tn):
    xb = x_ref[...].astype(BF16)
    for c in range(o_ref.shape[1] // tn):
        cols = slice(c * tn, (c + 1) * tn)
        o_ref[:, cols] = _dot(xb, w_ref[:, cols]).astype(o_ref.dtype)


def _qkv_proj(x, w_in, l, tm=512, tn=1152):
    T, D = x.shape
    return pl.pallas_call(
        functools.partial(_proj_kernel, tn=tn),
        grid=(T // tm,),
        in_specs=[
            pl.BlockSpec((tm, D), lambda i: (i, 0)),
            pl.BlockSpec((None, D, QKV_W), lambda i: (l, 0, 0), pipeline_mode=pl.Buffered(1)),
        ],
        out_specs=pl.BlockSpec((tm, QKV_W), lambda i: (i, 0)),
        out_shape=jax.ShapeDtypeStruct((T, QKV_W), BF16),
        compiler_params=pltpu.CompilerParams(
            dimension_semantics=("parallel",), vmem_limit_bytes=VMEM_LIMIT),
        name="qkv_proj",
    )(x, w_in)


def _lane_iota():
    return lax.broadcasted_iota(jnp.int32, (1, LANES), 1)


def _head_lanes(hh):
    lane = _lane_iota()
    return (lane >= hh * HEAD_DIM) & (lane < (hh + 1) * HEAD_DIM)


def _blk(j):
    return pl.ds(pl.multiple_of(j * ATT_T, ATT_T), ATT_T)


def _key_query_iotas():
    krow = lax.broadcasted_iota(jnp.int32, (ATT_T, ATT_T), 0)
    qcol = lax.broadcasted_iota(jnp.int32, (ATT_T, ATT_T), 1)
    return krow, qcol


def _build_softmax_operands(tab_ref, hb, k_ref, v_ref, kp_ref, vt_ref, kmax_ref, *, n_blocks, scale):
    lane = _lane_iota()
    pos = lax.broadcasted_iota(jnp.int32, (ATT_T, LANES), 0).astype(F32)
    bias_cols = []
    for hh in range(HEADS_PER_BLOCK):
        b = pos * tab_ref[hb * HEADS_PER_BLOCK + hh]
        lo = (1 - hh) * HEAD_DIM
        cols = jnp.zeros_like(b)
        for i in range(N_BIAS_COLS):
            piece = b.astype(BF16).astype(F32)
            cols = jnp.where(lane == lo + i, piece, cols)
            b = b - piece
        bias_cols.append(cols)
    kmax_ref[...] = jnp.zeros_like(kmax_ref)
    assert HEADS_PER_BLOCK == 2
    head_r = jnp.where(lax.broadcasted_iota(jnp.int32, (LANES, LANES), 0) < HEAD_DIM, 0, 1)
    head_c = jnp.where(lax.broadcasted_iota(jnp.int32, (LANES, LANES), 1) < HEAD_DIM, 0, 1)
    same_head = jnp.where(head_r == head_c, 1.0, 0.0).astype(BF16)

    def body(n, c):
        kf = k_ref[_blk(n), :].astype(F32) * (scale * LOG2E)
        v_t = v_ref[_blk(n), :].astype(F32).T.astype(BF16)
        for hh in range(HEADS_PER_BLOCK):
            kp_ref[hh, _blk(n), :] = jnp.where(_head_lanes(hh), kf, bias_cols[hh]).astype(BF16)
            vt_ref[hh, 0:HEAD_DIM, _blk(n)] = v_t[hh * HEAD_DIM:(hh + 1) * HEAD_DIM, :]
            vt_ref[hh, HEAD_DIM:V_ROWS, _blk(n)] = jnp.ones((DENOM_ROWS, ATT_T), BF16)
        norm2 = _dot((kf * kf).astype(BF16), same_head)
        kmax_ref[...] = jnp.maximum(kmax_ref[...], jnp.max(norm2, axis=0, keepdims=True))
        return c

    lax.fori_loop(0, n_blocks, body, 0, unroll=BUILD_UNROLL)


def _chain_qt(c, hh, seg_lo, seg_width, q_t, qt_ref):
    row = lax.broadcasted_iota(jnp.int32, (LANES, ATT_T), 0)
    lo = (1 - hh) * HEAD_DIM
    qsel = jnp.where((row >= seg_lo) & (row < seg_lo + seg_width), q_t, 0.0)
    qt_ref[c] = jnp.where((row >= lo) & (row < lo + N_BIAS_COLS), 1.0, qsel).astype(BF16)
    return qsel, jnp.sum(qsel * qsel, axis=0, keepdims=True)


def _stage_scores(c, hh, n, slot, kp_ref, qt_ref, sbuf_ref):
    sbuf_ref[c * STAGE_DEPTH + slot] = _dot(kp_ref[hh, _blk(jnp.maximum(n, 0)), :], qt_ref[c])


def _diag_blocks(chains, qi_of, qn2, slope2, kp_ref, vt_ref, kmax_ref, qt_ref, ref_ref, acc_ref, sbuf_ref):
    krow, qcol = _key_query_iotas()
    causal = krow <= qcol
    r = lax.broadcasted_iota(jnp.int32, (1, ATT_T), 1).astype(F32)
    scores = [_dot(kp_ref[hh, _blk(qi_of[c]), :], qt_ref[c]) for c, hh in enumerate(chains)]
    for slot in range(STAGE_DEPTH):
        for c, hh in enumerate(chains):
            _stage_scores(c, hh, qi_of[c] - 1 - slot, slot, kp_ref, qt_ref, sbuf_ref)
    gaps = []
    for c, hh in enumerate(chains):
        s = jnp.where(causal, scores[c], NEG)
        m = jnp.max(s, axis=0, keepdims=True)
        kmax2 = kmax_ref[:, hh * HEAD_DIM:hh * HEAD_DIM + 1]
        bound = jnp.sqrt(qn2[c] * kmax2) * BOUND_SLACK + slope2[hh] * r
        ref_ref[c] = m
        acc_ref[c] = _dot(vt_ref[hh, :, _blk(qi_of[c])], jnp.exp2(s - m).astype(BF16))
        gaps.append(jnp.max(bound - m))
    return gaps


def _first_block(gap, inv_blk, qi):
    reach = jnp.minimum((gap + SKIP_BITS) * inv_blk, 1e6).astype(jnp.int32)
    return jnp.maximum(qi - 1 - reach, 0)


def _fixed_ref_step(c, hh, n, slot, off, kp_ref, vt_ref, qt_ref, ref_ref, acc_ref, sbuf_ref):
    p = jnp.exp2(sbuf_ref[c * STAGE_DEPTH + slot] - (ref_ref[c] - off)).astype(BF16)
    _stage_scores(c, hh, n - STAGE_DEPTH, slot, kp_ref, qt_ref, sbuf_ref)
    acc_ref[c] += _dot(vt_ref[hh, :, _blk(jnp.maximum(n, 0))], p)


def _online_step(c, hh, n, off, kp_ref, vt_ref, qt_ref, m_ref, acc_ref):
    s = _dot(kp_ref[hh, _blk(n), :], qt_ref[c])
    m_old = m_ref[c]
    m_new = jnp.maximum(m_old, jnp.max(s, axis=0, keepdims=True) + off)
    p = jnp.exp2(s - (m_new - off)).astype(BF16)
    acc_ref[c] = acc_ref[c] * jnp.exp2(m_old - m_new) + _dot(vt_ref[hh, :, _blk(n)], p)
    m_ref[c] = m_new


def _softmax_sweep(qi_of, chains, gaps, depth, off_fn, kp_ref, vt_ref, qt_ref, ref_ref, acc_ref, sbuf_ref,
                   split_heads):
    refs = (kp_ref, vt_ref, qt_ref, ref_ref, acc_ref)
    assert STAGE_DEPTH == 2
    fixed_ok = functools.reduce(jnp.maximum, gaps) <= MAX_EXPONENT

    chains_per_tile = len(chains) // TILES_PER_PASS

    def run_fixed(start, count, heads):
        active = [(c, hh) for c, hh in enumerate(chains) if hh in heads]
        per_trip = max(STAGE_DEPTH, ITEMS_PER_TRIP // len(active))
        if len(heads) == 1:
            per_trip = min(per_trip, LONE_HEAD_BLOCKS_PER_TRIP)
        assert per_trip % STAGE_DEPTH == 0

        def step(below, slot, who):
            for c, hh in who:
                n = qi_of[c] - 1 - below
                off = off_fn(c, hh, jnp.maximum(n, 0)) + jnp.where(n < 0, NEG, 0.0)
                _fixed_ref_step(c, hh, n, slot, off, *refs, sbuf_ref)

        def sweep(start, trips, per_trip):
            def body(i, carry):
                for j in range(per_trip):
                    step(start + i * per_trip + j, j % STAGE_DEPTH, active)
                return carry

            lax.fori_loop(0, trips, body, 0)

        common = jnp.clip(qi_of[0] - start, 0, count) // STAGE_DEPTH * STAGE_DEPTH
        whole = common // per_trip
        sweep(start, whole, per_trip)
        if per_trip > STAGE_DEPTH:
            sweep(start + whole * per_trip, (common - whole * per_trip) // STAGE_DEPTH, STAGE_DEPTH)
        for k in range(TILES_PER_PASS):
            who = [(c, hh) for c, hh in active if c // chains_per_tile >= k]

            @pl.when(k < count - common)
            def _one_depth(k=k, who=who):
                step(start + common + k, k % STAGE_DEPTH, who)

    @pl.when(fixed_ok)
    def _fixed_reference():
        far = jnp.maximum(depth[0], depth[1])
        if split_heads:
            near = jnp.minimum(depth[0], depth[1])
            together = jnp.minimum((near + 1) // 2 * 2, far)
            run_fixed(0, together, (0, 1))
            for hh in range(HEADS_PER_BLOCK):
                run_fixed(together, jnp.maximum(depth[hh] - together, 0), (hh,))
        else:
            run_fixed(0, far, (0, 1))

    @pl.when(jnp.logical_not(fixed_ok))
    def _running_maximum():
        def body(n, carry):
            for c, hh in enumerate(chains):
                off = off_fn(c, hh, n) + jnp.where(n < qi_of[c], 0.0, NEG)
                _online_step(c, hh, n, off, *refs)
            return carry

        lax.fori_loop(0, functools.reduce(jnp.maximum, qi_of), body, 0)


def _normalized(acc):
    return acc[0:HEAD_DIM, :] * (1.0 / acc[HEAD_DIM:HEAD_DIM + 1, :])


def _row_is_head0():
    return lax.broadcasted_iota(jnp.int32, (LANES, ATT_T), 0) < HEAD_DIM


def _moba_kernel(tab_ref, q_ref, k_ref, v_ref, o_ref, kp_ref, vt_ref, kmax_ref, kmean_ref, selm_ref,
                 qt_ref, ref_ref, acc_ref, sbuf_ref, *, n_blocks, scale):
    hb = pl.program_id(1)
    nbp = kmean_ref.shape[0]
    _build_softmax_operands(tab_ref, hb, k_ref, v_ref, kp_ref, vt_ref, kmax_ref, n_blocks=n_blocks, scale=scale)
    kmean_ref[...] = jnp.zeros_like(kmean_ref)

    def block_mean(n, c):
        kmean_ref[pl.ds(n, 1), :] = jnp.mean(k_ref[_blk(n), :].astype(F32), axis=0, keepdims=True)
        return c

    lax.fori_loop(0, n_blocks, block_mean, 0, unroll=BUILD_UNROLL)

    blk_id = lax.broadcasted_iota(jnp.int32, (nbp, ATT_T), 0)
    blk_f = blk_id.astype(F32)
    chains = list(range(HEADS_PER_BLOCK))
    n_ch = len(chains)
    slope2 = [tab_ref[hb * HEADS_PER_BLOCK + hh] for hh in chains]
    assert n_blocks % TILES_PER_PASS == 0

    def tile_group(g, carry):
        tiles = [g * TILES_PER_PASS + s for s in range(TILES_PER_PASS)]
        qn2 = []
        for s, qi in enumerate(tiles):
            q_t = q_ref[_blk(qi), :].astype(F32).T
            for hh in chains:
                hm = _head_lanes(hh)
                qsel, norm2 = _chain_qt(s * n_ch + hh, hh, hh * HEAD_DIM, HEAD_DIM, q_t, qt_ref)
                qn2.append(norm2)

                gate = _dot(jnp.where(hm, kmean_ref[...], 0.0), qsel, precision=lax.Precision.HIGHEST)
                gt = jnp.where(blk_id < qi, gate, -jnp.inf)
                sel = jnp.zeros(gt.shape, dtype=jnp.bool_)
                for _ in range(MOBA_TOPK):
                    mx = jnp.max(gt, axis=0, keepdims=True)
                    pick = jnp.min(jnp.where(gt == mx, blk_f, float(nbp)), axis=0, keepdims=True)
                    hit = blk_f == pick
                    sel = sel | (hit & (mx > -jnp.inf))
                    gt = jnp.where(hit, -jnp.inf, gt)
                selm_ref[s * n_ch + hh] = jnp.where(sel, 0.0, NEG)

        qi_of = [tiles[c // n_ch] for c in range(TILES_PER_PASS * n_ch)]
        gaps = _diag_blocks(chains * TILES_PER_PASS, qi_of, qn2, slope2, kp_ref, vt_ref, kmax_ref, qt_ref,
                            ref_ref, acc_ref, sbuf_ref)
        all_chains = chains * TILES_PER_PASS
        depth = [None] * HEADS_PER_BLOCK
        for c, hh in enumerate(all_chains):
            d = qi_of[c] - _first_block(gaps[c], tab_ref[N_HEADS + hb * HEADS_PER_BLOCK + hh], qi_of[c])
            depth[hh] = d if depth[hh] is None else jnp.maximum(depth[hh], d)

        def off_fn(c, hh, n):
            return slope2[hh] * jnp.asarray((n - qi_of[c]) * ATT_T, F32) + selm_ref[c, pl.ds(n, 1), :]

        _softmax_sweep(qi_of, all_chains, gaps, depth, off_fn, kp_ref, vt_ref, qt_ref, ref_ref, acc_ref,
                       sbuf_ref, split_heads=False)
        for s, qi in enumerate(tiles):
            o = jnp.concatenate([_normalized(acc_ref[s * n_ch + hh]) for hh in chains], axis=0)
            o_ref[_blk(qi), :] = o.T.astype(o_ref.dtype)
        return carry

    lax.fori_loop(0, n_blocks // TILES_PER_PASS, tile_group, 0)


def _diff_kernel(tab_ref, lam_ref, g_ref, q_ref, k_ref, v_ref, o_ref, kp_ref, vt_ref, kmax_ref, qt_ref,
                 ref_ref, acc_ref, sbuf_ref, *, n_blocks, scale, lambda_init):
    hb = pl.program_id(1)
    _build_softmax_operands(tab_ref, hb, k_ref, v_ref, kp_ref, vt_ref, kmax_ref, n_blocks=n_blocks, scale=scale)

    lf = lam_ref[...].astype(F32)
    lam = (jnp.exp(jnp.sum(lf[0:1] * lf[1:2], axis=-1, keepdims=True))
           - jnp.exp(jnp.sum(lf[2:3] * lf[3:4], axis=-1, keepdims=True)) + lambda_init)

    chains = [hh for hh in range(HEADS_PER_BLOCK) for _ in range(2)]
    n_ch = len(chains)
    slope2 = [tab_ref[hb * HEADS_PER_BLOCK + hh] for hh in range(HEADS_PER_BLOCK)]
    assert n_blocks % TILES_PER_PASS == 0

    def tile_group(g, carry):
        tiles = [g * TILES_PER_PASS + s for s in range(TILES_PER_PASS)]
        qn2 = []
        for s, qi in enumerate(tiles):
            q_t = q_ref[_blk(qi), :].astype(F32).T
            for c, hh in enumerate(chains):
                seg_lo = hh * HEAD_DIM + (c % 2) * DIFF_QK_DIM
                qn2.append(_chain_qt(s * n_ch + c, hh, seg_lo, DIFF_QK_DIM, q_t, qt_ref)[1])
        qi_of = [tiles[c // n_ch] for c in range(TILES_PER_PASS * n_ch)]
        gaps = _diag_blocks(chains * TILES_PER_PASS, qi_of, qn2, slope2, kp_ref, vt_ref, kmax_ref, qt_ref,
                            ref_ref, acc_ref, sbuf_ref)
        all_chains = chains * TILES_PER_PASS
        depth = [None] * HEADS_PER_BLOCK
        for c, hh in enumerate(all_chains):
            d = qi_of[c] - _first_block(gaps[c], tab_ref[N_HEADS + hb * HEADS_PER_BLOCK + hh], qi_of[c])
            depth[hh] = d if depth[hh] is None else jnp.maximum(depth[hh], d)

        def off_fn(c, hh, n):
            return slope2[hh] * jnp.asarray((n - qi_of[c]) * ATT_T, F32)

        _softmax_sweep(qi_of, all_chains, gaps, depth, off_fn, kp_ref, vt_ref, qt_ref, ref_ref, acc_ref,
                       sbuf_ref, split_heads=True)
        for s, qi in enumerate(tiles):
            base = s * n_ch
            outs = []
            for hh in range(HEADS_PER_BLOCK):
                o = (_normalized(acc_ref[base + 2 * hh])
                     - lam * _normalized(acc_ref[base + 2 * hh + 1]))
                ms = jnp.mean(o * o, axis=0, keepdims=True)
                outs.append(o * lax.rsqrt(ms + SUBLN_EPS))
            o = jnp.concatenate(outs, axis=0).T * (g_ref[...] * (1.0 - lambda_init))
            o_ref[_blk(qi), :] = o.astype(o_ref.dtype)
        return carry

    lax.fori_loop(0, n_blocks // TILES_PER_PASS, tile_group, 0)


def _log_sigmoids(z):
    lneg = jnp.minimum(-z, 0.0) - jnp.log(1.0 + jnp.exp(-jnp.abs(z)))
    return lneg, lneg + z


def _suffix_sums(upper, lg):
    hi = lg.astype(BF16)
    lo = (lg - hi.astype(F32)).astype(BF16)
    return _dot(upper, hi) + _dot(upper, lo)


def _sb_pass(tiles, k_ref, vt_ref, qt_ref, carry_ref, acc_ref):
    krow, qcol = _key_query_iotas()
    before = krow < qcol
    upper = (qcol > krow).astype(BF16)
    items = [(slot * HEADS_PER_BLOCK + hh, hh, n, diag)
             for slot, blocks in tiles for n, diag in blocks for hh in range(HEADS_PER_BLOCK)]
    z = [_dot(k_ref[_blk(n), :], qt_ref[st]) for st, _, n, _ in items]
    log_sig, log_surv, tails = [], [], []
    for i, (st, hh, n, diag) in enumerate(items):
        lneg, lpos = _log_sigmoids(jnp.where(before, z[i], NEG) if diag else z[i])
        log_surv.append(lneg)
        log_sig.append(lpos)
        tails.append(_suffix_sums(upper, lneg))
    carry, acc = {}, {}
    for slot, blocks in tiles:
        for hh in range(HEADS_PER_BLOCK):
            st = slot * HEADS_PER_BLOCK + hh
            starts_fresh = blocks[0][1]
            carry[st] = None if starts_fresh else carry_ref[st]
            acc[st] = None if starts_fresh else acc_ref[st]
    for i, (st, hh, n, diag) in enumerate(items):
        w = jnp.exp(log_sig[i] + (tails[i] if diag else tails[i] + carry[st]))
        part = _dot(vt_ref[hh, :, _blk(n)], w.astype(BF16))
        acc[st] = part if acc[st] is None else acc[st] + part
        total = jnp.sum(log_surv[i], axis=0, keepdims=True)
        carry[st] = total if carry[st] is None else carry[st] + total
    worst = []
    for slot, _ in tiles:
        tops = []
        for hh in range(HEADS_PER_BLOCK):
            st = slot * HEADS_PER_BLOCK + hh
            acc_ref[st] = acc[st]
            carry_ref[st] = carry[st]
            tops.append(jnp.max(carry[st]))
        worst.append(functools.reduce(jnp.maximum, tops))
    return tuple(worst)


def _sb_kernel(q_ref, k_ref, v_ref, o_ref, vt_ref, qt_ref, carry_ref, acc_ref, *, n_blocks, scale):
    def transpose_values(n, c):
        v_t = v_ref[_blk(n), :].astype(F32).T.astype(BF16)
        for hh in range(HEADS_PER_BLOCK):
            vt_ref[hh, :, _blk(n)] = v_t[hh * HEAD_DIM:(hh + 1) * HEAD_DIM, :]
        return c

    lax.fori_loop(0, n_blocks, transpose_values, 0, unroll=BUILD_UNROLL)

    assert n_blocks % SB_TILES_PER_PASS == 0
    refs = (k_ref, vt_ref, qt_ref, carry_ref, acc_ref)
    row_h0 = _row_is_head0()

    def tile_group(g, carry):
        tiles = [g * SB_TILES_PER_PASS + slot for slot in range(SB_TILES_PER_PASS)]
        for slot, qi in enumerate(tiles):
            q_t = (q_ref[_blk(qi), :].astype(F32) * scale).T
            for hh in range(HEADS_PER_BLOCK):
                in_head = row_h0 if hh == 0 else jnp.logical_not(row_h0)
                qt_ref[slot * HEADS_PER_BLOCK + hh] = jnp.where(in_head, q_t, 0.0).astype(BF16)

        def blocks_of(qi, with_previous):
            return [(qi, True), (qi - 1, False)] if with_previous else [(qi, True)]

        worst = lax.cond(
            g > 0,
            lambda: _sb_pass([(slot, blocks_of(qi, True)) for slot, qi in enumerate(tiles)], *refs),
            lambda: _sb_pass([(slot, blocks_of(qi, slot > 0)) for slot, qi in enumerate(tiles)], *refs))

        for slot, qi in enumerate(tiles):
            def cond(c):
                j, live = c
                return (j >= 0) & live

            def body(c, slot=slot):
                j, _ = c
                return j - 1, _sb_pass([(slot, [(j, False)])], *refs)[0] > SB_EXIT

            lax.while_loop(cond, body, (qi - 2, worst[slot] > SB_EXIT))
            st = slot * HEADS_PER_BLOCK
            o = jnp.concatenate([acc_ref[st + hh] for hh in range(HEADS_PER_BLOCK)], axis=0)
            o_ref[_blk(qi), :] = o.T.astype(o_ref.dtype)
        return carry

    lax.fori_loop(0, n_blocks // SB_TILES_PER_PASS, tile_group, 0)


def _attn_specs(B, S, q_base, k_base, v_base):
    qspec = pl.BlockSpec((None, S, LANES), lambda b, h: (b, 0, q_base + h))
    kspec = pl.BlockSpec((None, S, LANES), lambda b, h: (b, 0, k_base + h))
    vspec = pl.BlockSpec((None, S, LANES), lambda b, h: (b, 0, v_base + h))
    ospec = pl.BlockSpec((None, S, LANES), lambda b, h: (b, 0, h))
    grid = (B, N_HEAD_BLOCKS)
    params = pltpu.CompilerParams(
        dimension_semantics=("parallel", "parallel"), vmem_limit_bytes=VMEM_LIMIT)
    return grid, qspec, kspec, vspec, ospec, params


_SMEM_SPEC = pl.BlockSpec(memory_space=pltpu.SMEM)


def _moba(qkv, tab):
    B, S, _ = qkv.shape
    grid, qspec, kspec, vspec, ospec, params = _attn_specs(B, S, 0, 3, 6)
    n_blocks = S // MOBA_BLOCK
    nbp = -(-n_blocks // SUBLANES) * SUBLANES
    n_slots = TILES_PER_PASS * HEADS_PER_BLOCK
    return pl.pallas_call(
        functools.partial(_moba_kernel, n_blocks=n_blocks, scale=HEAD_DIM ** -0.5),
        grid=grid,
        in_specs=[_SMEM_SPEC, qspec, kspec, vspec],
        out_specs=ospec,
        out_shape=jax.ShapeDtypeStruct((B, S, MIX_W), BF16),
        scratch_shapes=[
            pltpu.VMEM((HEADS_PER_BLOCK, S, LANES), BF16),
            pltpu.VMEM((HEADS_PER_BLOCK, V_ROWS, S), BF16),
            pltpu.VMEM((1, LANES), F32),
            pltpu.VMEM((nbp, LANES), F32),
            pltpu.VMEM((n_slots, nbp, ATT_T), F32),
            pltpu.VMEM((n_slots, LANES, ATT_T), BF16),
            pltpu.VMEM((n_slots, 1, ATT_T), F32),
            pltpu.VMEM((n_slots, V_ROWS, ATT_T), F32),
            pltpu.VMEM((n_slots * STAGE_DEPTH, ATT_T, ATT_T), F32),
        ],
        compiler_params=params,
        name="moba_attn",
    )(tab, qkv, qkv, qkv)


def _diff(qkv, tab, diff_lambda, subln_g, lambda_init):
    B, S, _ = qkv.shape
    grid, qspec, kspec, vspec, ospec, params = _attn_specs(B, S, 9, 12, 15)
    n_chains = TILES_PER_PASS * 2 * HEADS_PER_BLOCK
    return pl.pallas_call(
        functools.partial(_diff_kernel, n_blocks=S // ATT_T, scale=DIFF_QK_DIM ** -0.5,
                          lambda_init=lambda_init),
        grid=grid,
        in_specs=[
            _SMEM_SPEC,
            pl.BlockSpec(diff_lambda.shape, lambda b, h: (0, 0)),
            pl.BlockSpec((1, LANES), lambda b, h: (0, 0)),
            qspec, kspec, vspec,
        ],
        out_specs=ospec,
        out_shape=jax.ShapeDtypeStruct((B, S, MIX_W), BF16),
        scratch_shapes=[
            pltpu.VMEM((HEADS_PER_BLOCK, S, LANES), BF16),
            pltpu.VMEM((HEADS_PER_BLOCK, V_ROWS, S), BF16),
            pltpu.VMEM((1, LANES), F32),
            pltpu.VMEM((n_chains, LANES, ATT_T), BF16),
            pltpu.VMEM((n_chains, 1, ATT_T), F32),
            pltpu.VMEM((n_chains, V_ROWS, ATT_T), F32),
            pltpu.VMEM((n_chains * STAGE_DEPTH, ATT_T, ATT_T), F32),
        ],
        compiler_params=params,
        name="diff_attn",
    )(tab, diff_lambda, subln_g, qkv, qkv, qkv)


def _sb(qkv):
    B, S, _ = qkv.shape
    grid, qspec, kspec, vspec, ospec, params = _attn_specs(B, S, 18, 21, 24)
    return pl.pallas_call(
        functools.partial(_sb_kernel, n_blocks=S // ATT_T, scale=HEAD_DIM ** -0.5),
        grid=grid,
        in_specs=[qspec, kspec, vspec],
        out_specs=ospec,
        out_shape=jax.ShapeDtypeStruct((B, S, MIX_W), BF16),
        scratch_shapes=[
            pltpu.VMEM((HEADS_PER_BLOCK, HEAD_DIM, S), BF16),
            pltpu.VMEM((SB_TILES_PER_PASS * HEADS_PER_BLOCK, LANES, ATT_T), BF16),
            pltpu.VMEM((SB_TILES_PER_PASS * HEADS_PER_BLOCK, 1, ATT_T), F32),
            pltpu.VMEM((SB_TILES_PER_PASS * HEADS_PER_BLOCK, HEAD_DIM, ATT_T), F32),
        ],
        compiler_params=params,
        name="sb_attn",
    )(qkv, qkv, qkv)


def _merge_ln_kernel(x_ref, om_ref, od_ref, os_ref, wg_ref, bg_ref, wbr_ref, wo_ref, lg_ref, lb_ref,
                     o_ref, *, alpha, d_model):
    x = x_ref[...]
    xb = x.astype(BF16)
    merged = None
    for b, ob_ref in enumerate((om_ref, od_ref, os_ref)):
        glogit = _dot(xb, wg_ref[:, b * d_model:(b + 1) * d_model]) + bg_ref[b:b + 1, :]
        term = jax.nn.sigmoid(glogit) * _dot(ob_ref[...], wbr_ref[b])
        merged = term if merged is None else merged + term
    y = alpha * x + _dot(merged.astype(BF16), wo_ref[...])
    o_ref[...] = _layer_norm(y, lg_ref[...], lb_ref[...])


def _merge_ln(x, o_m, o_d, o_s, w_gate, b_gate, w_br, w_out, lg, lb, l, alpha, tm=512):
    T, D = x.shape
    return pl.pallas_call(
        functools.partial(_merge_ln_kernel, alpha=alpha, d_model=D),
        grid=(T // tm,),
        in_specs=[
            pl.BlockSpec((tm, D), lambda i: (i, 0)),
            pl.BlockSpec((tm, MIX_W), lambda i: (i, 0)),
            pl.BlockSpec((tm, MIX_W), lambda i: (i, 0)),
            pl.BlockSpec((tm, MIX_W), lambda i: (i, 0)),
            pl.BlockSpec((None, D, N_BRANCH * D), lambda i: (l, 0, 0), pipeline_mode=pl.Buffered(1)),
            pl.BlockSpec((None, N_BRANCH, D), lambda i: (l, 0, 0)),
            pl.BlockSpec((None, N_BRANCH, MIX_W, D), lambda i: (l, 0, 0, 0), pipeline_mode=pl.Buffered(1)),
            pl.BlockSpec((None, D, D), lambda i: (l, 0, 0), pipeline_mode=pl.Buffered(1)),
            pl.BlockSpec((1, D), lambda i: (0, 0)),
            pl.BlockSpec((1, D), lambda i: (0, 0)),
        ],
        out_specs=pl.BlockSpec((tm, D), lambda i: (i, 0)),
        out_shape=jax.ShapeDtypeStruct((T, D), F32),
        compiler_params=pltpu.CompilerParams(
            dimension_semantics=("parallel",), vmem_limit_bytes=VMEM_LIMIT),
        name="merge_ln",
    )(x, o_m, o_d, o_s, w_gate, b_gate, w_br, w_out, lg, lb)


def _alibi_slopes(n):
    return (2.0 ** (-8.0 * np.arange(1, n + 1, dtype=np.float32) / n)).astype(np.float32)


def _slope_table(slopes):
    s2 = slopes.astype(np.float64) * LOG2E
    return np.concatenate([s2, 1.0 / (s2 * ATT_T)]).astype(np.float32)


def kernel(x, ln_g, ln_b, ffn_w_gate, ffn_w_up, ffn_w_down, w_in, b_gate, diff_lambda, diff_subln_g,
           w_br_moba, w_br_diff, w_br_sb, w_out):
    B, S, D = x.shape
    depth = ln_g.shape[0]
    assert S % ATT_T == 0 and ATT_T == MOBA_BLOCK
    assert w_in.shape[-1] == QKV_W + N_BRANCH * D
    alpha = (2.0 * depth) ** 0.25

    wg = ffn_w_gate.astype(BF16)
    wu = ffn_w_up.astype(BF16)
    wd = ffn_w_down.astype(BF16)
    w_qkv = w_in[:, :, :QKV_W].astype(BF16)
    w_gate = w_in[:, :, QKV_W:].astype(BF16)
    w_br = jnp.stack([w_br_moba, w_br_diff, w_br_sb], axis=1).astype(BF16)
    w_out_b = w_out.astype(BF16)
    slopes = _alibi_slopes(2 * N_HEADS)
    tab_moba = jnp.asarray(_slope_table(slopes[0::2]))
    tab_diff = jnp.asarray(_slope_table(slopes[1::2]))
    subln_g = jnp.tile(diff_subln_g.astype(F32), (1, HEADS_PER_BLOCK))

    h = x.reshape(B * S, D)
    for l in range(depth):
        lg = ln_g[l][:, None, :]
        lb = ln_b[l][:, None, :]
        h = _ffn_ln(h, wg, wu, wd, lg[0], lb[0], l, 0, alpha)
        qkv = _qkv_proj(h, w_qkv, l).reshape(B, S, QKV_W)
        lambda_init = 0.8 - 0.6 * math.exp(-0.3 * l)
        o_m = _moba(qkv, tab_moba)
        o_d = _diff(qkv, tab_diff, diff_lambda[l], subln_g[l][None, :], lambda_init)
        o_s = _sb(qkv)
        T = B * S
        h = _merge_ln(h, o_m.reshape(T, MIX_W), o_d.reshape(T, MIX_W), o_s.reshape(T, MIX_W),
                      w_gate, b_gate, w_br, w_out_b, lg[1], lb[1], l, alpha)
        h = _ffn_ln(h, wg, wu, wd, lg[2], lb[2], l, 1, alpha)
    return h.reshape(B, S, D)
```

```python
import functools
import math

import numpy as np
import jax
import jax.numpy as jnp
from jax import lax
from jax.experimental import pallas as pl
from jax.experimental.pallas import tpu as pltpu

F32 = jnp.float32
BF16 = jnp.bfloat16

HEAD_DIM = 64
N_HEADS = 6
DIFF_QK_DIM = HEAD_DIM // 2
MOBA_BLOCK = 256
MOBA_TOPK = 3
N_BRANCH = 3
LN_EPS = 1e-5
SUBLN_EPS = 1e-5

LANES = 128
SUBLANES = 8
DENOM_ROWS = 16
V_ROWS = HEAD_DIM + DENOM_ROWS
HEADS_PER_BLOCK = LANES // HEAD_DIM
N_HEAD_BLOCKS = N_HEADS // HEADS_PER_BLOCK
MIX_W = N_HEADS * HEAD_DIM
QKV_W = 9 * MIX_W
ATT_T = 256
NEG = -1e30
SB_EXIT = -110.0
TILES_PER_PASS = 8
SB_TILES_PER_PASS = 4
LOG2E = math.log2(math.e)
N_BIAS_COLS = 3
BOUND_SLACK = 1.01
MAX_EXPONENT = 100.0
SKIP_BITS = 150.0
ITEMS_PER_TRIP = 64
LONE_HEAD_BLOCKS_PER_TRIP = 4
BUILD_UNROLL = 4
STAGE_DEPTH = 2
VMEM_LIMIT = 56 * 1024 * 1024


def _dot(a, b, precision=None):
    return jnp.dot(a, b, precision=precision, preferred_element_type=F32)


def _layer_norm(y, g, b):
    mu = jnp.mean(y, axis=-1, keepdims=True)
    yc = y - mu
    var = jnp.mean(yc * yc, axis=-1, keepdims=True)
    return yc * lax.rsqrt(var + LN_EPS) * g + b


def _ffn_ln_kernel(x_ref, wg_ref, wu_ref, wd_ref, lg_ref, lb_ref, o_ref, hid_ref, *, alpha, tf):
    x = x_ref[...]
    xb = x.astype(BF16)
    for c in range(hid_ref.shape[1] // tf):
        cols = slice(c * tf, (c + 1) * tf)
        gate = _dot(xb, wg_ref[:, cols])
        up = _dot(xb, wu_ref[:, cols])
        hid_ref[:, cols] = (gate * jax.nn.sigmoid(gate) * up).astype(BF16)
    y = alpha * x + 0.5 * _dot(hid_ref[...], wd_ref[...])
    o_ref[...] = _layer_norm(y, lg_ref[...], lb_ref[...])


def _ffn_ln(x, wg, wu, wd, lg, lb, l, j, alpha, tm=512, tf=256):
    T, D = x.shape
    FF = wg.shape[-1]
    resident = pl.Buffered(1)
    return pl.pallas_call(
        functools.partial(_ffn_ln_kernel, alpha=alpha, tf=tf),
        grid=(T // tm,),
        in_specs=[
            pl.BlockSpec((tm, D), lambda i: (i, 0)),
            pl.BlockSpec((None, None, D, FF), lambda i: (l, j, 0, 0), pipeline_mode=resident),
            pl.BlockSpec((None, None, D, FF), lambda i: (l, j, 0, 0), pipeline_mode=resident),
            pl.BlockSpec((None, None, FF, D), lambda i: (l, j, 0, 0), pipeline_mode=resident),
            pl.BlockSpec((1, D), lambda i: (0, 0)),
            pl.BlockSpec((1, D), lambda i: (0, 0)),
        ],
        out_specs=pl.BlockSpec((tm, D), lambda i: (i, 0)),
        out_shape=jax.ShapeDtypeStruct((T, D), F32),
        scratch_shapes=[pltpu.VMEM((tm, FF), BF16)],
        compiler_params=pltpu.CompilerParams(
            dimension_semantics=("parallel",), vmem_limit_bytes=VMEM_LIMIT),
        name="ffn_ln",
    )(x, wg, wu, wd, lg, lb)


def _proj_kernel(x_ref, w_ref, o_ref, *, tn):
    xb = x_ref[...].astype(BF16)
    for c in range(o_ref.shape[1] // tn):
        cols = slice(c * tn, (c + 1) * tn)
        o_ref[:, cols] = _dot(xb, w_ref[:, cols]).astype(o_ref.dtype)


def _qkv_proj(x, w_in, l, tm=512, tn=1152):
    T, D = x.shape
    return pl.pallas_call(
        functools.partial(_proj_kernel, tn=tn),
        grid=(T // tm,),
        in_specs=[
            pl.BlockSpec((tm, D), lambda i: (i, 0)),
            pl.BlockSpec((None, D, QKV_W), lambda i: (l, 0, 0), pipeline_mode=pl.Buffered(1)),
        ],
        out_specs=pl.BlockSpec((tm, QKV_W), lambda i: (i, 0)),
        out_shape=jax.ShapeDtypeStruct((T, QKV_W), BF16),
        compiler_params=pltpu.CompilerParams(
            dimension_semantics=("parallel",), vmem_limit_bytes=VMEM_LIMIT),
        name="qkv_proj",
    )(x, w_in)


def _lane_iota():
    return lax.broadcasted_iota(jnp.int32, (1, LANES), 1)


def _head_lanes(hh):
    lane = _lane_iota()
    return (lane >= hh * HEAD_DIM) & (lane < (hh + 1) * HEAD_DIM)


def _blk(j):
    return pl.ds(pl.multiple_of(j * ATT_T, ATT_T), ATT_T)


def _key_query_iotas():
    krow = lax.broadcasted_iota(jnp.int32, (ATT_T, ATT_T), 0)
    qcol = lax.broadcasted_iota(jnp.int32, (ATT_T, ATT_T), 1)
    return krow, qcol


def _build_softmax_operands(tab_ref, hb, k_ref, v_ref, kp_ref, vt_ref, kmax_ref, *, n_blocks, scale):
    lane = _lane_iota()
    pos = lax.broadcasted_iota(jnp.int32, (ATT_T, LANES), 0).astype(F32)
    bias_cols = []
    for hh in range(HEADS_PER_BLOCK):
        b = pos * tab_ref[hb * HEADS_PER_BLOCK + hh]
        lo = (1 - hh) * HEAD_DIM
        cols = jnp.zeros_like(b)
        for i in range(N_BIAS_COLS):
            piece = b.astype(BF16).astype(F32)
            cols = jnp.where(lane == lo + i, piece, cols)
            b = b - piece
        bias_cols.append(cols)
    kmax_ref[...] = jnp.zeros_like(kmax_ref)
    assert HEADS_PER_BLOCK == 2
    head_r = jnp.where(lax.broadcasted_iota(jnp.int32, (LANES, LANES), 0) < HEAD_DIM, 0, 1)
    head_c = jnp.where(lax.broadcasted_iota(jnp.int32, (LANES, LANES), 1) < HEAD_DIM, 0, 1)
    same_head = jnp.where(head_r == head_c, 1.0, 0.0).astype(BF16)

    def body(n, c):
        kf = k_ref[_blk(n), :].astype(F32) * (scale * LOG2E)
        v_t = v_ref[_blk(n), :].astype(F32).T.astype(BF16)
        for hh in range(HEADS_PER_BLOCK):
            kp_ref[hh, _blk(n), :] = jnp.where(_head_lanes(hh), kf, bias_cols[hh]).astype(BF16)
            vt_ref[hh, 0:HEAD_DIM, _blk(n)] = v_t[hh * HEAD_DIM:(hh + 1) * HEAD_DIM, :]
            vt_ref[hh, HEAD_DIM:V_ROWS, _blk(n)] = jnp.ones((DENOM_ROWS, ATT_T), BF16)
        norm2 = _dot((kf * kf).astype(BF16), same_head)
        kmax_ref[...] = jnp.maximum(kmax_ref[...], jnp.max(norm2, axis=0, keepdims=True))
        return c

    lax.fori_loop(0, n_blocks, body, 0, unroll=BUILD_UNROLL)


def _chain_qt(c, hh, seg_lo, seg_width, q_t, qt_ref):
    row = lax.broadcasted_iota(jnp.int32, (LANES, ATT_T), 0)
    lo = (1 - hh) * HEAD_DIM
    qsel = jnp.where((row >= seg_lo) & (row < seg_lo + seg_width), q_t, 0.0)
    qt_ref[c] = jnp.where((row >= lo) & (row < lo + N_BIAS_COLS), 1.0, qsel).astype(BF16)
    return qsel, jnp.sum(qsel * qsel, axis=0, keepdims=True)


def _stage_scores(c, hh, n, slot, kp_ref, qt_ref, sbuf_ref):
    sbuf_ref[c * STAGE_DEPTH + slot] = _dot(kp_ref[hh, _blk(jnp.maximum(n, 0)), :], qt_ref[c])


def _diag_blocks(chains, qi_of, qn2, slope2, kp_ref, vt_ref, kmax_ref, qt_ref, ref_ref, acc_ref, sbuf_ref):
    krow, qcol = _key_query_iotas()
    causal = krow <= qcol
    r = lax.broadcasted_iota(jnp.int32, (1, ATT_T), 1).astype(F32)
    scores = [_dot(kp_ref[hh, _blk(qi_of[c]), :], qt_ref[c]) for c, hh in enumerate(chains)]
    for slot in range(STAGE_DEPTH):
        for c, hh in enumerate(chains):
            _stage_scores(c, hh, qi_of[c] - 1 - slot, slot, kp_ref, qt_ref, sbuf_ref)
    gaps = []
    for c, hh in enumerate(chains):
        s = jnp.where(causal, scores[c], NEG)
        m = jnp.max(s, axis=0, keepdims=True)
        kmax2 = kmax_ref[:, hh * HEAD_DIM:hh * HEAD_DIM + 1]
        bound = jnp.sqrt(qn2[c] * kmax2) * BOUND_SLACK + slope2[hh] * r
        ref_ref[c] = m
        acc_ref[c] = _dot(vt_ref[hh, :, _blk(qi_of[c])], jnp.exp2(s - m).astype(BF16))
        gaps.append(jnp.max(bound - m))
    return gaps


def _first_block(gap, inv_blk, qi):
    reach = jnp.minimum((gap + SKIP_BITS) * inv_blk, 1e6).astype(jnp.int32)
    return jnp.maximum(qi - 1 - reach, 0)


def _fixed_ref_step(c, hh, n, slot, off, kp_ref, vt_ref, qt_ref, ref_ref, acc_ref, sbuf_ref):
    p = jnp.exp2(sbuf_ref[c * STAGE_DEPTH + slot] - (ref_ref[c] - off)).astype(BF16)
    _stage_scores(c, hh, n - STAGE_DEPTH, slot, kp_ref, qt_ref, sbuf_ref)
    acc_ref[c] += _dot(vt_ref[hh, :, _blk(jnp.maximum(n, 0))], p)


def _online_step(c, hh, n, off, kp_ref, vt_ref, qt_ref, m_ref, acc_ref):
    s = _dot(kp_ref[hh, _blk(n), :], qt_ref[c])
    m_old = m_ref[c]
    m_new = jnp.maximum(m_old, jnp.max(s, axis=0, keepdims=True) + off)
    p = jnp.exp2(s - (m_new - off)).astype(BF16)
    acc_ref[c] = acc_ref[c] * jnp.exp2(m_old - m_new) + _dot(vt_ref[hh, :, _blk(n)], p)
    m_ref[c] = m_new


def _softmax_sweep(qi_of, chains, gaps, depth, off_fn, kp_ref, vt_ref, qt_ref, ref_ref, acc_ref, sbuf_ref,
                   split_heads):
    refs = (kp_ref, vt_ref, qt_ref, ref_ref, acc_ref)
    assert STAGE_DEPTH == 2
    fixed_ok = functools.reduce(jnp.maximum, gaps) <= MAX_EXPONENT

    chains_per_tile = len(chains) // TILES_PER_PASS

    def run_fixed(start, count, heads):
        active = [(c, hh) for c, hh in enumerate(chains) if hh in heads]
        per_trip = max(STAGE_DEPTH, ITEMS_PER_TRIP // len(active))
        if len(heads) == 1:
            per_trip = min(per_trip, LONE_HEAD_BLOCKS_PER_TRIP)
        assert per_trip % STAGE_DEPTH == 0

        def step(below, slot, who):
            for c, hh in who:
                n = qi_of[c] - 1 - below
                off = off_fn(c, hh, jnp.maximum(n, 0)) + jnp.where(n < 0, NEG, 0.0)
                _fixed_ref_step(c, hh, n, slot, off, *refs, sbuf_ref)

        def sweep(start, trips, per_trip):
            def body(i, carry):
                for j in range(per_trip):
                    step(start + i * per_trip + j, j % STAGE_DEPTH, active)
                return carry

            lax.fori_loop(0, trips, body, 0)

        common = jnp.clip(qi_of[0] - start, 0, count) // STAGE_DEPTH * STAGE_DEPTH
        whole = common // per_trip
        sweep(start, whole, per_trip)
        if per_trip > STAGE_DEPTH:
            sweep(start + whole * per_trip, (common - whole * per_trip) // STAGE_DEPTH, STAGE_DEPTH)
        for k in range(TILES_PER_PASS):
            who = [(c, hh) for c, hh in active if c // chains_per_tile >= k]

            @pl.when(k < count - common)
            def _one_depth(k=k, who=who):
                step(start + common + k, k % STAGE_DEPTH, who)

    @pl.when(fixed_ok)
    def _fixed_reference():
        far = jnp.maximum(depth[0], depth[1])
        if split_heads:
            near = jnp.minimum(depth[0], depth[1])
            together = jnp.minimum((near + 1) // 2 * 2, far)
            run_fixed(0, together, (0, 1))
            for hh in range(HEADS_PER_BLOCK):
                run_fixed(together, jnp.maximum(depth[hh] - together, 0), (hh,))
        else:
            run_fixed(0, far, (0, 1))

    @pl.when(jnp.logical_not(fixed_ok))
    def _running_maximum():
        def body(n, carry):
            for c, hh in enumerate(chains):
                off = off_fn(c, hh, n) + jnp.where(n < qi_of[c], 0.0, NEG)
                _online_step(c, hh, n, off, *refs)
            return carry

        lax.fori_loop(0, functools.reduce(jnp.maximum, qi_of), body, 0)


def _normalized(acc):
    return acc[0:HEAD_DIM, :] * (1.0 / acc[HEAD_DIM:HEAD_DIM + 1, :])


def _row_is_head0():
    return lax.broadcasted_iota(jnp.int32, (LANES, ATT_T), 0) < HEAD_DIM


def _moba_kernel(tab_ref, q_ref, k_ref, v_ref, o_ref, kp_ref, vt_ref, kmax_ref, kmean_ref, selm_ref,
                 qt_ref, ref_ref, acc_ref, sbuf_ref, *, n_blocks, scale):
    hb = pl.program_id(1)
    nbp = kmean_ref.shape[0]
    _build_softmax_operands(tab_ref, hb, k_ref, v_ref, kp_ref, vt_ref, kmax_ref, n_blocks=n_blocks, scale=scale)
    kmean_ref[...] = jnp.zeros_like(kmean_ref)

    def block_mean(n, c):
        kmean_ref[pl.ds(n, 1), :] = jnp.mean(k_ref[_blk(n), :].astype(F32), axis=0, keepdims=True)
        return c

    lax.fori_loop(0, n_blocks, block_mean, 0, unroll=BUILD_UNROLL)

    blk_id = lax.broadcasted_iota(jnp.int32, (nbp, ATT_T), 0)
    blk_f = blk_id.astype(F32)
    chains = list(range(HEADS_PER_BLOCK))
    n_ch = len(chains)
    slope2 = [tab_ref[hb * HEADS_PER_BLOCK + hh] for hh in chains]
    assert n_blocks % TILES_PER_PASS == 0

    def tile_group(g, carry):
        tiles = [g * TILES_PER_PASS + s for s in range(TILES_PER_PASS)]
        qn2 = []
        for s, qi in enumerate(tiles):
            q_t = q_ref[_blk(qi), :].astype(F32).T
            for hh in chains:
                hm = _head_lanes(hh)
                qsel, norm2 = _chain_qt(s * n_ch + hh, hh, hh * HEAD_DIM, HEAD_DIM, q_t, qt_ref)
                qn2.append(norm2)

                gate = _dot(jnp.where(hm, kmean_ref[...], 0.0), qsel, precision=lax.Precision.HIGHEST)
                gt = jnp.where(blk_id < qi, gate, -jnp.inf)
                sel = jnp.zeros(gt.shape, dtype=jnp.bool_)
                for _ in range(MOBA_TOPK):
                    mx = jnp.max(gt, axis=0, keepdims=True)
                    pick = jnp.min(jnp.where(gt == mx, blk_f, float(nbp)), axis=0, keepdims=True)
                    hit = blk_f == pick
                    sel = sel | (hit & (mx > -jnp.inf))
                    gt = jnp.where(hit, -jnp.inf, gt)
                selm_ref[s * n_ch + hh] = jnp.where(sel, 0.0, NEG)

        qi_of = [tiles[c // n_ch] for c in range(TILES_PER_PASS * n_ch)]
        gaps = _diag_blocks(chains * TILES_PER_PASS, qi_of, qn2, slope2, kp_ref, vt_ref, kmax_ref, qt_ref,
                            ref_ref, acc_ref, sbuf_ref)
        all_chains = chains * TILES_PER_PASS
        depth = [None] * HEADS_PER_BLOCK
        for c, hh in enumerate(all_chains):
            d = qi_of[c] - _first_block(gaps[c], tab_ref[N_HEADS + hb * HEADS_PER_BLOCK + hh], qi_of[c])
            depth[hh] = d if depth[hh] is None else jnp.maximum(depth[hh], d)

        def off_fn(c, hh, n):
            return slope2[hh] * jnp.asarray((n - qi_of[c]) * ATT_T, F32) + selm_ref[c, pl.ds(n, 1), :]

        _softmax_sweep(qi_of, all_chains, gaps, depth, off_fn, kp_ref, vt_ref, qt_ref, ref_ref, acc_ref,
                       sbuf_ref, split_heads=False)
        for s, qi in enumerate(tiles):
            o = jnp.concatenate([_normalized(acc_ref[s * n_ch + hh]) for hh in chains], axis=0)
            o_ref[_blk(qi), :] = o.T.astype(o_ref.dtype)
        return carry

    lax.fori_loop(0, n_blocks // TILES_PER_PASS, tile_group, 0)


def _diff_kernel(tab_ref, lam_ref, g_ref, q_ref, k_ref, v_ref, o_ref, kp_ref, vt_ref, kmax_ref, qt_ref,
                 ref_ref, acc_ref, sbuf_ref, *, n_blocks, scale, lambda_init):
    hb = pl.program_id(1)
    _build_softmax_operands(tab_ref, hb, k_ref, v_ref, kp_ref, vt_ref, kmax_ref, n_blocks=n_blocks, scale=scale)

    lf = lam_ref[...].astype(F32)
    lam = (jnp.exp(jnp.sum(lf[0:1] * lf[1:2], axis=-1, keepdims=True))
           - jnp.exp(jnp.sum(lf[2:3] * lf[3:4], axis=-1, keepdims=True)) + lambda_init)

    chains = [hh for hh in range(HEADS_PER_BLOCK) for _ in range(2)]
    n_ch = len(chains)
    slope2 = [tab_ref[hb * HEADS_PER_BLOCK + hh] for hh in range(HEADS_PER_BLOCK)]
    assert n_blocks % TILES_PER_PASS == 0

    def tile_group(g, carry):
        tiles = [g * TILES_PER_PASS + s for s in range(TILES_PER_PASS)]
        qn2 = []
        for s, qi in enumerate(tiles):
            q_t = q_ref[_blk(qi), :].astype(F32).T
            for c, hh in enumerate(chains):
                seg_lo = hh * HEAD_DIM + (c % 2) * DIFF_QK_DIM
                qn2.append(_chain_qt(s * n_ch + c, hh, seg_lo, DIFF_QK_DIM, q_t, qt_ref)[1])
        qi_of = [tiles[c // n_ch] for c in range(TILES_PER_PASS * n_ch)]
        gaps = _diag_blocks(chains * TILES_PER_PASS, qi_of, qn2, slope2, kp_ref, vt_ref, kmax_ref, qt_ref,
                            ref_ref, acc_ref, sbuf_ref)
        all_chains = chains * TILES_PER_PASS
        depth = [None] * HEADS_PER_BLOCK
        for c, hh in enumerate(all_chains):
            d = qi_of[c] - _first_block(gaps[c], tab_ref[N_HEADS + hb * HEADS_PER_BLOCK + hh], qi_of[c])
            depth[hh] = d if depth[hh] is None else jnp.maximum(depth[hh], d)

        def off_fn(c, hh, n):
            return slope2[hh] * jnp.asarray((n - qi_of[c]) * ATT_T, F32)

        _softmax_sweep(qi_of, all_chains, gaps, depth, off_fn, kp_ref, vt_ref, qt_ref, ref_ref, acc_ref,
                       sbuf_ref, split_heads=True)
        for s, qi in enumerate(tiles):
            base = s * n_ch
            outs = []
            for hh in range(HEADS_PER_BLOCK):
                o = (_normalized(acc_ref[base + 2 * hh])
                     - lam * _normalized(acc_ref[base + 2 * hh + 1]))
                ms = jnp.mean(o * o, axis=0, keepdims=True)
                outs.append(o * lax.rsqrt(ms + SUBLN_EPS))
            o = jnp.concatenate(outs, axis=0).T * (g_ref[...] * (1.0 - lambda_init))
            o_ref[_blk(qi), :] = o.astype(o_ref.dtype)
        return carry

    lax.fori_loop(0, n_blocks // TILES_PER_PASS, tile_group, 0)


def _log_sigmoids(z):
    lneg = jnp.minimum(-z, 0.0) - jnp.log(1.0 + jnp.exp(-jnp.abs(z)))
    return lneg, lneg + z


def _suffix_sums(upper, lg):
    hi = lg.astype(BF16)
    lo = (lg - hi.astype(F32)).astype(BF16)
    return _dot(upper, hi) + _dot(upper, lo)


def _sb_pass(tiles, k_ref, vt_ref, qt_ref, carry_ref, acc_ref):
    krow, qcol = _key_query_iotas()
    before = krow < qcol
    upper = (qcol > krow).astype(BF16)
    items = [(slot * HEADS_PER_BLOCK + hh, hh, n, diag)
             for slot, blocks in tiles for n, diag in blocks for hh in range(HEADS_PER_BLOCK)]
    z = [_dot(k_ref[_blk(n), :], qt_ref[st]) for st, _, n, _ in items]
    log_sig, log_surv, tails = [], [], []
    for i, (st, hh, n, diag) in enumerate(items):
        lneg, lpos = _log_sigmoids(jnp.where(before, z[i], NEG) if diag else z[i])
        log_surv.append(lneg)
        log_sig.append(lpos)
        tails.append(_suffix_sums(upper, lneg))
    carry, acc = {}, {}
    for slot, blocks in tiles:
        for hh in range(HEADS_PER_BLOCK):
            st = slot * HEADS_PER_BLOCK + hh
            starts_fresh = blocks[0][1]
            carry[st] = None if starts_fresh else carry_ref[st]
            acc[st] = None if starts_fresh else acc_ref[st]
    for i, (st, hh, n, diag) in enumerate(items):
        w = jnp.exp(log_sig[i] + (tails[i] if diag else tails[i] + carry[st]))
        part = _dot(vt_ref[hh, :, _blk(n)], w.astype(BF16))
        acc[st] = part if acc[st] is None else acc[st] + part
        total = jnp.sum(log_surv[i], axis=0, keepdims=True)
        carry[st] = total if carry[st] is None else carry[st] + total
    worst = []
    for slot, _ in tiles:
        tops = []
        for hh in range(HEADS_PER_BLOCK):
            st = slot * HEADS_PER_BLOCK + hh
            acc_ref[st] = acc[st]
            carry_ref[st] = carry[st]
            tops.append(jnp.max(carry[st]))
        worst.append(functools.reduce(jnp.maximum, tops))
    return tuple(worst)


def _sb_kernel(q_ref, k_ref, v_ref, o_ref, vt_ref, qt_ref, carry_ref, acc_ref, *, n_blocks, scale):
    def transpose_values(n, c):
        v_t = v_ref[_blk(n), :].astype(F32).T.astype(BF16)
        for hh in range(HEADS_PER_BLOCK):
            vt_ref[hh, :, _blk(n)] = v_t[hh * HEAD_DIM:(hh + 1) * HEAD_DIM, :]
        return c

    lax.fori_loop(0, n_blocks, transpose_values, 0, unroll=BUILD_UNROLL)

    assert n_blocks % SB_TILES_PER_PASS == 0
    refs = (k_ref, vt_ref, qt_ref, carry_ref, acc_ref)
    row_h0 = _row_is_head0()

    def tile_group(g, carry):
        tiles = [g * SB_TILES_PER_PASS + slot for slot in range(SB_TILES_PER_PASS)]
        for slot, qi in enumerate(tiles):
            q_t = (q_ref[_blk(qi), :].astype(F32) * scale).T
            for hh in range(HEADS_PER_BLOCK):
                in_head = row_h0 if hh == 0 else jnp.logical_not(row_h0)
                qt_ref[slot * HEADS_PER_BLOCK + hh] = jnp.where(in_head, q_t, 0.0).astype(BF16)

        def blocks_of(qi, with_previous):
            return [(qi, True), (qi - 1, False)] if with_previous else [(qi, True)]

        worst = lax.cond(
            g > 0,
            lambda: _sb_pass([(slot, blocks_of(qi, True)) for slot, qi in enumerate(tiles)], *refs),
            lambda: _sb_pass([(slot, blocks_of(qi, slot > 0)) for slot, qi in enumerate(tiles)], *refs))

        for slot, qi in enumerate(tiles):
            def cond(c):
                j, live = c
                return (j >= 0) & live

            def body(c, slot=slot):
                j, _ = c
                return j - 1, _sb_pass([(slot, [(j, False)])], *refs)[0] > SB_EXIT

            lax.while_loop(cond, body, (qi - 2, worst[slot] > SB_EXIT))
            st = slot * HEADS_PER_BLOCK
            o = jnp.concatenate([acc_ref[st + hh] for hh in range(HEADS_PER_BLOCK)], axis=0)
            o_ref[_blk(qi), :] = o.T.astype(o_ref.dtype)
        return carry

    lax.fori_loop(0, n_blocks // SB_TILES_PER_PASS, tile_group, 0)


def _attn_specs(B, S, q_base, k_base, v_base):
    qspec = pl.BlockSpec((None, S, LANES), lambda b, h: (b, 0, q_base + h))
    kspec = pl.BlockSpec((None, S, LANES), lambda b, h: (b, 0, k_base + h))
    vspec = pl.BlockSpec((None, S, LANES), lambda b, h: (b, 0, v_base + h))
    ospec = pl.BlockSpec((None, S, LANES), lambda b, h: (b, 0, h))
    grid = (B, N_HEAD_BLOCKS)
    params = pltpu.CompilerParams(
        dimension_semantics=("parallel", "parallel"), vmem_limit_bytes=VMEM_LIMIT)
    return grid, qspec, kspec, vspec, ospec, params


_SMEM_SPEC = pl.BlockSpec(memory_space=pltpu.SMEM)


def _moba(qkv, tab):
    B, S, _ = qkv.shape
    grid, qspec, kspec, vspec, ospec, params = _attn_specs(B, S, 0, 3, 6)
    n_blocks = S // MOBA_BLOCK
    nbp = -(-n_blocks // SUBLANES) * SUBLANES
    n_slots = TILES_PER_PASS * HEADS_PER_BLOCK
    return pl.pallas_call(
        functools.partial(_moba_kernel, n_blocks=n_blocks, scale=HEAD_DIM ** -0.5),
        grid=grid,
        in_specs=[_SMEM_SPEC, qspec, kspec, vspec],
        out_specs=ospec,
        out_shape=jax.ShapeDtypeStruct((B, S, MIX_W), BF16),
        scratch_shapes=[
            pltpu.VMEM((HEADS_PER_BLOCK, S, LANES), BF16),
            pltpu.VMEM((HEADS_PER_BLOCK, V_ROWS, S), BF16),
            pltpu.VMEM((1, LANES), F32),
            pltpu.VMEM((nbp, LANES), F32),
            pltpu.VMEM((n_slots, nbp, ATT_T), F32),
            pltpu.VMEM((n_slots, LANES, ATT_T), BF16),
            pltpu.VMEM((n_slots, 1, ATT_T), F32),
            pltpu.VMEM((n_slots, V_ROWS, ATT_T), F32),
            pltpu.VMEM((n_slots * STAGE_DEPTH, ATT_T, ATT_T), F32),
        ],
        compiler_params=params,
        name="moba_attn",
    )(tab, qkv, qkv, qkv)


def _diff(qkv, tab, diff_lambda, subln_g, lambda_init):
    B, S, _ = qkv.shape
    grid, qspec, kspec, vspec, ospec, params = _attn_specs(B, S, 9, 12, 15)
    n_chains = TILES_PER_PASS * 2 * HEADS_PER_BLOCK
    return pl.pallas_call(
        functools.partial(_diff_kernel, n_blocks=S // ATT_T, scale=DIFF_QK_DIM ** -0.5,
                          lambda_init=lambda_init),
        grid=grid,
        in_specs=[
            _SMEM_SPEC,
            pl.BlockSpec(diff_lambda.shape, lambda b, h: (0, 0)),
            pl.BlockSpec((1, LANES), lambda b, h: (0, 0)),
            qspec, kspec, vspec,
        ],
        out_specs=ospec,
        out_shape=jax.ShapeDtypeStruct((B, S, MIX_W), BF16),
        scratch_shapes=[
            pltpu.VMEM((HEADS_PER_BLOCK, S, LANES), BF16),
            pltpu.VMEM((HEADS_PER_BLOCK, V_ROWS, S), BF16),
            pltpu.VMEM((1, LANES), F32),
            pltpu.VMEM((n_chains, LANES, ATT_T), BF16),
            pltpu.VMEM((n_chains, 1, ATT_T), F32),
            pltpu.VMEM((n_chains, V_ROWS, ATT_T), F32),
            pltpu.VMEM((n_chains * STAGE_DEPTH, ATT_T, ATT_T), F32),
        ],
        compiler_params=params,
        name="diff_attn",
    )(tab, diff_lambda, subln_g, qkv, qkv, qkv)


def _sb(qkv):
    B, S, _ = qkv.shape
    grid, qspec, kspec, vspec, ospec, params = _attn_specs(B, S, 18, 21, 24)
    return pl.pallas_call(
        functools.partial(_sb_kernel, n_blocks=S // ATT_T, scale=HEAD_DIM ** -0.5),
        grid=grid,
        in_specs=[qspec, kspec, vspec],
        out_specs=ospec,
        out_shape=jax.ShapeDtypeStruct((B, S, MIX_W), BF16),
        scratch_shapes=[
            pltpu.VMEM((HEADS_PER_BLOCK, HEAD_DIM, S), BF16),
            pltpu.VMEM((SB_TILES_PER_PASS * HEADS_PER_BLOCK, LANES, ATT_T), BF16),
            pltpu.VMEM((SB_TILES_PER_PASS * HEADS_PER_BLOCK, 1, ATT_T), F32),
            pltpu.VMEM((SB_TILES_PER_PASS * HEADS_PER_BLOCK, HEAD_DIM, ATT_T), F32),
        ],
        compiler_params=params,
        name="sb_attn",
    )(qkv, qkv, qkv)


def _merge_ln_kernel(x_ref, om_ref, od_ref, os_ref, wg_ref, bg_ref, wbr_ref, wo_ref, lg_ref, lb_ref,
                     o_ref, *, alpha, d_model):
    x = x_ref[...]
    xb = x.astype(BF16)
    merged = None
    for b, ob_ref in enumerate((om_ref, od_ref, os_ref)):
        glogit = _dot(xb, wg_ref[:, b * d_model:(b + 1) * d_model]) + bg_ref[b:b + 1, :]
        term = jax.nn.sigmoid(glogit) * _dot(ob_ref[...], wbr_ref[b])
        merged = term if merged is None else merged + term
    y = alpha * x + _dot(merged.astype(BF16), wo_ref[...])
    o_ref[...] = _layer_norm(y, lg_ref[...], lb_ref[...])


def _merge_ln(x, o_m, o_d, o_s, w_gate, b_gate, w_br, w_out, lg, lb, l, alpha, tm=512):
    T, D = x.shape
    return pl.pallas_call(
        functools.partial(_merge_ln_kernel, alpha=alpha, d_model=D),
        grid=(T // tm,),
        in_specs=[
            pl.BlockSpec((tm, D), lambda i: (i, 0)),
            pl.BlockSpec((tm, MIX_W), lambda i: (i, 0)),
            pl.BlockSpec((tm, MIX_W), lambda i: (i, 0)),
            pl.BlockSpec((tm, MIX_W), lambda i: (i, 0)),
            pl.BlockSpec((None, D, N_BRANCH * D), lambda i: (l, 0, 0), pipeline_mode=pl.Buffered(1)),
            pl.BlockSpec((None, N_BRANCH, D), lambda i: (l, 0, 0)),
            pl.BlockSpec((None, N_BRANCH, MIX_W, D), lambda i: (l, 0, 0, 0), pipeline_mode=pl.Buffered(1)),
            pl.BlockSpec((None, D, D), lambda i: (l, 0, 0), pipeline_mode=pl.Buffered(1)),
            pl.BlockSpec((1, D), lambda i: (0, 0)),
            pl.BlockSpec((1, D), lambda i: (0, 0)),
        ],
        out_specs=pl.BlockSpec((tm, D), lambda i: (i, 0)),
        out_shape=jax.ShapeDtypeStruct((T, D), F32),
        compiler_params=pltpu.CompilerParams(
            dimension_semantics=("parallel",), vmem_limit_bytes=VMEM_LIMIT),
        name="merge_ln",
    )(x, o_m, o_d, o_s, w_gate, b_gate, w_br, w_out, lg, lb)


def _alibi_slopes(n):
    return (2.0 ** (-8.0 * np.arange(1, n + 1, dtype=np.float32) / n)).astype(np.float32)


def _slope_table(slopes):
    s2 = slopes.astype(np.float64) * LOG2E
    return np.concatenate([s2, 1.0 / (s2 * ATT_T)]).astype(np.float32)


def kernel(x, ln_g, ln_b, ffn_w_gate, ffn_w_up, ffn_w_down, w_in, b_gate, diff_lambda, diff_subln_g,
           w_br_moba, w_br_diff, w_br_sb, w_out):
    B, S, D = x.shape
    depth = ln_g.shape[0]
    assert S % ATT_T == 0 and ATT_T == MOBA_BLOCK
    assert w_in.shape[-1] == QKV_W + N_BRANCH * D
    alpha = (2.0 * depth) ** 0.25

    wg = ffn_w_gate.astype(BF16)
    wu = ffn_w_up.astype(BF16)
    wd = ffn_w_down.astype(BF16)
    w_qkv = w_in[:, :, :QKV_W].astype(BF16)
    w_gate = w_in[:, :, QKV_W:].astype(BF16)
    w_br = jnp.stack([w_br_moba, w_br_diff, w_br_sb], axis=1).astype(BF16)
    w_out_b = w_out.astype(BF16)
    slopes = _alibi_slopes(2 * N_HEADS)
    tab_moba = jnp.asarray(_slope_table(slopes[0::2]))
    tab_diff = jnp.asarray(_slope_table(slopes[1::2]))
    subln_g = jnp.tile(diff_subln_g.astype(F32), (1, HEADS_PER_BLOCK))

    h = x.reshape(B * S, D)
    for l in range(depth):
        lg = ln_g[l][:, None, :]
        lb = ln_b[l][:, None, :]
        h = _ffn_ln(h, wg, wu, wd, lg[0], lb[0], l, 0, alpha)
        qkv = _qkv_proj(h, w_qkv, l).reshape(B, S, QKV_W)
        lambda_init = 0.8 - 0.6 * math.exp(-0.3 * l)
        o_m = _moba(qkv, tab_moba)
        o_d = _diff(qkv, tab_diff, diff_lambda[l], subln_g[l][None, :], lambda_init)
        o_s = _sb(qkv)
        T = B * S
        h = _merge_ln(h, o_m.reshape(T, MIX_W), o_d.reshape(T, MIX_W), o_s.reshape(T, MIX_W),
                      w_gate, b_gate, w_br, w_out_b, lg[1], lb[1], l, alpha)
        h = _ffn_ln(h, wg, wu, wd, lg[2], lb[2], l, 1, alpha)
    return h.reshape(B, S, D)
```

```python
import functools
import math

import numpy as np
import jax
import jax.numpy as jnp
from jax import lax
from jax.experimental import pallas as pl
from jax.experimental.pallas import tpu as pltpu

F32 = jnp.float32
BF16 = jnp.bfloat16

HEAD_DIM = 64
N_HEADS = 6
DIFF_QK_DIM = HEAD_DIM // 2
MOBA_BLOCK = 256
MOBA_TOPK = 3
N_BRANCH = 3
LN_EPS = 1e-5
SUBLN_EPS = 1e-5

LANES = 128
SUBLANES = 8
DENOM_ROWS = 16
V_ROWS = HEAD_DIM + DENOM_ROWS
HEADS_PER_BLOCK = LANES // HEAD_DIM
N_HEAD_BLOCKS = N_HEADS // HEADS_PER_BLOCK
MIX_W = N_HEADS * HEAD_DIM
QKV_W = 9 * MIX_W
ATT_T = 256
NEG = -1e30
SB_EXIT = -110.0
TILES_PER_PASS = 8
SB_TILES_PER_PASS = 4
LOG2E = math.log2(math.e)
N_BIAS_COLS = 3
BOUND_SLACK = 1.01
MAX_EXPONENT = 100.0
SKIP_BITS = 150.0
ITEMS_PER_TRIP = 32
LONE_HEAD_BLOCKS_PER_TRIP = 4
BUILD_UNROLL = 4
STAGE_DEPTH = 2
VMEM_LIMIT = 56 * 1024 * 1024


def _dot(a, b, precision=None):
    return jnp.dot(a, b, precision=precision, preferred_element_type=F32)


def _layer_norm(y, g, b):
    mu = jnp.mean(y, axis=-1, keepdims=True)
    yc = y - mu
    var = jnp.mean(yc * yc, axis=-1, keepdims=True)
    return yc * lax.rsqrt(var + LN_EPS) * g + b


def _ffn_ln_kernel(x_ref, wg_ref, wu_ref, wd_ref, lg_ref, lb_ref, o_ref, hid_ref, *, alpha, tf):
    x = x_ref[...]
    xb = x.astype(BF16)
    for c in range(hid_ref.shape[1] // tf):
        cols = slice(c * tf, (c + 1) * tf)
        gate = _dot(xb, wg_ref[:, cols])
        up = _dot(xb, wu_ref[:, cols])
        hid_ref[:, cols] = (gate * jax.nn.sigmoid(gate) * up).astype(BF16)
    y = alpha * x + 0.5 * _dot(hid_ref[...], wd_ref[...])
    o_ref[...] = _layer_norm(y, lg_ref[...], lb_ref[...])


def _ffn_ln(x, wg, wu, wd, lg, lb, l, j, alpha, tm=512, tf=256):
    T, D = x.shape
    FF = wg.shape[-1]
    resident = pl.Buffered(1)
    return pl.pallas_call(
        functools.partial(_ffn_ln_kernel, alpha=alpha, tf=tf),
        grid=(T // tm,),
        in_specs=[
            pl.BlockSpec((tm, D), lambda i: (i, 0)),
            pl.BlockSpec((None, None, D, FF), lambda i: (l, j, 0, 0), pipeline_mode=resident),
            pl.BlockSpec((None, None, D, FF), lambda i: (l, j, 0, 0), pipeline_mode=resident),
            pl.BlockSpec((None, None, FF, D), lambda i: (l, j, 0, 0), pipeline_mode=resident),
            pl.BlockSpec((1, D), lambda i: (0, 0)),
            pl.BlockSpec((1, D), lambda i: (0, 0)),
        ],
        out_specs=pl.BlockSpec((tm, D), lambda i: (i, 0)),
        out_shape=jax.ShapeDtypeStruct((T, D), F32),
        scratch_shapes=[pltpu.VMEM((tm, FF), BF16)],
        compiler_params=pltpu.CompilerParams(
            dimension_semantics=("parallel",), vmem_limit_bytes=VMEM_LIMIT),
        name="ffn_ln",
    )(x, wg, wu, wd, lg, lb)


def _proj_kernel(x_ref, w_ref, o_ref, *, tn):
    xb = x_ref[...].astype(BF16)
    for c in range(o_ref.shape[1] // tn):
        cols = slice(c * tn, (c + 1) * tn)
        o_ref[:, cols] = _dot(xb, w_ref[:, cols]).astype(o_ref.dtype)


def _qkv_proj(x, w_in, l, tm=512, tn=1152):
    T, D = x.shape
    return pl.pallas_call(
        functools.partial(_proj_kernel, tn=tn),
        grid=(T // tm,),
        in_specs=[
            pl.BlockSpec((tm, D), lambda i: (i, 0)),
            pl.BlockSpec((None, D, QKV_W), lambda i: (l, 0, 0), pipeline_mode=pl.Buffered(1)),
        ],
        out_specs=pl.BlockSpec((tm, QKV_W), lambda i: (i, 0)),
        out_shape=jax.ShapeDtypeStruct((T, QKV_W), BF16),
        compiler_params=pltpu.CompilerParams(
            dimension_semantics=("parallel",), vmem_limit_bytes=VMEM_LIMIT),
        name="qkv_proj",
    )(x, w_in)


def _lane_iota():
    return lax.broadcasted_iota(jnp.int32, (1, LANES), 1)


def _head_lanes(hh):
    lane = _lane_iota()
    return (lane >= hh * HEAD_DIM) & (lane < (hh + 1) * HEAD_DIM)


def _blk(j):
    return pl.ds(pl.multiple_of(j * ATT_T, ATT_T), ATT_T)


def _key_query_iotas():
    krow = lax.broadcasted_iota(jnp.int32, (ATT_T, ATT_T), 0)
    qcol = lax.broadcasted_iota(jnp.int32, (ATT_T, ATT_T), 1)
    return krow, qcol


def _build_softmax_operands(tab_ref, hb, k_ref, v_ref, kp_ref, vt_ref, kmax_ref, *, n_blocks, scale):
    lane = _lane_iota()
    pos = lax.broadcasted_iota(jnp.int32, (ATT_T, LANES), 0).astype(F32)
    bias_cols = []
    for hh in range(HEADS_PER_BLOCK):
        b = pos * tab_ref[hb * HEADS_PER_BLOCK + hh]
        lo = (1 - hh) * HEAD_DIM
        cols = jnp.zeros_like(b)
        for i in range(N_BIAS_COLS):
            piece = b.astype(BF16).astype(F32)
            cols = jnp.where(lane == lo + i, piece, cols)
            b = b - piece
        bias_cols.append(cols)
    kmax_ref[...] = jnp.zeros_like(kmax_ref)
    assert HEADS_PER_BLOCK == 2
    head_r = jnp.where(lax.broadcasted_iota(jnp.int32, (LANES, LANES), 0) < HEAD_DIM, 0, 1)
    head_c = jnp.where(lax.broadcasted_iota(jnp.int32, (LANES, LANES), 1) < HEAD_DIM, 0, 1)
    same_head = jnp.where(head_r == head_c, 1.0, 0.0).astype(BF16)

    def body(n, c):
        kf = k_ref[_blk(n), :].astype(F32) * (scale * LOG2E)
        v_t = v_ref[_blk(n), :].astype(F32).T.astype(BF16)
        for hh in range(HEADS_PER_BLOCK):
            kp_ref[hh, _blk(n), :] = jnp.where(_head_lanes(hh), kf, bias_cols[hh]).astype(BF16)
            vt_ref[hh, 0:HEAD_DIM, _blk(n)] = v_t[hh * HEAD_DIM:(hh + 1) * HEAD_DIM, :]
            vt_ref[hh, HEAD_DIM:V_ROWS, _blk(n)] = jnp.ones((DENOM_ROWS, ATT_T), BF16)
        norm2 = _dot((kf * kf).astype(BF16), same_head)
        kmax_ref[...] = jnp.maximum(kmax_ref[...], jnp.max(norm2, axis=0, keepdims=True))
        return c

    lax.fori_loop(0, n_blocks, body, 0, unroll=BUILD_UNROLL)


def _chain_qt(c, hh, seg_lo, seg_width, q_t, qt_ref):
    row = lax.broadcasted_iota(jnp.int32, (LANES, ATT_T), 0)
    lo = (1 - hh) * HEAD_DIM
    qsel = jnp.where((row >= seg_lo) & (row < seg_lo + seg_width), q_t, 0.0)
    qt_ref[c] = jnp.where((row >= lo) & (row < lo + N_BIAS_COLS), 1.0, qsel).astype(BF16)
    return qsel, jnp.sum(qsel * qsel, axis=0, keepdims=True)


def _stage_scores(c, hh, n, slot, kp_ref, qt_ref, sbuf_ref):
    sbuf_ref[c * STAGE_DEPTH + slot] = _dot(kp_ref[hh, _blk(jnp.maximum(n, 0)), :], qt_ref[c])


def _diag_blocks(chains, qi_of, qn2, slope2, kp_ref, vt_ref, kmax_ref, qt_ref, ref_ref, acc_ref, sbuf_ref):
    krow, qcol = _key_query_iotas()
    causal = krow <= qcol
    r = lax.broadcasted_iota(jnp.int32, (1, ATT_T), 1).astype(F32)
    scores = [_dot(kp_ref[hh, _blk(qi_of[c]), :], qt_ref[c]) for c, hh in enumerate(chains)]
    for slot in range(STAGE_DEPTH):
        for c, hh in enumerate(chains):
            _stage_scores(c, hh, qi_of[c] - 1 - slot, slot, kp_ref, qt_ref, sbuf_ref)
    gaps = []
    for c, hh in enumerate(chains):
        s = jnp.where(causal, scores[c], NEG)
        m = jnp.max(s, axis=0, keepdims=True)
        kmax2 = kmax_ref[:, hh * HEAD_DIM:hh * HEAD_DIM + 1]
        bound = jnp.sqrt(qn2[c] * kmax2) * BOUND_SLACK + slope2[hh] * r
        ref_ref[c] = m
        acc_ref[c] = _dot(vt_ref[hh, :, _blk(qi_of[c])], jnp.exp2(s - m).astype(BF16))
        gaps.append(jnp.max(bound - m))
    return gaps


def _first_block(gap, inv_blk, qi):
    reach = jnp.minimum((gap + SKIP_BITS) * inv_blk, 1e6).astype(jnp.int32)
    return jnp.maximum(qi - 1 - reach, 0)


def _fixed_ref_step(c, hh, n, slot, off, kp_ref, vt_ref, qt_ref, ref_ref, acc_ref, sbuf_ref):
    p = jnp.exp2(sbuf_ref[c * STAGE_DEPTH + slot] - (ref_ref[c] - off)).astype(BF16)
    _stage_scores(c, hh, n - STAGE_DEPTH, slot, kp_ref, qt_ref, sbuf_ref)
    acc_ref[c] += _dot(vt_ref[hh, :, _blk(jnp.maximum(n, 0))], p)


def _online_step(c, hh, n, off, kp_ref, vt_ref, qt_ref, m_ref, acc_ref):
    s = _dot(kp_ref[hh, _blk(n), :], qt_ref[c])
    m_old = m_ref[c]
    m_new = jnp.maximum(m_old, jnp.max(s, axis=0, keepdims=True) + off)
    p = jnp.exp2(s - (m_new - off)).astype(BF16)
    acc_ref[c] = acc_ref[c] * jnp.exp2(m_old - m_new) + _dot(vt_ref[hh, :, _blk(n)], p)
    m_ref[c] = m_new


def _softmax_sweep(qi_of, chains, gaps, depth, off_fn, kp_ref, vt_ref, qt_ref, ref_ref, acc_ref, sbuf_ref,
                   split_heads):
    refs = (kp_ref, vt_ref, qt_ref, ref_ref, acc_ref)
    assert STAGE_DEPTH == 2
    fixed_ok = functools.reduce(jnp.maximum, gaps) <= MAX_EXPONENT

    chains_per_tile = len(chains) // TILES_PER_PASS

    def run_fixed(start, count, heads):
        active = [(c, hh) for c, hh in enumerate(chains) if hh in heads]
        per_trip = max(STAGE_DEPTH, ITEMS_PER_TRIP // len(active))
        if len(heads) == 1:
            per_trip = min(per_trip, LONE_HEAD_BLOCKS_PER_TRIP)
        assert per_trip % STAGE_DEPTH == 0

        def step(below, slot, who):
            for c, hh in who:
                n = qi_of[c] - 1 - below
                off = off_fn(c, hh, jnp.maximum(n, 0)) + jnp.where(n < 0, NEG, 0.0)
                _fixed_ref_step(c, hh, n, slot, off, *refs, sbuf_ref)

        def sweep(start, trips, per_trip):
            def body(i, carry):
                for j in range(per_trip):
                    step(start + i * per_trip + j, j % STAGE_DEPTH, active)
                return carry

            lax.fori_loop(0, trips, body, 0)

        common = jnp.clip(qi_of[0] - start, 0, count) // STAGE_DEPTH * STAGE_DEPTH
        whole = common // per_trip
        sweep(start, whole, per_trip)
        if per_trip > STAGE_DEPTH:
            sweep(start + whole * per_trip, (common - whole * per_trip) // STAGE_DEPTH, STAGE_DEPTH)
        for k in range(TILES_PER_PASS):
            who = [(c, hh) for c, hh in active if c // chains_per_tile >= k]

            @pl.when(k < count - common)
            def _one_depth(k=k, who=who):
                step(start + common + k, k % STAGE_DEPTH, who)

    @pl.when(fixed_ok)
    def _fixed_reference():
        far = jnp.maximum(depth[0], depth[1])
        if split_heads:
            near = jnp.minimum(depth[0], depth[1])
            together = jnp.minimum((near + 1) // 2 * 2, far)
            run_fixed(0, together, (0, 1))
            for hh in range(HEADS_PER_BLOCK):
                run_fixed(together, jnp.maximum(depth[hh] - together, 0), (hh,))
        else:
            run_fixed(0, far, (0, 1))

    @pl.when(jnp.logical_not(fixed_ok))
    def _running_maximum():
        chains_per_head = chains_per_tile // HEADS_PER_BLOCK
        assert chains[:chains_per_tile] == [c // chains_per_head for c in range(chains_per_tile)]

        def chain_body(c, carry):
            hh = lax.div(lax.rem(c, chains_per_tile), chains_per_head)
            qi = qi_of[0] + lax.div(c, chains_per_tile)

            def body(n, carry):
                _online_step(c, hh, n, off_fn(c, hh, n, qi), *refs)
                return carry

            return lax.fori_loop(0, qi, body, carry)

        lax.fori_loop(0, len(chains), chain_body, 0)


def _normalized(acc):
    return acc[0:HEAD_DIM, :] * (1.0 / acc[HEAD_DIM:HEAD_DIM + 1, :])


def _row_is_head0():
    return lax.broadcasted_iota(jnp.int32, (LANES, ATT_T), 0) < HEAD_DIM


def _moba_kernel(tab_ref, q_ref, k_ref, v_ref, o_ref, kp_ref, vt_ref, kmax_ref, kmean_ref, selm_ref,
                 qt_ref, ref_ref, acc_ref, sbuf_ref, *, n_blocks, scale):
    hb = pl.program_id(1)
    nbp = kmean_ref.shape[0]
    _build_softmax_operands(tab_ref, hb, k_ref, v_ref, kp_ref, vt_ref, kmax_ref, n_blocks=n_blocks, scale=scale)
    kmean_ref[...] = jnp.zeros_like(kmean_ref)

    def block_mean(n, c):
        kmean_ref[pl.ds(n, 1), :] = jnp.mean(k_ref[_blk(n), :].astype(F32), axis=0, keepdims=True)
        return c

    lax.fori_loop(0, n_blocks, block_mean, 0, unroll=BUILD_UNROLL)

    blk_id = lax.broadcasted_iota(jnp.int32, (nbp, ATT_T), 0)
    blk_f = blk_id.astype(F32)
    chains = list(range(HEADS_PER_BLOCK))
    n_ch = len(chains)
    slope2 = [tab_ref[hb * HEADS_PER_BLOCK + hh] for hh in chains]
    assert n_blocks % TILES_PER_PASS == 0

    def tile_group(g, carry):
        tiles = [g * TILES_PER_PASS + s for s in range(TILES_PER_PASS)]
        qn2 = []
        for s, qi in enumerate(tiles):
            q_t = q_ref[_blk(qi), :].astype(F32).T
            for hh in chains:
                hm = _head_lanes(hh)
                qsel, norm2 = _chain_qt(s * n_ch + hh, hh, hh * HEAD_DIM, HEAD_DIM, q_t, qt_ref)
                qn2.append(norm2)

                gate = _dot(jnp.where(hm, kmean_ref[...], 0.0), qsel, precision=lax.Precision.HIGHEST)
                gt = jnp.where(blk_id < qi, gate, -jnp.inf)
                sel = jnp.zeros(gt.shape, dtype=jnp.bool_)
                for _ in range(MOBA_TOPK):
                    mx = jnp.max(gt, axis=0, keepdims=True)
                    pick = jnp.min(jnp.where(gt == mx, blk_f, float(nbp)), axis=0, keepdims=True)
                    hit = blk_f == pick
                    sel = sel | (hit & (mx > -jnp.inf))
                    gt = jnp.where(hit, -jnp.inf, gt)
                selm_ref[s * n_ch + hh] = jnp.where(sel, 0.0, NEG)

        qi_of = [tiles[c // n_ch] for c in range(TILES_PER_PASS * n_ch)]
        gaps = _diag_blocks(chains * TILES_PER_PASS, qi_of, qn2, slope2, kp_ref, vt_ref, kmax_ref, qt_ref,
                            ref_ref, acc_ref, sbuf_ref)
        all_chains = chains * TILES_PER_PASS
        depth = [None] * HEADS_PER_BLOCK
        for c, hh in enumerate(all_chains):
            d = qi_of[c] - _first_block(gaps[c], tab_ref[N_HEADS + hb * HEADS_PER_BLOCK + hh], qi_of[c])
            depth[hh] = d if depth[hh] is None else jnp.maximum(depth[hh], d)

        def off_fn(c, hh, n, qi=None):
            qi = qi_of[c] if qi is None else qi
            slope = tab_ref[hb * HEADS_PER_BLOCK + hh]
            return slope * jnp.asarray((n - qi) * ATT_T, F32) + selm_ref[c, pl.ds(n, 1), :]

        _softmax_sweep(qi_of, all_chains, gaps, depth, off_fn, kp_ref, vt_ref, qt_ref, ref_ref, acc_ref,
                       sbuf_ref, split_heads=False)
        for s, qi in enumerate(tiles):
            o = jnp.concatenate([_normalized(acc_ref[s * n_ch + hh]) for hh in chains], axis=0)
            o_ref[_blk(qi), :] = o.T.astype(o_ref.dtype)
        return carry

    lax.fori_loop(0, n_blocks // TILES_PER_PASS, tile_group, 0)


def _diff_kernel(tab_ref, lam_ref, g_ref, q_ref, k_ref, v_ref, o_ref, kp_ref, vt_ref, kmax_ref, qt_ref,
                 ref_ref, acc_ref, sbuf_ref, *, n_blocks, scale, lambda_init):
    hb = pl.program_id(1)
    _build_softmax_operands(tab_ref, hb, k_ref, v_ref, kp_ref, vt_ref, kmax_ref, n_blocks=n_blocks, scale=scale)

    lf = lam_ref[...].astype(F32)
    lam = (jnp.exp(jnp.sum(lf[0:1] * lf[1:2], axis=-1, keepdims=True))
           - jnp.exp(jnp.sum(lf[2:3] * lf[3:4], axis=-1, keepdims=True)) + lambda_init)

    chains = [hh for hh in range(HEADS_PER_BLOCK) for _ in range(2)]
    n_ch = len(chains)
    slope2 = [tab_ref[hb * HEADS_PER_BLOCK + hh] for hh in range(HEADS_PER_BLOCK)]
    assert n_blocks % TILES_PER_PASS == 0

    def tile_group(g, carry):
        tiles = [g * TILES_PER_PASS + s for s in range(TILES_PER_PASS)]
        qn2 = []
        for s, qi in enumerate(tiles):
            q_t = q_ref[_blk(qi), :].astype(F32).T
            for c, hh in enumerate(chains):
                seg_lo = hh * HEAD_DIM + (c % 2) * DIFF_QK_DIM
                qn2.append(_chain_qt(s * n_ch + c, hh, seg_lo, DIFF_QK_DIM, q_t, qt_ref)[1])
        qi_of = [tiles[c // n_ch] for c in range(TILES_PER_PASS * n_ch)]
        gaps = _diag_blocks(chains * TILES_PER_PASS, qi_of, qn2, slope2, kp_ref, vt_ref, kmax_ref, qt_ref,
                            ref_ref, acc_ref, sbuf_ref)
        all_chains = chains * TILES_PER_PASS
        depth = [None] * HEADS_PER_BLOCK
        for c, hh in enumerate(all_chains):
            d = qi_of[c] - _first_block(gaps[c], tab_ref[N_HEADS + hb * HEADS_PER_BLOCK + hh], qi_of[c])
            depth[hh] = d if depth[hh] is None else jnp.maximum(depth[hh], d)

        def off_fn(c, hh, n, qi=None):
            qi = qi_of[c] if qi is None else qi
            return tab_ref[hb * HEADS_PER_BLOCK + hh] * jnp.asarray((n - qi) * ATT_T, F32)

        _softmax_sweep(qi_of, all_chains, gaps, depth, off_fn, kp_ref, vt_ref, qt_ref, ref_ref, acc_ref,
                       sbuf_ref, split_heads=True)
        for s, qi in enumerate(tiles):
            base = s * n_ch
            outs = []
            for hh in range(HEADS_PER_BLOCK):
                o = (_normalized(acc_ref[base + 2 * hh])
                     - lam * _normalized(acc_ref[base + 2 * hh + 1]))
                ms = jnp.mean(o * o, axis=0, keepdims=True)
                outs.append(o * lax.rsqrt(ms + SUBLN_EPS))
            o = jnp.concatenate(outs, axis=0).T * (g_ref[...] * (1.0 - lambda_init))
            o_ref[_blk(qi), :] = o.astype(o_ref.dtype)
        return carry

    lax.fori_loop(0, n_blocks // TILES_PER_PASS, tile_group, 0)


def _log_sigmoids(z):
    lneg = jnp.minimum(-z, 0.0) - jnp.log(1.0 + jnp.exp(-jnp.abs(z)))
    return lneg, lneg + z


def _suffix_sums(upper, lg):
    hi = lg.astype(BF16)
    lo = (lg - hi.astype(F32)).astype(BF16)
    return _dot(upper, hi) + _dot(upper, lo)


def _sb_pass(tiles, k_ref, vt_ref, qt_ref, carry_ref, acc_ref):
    krow, qcol = _key_query_iotas()
    before = krow < qcol
    upper = (qcol > krow).astype(BF16)
    items = [(slot * HEADS_PER_BLOCK + hh, hh, n, diag)
             for slot, blocks in tiles for n, diag in blocks for hh in range(HEADS_PER_BLOCK)]
    z = [_dot(k_ref[_blk(n), :], qt_ref[st]) for st, _, n, _ in items]
    log_sig, log_surv, tails = [], [], []
    for i, (st, hh, n, diag) in enumerate(items):
        lneg, lpos = _log_sigmoids(jnp.where(before, z[i], NEG) if diag else z[i])
        log_surv.append(lneg)
        log_sig.append(lpos)
        tails.append(_suffix_sums(upper, lneg))
    carry, acc = {}, {}
    for slot, blocks in tiles:
        for hh in range(HEADS_PER_BLOCK):
            st = slot * HEADS_PER_BLOCK + hh
            starts_fresh = blocks[0][1]
            carry[st] = None if starts_fresh else carry_ref[st]
            acc[st] = None if starts_fresh else acc_ref[st]
    for i, (st, hh, n, diag) in enumerate(items):
        w = jnp.exp(log_sig[i] + (tails[i] if diag else tails[i] + carry[st]))
        part = _dot(vt_ref[hh, :, _blk(n)], w.astype(BF16))
        acc[st] = part if acc[st] is None else acc[st] + part
        total = jnp.sum(log_surv[i], axis=0, keepdims=True)
        carry[st] = total if carry[st] is None else carry[st] + total
    worst = []
    for slot, _ in tiles:
        tops = []
        for hh in range(HEADS_PER_BLOCK):
            st = slot * HEADS_PER_BLOCK + hh
            acc_ref[st] = acc[st]
            carry_ref[st] = carry[st]
            tops.append(jnp.max(carry[st]))
        worst.append(functools.reduce(jnp.maximum, tops))
    return tuple(worst)


def _sb_kernel(q_ref, k_ref, v_ref, o_ref, vt_ref, qt_ref, carry_ref, acc_ref, *, n_blocks, scale):
    def transpose_values(n, c):
        v_t = v_ref[_blk(n), :].astype(F32).T.astype(BF16)
        for hh in range(HEADS_PER_BLOCK):
            vt_ref[hh, :, _blk(n)] = v_t[hh * HEAD_DIM:(hh + 1) * HEAD_DIM, :]
        return c

    lax.fori_loop(0, n_blocks, transpose_values, 0, unroll=BUILD_UNROLL)

    assert n_blocks % SB_TILES_PER_PASS == 0
    refs = (k_ref, vt_ref, qt_ref, carry_ref, acc_ref)
    row_h0 = _row_is_head0()

    def tile_group(g, carry):
        tiles = [g * SB_TILES_PER_PASS + slot for slot in range(SB_TILES_PER_PASS)]
        for slot, qi in enumerate(tiles):
            q_t = (q_ref[_blk(qi), :].astype(F32) * scale).T
            for hh in range(HEADS_PER_BLOCK):
                in_head = row_h0 if hh == 0 else jnp.logical_not(row_h0)
                qt_ref[slot * HEADS_PER_BLOCK + hh] = jnp.where(in_head, q_t, 0.0).astype(BF16)

        def blocks_of(qi, with_previous):
            return [(qi, True), (qi - 1, False)] if with_previous else [(qi, True)]

        worst = lax.cond(
            g > 0,
            lambda: _sb_pass([(slot, blocks_of(qi, True)) for slot, qi in enumerate(tiles)], *refs),
            lambda: _sb_pass([(slot, blocks_of(qi, slot > 0)) for slot, qi in enumerate(tiles)], *refs))

        for slot, qi in enumerate(tiles):
            def cond(c):
                j, live = c
                return (j >= 0) & live

            def body(c, slot=slot):
                j, _ = c
                return j - 1, _sb_pass([(slot, [(j, False)])], *refs)[0] > SB_EXIT

            lax.while_loop(cond, body, (qi - 2, worst[slot] > SB_EXIT))
            st = slot * HEADS_PER_BLOCK
            o = jnp.concatenate([acc_ref[st + hh] for hh in range(HEADS_PER_BLOCK)], axis=0)
            o_ref[_blk(qi), :] = o.T.astype(o_ref.dtype)
        return carry

    lax.fori_loop(0, n_blocks // SB_TILES_PER_PASS, tile_group, 0)


def _attn_specs(B, S, q_base, k_base, v_base):
    qspec = pl.BlockSpec((None, S, LANES), lambda b, h: (b, 0, q_base + h))
    kspec = pl.BlockSpec((None, S, LANES), lambda b, h: (b, 0, k_base + h))
    vspec = pl.BlockSpec((None, S, LANES), lambda b, h: (b, 0, v_base + h))
    ospec = pl.BlockSpec((None, S, LANES), lambda b, h: (b, 0, h))
    grid = (B, N_HEAD_BLOCKS)
    params = pltpu.CompilerParams(
        dimension_semantics=("parallel", "parallel"), vmem_limit_bytes=VMEM_LIMIT)
    return grid, qspec, kspec, vspec, ospec, params


_SMEM_SPEC = pl.BlockSpec(memory_space=pltpu.SMEM)


def _moba(qkv, tab):
    B, S, _ = qkv.shape
    grid, qspec, kspec, vspec, ospec, params = _attn_specs(B, S, 0, 3, 6)
    n_blocks = S // MOBA_BLOCK
    nbp = -(-n_blocks // SUBLANES) * SUBLANES
    n_slots = TILES_PER_PASS * HEADS_PER_BLOCK
    return pl.pallas_call(
        functools.partial(_moba_kernel, n_blocks=n_blocks, scale=HEAD_DIM ** -0.5),
        grid=grid,
        in_specs=[_SMEM_SPEC, qspec, kspec, vspec],
        out_specs=ospec,
        out_shape=jax.ShapeDtypeStruct((B, S, MIX_W), BF16),
        scratch_shapes=[
            pltpu.VMEM((HEADS_PER_BLOCK, S, LANES), BF16),
            pltpu.VMEM((HEADS_PER_BLOCK, V_ROWS, S), BF16),
            pltpu.VMEM((1, LANES), F32),
            pltpu.VMEM((nbp, LANES), F32),
            pltpu.VMEM((n_slots, nbp, ATT_T), F32),
            pltpu.VMEM((n_slots, LANES, ATT_T), BF16),
            pltpu.VMEM((n_slots, 1, ATT_T), F32),
            pltpu.VMEM((n_slots, V_ROWS, ATT_T), F32),
            pltpu.VMEM((n_slots * STAGE_DEPTH, ATT_T, ATT_T), F32),
        ],
        compiler_params=params,
        name="moba_attn",
    )(tab, qkv, qkv, qkv)


def _diff(qkv, tab, diff_lambda, subln_g, lambda_init):
    B, S, _ = qkv.shape
    grid, qspec, kspec, vspec, ospec, params = _attn_specs(B, S, 9, 12, 15)
    n_chains = TILES_PER_PASS * 2 * HEADS_PER_BLOCK
    return pl.pallas_call(
        functools.partial(_diff_kernel, n_blocks=S // ATT_T, scale=DIFF_QK_DIM ** -0.5,
                          lambda_init=lambda_init),
        grid=grid,
        in_specs=[
            _SMEM_SPEC,
            pl.BlockSpec(diff_lambda.shape, lambda b, h: (0, 0)),
            pl.BlockSpec((1, LANES), lambda b, h: (0, 0)),
            qspec, kspec, vspec,
        ],
        out_specs=ospec,
        out_shape=jax.ShapeDtypeStruct((B, S, MIX_W), BF16),
        scratch_shapes=[
            pltpu.VMEM((HEADS_PER_BLOCK, S, LANES), BF16),
            pltpu.VMEM((HEADS_PER_BLOCK, V_ROWS, S), BF16),
            pltpu.VMEM((1, LANES), F32),
            pltpu.VMEM((n_chains, LANES, ATT_T), BF16),
            pltpu.VMEM((n_chains, 1, ATT_T), F32),
            pltpu.VMEM((n_chains, V_ROWS, ATT_T), F32),
            pltpu.VMEM((n_chains * STAGE_DEPTH, ATT_T, ATT_T), F32),
        ],
        compiler_params=params,
        name="diff_attn",
    )(tab, diff_lambda, subln_g, qkv, qkv, qkv)


def _sb(qkv):
    B, S, _ = qkv.shape
    grid, qspec, kspec, vspec, ospec, params = _attn_specs(B, S, 18, 21, 24)
    return pl.pallas_call(
        functools.partial(_sb_kernel, n_blocks=S // ATT_T, scale=HEAD_DIM ** -0.5),
        grid=grid,
        in_specs=[qspec, kspec, vspec],
        out_specs=ospec,
        out_shape=jax.ShapeDtypeStruct((B, S, MIX_W), BF16),
        scratch_shapes=[
            pltpu.VMEM((HEADS_PER_BLOCK, HEAD_DIM, S), BF16),
            pltpu.VMEM((SB_TILES_PER_PASS * HEADS_PER_BLOCK, LANES, ATT_T), BF16),
            pltpu.VMEM((SB_TILES_PER_PASS * HEADS_PER_BLOCK, 1, ATT_T), F32),
            pltpu.VMEM((SB_TILES_PER_PASS * HEADS_PER_BLOCK, HEAD_DIM, ATT_T), F32),
        ],
        compiler_params=params,
        name="sb_attn",
    )(qkv, qkv, qkv)


def _merge_ln_kernel(x_ref, om_ref, od_ref, os_ref, wg_ref, bg_ref, wbr_ref, wo_ref, lg_ref, lb_ref,
                     o_ref, *, alpha, d_model):
    x = x_ref[...]
    xb = x.astype(BF16)
    merged = None
    for b, ob_ref in enumerate((om_ref, od_ref, os_ref)):
        glogit = _dot(xb, wg_ref[:, b * d_model:(b + 1) * d_model]) + bg_ref[b:b + 1, :]
        term = jax.nn.sigmoid(glogit) * _dot(ob_ref[...], wbr_ref[b])
        merged = term if merged is None else merged + term
    y = alpha * x + _dot(merged.astype(BF16), wo_ref[...])
    o_ref[...] = _layer_norm(y, lg_ref[...], lb_ref[...])


def _merge_ln(x, o_m, o_d, o_s, w_gate, b_gate, w_br, w_out, lg, lb, l, alpha, tm=512):
    T, D = x.shape
    return pl.pallas_call(
        functools.partial(_merge_ln_kernel, alpha=alpha, d_model=D),
        grid=(T // tm,),
        in_specs=[
            pl.BlockSpec((tm, D), lambda i: (i, 0)),
            pl.BlockSpec((tm, MIX_W), lambda i: (i, 0)),
            pl.BlockSpec((tm, MIX_W), lambda i: (i, 0)),
            pl.BlockSpec((tm, MIX_W), lambda i: (i, 0)),
            pl.BlockSpec((None, D, N_BRANCH * D), lambda i: (l, 0, 0), pipeline_mode=pl.Buffered(1)),
            pl.BlockSpec((None, N_BRANCH, D), lambda i: (l, 0, 0)),
            pl.BlockSpec((None, N_BRANCH, MIX_W, D), lambda i: (l, 0, 0, 0), pipeline_mode=pl.Buffered(1)),
            pl.BlockSpec((None, D, D), lambda i: (l, 0, 0), pipeline_mode=pl.Buffered(1)),
            pl.BlockSpec((1, D), lambda i: (0, 0)),
            pl.BlockSpec((1, D), lambda i: (0, 0)),
        ],
        out_specs=pl.BlockSpec((tm, D), lambda i: (i, 0)),
        out_shape=jax.ShapeDtypeStruct((T, D), F32),
        compiler_params=pltpu.CompilerParams(
            dimension_semantics=("parallel",), vmem_limit_bytes=VMEM_LIMIT),
        name="merge_ln",
    )(x, o_m, o_d, o_s, w_gate, b_gate, w_br, w_out, lg, lb)


def _alibi_slopes(n):
    return (2.0 ** (-8.0 * np.arange(1, n + 1, dtype=np.float32) / n)).astype(np.float32)


def _slope_table(slopes):
    s2 = slopes.astype(np.float64) * LOG2E
    return np.concatenate([s2, 1.0 / (s2 * ATT_T)]).astype(np.float32)


def kernel(x, ln_g, ln_b, ffn_w_gate, ffn_w_up, ffn_w_down, w_in, b_gate, diff_lambda, diff_subln_g,
           w_br_moba, w_br_diff, w_br_sb, w_out):
    B, S, D = x.shape
    depth = ln_g.shape[0]
    assert S % ATT_T == 0 and ATT_T == MOBA_BLOCK
    assert w_in.shape[-1] == QKV_W + N_BRANCH * D
    alpha = (2.0 * depth) ** 0.25

    wg = ffn_w_gate.astype(BF16)
    wu = ffn_w_up.astype(BF16)
    wd = ffn_w_down.astype(BF16)
    w_qkv = w_in[:, :, :QKV_W].astype(BF16)
    w_gate = w_in[:, :, QKV_W:].astype(BF16)
    w_br = jnp.stack([w_br_moba, w_br_diff, w_br_sb], axis=1).astype(BF16)
    w_out_b = w_out.astype(BF16)
    slopes = _alibi_slopes(2 * N_HEADS)
    tab_moba = jnp.asarray(_slope_table(slopes[0::2]))
    tab_diff = jnp.asarray(_slope_table(slopes[1::2]))
    subln_g = jnp.tile(diff_subln_g.astype(F32), (1, HEADS_PER_BLOCK))

    h = x.reshape(B * S, D)
    for l in range(depth):
        lg = ln_g[l][:, None, :]
        lb = ln_b[l][:, None, :]
        h = _ffn_ln(h, wg, wu, wd, lg[0], lb[0], l, 0, alpha)
        qkv = _qkv_proj(h, w_qkv, l).reshape(B, S, QKV_W)
        lambda_init = 0.8 - 0.6 * math.exp(-0.3 * l)
        o_m = _moba(qkv, tab_moba)
        o_d = _diff(qkv, tab_diff, diff_lambda[l], subln_g[l][None, :], lambda_init)
        o_s = _sb(qkv)
        T = B * S
        h = _merge_ln(h, o_m.reshape(T, MIX_W), o_d.reshape(T, MIX_W), o_s.reshape(T, MIX_W),
                      w_gate, b_gate, w_br, w_out_b, lg[1], lb[1], l, alpha)
        h = _ffn_ln(h, wg, wu, wd, lg[2], lb[2], l, 1, alpha)
    return h.reshape(B, S, D)
```

```python
import functools
import math

import numpy as np
import jax
import jax.numpy as jnp
from jax import lax
from jax.experimental import pallas as pl
from jax.experimental.pallas import tpu as pltpu

F32 = jnp.float32
BF16 = jnp.bfloat16

HEAD_DIM = 64
N_HEADS = 6
DIFF_QK_DIM = HEAD_DIM // 2
MOBA_BLOCK = 256
MOBA_TOPK = 3
N_BRANCH = 3
LN_EPS = 1e-5
SUBLN_EPS = 1e-5

LANES = 128
SUBLANES = 8
DENOM_ROWS = 16
V_ROWS = HEAD_DIM + DENOM_ROWS
HEADS_PER_BLOCK = LANES // HEAD_DIM
N_HEAD_BLOCKS = N_HEADS // HEADS_PER_BLOCK
MIX_W = N_HEADS * HEAD_DIM
QKV_W = 9 * MIX_W
ATT_T = 256
NEG = -1e30
SB_EXIT = -110.0
TILES_PER_PASS = 8
SB_TILES_PER_PASS = 4
LOG2E = math.log2(math.e)
N_BIAS_COLS = 3
BOUND_SLACK = 1.01
MAX_EXPONENT = 100.0
SKIP_BITS = 150.0
ITEMS_PER_TRIP = 64
LONE_HEAD_BLOCKS_PER_TRIP = 4
BUILD_UNROLL = 4
STAGE_DEPTH = 2
VMEM_LIMIT = 56 * 1024 * 1024


def _dot(a, b, precision=None):
    return jnp.dot(a, b, precision=precision, preferred_element_type=F32)


def _layer_norm(y, g, b):
    mu = jnp.mean(y, axis=-1, keepdims=True)
    yc = y - mu
    var = jnp.mean(yc * yc, axis=-1, keepdims=True)
    return yc * lax.rsqrt(var + LN_EPS) * g + b


def _ffn_ln_kernel(x_ref, wg_ref, wu_ref, wd_ref, lg_ref, lb_ref, o_ref, hid_ref, *, alpha, tf):
    x = x_ref[...]
    xb = x.astype(BF16)
    for c in range(hid_ref.shape[1] // tf):
        cols = slice(c * tf, (c + 1) * tf)
        gate = _dot(xb, wg_ref[:, cols])
        up = _dot(xb, wu_ref[:, cols])
        hid_ref[:, cols] = (gate * jax.nn.sigmoid(gate) * up).astype(BF16)
    y = alpha * x + 0.5 * _dot(hid_ref[...], wd_ref[...])
    o_ref[...] = _layer_norm(y, lg_ref[...], lb_ref[...])


def _ffn_ln(x, wg, wu, wd, lg, lb, l, j, alpha, tm=512, tf=256):
    T, D = x.shape
    FF = wg.shape[-1]
    resident = pl.Buffered(1)
    return pl.pallas_call(
        functools.partial(_ffn_ln_kernel, alpha=alpha, tf=tf),
        grid=(T // tm,),
        in_specs=[
            pl.BlockSpec((tm, D), lambda i: (i, 0)),
            pl.BlockSpec((None, None, D, FF), lambda i: (l, j, 0, 0), pipeline_mode=resident),
            pl.BlockSpec((None, None, D, FF), lambda i: (l, j, 0, 0), pipeline_mode=resident),
            pl.BlockSpec((None, None, FF, D), lambda i: (l, j, 0, 0), pipeline_mode=resident),
            pl.BlockSpec((1, D), lambda i: (0, 0)),
            pl.BlockSpec((1, D), lambda i: (0, 0)),
        ],
        out_specs=pl.BlockSpec((tm, D), lambda i: (i, 0)),
        out_shape=jax.ShapeDtypeStruct((T, D), F32),
        scratch_shapes=[pltpu.VMEM((tm, FF), BF16)],
        compiler_params=pltpu.CompilerParams(
            dimension_semantics=("parallel",), vmem_limit_bytes=VMEM_LIMIT),
        name="ffn_ln",
    )(x, wg, wu, wd, lg, lb)


def _proj_kernel(x_ref, w_ref, o_ref, *, tn):
    xb = x_ref[...].astype(BF16)
    for c in range(o_ref.shape[1] // tn):
        cols = slice(c * tn, (c + 1) * tn)
        o_ref[:, cols] = _dot(xb, w_ref[:, cols]).astype(o_ref.dtype)


def _qkv_proj(x, w_in, l, tm=512, tn=1152):
    T, D = x.shape
    return pl.pallas_call(
        functools.partial(_proj_kernel, tn=tn),
        grid=(T // tm,),
        in_specs=[
            pl.BlockSpec((tm, D), lambda i: (i, 0)),
            pl.BlockSpec((None, D, QKV_W), lambda i: (l, 0, 0), pipeline_mode=pl.Buffered(1)),
        ],
        out_specs=pl.BlockSpec((tm, QKV_W), lambda i: (i, 0)),
        out_shape=jax.ShapeDtypeStruct((T, QKV_W), BF16),
        compiler_params=pltpu.CompilerParams(
            dimension_semantics=("parallel",), vmem_limit_bytes=VMEM_LIMIT),
        name="qkv_proj",
    )(x, w_in)


def _lane_iota():
    return lax.broadcasted_iota(jnp.int32, (1, LANES), 1)


def _head_lanes(hh):
    lane = _lane_iota()
    return (lane >= hh * HEAD_DIM) & (lane < (hh + 1) * HEAD_DIM)


def _blk(j):
    return pl.ds(pl.multiple_of(j * ATT_T, ATT_T), ATT_T)


def _key_query_iotas():
    krow = lax.broadcasted_iota(jnp.int32, (ATT_T, ATT_T), 0)
    qcol = lax.broadcasted_iota(jnp.int32, (ATT_T, ATT_T), 1)
    return krow, qcol


def _build_softmax_operands(tab_ref, hb, k_ref, v_ref, kp_ref, vt_ref, kmax_ref, *, n_blocks, scale):
    lane = _lane_iota()
    pos = lax.broadcasted_iota(jnp.int32, (ATT_T, LANES), 0).astype(F32)
    bias_cols = []
    for hh in range(HEADS_PER_BLOCK):
        b = pos * tab_ref[hb * HEADS_PER_BLOCK + hh]
        lo = (1 - hh) * HEAD_DIM
        cols = jnp.zeros_like(b)
        for i in range(N_BIAS_COLS):
            piece = b.astype(BF16).astype(F32)
            cols = jnp.where(lane == lo + i, piece, cols)
            b = b - piece
        bias_cols.append(cols)
    kmax_ref[...] = jnp.zeros_like(kmax_ref)
    assert HEADS_PER_BLOCK == 2
    head_r = jnp.where(lax.broadcasted_iota(jnp.int32, (LANES, LANES), 0) < HEAD_DIM, 0, 1)
    head_c = jnp.where(lax.broadcasted_iota(jnp.int32, (LANES, LANES), 1) < HEAD_DIM, 0, 1)
    same_head = jnp.where(head_r == head_c, 1.0, 0.0).astype(BF16)

    def body(n, c):
        kf = k_ref[_blk(n), :].astype(F32) * (scale * LOG2E)
        v_t = v_ref[_blk(n), :].astype(F32).T.astype(BF16)
        for hh in range(HEADS_PER_BLOCK):
            kp_ref[hh, _blk(n), :] = jnp.where(_head_lanes(hh), kf, bias_cols[hh]).astype(BF16)
            vt_ref[hh, 0:HEAD_DIM, _blk(n)] = v_t[hh * HEAD_DIM:(hh + 1) * HEAD_DIM, :]
            vt_ref[hh, HEAD_DIM:V_ROWS, _blk(n)] = jnp.ones((DENOM_ROWS, ATT_T), BF16)
        norm2 = _dot((kf * kf).astype(BF16), same_head)
        kmax_ref[...] = jnp.maximum(kmax_ref[...], jnp.max(norm2, axis=0, keepdims=True))
        return c

    lax.fori_loop(0, n_blocks, body, 0, unroll=BUILD_UNROLL)


def _chain_qt(c, hh, seg_lo, seg_width, q_t, qt_ref):
    row = lax.broadcasted_iota(jnp.int32, (LANES, ATT_T), 0)
    lo = (1 - hh) * HEAD_DIM
    qsel = jnp.where((row >= seg_lo) & (row < seg_lo + seg_width), q_t, 0.0)
    qt_ref[c] = jnp.where((row >= lo) & (row < lo + N_BIAS_COLS), 1.0, qsel).astype(BF16)
    return qsel, jnp.sum(qsel * qsel, axis=0, keepdims=True)


def _stage_scores(c, hh, n, slot, kp_ref, qt_ref, sbuf_ref):
    sbuf_ref[c * STAGE_DEPTH + slot] = _dot(kp_ref[hh, _blk(jnp.maximum(n, 0)), :], qt_ref[c])


def _diag_blocks(chains, qi_of, qn2, slope2, kp_ref, vt_ref, kmax_ref, qt_ref, ref_ref, acc_ref, sbuf_ref):
    krow, qcol = _key_query_iotas()
    causal = krow <= qcol
    r = lax.broadcasted_iota(jnp.int32, (1, ATT_T), 1).astype(F32)
    scores = [_dot(kp_ref[hh, _blk(qi_of[c]), :], qt_ref[c]) for c, hh in enumerate(chains)]
    for slot in range(STAGE_DEPTH):
        for c, hh in enumerate(chains):
            _stage_scores(c, hh, qi_of[c] - 1 - slot, slot, kp_ref, qt_ref, sbuf_ref)
    gaps = []
    for c, hh in enumerate(chains):
        s = jnp.where(causal, scores[c], NEG)
        m = jnp.max(s, axis=0, keepdims=True)
        kmax2 = kmax_ref[:, hh * HEAD_DIM:hh * HEAD_DIM + 1]
        bound = jnp.sqrt(qn2[c] * kmax2) * BOUND_SLACK + slope2[hh] * r
        ref_ref[c] = m
        acc_ref[c] = _dot(vt_ref[hh, :, _blk(qi_of[c])], jnp.exp2(s - m).astype(BF16))
        gaps.append(jnp.max(bound - m))
    return gaps


def _first_block(gap, inv_blk, qi):
    reach = jnp.minimum((gap + SKIP_BITS) * inv_blk, 1e6).astype(jnp.int32)
    return jnp.maximum(qi - 1 - reach, 0)


def _fixed_ref_step(c, hh, n, slot, off, kp_ref, vt_ref, qt_ref, ref_ref, acc_ref, sbuf_ref):
    p = jnp.exp2(sbuf_ref[c * STAGE_DEPTH + slot] - (ref_ref[c] - off)).astype(BF16)
    _stage_scores(c, hh, n - STAGE_DEPTH, slot, kp_ref, qt_ref, sbuf_ref)
    acc_ref[c] += _dot(vt_ref[hh, :, _blk(jnp.maximum(n, 0))], p)


def _online_step(c, hh, n, off, kp_ref, vt_ref, qt_ref, m_ref, acc_ref):
    s = _dot(kp_ref[hh, _blk(n), :], qt_ref[c])
    m_old = m_ref[c]
    m_new = jnp.maximum(m_old, jnp.max(s, axis=0, keepdims=True) + off)
    p = jnp.exp2(s - (m_new - off)).astype(BF16)
    acc_ref[c] = acc_ref[c] * jnp.exp2(m_old - m_new) + _dot(vt_ref[hh, :, _blk(n)], p)
    m_ref[c] = m_new


def _softmax_sweep(qi_of, chains, gaps, depth, off_fn, kp_ref, vt_ref, qt_ref, ref_ref, acc_ref, sbuf_ref,
                   split_heads):
    refs = (kp_ref, vt_ref, qt_ref, ref_ref, acc_ref)
    assert STAGE_DEPTH == 2
    fixed_ok = functools.reduce(jnp.maximum, gaps) <= MAX_EXPONENT

    chains_per_tile = len(chains) // TILES_PER_PASS

    def run_fixed(start, count, heads):
        active = [(c, hh) for c, hh in enumerate(chains) if hh in heads]
        per_trip = max(STAGE_DEPTH, ITEMS_PER_TRIP // len(active))
        if len(heads) == 1:
            per_trip = min(per_trip, LONE_HEAD_BLOCKS_PER_TRIP)
        assert per_trip % STAGE_DEPTH == 0

        def step(below, slot, who):
            for c, hh in who:
                n = qi_of[c] - 1 - below
                off = off_fn(c, hh, jnp.maximum(n, 0)) + jnp.where(n < 0, NEG, 0.0)
                _fixed_ref_step(c, hh, n, slot, off, *refs, sbuf_ref)

        def sweep(start, trips, per_trip):
            def body(i, carry):
                for j in range(per_trip):
                    step(start + i * per_trip + j, j % STAGE_DEPTH, active)
                return carry

            lax.fori_loop(0, trips, body, 0)

        common = jnp.clip(qi_of[0] - start, 0, count) // STAGE_DEPTH * STAGE_DEPTH
        whole = common // per_trip
        sweep(start, whole, per_trip)
        if per_trip > STAGE_DEPTH:
            sweep(start + whole * per_trip, (common - whole * per_trip) // STAGE_DEPTH, STAGE_DEPTH)
        for k in range(TILES_PER_PASS):
            who = [(c, hh) for c, hh in active if c // chains_per_tile >= k]

            @pl.when(k < count - common)
            def _one_depth(k=k, who=who):
                step(start + common + k, k % STAGE_DEPTH, who)

    @pl.when(fixed_ok)
    def _fixed_reference():
        far = jnp.maximum(depth[0], depth[1])
        if split_heads:
            near = jnp.minimum(depth[0], depth[1])
            together = jnp.minimum((near + 1) // 2 * 2, far)
            run_fixed(0, together, (0, 1))
            for hh in range(HEADS_PER_BLOCK):
                run_fixed(together, jnp.maximum(depth[hh] - together, 0), (hh,))
        else:
            run_fixed(0, far, (0, 1))

    @pl.when(jnp.logical_not(fixed_ok))
    def _running_maximum():
        chains_per_head = chains_per_tile // HEADS_PER_BLOCK
        assert chains[:chains_per_tile] == [c // chains_per_head for c in range(chains_per_tile)]

        def chain_body(c, carry):
            hh = lax.div(lax.rem(c, chains_per_tile), chains_per_head)
            qi = qi_of[0] + lax.div(c, chains_per_tile)

            def body(n, carry):
                _online_step(c, hh, n, off_fn(c, hh, n, qi), *refs)
                return carry

            return lax.fori_loop(0, qi, body, carry)

        lax.fori_loop(0, len(chains), chain_body, 0)


def _normalized(acc):
    return acc[0:HEAD_DIM, :] * (1.0 / acc[HEAD_DIM:HEAD_DIM + 1, :])


def _row_is_head0():
    return lax.broadcasted_iota(jnp.int32, (LANES, ATT_T), 0) < HEAD_DIM


def _moba_kernel(tab_ref, q_ref, k_ref, v_ref, o_ref, kp_ref, vt_ref, kmax_ref, kmean_ref, selm_ref,
                 qt_ref, ref_ref, acc_ref, sbuf_ref, *, n_blocks, scale):
    hb = pl.program_id(1)
    nbp = kmean_ref.shape[0]
    _build_softmax_operands(tab_ref, hb, k_ref, v_ref, kp_ref, vt_ref, kmax_ref, n_blocks=n_blocks, scale=scale)
    kmean_ref[...] = jnp.zeros_like(kmean_ref)

    def block_mean(n, c):
        kmean_ref[pl.ds(n, 1), :] = jnp.mean(k_ref[_blk(n), :].astype(F32), axis=0, keepdims=True)
        return c

    lax.fori_loop(0, n_blocks, block_mean, 0, unroll=BUILD_UNROLL)

    blk_id = lax.broadcasted_iota(jnp.int32, (nbp, ATT_T), 0)
    blk_f = blk_id.astype(F32)
    chains = list(range(HEADS_PER_BLOCK))
    n_ch = len(chains)
    slope2 = [tab_ref[hb * HEADS_PER_BLOCK + hh] for hh in chains]
    assert n_blocks % TILES_PER_PASS == 0

    def tile_group(g, carry):
        tiles = [g * TILES_PER_PASS + s for s in range(TILES_PER_PASS)]
        qn2 = []
        for s, qi in enumerate(tiles):
            q_t = q_ref[_blk(qi), :].astype(F32).T
            for hh in chains:
                hm = _head_lanes(hh)
                qsel, norm2 = _chain_qt(s * n_ch + hh, hh, hh * HEAD_DIM, HEAD_DIM, q_t, qt_ref)
                qn2.append(norm2)

                gate = _dot(jnp.where(hm, kmean_ref[...], 0.0), qsel, precision=lax.Precision.HIGHEST)
                gt = jnp.where(blk_id < qi, gate, -jnp.inf)
                sel = jnp.zeros(gt.shape, dtype=jnp.bool_)
                for _ in range(MOBA_TOPK):
                    mx = jnp.max(gt, axis=0, keepdims=True)
                    pick = jnp.min(jnp.where(gt == mx, blk_f, float(nbp)), axis=0, keepdims=True)
                    hit = blk_f == pick
                    sel = sel | (hit & (mx > -jnp.inf))
                    gt = jnp.where(hit, -jnp.inf, gt)
                selm_ref[s * n_ch + hh] = jnp.where(sel, 0.0, NEG)

        qi_of = [tiles[c // n_ch] for c in range(TILES_PER_PASS * n_ch)]
        gaps = _diag_blocks(chains * TILES_PER_PASS, qi_of, qn2, slope2, kp_ref, vt_ref, kmax_ref, qt_ref,
                            ref_ref, acc_ref, sbuf_ref)
        all_chains = chains * TILES_PER_PASS
        depth = [None] * HEADS_PER_BLOCK
        for c, hh in enumerate(all_chains):
            d = qi_of[c] - _first_block(gaps[c], tab_ref[N_HEADS + hb * HEADS_PER_BLOCK + hh], qi_of[c])
            depth[hh] = d if depth[hh] is None else jnp.maximum(depth[hh], d)

        def off_fn(c, hh, n, qi=None):
            qi = qi_of[c] if qi is None else qi
            slope = tab_ref[hb * HEADS_PER_BLOCK + hh]
            return slope * jnp.asarray((n - qi) * ATT_T, F32) + selm_ref[c, pl.ds(n, 1), :]

        _softmax_sweep(qi_of, all_chains, gaps, depth, off_fn, kp_ref, vt_ref, qt_ref, ref_ref, acc_ref,
                       sbuf_ref, split_heads=False)
        for s, qi in enumerate(tiles):
            o = jnp.concatenate([_normalized(acc_ref[s * n_ch + hh]) for hh in chains], axis=0)
            o_ref[_blk(qi), :] = o.T.astype(o_ref.dtype)
        return carry

    lax.fori_loop(0, n_blocks // TILES_PER_PASS, tile_group, 0)


def _diff_kernel(tab_ref, lam_ref, g_ref, q_ref, k_ref, v_ref, o_ref, kp_ref, vt_ref, kmax_ref, qt_ref,
                 ref_ref, acc_ref, sbuf_ref, *, n_blocks, scale, lambda_init):
    hb = pl.program_id(1)
    _build_softmax_operands(tab_ref, hb, k_ref, v_ref, kp_ref, vt_ref, kmax_ref, n_blocks=n_blocks, scale=scale)

    lf = lam_ref[...].astype(F32)
    lam = (jnp.exp(jnp.sum(lf[0:1] * lf[1:2], axis=-1, keepdims=True))
           - jnp.exp(jnp.sum(lf[2:3] * lf[3:4], axis=-1, keepdims=True)) + lambda_init)

    chains = [hh for hh in range(HEADS_PER_BLOCK) for _ in range(2)]
    n_ch = len(chains)
    slope2 = [tab_ref[hb * HEADS_PER_BLOCK + hh] for hh in range(HEADS_PER_BLOCK)]
    assert n_blocks % TILES_PER_PASS == 0

    def tile_group(g, carry):
        tiles = [g * TILES_PER_PASS + s for s in range(TILES_PER_PASS)]
        qn2 = []
        for s, qi in enumerate(tiles):
            q_t = q_ref[_blk(qi), :].astype(F32).T
            for c, hh in enumerate(chains):
                seg_lo = hh * HEAD_DIM + (c % 2) * DIFF_QK_DIM
                qn2.append(_chain_qt(s * n_ch + c, hh, seg_lo, DIFF_QK_DIM, q_t, qt_ref)[1])
        qi_of = [tiles[c // n_ch] for c in range(TILES_PER_PASS * n_ch)]
        gaps = _diag_blocks(chains * TILES_PER_PASS, qi_of, qn2, slope2, kp_ref, vt_ref, kmax_ref, qt_ref,
                            ref_ref, acc_ref, sbuf_ref)
        all_chains = chains * TILES_PER_PASS
        depth = [None] * HEADS_PER_BLOCK
        for c, hh in enumerate(all_chains):
            d = qi_of[c] - _first_block(gaps[c], tab_ref[N_HEADS + hb * HEADS_PER_BLOCK + hh], qi_of[c])
            depth[hh] = d if depth[hh] is None else jnp.maximum(depth[hh], d)

        def off_fn(c, hh, n, qi=None):
            qi = qi_of[c] if qi is None else qi
            return tab_ref[hb * HEADS_PER_BLOCK + hh] * jnp.asarray((n - qi) * ATT_T, F32)

        _softmax_sweep(qi_of, all_chains, gaps, depth, off_fn, kp_ref, vt_ref, qt_ref, ref_ref, acc_ref,
                       sbuf_ref, split_heads=True)
        for s, qi in enumerate(tiles):
            base = s * n_ch
            outs = []
            for hh in range(HEADS_PER_BLOCK):
                o = (_normalized(acc_ref[base + 2 * hh])
                     - lam * _normalized(acc_ref[base + 2 * hh + 1]))
                ms = jnp.mean(o * o, axis=0, keepdims=True)
                outs.append(o * lax.rsqrt(ms + SUBLN_EPS))
            o = jnp.concatenate(outs, axis=0).T * (g_ref[...] * (1.0 - lambda_init))
            o_ref[_blk(qi), :] = o.astype(o_ref.dtype)
        return carry

    lax.fori_loop(0, n_blocks // TILES_PER_PASS, tile_group, 0)


def _log_sigmoids(z):
    lneg = jnp.minimum(-z, 0.0) - jnp.log(1.0 + jnp.exp(-jnp.abs(z)))
    return lneg, lneg + z


def _suffix_sums(upper, lg):
    hi = lg.astype(BF16)
    lo = (lg - hi.astype(F32)).astype(BF16)
    return _dot(upper, hi) + _dot(upper, lo)


def _sb_pass(tiles, k_ref, vt_ref, qt_ref, carry_ref, acc_ref):
    krow, qcol = _key_query_iotas()
    before = krow < qcol
    upper = (qcol > krow).astype(BF16)
    items = [(slot * HEADS_PER_BLOCK + hh, hh, n, diag)
             for slot, blocks in tiles for n, diag in blocks for hh in range(HEADS_PER_BLOCK)]
    z = [_dot(k_ref[_blk(n), :], qt_ref[st]) for st, _, n, _ in items]
    log_sig, log_surv, tails = [], [], []
    for i, (st, hh, n, diag) in enumerate(items):
        lneg, lpos = _log_sigmoids(jnp.where(before, z[i], NEG) if diag else z[i])
        log_surv.append(lneg)
        log_sig.append(lpos)
        tails.append(_suffix_sums(upper, lneg))
    carry, acc = {}, {}
    for slot, blocks in tiles:
        for hh in range(HEADS_PER_BLOCK):
            st = slot * HEADS_PER_BLOCK + hh
            starts_fresh = blocks[0][1]
            carry[st] = None if starts_fresh else carry_ref[st]
            acc[st] = None if starts_fresh else acc_ref[st]
    for i, (st, hh, n, diag) in enumerate(items):
        w = jnp.exp(log_sig[i] + (tails[i] if diag else tails[i] + carry[st]))
        part = _dot(vt_ref[hh, :, _blk(n)], w.astype(BF16))
        acc[st] = part if acc[st] is None else acc[st] + part
        total = jnp.sum(log_surv[i], axis=0, keepdims=True)
        carry[st] = total if carry[st] is None else carry[st] + total
    worst = []
    for slot, _ in tiles:
        tops = []
        for hh in range(HEADS_PER_BLOCK):
            st = slot * HEADS_PER_BLOCK + hh
            acc_ref[st] = acc[st]
            carry_ref[st] = carry[st]
            tops.append(jnp.max(carry[st]))
        worst.append(functools.reduce(jnp.maximum, tops))
    return tuple(worst)


def _sb_kernel(q_ref, k_ref, v_ref, o_ref, vt_ref, qt_ref, carry_ref, acc_ref, *, n_blocks, scale):
    def transpose_values(n, c):
        v_t = v_ref[_blk(n), :].astype(F32).T.astype(BF16)
        for hh in range(HEADS_PER_BLOCK):
            vt_ref[hh, :, _blk(n)] = v_t[hh * HEAD_DIM:(hh + 1) * HEAD_DIM, :]
        return c

    lax.fori_loop(0, n_blocks, transpose_values, 0, unroll=BUILD_UNROLL)

    assert n_blocks % SB_TILES_PER_PASS == 0
    refs = (k_ref, vt_ref, qt_ref, carry_ref, acc_ref)
    row_h0 = _row_is_head0()

    def tile_group(g, carry):
        tiles = [g * SB_TILES_PER_PASS + slot for slot in range(SB_TILES_PER_PASS)]
        for slot, qi in enumerate(tiles):
            q_t = (q_ref[_blk(qi), :].astype(F32) * scale).T
            for hh in range(HEADS_PER_BLOCK):
                in_head = row_h0 if hh == 0 else jnp.logical_not(row_h0)
                qt_ref[slot * HEADS_PER_BLOCK + hh] = jnp.where(in_head, q_t, 0.0).astype(BF16)

        def blocks_of(qi, with_previous):
            return [(qi, True), (qi - 1, False)] if with_previous else [(qi, True)]

        worst = lax.cond(
            g > 0,
            lambda: _sb_pass([(slot, blocks_of(qi, True)) for slot, qi in enumerate(tiles)], *refs),
            lambda: _sb_pass([(slot, blocks_of(qi, slot > 0)) for slot, qi in enumerate(tiles)], *refs))

        for slot, qi in enumerate(tiles):
            def cond(c):
                j, live = c
                return (j >= 0) & live

            def body(c, slot=slot):
                j, _ = c
                return j - 1, _sb_pass([(slot, [(j, False)])], *refs)[0] > SB_EXIT

            lax.while_loop(cond, body, (qi - 2, worst[slot] > SB_EXIT))
            st = slot * HEADS_PER_BLOCK
            o = jnp.concatenate([acc_ref[st + hh] for hh in range(HEADS_PER_BLOCK)], axis=0)
            o_ref[_blk(qi), :] = o.T.astype(o_ref.dtype)
        return carry

    lax.fori_loop(0, n_blocks // SB_TILES_PER_PASS, tile_group, 0)


def _attn_specs(B, S, q_base, k_base, v_base):
    qspec = pl.BlockSpec((None, S, LANES), lambda b, h: (b, 0, q_base + h))
    kspec = pl.BlockSpec((None, S, LANES), lambda b, h: (b, 0, k_base + h))
    vspec = pl.BlockSpec((None, S, LANES), lambda b, h: (b, 0, v_base + h))
    ospec = pl.BlockSpec((None, S, LANES), lambda b, h: (b, 0, h))
    grid = (B, N_HEAD_BLOCKS)
    params = pltpu.CompilerParams(
        dimension_semantics=("parallel", "parallel"), vmem_limit_bytes=VMEM_LIMIT)
    return grid, qspec, kspec, vspec, ospec, params


_SMEM_SPEC = pl.BlockSpec(memory_space=pltpu.SMEM)


def _moba(qkv, tab):
    B, S, _ = qkv.shape
    grid, qspec, kspec, vspec, ospec, params = _attn_specs(B, S, 0, 3, 6)
    n_blocks = S // MOBA_BLOCK
    nbp = -(-n_blocks // SUBLANES) * SUBLANES
    n_slots = TILES_PER_PASS * HEADS_PER_BLOCK
    return pl.pallas_call(
        functools.partial(_moba_kernel, n_blocks=n_blocks, scale=HEAD_DIM ** -0.5),
        grid=grid,
        in_specs=[_SMEM_SPEC, qspec, kspec, vspec],
        out_specs=ospec,
        out_shape=jax.ShapeDtypeStruct((B, S, MIX_W), BF16),
        scratch_shapes=[
            pltpu.VMEM((HEADS_PER_BLOCK, S, LANES), BF16),
            pltpu.VMEM((HEADS_PER_BLOCK, V_ROWS, S), BF16),
            pltpu.VMEM((1, LANES), F32),
            pltpu.VMEM((nbp, LANES), F32),
            pltpu.VMEM((n_slots, nbp, ATT_T), F32),
            pltpu.VMEM((n_slots, LANES, ATT_T), BF16),
            pltpu.VMEM((n_slots, 1, ATT_T), F32),
            pltpu.VMEM((n_slots, V_ROWS, ATT_T), F32),
            pltpu.VMEM((n_slots * STAGE_DEPTH, ATT_T, ATT_T), F32),
        ],
        compiler_params=params,
        name="moba_attn",
    )(tab, qkv, qkv, qkv)


def _diff(qkv, tab, diff_lambda, subln_g, lambda_init):
    B, S, _ = qkv.shape
    grid, qspec, kspec, vspec, ospec, params = _attn_specs(B, S, 9, 12, 15)
    n_chains = TILES_PER_PASS * 2 * HEADS_PER_BLOCK
    return pl.pallas_call(
        functools.partial(_diff_kernel, n_blocks=S // ATT_T, scale=DIFF_QK_DIM ** -0.5,
                          lambda_init=lambda_init),
        grid=grid,
        in_specs=[
            _SMEM_SPEC,
            pl.BlockSpec(diff_lambda.shape, lambda b, h: (0, 0)),
            pl.BlockSpec((1, LANES), lambda b, h: (0, 0)),
            qspec, kspec, vspec,
        ],
        out_specs=ospec,
        out_shape=jax.ShapeDtypeStruct((B, S, MIX_W), BF16),
        scratch_shapes=[
            pltpu.VMEM((HEADS_PER_BLOCK, S, LANES), BF16),
            pltpu.VMEM((HEADS_PER_BLOCK, V_ROWS, S), BF16),
            pltpu.VMEM((1, LANES), F32),
            pltpu.VMEM((n_chains, LANES, ATT_T), BF16),
            pltpu.VMEM((n_chains, 1, ATT_T), F32),
            pltpu.VMEM((n_chains, V_ROWS, ATT_T), F32),
            pltpu.VMEM((n_chains * STAGE_DEPTH, ATT_T, ATT_T), F32),
        ],
        compiler_params=params,
        name="diff_attn",
    )(tab, diff_lambda, subln_g, qkv, qkv, qkv)


def _sb(qkv):
    B, S, _ = qkv.shape
    grid, qspec, kspec, vspec, ospec, params = _attn_specs(B, S, 18, 21, 24)
    return pl.pallas_call(
        functools.partial(_sb_kernel, n_blocks=S // ATT_T, scale=HEAD_DIM ** -0.5),
        grid=grid,
        in_specs=[qspec, kspec, vspec],
        out_specs=ospec,
        out_shape=jax.ShapeDtypeStruct((B, S, MIX_W), BF16),
        scratch_shapes=[
            pltpu.VMEM((HEADS_PER_BLOCK, HEAD_DIM, S), BF16),
            pltpu.VMEM((SB_TILES_PER_PASS * HEADS_PER_BLOCK, LANES, ATT_T), BF16),
            pltpu.VMEM((SB_TILES_PER_PASS * HEADS_PER_BLOCK, 1, ATT_T), F32),
            pltpu.VMEM((SB_TILES_PER_PASS * HEADS_PER_BLOCK, HEAD_DIM, ATT_T), F32),
        ],
        compiler_params=params,
        name="sb_attn",
    )(qkv, qkv, qkv)


def _merge_ln_kernel(x_ref, om_ref, od_ref, os_ref, wg_ref, bg_ref, wbr_ref, wo_ref, lg_ref, lb_ref,
                     o_ref, *, alpha, d_model):
    x = x_ref[...]
    xb = x.astype(BF16)
    merged = None
    for b, ob_ref in enumerate((om_ref, od_ref, os_ref)):
        glogit = _dot(xb, wg_ref[:, b * d_model:(b + 1) * d_model]) + bg_ref[b:b + 1, :]
        term = jax.nn.sigmoid(glogit) * _dot(ob_ref[...], wbr_ref[b])
        merged = term if merged is None else merged + term
    y = alpha * x + _dot(merged.astype(BF16), wo_ref[...])
    o_ref[...] = _layer_norm(y, lg_ref[...], lb_ref[...])


def _merge_ln(x, o_m, o_d, o_s, w_gate, b_gate, w_br, w_out, lg, lb, l, alpha, tm=512):
    T, D = x.shape
    return pl.pallas_call(
        functools.partial(_merge_ln_kernel, alpha=alpha, d_model=D),
        grid=(T // tm,),
        in_specs=[
            pl.BlockSpec((tm, D), lambda i: (i, 0)),
            pl.BlockSpec((tm, MIX_W), lambda i: (i, 0)),
            pl.BlockSpec((tm, MIX_W), lambda i: (i, 0)),
            pl.BlockSpec((tm, MIX_W), lambda i: (i, 0)),
            pl.BlockSpec((None, D, N_BRANCH * D), lambda i: (l, 0, 0), pipeline_mode=pl.Buffered(1)),
            pl.BlockSpec((None, N_BRANCH, D), lambda i: (l, 0, 0)),
            pl.BlockSpec((None, N_BRANCH, MIX_W, D), lambda i: (l, 0, 0, 0), pipeline_mode=pl.Buffered(1)),
            pl.BlockSpec((None, D, D), lambda i: (l, 0, 0), pipeline_mode=pl.Buffered(1)),
            pl.BlockSpec((1, D), lambda i: (0, 0)),
            pl.BlockSpec((1, D), lambda i: (0, 0)),
        ],
        out_specs=pl.BlockSpec((tm, D), lambda i: (i, 0)),
        out_shape=jax.ShapeDtypeStruct((T, D), F32),
        compiler_params=pltpu.CompilerParams(
            dimension_semantics=("parallel",), vmem_limit_bytes=VMEM_LIMIT),
        name="merge_ln",
    )(x, o_m, o_d, o_s, w_gate, b_gate, w_br, w_out, lg, lb)


def _alibi_slopes(n):
    return (2.0 ** (-8.0 * np.arange(1, n + 1, dtype=np.float32) / n)).astype(np.float32)


def _slope_table(slopes):
    s2 = slopes.astype(np.float64) * LOG2E
    return np.concatenate([s2, 1.0 / (s2 * ATT_T)]).astype(np.float32)


def kernel(x, ln_g, ln_b, ffn_w_gate, ffn_w_up, ffn_w_down, w_in, b_gate, diff_lambda, diff_subln_g,
           w_br_moba, w_br_diff, w_br_sb, w_out):
    B, S, D = x.shape
    depth = ln_g.shape[0]
    assert S % ATT_T == 0 and ATT_T == MOBA_BLOCK
    assert w_in.shape[-1] == QKV_W + N_BRANCH * D
    alpha = (2.0 * depth) ** 0.25

    wg = ffn_w_gate.astype(BF16)
    wu = ffn_w_up.astype(BF16)
    wd = ffn_w_down.astype(BF16)
    w_qkv = w_in[:, :, :QKV_W].astype(BF16)
    w_gate = w_in[:, :, QKV_W:].astype(BF16)
    w_br = jnp.stack([w_br_moba, w_br_diff, w_br_sb], axis=1).astype(BF16)
    w_out_b = w_out.astype(BF16)
    slopes = _alibi_slopes(2 * N_HEADS)
    tab_moba = jnp.asarray(_slope_table(slopes[0::2]))
    tab_diff = jnp.asarray(_slope_table(slopes[1::2]))
    subln_g = jnp.tile(diff_subln_g.astype(F32), (1, HEADS_PER_BLOCK))

    h = x.reshape(B * S, D)
    for l in range(depth):
        lg = ln_g[l][:, None, :]
        lb = ln_b[l][:, None, :]
        h = _ffn_ln(h, wg, wu, wd, lg[0], lb[0], l, 0, alpha)
        qkv = _qkv_proj(h, w_qkv, l).reshape(B, S, QKV_W)
        lambda_init = 0.8 - 0.6 * math.exp(-0.3 * l)
        o_m = _moba(qkv, tab_moba)
        o_d = _diff(qkv, tab_diff, diff_lambda[l], subln_g[l][None, :], lambda_init)
        o_s = _sb(qkv)
        T = B * S
        h = _merge_ln(h, o_m.reshape(T, MIX_W), o_d.reshape(T, MIX_W), o_s.reshape(T, MIX_W),
                      w_gate, b_gate, w_br, w_out_b, lg[1], lb[1], l, alpha)
        h = _ffn_ln(h, wg, wu, wd, lg[2], lb[2], l, 1, alpha)
    return h.reshape(B, S, D)
```

```python
import functools
import math

import numpy as np
import jax
import jax.numpy as jnp
from jax import lax
from jax.experimental import pallas as pl
from jax.experimental.pallas import tpu as pltpu

F32 = jnp.float32
BF16 = jnp.bfloat16

HEAD_DIM = 64
N_HEADS = 6
DIFF_QK_DIM = HEAD_DIM // 2
MOBA_BLOCK = 256
MOBA_TOPK = 3
N_BRANCH = 3
LN_EPS = 1e-5
SUBLN_EPS = 1e-5

LANES = 128
SUBLANES = 8
DENOM_ROWS = 16
V_ROWS = HEAD_DIM + DENOM_ROWS
HEADS_PER_BLOCK = LANES // HEAD_DIM
N_HEAD_BLOCKS = N_HEADS // HEADS_PER_BLOCK
MIX_W = N_HEADS * HEAD_DIM
QKV_W = 9 * MIX_W
ATT_T = 256
NEG = -1e30
SB_EXIT = -110.0
TILES_PER_PASS = 8
SB_TILES_PER_PASS = 4
LOG2E = math.log2(math.e)
N_BIAS_COLS = 3
BOUND_SLACK = 1.01
MAX_EXPONENT = 100.0
SKIP_BITS = 150.0
ITEMS_PER_TRIP = 64
LONE_HEAD_BLOCKS_PER_TRIP = 4
BUILD_UNROLL = 4
STAGE_DEPTH = 2
VMEM_LIMIT = 56 * 1024 * 1024


def _dot(a, b, precision=None):
    return jnp.dot(a, b, precision=precision, preferred_element_type=F32)


def _layer_norm(y, g, b):
    mu = jnp.mean(y, axis=-1, keepdims=True)
    yc = y - mu
    var = jnp.mean(yc * yc, axis=-1, keepdims=True)
    return yc * lax.rsqrt(var + LN_EPS) * g + b


def _ffn_ln_kernel(x_ref, wg_ref, wu_ref, wd_ref, lg_ref, lb_ref, o_ref, hid_ref, *, alpha, tf):
    x = x_ref[...]
    xb = x.astype(BF16)
    for c in range(hid_ref.shape[1] // tf):
        cols = slice(c * tf, (c + 1) * tf)
        gate = _dot(xb, wg_ref[:, cols])
        up = _dot(xb, wu_ref[:, cols])
        hid_ref[:, cols] = (gate * jax.nn.sigmoid(gate) * up).astype(BF16)
    y = alpha * x + 0.5 * _dot(hid_ref[...], wd_ref[...])
    o_ref[...] = _layer_norm(y, lg_ref[...], lb_ref[...])


def _ffn_ln(x, wg, wu, wd, lg, lb, l, j, alpha, tm=512, tf=256):
    T, D = x.shape
    FF = wg.shape[-1]
    resident = pl.Buffered(1)
    return pl.pallas_call(
        functools.partial(_ffn_ln_kernel, alpha=alpha, tf=tf),
        grid=(T // tm,),
        in_specs=[
            pl.BlockSpec((tm, D), lambda i: (i, 0)),
            pl.BlockSpec((None, None, D, FF), lambda i: (l, j, 0, 0), pipeline_mode=resident),
            pl.BlockSpec((None, None, D, FF), lambda i: (l, j, 0, 0), pipeline_mode=resident),
            pl.BlockSpec((None, None, FF, D), lambda i: (l, j, 0, 0), pipeline_mode=resident),
            pl.BlockSpec((1, D), lambda i: (0, 0)),
            pl.BlockSpec((1, D), lambda i: (0, 0)),
        ],
        out_specs=pl.BlockSpec((tm, D), lambda i: (i, 0)),
        out_shape=jax.ShapeDtypeStruct((T, D), F32),
        scratch_shapes=[pltpu.VMEM((tm, FF), BF16)],
        compiler_params=pltpu.CompilerParams(
            dimension_semantics=("parallel",), vmem_limit_bytes=VMEM_LIMIT),
        name="ffn_ln",
    )(x, wg, wu, wd, lg, lb)


def _proj_kernel(x_ref, w_ref, o_ref, *, tn):
    xb = x_ref[...].astype(BF16)
    for c in range(o_ref.shape[1] // tn):
        cols = slice(c * tn, (c + 1) * tn)
        o_ref[:, cols] = _dot(xb, w_ref[:, cols]).astype(o_ref.dtype)


def _qkv_proj(x, w_in, l, tm=512, tn=1152):
    T, D = x.shape
    return pl.pallas_call(
        functools.partial(_proj_kernel, tn=tn),
        grid=(T // tm,),
        in_specs=[
            pl.BlockSpec((tm, D), lambda i: (i, 0)),
            pl.BlockSpec((None, D, QKV_W), lambda i: (l, 0, 0), pipeline_mode=pl.Buffered(1)),
        ],
        out_specs=pl.BlockSpec((tm, QKV_W), lambda i: (i, 0)),
        out_shape=jax.ShapeDtypeStruct((T, QKV_W), BF16),
        compiler_params=pltpu.CompilerParams(
            dimension_semantics=("parallel",), vmem_limit_bytes=VMEM_LIMIT),
        name="qkv_proj",
    )(x, w_in)


def _lane_iota():
    return lax.broadcasted_iota(jnp.int32, (1, LANES), 1)


def _head_lanes(hh):
    lane = _lane_iota()
    return (lane >= hh * HEAD_DIM) & (lane < (hh + 1) * HEAD_DIM)


def _blk(j):
    return pl.ds(pl.multiple_of(j * ATT_T, ATT_T), ATT_T)


def _key_query_iotas():
    krow = lax.broadcasted_iota(jnp.int32, (ATT_T, ATT_T), 0)
    qcol = lax.broadcasted_iota(jnp.int32, (ATT_T, ATT_T), 1)
    return krow, qcol


def _build_softmax_operands(tab_ref, hb, k_ref, v_ref, kp_ref, vt_ref, kmax_ref, *, n_blocks, scale):
    lane = _lane_iota()
    pos = lax.broadcasted_iota(jnp.int32, (ATT_T, LANES), 0).astype(F32)
    bias_cols = []
    for hh in range(HEADS_PER_BLOCK):
        b = pos * tab_ref[hb * HEADS_PER_BLOCK + hh]
        lo = (1 - hh) * HEAD_DIM
        cols = jnp.zeros_like(b)
        for i in range(N_BIAS_COLS):
            piece = b.astype(BF16).astype(F32)
            cols = jnp.where(lane == lo + i, piece, cols)
            b = b - piece
        bias_cols.append(cols)
    kmax_ref[...] = jnp.zeros_like(kmax_ref)
    assert HEADS_PER_BLOCK == 2
    head_r = jnp.where(lax.broadcasted_iota(jnp.int32, (LANES, LANES), 0) < HEAD_DIM, 0, 1)
    head_c = jnp.where(lax.broadcasted_iota(jnp.int32, (LANES, LANES), 1) < HEAD_DIM, 0, 1)
    same_head = jnp.where(head_r == head_c, 1.0, 0.0).astype(BF16)

    def body(n, c):
        kf = k_ref[_blk(n), :].astype(F32) * (scale * LOG2E)
        v_t = v_ref[_blk(n), :].astype(F32).T.astype(BF16)
        for hh in range(HEADS_PER_BLOCK):
            kp_ref[hh, _blk(n), :] = jnp.where(_head_lanes(hh), kf, bias_cols[hh]).astype(BF16)
            vt_ref[hh, 0:HEAD_DIM, _blk(n)] = v_t[hh * HEAD_DIM:(hh + 1) * HEAD_DIM, :]
            vt_ref[hh, HEAD_DIM:V_ROWS, _blk(n)] = jnp.ones((DENOM_ROWS, ATT_T), BF16)
        norm2 = _dot((kf * kf).astype(BF16), same_head)
        kmax_ref[...] = jnp.maximum(kmax_ref[...], jnp.max(norm2, axis=0, keepdims=True))
        return c

    lax.fori_loop(0, n_blocks, body, 0, unroll=BUILD_UNROLL)


def _chain_qt(c, hh, seg_lo, seg_width, q_t, qt_ref):
    row = lax.broadcasted_iota(jnp.int32, (LANES, ATT_T), 0)
    lo = (1 - hh) * HEAD_DIM
    qsel = jnp.where((row >= seg_lo) & (row < seg_lo + seg_width), q_t, 0.0)
    qt_ref[c] = jnp.where((row >= lo) & (row < lo + N_BIAS_COLS), 1.0, qsel).astype(BF16)
    return qsel, jnp.sum(qsel * qsel, axis=0, keepdims=True)


def _stage_scores(c, hh, n, slot, kp_ref, qt_ref, sbuf_ref):
    sbuf_ref[c * STAGE_DEPTH + slot] = _dot(kp_ref[hh, _blk(jnp.maximum(n, 0)), :], qt_ref[c])


def _diag_blocks(chains, qi_of, qn2, slope2, kp_ref, vt_ref, kmax_ref, qt_ref, ref_ref, acc_ref, sbuf_ref):
    krow, qcol = _key_query_iotas()
    causal = krow <= qcol
    r = lax.broadcasted_iota(jnp.int32, (1, ATT_T), 1).astype(F32)
    scores = [_dot(kp_ref[hh, _blk(qi_of[c]), :], qt_ref[c]) for c, hh in enumerate(chains)]
    for slot in range(STAGE_DEPTH):
        for c, hh in enumerate(chains):
            _stage_scores(c, hh, qi_of[c] - 1 - slot, slot, kp_ref, qt_ref, sbuf_ref)
    gaps = []
    for c, hh in enumerate(chains):
        s = jnp.where(causal, scores[c], NEG)
        m = jnp.max(s, axis=0, keepdims=True)
        kmax2 = kmax_ref[:, hh * HEAD_DIM:hh * HEAD_DIM + 1]
        bound = jnp.sqrt(qn2[c] * kmax2) * BOUND_SLACK + slope2[hh] * r
        ref_ref[c] = m
        acc_ref[c] = _dot(vt_ref[hh, :, _blk(qi_of[c])], jnp.exp2(s - m).astype(BF16))
        gaps.append(jnp.max(bound - m))
    return gaps


def _first_block(gap, inv_blk, qi):
    reach = jnp.minimum((gap + SKIP_BITS) * inv_blk, 1e6).astype(jnp.int32)
    return jnp.maximum(qi - 1 - reach, 0)


def _fixed_ref_step(c, hh, n, slot, off, kp_ref, vt_ref, qt_ref, ref_ref, acc_ref, sbuf_ref):
    p = jnp.exp2(sbuf_ref[c * STAGE_DEPTH + slot] - (ref_ref[c] - off)).astype(BF16)
    _stage_scores(c, hh, n - STAGE_DEPTH, slot, kp_ref, qt_ref, sbuf_ref)
    acc_ref[c] += _dot(vt_ref[hh, :, _blk(jnp.maximum(n, 0))], p)


def _online_step(c, hh, n, off, kp_ref, vt_ref, qt_ref, m_ref, acc_ref):
    s = _dot(kp_ref[hh, _blk(n), :], qt_ref[c])
    m_old = m_ref[c]
    m_new = jnp.maximum(m_old, jnp.max(s, axis=0, keepdims=True) + off)
    p = jnp.exp2(s - (m_new - off)).astype(BF16)
    acc_ref[c] = acc_ref[c] * jnp.exp2(m_old - m_new) + _dot(vt_ref[hh, :, _blk(n)], p)
    m_ref[c] = m_new


def _softmax_sweep(qi_of, chains, gaps, depth, off_fn, kp_ref, vt_ref, qt_ref, ref_ref, acc_ref, sbuf_ref,
                   split_heads):
    refs = (kp_ref, vt_ref, qt_ref, ref_ref, acc_ref)
    assert STAGE_DEPTH == 2
    fixed_ok = functools.reduce(jnp.maximum, gaps) <= MAX_EXPONENT

    chains_per_tile = len(chains) // TILES_PER_PASS

    def run_fixed(start, count, heads):
        active = [(c, hh) for c, hh in enumerate(chains) if hh in heads]
        per_trip = max(STAGE_DEPTH, ITEMS_PER_TRIP // len(active))
        if len(heads) == 1:
            per_trip = min(per_trip, LONE_HEAD_BLOCKS_PER_TRIP)
        assert per_trip % STAGE_DEPTH == 0

        def step(below, slot, who):
            for c, hh in who:
                n = qi_of[c] - 1 - below
                off = off_fn(c, hh, jnp.maximum(n, 0)) + jnp.where(n < 0, NEG, 0.0)
                _fixed_ref_step(c, hh, n, slot, off, *refs, sbuf_ref)

        def sweep(start, trips, per_trip):
            def body(i, carry):
                for j in range(per_trip):
                    step(start + i * per_trip + j, j % STAGE_DEPTH, active)
                return carry

            lax.fori_loop(0, trips, body, 0)

        common = jnp.clip(qi_of[0] - start, 0, count) // STAGE_DEPTH * STAGE_DEPTH
        whole = common // per_trip
        sweep(start, whole, per_trip)
        if per_trip > STAGE_DEPTH:
            sweep(start + whole * per_trip, (common - whole * per_trip) // STAGE_DEPTH, STAGE_DEPTH)
        for k in range(0, TILES_PER_PASS, STAGE_DEPTH):
            @pl.when(k < count - common)
            def _some_depths(k=k):
                for j in range(STAGE_DEPTH):
                    who = [(c, hh) for c, hh in active if c // chains_per_tile >= k + j]
                    step(start + common + k + j, j, who)

    @pl.when(fixed_ok)
    def _fixed_reference():
        far = jnp.maximum(depth[0], depth[1])
        if split_heads:
            near = jnp.minimum(depth[0], depth[1])
            together = jnp.minimum((near + 1) // 2 * 2, far)
            run_fixed(0, together, (0, 1))
            for hh in range(HEADS_PER_BLOCK):
                run_fixed(together, jnp.maximum(depth[hh] - together, 0), (hh,))
        else:
            run_fixed(0, far, (0, 1))

    @pl.when(jnp.logical_not(fixed_ok))
    def _running_maximum():
        chains_per_head = chains_per_tile // HEADS_PER_BLOCK
        assert chains[:chains_per_tile] == [c // chains_per_head for c in range(chains_per_tile)]

        def chain_body(c, carry):
            hh = lax.div(lax.rem(c, chains_per_tile), chains_per_head)
            qi = qi_of[0] + lax.div(c, chains_per_tile)

            def body(n, carry):
                _online_step(c, hh, n, off_fn(c, hh, n, qi), *refs)
                return carry

            return lax.fori_loop(0, qi, body, carry)

        lax.fori_loop(0, len(chains), chain_body, 0)


def _normalized(acc):
    return acc[0:HEAD_DIM, :] * (1.0 / acc[HEAD_DIM:HEAD_DIM + 1, :])


def _row_is_head0():
    return lax.broadcasted_iota(jnp.int32, (LANES, ATT_T), 0) < HEAD_DIM


def _moba_kernel(tab_ref, q_ref, k_ref, v_ref, o_ref, kp_ref, vt_ref, kmax_ref, kmean_ref, selm_ref,
                 qt_ref, ref_ref, acc_ref, sbuf_ref, *, n_blocks, scale):
    hb = pl.program_id(1)
    nbp = kmean_ref.shape[0]
    _build_softmax_operands(tab_ref, hb, k_ref, v_ref, kp_ref, vt_ref, kmax_ref, n_blocks=n_blocks, scale=scale)
    kmean_ref[...] = jnp.zeros_like(kmean_ref)

    def block_mean(n, c):
        kmean_ref[pl.ds(n, 1), :] = jnp.mean(k_ref[_blk(n), :].astype(F32), axis=0, keepdims=True)
        return c

    lax.fori_loop(0, n_blocks, block_mean, 0, unroll=BUILD_UNROLL)

    blk_id = lax.broadcasted_iota(jnp.int32, (nbp, ATT_T), 0)
    blk_f = blk_id.astype(F32)
    chains = list(range(HEADS_PER_BLOCK))
    n_ch = len(chains)
    slope2 = [tab_ref[hb * HEADS_PER_BLOCK + hh] for hh in chains]
    assert n_blocks % TILES_PER_PASS == 0

    def tile_group(g, carry):
        tiles = [g * TILES_PER_PASS + s for s in range(TILES_PER_PASS)]
        qn2 = []
        for s, qi in enumerate(tiles):
            q_t = q_ref[_blk(qi), :].astype(F32).T
            for hh in chains:
                hm = _head_lanes(hh)
                qsel, norm2 = _chain_qt(s * n_ch + hh, hh, hh * HEAD_DIM, HEAD_DIM, q_t, qt_ref)
                qn2.append(norm2)

                gate = _dot(jnp.where(hm, kmean_ref[...], 0.0), qsel, precision=lax.Precision.HIGHEST)
                gt = jnp.where(blk_id < qi, gate, -jnp.inf)
                sel = jnp.zeros(gt.shape, dtype=jnp.bool_)
                for _ in range(MOBA_TOPK):
                    mx = jnp.max(gt, axis=0, keepdims=True)
                    pick = jnp.min(jnp.where(gt == mx, blk_f, float(nbp)), axis=0, keepdims=True)
                    hit = blk_f == pick
                    sel = sel | (hit & (mx > -jnp.inf))
                    gt = jnp.where(hit, -jnp.inf, gt)
                selm_ref[s * n_ch + hh] = jnp.where(sel, 0.0, NEG)

        qi_of = [tiles[c // n_ch] for c in range(TILES_PER_PASS * n_ch)]
        gaps = _diag_blocks(chains * TILES_PER_PASS, qi_of, qn2, slope2, kp_ref, vt_ref, kmax_ref, qt_ref,
                            ref_ref, acc_ref, sbuf_ref)
        all_chains = chains * TILES_PER_PASS
        depth = [None] * HEADS_PER_BLOCK
        for c, hh in enumerate(all_chains):
            d = qi_of[c] - _first_block(gaps[c], tab_ref[N_HEADS + hb * HEADS_PER_BLOCK + hh], qi_of[c])
            depth[hh] = d if depth[hh] is None else jnp.maximum(depth[hh], d)

        def off_fn(c, hh, n, qi=None):
            qi = qi_of[c] if qi is None else qi
            slope = tab_ref[hb * HEADS_PER_BLOCK + hh]
            return slope * jnp.asarray((n - qi) * ATT_T, F32) + selm_ref[c, pl.ds(n, 1), :]

        _softmax_sweep(qi_of, all_chains, gaps, depth, off_fn, kp_ref, vt_ref, qt_ref, ref_ref, acc_ref,
                       sbuf_ref, split_heads=False)
        for s, qi in enumerate(tiles):
            o = jnp.concatenate([_normalized(acc_ref[s * n_ch + hh]) for hh in chains], axis=0)
            o_ref[_blk(qi), :] = o.T.astype(o_ref.dtype)
        return carry

    lax.fori_loop(0, n_blocks // TILES_PER_PASS, tile_group, 0)


def _diff_kernel(tab_ref, lam_ref, g_ref, q_ref, k_ref, v_ref, o_ref, kp_ref, vt_ref, kmax_ref, qt_ref,
                 ref_ref, acc_ref, sbuf_ref, *, n_blocks, scale, lambda_init):
    hb = pl.program_id(1)
    _build_softmax_operands(tab_ref, hb, k_ref, v_ref, kp_ref, vt_ref, kmax_ref, n_blocks=n_blocks, scale=scale)

    lf = lam_ref[...].astype(F32)
    lam = (jnp.exp(jnp.sum(lf[0:1] * lf[1:2], axis=-1, keepdims=True))
           - jnp.exp(jnp.sum(lf[2:3] * lf[3:4], axis=-1, keepdims=True)) + lambda_init)

    chains = [hh for hh in range(HEADS_PER_BLOCK) for _ in range(2)]
    n_ch = len(chains)
    slope2 = [tab_ref[hb * HEADS_PER_BLOCK + hh] for hh in range(HEADS_PER_BLOCK)]
    assert n_blocks % TILES_PER_PASS == 0

    def tile_group(g, carry):
        tiles = [g * TILES_PER_PASS + s for s in range(TILES_PER_PASS)]
        qn2 = []
        for s, qi in enumerate(tiles):
            q_t = q_ref[_blk(qi), :].astype(F32).T
            for c, hh in enumerate(chains):
                seg_lo = hh * HEAD_DIM + (c % 2) * DIFF_QK_DIM
                qn2.append(_chain_qt(s * n_ch + c, hh, seg_lo, DIFF_QK_DIM, q_t, qt_ref)[1])
        qi_of = [tiles[c // n_ch] for c in range(TILES_PER_PASS * n_ch)]
        gaps = _diag_blocks(chains * TILES_PER_PASS, qi_of, qn2, slope2, kp_ref, vt_ref, kmax_ref, qt_ref,
                            ref_ref, acc_ref, sbuf_ref)
        all_chains = chains * TILES_PER_PASS
        depth = [None] * HEADS_PER_BLOCK
        for c, hh in enumerate(all_chains):
            d = qi_of[c] - _first_block(gaps[c], tab_ref[N_HEADS + hb * HEADS_PER_BLOCK + hh], qi_of[c])
            depth[hh] = d if depth[hh] is None else jnp.maximum(depth[hh], d)

        def off_fn(c, hh, n, qi=None):
            qi = qi_of[c] if qi is None else qi
            return tab_ref[hb * HEADS_PER_BLOCK + hh] * jnp.asarray((n - qi) * ATT_T, F32)

        _softmax_sweep(qi_of, all_chains, gaps, depth, off_fn, kp_ref, vt_ref, qt_ref, ref_ref, acc_ref,
                       sbuf_ref, split_heads=True)
        for s, qi in enumerate(tiles):
            base = s * n_ch
            outs = []
            for hh in range(HEADS_PER_BLOCK):
                o = (_normalized(acc_ref[base + 2 * hh])
                     - lam * _normalized(acc_ref[base + 2 * hh + 1]))
                ms = jnp.mean(o * o, axis=0, keepdims=True)
                outs.append(o * lax.rsqrt(ms + SUBLN_EPS))
            o = jnp.concatenate(outs, axis=0).T * (g_ref[...] * (1.0 - lambda_init))
            o_ref[_blk(qi), :] = o.astype(o_ref.dtype)
        return carry

    lax.fori_loop(0, n_blocks // TILES_PER_PASS, tile_group, 0)


def _log_sigmoids(z):
    lneg = jnp.minimum(-z, 0.0) - jnp.log(1.0 + jnp.exp(-jnp.abs(z)))
    return lneg, lneg + z


def _suffix_sums(upper, lg):
    hi = lg.astype(BF16)
    lo = (lg - hi.astype(F32)).astype(BF16)
    return _dot(upper, hi) + _dot(upper, lo)


def _sb_pass(tiles, k_ref, vt_ref, qt_ref, carry_ref, acc_ref):
    krow, qcol = _key_query_iotas()
    before = krow < qcol
    upper = (qcol > krow).astype(BF16)
    items = [(slot * HEADS_PER_BLOCK + hh, hh, n, diag)
             for slot, blocks in tiles for n, diag in blocks for hh in range(HEADS_PER_BLOCK)]
    z = [_dot(k_ref[_blk(n), :], qt_ref[st]) for st, _, n, _ in items]
    log_sig, log_surv, tails = [], [], []
    for i, (st, hh, n, diag) in enumerate(items):
        lneg, lpos = _log_sigmoids(jnp.where(before, z[i], NEG) if diag else z[i])
        log_surv.append(lneg)
        log_sig.append(lpos)
        tails.append(_suffix_sums(upper, lneg))
    carry, acc = {}, {}
    for slot, blocks in tiles:
        for hh in range(HEADS_PER_BLOCK):
            st = slot * HEADS_PER_BLOCK + hh
            starts_fresh = blocks[0][1]
            carry[st] = None if starts_fresh else carry_ref[st]
            acc[st] = None if starts_fresh else acc_ref[st]
    for i, (st, hh, n, diag) in enumerate(items):
        w = jnp.exp(log_sig[i] + (tails[i] if diag else tails[i] + carry[st]))
        part = _dot(vt_ref[hh, :, _blk(n)], w.astype(BF16))
        acc[st] = part if acc[st] is None else acc[st] + part
        total = jnp.sum(log_surv[i], axis=0, keepdims=True)
        carry[st] = total if carry[st] is None else carry[st] + total
    worst = []
    for slot, _ in tiles:
        tops = []
        for hh in range(HEADS_PER_BLOCK):
            st = slot * HEADS_PER_BLOCK + hh
            acc_ref[st] = acc[st]
            carry_ref[st] = carry[st]
            tops.append(jnp.max(carry[st]))
        worst.append(functools.reduce(jnp.maximum, tops))
    return tuple(worst)


def _sb_kernel(q_ref, k_ref, v_ref, o_ref, vt_ref, qt_ref, carry_ref, acc_ref, *, n_blocks, scale):
    def transpose_values(n, c):
        v_t = v_ref[_blk(n), :].astype(F32).T.astype(BF16)
        for hh in range(HEADS_PER_BLOCK):
            vt_ref[hh, :, _blk(n)] = v_t[hh * HEAD_DIM:(hh + 1) * HEAD_DIM, :]
        return c

    lax.fori_loop(0, n_blocks, transpose_values, 0, unroll=BUILD_UNROLL)

    assert n_blocks % SB_TILES_PER_PASS == 0
    refs = (k_ref, vt_ref, qt_ref, carry_ref, acc_ref)
    row_h0 = _row_is_head0()

    def tile_group(g, carry):
        tiles = [g * SB_TILES_PER_PASS + slot for slot in range(SB_TILES_PER_PASS)]
        for slot, qi in enumerate(tiles):
            q_t = (q_ref[_blk(qi), :].astype(F32) * scale).T
            for hh in range(HEADS_PER_BLOCK):
                in_head = row_h0 if hh == 0 else jnp.logical_not(row_h0)
                qt_ref[slot * HEADS_PER_BLOCK + hh] = jnp.where(in_head, q_t, 0.0).astype(BF16)

        def blocks_of(qi, with_previous):
            return [(qi, True), (qi - 1, False)] if with_previous else [(qi, True)]

        worst = lax.cond(
            g > 0,
            lambda: _sb_pass([(slot, blocks_of(qi, True)) for slot, qi in enumerate(tiles)], *refs),
            lambda: _sb_pass([(slot, blocks_of(qi, slot > 0)) for slot, qi in enumerate(tiles)], *refs))

        for slot, qi in enumerate(tiles):
            def cond(c):
                j, live = c
                return (j >= 0) & live

            def body(c, slot=slot):
                j, _ = c
                return j - 1, _sb_pass([(slot, [(j, False)])], *refs)[0] > SB_EXIT

            lax.while_loop(cond, body, (qi - 2, worst[slot] > SB_EXIT))
            st = slot * HEADS_PER_BLOCK
            o = jnp.concatenate([acc_ref[st + hh] for hh in range(HEADS_PER_BLOCK)], axis=0)
            o_ref[_blk(qi), :] = o.T.astype(o_ref.dtype)
        return carry

    lax.fori_loop(0, n_blocks // SB_TILES_PER_PASS, tile_group, 0)


def _attn_specs(B, S, q_base, k_base, v_base):
    qspec = pl.BlockSpec((None, S, LANES), lambda b, h: (b, 0, q_base + h))
    kspec = pl.BlockSpec((None, S, LANES), lambda b, h: (b, 0, k_base + h))
    vspec = pl.BlockSpec((None, S, LANES), lambda b, h: (b, 0, v_base + h))
    ospec = pl.BlockSpec((None, S, LANES), lambda b, h: (b, 0, h))
    grid = (B, N_HEAD_BLOCKS)
    params = pltpu.CompilerParams(
        dimension_semantics=("parallel", "parallel"), vmem_limit_bytes=VMEM_LIMIT)
    return grid, qspec, kspec, vspec, ospec, params


_SMEM_SPEC = pl.BlockSpec(memory_space=pltpu.SMEM)


def _moba(qkv, tab):
    B, S, _ = qkv.shape
    grid, qspec, kspec, vspec, ospec, params = _attn_specs(B, S, 0, 3, 6)
    n_blocks = S // MOBA_BLOCK
    nbp = -(-n_blocks // SUBLANES) * SUBLANES
    n_slots = TILES_PER_PASS * HEADS_PER_BLOCK
    return pl.pallas_call(
        functools.partial(_moba_kernel, n_blocks=n_blocks, scale=HEAD_DIM ** -0.5),
        grid=grid,
        in_specs=[_SMEM_SPEC, qspec, kspec, vspec],
        out_specs=ospec,
        out_shape=jax.ShapeDtypeStruct((B, S, MIX_W), BF16),
        scratch_shapes=[
            pltpu.VMEM((HEADS_PER_BLOCK, S, LANES), BF16),
            pltpu.VMEM((HEADS_PER_BLOCK, V_ROWS, S), BF16),
            pltpu.VMEM((1, LANES), F32),
            pltpu.VMEM((nbp, LANES), F32),
            pltpu.VMEM((n_slots, nbp, ATT_T), F32),
            pltpu.VMEM((n_slots, LANES, ATT_T), BF16),
            pltpu.VMEM((n_slots, 1, ATT_T), F32),
            pltpu.VMEM((n_slots, V_ROWS, ATT_T), F32),
            pltpu.VMEM((n_slots * STAGE_DEPTH, ATT_T, ATT_T), F32),
        ],
        compiler_params=params,
        name="moba_attn",
    )(tab, qkv, qkv, qkv)


def _diff(qkv, tab, diff_lambda, subln_g, lambda_init):
    B, S, _ = qkv.shape
    grid, qspec, kspec, vspec, ospec, params = _attn_specs(B, S, 9, 12, 15)
    n_chains = TILES_PER_PASS * 2 * HEADS_PER_BLOCK
    return pl.pallas_call(
        functools.partial(_diff_kernel, n_blocks=S // ATT_T, scale=DIFF_QK_DIM ** -0.5,
                          lambda_init=lambda_init),
        grid=grid,
        in_specs=[
            _SMEM_SPEC,
            pl.BlockSpec(diff_lambda.shape, lambda b, h: (0, 0)),
            pl.BlockSpec((1, LANES), lambda b, h: (0, 0)),
            qspec, kspec, vspec,
        ],
        out_specs=ospec,
        out_shape=jax.ShapeDtypeStruct((B, S, MIX_W), BF16),
        scratch_shapes=[
            pltpu.VMEM((HEADS_PER_BLOCK, S, LANES), BF16),
            pltpu.VMEM((HEADS_PER_BLOCK, V_ROWS, S), BF16),
            pltpu.VMEM((1, LANES), F32),
            pltpu.VMEM((n_chains, LANES, ATT_T), BF16),
            pltpu.VMEM((n_chains, 1, ATT_T), F32),
            pltpu.VMEM((n_chains, V_ROWS, ATT_T), F32),
            pltpu.VMEM((n_chains * STAGE_DEPTH, ATT_T, ATT_T), F32),
        ],
        compiler_params=params,
        name="diff_attn",
    )(tab, diff_lambda, subln_g, qkv, qkv, qkv)


def _sb(qkv):
    B, S, _ = qkv.shape
    grid, qspec, kspec, vspec, ospec, params = _attn_specs(B, S, 18, 21, 24)
    return pl.pallas_call(
        functools.partial(_sb_kernel, n_blocks=S // ATT_T, scale=HEAD_DIM ** -0.5),
        grid=grid,
        in_specs=[qspec, kspec, vspec],
        out_specs=ospec,
        out_shape=jax.ShapeDtypeStruct((B, S, MIX_W), BF16),
        scratch_shapes=[
            pltpu.VMEM((HEADS_PER_BLOCK, HEAD_DIM, S), BF16),
            pltpu.VMEM((SB_TILES_PER_PASS * HEADS_PER_BLOCK, LANES, ATT_T), BF16),
            pltpu.VMEM((SB_TILES_PER_PASS * HEADS_PER_BLOCK, 1, ATT_T), F32),
            pltpu.VMEM((SB_TILES_PER_PASS * HEADS_PER_BLOCK, HEAD_DIM, ATT_T), F32),
        ],
        compiler_params=params,
        name="sb_attn",
    )(qkv, qkv, qkv)


def _merge_ln_kernel(x_ref, om_ref, od_ref, os_ref, wg_ref, bg_ref, wbr_ref, wo_ref, lg_ref, lb_ref,
                     o_ref, *, alpha, d_model):
    x = x_ref[...]
    xb = x.astype(BF16)
    merged = None
    for b, ob_ref in enumerate((om_ref, od_ref, os_ref)):
        glogit = _dot(xb, wg_ref[:, b * d_model:(b + 1) * d_model]) + bg_ref[b:b + 1, :]
        term = jax.nn.sigmoid(glogit) * _dot(ob_ref[...], wbr_ref[b])
        merged = term if merged is None else merged + term
    y = alpha * x + _dot(merged.astype(BF16), wo_ref[...])
    o_ref[...] = _layer_norm(y, lg_ref[...], lb_ref[...])


def _merge_ln(x, o_m, o_d, o_s, w_gate, b_gate, w_br, w_out, lg, lb, l, alpha, tm=512):
    T, D = x.shape
    return pl.pallas_call(
        functools.partial(_merge_ln_kernel, alpha=alpha, d_model=D),
        grid=(T // tm,),
        in_specs=[
            pl.BlockSpec((tm, D), lambda i: (i, 0)),
            pl.BlockSpec((tm, MIX_W), lambda i: (i, 0)),
            pl.BlockSpec((tm, MIX_W), lambda i: (i, 0)),
            pl.BlockSpec((tm, MIX_W), lambda i: (i, 0)),
            pl.BlockSpec((None, D, N_BRANCH * D), lambda i: (l, 0, 0), pipeline_mode=pl.Buffered(1)),
            pl.BlockSpec((None, N_BRANCH, D), lambda i: (l, 0, 0)),
            pl.BlockSpec((None, N_BRANCH, MIX_W, D), lambda i: (l, 0, 0, 0), pipeline_mode=pl.Buffered(1)),
            pl.BlockSpec((None, D, D), lambda i: (l, 0, 0), pipeline_mode=pl.Buffered(1)),
            pl.BlockSpec((1, D), lambda i: (0, 0)),
            pl.BlockSpec((1, D), lambda i: (0, 0)),
        ],
        out_specs=pl.BlockSpec((tm, D), lambda i: (i, 0)),
        out_shape=jax.ShapeDtypeStruct((T, D), F32),
        compiler_params=pltpu.CompilerParams(
            dimension_semantics=("parallel",), vmem_limit_bytes=VMEM_LIMIT),
        name="merge_ln",
    )(x, o_m, o_d, o_s, w_gate, b_gate, w_br, w_out, lg, lb)


def _alibi_slopes(n):
    return (2.0 ** (-8.0 * np.arange(1, n + 1, dtype=np.float32) / n)).astype(np.float32)


def _slope_table(slopes):
    s2 = slopes.astype(np.float64) * LOG2E
    return np.concatenate([s2, 1.0 / (s2 * ATT_T)]).astype(np.float32)


def kernel(x, ln_g, ln_b, ffn_w_gate, ffn_w_up, ffn_w_down, w_in, b_gate, diff_lambda, diff_subln_g,
           w_br_moba, w_br_diff, w_br_sb, w_out):
    B, S, D = x.shape
    depth = ln_g.shape[0]
    assert S % ATT_T == 0 and ATT_T == MOBA_BLOCK
    assert w_in.shape[-1] == QKV_W + N_BRANCH * D
    alpha = (2.0 * depth) ** 0.25

    wg = ffn_w_gate.astype(BF16)
    wu = ffn_w_up.astype(BF16)
    wd = ffn_w_down.astype(BF16)
    w_qkv = w_in[:, :, :QKV_W].astype(BF16)
    w_gate = w_in[:, :, QKV_W:].astype(BF16)
    w_br = jnp.stack([w_br_moba, w_br_diff, w_br_sb], axis=1).astype(BF16)
    w_out_b = w_out.astype(BF16)
    slopes = _alibi_slopes(2 * N_HEADS)
    tab_moba = jnp.asarray(_slope_table(slopes[0::2]))
    tab_diff = jnp.asarray(_slope_table(slopes[1::2]))
    subln_g = jnp.tile(diff_subln_g.astype(F32), (1, HEADS_PER_BLOCK))

    h = x.reshape(B * S, D)
    for l in range(depth):
        lg = ln_g[l][:, None, :]
        lb = ln_b[l][:, None, :]
        h = _ffn_ln(h, wg, wu, wd, lg[0], lb[0], l, 0, alpha)
        qkv = _qkv_proj(h, w_qkv, l).reshape(B, S, QKV_W)
        lambda_init = 0.8 - 0.6 * math.exp(-0.3 * l)
        o_m = _moba(qkv, tab_moba)
        o_d = _diff(qkv, tab_diff, diff_lambda[l], subln_g[l][None, :], lambda_init)
        o_s = _sb(qkv)
        T = B * S
        h = _merge_ln(h, o_m.reshape(T, MIX_W), o_d.reshape(T, MIX_W), o_s.reshape(T, MIX_W),
                      w_gate, b_gate, w_br, w_out_b, lg[1], lb[1], l, alpha)
        h = _ffn_ln(h, wg, wu, wd, lg[2], lb[2], l, 1, alpha)
    return h.reshape(B, S, D)
```

```python
import functools
import math

import numpy as np
import jax
import jax.numpy as jnp
from jax import lax
from jax.experimental import pallas as pl
from jax.experimental.pallas import tpu as pltpu

F32 = jnp.float32
BF16 = jnp.bfloat16

HEAD_DIM = 64
N_HEADS = 6
DIFF_QK_DIM = HEAD_DIM // 2
MOBA_BLOCK = 256
MOBA_TOPK = 3
N_BRANCH = 3
LN_EPS = 1e-5
SUBLN_EPS = 1e-5

LANES = 128
SUBLANES = 8
DENOM_ROWS = 16
V_ROWS = HEAD_DIM + DENOM_ROWS
HEADS_PER_BLOCK = LANES // HEAD_DIM
N_HEAD_BLOCKS = N_HEADS // HEADS_PER_BLOCK
MIX_W = N_HEADS * HEAD_DIM
QKV_W = 9 * MIX_W
ATT_T = 256
NEG = -1e30
SB_EXIT = -110.0
TILES_PER_PASS = 8
SB_TILES_PER_PASS = 4
LOG2E = math.log2(math.e)
N_BIAS_COLS = 3
BOUND_SLACK = 1.01
MAX_EXPONENT = 100.0
SKIP_BITS = 150.0
ITEMS_PER_TRIP = 64
LONE_HEAD_BLOCKS_PER_TRIP = 4
BUILD_UNROLL = 4
STAGE_DEPTH = 2
VMEM_LIMIT = 56 * 1024 * 1024


def _dot(a, b, precision=None):
    return jnp.dot(a, b, precision=precision, preferred_element_type=F32)


def _layer_norm(y, g, b):
    mu = jnp.mean(y, axis=-1, keepdims=True)
    yc = y - mu
    var = jnp.mean(yc * yc, axis=-1, keepdims=True)
    return yc * lax.rsqrt(var + LN_EPS) * g + b


def _ffn_ln_kernel(x_ref, wg_ref, wu_ref, wd_ref, lg_ref, lb_ref, *rest, alpha, tf, tn):
    o_ref, hid_ref = rest[-3 if len(rest) == 4 else 0], rest[-1]
    x = x_ref[...]
    xb = x.astype(BF16)
    for c in range(hid_ref.shape[1] // tf):
        cols = slice(c * tf, (c + 1) * tf)
        gate = _dot(xb, wg_ref[:, cols])
        up = _dot(xb, wu_ref[:, cols])
        hid_ref[:, cols] = (gate * jax.nn.sigmoid(gate) * up).astype(BF16)
    y = alpha * x + 0.5 * _dot(hid_ref[...], wd_ref[...])
    out = _layer_norm(y, lg_ref[...], lb_ref[...])
    o_ref[...] = out
    if len(rest) == 4:
        w_qkv_ref, qkv_ref = rest[0], rest[2]
        ob = out.astype(BF16)
        for c in range(qkv_ref.shape[1] // tn):
            cols = slice(c * tn, (c + 1) * tn)
            qkv_ref[:, cols] = _dot(ob, w_qkv_ref[:, cols]).astype(qkv_ref.dtype)


def _ffn_ln(x, wg, wu, wd, lg, lb, l, j, alpha, w_qkv=None, tm=512, tf=256, tn=1152):
    T, D = x.shape
    FF = wg.shape[-1]
    resident = pl.Buffered(1)
    in_specs = [
        pl.BlockSpec((tm, D), lambda i: (i, 0)),
        pl.BlockSpec((None, None, D, FF), lambda i: (l, j, 0, 0), pipeline_mode=resident),
        pl.BlockSpec((None, None, D, FF), lambda i: (l, j, 0, 0), pipeline_mode=resident),
        pl.BlockSpec((None, None, FF, D), lambda i: (l, j, 0, 0), pipeline_mode=resident),
        pl.BlockSpec((1, D), lambda i: (0, 0)),
        pl.BlockSpec((1, D), lambda i: (0, 0)),
    ]
    out_specs = pl.BlockSpec((tm, D), lambda i: (i, 0))
    out_shape = jax.ShapeDtypeStruct((T, D), F32)
    args = (x, wg, wu, wd, lg, lb)
    if w_qkv is not None:
        in_specs.append(pl.BlockSpec((None, D, QKV_W), lambda i: (l, 0, 0), pipeline_mode=resident))
        out_specs = (out_specs, pl.BlockSpec((tm, QKV_W), lambda i: (i, 0)))
        out_shape = (out_shape, jax.ShapeDtypeStruct((T, QKV_W), BF16))
        args = args + (w_qkv,)
    return pl.pallas_call(
        functools.partial(_ffn_ln_kernel, alpha=alpha, tf=tf, tn=tn),
        grid=(T // tm,),
        in_specs=in_specs,
        out_specs=out_specs,
        out_shape=out_shape,
        scratch_shapes=[pltpu.VMEM((tm, FF), BF16)],
        compiler_params=pltpu.CompilerParams(
            dimension_semantics=("parallel",), vmem_limit_bytes=VMEM_LIMIT),
        name="ffn_ln",
    )(*args)


def _lane_iota():
    return lax.broadcasted_iota(jnp.int32, (1, LANES), 1)


def _head_lanes(hh):
    lane = _lane_iota()
    return (lane >= hh * HEAD_DIM) & (lane < (hh + 1) * HEAD_DIM)


def _blk(j):
    return pl.ds(pl.multiple_of(j * ATT_T, ATT_T), ATT_T)


def _key_query_iotas():
    krow = lax.broadcasted_iota(jnp.int32, (ATT_T, ATT_T), 0)
    qcol = lax.broadcasted_iota(jnp.int32, (ATT_T, ATT_T), 1)
    return krow, qcol


def _build_softmax_operands(tab_ref, hb, k_ref, v_ref, kp_ref, vt_ref, kmax_ref, *, n_blocks, scale):
    lane = _lane_iota()
    pos = lax.broadcasted_iota(jnp.int32, (ATT_T, LANES), 0).astype(F32)
    bias_cols = []
    for hh in range(HEADS_PER_BLOCK):
        b = pos * tab_ref[hb * HEADS_PER_BLOCK + hh]
        lo = (1 - hh) * HEAD_DIM
        cols = jnp.zeros_like(b)
        for i in range(N_BIAS_COLS):
            piece = b.astype(BF16).astype(F32)
            cols = jnp.where(lane == lo + i, piece, cols)
            b = b - piece
        bias_cols.append(cols)
    kmax_ref[...] = jnp.zeros_like(kmax_ref)
    assert HEADS_PER_BLOCK == 2
    head_r = jnp.where(lax.broadcasted_iota(jnp.int32, (LANES, LANES), 0) < HEAD_DIM, 0, 1)
    head_c = jnp.where(lax.broadcasted_iota(jnp.int32, (LANES, LANES), 1) < HEAD_DIM, 0, 1)
    same_head = jnp.where(head_r == head_c, 1.0, 0.0).astype(BF16)

    def body(n, c):
        kf = k_ref[_blk(n), :].astype(F32) * (scale * LOG2E)
        v_t = v_ref[_blk(n), :].astype(F32).T.astype(BF16)
        for hh in range(HEADS_PER_BLOCK):
            kp_ref[hh, _blk(n), :] = jnp.where(_head_lanes(hh), kf, bias_cols[hh]).astype(BF16)
            vt_ref[hh, 0:HEAD_DIM, _blk(n)] = v_t[hh * HEAD_DIM:(hh + 1) * HEAD_DIM, :]
            vt_ref[hh, HEAD_DIM:V_ROWS, _blk(n)] = jnp.ones((DENOM_ROWS, ATT_T), BF16)
        norm2 = _dot((kf * kf).astype(BF16), same_head)
        kmax_ref[...] = jnp.maximum(kmax_ref[...], jnp.max(norm2, axis=0, keepdims=True))
        return c

    lax.fori_loop(0, n_blocks, body, 0, unroll=BUILD_UNROLL)


def _chain_qt(c, hh, seg_lo, seg_width, q_t, qt_ref):
    row = lax.broadcasted_iota(jnp.int32, (LANES, ATT_T), 0)
    lo = (1 - hh) * HEAD_DIM
    qsel = jnp.where((row >= seg_lo) & (row < seg_lo + seg_width), q_t, 0.0)
    qt_ref[c] = jnp.where((row >= lo) & (row < lo + N_BIAS_COLS), 1.0, qsel).astype(BF16)
    return qsel, jnp.sum(qsel * qsel, axis=0, keepdims=True)


def _stage_scores(c, hh, n, slot, kp_ref, qt_ref, sbuf_ref):
    sbuf_ref[c * STAGE_DEPTH + slot] = _dot(kp_ref[hh, _blk(jnp.maximum(n, 0)), :], qt_ref[c])


def _diag_blocks(chains, qi_of, qn2, slope2, kp_ref, vt_ref, kmax_ref, qt_ref, ref_ref, acc_ref, sbuf_ref):
    krow, qcol = _key_query_iotas()
    causal = krow <= qcol
    r = lax.broadcasted_iota(jnp.int32, (1, ATT_T), 1).astype(F32)
    scores = [_dot(kp_ref[hh, _blk(qi_of[c]), :], qt_ref[c]) for c, hh in enumerate(chains)]
    for slot in range(STAGE_DEPTH):
        for c, hh in enumerate(chains):
            _stage_scores(c, hh, qi_of[c] - 1 - slot, slot, kp_ref, qt_ref, sbuf_ref)
    gaps = []
    for c, hh in enumerate(chains):
        s = jnp.where(causal, scores[c], NEG)
        m = jnp.max(s, axis=0, keepdims=True)
        kmax2 = kmax_ref[:, hh * HEAD_DIM:hh * HEAD_DIM + 1]
        bound = jnp.sqrt(qn2[c] * kmax2) * BOUND_SLACK + slope2[hh] * r
        ref_ref[c] = m
        acc_ref[c] = _dot(vt_ref[hh, :, _blk(qi_of[c])], jnp.exp2(s - m).astype(BF16))
        gaps.append(jnp.max(bound - m))
    return gaps


def _first_block(gap, inv_blk, qi):
    reach = jnp.minimum((gap + SKIP_BITS) * inv_blk, 1e6).astype(jnp.int32)
    return jnp.maximum(qi - 1 - reach, 0)


def _fixed_ref_step(c, hh, n, slot, off, kp_ref, vt_ref, qt_ref, ref_ref, acc_ref, sbuf_ref):
    p = jnp.exp2(sbuf_ref[c * STAGE_DEPTH + slot] - (ref_ref[c] - off)).astype(BF16)
    _stage_scores(c, hh, n - STAGE_DEPTH, slot, kp_ref, qt_ref, sbuf_ref)
    acc_ref[c] += _dot(vt_ref[hh, :, _blk(jnp.maximum(n, 0))], p)


def _online_step(c, hh, n, off, kp_ref, vt_ref, qt_ref, m_ref, acc_ref):
    s = _dot(kp_ref[hh, _blk(n), :], qt_ref[c])
    m_old = m_ref[c]
    m_new = jnp.maximum(m_old, jnp.max(s, axis=0, keepdims=True) + off)
    p = jnp.exp2(s - (m_new - off)).astype(BF16)
    acc_ref[c] = acc_ref[c] * jnp.exp2(m_old - m_new) + _dot(vt_ref[hh, :, _blk(n)], p)
    m_ref[c] = m_new


def _softmax_sweep(qi_of, chains, gaps, depth, off_fn, kp_ref, vt_ref, qt_ref, ref_ref, acc_ref, sbuf_ref,
                   split_heads):
    refs = (kp_ref, vt_ref, qt_ref, ref_ref, acc_ref)
    assert STAGE_DEPTH == 2
    fixed_ok = functools.reduce(jnp.maximum, gaps) <= MAX_EXPONENT

    chains_per_tile = len(chains) // TILES_PER_PASS

    def run_fixed(start, count, heads):
        active = [(c, hh) for c, hh in enumerate(chains) if hh in heads]
        per_trip = max(STAGE_DEPTH, ITEMS_PER_TRIP // len(active))
        if len(heads) == 1:
            per_trip = min(per_trip, LONE_HEAD_BLOCKS_PER_TRIP)
        assert per_trip % STAGE_DEPTH == 0

        def step(below, slot, who):
            for c, hh in who:
                n = qi_of[c] - 1 - below
                off = off_fn(c, hh, jnp.maximum(n, 0)) + jnp.where(n < 0, NEG, 0.0)
                _fixed_ref_step(c, hh, n, slot, off, *refs, sbuf_ref)

        def sweep(start, trips, per_trip):
            def body(i, carry):
                for j in range(per_trip):
                    step(start + i * per_trip + j, j % STAGE_DEPTH, active)
                return carry

            lax.fori_loop(0, trips, body, 0)

        common = jnp.clip(qi_of[0] - start, 0, count) // STAGE_DEPTH * STAGE_DEPTH
        whole = common // per_trip
        sweep(start, whole, per_trip)
        if per_trip > STAGE_DEPTH:
            sweep(start + whole * per_trip, (common - whole * per_trip) // STAGE_DEPTH, STAGE_DEPTH)
        for k in range(0, TILES_PER_PASS, STAGE_DEPTH):
            @pl.when(k < count - common)
            def _some_depths(k=k):
                for j in range(STAGE_DEPTH):
                    who = [(c, hh) for c, hh in active if c // chains_per_tile >= k + j]
                    step(start + common + k + j, j, who)

    @pl.when(fixed_ok)
    def _fixed_reference():
        far = jnp.maximum(depth[0], depth[1])
        if split_heads:
            near = jnp.minimum(depth[0], depth[1])
            together = jnp.minimum((near + 1) // 2 * 2, far)
            run_fixed(0, together, (0, 1))
            for hh in range(HEADS_PER_BLOCK):
                run_fixed(together, jnp.maximum(depth[hh] - together, 0), (hh,))
        else:
            run_fixed(0, far, (0, 1))

    @pl.when(jnp.logical_not(fixed_ok))
    def _running_maximum():
        chains_per_head = chains_per_tile // HEADS_PER_BLOCK
        assert chains[:chains_per_tile] == [c // chains_per_head for c in range(chains_per_tile)]

        def chain_body(c, carry):
            hh = lax.div(lax.rem(c, chains_per_tile), chains_per_head)
            qi = qi_of[0] + lax.div(c, chains_per_tile)

            def body(n, carry):
                _online_step(c, hh, n, off_fn(c, hh, n, qi), *refs)
                return carry

            return lax.fori_loop(0, qi, body, carry)

        lax.fori_loop(0, len(chains), chain_body, 0)


def _normalized(acc):
    return acc[0:HEAD_DIM, :] * (1.0 / acc[HEAD_DIM:HEAD_DIM + 1, :])


def _row_is_head0():
    return lax.broadcasted_iota(jnp.int32, (LANES, ATT_T), 0) < HEAD_DIM


def _moba_kernel(tab_ref, q_ref, k_ref, v_ref, o_ref, kp_ref, vt_ref, kmax_ref, kmean_ref, selm_ref,
                 qt_ref, ref_ref, acc_ref, sbuf_ref, *, n_blocks, scale):
    hb = pl.program_id(1)
    nbp = kmean_ref.shape[0]
    _build_softmax_operands(tab_ref, hb, k_ref, v_ref, kp_ref, vt_ref, kmax_ref, n_blocks=n_blocks, scale=scale)
    kmean_ref[...] = jnp.zeros_like(kmean_ref)

    def block_mean(n, c):
        kmean_ref[pl.ds(n, 1), :] = jnp.mean(k_ref[_blk(n), :].astype(F32), axis=0, keepdims=True)
        return c

    lax.fori_loop(0, n_blocks, block_mean, 0, unroll=BUILD_UNROLL)

    blk_id = lax.broadcasted_iota(jnp.int32, (nbp, ATT_T), 0)
    blk_f = blk_id.astype(F32)
    chains = list(range(HEADS_PER_BLOCK))
    n_ch = len(chains)
    slope2 = [tab_ref[hb * HEADS_PER_BLOCK + hh] for hh in chains]
    assert n_blocks % TILES_PER_PASS == 0

    def tile_group(g, carry):
        tiles = [g * TILES_PER_PASS + s for s in range(TILES_PER_PASS)]
        qn2 = []
        for s, qi in enumerate(tiles):
            q_t = q_ref[_blk(qi), :].astype(F32).T
            for hh in chains:
                hm = _head_lanes(hh)
                qsel, norm2 = _chain_qt(s * n_ch + hh, hh, hh * HEAD_DIM, HEAD_DIM, q_t, qt_ref)
                qn2.append(norm2)

                gate = _dot(jnp.where(hm, kmean_ref[...], 0.0), qsel, precision=lax.Precision.HIGHEST)
                gt = jnp.where(blk_id < qi, gate, -jnp.inf)
                sel = jnp.zeros(gt.shape, dtype=jnp.bool_)
                for _ in range(MOBA_TOPK):
                    mx = jnp.max(gt, axis=0, keepdims=True)
                    pick = jnp.min(jnp.where(gt == mx, blk_f, float(nbp)), axis=0, keepdims=True)
                    hit = blk_f == pick
                    sel = sel | (hit & (mx > -jnp.inf))
                    gt = jnp.where(hit, -jnp.inf, gt)
                selm_ref[s * n_ch + hh] = jnp.where(sel, 0.0, NEG)

        qi_of = [tiles[c // n_ch] for c in range(TILES_PER_PASS * n_ch)]
        gaps = _diag_blocks(chains * TILES_PER_PASS, qi_of, qn2, slope2, kp_ref, vt_ref, kmax_ref, qt_ref,
                            ref_ref, acc_ref, sbuf_ref)
        all_chains = chains * TILES_PER_PASS
        depth = [None] * HEADS_PER_BLOCK
        for c, hh in enumerate(all_chains):
            d = qi_of[c] - _first_block(gaps[c], tab_ref[N_HEADS + hb * HEADS_PER_BLOCK + hh], qi_of[c])
            depth[hh] = d if depth[hh] is None else jnp.maximum(depth[hh], d)

        def off_fn(c, hh, n, qi=None):
            qi = qi_of[c] if qi is None else qi
            slope = tab_ref[hb * HEADS_PER_BLOCK + hh]
            return slope * jnp.asarray((n - qi) * ATT_T, F32) + selm_ref[c, pl.ds(n, 1), :]

        _softmax_sweep(qi_of, all_chains, gaps, depth, off_fn, kp_ref, vt_ref, qt_ref, ref_ref, acc_ref,
                       sbuf_ref, split_heads=False)
        for s, qi in enumerate(tiles):
            o = jnp.concatenate([_normalized(acc_ref[s * n_ch + hh]) for hh in chains], axis=0)
            o_ref[_blk(qi), :] = o.T.astype(o_ref.dtype)
        return carry

    lax.fori_loop(0, n_blocks // TILES_PER_PASS, tile_group, 0)


def _diff_kernel(tab_ref, lam_ref, g_ref, q_ref, k_ref, v_ref, o_ref, kp_ref, vt_ref, kmax_ref, qt_ref,
                 ref_ref, acc_ref, sbuf_ref, *, n_blocks, scale, lambda_init):
    hb = pl.program_id(1)
    _build_softmax_operands(tab_ref, hb, k_ref, v_ref, kp_ref, vt_ref, kmax_ref, n_blocks=n_blocks, scale=scale)

    lf = lam_ref[...].astype(F32)
    lam = (jnp.exp(jnp.sum(lf[0:1] * lf[1:2], axis=-1, keepdims=True))
           - jnp.exp(jnp.sum(lf[2:3] * lf[3:4], axis=-1, keepdims=True)) + lambda_init)

    chains = [hh for hh in range(HEADS_PER_BLOCK) for _ in range(2)]
    n_ch = len(chains)
    slope2 = [tab_ref[hb * HEADS_PER_BLOCK + hh] for hh in range(HEADS_PER_BLOCK)]
    assert n_blocks % TILES_PER_PASS == 0

    def tile_group(g, carry):
        tiles = [g * TILES_PER_PASS + s for s in range(TILES_PER_PASS)]
        qn2 = []
        for s, qi in enumerate(tiles):
            q_t = q_ref[_blk(qi), :].astype(F32).T
            for c, hh in enumerate(chains):
                seg_lo = hh * HEAD_DIM + (c % 2) * DIFF_QK_DIM
                qn2.append(_chain_qt(s * n_ch + c, hh, seg_lo, DIFF_QK_DIM, q_t, qt_ref)[1])
        qi_of = [tiles[c // n_ch] for c in range(TILES_PER_PASS * n_ch)]
        gaps = _diag_blocks(chains * TILES_PER_PASS, qi_of, qn2, slope2, kp_ref, vt_ref, kmax_ref, qt_ref,
                            ref_ref, acc_ref, sbuf_ref)
        all_chains = chains * TILES_PER_PASS
        depth = [None] * HEADS_PER_BLOCK
        for c, hh in enumerate(all_chains):
            d = qi_of[c] - _first_block(gaps[c], tab_ref[N_HEADS + hb * HEADS_PER_BLOCK + hh], qi_of[c])
            depth[hh] = d if depth[hh] is None else jnp.maximum(depth[hh], d)

        def off_fn(c, hh, n, qi=None):
            qi = qi_of[c] if qi is None else qi
            return tab_ref[hb * HEADS_PER_BLOCK + hh] * jnp.asarray((n - qi) * ATT_T, F32)

        _softmax_sweep(qi_of, all_chains, gaps, depth, off_fn, kp_ref, vt_ref, qt_ref, ref_ref, acc_ref,
                       sbuf_ref, split_heads=True)
        for s, qi in enumerate(tiles):
            base = s * n_ch
            outs = []
            for hh in range(HEADS_PER_BLOCK):
                o = (_normalized(acc_ref[base + 2 * hh])
                     - lam * _normalized(acc_ref[base + 2 * hh + 1]))
                ms = jnp.mean(o * o, axis=0, keepdims=True)
                outs.append(o * lax.rsqrt(ms + SUBLN_EPS))
            o = jnp.concatenate(outs, axis=0).T * (g_ref[...] * (1.0 - lambda_init))
            o_ref[_blk(qi), :] = o.astype(o_ref.dtype)
        return carry

    lax.fori_loop(0, n_blocks // TILES_PER_PASS, tile_group, 0)


def _log_sigmoids(z):
    lneg = jnp.minimum(-z, 0.0) - jnp.log(1.0 + jnp.exp(-jnp.abs(z)))
    return lneg, lneg + z


def _suffix_sums(upper, lg):
    hi = lg.astype(BF16)
    lo = (lg - hi.astype(F32)).astype(BF16)
    return _dot(upper, hi) + _dot(upper, lo)


def _sb_pass(tiles, k_ref, vt_ref, qt_ref, carry_ref, acc_ref):
    krow, qcol = _key_query_iotas()
    before = krow < qcol
    upper = (qcol > krow).astype(BF16)
    items = [(slot * HEADS_PER_BLOCK + hh, hh, n, diag)
             for slot, blocks in tiles for n, diag in blocks for hh in range(HEADS_PER_BLOCK)]
    z = [_dot(k_ref[_blk(n), :], qt_ref[st]) for st, _, n, _ in items]
    log_sig, log_surv, tails = [], [], []
    for i, (st, hh, n, diag) in enumerate(items):
        lneg, lpos = _log_sigmoids(jnp.where(before, z[i], NEG) if diag else z[i])
        log_surv.append(lneg)
        log_sig.append(lpos)
        tails.append(_suffix_sums(upper, lneg))
    carry, acc = {}, {}
    for slot, blocks in tiles:
        for hh in range(HEADS_PER_BLOCK):
            st = slot * HEADS_PER_BLOCK + hh
            starts_fresh = blocks[0][1]
            carry[st] = None if starts_fresh else carry_ref[st]
            acc[st] = None if starts_fresh else acc_ref[st]
    for i, (st, hh, n, diag) in enumerate(items):
        w = jnp.exp(log_sig[i] + (tails[i] if diag else tails[i] + carry[st]))
        part = _dot(vt_ref[hh, :, _blk(n)], w.astype(BF16))
        acc[st] = part if acc[st] is None else acc[st] + part
        total = jnp.sum(log_surv[i], axis=0, keepdims=True)
        carry[st] = total if carry[st] is None else carry[st] + total
    worst = []
    for slot, _ in tiles:
        tops = []
        for hh in range(HEADS_PER_BLOCK):
            st = slot * HEADS_PER_BLOCK + hh
            acc_ref[st] = acc[st]
            carry_ref[st] = carry[st]
            tops.append(jnp.max(carry[st]))
        worst.append(functools.reduce(jnp.maximum, tops))
    return tuple(worst)


def _sb_kernel(q_ref, k_ref, v_ref, o_ref, vt_ref, qt_ref, carry_ref, acc_ref, *, n_blocks, scale):
    def transpose_values(n, c):
        v_t = v_ref[_blk(n), :].astype(F32).T.astype(BF16)
        for hh in range(HEADS_PER_BLOCK):
            vt_ref[hh, :, _blk(n)] = v_t[hh * HEAD_DIM:(hh + 1) * HEAD_DIM, :]
        return c

    lax.fori_loop(0, n_blocks, transpose_values, 0, unroll=BUILD_UNROLL)

    assert n_blocks % SB_TILES_PER_PASS == 0
    refs = (k_ref, vt_ref, qt_ref, carry_ref, acc_ref)
    row_h0 = _row_is_head0()

    def tile_group(g, carry):
        tiles = [g * SB_TILES_PER_PASS + slot for slot in range(SB_TILES_PER_PASS)]
        for slot, qi in enumerate(tiles):
            q_t = (q_ref[_blk(qi), :].astype(F32) * scale).T
            for hh in range(HEADS_PER_BLOCK):
                in_head = row_h0 if hh == 0 else jnp.logical_not(row_h0)
                qt_ref[slot * HEADS_PER_BLOCK + hh] = jnp.where(in_head, q_t, 0.0).astype(BF16)

        def blocks_of(qi, with_previous):
            return [(qi, True), (qi - 1, False)] if with_previous else [(qi, True)]

        worst = lax.cond(
            g > 0,
            lambda: _sb_pass([(slot, blocks_of(qi, True)) for slot, qi in enumerate(tiles)], *refs),
            lambda: _sb_pass([(slot, blocks_of(qi, slot > 0)) for slot, qi in enumerate(tiles)], *refs))

        for slot, qi in enumerate(tiles):
            def cond(c):
                j, live = c
                return (j >= 0) & live

            def body(c, slot=slot):
                j, _ = c
                return j - 1, _sb_pass([(slot, [(j, False)])], *refs)[0] > SB_EXIT

            lax.while_loop(cond, body, (qi - 2, worst[slot] > SB_EXIT))
            st = slot * HEADS_PER_BLOCK
            o = jnp.concatenate([acc_ref[st + hh] for hh in range(HEADS_PER_BLOCK)], axis=0)
            o_ref[_blk(qi), :] = o.T.astype(o_ref.dtype)
        return carry

    lax.fori_loop(0, n_blocks // SB_TILES_PER_PASS, tile_group, 0)


def _attn_specs(B, S, q_base, k_base, v_base):
    qspec = pl.BlockSpec((None, S, LANES), lambda b, h: (b, 0, q_base + h))
    kspec = pl.BlockSpec((None, S, LANES), lambda b, h: (b, 0, k_base + h))
    vspec = pl.BlockSpec((None, S, LANES), lambda b, h: (b, 0, v_base + h))
    ospec = pl.BlockSpec((None, S, LANES), lambda b, h: (b, 0, h))
    grid = (B, N_HEAD_BLOCKS)
    params = pltpu.CompilerParams(
        dimension_semantics=("parallel", "parallel"), vmem_limit_bytes=VMEM_LIMIT)
    return grid, qspec, kspec, vspec, ospec, params


_SMEM_SPEC = pl.BlockSpec(memory_space=pltpu.SMEM)


def _moba(qkv, tab):
    B, S, _ = qkv.shape
    grid, qspec, kspec, vspec, ospec, params = _attn_specs(B, S, 0, 3, 6)
    n_blocks = S // MOBA_BLOCK
    nbp = -(-n_blocks // SUBLANES) * SUBLANES
    n_slots = TILES_PER_PASS * HEADS_PER_BLOCK
    return pl.pallas_call(
        functools.partial(_moba_kernel, n_blocks=n_blocks, scale=HEAD_DIM ** -0.5),
        grid=grid,
        in_specs=[_SMEM_SPEC, qspec, kspec, vspec],
        out_specs=ospec,
        out_shape=jax.ShapeDtypeStruct((B, S, MIX_W), BF16),
        scratch_shapes=[
            pltpu.VMEM((HEADS_PER_BLOCK, S, LANES), BF16),
            pltpu.VMEM((HEADS_PER_BLOCK, V_ROWS, S), BF16),
            pltpu.VMEM((1, LANES), F32),
            pltpu.VMEM((nbp, LANES), F32),
            pltpu.VMEM((n_slots, nbp, ATT_T), F32),
            pltpu.VMEM((n_slots, LANES, ATT_T), BF16),
            pltpu.VMEM((n_slots, 1, ATT_T), F32),
            pltpu.VMEM((n_slots, V_ROWS, ATT_T), F32),
            pltpu.VMEM((n_slots * STAGE_DEPTH, ATT_T, ATT_T), F32),
        ],
        compiler_params=params,
        name="moba_attn",
    )(tab, qkv, qkv, qkv)


def _diff(qkv, tab, diff_lambda, subln_g, lambda_init):
    B, S, _ = qkv.shape
    grid, qspec, kspec, vspec, ospec, params = _attn_specs(B, S, 9, 12, 15)
    n_chains = TILES_PER_PASS * 2 * HEADS_PER_BLOCK
    return pl.pallas_call(
        functools.partial(_diff_kernel, n_blocks=S // ATT_T, scale=DIFF_QK_DIM ** -0.5,
                          lambda_init=lambda_init),
        grid=grid,
        in_specs=[
            _SMEM_SPEC,
            pl.BlockSpec(diff_lambda.shape, lambda b, h: (0, 0)),
            pl.BlockSpec((1, LANES), lambda b, h: (0, 0)),
            qspec, kspec, vspec,
        ],
        out_specs=ospec,
        out_shape=jax.ShapeDtypeStruct((B, S, MIX_W), BF16),
        scratch_shapes=[
            pltpu.VMEM((HEADS_PER_BLOCK, S, LANES), BF16),
            pltpu.VMEM((HEADS_PER_BLOCK, V_ROWS, S), BF16),
            pltpu.VMEM((1, LANES), F32),
            pltpu.VMEM((n_chains, LANES, ATT_T), BF16),
            pltpu.VMEM((n_chains, 1, ATT_T), F32),
            pltpu.VMEM((n_chains, V_ROWS, ATT_T), F32),
            pltpu.VMEM((n_chains * STAGE_DEPTH, ATT_T, ATT_T), F32),
        ],
        compiler_params=params,
        name="diff_attn",
    )(tab, diff_lambda, subln_g, qkv, qkv, qkv)


def _sb(qkv):
    B, S, _ = qkv.shape
    grid, qspec, kspec, vspec, ospec, params = _attn_specs(B, S, 18, 21, 24)
    return pl.pallas_call(
        functools.partial(_sb_kernel, n_blocks=S // ATT_T, scale=HEAD_DIM ** -0.5),
        grid=grid,
        in_specs=[qspec, kspec, vspec],
        out_specs=ospec,
        out_shape=jax.ShapeDtypeStruct((B, S, MIX_W), BF16),
        scratch_shapes=[
            pltpu.VMEM((HEADS_PER_BLOCK, HEAD_DIM, S), BF16),
            pltpu.VMEM((SB_TILES_PER_PASS * HEADS_PER_BLOCK, LANES, ATT_T), BF16),
            pltpu.VMEM((SB_TILES_PER_PASS * HEADS_PER_BLOCK, 1, ATT_T), F32),
            pltpu.VMEM((SB_TILES_PER_PASS * HEADS_PER_BLOCK, HEAD_DIM, ATT_T), F32),
        ],
        compiler_params=params,
        name="sb_attn",
    )(qkv, qkv, qkv)


def _merge_ln_kernel(x_ref, om_ref, od_ref, os_ref, wg_ref, bg_ref, wbr_ref, wo_ref, lg_ref, lb_ref,
                     o_ref, *, alpha, d_model):
    x = x_ref[...]
    xb = x.astype(BF16)
    merged = None
    for b, ob_ref in enumerate((om_ref, od_ref, os_ref)):
        glogit = _dot(xb, wg_ref[:, b * d_model:(b + 1) * d_model]) + bg_ref[b:b + 1, :]
        term = jax.nn.sigmoid(glogit) * _dot(ob_ref[...], wbr_ref[b])
        merged = term if merged is None else merged + term
    y = alpha * x + _dot(merged.astype(BF16), wo_ref[...])
    o_ref[...] = _layer_norm(y, lg_ref[...], lb_ref[...])


def _merge_ln(x, o_m, o_d, o_s, w_gate, b_gate, w_br, w_out, lg, lb, l, alpha, tm=512):
    T, D = x.shape
    return pl.pallas_call(
        functools.partial(_merge_ln_kernel, alpha=alpha, d_model=D),
        grid=(T // tm,),
        in_specs=[
            pl.BlockSpec((tm, D), lambda i: (i, 0)),
            pl.BlockSpec((tm, MIX_W), lambda i: (i, 0)),
            pl.BlockSpec((tm, MIX_W), lambda i: (i, 0)),
            pl.BlockSpec((tm, MIX_W), lambda i: (i, 0)),
            pl.BlockSpec((None, D, N_BRANCH * D), lambda i: (l, 0, 0), pipeline_mode=pl.Buffered(1)),
            pl.BlockSpec((None, N_BRANCH, D), lambda i: (l, 0, 0)),
            pl.BlockSpec((None, N_BRANCH, MIX_W, D), lambda i: (l, 0, 0, 0), pipeline_mode=pl.Buffered(1)),
            pl.BlockSpec((None, D, D), lambda i: (l, 0, 0), pipeline_mode=pl.Buffered(1)),
            pl.BlockSpec((1, D), lambda i: (0, 0)),
            pl.BlockSpec((1, D), lambda i: (0, 0)),
        ],
        out_specs=pl.BlockSpec((tm, D), lambda i: (i, 0)),
        out_shape=jax.ShapeDtypeStruct((T, D), F32),
        compiler_params=pltpu.CompilerParams(
            dimension_semantics=("parallel",), vmem_limit_bytes=VMEM_LIMIT),
        name="merge_ln",
    )(x, o_m, o_d, o_s, w_gate, b_gate, w_br, w_out, lg, lb)


def _alibi_slopes(n):
    return (2.0 ** (-8.0 * np.arange(1, n + 1, dtype=np.float32) / n)).astype(np.float32)


def _slope_table(slopes):
    s2 = slopes.astype(np.float64) * LOG2E
    return np.concatenate([s2, 1.0 / (s2 * ATT_T)]).astype(np.float32)


def kernel(x, ln_g, ln_b, ffn_w_gate, ffn_w_up, ffn_w_down, w_in, b_gate, diff_lambda, diff_subln_g,
           w_br_moba, w_br_diff, w_br_sb, w_out):
    B, S, D = x.shape
    depth = ln_g.shape[0]
    assert S % ATT_T == 0 and ATT_T == MOBA_BLOCK
    assert w_in.shape[-1] == QKV_W + N_BRANCH * D
    alpha = (2.0 * depth) ** 0.25

    wg = ffn_w_gate.astype(BF16)
    wu = ffn_w_up.astype(BF16)
    wd = ffn_w_down.astype(BF16)
    w_qkv = w_in[:, :, :QKV_W].astype(BF16)
    w_gate = w_in[:, :, QKV_W:].astype(BF16)
    w_br = jnp.stack([w_br_moba, w_br_diff, w_br_sb], axis=1).astype(BF16)
    w_out_b = w_out.astype(BF16)
    slopes = _alibi_slopes(2 * N_HEADS)
    tab_moba = jnp.asarray(_slope_table(slopes[0::2]))
    tab_diff = jnp.asarray(_slope_table(slopes[1::2]))
    subln_g = jnp.tile(diff_subln_g.astype(F32), (1, HEADS_PER_BLOCK))

    h = x.reshape(B * S, D)
    for l in range(depth):
        lg = ln_g[l][:, None, :]
        lb = ln_b[l][:, None, :]
        h, qkv = _ffn_ln(h, wg, wu, wd, lg[0], lb[0], l, 0, alpha, w_qkv=w_qkv)
        qkv = qkv.reshape(B, S, QKV_W)
        lambda_init = 0.8 - 0.6 * math.exp(-0.3 * l)
        o_m = _moba(qkv, tab_moba)
        o_d = _diff(qkv, tab_diff, diff_lambda[l], subln_g[l][None, :], lambda_init)
        o_s = _sb(qkv)
        T = B * S
        h = _merge_ln(h, o_m.reshape(T, MIX_W), o_d.reshape(T, MIX_W), o_s.reshape(T, MIX_W),
                      w_gate, b_gate, w_br, w_out_b, lg[1], lb[1], l, alpha)
        h = _ffn_ln(h, wg, wu, wd, lg[2], lb[2], l, 1, alpha)
    return h.reshape(B, S, D)
```

```python
import functools
import math

import numpy as np
import jax
import jax.numpy as jnp
from jax import lax
from jax.experimental import pallas as pl
from jax.experimental.pallas import tpu as pltpu

F32 = jnp.float32
BF16 = jnp.bfloat16

HEAD_DIM = 64
N_HEADS = 6
DIFF_QK_DIM = HEAD_DIM // 2
MOBA_BLOCK = 256
MOBA_TOPK = 3
N_BRANCH = 3
LN_EPS = 1e-5
SUBLN_EPS = 1e-5

LANES = 128
SUBLANES = 8
DENOM_ROWS = 16
V_ROWS = HEAD_DIM + DENOM_ROWS
HEADS_PER_BLOCK = LANES // HEAD_DIM
N_HEAD_BLOCKS = N_HEADS // HEADS_PER_BLOCK
MIX_W = N_HEADS * HEAD_DIM
QKV_W = 9 * MIX_W
ATT_T = 256
NEG = -1e30
SB_EXIT = -110.0
TILES_PER_PASS = 8
SB_TILES_PER_PASS = 4
LOG2E = math.log2(math.e)
N_BIAS_COLS = 3
BOUND_SLACK = 1.01
MAX_EXPONENT = 100.0
SKIP_BITS = 150.0
ITEMS_PER_TRIP = 64
LONE_HEAD_BLOCKS_PER_TRIP = 4
BUILD_UNROLL = 4
STAGE_DEPTH = 2
MERGE_ROW_SPLIT = 2
VMEM_LIMIT = 56 * 1024 * 1024


def _dot(a, b, precision=None):
    return jnp.dot(a, b, precision=precision, preferred_element_type=F32)


def _layer_norm(y, g, b):
    mu = jnp.mean(y, axis=-1, keepdims=True)
    yc = y - mu
    var = jnp.mean(yc * yc, axis=-1, keepdims=True)
    return yc * lax.rsqrt(var + LN_EPS) * g + b


def _ffn_ln_kernel(x_ref, wg_ref, wu_ref, wd_ref, lg_ref, lb_ref, o_ref, hid_ref, *, alpha, tf):
    x = x_ref[...]
    xb = x.astype(BF16)
    for c in range(hid_ref.shape[1] // tf):
        cols = slice(c * tf, (c + 1) * tf)
        gate = _dot(xb, wg_ref[:, cols])
        up = _dot(xb, wu_ref[:, cols])
        hid_ref[:, cols] = (gate * jax.nn.sigmoid(gate) * up).astype(BF16)
    y = alpha * x + 0.5 * _dot(hid_ref[...], wd_ref[...])
    o_ref[...] = _layer_norm(y, lg_ref[...], lb_ref[...])


def _ffn_ln(x, wg, wu, wd, lg, lb, l, j, alpha, tm=512, tf=256):
    T, D = x.shape
    FF = wg.shape[-1]
    resident = pl.Buffered(1)
    return pl.pallas_call(
        functools.partial(_ffn_ln_kernel, alpha=alpha, tf=tf),
        grid=(T // tm,),
        in_specs=[
            pl.BlockSpec((tm, D), lambda i: (i, 0)),
            pl.BlockSpec((None, None, D, FF), lambda i: (l, j, 0, 0), pipeline_mode=resident),
            pl.BlockSpec((None, None, D, FF), lambda i: (l, j, 0, 0), pipeline_mode=resident),
            pl.BlockSpec((None, None, FF, D), lambda i: (l, j, 0, 0), pipeline_mode=resident),
            pl.BlockSpec((1, D), lambda i: (0, 0)),
            pl.BlockSpec((1, D), lambda i: (0, 0)),
        ],
        out_specs=pl.BlockSpec((tm, D), lambda i: (i, 0)),
        out_shape=jax.ShapeDtypeStruct((T, D), F32),
        scratch_shapes=[pltpu.VMEM((tm, FF), BF16)],
        compiler_params=pltpu.CompilerParams(
            dimension_semantics=("parallel",), vmem_limit_bytes=VMEM_LIMIT),
        name="ffn_ln",
    )(x, wg, wu, wd, lg, lb)


def _proj_kernel(x_ref, w_ref, o_ref, *, tn):
    xb = x_ref[...].astype(BF16)
    for c in range(o_ref.shape[1] // tn):
        cols = slice(c * tn, (c + 1) * tn)
        o_ref[:, cols] = _dot(xb, w_ref[:, cols]).astype(o_ref.dtype)


def _qkv_proj(x, w_in, l, tm=512, tn=1152):
    T, D = x.shape
    return pl.pallas_call(
        functools.partial(_proj_kernel, tn=tn),
        grid=(T // tm,),
        in_specs=[
            pl.BlockSpec((tm, D), lambda i: (i, 0)),
            pl.BlockSpec((None, D, QKV_W), lambda i: (l, 0, 0), pipeline_mode=pl.Buffered(1)),
        ],
        out_specs=pl.BlockSpec((tm, QKV_W), lambda i: (i, 0)),
        out_shape=jax.ShapeDtypeStruct((T, QKV_W), BF16),
        compiler_params=pltpu.CompilerParams(
            dimension_semantics=("parallel",), vmem_limit_bytes=VMEM_LIMIT),
        name="qkv_proj",
    )(x, w_in)


def _lane_iota():
    return lax.broadcasted_iota(jnp.int32, (1, LANES), 1)


def _head_lanes(hh):
    lane = _lane_iota()
    return (lane >= hh * HEAD_DIM) & (lane < (hh + 1) * HEAD_DIM)


def _blk(j):
    return pl.ds(pl.multiple_of(j * ATT_T, ATT_T), ATT_T)


def _key_query_iotas():
    krow = lax.broadcasted_iota(jnp.int32, (ATT_T, ATT_T), 0)
    qcol = lax.broadcasted_iota(jnp.int32, (ATT_T, ATT_T), 1)
    return krow, qcol


def _build_softmax_operands(tab_ref, hb, k_ref, v_ref, kp_ref, vt_ref, kmax_ref, *, n_blocks, scale):
    lane = _lane_iota()
    pos = lax.broadcasted_iota(jnp.int32, (ATT_T, LANES), 0).astype(F32)
    bias_cols = []
    for hh in range(HEADS_PER_BLOCK):
        b = pos * tab_ref[hb * HEADS_PER_BLOCK + hh]
        lo = (1 - hh) * HEAD_DIM
        cols = jnp.zeros_like(b)
        for i in range(N_BIAS_COLS):
            piece = b.astype(BF16).astype(F32)
            cols = jnp.where(lane == lo + i, piece, cols)
            b = b - piece
        bias_cols.append(cols)
    kmax_ref[...] = jnp.zeros_like(kmax_ref)
    assert HEADS_PER_BLOCK == 2
    head_r = jnp.where(lax.broadcasted_iota(jnp.int32, (LANES, LANES), 0) < HEAD_DIM, 0, 1)
    head_c = jnp.where(lax.broadcasted_iota(jnp.int32, (LANES, LANES), 1) < HEAD_DIM, 0, 1)
    same_head = jnp.where(head_r == head_c, 1.0, 0.0).astype(BF16)

    def body(n, c):
        kf = k_ref[_blk(n), :].astype(F32) * (scale * LOG2E)
        v_t = v_ref[_blk(n), :].astype(F32).T.astype(BF16)
        for hh in range(HEADS_PER_BLOCK):
            kp_ref[hh, _blk(n), :] = jnp.where(_head_lanes(hh), kf, bias_cols[hh]).astype(BF16)
            vt_ref[hh, 0:HEAD_DIM, _blk(n)] = v_t[hh * HEAD_DIM:(hh + 1) * HEAD_DIM, :]
            vt_ref[hh, HEAD_DIM:V_ROWS, _blk(n)] = jnp.ones((DENOM_ROWS, ATT_T), BF16)
        norm2 = _dot((kf * kf).astype(BF16), same_head)
        kmax_ref[...] = jnp.maximum(kmax_ref[...], jnp.max(norm2, axis=0, keepdims=True))
        return c

    lax.fori_loop(0, n_blocks, body, 0, unroll=BUILD_UNROLL)


def _chain_qt(c, hh, seg_lo, seg_width, q_t, qt_ref):
    row = lax.broadcasted_iota(jnp.int32, (LANES, ATT_T), 0)
    lo = (1 - hh) * HEAD_DIM
    qsel = jnp.where((row >= seg_lo) & (row < seg_lo + seg_width), q_t, 0.0)
    qt_ref[c] = jnp.where((row >= lo) & (row < lo + N_BIAS_COLS), 1.0, qsel).astype(BF16)
    return qsel, jnp.sum(qsel * qsel, axis=0, keepdims=True)


def _stage_scores(c, hh, n, slot, kp_ref, qt_ref, sbuf_ref):
    sbuf_ref[c * STAGE_DEPTH + slot] = _dot(kp_ref[hh, _blk(jnp.maximum(n, 0)), :], qt_ref[c])


def _diag_blocks(chains, qi_of, qn2, slope2, kp_ref, vt_ref, kmax_ref, qt_ref, ref_ref, acc_ref, sbuf_ref):
    krow, qcol = _key_query_iotas()
    causal = krow <= qcol
    r = lax.broadcasted_iota(jnp.int32, (1, ATT_T), 1).astype(F32)
    scores = [_dot(kp_ref[hh, _blk(qi_of[c]), :], qt_ref[c]) for c, hh in enumerate(chains)]
    for slot in range(STAGE_DEPTH):
        for c, hh in enumerate(chains):
            _stage_scores(c, hh, qi_of[c] - 1 - slot, slot, kp_ref, qt_ref, sbuf_ref)
    gaps = []
    for c, hh in enumerate(chains):
        s = jnp.where(causal, scores[c], NEG)
        m = jnp.max(s, axis=0, keepdims=True)
        kmax2 = kmax_ref[:, hh * HEAD_DIM:hh * HEAD_DIM + 1]
        bound = jnp.sqrt(qn2[c] * kmax2) * BOUND_SLACK + slope2[hh] * r
        ref_ref[c] = m
        acc_ref[c] = _dot(vt_ref[hh, :, _blk(qi_of[c])], jnp.exp2(s - m).astype(BF16))
        gaps.append(jnp.max(bound - m))
    return gaps


def _first_block(gap, inv_blk, qi):
    reach = jnp.minimum((gap + SKIP_BITS) * inv_blk, 1e6).astype(jnp.int32)
    return jnp.maximum(qi - 1 - reach, 0)


def _fixed_ref_step(c, hh, n, slot, off, kp_ref, vt_ref, qt_ref, ref_ref, acc_ref, sbuf_ref):
    p = jnp.exp2(sbuf_ref[c * STAGE_DEPTH + slot] - (ref_ref[c] - off)).astype(BF16)
    _stage_scores(c, hh, n - STAGE_DEPTH, slot, kp_ref, qt_ref, sbuf_ref)
    acc_ref[c] += _dot(vt_ref[hh, :, _blk(jnp.maximum(n, 0))], p)


def _online_step(c, hh, n, off, kp_ref, vt_ref, qt_ref, m_ref, acc_ref):
    s = _dot(kp_ref[hh, _blk(n), :], qt_ref[c])
    m_old = m_ref[c]
    m_new = jnp.maximum(m_old, jnp.max(s, axis=0, keepdims=True) + off)
    p = jnp.exp2(s - (m_new - off)).astype(BF16)
    acc_ref[c] = acc_ref[c] * jnp.exp2(m_old - m_new) + _dot(vt_ref[hh, :, _blk(n)], p)
    m_ref[c] = m_new


def _softmax_sweep(qi_of, chains, gaps, depth, off_fn, kp_ref, vt_ref, qt_ref, ref_ref, acc_ref, sbuf_ref,
                   split_heads):
    refs = (kp_ref, vt_ref, qt_ref, ref_ref, acc_ref)
    assert STAGE_DEPTH == 2
    fixed_ok = functools.reduce(jnp.maximum, gaps) <= MAX_EXPONENT

    chains_per_tile = len(chains) // TILES_PER_PASS

    def run_fixed(start, count, heads):
        active = [(c, hh) for c, hh in enumerate(chains) if hh in heads]
        per_trip = max(STAGE_DEPTH, ITEMS_PER_TRIP // len(active))
        if len(heads) == 1:
            per_trip = min(per_trip, LONE_HEAD_BLOCKS_PER_TRIP)
        assert per_trip % STAGE_DEPTH == 0

        def step(below, slot, who):
            for c, hh in who:
                n = qi_of[c] - 1 - below
                off = off_fn(c, hh, jnp.maximum(n, 0)) + jnp.where(n < 0, NEG, 0.0)
                _fixed_ref_step(c, hh, n, slot, off, *refs, sbuf_ref)

        def sweep(start, trips, per_trip):
            def body(i, carry):
                for j in range(per_trip):
                    step(start + i * per_trip + j, j % STAGE_DEPTH, active)
                return carry

            lax.fori_loop(0, trips, body, 0)

        common = jnp.clip(qi_of[0] - start, 0, count) // STAGE_DEPTH * STAGE_DEPTH
        whole = common // per_trip
        sweep(start, whole, per_trip)
        if per_trip > STAGE_DEPTH:
            sweep(start + whole * per_trip, (common - whole * per_trip) // STAGE_DEPTH, STAGE_DEPTH)
        for k in range(0, TILES_PER_PASS, STAGE_DEPTH):
            @pl.when(k < count - common)
            def _some_depths(k=k):
                for j in range(STAGE_DEPTH):
                    who = [(c, hh) for c, hh in active if c // chains_per_tile >= k + j]
                    step(start + common + k + j, j, who)

    @pl.when(fixed_ok)
    def _fixed_reference():
        far = jnp.maximum(depth[0], depth[1])
        if split_heads:
            near = jnp.minimum(depth[0], depth[1])
            together = jnp.minimum((near + 1) // 2 * 2, far)
            run_fixed(0, together, (0, 1))
            for hh in range(HEADS_PER_BLOCK):
                run_fixed(together, jnp.maximum(depth[hh] - together, 0), (hh,))
        else:
            run_fixed(0, far, (0, 1))

    @pl.when(jnp.logical_not(fixed_ok))
    def _running_maximum():
        chains_per_head = chains_per_tile // HEADS_PER_BLOCK
        assert chains[:chains_per_tile] == [c // chains_per_head for c in range(chains_per_tile)]

        def chain_body(c, carry):
            hh = lax.div(lax.rem(c, chains_per_tile), chains_per_head)
            qi = qi_of[0] + lax.div(c, chains_per_tile)

            def body(n, carry):
                _online_step(c, hh, n, off_fn(c, hh, n, qi), *refs)
                return carry

            return lax.fori_loop(0, qi, body, carry)

        lax.fori_loop(0, len(chains), chain_body, 0)


def _normalized(acc):
    return acc[0:HEAD_DIM, :] * (1.0 / acc[HEAD_DIM:HEAD_DIM + 1, :])


def _row_is_head0():
    return lax.broadcasted_iota(jnp.int32, (LANES, ATT_T), 0) < HEAD_DIM


def _moba_kernel(tab_ref, q_ref, k_ref, v_ref, o_ref, kp_ref, vt_ref, kmax_ref, kmean_ref, selm_ref,
                 qt_ref, ref_ref, acc_ref, sbuf_ref, *, n_blocks, scale):
    hb = pl.program_id(1)
    nbp = kmean_ref.shape[0]
    _build_softmax_operands(tab_ref, hb, k_ref, v_ref, kp_ref, vt_ref, kmax_ref, n_blocks=n_blocks, scale=scale)
    kmean_ref[...] = jnp.zeros_like(kmean_ref)

    def block_mean(n, c):
        kmean_ref[pl.ds(n, 1), :] = jnp.mean(k_ref[_blk(n), :].astype(F32), axis=0, keepdims=True)
        return c

    lax.fori_loop(0, n_blocks, block_mean, 0, unroll=BUILD_UNROLL)

    blk_id = lax.broadcasted_iota(jnp.int32, (nbp, ATT_T), 0)
    blk_f = blk_id.astype(F32)
    chains = list(range(HEADS_PER_BLOCK))
    n_ch = len(chains)
    slope2 = [tab_ref[hb * HEADS_PER_BLOCK + hh] for hh in chains]
    assert n_blocks % TILES_PER_PASS == 0

    def tile_group(g, carry):
        tiles = [g * TILES_PER_PASS + s for s in range(TILES_PER_PASS)]
        qn2 = []
        for s, qi in enumerate(tiles):
            q_t = q_ref[_blk(qi), :].astype(F32).T
            for hh in chains:
                hm = _head_lanes(hh)
                qsel, norm2 = _chain_qt(s * n_ch + hh, hh, hh * HEAD_DIM, HEAD_DIM, q_t, qt_ref)
                qn2.append(norm2)

                gate = _dot(jnp.where(hm, kmean_ref[...], 0.0), qsel, precision=lax.Precision.HIGHEST)
                gt = jnp.where(blk_id < qi, gate, -jnp.inf)
                sel = jnp.zeros(gt.shape, dtype=jnp.bool_)
                for _ in range(MOBA_TOPK):
                    mx = jnp.max(gt, axis=0, keepdims=True)
                    pick = jnp.min(jnp.where(gt == mx, blk_f, float(nbp)), axis=0, keepdims=True)
                    hit = blk_f == pick
                    sel = sel | (hit & (mx > -jnp.inf))
                    gt = jnp.where(hit, -jnp.inf, gt)
                selm_ref[s * n_ch + hh] = jnp.where(sel, 0.0, NEG)

        qi_of = [tiles[c // n_ch] for c in range(TILES_PER_PASS * n_ch)]
        gaps = _diag_blocks(chains * TILES_PER_PASS, qi_of, qn2, slope2, kp_ref, vt_ref, kmax_ref, qt_ref,
                            ref_ref, acc_ref, sbuf_ref)
        all_chains = chains * TILES_PER_PASS
        depth = [None] * HEADS_PER_BLOCK
        for c, hh in enumerate(all_chains):
            d = qi_of[c] - _first_block(gaps[c], tab_ref[N_HEADS + hb * HEADS_PER_BLOCK + hh], qi_of[c])
            depth[hh] = d if depth[hh] is None else jnp.maximum(depth[hh], d)

        def off_fn(c, hh, n, qi=None):
            qi = qi_of[c] if qi is None else qi
            slope = tab_ref[hb * HEADS_PER_BLOCK + hh]
            return slope * jnp.asarray((n - qi) * ATT_T, F32) + selm_ref[c, pl.ds(n, 1), :]

        _softmax_sweep(qi_of, all_chains, gaps, depth, off_fn, kp_ref, vt_ref, qt_ref, ref_ref, acc_ref,
                       sbuf_ref, split_heads=False)
        for s, qi in enumerate(tiles):
            o = jnp.concatenate([_normalized(acc_ref[s * n_ch + hh]) for hh in chains], axis=0)
            o_ref[_blk(qi), :] = o.T.astype(o_ref.dtype)
        return carry

    lax.fori_loop(0, n_blocks // TILES_PER_PASS, tile_group, 0)


def _diff_kernel(tab_ref, lam_ref, g_ref, q_ref, k_ref, v_ref, o_ref, kp_ref, vt_ref, kmax_ref, qt_ref,
                 ref_ref, acc_ref, sbuf_ref, *, n_blocks, scale, lambda_init):
    hb = pl.program_id(1)
    _build_softmax_operands(tab_ref, hb, k_ref, v_ref, kp_ref, vt_ref, kmax_ref, n_blocks=n_blocks, scale=scale)

    lf = lam_ref[...].astype(F32)
    lam = (jnp.exp(jnp.sum(lf[0:1] * lf[1:2], axis=-1, keepdims=True))
           - jnp.exp(jnp.sum(lf[2:3] * lf[3:4], axis=-1, keepdims=True)) + lambda_init)

    chains = [hh for hh in range(HEADS_PER_BLOCK) for _ in range(2)]
    n_ch = len(chains)
    slope2 = [tab_ref[hb * HEADS_PER_BLOCK + hh] for hh in range(HEADS_PER_BLOCK)]
    assert n_blocks % TILES_PER_PASS == 0

    def tile_group(g, carry):
        tiles = [g * TILES_PER_PASS + s for s in range(TILES_PER_PASS)]
        qn2 = []
        for s, qi in enumerate(tiles):
            q_t = q_ref[_blk(qi), :].astype(F32).T
            for c, hh in enumerate(chains):
                seg_lo = hh * HEAD_DIM + (c % 2) * DIFF_QK_DIM
                qn2.append(_chain_qt(s * n_ch + c, hh, seg_lo, DIFF_QK_DIM, q_t, qt_ref)[1])
        qi_of = [tiles[c // n_ch] for c in range(TILES_PER_PASS * n_ch)]
        gaps = _diag_blocks(chains * TILES_PER_PASS, qi_of, qn2, slope2, kp_ref, vt_ref, kmax_ref, qt_ref,
                            ref_ref, acc_ref, sbuf_ref)
        all_chains = chains * TILES_PER_PASS
        depth = [None] * HEADS_PER_BLOCK
        for c, hh in enumerate(all_chains):
            d = qi_of[c] - _first_block(gaps[c], tab_ref[N_HEADS + hb * HEADS_PER_BLOCK + hh], qi_of[c])
            depth[hh] = d if depth[hh] is None else jnp.maximum(depth[hh], d)

        def off_fn(c, hh, n, qi=None):
            qi = qi_of[c] if qi is None else qi
            return tab_ref[hb * HEADS_PER_BLOCK + hh] * jnp.asarray((n - qi) * ATT_T, F32)

        _softmax_sweep(qi_of, all_chains, gaps, depth, off_fn, kp_ref, vt_ref, qt_ref, ref_ref, acc_ref,
                       sbuf_ref, split_heads=True)
        for s, qi in enumerate(tiles):
            base = s * n_ch
            outs = []
            for hh in range(HEADS_PER_BLOCK):
                o = (_normalized(acc_ref[base + 2 * hh])
                     - lam * _normalized(acc_ref[base + 2 * hh + 1]))
                ms = jnp.mean(o * o, axis=0, keepdims=True)
                outs.append(o * lax.rsqrt(ms + SUBLN_EPS))
            o = jnp.concatenate(outs, axis=0).T * (g_ref[...] * (1.0 - lambda_init))
            o_ref[_blk(qi), :] = o.astype(o_ref.dtype)
        return carry

    lax.fori_loop(0, n_blocks // TILES_PER_PASS, tile_group, 0)


def _log_sigmoids(z):
    lneg = jnp.minimum(-z, 0.0) - jnp.log(1.0 + jnp.exp(-jnp.abs(z)))
    return lneg, lneg + z


def _suffix_sums(upper, lg):
    hi = lg.astype(BF16)
    lo = (lg - hi.astype(F32)).astype(BF16)
    return _dot(upper, hi) + _dot(upper, lo)


def _sb_pass(tiles, k_ref, vt_ref, qt_ref, carry_ref, acc_ref):
    krow, qcol = _key_query_iotas()
    before = krow < qcol
    upper = (qcol > krow).astype(BF16)
    items = [(slot * HEADS_PER_BLOCK + hh, hh, n, diag)
             for slot, blocks in tiles for n, diag in blocks for hh in range(HEADS_PER_BLOCK)]
    z = [_dot(k_ref[_blk(n), :], qt_ref[st]) for st, _, n, _ in items]
    log_sig, log_surv, tails = [], [], []
    for i, (st, hh, n, diag) in enumerate(items):
        lneg, lpos = _log_sigmoids(jnp.where(before, z[i], NEG) if diag else z[i])
        log_surv.append(lneg)
        log_sig.append(lpos)
        tails.append(_suffix_sums(upper, lneg))
    carry, acc = {}, {}
    for slot, blocks in tiles:
        for hh in range(HEADS_PER_BLOCK):
            st = slot * HEADS_PER_BLOCK + hh
            starts_fresh = blocks[0][1]
            carry[st] = None if starts_fresh else carry_ref[st]
            acc[st] = None if starts_fresh else acc_ref[st]
    for i, (st, hh, n, diag) in enumerate(items):
        w = jnp.exp(log_sig[i] + (tails[i] if diag else tails[i] + carry[st]))
        part = _dot(vt_ref[hh, :, _blk(n)], w.astype(BF16))
        acc[st] = part if acc[st] is None else acc[st] + part
        total = jnp.sum(log_surv[i], axis=0, keepdims=True)
        carry[st] = total if carry[st] is None else carry[st] + total
    worst = []
    for slot, _ in tiles:
        tops = []
        for hh in range(HEADS_PER_BLOCK):
            st = slot * HEADS_PER_BLOCK + hh
            acc_ref[st] = acc[st]
            carry_ref[st] = carry[st]
            tops.append(jnp.max(carry[st]))
        worst.append(functools.reduce(jnp.maximum, tops))
    return tuple(worst)


def _sb_kernel(q_ref, k_ref, v_ref, o_ref, vt_ref, qt_ref, carry_ref, acc_ref, *, n_blocks, scale):
    def transpose_values(n, c):
        v_t = v_ref[_blk(n), :].astype(F32).T.astype(BF16)
        for hh in range(HEADS_PER_BLOCK):
            vt_ref[hh, :, _blk(n)] = v_t[hh * HEAD_DIM:(hh + 1) * HEAD_DIM, :]
        return c

    lax.fori_loop(0, n_blocks, transpose_values, 0, unroll=BUILD_UNROLL)

    assert n_blocks % SB_TILES_PER_PASS == 0
    refs = (k_ref, vt_ref, qt_ref, carry_ref, acc_ref)
    row_h0 = _row_is_head0()

    def tile_group(g, carry):
        tiles = [g * SB_TILES_PER_PASS + slot for slot in range(SB_TILES_PER_PASS)]
        for slot, qi in enumerate(tiles):
            q_t = (q_ref[_blk(qi), :].astype(F32) * scale).T
            for hh in range(HEADS_PER_BLOCK):
                in_head = row_h0 if hh == 0 else jnp.logical_not(row_h0)
                qt_ref[slot * HEADS_PER_BLOCK + hh] = jnp.where(in_head, q_t, 0.0).astype(BF16)

        def blocks_of(qi, with_previous):
            return [(qi, True), (qi - 1, False)] if with_previous else [(qi, True)]

        worst = lax.cond(
            g > 0,
            lambda: _sb_pass([(slot, blocks_of(qi, True)) for slot, qi in enumerate(tiles)], *refs),
            lambda: _sb_pass([(slot, blocks_of(qi, slot > 0)) for slot, qi in enumerate(tiles)], *refs))

        for slot, qi in enumerate(tiles):
            def cond(c):
                j, live = c
                return (j >= 0) & live

            def body(c, slot=slot):
                j, _ = c
                return j - 1, _sb_pass([(slot, [(j, False)])], *refs)[0] > SB_EXIT

            lax.while_loop(cond, body, (qi - 2, worst[slot] > SB_EXIT))
            st = slot * HEADS_PER_BLOCK
            o = jnp.concatenate([acc_ref[st + hh] for hh in range(HEADS_PER_BLOCK)], axis=0)
            o_ref[_blk(qi), :] = o.T.astype(o_ref.dtype)
        return carry

    lax.fori_loop(0, n_blocks // SB_TILES_PER_PASS, tile_group, 0)


def _attn_specs(B, S, q_base, k_base, v_base):
    qspec = pl.BlockSpec((None, S, LANES), lambda b, h: (b, 0, q_base + h))
    kspec = pl.BlockSpec((None, S, LANES), lambda b, h: (b, 0, k_base + h))
    vspec = pl.BlockSpec((None, S, LANES), lambda b, h: (b, 0, v_base + h))
    ospec = pl.BlockSpec((None, S, LANES), lambda b, h: (b, 0, h))
    grid = (B, N_HEAD_BLOCKS)
    params = pltpu.CompilerParams(
        dimension_semantics=("parallel", "parallel"), vmem_limit_bytes=VMEM_LIMIT)
    return grid, qspec, kspec, vspec, ospec, params


_SMEM_SPEC = pl.BlockSpec(memory_space=pltpu.SMEM)


def _moba(qkv, tab):
    B, S, _ = qkv.shape
    grid, qspec, kspec, vspec, ospec, params = _attn_specs(B, S, 0, 3, 6)
    n_blocks = S // MOBA_BLOCK
    nbp = -(-n_blocks // SUBLANES) * SUBLANES
    n_slots = TILES_PER_PASS * HEADS_PER_BLOCK
    return pl.pallas_call(
        functools.partial(_moba_kernel, n_blocks=n_blocks, scale=HEAD_DIM ** -0.5),
        grid=grid,
        in_specs=[_SMEM_SPEC, qspec, kspec, vspec],
        out_specs=ospec,
        out_shape=jax.ShapeDtypeStruct((B, S, MIX_W), BF16),
        scratch_shapes=[
            pltpu.VMEM((HEADS_PER_BLOCK, S, LANES), BF16),
            pltpu.VMEM((HEADS_PER_BLOCK, V_ROWS, S), BF16),
            pltpu.VMEM((1, LANES), F32),
            pltpu.VMEM((nbp, LANES), F32),
            pltpu.VMEM((n_slots, nbp, ATT_T), F32),
            pltpu.VMEM((n_slots, LANES, ATT_T), BF16),
            pltpu.VMEM((n_slots, 1, ATT_T), F32),
            pltpu.VMEM((n_slots, V_ROWS, ATT_T), F32),
            pltpu.VMEM((n_slots * STAGE_DEPTH, ATT_T, ATT_T), F32),
        ],
        compiler_params=params,
        name="moba_attn",
    )(tab, qkv, qkv, qkv)


def _diff(qkv, tab, diff_lambda, subln_g, lambda_init):
    B, S, _ = qkv.shape
    grid, qspec, kspec, vspec, ospec, params = _attn_specs(B, S, 9, 12, 15)
    n_chains = TILES_PER_PASS * 2 * HEADS_PER_BLOCK
    return pl.pallas_call(
        functools.partial(_diff_kernel, n_blocks=S // ATT_T, scale=DIFF_QK_DIM ** -0.5,
                          lambda_init=lambda_init),
        grid=grid,
        in_specs=[
            _SMEM_SPEC,
            pl.BlockSpec(diff_lambda.shape, lambda b, h: (0, 0)),
            pl.BlockSpec((1, LANES), lambda b, h: (0, 0)),
            qspec, kspec, vspec,
        ],
        out_specs=ospec,
        out_shape=jax.ShapeDtypeStruct((B, S, MIX_W), BF16),
        scratch_shapes=[
            pltpu.VMEM((HEADS_PER_BLOCK, S, LANES), BF16),
            pltpu.VMEM((HEADS_PER_BLOCK, V_ROWS, S), BF16),
            pltpu.VMEM((1, LANES), F32),
            pltpu.VMEM((n_chains, LANES, ATT_T), BF16),
            pltpu.VMEM((n_chains, 1, ATT_T), F32),
            pltpu.VMEM((n_chains, V_ROWS, ATT_T), F32),
            pltpu.VMEM((n_chains * STAGE_DEPTH, ATT_T, ATT_T), F32),
        ],
        compiler_params=params,
        name="diff_attn",
    )(tab, diff_lambda, subln_g, qkv, qkv, qkv)


def _sb(qkv):
    B, S, _ = qkv.shape
    grid, qspec, kspec, vspec, ospec, params = _attn_specs(B, S, 18, 21, 24)
    return pl.pallas_call(
        functools.partial(_sb_kernel, n_blocks=S // ATT_T, scale=HEAD_DIM ** -0.5),
        grid=grid,
        in_specs=[qspec, kspec, vspec],
        out_specs=ospec,
        out_shape=jax.ShapeDtypeStruct((B, S, MIX_W), BF16),
        scratch_shapes=[
            pltpu.VMEM((HEADS_PER_BLOCK, HEAD_DIM, S), BF16),
            pltpu.VMEM((SB_TILES_PER_PASS * HEADS_PER_BLOCK, LANES, ATT_T), BF16),
            pltpu.VMEM((SB_TILES_PER_PASS * HEADS_PER_BLOCK, 1, ATT_T), F32),
            pltpu.VMEM((SB_TILES_PER_PASS * HEADS_PER_BLOCK, HEAD_DIM, ATT_T), F32),
        ],
        compiler_params=params,
        name="sb_attn",
    )(qkv, qkv, qkv)


def _merge_ln_kernel(x_ref, om_ref, od_ref, os_ref, wg_ref, bg_ref, wbr_ref, wo_ref, lg_ref, lb_ref,
                     o_ref, *, alpha, d_model):
    half = x_ref.shape[0] // MERGE_ROW_SPLIT
    merged = []
    for r in range(MERGE_ROW_SPLIT):
        rows = slice(r * half, (r + 1) * half)
        xb = x_ref[rows, :].astype(BF16)
        acc = None
        for b, ob_ref in enumerate((om_ref, od_ref, os_ref)):
            glogit = _dot(xb, wg_ref[:, b * d_model:(b + 1) * d_model]) + bg_ref[b:b + 1, :]
            term = jax.nn.sigmoid(glogit) * _dot(ob_ref[rows, :], wbr_ref[b])
            acc = term if acc is None else acc + term
        merged.append(acc.astype(BF16))
    for r in range(MERGE_ROW_SPLIT):
        rows = slice(r * half, (r + 1) * half)
        y = alpha * x_ref[rows, :] + _dot(merged[r], wo_ref[...])
        o_ref[rows, :] = _layer_norm(y, lg_ref[...], lb_ref[...])


def _merge_ln(x, o_m, o_d, o_s, w_gate, b_gate, w_br, w_out, lg, lb, l, alpha, tm=512):
    T, D = x.shape
    return pl.pallas_call(
        functools.partial(_merge_ln_kernel, alpha=alpha, d_model=D),
        grid=(T // tm,),
        in_specs=[
            pl.BlockSpec((tm, D), lambda i: (i, 0)),
            pl.BlockSpec((tm, MIX_W), lambda i: (i, 0)),
            pl.BlockSpec((tm, MIX_W), lambda i: (i, 0)),
            pl.BlockSpec((tm, MIX_W), lambda i: (i, 0)),
            pl.BlockSpec((None, D, N_BRANCH * D), lambda i: (l, 0, 0), pipeline_mode=pl.Buffered(1)),
            pl.BlockSpec((None, N_BRANCH, D), lambda i: (l, 0, 0)),
            pl.BlockSpec((None, N_BRANCH, MIX_W, D), lambda i: (l, 0, 0, 0), pipeline_mode=pl.Buffered(1)),
            pl.BlockSpec((None, D, D), lambda i: (l, 0, 0), pipeline_mode=pl.Buffered(1)),
            pl.BlockSpec((1, D), lambda i: (0, 0)),
            pl.BlockSpec((1, D), lambda i: (0, 0)),
        ],
        out_specs=pl.BlockSpec((tm, D), lambda i: (i, 0)),
        out_shape=jax.ShapeDtypeStruct((T, D), F32),
        compiler_params=pltpu.CompilerParams(
            dimension_semantics=("parallel",), vmem_limit_bytes=VMEM_LIMIT),
        name="merge_ln",
    )(x, o_m, o_d, o_s, w_gate, b_gate, w_br, w_out, lg, lb)


def _alibi_slopes(n):
    return (2.0 ** (-8.0 * np.arange(1, n + 1, dtype=np.float32) / n)).astype(np.float32)


def _slope_table(slopes):
    s2 = slopes.astype(np.float64) * LOG2E
    return np.concatenate([s2, 1.0 / (s2 * ATT_T)]).astype(np.float32)


def kernel(x, ln_g, ln_b, ffn_w_gate, ffn_w_up, ffn_w_down, w_in, b_gate, diff_lambda, diff_subln_g,
           w_br_moba, w_br_diff, w_br_sb, w_out):
    B, S, D = x.shape
    depth = ln_g.shape[0]
    assert S % ATT_T == 0 and ATT_T == MOBA_BLOCK
    assert w_in.shape[-1] == QKV_W + N_BRANCH * D
    alpha = (2.0 * depth) ** 0.25

    wg = ffn_w_gate.astype(BF16)
    wu = ffn_w_up.astype(BF16)
    wd = ffn_w_down.astype(BF16)
    w_qkv = w_in[:, :, :QKV_W].astype(BF16)
    w_gate = w_in[:, :, QKV_W:].astype(BF16)
    w_br = jnp.stack([w_br_moba, w_br_diff, w_br_sb], axis=1).astype(BF16)
    w_out_b = w_out.astype(BF16)
    slopes = _alibi_slopes(2 * N_HEADS)
    tab_moba = jnp.asarray(_slope_table(slopes[0::2]))
    tab_diff = jnp.asarray(_slope_table(slopes[1::2]))
    subln_g = jnp.tile(diff_subln_g.astype(F32), (1, HEADS_PER_BLOCK))

    h = x.reshape(B * S, D)
    for l in range(depth):
        lg = ln_g[l][:, None, :]
        lb = ln_b[l][:, None, :]
        h = _ffn_ln(h, wg, wu, wd, lg[0], lb[0], l, 0, alpha)
        qkv = _qkv_proj(h, w_qkv, l).reshape(B, S, QKV_W)
        lambda_init = 0.8 - 0.6 * math.exp(-0.3 * l)
        o_m = _moba(qkv, tab_moba)
        o_d = _diff(qkv, tab_diff, diff_lambda[l], subln_g[l][None, :], lambda_init)
        o_s = _sb(qkv)
        T = B * S
        h = _merge_ln(h, o_m.reshape(T, MIX_W), o_d.reshape(T, MIX_W), o_s.reshape(T, MIX_W),
                      w_gate, b_gate, w_br, w_out_b, lg[1], lb[1], l, alpha)
        h = _ffn_ln(h, wg, wu, wd, lg[2], lb[2], l, 1, alpha)
    return h.reshape(B, S, D)
```

```python
import functools
import math

import numpy as np
import jax
import jax.numpy as jnp
from jax import lax
from jax.experimental import pallas as pl
from jax.experimental.pallas import tpu as pltpu

F32 = jnp.float32
BF16 = jnp.bfloat16

HEAD_DIM = 64
N_HEADS = 6
DIFF_QK_DIM = HEAD_DIM // 2
MOBA_BLOCK = 256
MOBA_TOPK = 3
N_BRANCH = 3
LN_EPS = 1e-5
SUBLN_EPS = 1e-5

LANES = 128
SUBLANES = 8
DENOM_ROWS = 16
V_ROWS = HEAD_DIM + DENOM_ROWS
HEADS_PER_BLOCK = LANES // HEAD_DIM
N_HEAD_BLOCKS = N_HEADS // HEADS_PER_BLOCK
MIX_W = N_HEADS * HEAD_DIM
QKV_W = 9 * MIX_W
ATT_T = 256
NEG = -1e30
SB_EXIT = -110.0
TILES_PER_PASS = 8
SB_TILES_PER_PASS = 4
LOG2E = math.log2(math.e)
N_BIAS_COLS = 3
BOUND_SLACK = 1.01
MAX_EXPONENT = 100.0
SKIP_BITS = 150.0
ITEMS_PER_TRIP = 64
LONE_HEAD_BLOCKS_PER_TRIP = 4
BUILD_UNROLL = 4
STAGE_DEPTH = 2
MERGE_ROW_SPLIT = 2
VMEM_LIMIT = 56 * 1024 * 1024


def _dot(a, b, precision=None):
    return jnp.dot(a, b, precision=precision, preferred_element_type=F32)


def _layer_norm(y, g, b):
    mu = jnp.mean(y, axis=-1, keepdims=True)
    yc = y - mu
    var = jnp.mean(yc * yc, axis=-1, keepdims=True)
    return yc * lax.rsqrt(var + LN_EPS) * g + b


def _ffn_ln_kernel(x_ref, wg_ref, wu_ref, wd_ref, lg_ref, lb_ref, o_ref, hid_ref, *, alpha, tf):
    xb = x_ref[...].astype(BF16)
    for c in range(hid_ref.shape[1] // tf):
        cols = slice(c * tf, (c + 1) * tf)
        gate = _dot(xb, wg_ref[:, cols])
        up = _dot(xb, wu_ref[:, cols])
        hid_ref[:, cols] = (gate * jax.nn.sigmoid(gate) * up).astype(BF16)
    half = x_ref.shape[0] // 2
    for r in range(2):
        rows = slice(r * half, (r + 1) * half)
        y = alpha * x_ref[rows, :] + 0.5 * _dot(hid_ref[rows, :], wd_ref[...])
        o_ref[rows, :] = _layer_norm(y, lg_ref[...], lb_ref[...])


def _ffn_ln(x, wg, wu, wd, lg, lb, l, j, alpha, tm=512, tf=256):
    T, D = x.shape
    FF = wg.shape[-1]
    resident = pl.Buffered(1)
    return pl.pallas_call(
        functools.partial(_ffn_ln_kernel, alpha=alpha, tf=tf),
        grid=(T // tm,),
        in_specs=[
            pl.BlockSpec((tm, D), lambda i: (i, 0)),
            pl.BlockSpec((None, None, D, FF), lambda i: (l, j, 0, 0), pipeline_mode=resident),
            pl.BlockSpec((None, None, D, FF), lambda i: (l, j, 0, 0), pipeline_mode=resident),
            pl.BlockSpec((None, None, FF, D), lambda i: (l, j, 0, 0), pipeline_mode=resident),
            pl.BlockSpec((1, D), lambda i: (0, 0)),
            pl.BlockSpec((1, D), lambda i: (0, 0)),
        ],
        out_specs=pl.BlockSpec((tm, D), lambda i: (i, 0)),
        out_shape=jax.ShapeDtypeStruct((T, D), F32),
        scratch_shapes=[pltpu.VMEM((tm, FF), BF16)],
        compiler_params=pltpu.CompilerParams(
            dimension_semantics=("parallel",), vmem_limit_bytes=VMEM_LIMIT),
        name="ffn_ln",
    )(x, wg, wu, wd, lg, lb)


def _proj_kernel(x_ref, w_ref, o_ref, *, tn):
    xb = x_ref[...].astype(BF16)
    for c in range(o_ref.shape[1] // tn):
        cols = slice(c * tn, (c + 1) * tn)
        o_ref[:, cols] = _dot(xb, w_ref[:, cols]).astype(o_ref.dtype)


def _qkv_proj(x, w_in, l, tm=512, tn=1152):
    T, D = x.shape
    return pl.pallas_call(
        functools.partial(_proj_kernel, tn=tn),
        grid=(T // tm,),
        in_specs=[
            pl.BlockSpec((tm, D), lambda i: (i, 0)),
            pl.BlockSpec((None, D, QKV_W), lambda i: (l, 0, 0), pipeline_mode=pl.Buffered(1)),
        ],
        out_specs=pl.BlockSpec((tm, QKV_W), lambda i: (i, 0)),
        out_shape=jax.ShapeDtypeStruct((T, QKV_W), BF16),
        compiler_params=pltpu.CompilerParams(
            dimension_semantics=("parallel",), vmem_limit_bytes=VMEM_LIMIT),
        name="qkv_proj",
    )(x, w_in)


def _lane_iota():
    return lax.broadcasted_iota(jnp.int32, (1, LANES), 1)


def _head_lanes(hh):
    lane = _lane_iota()
    return (lane >= hh * HEAD_DIM) & (lane < (hh + 1) * HEAD_DIM)


def _blk(j):
    return pl.ds(pl.multiple_of(j * ATT_T, ATT_T), ATT_T)


def _key_query_iotas():
    krow = lax.broadcasted_iota(jnp.int32, (ATT_T, ATT_T), 0)
    qcol = lax.broadcasted_iota(jnp.int32, (ATT_T, ATT_T), 1)
    return krow, qcol


def _build_softmax_operands(tab_ref, hb, k_ref, v_ref, kp_ref, vt_ref, kmax_ref, *, n_blocks, scale):
    lane = _lane_iota()
    pos = lax.broadcasted_iota(jnp.int32, (ATT_T, LANES), 0).astype(F32)
    bias_cols = []
    for hh in range(HEADS_PER_BLOCK):
        b = pos * tab_ref[hb * HEADS_PER_BLOCK + hh]
        lo = (1 - hh) * HEAD_DIM
        cols = jnp.zeros_like(b)
        for i in range(N_BIAS_COLS):
            piece = b.astype(BF16).astype(F32)
            cols = jnp.where(lane == lo + i, piece, cols)
            b = b - piece
        bias_cols.append(cols)
    kmax_ref[...] = jnp.zeros_like(kmax_ref)
    assert HEADS_PER_BLOCK == 2
    head_r = jnp.where(lax.broadcasted_iota(jnp.int32, (LANES, LANES), 0) < HEAD_DIM, 0, 1)
    head_c = jnp.where(lax.broadcasted_iota(jnp.int32, (LANES, LANES), 1) < HEAD_DIM, 0, 1)
    same_head = jnp.where(head_r == head_c, 1.0, 0.0).astype(BF16)

    def body(n, c):
        kf = k_ref[_blk(n), :].astype(F32) * (scale * LOG2E)
        v_t = v_ref[_blk(n), :].astype(F32).T.astype(BF16)
        for hh in range(HEADS_PER_BLOCK):
            kp_ref[hh, _blk(n), :] = jnp.where(_head_lanes(hh), kf, bias_cols[hh]).astype(BF16)
            vt_ref[hh, 0:HEAD_DIM, _blk(n)] = v_t[hh * HEAD_DIM:(hh + 1) * HEAD_DIM, :]
            vt_ref[hh, HEAD_DIM:V_ROWS, _blk(n)] = jnp.ones((DENOM_ROWS, ATT_T), BF16)
        norm2 = _dot((kf * kf).astype(BF16), same_head)
        kmax_ref[...] = jnp.maximum(kmax_ref[...], jnp.max(norm2, axis=0, keepdims=True))
        return c

    lax.fori_loop(0, n_blocks, body, 0, unroll=BUILD_UNROLL)


def _chain_qt(c, hh, seg_lo, seg_width, q_t, qt_ref):
    row = lax.broadcasted_iota(jnp.int32, (LANES, ATT_T), 0)
    lo = (1 - hh) * HEAD_DIM
    qsel = jnp.where((row >= seg_lo) & (row < seg_lo + seg_width), q_t, 0.0)
    qt_ref[c] = jnp.where((row >= lo) & (row < lo + N_BIAS_COLS), 1.0, qsel).astype(BF16)
    return qsel, jnp.sum(qsel * qsel, axis=0, keepdims=True)


def _stage_scores(c, hh, n, slot, kp_ref, qt_ref, sbuf_ref):
    sbuf_ref[c * STAGE_DEPTH + slot] = _dot(kp_ref[hh, _blk(jnp.maximum(n, 0)), :], qt_ref[c])


def _diag_blocks(chains, qi_of, qn2, slope2, kp_ref, vt_ref, kmax_ref, qt_ref, ref_ref, acc_ref, sbuf_ref):
    krow, qcol = _key_query_iotas()
    causal = krow <= qcol
    r = lax.broadcasted_iota(jnp.int32, (1, ATT_T), 1).astype(F32)
    scores = [_dot(kp_ref[hh, _blk(qi_of[c]), :], qt_ref[c]) for c, hh in enumerate(chains)]
    for slot in range(STAGE_DEPTH):
        for c, hh in enumerate(chains):
            _stage_scores(c, hh, qi_of[c] - 1 - slot, slot, kp_ref, qt_ref, sbuf_ref)
    gaps = []
    for c, hh in enumerate(chains):
        s = jnp.where(causal, scores[c], NEG)
        m = jnp.max(s, axis=0, keepdims=True)
        kmax2 = kmax_ref[:, hh * HEAD_DIM:hh * HEAD_DIM + 1]
        bound = jnp.sqrt(qn2[c] * kmax2) * BOUND_SLACK + slope2[hh] * r
        ref_ref[c] = m
        acc_ref[c] = _dot(vt_ref[hh, :, _blk(qi_of[c])], jnp.exp2(s - m).astype(BF16))
        gaps.append(jnp.max(bound - m))
    return gaps


def _first_block(gap, inv_blk, qi):
    reach = jnp.minimum((gap + SKIP_BITS) * inv_blk, 1e6).astype(jnp.int32)
    return jnp.maximum(qi - 1 - reach, 0)


def _fixed_ref_step(c, hh, n, slot, off, kp_ref, vt_ref, qt_ref, ref_ref, acc_ref, sbuf_ref):
    p = jnp.exp2(sbuf_ref[c * STAGE_DEPTH + slot] - (ref_ref[c] - off)).astype(BF16)
    _stage_scores(c, hh, n - STAGE_DEPTH, slot, kp_ref, qt_ref, sbuf_ref)
    acc_ref[c] += _dot(vt_ref[hh, :, _blk(jnp.maximum(n, 0))], p)


def _online_step(c, hh, n, off, kp_ref, vt_ref, qt_ref, m_ref, acc_ref):
    s = _dot(kp_ref[hh, _blk(n), :], qt_ref[c])
    m_old = m_ref[c]
    m_new = jnp.maximum(m_old, jnp.max(s, axis=0, keepdims=True) + off)
    p = jnp.exp2(s - (m_new - off)).astype(BF16)
    acc_ref[c] = acc_ref[c] * jnp.exp2(m_old - m_new) + _dot(vt_ref[hh, :, _blk(n)], p)
    m_ref[c] = m_new


def _softmax_sweep(qi_of, chains, gaps, depth, off_fn, kp_ref, vt_ref, qt_ref, ref_ref, acc_ref, sbuf_ref,
                   split_heads):
    refs = (kp_ref, vt_ref, qt_ref, ref_ref, acc_ref)
    assert STAGE_DEPTH == 2
    fixed_ok = functools.reduce(jnp.maximum, gaps) <= MAX_EXPONENT

    chains_per_tile = len(chains) // TILES_PER_PASS

    def run_fixed(start, count, heads):
        active = [(c, hh) for c, hh in enumerate(chains) if hh in heads]
        per_trip = max(STAGE_DEPTH, ITEMS_PER_TRIP // len(active))
        if len(heads) == 1:
            per_trip = min(per_trip, LONE_HEAD_BLOCKS_PER_TRIP)
        assert per_trip % STAGE_DEPTH == 0

        def step(below, slot, who):
            for c, hh in who:
                n = qi_of[c] - 1 - below
                off = off_fn(c, hh, jnp.maximum(n, 0)) + jnp.where(n < 0, NEG, 0.0)
                _fixed_ref_step(c, hh, n, slot, off, *refs, sbuf_ref)

        def sweep(start, trips, per_trip):
            def body(i, carry):
                for j in range(per_trip):
                    step(start + i * per_trip + j, j % STAGE_DEPTH, active)
                return carry

            lax.fori_loop(0, trips, body, 0)

        common = jnp.clip(qi_of[0] - start, 0, count) // STAGE_DEPTH * STAGE_DEPTH
        whole = common // per_trip
        sweep(start, whole, per_trip)
        if per_trip > STAGE_DEPTH:
            sweep(start + whole * per_trip, (common - whole * per_trip) // STAGE_DEPTH, STAGE_DEPTH)
        for k in range(0, TILES_PER_PASS, STAGE_DEPTH):
            @pl.when(k < count - common)
            def _some_depths(k=k):
                for j in range(STAGE_DEPTH):
                    who = [(c, hh) for c, hh in active if c // chains_per_tile >= k + j]
                    step(start + common + k + j, j, who)

    @pl.when(fixed_ok)
    def _fixed_reference():
        far = jnp.maximum(depth[0], depth[1])
        if split_heads:
            near = jnp.minimum(depth[0], depth[1])
            together = jnp.minimum((near + 1) // 2 * 2, far)
            run_fixed(0, together, (0, 1))
            for hh in range(HEADS_PER_BLOCK):
                run_fixed(together, jnp.maximum(depth[hh] - together, 0), (hh,))
        else:
            run_fixed(0, far, (0, 1))

    @pl.when(jnp.logical_not(fixed_ok))
    def _running_maximum():
        chains_per_head = chains_per_tile // HEADS_PER_BLOCK
        assert chains[:chains_per_tile] == [c // chains_per_head for c in range(chains_per_tile)]

        def chain_body(c, carry):
            hh = lax.div(lax.rem(c, chains_per_tile), chains_per_head)
            qi = qi_of[0] + lax.div(c, chains_per_tile)

            def body(n, carry):
                _online_step(c, hh, n, off_fn(c, hh, n, qi), *refs)
                return carry

            return lax.fori_loop(0, qi, body, carry)

        lax.fori_loop(0, len(chains), chain_body, 0)


def _normalized(acc):
    return acc[0:HEAD_DIM, :] * (1.0 / acc[HEAD_DIM:HEAD_DIM + 1, :])


def _row_is_head0():
    return lax.broadcasted_iota(jnp.int32, (LANES, ATT_T), 0) < HEAD_DIM


def _moba_kernel(tab_ref, q_ref, k_ref, v_ref, o_ref, kp_ref, vt_ref, kmax_ref, kmean_ref, selm_ref,
                 qt_ref, ref_ref, acc_ref, sbuf_ref, *, n_blocks, scale):
    hb = pl.program_id(1)
    nbp = kmean_ref.shape[0]
    _build_softmax_operands(tab_ref, hb, k_ref, v_ref, kp_ref, vt_ref, kmax_ref, n_blocks=n_blocks, scale=scale)
    kmean_ref[...] = jnp.zeros_like(kmean_ref)

    def block_mean(n, c):
        kmean_ref[pl.ds(n, 1), :] = jnp.mean(k_ref[_blk(n), :].astype(F32), axis=0, keepdims=True)
        return c

    lax.fori_loop(0, n_blocks, block_mean, 0, unroll=BUILD_UNROLL)

    blk_id = lax.broadcasted_iota(jnp.int32, (nbp, ATT_T), 0)
    blk_f = blk_id.astype(F32)
    chains = list(range(HEADS_PER_BLOCK))
    n_ch = len(chains)
    slope2 = [tab_ref[hb * HEADS_PER_BLOCK + hh] for hh in chains]
    assert n_blocks % TILES_PER_PASS == 0

    def tile_group(g, carry):
        tiles = [g * TILES_PER_PASS + s for s in range(TILES_PER_PASS)]
        qn2 = []
        for s, qi in enumerate(tiles):
            q_t = q_ref[_blk(qi), :].astype(F32).T
            for hh in chains:
                hm = _head_lanes(hh)
                qsel, norm2 = _chain_qt(s * n_ch + hh, hh, hh * HEAD_DIM, HEAD_DIM, q_t, qt_ref)
                qn2.append(norm2)

                gate = _dot(jnp.where(hm, kmean_ref[...], 0.0), qsel, precision=lax.Precision.HIGHEST)
                gt = jnp.where(blk_id < qi, gate, -jnp.inf)
                sel = jnp.zeros(gt.shape, dtype=jnp.bool_)
                for _ in range(MOBA_TOPK):
                    mx = jnp.max(gt, axis=0, keepdims=True)
                    pick = jnp.min(jnp.where(gt == mx, blk_f, float(nbp)), axis=0, keepdims=True)
                    hit = blk_f == pick
                    sel = sel | (hit & (mx > -jnp.inf))
                    gt = jnp.where(hit, -jnp.inf, gt)
                selm_ref[s * n_ch + hh] = jnp.where(sel, 0.0, NEG)

        qi_of = [tiles[c // n_ch] for c in range(TILES_PER_PASS * n_ch)]
        gaps = _diag_blocks(chains * TILES_PER_PASS, qi_of, qn2, slope2, kp_ref, vt_ref, kmax_ref, qt_ref,
                            ref_ref, acc_ref, sbuf_ref)
        all_chains = chains * TILES_PER_PASS
        depth = [None] * HEADS_PER_BLOCK
        for c, hh in enumerate(all_chains):
            d = qi_of[c] - _first_block(gaps[c], tab_ref[N_HEADS + hb * HEADS_PER_BLOCK + hh], qi_of[c])
            depth[hh] = d if depth[hh] is None else jnp.maximum(depth[hh], d)

        def off_fn(c, hh, n, qi=None):
            qi = qi_of[c] if qi is None else qi
            slope = tab_ref[hb * HEADS_PER_BLOCK + hh]
            return slope * jnp.asarray((n - qi) * ATT_T, F32) + selm_ref[c, pl.ds(n, 1), :]

        _softmax_sweep(qi_of, all_chains, gaps, depth, off_fn, kp_ref, vt_ref, qt_ref, ref_ref, acc_ref,
                       sbuf_ref, split_heads=False)
        for s, qi in enumerate(tiles):
            o = jnp.concatenate([_normalized(acc_ref[s * n_ch + hh]) for hh in chains], axis=0)
            o_ref[_blk(qi), :] = o.T.astype(o_ref.dtype)
        return carry

    lax.fori_loop(0, n_blocks // TILES_PER_PASS, tile_group, 0)


def _diff_kernel(tab_ref, lam_ref, g_ref, q_ref, k_ref, v_ref, o_ref, kp_ref, vt_ref, kmax_ref, qt_ref,
                 ref_ref, acc_ref, sbuf_ref, *, n_blocks, scale, lambda_init):
    hb = pl.program_id(1)
    _build_softmax_operands(tab_ref, hb, k_ref, v_ref, kp_ref, vt_ref, kmax_ref, n_blocks=n_blocks, scale=scale)

    lf = lam_ref[...].astype(F32)
    lam = (jnp.exp(jnp.sum(lf[0:1] * lf[1:2], axis=-1, keepdims=True))
           - jnp.exp(jnp.sum(lf[2:3] * lf[3:4], axis=-1, keepdims=True)) + lambda_init)

    chains = [hh for hh in range(HEADS_PER_BLOCK) for _ in range(2)]
    n_ch = len(chains)
    slope2 = [tab_ref[hb * HEADS_PER_BLOCK + hh] for hh in range(HEADS_PER_BLOCK)]
    assert n_blocks % TILES_PER_PASS == 0

    def tile_group(g, carry):
        tiles = [g * TILES_PER_PASS + s for s in range(TILES_PER_PASS)]
        qn2 = []
        for s, qi in enumerate(tiles):
            q_t = q_ref[_blk(qi), :].astype(F32).T
            for c, hh in enumerate(chains):
                seg_lo = hh * HEAD_DIM + (c % 2) * DIFF_QK_DIM
                qn2.append(_chain_qt(s * n_ch + c, hh, seg_lo, DIFF_QK_DIM, q_t, qt_ref)[1])
        qi_of = [tiles[c // n_ch] for c in range(TILES_PER_PASS * n_ch)]
        gaps = _diag_blocks(chains * TILES_PER_PASS, qi_of, qn2, slope2, kp_ref, vt_ref, kmax_ref, qt_ref,
                            ref_ref, acc_ref, sbuf_ref)
        all_chains = chains * TILES_PER_PASS
        depth = [None] * HEADS_PER_BLOCK
        for c, hh in enumerate(all_chains):
            d = qi_of[c] - _first_block(gaps[c], tab_ref[N_HEADS + hb * HEADS_PER_BLOCK + hh], qi_of[c])
            depth[hh] = d if depth[hh] is None else jnp.maximum(depth[hh], d)

        def off_fn(c, hh, n, qi=None):
            qi = qi_of[c] if qi is None else qi
            return tab_ref[hb * HEADS_PER_BLOCK + hh] * jnp.asarray((n - qi) * ATT_T, F32)

        _softmax_sweep(qi_of, all_chains, gaps, depth, off_fn, kp_ref, vt_ref, qt_ref, ref_ref, acc_ref,
                       sbuf_ref, split_heads=True)
        for s, qi in enumerate(tiles):
            base = s * n_ch
            outs = []
            for hh in range(HEADS_PER_BLOCK):
                o = (_normalized(acc_ref[base + 2 * hh])
                     - lam * _normalized(acc_ref[base + 2 * hh + 1]))
                ms = jnp.mean(o * o, axis=0, keepdims=True)
                outs.append(o * lax.rsqrt(ms + SUBLN_EPS))
            o = jnp.concatenate(outs, axis=0).T * (g_ref[...] * (1.0 - lambda_init))
            o_ref[_blk(qi), :] = o.astype(o_ref.dtype)
        return carry

    lax.fori_loop(0, n_blocks // TILES_PER_PASS, tile_group, 0)


def _log_sigmoids(z):
    lneg = jnp.minimum(-z, 0.0) - jnp.log(1.0 + jnp.exp(-jnp.abs(z)))
    return lneg, lneg + z


def _suffix_sums(upper, lg):
    hi = lg.astype(BF16)
    lo = (lg - hi.astype(F32)).astype(BF16)
    return _dot(upper, hi) + _dot(upper, lo)


def _sb_pass(tiles, k_ref, vt_ref, qt_ref, carry_ref, acc_ref):
    krow, qcol = _key_query_iotas()
    before = krow < qcol
    upper = (qcol > krow).astype(BF16)
    items = [(slot * HEADS_PER_BLOCK + hh, hh, n, diag)
             for slot, blocks in tiles for n, diag in blocks for hh in range(HEADS_PER_BLOCK)]
    z = [_dot(k_ref[_blk(n), :], qt_ref[st]) for st, _, n, _ in items]
    log_sig, log_surv, tails = [], [], []
    for i, (st, hh, n, diag) in enumerate(items):
        lneg, lpos = _log_sigmoids(jnp.where(before, z[i], NEG) if diag else z[i])
        log_surv.append(lneg)
        log_sig.append(lpos)
        tails.append(_suffix_sums(upper, lneg))
    carry, acc = {}, {}
    for slot, blocks in tiles:
        for hh in range(HEADS_PER_BLOCK):
            st = slot * HEADS_PER_BLOCK + hh
            starts_fresh = blocks[0][1]
            carry[st] = None if starts_fresh else carry_ref[st]
            acc[st] = None if starts_fresh else acc_ref[st]
    for i, (st, hh, n, diag) in enumerate(items):
        w = jnp.exp(log_sig[i] + (tails[i] if diag else tails[i] + carry[st]))
        part = _dot(vt_ref[hh, :, _blk(n)], w.astype(BF16))
        acc[st] = part if acc[st] is None else acc[st] + part
        total = jnp.sum(log_surv[i], axis=0, keepdims=True)
        carry[st] = total if carry[st] is None else carry[st] + total
    worst = []
    for slot, _ in tiles:
        tops = []
        for hh in range(HEADS_PER_BLOCK):
            st = slot * HEADS_PER_BLOCK + hh
            acc_ref[st] = acc[st]
            carry_ref[st] = carry[st]
            tops.append(jnp.max(carry[st]))
        worst.append(functools.reduce(jnp.maximum, tops))
    return tuple(worst)


def _sb_kernel(q_ref, k_ref, v_ref, o_ref, vt_ref, qt_ref, carry_ref, acc_ref, *, n_blocks, scale):
    def transpose_values(n, c):
        v_t = v_ref[_blk(n), :].astype(F32).T.astype(BF16)
        for hh in range(HEADS_PER_BLOCK):
            vt_ref[hh, :, _blk(n)] = v_t[hh * HEAD_DIM:(hh + 1) * HEAD_DIM, :]
        return c

    lax.fori_loop(0, n_blocks, transpose_values, 0, unroll=BUILD_UNROLL)

    assert n_blocks % SB_TILES_PER_PASS == 0
    refs = (k_ref, vt_ref, qt_ref, carry_ref, acc_ref)
    row_h0 = _row_is_head0()

    def tile_group(g, carry):
        tiles = [g * SB_TILES_PER_PASS + slot for slot in range(SB_TILES_PER_PASS)]
        for slot, qi in enumerate(tiles):
            q_t = (q_ref[_blk(qi), :].astype(F32) * scale).T
            for hh in range(HEADS_PER_BLOCK):
                in_head = row_h0 if hh == 0 else jnp.logical_not(row_h0)
                qt_ref[slot * HEADS_PER_BLOCK + hh] = jnp.where(in_head, q_t, 0.0).astype(BF16)

        def blocks_of(qi, with_previous):
            return [(qi, True), (qi - 1, False)] if with_previous else [(qi, True)]

        worst = lax.cond(
            g > 0,
            lambda: _sb_pass([(slot, blocks_of(qi, True)) for slot, qi in enumerate(tiles)], *refs),
            lambda: _sb_pass([(slot, blocks_of(qi, slot > 0)) for slot, qi in enumerate(tiles)], *refs))

        for slot, qi in enumerate(tiles):
            def cond(c):
                j, live = c
                return (j >= 0) & live

            def body(c, slot=slot):
                j, _ = c
                return j - 1, _sb_pass([(slot, [(j, False)])], *refs)[0] > SB_EXIT

            lax.while_loop(cond, body, (qi - 2, worst[slot] > SB_EXIT))
            st = slot * HEADS_PER_BLOCK
            o = jnp.concatenate([acc_ref[st + hh] for hh in range(HEADS_PER_BLOCK)], axis=0)
            o_ref[_blk(qi), :] = o.T.astype(o_ref.dtype)
        return carry

    lax.fori_loop(0, n_blocks // SB_TILES_PER_PASS, tile_group, 0)


def _attn_specs(B, S, q_base, k_base, v_base):
    qspec = pl.BlockSpec((None, S, LANES), lambda b, h: (b, 0, q_base + h))
    kspec = pl.BlockSpec((None, S, LANES), lambda b, h: (b, 0, k_base + h))
    vspec = pl.BlockSpec((None, S, LANES), lambda b, h: (b, 0, v_base + h))
    ospec = pl.BlockSpec((None, S, LANES), lambda b, h: (b, 0, h))
    grid = (B, N_HEAD_BLOCKS)
    params = pltpu.CompilerParams(
        dimension_semantics=("parallel", "parallel"), vmem_limit_bytes=VMEM_LIMIT)
    return grid, qspec, kspec, vspec, ospec, params


_SMEM_SPEC = pl.BlockSpec(memory_space=pltpu.SMEM)


def _moba(qkv, tab):
    B, S, _ = qkv.shape
    grid, qspec, kspec, vspec, ospec, params = _attn_specs(B, S, 0, 3, 6)
    n_blocks = S // MOBA_BLOCK
    nbp = -(-n_blocks // SUBLANES) * SUBLANES
    n_slots = TILES_PER_PASS * HEADS_PER_BLOCK
    return pl.pallas_call(
        functools.partial(_moba_kernel, n_blocks=n_blocks, scale=HEAD_DIM ** -0.5),
        grid=grid,
        in_specs=[_SMEM_SPEC, qspec, kspec, vspec],
        out_specs=ospec,
        out_shape=jax.ShapeDtypeStruct((B, S, MIX_W), BF16),
        scratch_shapes=[
            pltpu.VMEM((HEADS_PER_BLOCK, S, LANES), BF16),
            pltpu.VMEM((HEADS_PER_BLOCK, V_ROWS, S), BF16),
            pltpu.VMEM((1, LANES), F32),
            pltpu.VMEM((nbp, LANES), F32),
            pltpu.VMEM((n_slots, nbp, ATT_T), F32),
            pltpu.VMEM((n_slots, LANES, ATT_T), BF16),
            pltpu.VMEM((n_slots, 1, ATT_T), F32),
            pltpu.VMEM((n_slots, V_ROWS, ATT_T), F32),
            pltpu.VMEM((n_slots * STAGE_DEPTH, ATT_T, ATT_T), F32),
        ],
        compiler_params=params,
        name="moba_attn",
    )(tab, qkv, qkv, qkv)


def _diff(qkv, tab, diff_lambda, subln_g, lambda_init):
    B, S, _ = qkv.shape
    grid, qspec, kspec, vspec, ospec, params = _attn_specs(B, S, 9, 12, 15)
    n_chains = TILES_PER_PASS * 2 * HEADS_PER_BLOCK
    return pl.pallas_call(
        functools.partial(_diff_kernel, n_blocks=S // ATT_T, scale=DIFF_QK_DIM ** -0.5,
                          lambda_init=lambda_init),
        grid=grid,
        in_specs=[
            _SMEM_SPEC,
            pl.BlockSpec(diff_lambda.shape, lambda b, h: (0, 0)),
            pl.BlockSpec((1, LANES), lambda b, h: (0, 0)),
            qspec, kspec, vspec,
        ],
        out_specs=ospec,
        out_shape=jax.ShapeDtypeStruct((B, S, MIX_W), BF16),
        scratch_shapes=[
            pltpu.VMEM((HEADS_PER_BLOCK, S, LANES), BF16),
            pltpu.VMEM((HEADS_PER_BLOCK, V_ROWS, S), BF16),
            pltpu.VMEM((1, LANES), F32),
            pltpu.VMEM((n_chains, LANES, ATT_T), BF16),
            pltpu.VMEM((n_chains, 1, ATT_T), F32),
            pltpu.VMEM((n_chains, V_ROWS, ATT_T), F32),
            pltpu.VMEM((n_chains * STAGE_DEPTH, ATT_T, ATT_T), F32),
        ],
        compiler_params=params,
        name="diff_attn",
    )(tab, diff_lambda, subln_g, qkv, qkv, qkv)


def _sb(qkv):
    B, S, _ = qkv.shape
    grid, qspec, kspec, vspec, ospec, params = _attn_specs(B, S, 18, 21, 24)
    return pl.pallas_call(
        functools.partial(_sb_kernel, n_blocks=S // ATT_T, scale=HEAD_DIM ** -0.5),
        grid=grid,
        in_specs=[qspec, kspec, vspec],
        out_specs=ospec,
        out_shape=jax.ShapeDtypeStruct((B, S, MIX_W), BF16),
        scratch_shapes=[
            pltpu.VMEM((HEADS_PER_BLOCK, HEAD_DIM, S), BF16),
            pltpu.VMEM((SB_TILES_PER_PASS * HEADS_PER_BLOCK, LANES, ATT_T), BF16),
            pltpu.VMEM((SB_TILES_PER_PASS * HEADS_PER_BLOCK, 1, ATT_T), F32),
            pltpu.VMEM((SB_TILES_PER_PASS * HEADS_PER_BLOCK, HEAD_DIM, ATT_T), F32),
        ],
        compiler_params=params,
        name="sb_attn",
    )(qkv, qkv, qkv)


def _merge_ln_kernel(x_ref, om_ref, od_ref, os_ref, wg_ref, bg_ref, wbr_ref, wo_ref, lg_ref, lb_ref,
                     o_ref, *, alpha, d_model):
    half = x_ref.shape[0] // MERGE_ROW_SPLIT
    merged = []
    for r in range(MERGE_ROW_SPLIT):
        rows = slice(r * half, (r + 1) * half)
        xb = x_ref[rows, :].astype(BF16)
        acc = None
        for b, ob_ref in enumerate((om_ref, od_ref, os_ref)):
            glogit = _dot(xb, wg_ref[:, b * d_model:(b + 1) * d_model]) + bg_ref[b:b + 1, :]
            term = jax.nn.sigmoid(glogit) * _dot(ob_ref[rows, :], wbr_ref[b])
            acc = term if acc is None else acc + term
        merged.append(acc.astype(BF16))
    for r in range(MERGE_ROW_SPLIT):
        rows = slice(r * half, (r + 1) * half)
        y = alpha * x_ref[rows, :] + _dot(merged[r], wo_ref[...])
        o_ref[rows, :] = _layer_norm(y, lg_ref[...], lb_ref[...])


def _merge_ln(x, o_m, o_d, o_s, w_gate, b_gate, w_br, w_out, lg, lb, l, alpha, tm=512):
    T, D = x.shape
    return pl.pallas_call(
        functools.partial(_merge_ln_kernel, alpha=alpha, d_model=D),
        grid=(T // tm,),
        in_specs=[
            pl.BlockSpec((tm, D), lambda i: (i, 0)),
            pl.BlockSpec((tm, MIX_W), lambda i: (i, 0)),
            pl.BlockSpec((tm, MIX_W), lambda i: (i, 0)),
            pl.BlockSpec((tm, MIX_W), lambda i: (i, 0)),
            pl.BlockSpec((None, D, N_BRANCH * D), lambda i: (l, 0, 0), pipeline_mode=pl.Buffered(1)),
            pl.BlockSpec((None, N_BRANCH, D), lambda i: (l, 0, 0)),
            pl.BlockSpec((None, N_BRANCH, MIX_W, D), lambda i: (l, 0, 0, 0), pipeline_mode=pl.Buffered(1)),
            pl.BlockSpec((None, D, D), lambda i: (l, 0, 0), pipeline_mode=pl.Buffered(1)),
            pl.BlockSpec((1, D), lambda i: (0, 0)),
            pl.BlockSpec((1, D), lambda i: (0, 0)),
        ],
        out_specs=pl.BlockSpec((tm, D), lambda i: (i, 0)),
        out_shape=jax.ShapeDtypeStruct((T, D), F32),
        compiler_params=pltpu.CompilerParams(
            dimension_semantics=("parallel",), vmem_limit_bytes=VMEM_LIMIT),
        name="merge_ln",
    )(x, o_m, o_d, o_s, w_gate, b_gate, w_br, w_out, lg, lb)


def _alibi_slopes(n):
    return (2.0 ** (-8.0 * np.arange(1, n + 1, dtype=np.float32) / n)).astype(np.float32)


def _slope_table(slopes):
    s2 = slopes.astype(np.float64) * LOG2E
    return np.concatenate([s2, 1.0 / (s2 * ATT_T)]).astype(np.float32)


def kernel(x, ln_g, ln_b, ffn_w_gate, ffn_w_up, ffn_w_down, w_in, b_gate, diff_lambda, diff_subln_g,
           w_br_moba, w_br_diff, w_br_sb, w_out):
    B, S, D = x.shape
    depth = ln_g.shape[0]
    assert S % ATT_T == 0 and ATT_T == MOBA_BLOCK
    assert w_in.shape[-1] == QKV_W + N_BRANCH * D
    alpha = (2.0 * depth) ** 0.25

    wg = ffn_w_gate.astype(BF16)
    wu = ffn_w_up.astype(BF16)
    wd = ffn_w_down.astype(BF16)
    w_qkv = w_in[:, :, :QKV_W].astype(BF16)
    w_gate = w_in[:, :, QKV_W:].astype(BF16)
    w_br = jnp.stack([w_br_moba, w_br_diff, w_br_sb], axis=1).astype(BF16)
    w_out_b = w_out.astype(BF16)
    slopes = _alibi_slopes(2 * N_HEADS)
    tab_moba = jnp.asarray(_slope_table(slopes[0::2]))
    tab_diff = jnp.asarray(_slope_table(slopes[1::2]))
    subln_g = jnp.tile(diff_subln_g.astype(F32), (1, HEADS_PER_BLOCK))

    h = x.reshape(B * S, D)
    for l in range(depth):
        lg = ln_g[l][:, None, :]
        lb = ln_b[l][:, None, :]
        h = _ffn_ln(h, wg, wu, wd, lg[0], lb[0], l, 0, alpha)
        qkv = _qkv_proj(h, w_qkv, l).reshape(B, S, QKV_W)
        lambda_init = 0.8 - 0.6 * math.exp(-0.3 * l)
        o_m = _moba(qkv, tab_moba)
        o_d = _diff(qkv, tab_diff, diff_lambda[l], subln_g[l][None, :], lambda_init)
        o_s = _sb(qkv)
        T = B * S
        h = _merge_ln(h, o_m.reshape(T, MIX_W), o_d.reshape(T, MIX_W), o_s.reshape(T, MIX_W),
                      w_gate, b_gate, w_br, w_out_b, lg[1], lb[1], l, alpha)
        h = _ffn_ln(h, wg, wu, wd, lg[2], lb[2], l, 1, alpha)
    return h.reshape(B, S, D)
```
